```python
import jax, jax.numpy as jnp
from jax import lax
import numpy as np

D_MODEL = 1024
BATCH = 8
SEQ = 4096
DEPTH = 4

EPS = 1e-6
PLE_DIM = 256
D_FF = 2816
SC_WIDTH = D_MODEL
SC_KERNEL = 3
SSM_INNER = 2 * D_MODEL
SSM_HEADDIM = 64
SSM_HEADS = SSM_INNER // SSM_HEADDIM
SSM_GROUPS = 4
SSM_STATE = 128
SSM_CONV = 4
SSM_CHUNK = 128
SSM_CONV_DIM = SSM_INNER + 2 * SSM_GROUPS * SSM_STATE
PROJ_SIZES = (SC_WIDTH, SC_WIDTH, SC_WIDTH,
              SSM_INNER, SSM_CONV_DIM, SSM_HEADS,
              D_MODEL, D_MODEL)
PROJ_DIM = sum(PROJ_SIZES)
PROJ_SPLITS = tuple(int(v) for v in np.cumsum(PROJ_SIZES)[:-1])

kernel_name = "hybrid_shortconv_ssd_macaron_block"


def rmsnorm(x, g):
    xf = x.astype(jnp.float32)
    y = xf * lax.rsqrt(jnp.mean(xf * xf, axis=-1, keepdims=True) + EPS)
    return y.astype(x.dtype) * g


def grouped_rmsnorm(x, g, groups):
    shp = x.shape
    xf = x.astype(jnp.float32).reshape(shp[:-1] + (groups, shp[-1] // groups))
    y = xf * lax.rsqrt(jnp.mean(xf * xf, axis=-1, keepdims=True) + EPS)
    return y.reshape(shp).astype(x.dtype) * g


def swiglu(x, wg, wu, wd):
    return (jax.nn.silu(x @ wg) * (x @ wu)) @ wd


def causal_depthwise_conv(x, w):
    k, c = w.shape
    return lax.conv_general_dilated(
        x, w.reshape(k, 1, c).astype(x.dtype), window_strides=(1,), padding=[(k - 1, 0)],
        dimension_numbers=("NWC", "WIO", "NWC"), feature_group_count=c)


def ssd_chunked(xdt, a, bm, cm):
    b, s, h, p = xdt.shape
    g, n = bm.shape[2], bm.shape[3]
    r = h // g
    nc, L = s // SSM_CHUNK, SSM_CHUNK
    dt_ = xdt.dtype
    X = xdt.reshape(b, nc, L, g, r, p)
    A = a.reshape(b, nc, L, g, r).astype(jnp.float32)
    Bc = bm.reshape(b, nc, L, g, n)
    Cc = cm.reshape(b, nc, L, g, n)
    a_cum = jnp.cumsum(A, axis=2)
    causal = jnp.tril(jnp.ones((L, L), dtype=bool))[None, None, :, :, None, None]
    diff = a_cum[:, :, :, None] - a_cum[:, :, None, :]
    decay = jnp.exp(jnp.where(causal, diff, -jnp.inf)).astype(dt_)
    cb = jnp.einsum("bclgn,bcsgn->bclsg", Cc, Bc)
    y_diag = jnp.einsum("bclsg,bclsgr,bcsgrp->bclgrp", cb, decay, X)
    decay_to_end = jnp.exp(a_cum[:, :, -1:] - a_cum).astype(dt_)
    states = jnp.einsum("bclgn,bclgr,bclgrp->bcgrpn", Bc, decay_to_end, X)
    chunk_decay = jnp.exp(a_cum[:, :, -1]).astype(dt_)

    def step(carry, inp):
        st, dec = inp
        return carry * dec[..., None, None] + st, carry

    init = jnp.zeros((b, g, r, p, n), dtype=states.dtype)
    _, prev = lax.scan(step, init, (jnp.swapaxes(states, 0, 1), jnp.swapaxes(chunk_decay, 0, 1)))
    prev = jnp.swapaxes(prev, 0, 1)
    y_off = jnp.einsum("bclgn,bcgrpn,bclgr->bclgrp", Cc, prev, jnp.exp(a_cum).astype(dt_))
    return (y_diag + y_off).reshape(b, s, h, p)


def hybrid_mixer(u, w_in, sc_conv_w, sc_w_out, m_conv_w, m_conv_b, m_dt_bias, m_A_log, m_D,
                 m_norm, m_w_out, w_o):
    b, s, _ = u.shape
    proj = u @ w_in
    sc_b, sc_c, sc_x, m_z, m_xbc, m_dt, gate_a, gate_m = jnp.split(proj, PROJ_SPLITS, axis=-1)
    y_a = (sc_b * causal_depthwise_conv(sc_c * sc_x, sc_conv_w)) @ sc_w_out
    xbc = jax.nn.silu(causal_depthwise_conv(m_xbc, m_conv_w) + m_conv_b)
    xs, bm, cm = jnp.split(xbc, (SSM_INNER, SSM_INNER + SSM_GROUPS * SSM_STATE), axis=-1)
    dt = jax.nn.softplus((m_dt + m_dt_bias).astype(jnp.float32))
    A = -jnp.exp(m_A_log.astype(jnp.float32))
    xh = xs.reshape(b, s, SSM_HEADS, SSM_HEADDIM)
    y = ssd_chunked(xh * dt.astype(xh.dtype)[..., None], A * dt,
                    bm.reshape(b, s, SSM_GROUPS, SSM_STATE), cm.reshape(b, s, SSM_GROUPS, SSM_STATE))
    y = (y + m_D[:, None] * xh).reshape(b, s, SSM_INNER)
    y_m = grouped_rmsnorm(y * jax.nn.silu(m_z), m_norm, SSM_GROUPS) @ m_w_out
    merged = jax.nn.sigmoid(gate_a) * y_a + jax.nn.sigmoid(gate_m) * y_m
    return merged @ w_o


def _fwd_setup_inputs(seed: int = 0) -> dict:
    key = jax.random.key(seed)
    ks = iter(jax.random.split(key, 40))

    def nrm(shape, fan_in):
        return jax.random.normal(next(ks), shape, jnp.float32) * (fan_in ** -0.5)

    def gain(shape):
        return 1.0 + 0.05 * jax.random.normal(next(ks), shape, jnp.float32)

    dt0 = jnp.exp(jax.random.uniform(next(ks), (DEPTH, SSM_HEADS), jnp.float32)
                  * (np.log(0.1) - np.log(0.001)) + np.log(0.001))
    return {
        "x": jax.random.normal(next(ks), (BATCH, SEQ, D_MODEL), jnp.float32),
        "p": jax.random.normal(next(ks), (DEPTH, BATCH, SEQ, PLE_DIM), jnp.float32),
        "ffn1_norm": gain((DEPTH, D_MODEL)),
        "ffn1_wg": nrm((DEPTH, D_MODEL, D_FF), D_MODEL),
        "ffn1_wu": nrm((DEPTH, D_MODEL, D_FF), D_MODEL),
        "ffn1_wd": nrm((DEPTH, D_FF, D_MODEL), D_FF),
        "mix_norm": gain((DEPTH, D_MODEL)),
        "w_in": nrm((DEPTH, D_MODEL, PROJ_DIM), D_MODEL),
        "sc_conv_w": nrm((DEPTH, SC_KERNEL, SC_WIDTH), SC_KERNEL),
        "sc_w_out": nrm((DEPTH, SC_WIDTH, D_MODEL), SC_WIDTH),
        "m_conv_w": nrm((DEPTH, SSM_CONV, SSM_CONV_DIM), SSM_CONV),
        "m_conv_b": 0.02 * jax.random.normal(next(ks), (DEPTH, SSM_CONV_DIM), jnp.float32),
        "m_dt_bias": dt0 + jnp.log(-jnp.expm1(-dt0)),
        "m_A_log": jnp.log(jax.random.uniform(next(ks), (DEPTH, SSM_HEADS), jnp.float32, 1.0, 16.0)),
        "m_D": gain((DEPTH, SSM_HEADS)),
        "m_norm": gain((DEPTH, SSM_INNER)),
        "m_w_out": nrm((DEPTH, SSM_INNER, D_MODEL), SSM_INNER),
        "w_o": nrm((DEPTH, D_MODEL, D_MODEL), D_MODEL),
        "ffn2_norm": gain((DEPTH, D_MODEL)),
        "ffn2_wg": nrm((DEPTH, D_MODEL, D_FF), D_MODEL),
        "ffn2_wu": nrm((DEPTH, D_MODEL, D_FF), D_MODEL),
        "ffn2_wd": nrm((DEPTH, D_FF, D_MODEL), D_FF),
        "ple_norm": gain((DEPTH, D_MODEL)),
        "ple_w_gate": nrm((DEPTH, D_MODEL, D_MODEL), D_MODEL),
        "ple_w_proj": nrm((DEPTH, PLE_DIM, D_MODEL), PLE_DIM),
        "final_norm": gain((D_MODEL,)),
    }


def _fwd_reference(x, p, ffn1_norm, ffn1_wg, ffn1_wu, ffn1_wd, mix_norm, w_in, sc_conv_w, sc_w_out,
              m_conv_w, m_conv_b, m_dt_bias, m_A_log, m_D, m_norm, m_w_out, w_o,
              ffn2_norm, ffn2_wg, ffn2_wu, ffn2_wd, ple_norm, ple_w_gate, ple_w_proj, final_norm):
    h = x
    for i in range(DEPTH):
        h = h + 0.5 * swiglu(rmsnorm(h, ffn1_norm[i]), ffn1_wg[i], ffn1_wu[i], ffn1_wd[i])
        h = h + hybrid_mixer(rmsnorm(h, mix_norm[i]), w_in[i], sc_conv_w[i], sc_w_out[i],
                             m_conv_w[i], m_conv_b[i], m_dt_bias[i], m_A_log[i], m_D[i],
                             m_norm[i], m_w_out[i], w_o[i])
        h = h + 0.5 * swiglu(rmsnorm(h, ffn2_norm[i]), ffn2_wg[i], ffn2_wu[i], ffn2_wd[i])
        gate = jax.nn.sigmoid(rmsnorm(h, ple_norm[i]) @ ple_w_gate[i])
        h = h + gate * (p[i] @ ple_w_proj[i])
    return rmsnorm(h, final_norm)


import jax as _jax
import jax.numpy as _jnp

TWIN_FORMAT = 'train_step'
FWD_PARAMS = ['x', 'p', 'ffn1_norm', 'ffn1_wg', 'ffn1_wu', 'ffn1_wd', 'mix_norm', 'w_in', 'sc_conv_w', 'sc_w_out', 'm_conv_w', 'm_conv_b', 'm_dt_bias', 'm_A_log', 'm_D', 'm_norm', 'm_w_out', 'w_o', 'ffn2_norm', 'ffn2_wg', 'ffn2_wu', 'ffn2_wd', 'ple_norm', 'ple_w_gate', 'ple_w_proj', 'final_norm']
TWIN_WEIGHTS = ['ffn1_norm', 'ffn1_wg', 'ffn1_wu', 'ffn1_wd', 'mix_norm', 'w_in', 'sc_conv_w', 'sc_w_out', 'm_conv_w', 'm_conv_b', 'm_dt_bias', 'm_A_log', 'm_D', 'm_norm', 'm_w_out', 'w_o', 'ffn2_norm', 'ffn2_wg', 'ffn2_wu', 'ffn2_wd', 'ple_norm', 'ple_w_gate', 'ple_w_proj', 'final_norm']
TWIN_DIFF_INPUT = 'x'
TWIN_INPUTS = ['x', 'p', 'ffn1_norm', 'ffn1_wg', 'ffn1_wu', 'ffn1_wd', 'mix_norm', 'w_in', 'sc_conv_w', 'sc_w_out', 'm_conv_w', 'm_conv_b', 'm_dt_bias', 'm_A_log', 'm_D', 'm_norm', 'm_w_out', 'w_o', 'ffn2_norm', 'ffn2_wg', 'ffn2_wu', 'ffn2_wd', 'ple_norm', 'ple_w_gate', 'ple_w_proj', 'final_norm', 'loss_target', 'm_ffn1_norm', 'm_ffn1_wg', 'm_ffn1_wu', 'm_ffn1_wd', 'm_mix_norm', 'm_w_in', 'm_sc_conv_w', 'm_sc_w_out', 'm_m_conv_w', 'm_m_conv_b', 'm_m_dt_bias', 'm_m_A_log', 'm_m_D', 'm_m_norm', 'm_m_w_out', 'm_w_o', 'm_ffn2_norm', 'm_ffn2_wg', 'm_ffn2_wu', 'm_ffn2_wd', 'm_ple_norm', 'm_ple_w_gate', 'm_ple_w_proj', 'm_final_norm', 'v_ffn1_norm', 'v_ffn1_wg', 'v_ffn1_wu', 'v_ffn1_wd', 'v_mix_norm', 'v_w_in', 'v_sc_conv_w', 'v_sc_w_out', 'v_m_conv_w', 'v_m_conv_b', 'v_m_dt_bias', 'v_m_A_log', 'v_m_D', 'v_m_norm', 'v_m_w_out', 'v_w_o', 'v_ffn2_norm', 'v_ffn2_wg', 'v_ffn2_wu', 'v_ffn2_wd', 'v_ple_norm', 'v_ple_w_gate', 'v_ple_w_proj', 'v_final_norm']
TWIN_OUTPUTS = ['loss', 'grad_x', 'grad_ffn1_norm', 'grad_ffn1_wg', 'grad_ffn1_wu', 'grad_ffn1_wd', 'grad_mix_norm', 'grad_w_in', 'grad_sc_conv_w', 'grad_sc_w_out', 'grad_m_conv_w', 'grad_m_conv_b', 'grad_m_dt_bias', 'grad_m_A_log', 'grad_m_D', 'grad_m_norm', 'grad_m_w_out', 'grad_w_o', 'grad_ffn2_norm', 'grad_ffn2_wg', 'grad_ffn2_wu', 'grad_ffn2_wd', 'grad_ple_norm', 'grad_ple_w_gate', 'grad_ple_w_proj', 'grad_final_norm', 'delta_ffn1_norm', 'delta_ffn1_wg', 'delta_ffn1_wu', 'delta_ffn1_wd', 'delta_mix_norm', 'delta_w_in', 'delta_sc_conv_w', 'delta_sc_w_out', 'delta_m_conv_w', 'delta_m_conv_b', 'delta_m_dt_bias', 'delta_m_A_log', 'delta_m_D', 'delta_m_norm', 'delta_m_w_out', 'delta_w_o', 'delta_ffn2_norm', 'delta_ffn2_wg', 'delta_ffn2_wu', 'delta_ffn2_wd', 'delta_ple_norm', 'delta_ple_w_gate', 'delta_ple_w_proj', 'delta_final_norm', 'new_m_ffn1_norm', 'new_m_ffn1_wg', 'new_m_ffn1_wu', 'new_m_ffn1_wd', 'new_m_mix_norm', 'new_m_w_in', 'new_m_sc_conv_w', 'new_m_sc_w_out', 'new_m_m_conv_w', 'new_m_m_conv_b', 'new_m_m_dt_bias', 'new_m_m_A_log', 'new_m_m_D', 'new_m_m_norm', 'new_m_m_w_out', 'new_m_w_o', 'new_m_ffn2_norm', 'new_m_ffn2_wg', 'new_m_ffn2_wu', 'new_m_ffn2_wd', 'new_m_ple_norm', 'new_m_ple_w_gate', 'new_m_ple_w_proj', 'new_m_final_norm', 'new_v_ffn1_norm', 'new_v_ffn1_wg', 'new_v_ffn1_wu', 'new_v_ffn1_wd', 'new_v_mix_norm', 'new_v_w_in', 'new_v_sc_conv_w', 'new_v_sc_w_out', 'new_v_m_conv_w', 'new_v_m_conv_b', 'new_v_m_dt_bias', 'new_v_m_A_log', 'new_v_m_D', 'new_v_m_norm', 'new_v_m_w_out', 'new_v_w_o', 'new_v_ffn2_norm', 'new_v_ffn2_wg', 'new_v_ffn2_wu', 'new_v_ffn2_wd', 'new_v_ple_norm', 'new_v_ple_w_gate', 'new_v_ple_w_proj', 'new_v_final_norm']
TWIN_LEAF_KINDS = {'loss': 'loss', 'grad_x': 'grad_x', 'grad_ffn1_norm': 'grad_w', 'grad_ffn1_wg': 'grad_w', 'grad_ffn1_wu': 'grad_w', 'grad_ffn1_wd': 'grad_w', 'grad_mix_norm': 'grad_w', 'grad_w_in': 'grad_w', 'grad_sc_conv_w': 'grad_w', 'grad_sc_w_out': 'grad_w', 'grad_m_conv_w': 'grad_w', 'grad_m_conv_b': 'grad_w', 'grad_m_dt_bias': 'grad_w', 'grad_m_A_log': 'grad_w', 'grad_m_D': 'grad_w', 'grad_m_norm': 'grad_w', 'grad_m_w_out': 'grad_w', 'grad_w_o': 'grad_w', 'grad_ffn2_norm': 'grad_w', 'grad_ffn2_wg': 'grad_w', 'grad_ffn2_wu': 'grad_w', 'grad_ffn2_wd': 'grad_w', 'grad_ple_norm': 'grad_w', 'grad_ple_w_gate': 'grad_w', 'grad_ple_w_proj': 'grad_w', 'grad_final_norm': 'grad_w', 'delta_ffn1_norm': 'delta_w', 'delta_ffn1_wg': 'delta_w', 'delta_ffn1_wu': 'delta_w', 'delta_ffn1_wd': 'delta_w', 'delta_mix_norm': 'delta_w', 'delta_w_in': 'delta_w', 'delta_sc_conv_w': 'delta_w', 'delta_sc_w_out': 'delta_w', 'delta_m_conv_w': 'delta_w', 'delta_m_conv_b': 'delta_w', 'delta_m_dt_bias': 'delta_w', 'delta_m_A_log': 'delta_w', 'delta_m_D': 'delta_w', 'delta_m_norm': 'delta_w', 'delta_m_w_out': 'delta_w', 'delta_w_o': 'delta_w', 'delta_ffn2_norm': 'delta_w', 'delta_ffn2_wg': 'delta_w', 'delta_ffn2_wu': 'delta_w', 'delta_ffn2_wd': 'delta_w', 'delta_ple_norm': 'delta_w', 'delta_ple_w_gate': 'delta_w', 'delta_ple_w_proj': 'delta_w', 'delta_final_norm': 'delta_w', 'new_m_ffn1_norm': 'new_m', 'new_m_ffn1_wg': 'new_m', 'new_m_ffn1_wu': 'new_m', 'new_m_ffn1_wd': 'new_m', 'new_m_mix_norm': 'new_m', 'new_m_w_in': 'new_m', 'new_m_sc_conv_w': 'new_m', 'new_m_sc_w_out': 'new_m', 'new_m_m_conv_w': 'new_m', 'new_m_m_conv_b': 'new_m', 'new_m_m_dt_bias': 'new_m', 'new_m_m_A_log': 'new_m', 'new_m_m_D': 'new_m', 'new_m_m_norm': 'new_m', 'new_m_m_w_out': 'new_m', 'new_m_w_o': 'new_m', 'new_m_ffn2_norm': 'new_m', 'new_m_ffn2_wg': 'new_m', 'new_m_ffn2_wu': 'new_m', 'new_m_ffn2_wd': 'new_m', 'new_m_ple_norm': 'new_m', 'new_m_ple_w_gate': 'new_m', 'new_m_ple_w_proj': 'new_m', 'new_m_final_norm': 'new_m', 'new_v_ffn1_norm': 'new_v', 'new_v_ffn1_wg': 'new_v', 'new_v_ffn1_wu': 'new_v', 'new_v_ffn1_wd': 'new_v', 'new_v_mix_norm': 'new_v', 'new_v_w_in': 'new_v', 'new_v_sc_conv_w': 'new_v', 'new_v_sc_w_out': 'new_v', 'new_v_m_conv_w': 'new_v', 'new_v_m_conv_b': 'new_v', 'new_v_m_dt_bias': 'new_v', 'new_v_m_A_log': 'new_v', 'new_v_m_D': 'new_v', 'new_v_m_norm': 'new_v', 'new_v_m_w_out': 'new_v', 'new_v_w_o': 'new_v', 'new_v_ffn2_norm': 'new_v', 'new_v_ffn2_wg': 'new_v', 'new_v_ffn2_wu': 'new_v', 'new_v_ffn2_wd': 'new_v', 'new_v_ple_norm': 'new_v', 'new_v_ple_w_gate': 'new_v', 'new_v_ple_w_proj': 'new_v', 'new_v_final_norm': 'new_v'}


def _forward(args):
    return _fwd_reference(*[args[k] for k in FWD_PARAMS])


def _output_shape():
    out = _jax.eval_shape(lambda: _forward(_fwd_setup_inputs(0)))
    return out.shape, out.dtype

N_MICROBATCH = 1
ADAM_LR = 0.001
ADAM_B1 = 0.9
ADAM_B2 = 0.999
ADAM_EPS = 1e-08
ADAM_WD = 0.01
ADAM_STEP = 10
PER_EXAMPLE_BATCH_AXIS = {'x': 0, 'p': 1, 'loss_target': 0}
SHARED_INPUTS = []
_WEIGHT_DTYPES = {'ffn1_norm': _jnp.float32, 'ffn1_wg': _jnp.float32, 'ffn1_wu': _jnp.float32, 'ffn1_wd': _jnp.float32, 'mix_norm': _jnp.float32, 'w_in': _jnp.float32, 'sc_conv_w': _jnp.float32, 'sc_w_out': _jnp.float32, 'm_conv_w': _jnp.float32, 'm_conv_b': _jnp.float32, 'm_dt_bias': _jnp.float32, 'm_A_log': _jnp.float32, 'm_D': _jnp.float32, 'm_norm': _jnp.float32, 'm_w_out': _jnp.float32, 'w_o': _jnp.float32, 'ffn2_norm': _jnp.float32, 'ffn2_wg': _jnp.float32, 'ffn2_wu': _jnp.float32, 'ffn2_wd': _jnp.float32, 'ple_norm': _jnp.float32, 'ple_w_gate': _jnp.float32, 'ple_w_proj': _jnp.float32, 'final_norm': _jnp.float32}
MOMENT_SCALE = {'ffn1_norm': 8.258934e-02, 'ffn1_wg': 3.557035e-02, 'ffn1_wu': 3.446121e-02, 'ffn1_wd': 5.714113e-02, 'mix_norm': 1.828042e-01, 'w_in': 5.354436e-02, 'sc_conv_w': 7.317987e-02, 'sc_w_out': 7.199167e-02, 'm_conv_w': 4.527008e-02, 'm_conv_b': 6.227937e-02, 'm_dt_bias': 1.028519e-01, 'm_A_log': 1.346648e-01, 'm_D': 2.833362e-01, 'm_norm': 5.278604e-02, 'm_w_out': 7.217518e-02, 'w_o': 1.021984e-01, 'ffn2_norm': 5.540587e-02, 'ffn2_wg': 2.370514e-02, 'ffn2_wu': 2.297349e-02, 'ffn2_wd': 3.814509e-02, 'ple_norm': 2.652762e-02, 'ple_w_gate': 2.680512e-02, 'ple_w_proj': 6.825797e-02, 'final_norm': 3.204335e+01}


def _to_microbatches(a, axis):
    t = _jnp.moveaxis(a, axis, 0)
    t = t.reshape((N_MICROBATCH, t.shape[0] // N_MICROBATCH) + t.shape[1:])
    return _jnp.moveaxis(t, 1, axis + 1)


def setup_inputs(seed: int = 0) -> dict:
    inp = _fwd_setup_inputs(seed)
    key = _jax.random.fold_in(_jax.random.key(seed), 7919)
    shape, _ = _output_shape()
    out = dict(inp)
    out["loss_target"] = _jax.random.normal(_jax.random.fold_in(key, 0), shape, _jnp.float32)
    for i, name in enumerate(TWIN_WEIGHTS):
        w = inp[name].astype(_jnp.float32)
        if MOMENT_SCALE is None:
            s = _jnp.sqrt(_jnp.mean(_jnp.square(w)) + 1e-30)
        else:
            s = MOMENT_SCALE[name]
        km, kv = _jax.random.split(_jax.random.fold_in(key, i + 1))
        out[name] = w
        out["m_" + name] = s * _jax.random.normal(km, w.shape, _jnp.float32)
        out["v_" + name] = (s * s) * _jax.random.uniform(kv, w.shape, _jnp.float32, 0.5, 1.5)
    if N_MICROBATCH > 1:
        for name, axis in PER_EXAMPLE_BATCH_AXIS.items():
            out[name] = _to_microbatches(out[name], axis)
    return {'x': out['x'], 'p': out['p'], 'ffn1_norm': out['ffn1_norm'], 'ffn1_wg': out['ffn1_wg'], 'ffn1_wu': out['ffn1_wu'], 'ffn1_wd': out['ffn1_wd'], 'mix_norm': out['mix_norm'], 'w_in': out['w_in'], 'sc_conv_w': out['sc_conv_w'], 'sc_w_out': out['sc_w_out'], 'm_conv_w': out['m_conv_w'], 'm_conv_b': out['m_conv_b'], 'm_dt_bias': out['m_dt_bias'], 'm_A_log': out['m_A_log'], 'm_D': out['m_D'], 'm_norm': out['m_norm'], 'm_w_out': out['m_w_out'], 'w_o': out['w_o'], 'ffn2_norm': out['ffn2_norm'], 'ffn2_wg': out['ffn2_wg'], 'ffn2_wu': out['ffn2_wu'], 'ffn2_wd': out['ffn2_wd'], 'ple_norm': out['ple_norm'], 'ple_w_gate': out['ple_w_gate'], 'ple_w_proj': out['ple_w_proj'], 'final_norm': out['final_norm'], 'loss_target': out['loss_target'], 'm_ffn1_norm': out['m_ffn1_norm'], 'm_ffn1_wg': out['m_ffn1_wg'], 'm_ffn1_wu': out['m_ffn1_wu'], 'm_ffn1_wd': out['m_ffn1_wd'], 'm_mix_norm': out['m_mix_norm'], 'm_w_in': out['m_w_in'], 'm_sc_conv_w': out['m_sc_conv_w'], 'm_sc_w_out': out['m_sc_w_out'], 'm_m_conv_w': out['m_m_conv_w'], 'm_m_conv_b': out['m_m_conv_b'], 'm_m_dt_bias': out['m_m_dt_bias'], 'm_m_A_log': out['m_m_A_log'], 'm_m_D': out['m_m_D'], 'm_m_norm': out['m_m_norm'], 'm_m_w_out': out['m_m_w_out'], 'm_w_o': out['m_w_o'], 'm_ffn2_norm': out['m_ffn2_norm'], 'm_ffn2_wg': out['m_ffn2_wg'], 'm_ffn2_wu': out['m_ffn2_wu'], 'm_ffn2_wd': out['m_ffn2_wd'], 'm_ple_norm': out['m_ple_norm'], 'm_ple_w_gate': out['m_ple_w_gate'], 'm_ple_w_proj': out['m_ple_w_proj'], 'm_final_norm': out['m_final_norm'], 'v_ffn1_norm': out['v_ffn1_norm'], 'v_ffn1_wg': out['v_ffn1_wg'], 'v_ffn1_wu': out['v_ffn1_wu'], 'v_ffn1_wd': out['v_ffn1_wd'], 'v_mix_norm': out['v_mix_norm'], 'v_w_in': out['v_w_in'], 'v_sc_conv_w': out['v_sc_conv_w'], 'v_sc_w_out': out['v_sc_w_out'], 'v_m_conv_w': out['v_m_conv_w'], 'v_m_conv_b': out['v_m_conv_b'], 'v_m_dt_bias': out['v_m_dt_bias'], 'v_m_A_log': out['v_m_A_log'], 'v_m_D': out['v_m_D'], 'v_m_norm': out['v_m_norm'], 'v_m_w_out': out['v_m_w_out'], 'v_w_o': out['v_w_o'], 'v_ffn2_norm': out['v_ffn2_norm'], 'v_ffn2_wg': out['v_ffn2_wg'], 'v_ffn2_wu': out['v_ffn2_wu'], 'v_ffn2_wd': out['v_ffn2_wd'], 'v_ple_norm': out['v_ple_norm'], 'v_ple_w_gate': out['v_ple_w_gate'], 'v_ple_w_proj': out['v_ple_w_proj'], 'v_final_norm': out['v_final_norm']}


def _loss(weights, diff, rest, loss_target):
    with _jax.named_scope("forward"):
        args = {**rest, TWIN_DIFF_INPUT: diff, **{k: w.astype(_WEIGHT_DTYPES[k]) for k, w in weights.items()}}
        y = _forward(args)
    with _jax.named_scope("loss_head"):
        err = _jnp.square(y.astype(_jnp.float32) - loss_target)
        return 0.5 * _jnp.sum(_jnp.mean(err, axis=-1)) if err.ndim else 0.5 * err


def _adamw(w, g, m, v):
    m = ADAM_B1 * m + (1.0 - ADAM_B1) * g
    v = ADAM_B2 * v + (1.0 - ADAM_B2) * _jnp.square(g)
    m_hat = m / (1.0 - ADAM_B1 ** ADAM_STEP)
    v_hat = v / (1.0 - ADAM_B2 ** ADAM_STEP)
    delta = -ADAM_LR * (m_hat / (_jnp.sqrt(v_hat) + ADAM_EPS) + ADAM_WD * w)
    return delta, m, v


def reference(x, p, ffn1_norm, ffn1_wg, ffn1_wu, ffn1_wd, mix_norm, w_in, sc_conv_w, sc_w_out, m_conv_w, m_conv_b, m_dt_bias, m_A_log, m_D, m_norm, m_w_out, w_o, ffn2_norm, ffn2_wg, ffn2_wu, ffn2_wd, ple_norm, ple_w_gate, ple_w_proj, final_norm, loss_target, m_ffn1_norm, m_ffn1_wg, m_ffn1_wu, m_ffn1_wd, m_mix_norm, m_w_in, m_sc_conv_w, m_sc_w_out, m_m_conv_w, m_m_conv_b, m_m_dt_bias, m_m_A_log, m_m_D, m_m_norm, m_m_w_out, m_w_o, m_ffn2_norm, m_ffn2_wg, m_ffn2_wu, m_ffn2_wd, m_ple_norm, m_ple_w_gate, m_ple_w_proj, m_final_norm, v_ffn1_norm, v_ffn1_wg, v_ffn1_wu, v_ffn1_wd, v_mix_norm, v_w_in, v_sc_conv_w, v_sc_w_out, v_m_conv_w, v_m_conv_b, v_m_dt_bias, v_m_A_log, v_m_D, v_m_norm, v_m_w_out, v_w_o, v_ffn2_norm, v_ffn2_wg, v_ffn2_wu, v_ffn2_wd, v_ple_norm, v_ple_w_gate, v_ple_w_proj, v_final_norm):
    given = dict(x=x, p=p, ffn1_norm=ffn1_norm, ffn1_wg=ffn1_wg, ffn1_wu=ffn1_wu, ffn1_wd=ffn1_wd, mix_norm=mix_norm, w_in=w_in, sc_conv_w=sc_conv_w, sc_w_out=sc_w_out, m_conv_w=m_conv_w, m_conv_b=m_conv_b, m_dt_bias=m_dt_bias, m_A_log=m_A_log, m_D=m_D, m_norm=m_norm, m_w_out=m_w_out, w_o=w_o, ffn2_norm=ffn2_norm, ffn2_wg=ffn2_wg, ffn2_wu=ffn2_wu, ffn2_wd=ffn2_wd, ple_norm=ple_norm, ple_w_gate=ple_w_gate, ple_w_proj=ple_w_proj, final_norm=final_norm, loss_target=loss_target, m_ffn1_norm=m_ffn1_norm, m_ffn1_wg=m_ffn1_wg, m_ffn1_wu=m_ffn1_wu, m_ffn1_wd=m_ffn1_wd, m_mix_norm=m_mix_norm, m_w_in=m_w_in, m_sc_conv_w=m_sc_conv_w, m_sc_w_out=m_sc_w_out, m_m_conv_w=m_m_conv_w, m_m_conv_b=m_m_conv_b, m_m_dt_bias=m_m_dt_bias, m_m_A_log=m_m_A_log, m_m_D=m_m_D, m_m_norm=m_m_norm, m_m_w_out=m_m_w_out, m_w_o=m_w_o, m_ffn2_norm=m_ffn2_norm, m_ffn2_wg=m_ffn2_wg, m_ffn2_wu=m_ffn2_wu, m_ffn2_wd=m_ffn2_wd, m_ple_norm=m_ple_norm, m_ple_w_gate=m_ple_w_gate, m_ple_w_proj=m_ple_w_proj, m_final_norm=m_final_norm, v_ffn1_norm=v_ffn1_norm, v_ffn1_wg=v_ffn1_wg, v_ffn1_wu=v_ffn1_wu, v_ffn1_wd=v_ffn1_wd, v_mix_norm=v_mix_norm, v_w_in=v_w_in, v_sc_conv_w=v_sc_conv_w, v_sc_w_out=v_sc_w_out, v_m_conv_w=v_m_conv_w, v_m_conv_b=v_m_conv_b, v_m_dt_bias=v_m_dt_bias, v_m_A_log=v_m_A_log, v_m_D=v_m_D, v_m_norm=v_m_norm, v_m_w_out=v_m_w_out, v_w_o=v_w_o, v_ffn2_norm=v_ffn2_norm, v_ffn2_wg=v_ffn2_wg, v_ffn2_wu=v_ffn2_wu, v_ffn2_wd=v_ffn2_wd, v_ple_norm=v_ple_norm, v_ple_w_gate=v_ple_w_gate, v_ple_w_proj=v_ple_w_proj, v_final_norm=v_final_norm)
    weights = {n: given[n] for n in TWIN_WEIGHTS}
    shared = {n: given[n] for n in SHARED_INPUTS}
    per_example = {n: given[n] for n in ['x', 'p']}
    grad_fn = _jax.value_and_grad(_loss, argnums=(0, 1))

    def one_microbatch(ex, loss_target):
        ex = dict(ex)
        diff = ex.pop(TWIN_DIFF_INPUT)
        return grad_fn(weights, diff, {**shared, **ex}, loss_target)

    if N_MICROBATCH == 1:
        loss, (grad_w, grad_x) = one_microbatch(per_example, given["loss_target"])
    else:
        def body(carry, xs):
            loss_sum, grad_sum = carry
            l_k, (gw_k, gx_k) = one_microbatch(xs[0], xs[1])
            with _jax.named_scope("update"):
                return (loss_sum + l_k, _jax.tree.map(_jnp.add, grad_sum, gw_k)), gx_k

        init = (_jnp.zeros((), _jnp.float32), _jax.tree.map(_jnp.zeros_like, weights))
        (loss, grad_w), grad_x = _jax.lax.scan(body, init, (per_example, given["loss_target"]))
    with _jax.named_scope("update"):
        delta_w, new_m, new_v = {}, {}, {}
        for n in TWIN_WEIGHTS:
            delta_w[n], new_m[n], new_v[n] = _adamw(weights[n], grad_w[n], given["m_" + n], given["v_" + n])
    return (loss, grad_x, *[grad_w[n] for n in TWIN_WEIGHTS], *[delta_w[n] for n in TWIN_WEIGHTS],
            *[new_m[n] for n in TWIN_WEIGHTS], *[new_v[n] for n in TWIN_WEIGHTS])
```

```python
import functools

import jax
import jax.numpy as jnp
import numpy as np
from jax import lax
from jax.experimental import pallas as pl
from jax.experimental.pallas import tpu as pltpu

BF = jnp.bfloat16
F32 = jnp.float32
EPS = 1e-6
LANE = 128
HALO = 8
SSM_P = 64
SSM_N = 128
SSM_G = 4
SSM_L = 128
ADAM_LR, ADAM_B1, ADAM_B2, ADAM_EPS, ADAM_WD, ADAM_STEP = 0.001, 0.9, 0.999, 1e-08, 0.01, 10
VMEM_LIMIT = 56 * 1024 * 1024
ADAMW_TILE_ELEMS = 400_000
MESH = pl.DeviceIdType.MESH
HI = lax.Precision.HIGHEST


def _tile(n, cap, mult=LANE):
    best = None
    t = mult
    while t <= min(n, cap):
        if n % t == 0:
            best = t
        t += mult
    return best if best is not None else n


def _params(sem):
    return pltpu.CompilerParams(dimension_semantics=sem, vmem_limit_bytes=VMEM_LIMIT)


def _sigmoid(x):
    return 1.0 / (1.0 + jnp.exp(-x))


def _dot(a, b, ca=1, cb=0, precision=None):
    return lax.dot_general(a, b, (((ca,), (cb,)), ((), ())), precision=precision, preferred_element_type=F32)


def _rms(x, g):
    r = lax.rsqrt(jnp.mean(x * x, axis=-1, keepdims=True) + EPS)
    return x * r * g


def _rms_bwd(x, g, dy):
    r = lax.rsqrt(jnp.mean(x * x, axis=-1, keepdims=True) + EPS)
    xh = x * r
    dxh = dy * g
    dx = r * (dxh - xh * jnp.mean(dxh * xh, axis=-1, keepdims=True))
    return dx, jnp.sum(dy * xh, axis=0, keepdims=True)


def mm(a, b, *, ta=False, tb=False, out_dtype=F32, name, tm_cap=512, tn_cap=512, tk_cap=2048):
    m, k = (a.shape[1], a.shape[0]) if ta else a.shape
    n = b.shape[0] if tb else b.shape[1]
    assert (b.shape[1] if tb else b.shape[0]) == k
    tm, tn, tk = _tile(m, tm_cap), _tile(n, tn_cap), _tile(k, tk_cap)
    nk = k // tk
    ca, cb = (0 if ta else 1), (1 if tb else 0)

    def body(a_ref, b_ref, o_ref, *acc):
        p = _dot(a_ref[...].astype(BF), b_ref[...].astype(BF), ca, cb)
        if nk == 1:
            o_ref[...] = p.astype(o_ref.dtype)
        else:
            kk = pl.program_id(2)

            @pl.when(kk == 0)
            def _():
                acc[0][...] = p

            @pl.when(kk > 0)
            def _():
                acc[0][...] += p

            @pl.when(kk == nk - 1)
            def _():
                o_ref[...] = acc[0][...].astype(o_ref.dtype)

    a_spec = pl.BlockSpec((tk, tm), lambda i, j, kk: (kk, i)) if ta else pl.BlockSpec((tm, tk), lambda i, j, kk: (i, kk))
    b_spec = pl.BlockSpec((tn, tk), lambda i, j, kk: (j, kk)) if tb else pl.BlockSpec((tk, tn), lambda i, j, kk: (kk, j))
    return pl.pallas_call(
        body,
        name=name,
        grid=(m // tm, n // tn, nk),
        in_specs=[a_spec, b_spec],
        out_specs=pl.BlockSpec((tm, tn), lambda i, j, kk: (i, j)),
        out_shape=jax.ShapeDtypeStruct((m, n), out_dtype),
        scratch_shapes=[pltpu.VMEM((tm, tn), F32)] if nk > 1 else [],
        compiler_params=_params(("parallel", "parallel", "arbitrary")),
    )(a, b)


def ew(fn, rows, vecs, out_rows, out_red=(), *, tm, name, prev_halo=(), next_halo=()):
    t = rows[0].shape[0]
    tm = min(tm, t)
    nt = t // tm
    assert t % tm == 0 and tm % HALO == 0
    nr, nv, npv, nnx, nor = len(rows), len(vecs), len(prev_halo), len(next_halo), len(out_rows)
    hb = tm // HALO

    def body(*refs):
        i = pl.program_id(0)
        ins = [r[...] for r in refs[: nr + nv + npv + nnx]]
        outs = refs[nr + nv + npv + nnx:]
        o_rows, o_red = fn(i, nt, ins[:nr], ins[nr:nr + nv], ins[nr + nv:nr + nv + npv], ins[nr + nv + npv:])
        for ref, val in zip(outs[:nor], o_rows, strict=True):
            ref[...] = val.astype(ref.dtype)
        for ref, val in zip(outs[nor:], o_red, strict=True):

            @pl.when(i == 0)
            def _(ref=ref, val=val):
                ref[...] = val

            @pl.when(i > 0)
            def _(ref=ref, val=val):
                ref[...] += val

    in_specs = [pl.BlockSpec((tm, r.shape[1]), lambda i: (i, 0)) for r in rows]
    in_specs += [pl.BlockSpec(v.shape, lambda i: (0, 0)) for v in vecs]
    in_specs += [pl.BlockSpec((HALO, rows[k].shape[1]), lambda i: (jnp.maximum(i * hb - 1, 0), 0)) for k in prev_halo]
    in_specs += [pl.BlockSpec((HALO, rows[k].shape[1]), lambda i: (jnp.minimum((i + 1) * hb, t // HALO - 1), 0)) for k in next_halo]
    out_specs = [pl.BlockSpec((tm, c), lambda i: (i, 0)) for c, _ in out_rows]
    out_specs += [pl.BlockSpec(s, lambda i: (0, 0)) for s in out_red]
    out_shape = [jax.ShapeDtypeStruct((t, c), d) for c, d in out_rows] + [jax.ShapeDtypeStruct(s, F32) for s in out_red]
    res = pl.pallas_call(
        body,
        name=name,
        grid=(nt,),
        in_specs=in_specs,
        out_specs=out_specs,
        out_shape=out_shape,
        compiler_params=_params(("arbitrary",) if out_red else ("parallel",)),
    )(*rows, *vecs, *[rows[k] for k in prev_halo], *[rows[k] for k in next_halo])
    return res


def _shift_down(x, prev, j):
    if j == 0:
        return x
    r = pltpu.roll(x, j, 0)
    rh = pltpu.roll(prev, j, 0)
    row = lax.broadcasted_iota(jnp.int32, (HALO, x.shape[1]), 0)
    head = jnp.where(row < j, rh, r[:HALO])
    return jnp.concatenate([head, r[HALO:]], axis=0)


def _shift_up(x, nxt, j):
    if j == 0:
        return x
    n = x.shape[0]
    r = pltpu.roll(x, n - j, 0)
    rh = pltpu.roll(nxt, HALO - j, 0)
    row = lax.broadcasted_iota(jnp.int32, (HALO, x.shape[1]), 0)
    tail = jnp.where(row >= HALO - j, rh, r[n - HALO:])
    return jnp.concatenate([r[: n - HALO], tail], axis=0)


def _conv_fwd(x, prev, w):
    kk = w.shape[0]
    acc = None
    for k in range(kk):
        term = w[k:k + 1, :] * _shift_down(x, prev, kk - 1 - k)
        acc = term if acc is None else acc + term
    return acc


def ffn_fwd(h, g, wg, wu, wd):
    t, d = h.shape
    f = wg.shape[1]
    tm, tf = _tile(t, 512), _tile(f, 512)
    nf = f // tf

    def body(h_ref, g_ref, wg_ref, wu_ref, wd_ref, ho_ref, a_ref, b_ref, n_sc, acc):
        j = pl.program_id(1)

        @pl.when(j == 0)
        def _():
            n_sc[...] = _rms(h_ref[...], g_ref[...]).astype(BF)

        n = n_sc[...]
        a = _dot(n, wg_ref[...])
        b = _dot(n, wu_ref[...])
        a_ref[...] = a
        b_ref[...] = b
        s = (a * _sigmoid(a) * b).astype(BF)
        p = _dot(s, wd_ref[...])

        @pl.when(j == 0)
        def _():
            acc[...] = p

        @pl.when(j > 0)
        def _():
            acc[...] += p

        @pl.when(j == nf - 1)
        def _():
            ho_ref[...] = h_ref[...] + 0.5 * acc[...]

    return pl.pallas_call(
        body,
        name="ffn_fwd",
        grid=(t // tm, nf),
        in_specs=[
            pl.BlockSpec((tm, d), lambda i, j: (i, 0)),
            pl.BlockSpec((1, d), lambda i, j: (0, 0)),
            pl.BlockSpec((d, tf), lambda i, j: (0, j)),
            pl.BlockSpec((d, tf), lambda i, j: (0, j)),
            pl.BlockSpec((tf, d), lambda i, j: (j, 0)),
        ],
        out_specs=[
            pl.BlockSpec((tm, d), lambda i, j: (i, 0)),
            pl.BlockSpec((tm, tf), lambda i, j: (i, j)),
            pl.BlockSpec((tm, tf), lambda i, j: (i, j)),
        ],
        out_shape=[jax.ShapeDtypeStruct((t, d), F32), jax.ShapeDtypeStruct((t, f), F32), jax.ShapeDtypeStruct((t, f), F32)],
        scratch_shapes=[pltpu.VMEM((tm, d), BF), pltpu.VMEM((tm, d), F32)],
        compiler_params=_params(("parallel", "arbitrary")),
    )(h, g, wg, wu, wd)


def ffn_bwd(dho, h, g, a, b, wg, wu, wd):
    t, d = h.shape
    f = wg.shape[1]
    tm, tf = _tile(t, 512), _tile(f, 512)
    nf = f // tf

    def body(dho_ref, h_ref, g_ref, a_ref, b_ref, wg_ref, wu_ref, wd_ref, dh_ref, dg_ref, n_ref, do_ref, s_ref, da_ref, db_ref, acc):
        i, j = pl.program_id(0), pl.program_id(1)

        @pl.when(j == 0)
        def _():
            n_ref[...] = _rms(h_ref[...], g_ref[...]).astype(BF)
            do_ref[...] = (0.5 * dho_ref[...]).astype(BF)

        ds = _dot(do_ref[...], wd_ref[...], 1, 1)
        av, bv = a_ref[...], b_ref[...]
        sig = _sigmoid(av)
        sl = av * sig
        s_ref[...] = (sl * bv).astype(BF)
        da = (ds * bv * (sig * (1.0 + av * (1.0 - sig)))).astype(BF)
        db = (ds * sl).astype(BF)
        da_ref[...] = da
        db_ref[...] = db
        p = _dot(da, wg_ref[...], 1, 1) + _dot(db, wu_ref[...], 1, 1)

        @pl.when(j == 0)
        def _():
            acc[...] = p

        @pl.when(j > 0)
        def _():
            acc[...] += p

        @pl.when(j == nf - 1)
        def _():
            dx, dg = _rms_bwd(h_ref[...], g_ref[...], acc[...])
            dh_ref[...] = dho_ref[...] + dx

            @pl.when(i == 0)
            def _():
                dg_ref[...] = dg

            @pl.when(i > 0)
            def _():
                dg_ref[...] += dg

    row = lambda c: pl.BlockSpec((tm, c), lambda i, j: (i, 0))
    blk = pl.BlockSpec((tm, tf), lambda i, j: (i, j))
    return pl.pallas_call(
        body,
        name="ffn_bwd",
        grid=(t // tm, nf),
        in_specs=[
            row(d), row(d), pl.BlockSpec((1, d), lambda i, j: (0, 0)), blk, blk,
            pl.BlockSpec((d, tf), lambda i, j: (0, j)),
            pl.BlockSpec((d, tf), lambda i, j: (0, j)),
            pl.BlockSpec((tf, d), lambda i, j: (j, 0)),
        ],
        out_specs=[row(d), pl.BlockSpec((1, d), lambda i, j: (0, 0)), row(d), row(d), blk, blk, blk],
        out_shape=[
            jax.ShapeDtypeStruct((t, d), F32), jax.ShapeDtypeStruct((1, d), F32),
            jax.ShapeDtypeStruct((t, d), BF), jax.ShapeDtypeStruct((t, d), BF),
            jax.ShapeDtypeStruct((t, f), BF), jax.ShapeDtypeStruct((t, f), BF), jax.ShapeDtypeStruct((t, f), BF),
        ],
        scratch_shapes=[pltpu.VMEM((tm, d), F32)],
        compiler_params=_params(("arbitrary", "arbitrary")),
    )(dho, h, g, a, b, wg, wu, wd)


def norm_cast(h, g):
    def fn(i, nt, rows, vecs, prevs, nexts):
        return [_rms(rows[0], vecs[0])], []
    return ew(fn, [h], [g], [(h.shape[1], BF)], tm=512, name="norm_cast")[0]


def _zero_if(cond, x):
    return jnp.where(cond, jnp.zeros_like(x), x)


def conv_a_fwd(sc3, w_sc):
    d = sc3.shape[1] // 3

    def fn(i, nt, rows, vecs, prevs, nexts):
        x, pv = rows[0], _zero_if(i == 0, prevs[0])
        v = x[:, d:2 * d] * x[:, 2 * d:]
        vp = pv[:, d:2 * d] * pv[:, 2 * d:]
        return [x[:, :d] * _conv_fwd(v, vp, vecs[0])], []

    return ew(fn, [sc3], [w_sc], [(d, BF)], tm=256, name="conv_a_fwd", prev_halo=(0,))[0]


def _softplus(x):
    e = jnp.exp(-jnp.abs(x))
    return jnp.maximum(x, 0.0) + jnp.where(e < 1e-4, e - 0.5 * e * e, jnp.log(1.0 + e))


def conv_m_fwd(xbc_raw, dt_raw, w_mc, b_mc, dt_bias):
    def fn(i, nt, rows, vecs, prevs, nexts):
        pre = _conv_fwd(rows[0], _zero_if(i == 0, prevs[0]), vecs[0]) + vecs[1]
        return [pre * _sigmoid(pre), _softplus(rows[1] + vecs[2])], []

    return ew(fn, [xbc_raw, dt_raw], [w_mc, b_mc, dt_bias], [(xbc_raw.shape[1], F32), (LANE, F32)], tm=128, name="conv_m_fwd",
              prev_halo=(0,))


def conv_m_bwd1(dxbc, xbc_raw, ddt, dt_raw, w_mc, b_mc, dt_bias):
    def fn(i, nt, rows, vecs, prevs, nexts):
        pre = _conv_fwd(rows[1], _zero_if(i == 0, prevs[0]), vecs[0]) + vecs[1]
        sig = _sigmoid(pre)
        dpre = rows[0] * (sig * (1.0 + pre * (1.0 - sig)))
        ddr = rows[2] * _sigmoid(rows[3] + vecs[2])
        return [dpre, ddr], [jnp.sum(ddr, axis=0, keepdims=True)]

    return ew(fn, [dxbc, xbc_raw, ddt, dt_raw], [w_mc, b_mc, dt_bias], [(dxbc.shape[1], F32), (LANE, BF)], [(1, LANE)], tm=128,
              name="conv_m_bwd1", prev_halo=(1,))


def conv_bwd2(dpre, x, w, name):
    kk = w.shape[0]

    def fn(i, nt, rows, vecs, prevs, nexts):
        dp, xv = rows[0], rows[1]
        nx = _zero_if(i == nt - 1, nexts[0])
        pv = _zero_if(i == 0, prevs[0])
        dx = None
        dws = []
        for k in range(kk):
            term = vecs[0][k:k + 1, :] * _shift_up(dp, nx, kk - 1 - k)
            dx = term if dx is None else dx + term
            dws.append(jnp.sum(dp * _shift_down(xv, pv, kk - 1 - k), axis=0, keepdims=True))
        return [dx], [jnp.concatenate(dws, axis=0), jnp.sum(dp, axis=0, keepdims=True)]

    c = x.shape[1]
    return ew(fn, [dpre, x], [w], [(c, BF)], [(kk, c), (1, c)], tm=128, name=name, prev_halo=(1,), next_halo=(0,))


def conv_a_bwd1(dya, sc3, w_sc):
    d = sc3.shape[1] // 3

    def fn(i, nt, rows, vecs, prevs, nexts):
        x, pv = rows[1], _zero_if(i == 0, prevs[0])
        v = x[:, d:2 * d] * x[:, 2 * d:]
        vp = pv[:, d:2 * d] * pv[:, 2 * d:]
        return [rows[0] * x[:, :d], rows[0] * _conv_fwd(v, vp, vecs[0]), v], []

    return ew(fn, [dya, sc3], [w_sc], [(d, F32), (d, BF), (d, F32)], tm=256, name="conv_a_bwd1", prev_halo=(1,))


def conv_a_bwd2(dcv, v, sc3, w_sc):
    d = v.shape[1]
    kk = w_sc.shape[0]

    def fn(i, nt, rows, vecs, prevs, nexts):
        dp, vv, x = rows
        nx = _zero_if(i == nt - 1, nexts[0])
        pv = _zero_if(i == 0, prevs[0])
        dv = None
        dws = []
        for k in range(kk):
            term = vecs[0][k:k + 1, :] * _shift_up(dp, nx, kk - 1 - k)
            dv = term if dv is None else dv + term
            dws.append(jnp.sum(dp * _shift_down(vv, pv, kk - 1 - k), axis=0, keepdims=True))
        return [dv * x[:, 2 * d:], dv * x[:, d:2 * d]], [jnp.concatenate(dws, axis=0)]

    return ew(fn, [dcv, v, sc3], [w_sc], [(d, BF), (d, BF)], [(kk, d)], tm=256, name="conv_a_bwd2", prev_halo=(1,), next_halo=(0,))


def _ssd_common(xbc_ref, dt_ref, alog_ref, e_ref, w, gn):
    ll = SSM_L
    xs = xbc_ref[:, 0:w]
    dtv = dt_ref[...]
    a_row = -jnp.exp(alog_ref[...])
    a = dtv * a_row
    row = lax.broadcasted_iota(jnp.int32, (ll, ll), 0)
    col = lax.broadcasted_iota(jnp.int32, (ll, ll), 1)
    tril = (row >= col).astype(F32)
    triu = (row <= col).astype(F32)
    acl = _dot(tril, a, precision=HI)
    acl_t = _dot(a, triu, 0, 0, precision=HI)
    e = e_ref[...]
    aclx = _dot(acl, e, precision=HI)
    dtx = _dot(dtv, e, precision=HI)
    last = aclx[ll - 1:ll, :]
    e_in = jnp.exp(aclx)
    e_end = jnp.exp(last - aclx)
    e_tot = jnp.exp(last)
    x = xs * dtx
    return dict(xs=xs, dtv=dtv, a_row=a_row, a=a, row=row, col=col, triu=triu, acl=acl, acl_t=acl_t, dtx=dtx, e_in=e_in, e_end=e_end,
                e_tot=e_tot, x=x)


def _decay(q, hh):
    diff = q["acl"][:, hh:hh + 1] - q["acl_t"][hh:hh + 1, :]
    return jnp.exp(jnp.where(q["row"] >= q["col"], diff, -jnp.inf))


def ssd_fwd(xbc, dt, z, a_log, d_exp, m_norm, e_mat):
    t = xbc.shape[0]
    w = z.shape[1]
    gn = SSM_G * SSM_N
    gw = w // SSM_G
    ll, nn = SSM_L, SSM_N
    nc = t // ll
    cw = xbc.shape[1]

    def body(xbc_ref, dt_ref, z_ref, alog_ref, dexp_ref, mn_ref, e_ref, yn_ref, y_ref, sp_ref, s_sc):
        c = pl.program_id(0)

        @pl.when(c == 0)
        def _():
            s_sc[...] = jnp.zeros_like(s_sc)

        q = _ssd_common(xbc_ref, dt_ref, alog_ref, e_ref, w, gn)
        xb = q["x"].astype(BF)
        xsb = (q["x"] * q["e_end"]).astype(BF)
        sp = s_sc[...]
        sp_ref[0] = sp
        spb = sp.astype(BF)
        lane = lax.broadcasted_iota(jnp.int32, (ll, LANE), 1)
        for g in range(SSM_G):
            lo = g * gw
            bg = xbc_ref[:, w + g * nn:w + (g + 1) * nn].astype(BF)
            cg = xbc_ref[:, w + gn + g * nn:w + gn + (g + 1) * nn].astype(BF)
            yoff = _dot(cg, spb[:, lo:lo + gw]) * q["e_in"][:, lo:lo + gw]
            s_sc[:, lo:lo + gw] = sp[:, lo:lo + gw] * q["e_tot"][:, lo:lo + gw] + _dot(bg, xsb[:, lo:lo + gw], 0, 0)
            cb = _dot(cg, bg, 1, 1)
            for pr in range(gw // LANE):
                l0 = lo + pr * LANE
                xp = xb[:, l0:l0 + LANE]
                ys = []
                for hh in (l0 // SSM_P, l0 // SSM_P + 1):
                    wm = (cb * _decay(q, hh)).astype(BF)
                    ys.append(_dot(wm, xp))
                ydiag = jnp.where(lane < SSM_P, ys[0], ys[1])
                y_ref[:, l0:l0 + LANE] = ydiag + yoff[:, pr * LANE:(pr + 1) * LANE] + dexp_ref[:, l0:l0 + LANE] * q["xs"][:, l0:l0 + LANE]
        zv = z_ref[...]
        yz = y_ref[...] * (zv * _sigmoid(zv))
        for g in range(SSM_G):
            lo = g * gw
            yn_ref[:, lo:lo + gw] = _rms(yz[:, lo:lo + gw], mn_ref[:, lo:lo + gw]).astype(BF)

    vec = lambda s: pl.BlockSpec(s, lambda c: (0, 0))
    return pl.pallas_call(
        body,
        name="ssd_fwd",
        grid=(nc,),
        in_specs=[
            pl.BlockSpec((ll, cw), lambda c: (c, 0)), pl.BlockSpec((ll, LANE), lambda c: (c, 0)), pl.BlockSpec((ll, w), lambda c: (c, 0)),
            vec((1, LANE)), vec((1, w)), vec((1, w)), vec((LANE, w)),
        ],
        out_specs=[pl.BlockSpec((ll, w), lambda c: (c, 0)), pl.BlockSpec((ll, w), lambda c: (c, 0)), pl.BlockSpec((1, nn, w), lambda c: (c, 0, 0))],
        out_shape=[jax.ShapeDtypeStruct((t, w), BF), jax.ShapeDtypeStruct((t, w), F32), jax.ShapeDtypeStruct((nc, nn, w), F32)],
        scratch_shapes=[pltpu.VMEM((nn, w), F32)],
        compiler_params=_params(("arbitrary",)),
    )(xbc, dt, z, a_log, d_exp, m_norm, e_mat)


def ssd_bwd(dyn, y, z, xbc, dt, sprev, a_log, d_exp, m_norm, e_mat, et_mat):
    t = xbc.shape[0]
    w = z.shape[1]
    gn = SSM_G * SSM_N
    gw = w // SSM_G
    ll, nn = SSM_L, SSM_N
    nc = t // ll
    cw = xbc.shape[1]

    def body(dyn_ref, y_ref, z_ref, xbc_ref, dt_ref, sp_ref, alog_ref, dexp_ref, mn_ref, e_ref, et_ref,
             dz_ref, dxbc_ref, ddt_ref, dmn_ref, dd_ref, dal_ref, ds_sc, dy_sc, dx_sc):
        step = pl.program_id(0)

        @pl.when(step == 0)
        def _():
            ds_sc[...] = jnp.zeros_like(ds_sc)

        zv, yv = z_ref[...], y_ref[...]
        sg = _sigmoid(zv)
        sz = zv * sg
        yz = yv * sz
        dmn = []
        for g in range(SSM_G):
            lo = g * gw
            dseg, dmn_g = _rms_bwd(yz[:, lo:lo + gw], mn_ref[:, lo:lo + gw], dyn_ref[:, lo:lo + gw])
            dy_sc[:, lo:lo + gw] = dseg
            dmn.append(dmn_g)
        dmn = jnp.concatenate(dmn, axis=1)
        dyz = dy_sc[...]
        dz_ref[...] = (dyz * yv * (sg * (1.0 + zv * (1.0 - sg)))).astype(BF)
        dy = dyz * sz

        q = _ssd_common(xbc_ref, dt_ref, alog_ref, e_ref, w, gn)
        x = q["x"]
        xb = x.astype(BF)
        xsb = (x * q["e_end"]).astype(BF)
        sp = sp_ref[0]
        spb = sp.astype(BF)
        dsn = ds_sc[...]
        dsnb = dsn.astype(BF)
        dyb = dy.astype(BF)
        lane = lax.broadcasted_iota(jnp.int32, (ll, LANE), 1)
        lane1 = lax.broadcasted_iota(jnp.int32, (1, LANE), 1)
        sub1 = lax.broadcasted_iota(jnp.int32, (LANE, 1), 0)
        dacl = jnp.zeros((ll, LANE), F32)
        dacl_t = jnp.zeros((LANE, ll), F32)
        d_ein, d_eend, d_etot = [], [], []
        for g in range(SSM_G):
            lo = g * gw
            sl = slice(lo, lo + gw)
            bg = xbc_ref[:, w + g * nn:w + (g + 1) * nn].astype(BF)
            cg = xbc_ref[:, w + gn + g * nn:w + gn + (g + 1) * nn].astype(BF)
            zg = _dot(cg, spb[:, sl])
            dzz = (dy[:, sl] * q["e_in"][:, sl]).astype(BF)
            d_ein.append(dy[:, sl] * zg)
            dcg = _dot(dzz, spb[:, sl], 1, 1)
            ds_sc[:, sl] = _dot(cg, dzz, 0, 0) + dsn[:, sl] * q["e_tot"][:, sl]
            d_etot.append(jnp.sum(dsn[:, sl] * sp[:, sl], axis=0, keepdims=True))
            dbg = _dot(xsb[:, sl], dsnb[:, sl], 1, 1)
            dxs_g = _dot(bg, dsnb[:, sl])
            d_eend.append(dxs_g * x[:, sl])
            cb = _dot(cg, bg, 1, 1)
            dcb = jnp.zeros((ll, ll), F32)
            for pr in range(gw // LANE):
                l0 = lo + pr * LANE
                xp = xb[:, l0:l0 + LANE]
                dyp = dyb[:, l0:l0 + LANE]
                dxp = []
                for hi, hh in enumerate((l0 // SSM_P, l0 // SSM_P + 1)):
                    lm = _decay(q, hh)
                    wm = (cb * lm).astype(BF)
                    in_head = (lane < SSM_P) if hi == 0 else (lane >= SSM_P)
                    dwm = _dot(jnp.where(in_head, dyp, jnp.zeros_like(dyp)), xp, 1, 1)
                    dxp.append(_dot(wm, dyp, 0, 0))
                    dlm = dwm * lm
                    dcb = dcb + dlm
                    dd = dlm * cb
                    dacl = dacl + jnp.sum(dd, axis=1, keepdims=True) * (lane1 == hh).astype(F32)
                    dacl_t = dacl_t + (sub1 == hh).astype(F32) * jnp.sum(dd, axis=0, keepdims=True)
                dx_sc[:, l0:l0 + LANE] = jnp.where(lane < SSM_P, dxp[0], dxp[1]) + dxs_g[:, pr * LANE:(pr + 1) * LANE] * q["e_end"][:, l0:l0 + LANE]
            dcbb = dcb.astype(BF)
            dxbc_ref[:, w + g * nn:w + (g + 1) * nn] = dbg + _dot(dcbb, cg, 0, 0)
            dxbc_ref[:, w + gn + g * nn:w + gn + (g + 1) * nn] = dcg + _dot(dcbb, bg)
        d_ein = jnp.concatenate(d_ein, axis=1) * q["e_in"]
        d_eend = jnp.concatenate(d_eend, axis=1) * q["e_end"]
        d_etot = jnp.concatenate(d_etot, axis=1) * q["e_tot"]
        et = et_ref[...]
        last_add = jnp.sum(d_eend, axis=0, keepdims=True) + d_etot
        last_add = _dot(jnp.broadcast_to(last_add, (HALO, w)), et, precision=HI)[0:1]
        row1 = lax.broadcasted_iota(jnp.int32, (ll, LANE), 0)
        dacl = dacl + _dot(d_ein - d_eend, et, precision=HI) + jnp.where(row1 == ll - 1, last_add, 0.0)
        da = _dot(q["triu"], dacl, precision=HI) - _dot(q["triu"], dacl_t, 1, 1, precision=HI)
        dxv = dx_sc[...]
        dxbc_ref[:, 0:w] = dexp_ref[...] * dy + dxv * q["dtx"]
        ddt_ref[...] = _dot(dxv * q["xs"], et, precision=HI) + da * q["a_row"]
        dal = jnp.sum(da * q["dtv"], axis=0, keepdims=True) * q["a_row"]
        ddv = jnp.sum(dy * q["xs"], axis=0, keepdims=True)
        ddv = _dot(jnp.broadcast_to(ddv, (HALO, w)), et, precision=HI)[0:1]

        @pl.when(step == 0)
        def _():
            dmn_ref[...] = dmn
            dd_ref[...] = ddv
            dal_ref[...] = dal

        @pl.when(step > 0)
        def _():
            dmn_ref[...] += dmn
            dd_ref[...] += ddv
            dal_ref[...] += dal

    rev = lambda c_: pl.BlockSpec((ll, c_), lambda s: (nc - 1 - s, 0))
    vec = lambda s_: pl.BlockSpec(s_, lambda s: (0, 0))
    return pl.pallas_call(
        body,
        name="ssd_bwd",
        grid=(nc,),
        in_specs=[
            rev(w), rev(w), rev(w), rev(cw), rev(LANE), pl.BlockSpec((1, nn, w), lambda s: (nc - 1 - s, 0, 0)),
            vec((1, LANE)), vec((1, w)), vec((1, w)), vec((LANE, w)), vec((w, LANE)),
        ],
        out_specs=[rev(w), rev(cw), rev(LANE), vec((1, w)), vec((1, LANE)), vec((1, LANE))],
        out_shape=[
            jax.ShapeDtypeStruct((t, w), BF), jax.ShapeDtypeStruct((t, cw), F32), jax.ShapeDtypeStruct((t, LANE), F32),
            jax.ShapeDtypeStruct((1, w), F32), jax.ShapeDtypeStruct((1, LANE), F32), jax.ShapeDtypeStruct((1, LANE), F32),
        ],
        scratch_shapes=[pltpu.VMEM((nn, w), F32), pltpu.VMEM((ll, w), F32), pltpu.VMEM((ll, w), F32)],
        compiler_params=_params(("arbitrary",)),
    )(dyn, y, z, xbc, dt, sprev, a_log, d_exp, m_norm, e_mat, et_mat)


def mix_out_fwd(ya_in, yn, gates, h, w_sco, w_mo, w_o):
    t, d = h.shape
    tm = _tile(t, 256)

    def body(ya_ref, yn_ref, g_ref, h_ref, wa_ref, wm_ref, wo_ref, ho_ref, oa_ref, om_ref, mg_ref):
        y_a = _dot(ya_ref[...], wa_ref[...])
        y_m = _dot(yn_ref[...], wm_ref[...])
        oa_ref[...] = y_a
        om_ref[...] = y_m
        mg = (_sigmoid(g_ref[:, :d]) * y_a + _sigmoid(g_ref[:, d:]) * y_m).astype(BF)
        mg_ref[...] = mg
        ho_ref[...] = h_ref[...] + _dot(mg, wo_ref[...])

    row = lambda c: pl.BlockSpec((tm, c), lambda i: (i, 0))
    full = lambda a: pl.BlockSpec(a.shape, lambda i: (0, 0))
    return pl.pallas_call(
        body,
        name="mix_out_fwd",
        grid=(t // tm,),
        in_specs=[row(d), row(2 * d), row(2 * d), row(d), full(w_sco), full(w_mo), full(w_o)],
        out_specs=[row(d), row(d), row(d), row(d)],
        out_shape=[jax.ShapeDtypeStruct((t, d), F32), jax.ShapeDtypeStruct((t, d), F32), jax.ShapeDtypeStruct((t, d), F32),
                   jax.ShapeDtypeStruct((t, d), BF)],
        compiler_params=_params(("parallel",)),
    )(ya_in, yn, gates, h, w_sco, w_mo, w_o)


def mix_out_bwd(dh, gates, y_a, y_m, w_sco, w_mo, w_o):
    t, d = dh.shape
    tm = _tile(t, 256)

    def body(dh_ref, g_ref, ya_ref, ym_ref, wa_ref, wm_ref, wo_ref, dg_ref, dya_ref, dyn_ref, da_ref, dm_ref):
        dmg = _dot(dh_ref[...].astype(BF), wo_ref[...], 1, 1)
        sa, sm = _sigmoid(g_ref[:, :d]), _sigmoid(g_ref[:, d:])
        dg_ref[:, :d] = (dmg * ya_ref[...] * sa * (1.0 - sa)).astype(BF)
        dg_ref[:, d:] = (dmg * ym_ref[...] * sm * (1.0 - sm)).astype(BF)
        da = (dmg * sa).astype(BF)
        dm = (dmg * sm).astype(BF)
        da_ref[...] = da
        dm_ref[...] = dm
        dya_ref[...] = _dot(da, wa_ref[...], 1, 1)
        dyn_ref[...] = _dot(dm, wm_ref[...], 1, 1)

    row = lambda c: pl.BlockSpec((tm, c), lambda i: (i, 0))
    full = lambda a: pl.BlockSpec(a.shape, lambda i: (0, 0))
    return pl.pallas_call(
        body,
        name="mix_out_bwd",
        grid=(t // tm,),
        in_specs=[row(d), row(2 * d), row(d), row(d), full(w_sco), full(w_mo), full(w_o)],
        out_specs=[row(2 * d), row(d), row(2 * d), row(d), row(d)],
        out_shape=[jax.ShapeDtypeStruct((t, 2 * d), BF), jax.ShapeDtypeStruct((t, d), F32), jax.ShapeDtypeStruct((t, 2 * d), F32),
                   jax.ShapeDtypeStruct((t, d), BF), jax.ShapeDtypeStruct((t, d), BF)],
        compiler_params=_params(("parallel",)),
    )(dh, gates, y_a, y_m, w_sco, w_mo, w_o)


def norm_bwd_add(dh, h, g, dn):
    def fn(i, nt, rows, vecs, prevs, nexts):
        dx, dg = _rms_bwd(rows[1], vecs[0], rows[2])
        return [rows[0] + dx], [dg]
    d = h.shape[1]
    return ew(fn, [dh, h, dn], [g], [(d, F32)], [(1, d)], tm=512, name="norm_bwd_add")


def ple_fwd(h, g, p, w_pg, w_pp):
    t, d = h.shape
    tm = _tile(t, 512)

    def body(h_ref, g_ref, p_ref, wg_ref, wp_ref, ho_ref):
        hv = h_ref[...]
        gate = _sigmoid(_dot(_rms(hv, g_ref[...]).astype(BF), wg_ref[...]))
        ho_ref[...] = hv + gate * _dot(p_ref[...].astype(BF), wp_ref[...])

    row = lambda c: pl.BlockSpec((tm, c), lambda i: (i, 0))
    full = lambda a: pl.BlockSpec(a.shape, lambda i: (0, 0))
    return pl.pallas_call(
        body,
        name="ple_fwd",
        grid=(t // tm,),
        in_specs=[row(d), full(g), row(p.shape[1]), full(w_pg), full(w_pp)],
        out_specs=row(d),
        out_shape=jax.ShapeDtypeStruct((t, d), F32),
        compiler_params=_params(("parallel",)),
    )(h, g, p, w_pg, w_pp)


def ple_bwd(dho, h, g, p, w_pg, w_pp):
    t, d = h.shape
    tm = _tile(t, 512)

    def body(dho_ref, h_ref, g_ref, p_ref, wg_ref, wp_ref, dh_ref, dg_ref, n_ref, dgp_ref, dpe_ref):
        i = pl.program_id(0)
        hv, dv = h_ref[...], dho_ref[...]
        n = _rms(hv, g_ref[...]).astype(BF)
        n_ref[...] = n
        gate = _sigmoid(_dot(n, wg_ref[...]))
        pe = _dot(p_ref[...].astype(BF), wp_ref[...])
        dpe_ref[...] = (dv * gate).astype(BF)
        dgp = (dv * pe * gate * (1.0 - gate)).astype(BF)
        dgp_ref[...] = dgp
        dx, dg = _rms_bwd(hv, g_ref[...], _dot(dgp, wg_ref[...], 1, 1))
        dh_ref[...] = dv + dx

        @pl.when(i == 0)
        def _():
            dg_ref[...] = dg

        @pl.when(i > 0)
        def _():
            dg_ref[...] += dg

    row = lambda c: pl.BlockSpec((tm, c), lambda i: (i, 0))
    full = lambda a: pl.BlockSpec(a.shape, lambda i: (0, 0))
    return pl.pallas_call(
        body,
        name="ple_bwd",
        grid=(t // tm,),
        in_specs=[row(d), row(d), full(g), row(p.shape[1]), full(w_pg), full(w_pp)],
        out_specs=[row(d), pl.BlockSpec((1, d), lambda i: (0, 0)), row(d), row(d), row(d)],
        out_shape=[jax.ShapeDtypeStruct((t, d), F32), jax.ShapeDtypeStruct((1, d), F32), jax.ShapeDtypeStruct((t, d), BF),
                   jax.ShapeDtypeStruct((t, d), BF), jax.ShapeDtypeStruct((t, d), BF)],
        compiler_params=_params(("arbitrary",)),
    )(dho, h, g, p, w_pg, w_pp)


def loss_bwd(h, g, target):
    d = h.shape[1]

    def fn(i, nt, rows, vecs, prevs, nexts):
        err = _rms(rows[0], vecs[0]) - rows[1]
        dx, dg = _rms_bwd(rows[0], vecs[0], err * (1.0 / d))
        return [dx], [jnp.sum(err * err, axis=0, keepdims=True) * (0.5 / d), dg]

    return ew(fn, [h, target], [g], [(d, F32)], [(1, d), (1, d)], tm=512, name="loss_bwd")


def adamw(w, g, m, v, name):
    c1, c2 = 1.0 / (1.0 - ADAM_B1 ** ADAM_STEP), 1.0 / (1.0 - ADAM_B2 ** ADAM_STEP)

    def fn(i, nt, rows, vecs, prevs, nexts):
        wv, gv, mv, vv = rows
        mn = ADAM_B1 * mv + (1.0 - ADAM_B1) * gv
        vn = ADAM_B2 * vv + (1.0 - ADAM_B2) * (gv * gv)
        delta = -ADAM_LR * ((mn * c1) / (jnp.sqrt(vn * c2) + ADAM_EPS) + ADAM_WD * wv)
        return [delta, mn, vn], []

    c = w.shape[1]
    cap = max(HALO, ADAMW_TILE_ELEMS // c // HALO * HALO)
    return ew(fn, [w, g, m, v], [], [(c, F32)] * 3, tm=_tile(w.shape[0], cap, HALO), name=name)


def _place():
    return lax.axis_index("x"), lax.axis_index("y"), lax.axis_index("c")


def _other_chips(x, y):
    return [(1 - x, y), (x, 1 - y), (1 - x, 1 - y)]


ANY = pl.BlockSpec(memory_space=pl.ANY)


def all_gather_xy(shard, name):
    r, c = shard.shape
    hr = r // 2
    assert r % 32 == 0

    def body(x_ref, out_ref, send_sems, recv_sems, local_sem):
        x, y, cc = _place()
        chips = _other_chips(x, y)
        mine = pl.ds(pl.multiple_of(cc * hr, 16), hr)
        theirs = pl.ds(pl.multiple_of((1 - cc) * hr, 16), hr)
        k_me = 2 * x + y

        def copy(k, src, dst, to):
            return pltpu.make_async_remote_copy(src_ref=src, dst_ref=dst, send_sem=send_sems.at[k], recv_sem=recv_sems.at[k],
                                                device_id=to, device_id_type=MESH)

        own = pltpu.make_async_copy(x_ref, out_ref.at[k_me], local_sem)
        own.start()
        first = [copy(j, x_ref.at[mine], out_ref.at[k_me, mine], (*chip, cc)) for j, chip in enumerate(chips)]
        for cp in first:
            cp.start()
        passed = []
        for j, (px, py) in enumerate(chips):
            landed = out_ref.at[2 * px + py, mine]
            copy(j, landed, landed, (px, py, cc)).wait_recv()
            fw = copy(3 + j, landed, landed, (x, y, 1 - cc))
            fw.start()
            passed.append(fw)
        for j, (px, py) in enumerate(chips):
            landed = out_ref.at[2 * px + py, theirs]
            copy(3 + j, landed, landed, (x, y, 1 - cc)).wait_recv()
        for cp in first + passed:
            cp.wait_send()
        own.wait()

    return pl.pallas_call(
        body,
        name=name,
        in_specs=[ANY],
        out_specs=ANY,
        out_shape=jax.ShapeDtypeStruct((4, r, c), shard.dtype),
        scratch_shapes=[pltpu.SemaphoreType.DMA((6,)), pltpu.SemaphoreType.DMA((6,)), pltpu.SemaphoreType.DMA],
    )(shard)


def swap_halves(g4, name):
    _, r, c = g4.shape
    hr = r // 2

    def body(g_ref, out_ref, send_sem, recv_sem):
        x, y, cc = _place()
        theirs = pl.ds(pl.multiple_of((1 - cc) * hr, 16), hr)
        cp = pltpu.make_async_remote_copy(src_ref=g_ref.at[:, theirs], dst_ref=out_ref, send_sem=send_sem, recv_sem=recv_sem,
                                          device_id=(x, y, 1 - cc), device_id_type=MESH)
        cp.start()
        cp.wait()

    return pl.pallas_call(
        body,
        name=name,
        in_specs=[ANY],
        out_specs=ANY,
        out_shape=jax.ShapeDtypeStruct((4, hr, c), g4.dtype),
        scratch_shapes=[pltpu.SemaphoreType.DMA, pltpu.SemaphoreType.DMA],
    )(g4)


def scatter_xy(cs4, name):
    _, hr, c = cs4.shape

    def body(g_ref, out_ref, send_sems, recv_sems):
        x, y, cc = _place()
        cps = []
        for j, (px, py) in enumerate(_other_chips(x, y)):
            cp = pltpu.make_async_remote_copy(src_ref=g_ref.at[2 * px + py], dst_ref=out_ref.at[j], send_sem=send_sems.at[j],
                                              recv_sem=recv_sems.at[j], device_id=(px, py, cc), device_id_type=MESH)
            cp.start()
            cps.append(cp)
        for cp in cps:
            cp.wait()

    return pl.pallas_call(
        body,
        name=name,
        in_specs=[ANY],
        out_specs=ANY,
        out_shape=jax.ShapeDtypeStruct((3, hr, c), cs4.dtype),
        scratch_shapes=[pltpu.SemaphoreType.DMA((3,)), pltpu.SemaphoreType.DMA((3,))],
    )(cs4)


def join_halves(half, name):
    hr, c = half.shape

    def body(h_ref, out_ref, send_sem, recv_sem, local_sem):
        x, y, cc = _place()
        mine = pl.ds(pl.multiple_of(cc * hr, 8), hr)
        own = pltpu.make_async_copy(h_ref, out_ref.at[mine], local_sem)
        own.start()
        cp = pltpu.make_async_remote_copy(src_ref=h_ref, dst_ref=out_ref.at[mine], send_sem=send_sem, recv_sem=recv_sem,
                                          device_id=(x, y, 1 - cc), device_id_type=MESH)
        cp.start()
        cp.wait()
        own.wait()

    return pl.pallas_call(
        body,
        name=name,
        in_specs=[ANY],
        out_specs=ANY,
        out_shape=jax.ShapeDtypeStruct((2 * hr, c), half.dtype),
        scratch_shapes=[pltpu.SemaphoreType.DMA, pltpu.SemaphoreType.DMA, pltpu.SemaphoreType.DMA],
    )(half)


def all_gather_8(block, name):
    m, c = block.shape

    def body(x_ref, out_ref, send_sems, recv_sems, local_sem):
        x, y, cc = _place()
        me, sibling = (x, y, cc), (x, y, 1 - cc)
        chips = _other_chips(x, y)

        def rows(px, py, pc):
            return out_ref.at[4 * px + 2 * py + pc]

        def copy(k, blk, to, src=None):
            return pltpu.make_async_remote_copy(src_ref=rows(*blk) if src is None else src, dst_ref=rows(*blk), send_sem=send_sems.at[k],
                                                recv_sem=recv_sems.at[k], device_id=to, device_id_type=MESH)

        mine = pltpu.make_async_copy(x_ref, rows(*me), local_sem)
        mine.start()
        first = [copy(0, me, sibling, src=x_ref)]
        first += [copy(1 + j, me, (*chip, cc), src=x_ref) for j, chip in enumerate(chips)]
        for cp in first:
            cp.start()
        passed = [copy(4 + j, (*chip, cc), sibling) for j, chip in enumerate(chips)]
        for j, chip in enumerate(chips):
            copy(1 + j, (*chip, cc), me).wait_recv()
            passed[j].start()
        copy(0, sibling, me).wait_recv()
        for j, chip in enumerate(chips):
            copy(4 + j, (*chip, 1 - cc), me).wait_recv()
        for cp in first + passed:
            cp.wait_send()
        mine.wait()

    return pl.pallas_call(
        body,
        name=name,
        in_specs=[pl.BlockSpec(memory_space=pltpu.VMEM)],
        out_specs=pl.BlockSpec(memory_space=pltpu.VMEM),
        out_shape=jax.ShapeDtypeStruct((8, m, c), block.dtype),
        scratch_shapes=[pltpu.SemaphoreType.DMA((7,)), pltpu.SemaphoreType.DMA((7,)), pltpu.SemaphoreType.DMA],
        compiler_params=pltpu.CompilerParams(vmem_limit_bytes=VMEM_LIMIT),
    )(block)


def add_parts(parts, out_dtype, name, tm=512):
    def fn(i, nt, rows, vecs, prevs, nexts):
        acc = rows[0].astype(F32)
        for r_ in rows[1:]:
            acc = acc + r_.astype(F32)
        return [acc], []
    r, c = parts[0].shape
    return ew(fn, list(parts), [], [(c, out_dtype)], tm=_tile(r, tm, 16), name=name)[0]


PACK_COLS = 1024
BIG = (
    ("ffn1_wg", 2), ("ffn1_wu", 2), ("ffn1_wd", 1), ("w_in", 2), ("sc_w_out", 1), ("m_w_out", 1), ("w_o", 1),
    ("ffn2_wg", 2), ("ffn2_wu", 2), ("ffn2_wd", 1), ("ple_w_gate", 1), ("ple_w_proj", 2),
)
SMALL_SHARDED = (("sc_conv_w", 2), ("m_conv_w", 2))
SMALL_REPL = ("ffn1_norm", "mix_norm", "m_conv_b", "m_dt_bias", "m_A_log", "m_D", "m_norm", "ffn2_norm", "ple_norm", "final_norm")


def _pack(arrs, cols, row_mult):
    flat = jnp.concatenate([a.reshape(-1) for a in arrs])
    n = flat.shape[0]
    rows = -(-n // cols)
    rows = -(-rows // row_mult) * row_mult
    return jnp.pad(flat, (0, rows * cols - n)).reshape(rows, cols)


def _unpack(flat2d, shapes):
    flat = flat2d.reshape(-1)
    out, off = [], 0
    for s in shapes:
        n = int(np.prod(s))
        out.append(flat[off:off + n].reshape(s))
        off += n
    return out


def _split4(full, axis):
    return jnp.split(full, 4, axis=axis)


def kernel(x, p, ffn1_norm, ffn1_wg, ffn1_wu, ffn1_wd, mix_norm, w_in, sc_conv_w, sc_w_out, m_conv_w, m_conv_b, m_dt_bias, m_A_log, m_D, m_norm, m_w_out, w_o, ffn2_norm, ffn2_wg, ffn2_wu, ffn2_wd, ple_norm, ple_w_gate, ple_w_proj, final_norm, loss_target, m_ffn1_norm, m_ffn1_wg, m_ffn1_wu, m_ffn1_wd, m_mix_norm, m_w_in, m_sc_conv_w, m_sc_w_out, m_m_conv_w, m_m_conv_b, m_m_dt_bias, m_m_A_log, m_m_D, m_m_norm, m_m_w_out, m_w_o, m_ffn2_norm, m_ffn2_wg, m_ffn2_wu, m_ffn2_wd, m_ple_norm, m_ple_w_gate, m_ple_w_proj, m_final_norm, v_ffn1_norm, v_ffn1_wg, v_ffn1_wu, v_ffn1_wd, v_mix_norm, v_w_in, v_sc_conv_w, v_sc_w_out, v_m_conv_w, v_m_conv_b, v_m_dt_bias, v_m_A_log, v_m_D, v_m_norm, v_m_w_out, v_w_o, v_ffn2_norm, v_ffn2_wg, v_ffn2_wu, v_ffn2_wd, v_ple_norm, v_ple_w_gate, v_ple_w_proj, v_final_norm):
    args = dict(locals())
    names = list(BIG_NAMES) + [n for n, _ in SMALL_SHARDED] + list(SMALL_REPL)
    wts = {n: args[n] for n in names}
    mom = {n: args["m_" + n] for n in names}
    vel = {n: args["v_" + n] for n in names}

    depth = ffn1_norm.shape[0]
    d = x.shape[-1]
    t = x.shape[1]
    w = 2 * d
    hh = w // SSM_P
    cw = w + 2 * SSM_G * SSM_N
    my_x, my_y, my_c = _place()
    k_me = 2 * my_x + my_y

    big_local = [wts[n] for n, _ in BIG]
    packed = _pack([a.astype(BF) for a in big_local], PACK_COLS, 32)
    gathered = all_gather_xy(packed, "gather_weights")
    full = {}
    per_shard = [_unpack(gathered[k], [a.shape for a in big_local]) for k in range(4)]
    for idx, (n, axis) in enumerate(BIG):
        full[n] = jnp.concatenate([per_shard[k][idx] for k in range(4)], axis=axis)
    small_local = [wts[n] for n, _ in SMALL_SHARDED]
    gathered_s = all_gather_xy(_pack(small_local, LANE, 32), "gather_conv_weights")
    per_shard_s = [_unpack(gathered_s[k], [a.shape for a in small_local]) for k in range(4)]
    for idx, (n, axis) in enumerate(SMALL_SHARDED):
        full[n] = jnp.concatenate([per_shard_s[k][idx] for k in range(4)], axis=axis)

    wi = full["w_in"]
    o_z, o_xbc, o_dt, o_g = 3 * d, 5 * d, 5 * d + cw, 5 * d + cw + hh
    w_sc3, w_z, w_xbc = wi[:, :, :o_z], wi[:, :, o_z:o_xbc], wi[:, :, o_xbc:o_dt]
    w_g2 = wi[:, :, o_g:o_g + 2 * d]
    w_dt = jnp.pad(wi[:, :, o_dt:o_g], ((0, 0), (0, 0), (0, LANE - hh)))
    w_in_p = jnp.concatenate([w_sc3, w_z, w_xbc, w_g2, w_dt], axis=2)

    pad_h = lambda a: jnp.pad(a, ((0, 0), (0, LANE - hh)))
    dt_bias_p, a_log_p = pad_h(m_dt_bias), pad_h(m_A_log)
    d_exp = jnp.repeat(m_D, SSM_P, axis=1)
    e_mat = (jnp.arange(w)[None, :] // SSM_P == jnp.arange(LANE)[:, None]).astype(F32)
    et_mat = e_mat.T

    h = x[0]
    saved = []
    for i in range(depth):
        s = {}
        s["h0"] = h
        h, s["a1"], s["b1"] = ffn_fwd(h, ffn1_norm[i:i + 1], full["ffn1_wg"][i], full["ffn1_wu"][i], full["ffn1_wd"][i])
        s["h1"] = h
        u = norm_cast(h, mix_norm[i:i + 1])
        s["u"] = u
        s["sc3"] = mm(u, w_sc3[i], name="proj_sc")
        s["z"] = mm(u, w_z[i], name="proj_z")
        s["xbc_raw"] = mm(u, w_xbc[i], name="proj_xbc")
        s["gates"] = mm(u, w_g2[i], name="proj_gates")
        s["dt_raw"] = mm(u, w_dt[i], name="proj_dt")
        s["ya_in"] = conv_a_fwd(s["sc3"], full["sc_conv_w"][i])
        s["xbc"], s["dt"] = conv_m_fwd(s["xbc_raw"], s["dt_raw"], full["m_conv_w"][i], m_conv_b[i:i + 1], dt_bias_p[i:i + 1])
        s["yn"], s["y"], s["sprev"] = ssd_fwd(s["xbc"], s["dt"], s["z"], a_log_p[i:i + 1], d_exp[i:i + 1], m_norm[i:i + 1], e_mat)
        h, s["y_a"], s["y_m"], s["merged"] = mix_out_fwd(s["ya_in"], s["yn"], s["gates"], h, full["sc_w_out"][i], full["m_w_out"][i],
                                                          full["w_o"][i])
        s["h2"] = h
        h, s["a2"], s["b2"] = ffn_fwd(h, ffn2_norm[i:i + 1], full["ffn2_wg"][i], full["ffn2_wu"][i], full["ffn2_wd"][i])
        s["h3"] = h
        h = ple_fwd(h, ple_norm[i:i + 1], p[i, 0], full["ple_w_gate"][i], full["ple_w_proj"][i])
        saved.append(s)

    dh, loss_lanes, g_final = loss_bwd(h, final_norm[None, :], loss_target[0])
    loss = lax.psum(jnp.sum(loss_lanes), ("x", "y", "c"))

    gb = {n: [None] * depth for n, _ in BIG}
    gs = {n: [None] * depth for n in [n for n, _ in SMALL_SHARDED] + list(SMALL_REPL) if n != "final_norm"}
    for i in reversed(range(depth)):
        s = saved[i]
        dh, gs["ple_norm"][i], n3, dgp, dpe = ple_bwd(dh, s["h3"], ple_norm[i:i + 1], p[i, 0], full["ple_w_gate"][i], full["ple_w_proj"][i])
        gb["ple_w_gate"][i] = mm(n3, dgp, ta=True, out_dtype=BF, name="g_ple_gate")
        gb["ple_w_proj"][i] = mm(p[i, 0], dpe, ta=True, out_dtype=BF, name="g_ple_proj")
        dh, gs["ffn2_norm"][i], n2, do2, s2, da2, db2 = ffn_bwd(dh, s["h2"], ffn2_norm[i:i + 1], s["a2"], s["b2"], full["ffn2_wg"][i],
                                                                full["ffn2_wu"][i], full["ffn2_wd"][i])
        gb["ffn2_wg"][i] = mm(n2, da2, ta=True, out_dtype=BF, name="g_ffn_in")
        gb["ffn2_wu"][i] = mm(n2, db2, ta=True, out_dtype=BF, name="g_ffn_in")
        gb["ffn2_wd"][i] = mm(s2, do2, ta=True, out_dtype=BF, name="g_ffn_out")
        dgates, dya, dyn, dy_a, dy_m = mix_out_bwd(dh, s["gates"], s["y_a"], s["y_m"], full["sc_w_out"][i], full["m_w_out"][i], full["w_o"][i])
        gb["w_o"][i] = mm(s["merged"], dh, ta=True, out_dtype=BF, name="g_w_o")
        gb["sc_w_out"][i] = mm(s["ya_in"], dy_a, ta=True, out_dtype=BF, name="g_sc_out")
        gb["m_w_out"][i] = mm(s["yn"], dy_m, ta=True, out_dtype=BF, name="g_m_out")
        dz, dxbc, ddt, gs["m_norm"][i], gd, gal = ssd_bwd(dyn, s["y"], s["z"], s["xbc"], s["dt"], s["sprev"], a_log_p[i:i + 1], d_exp[i:i + 1],
                                                          m_norm[i:i + 1], e_mat, et_mat)
        gs["m_D"][i], gs["m_A_log"][i] = gd[:, :hh], gal[:, :hh]
        dpre, ddt_raw, gdb = conv_m_bwd1(dxbc, s["xbc_raw"], ddt, s["dt_raw"], full["m_conv_w"][i], m_conv_b[i:i + 1], dt_bias_p[i:i + 1])
        gs["m_dt_bias"][i] = gdb[:, :hh]
        dxbc_raw, gs["m_conv_w"][i], gs["m_conv_b"][i] = conv_bwd2(dpre, s["xbc_raw"], full["m_conv_w"][i], "conv_m_bwd2")
        dcv, dsc_b, v = conv_a_bwd1(dya, s["sc3"], full["sc_conv_w"][i])
        dsc_c, dsc_x, gs["sc_conv_w"][i] = conv_a_bwd2(dcv, v, s["sc3"], full["sc_conv_w"][i])
        dproj = jnp.concatenate([dsc_b, dsc_c, dsc_x, dz, dxbc_raw, dgates, ddt_raw], axis=1)
        du = mm(dproj, w_in_p[i], tb=True, name="d_proj_in")
        gwp = mm(s["u"], dproj, ta=True, out_dtype=BF, name="g_w_in")
        gb["w_in"][i] = jnp.concatenate([gwp[:, :5 * d + cw], gwp[:, 7 * d + cw:7 * d + cw + hh], gwp[:, 5 * d + cw:7 * d + cw]], axis=1)
        dh, gs["mix_norm"][i] = norm_bwd_add(dh, s["h1"], mix_norm[i:i + 1], du)
        dh, gs["ffn1_norm"][i], n1, do1, s1, da1, db1 = ffn_bwd(dh, s["h0"], ffn1_norm[i:i + 1], s["a1"], s["b1"], full["ffn1_wg"][i],
                                                                full["ffn1_wu"][i], full["ffn1_wd"][i])
        gb["ffn1_wg"][i] = mm(n1, da1, ta=True, out_dtype=BF, name="g_ffn_in")
        gb["ffn1_wu"][i] = mm(n1, db1, ta=True, out_dtype=BF, name="g_ffn_in")
        gb["ffn1_wd"][i] = mm(s1, do1, ta=True, out_dtype=BF, name="g_ffn_out")
    grad_x = dh[None]

    gfull = {n: jnp.stack(gb[n]) for n, _ in BIG}
    blocks = [_pack([_split4(gfull[n], axis)[k] for n, axis in BIG], PACK_COLS, 32) for k in range(4)]
    g4 = jnp.stack(blocks)
    rr = g4.shape[1]
    hr = rr // 2
    from_sibling = swap_halves(g4, "grad_swap_halves")
    my_half = lax.dynamic_slice_in_dim(g4, my_c * hr, hr, axis=1)
    cs4 = add_parts([my_half.reshape(4 * hr, PACK_COLS), from_sibling.reshape(4 * hr, PACK_COLS)], BF, "grad_add_sibling").reshape(4, hr, PACK_COLS)
    got = scatter_xy(cs4, "grad_scatter")
    own = lax.dynamic_index_in_dim(cs4, k_me, axis=0, keepdims=False)
    half_sum = add_parts([own, got[0], got[1], got[2]], F32, "grad_add_chips")
    g_local = join_halves(half_sum, "grad_join_halves")
    g_big = dict(zip([n for n, _ in BIG], _unpack(g_local, [a.shape for a in big_local]), strict=True))

    small_names = [n for n, _ in SMALL_SHARDED] + list(SMALL_REPL)
    small_full = []
    for n in small_names:
        if n == "final_norm":
            small_full.append(g_final[0])
        else:
            small_full.append(jnp.stack(gs[n]))
    small_pack = _pack(small_full, LANE, HALO)
    all8 = all_gather_8(small_pack, "gather_small_grads")
    small_sum = add_parts([all8[k] for k in range(8)], F32, "add_small_grads", tm=256)
    small_tot = _unpack(small_sum, [a.shape for a in small_full])
    g_small = {}
    for n, tot in zip(small_names, small_tot, strict=True):
        if n == "final_norm":
            g_small[n] = tot
        elif n in ("sc_conv_w", "m_conv_w"):
            cl = wts[n].shape[2]
            g_small[n] = lax.dynamic_slice_in_dim(tot, k_me * cl, cl, axis=2)
        else:
            g_small[n] = tot.reshape(wts[n].shape)

    grads, delta, new_m, new_v = {}, {}, {}, {}
    for n, _ in BIG:
        shp = wts[n].shape
        two = lambda a: a.reshape(-1, shp[-1])
        grads[n] = g_big[n]
        dl, nm, nv = adamw(two(wts[n]), two(g_big[n]), two(mom[n]), two(vel[n]), "adamw_" + "x".join(map(str, shp[1:])))
        delta[n], new_m[n], new_v[n] = dl.reshape(shp), nm.reshape(shp), nv.reshape(shp)
    sm_shapes = [wts[n].shape for n in small_names]
    pk = lambda dct: _pack([dct[n] for n in small_names], LANE, HALO)
    dl, nm, nv = adamw(pk(wts), pk(g_small), pk(mom), pk(vel), "adamw_small")
    for n, a, b_, c_ in zip(small_names, _unpack(dl, sm_shapes), _unpack(nm, sm_shapes), _unpack(nv, sm_shapes), strict=True):
        grads[n], delta[n], new_m[n], new_v[n] = g_small[n], a, b_, c_

    order = ["ffn1_norm", "ffn1_wg", "ffn1_wu", "ffn1_wd", "mix_norm", "w_in", "sc_conv_w", "sc_w_out", "m_conv_w", "m_conv_b", "m_dt_bias",
             "m_A_log", "m_D", "m_norm", "m_w_out", "w_o", "ffn2_norm", "ffn2_wg", "ffn2_wu", "ffn2_wd", "ple_norm", "ple_w_gate", "ple_w_proj",
             "final_norm"]
    return (loss, grad_x, *[grads[n] for n in order], *[delta[n] for n in order], *[new_m[n] for n in order], *[new_v[n] for n in order])


BIG_NAMES = tuple(n for n, _ in BIG)
```

```python
import jax
import jax.numpy as jnp
import numpy as np
from jax import lax
from jax.experimental import pallas as pl
from jax.experimental.pallas import tpu as pltpu

BF = jnp.bfloat16
F32 = jnp.float32
EPS = 1e-6
LANE = 128
HALO = 8
SSM_P = 64
SSM_N = 128
SSM_G = 4
SSM_L = 128
ADAM_LR, ADAM_B1, ADAM_B2, ADAM_EPS, ADAM_WD, ADAM_STEP = 0.001, 0.9, 0.999, 1e-08, 0.01, 10
VMEM_LIMIT = 56 * 1024 * 1024
TILE_ELEMS = 400_000
NCHIP = 4
MESH = pl.DeviceIdType.MESH
HI = lax.Precision.HIGHEST


def _tile(n, cap, mult=LANE):
    best = None
    t = mult
    while t <= min(n, cap):
        if n % t == 0:
            best = t
        t += mult
    return best if best is not None else n


def _row_tile(r, c, mult=16):
    return _tile(r, max(mult, TILE_ELEMS // c // mult * mult), mult)


def _params(sem):
    return pltpu.CompilerParams(dimension_semantics=sem, vmem_limit_bytes=VMEM_LIMIT)


def _sigmoid(x):
    return 1.0 / (1.0 + jnp.exp(-x))


def _dot(a, b, ca=1, cb=0, precision=None):
    return lax.dot_general(a, b, (((ca,), (cb,)), ((), ())), precision=precision, preferred_element_type=F32)


def _rms(x, g):
    r = lax.rsqrt(jnp.mean(x * x, axis=-1, keepdims=True) + EPS)
    return x * r * g


def _rms_bwd(x, g, dy):
    r = lax.rsqrt(jnp.mean(x * x, axis=-1, keepdims=True) + EPS)
    xh = x * r
    dxh = dy * g
    dx = r * (dxh - xh * jnp.mean(dxh * xh, axis=-1, keepdims=True))
    return dx, jnp.sum(dy * xh, axis=0, keepdims=True)


def _accumulate(ref, val, first):
    @pl.when(first)
    def _():
        ref[...] = val

    @pl.when(jnp.logical_not(first))
    def _():
        ref[...] += val


def mmx(name, a, b, *, grid, a_spec, b_spec, o_spec, o_shape, o_dtype, ca, cb, acc_shape=None):
    nk = grid[-1] if acc_shape is not None else 1

    def body(a_ref, b_ref, o_ref, *acc):
        p = _dot(a_ref[...].astype(BF), b_ref[...].astype(BF), ca, cb)
        if nk == 1:
            o_ref[...] = p.astype(o_ref.dtype)
        else:
            kk = pl.program_id(len(grid) - 1)
            _accumulate(acc[0], p, kk == 0)

            @pl.when(kk == nk - 1)
            def _():
                o_ref[...] = acc[0][...].astype(o_ref.dtype)

    sem = ("parallel",) * (len(grid) - 1) + ("arbitrary" if nk > 1 else "parallel",)
    return pl.pallas_call(
        body,
        name=name,
        grid=grid,
        in_specs=[a_spec, b_spec],
        out_specs=o_spec,
        out_shape=jax.ShapeDtypeStruct(o_shape, o_dtype),
        scratch_shapes=[pltpu.VMEM(acc_shape, F32)] if nk > 1 else [],
        compiler_params=_params(sem),
    )(a, b)


def mm(a, b, *, ta=False, tb=False, out_dtype=F32, name, tm_cap=1024, tn_cap=1024, tk_cap=4096):
    m, k = (a.shape[1], a.shape[0]) if ta else a.shape
    n = b.shape[0] if tb else b.shape[1]
    assert (b.shape[1] if tb else b.shape[0]) == k
    tm, tn, tk = _tile(m, tm_cap), _tile(n, tn_cap), _tile(k, tk_cap)
    nk = k // tk
    a_spec = pl.BlockSpec((tk, tm), lambda i, j, kk: (kk, i)) if ta else pl.BlockSpec((tm, tk), lambda i, j, kk: (i, kk))
    b_spec = pl.BlockSpec((tn, tk), lambda i, j, kk: (j, kk)) if tb else pl.BlockSpec((tk, tn), lambda i, j, kk: (kk, j))
    return mmx(name, a, b, grid=(m // tm, n // tn, nk), a_spec=a_spec, b_spec=b_spec, o_spec=pl.BlockSpec((tm, tn), lambda i, j, kk: (i, j)),
               o_shape=(m, n), o_dtype=out_dtype, ca=0 if ta else 1, cb=1 if tb else 0, acc_shape=(tm, tn) if nk > 1 else None)


def ew(fn, rows, vecs, out_rows, out_red=(), *, tm, name, prev_halo=(), next_halo=()):
    t = rows[0].shape[0]
    tm = min(tm, t)
    nt = t // tm
    assert t % tm == 0 and tm % HALO == 0
    nr, nv, npv, nnx, nor = len(rows), len(vecs), len(prev_halo), len(next_halo), len(out_rows)
    hb = tm // HALO

    def body(*refs):
        i = pl.program_id(0)
        ins = [r[...].astype(F32) for r in refs[: nr + nv + npv + nnx]]
        outs = refs[nr + nv + npv + nnx:]
        o_rows, o_red = fn(i, nt, ins[:nr], ins[nr:nr + nv], ins[nr + nv:nr + nv + npv], ins[nr + nv + npv:])
        for ref, val in zip(outs[:nor], o_rows, strict=True):
            ref[...] = val.astype(ref.dtype)
        for ref, val in zip(outs[nor:], o_red, strict=True):
            _accumulate(ref, val, i == 0)

    in_specs = [pl.BlockSpec((tm, r.shape[1]), lambda i: (i, 0)) for r in rows]
    in_specs += [pl.BlockSpec(v.shape, lambda i: (0, 0)) for v in vecs]
    in_specs += [pl.BlockSpec((HALO, rows[k].shape[1]), lambda i: (jnp.maximum(i * hb - 1, 0), 0)) for k in prev_halo]
    in_specs += [pl.BlockSpec((HALO, rows[k].shape[1]), lambda i: (jnp.minimum((i + 1) * hb, t // HALO - 1), 0)) for k in next_halo]
    out_specs = [pl.BlockSpec((tm, c), lambda i: (i, 0)) for c, _ in out_rows]
    out_specs += [pl.BlockSpec(s, lambda i: (0, 0)) for s in out_red]
    out_shape = [jax.ShapeDtypeStruct((t, c), d) for c, d in out_rows] + [jax.ShapeDtypeStruct(s, F32) for s in out_red]
    return pl.pallas_call(
        body,
        name=name,
        grid=(nt,),
        in_specs=in_specs,
        out_specs=out_specs,
        out_shape=out_shape,
        compiler_params=_params(("arbitrary",) if out_red else ("parallel",)),
    )(*rows, *vecs, *[rows[k] for k in prev_halo], *[rows[k] for k in next_halo])


def _shift_down(x, prev, j):
    if j == 0:
        return x
    r = pltpu.roll(x, j, 0)
    rh = pltpu.roll(prev, j, 0)
    row = lax.broadcasted_iota(jnp.int32, (HALO, x.shape[1]), 0)
    head = jnp.where(row < j, rh, r[:HALO])
    return jnp.concatenate([head, r[HALO:]], axis=0)


def _shift_up(x, nxt, j):
    if j == 0:
        return x
    n = x.shape[0]
    r = pltpu.roll(x, n - j, 0)
    rh = pltpu.roll(nxt, HALO - j, 0)
    row = lax.broadcasted_iota(jnp.int32, (HALO, x.shape[1]), 0)
    tail = jnp.where(row >= HALO - j, rh, r[n - HALO:])
    return jnp.concatenate([r[: n - HALO], tail], axis=0)


def _conv_fwd(x, prev, w):
    kk = w.shape[0]
    acc = None
    for k in range(kk):
        term = w[k:k + 1, :] * _shift_down(x, prev, kk - 1 - k)
        acc = term if acc is None else acc + term
    return acc


def ffn_up(h, g, w704, layer, ig, iu):
    t, d = h.shape
    f4 = w704.shape[3]
    tm = _tile(t, 512)

    def body(h_ref, g_ref, wg_ref, wu_ref, ab_ref, s_ref, n_sc):
        @pl.when(pl.program_id(1) == 0)
        def _():
            n_sc[...] = _rms(h_ref[...], g_ref[...]).astype(BF)

        n = n_sc[...]
        a = _dot(n, wg_ref[...])
        b = _dot(n, wu_ref[...])
        ab_ref[0] = a
        ab_ref[1] = b
        s_ref[...] = (a * _sigmoid(a) * b).astype(BF)

    wspec = lambda ib: pl.BlockSpec((None, None, d, f4), lambda i, j: (j, layer, ib, 0))
    return pl.pallas_call(
        body,
        name="ffn_up",
        grid=(t // tm, NCHIP),
        in_specs=[pl.BlockSpec((tm, d), lambda i, j: (i, 0)), pl.BlockSpec((1, d), lambda i, j: (0, 0)), wspec(ig), wspec(iu)],
        out_specs=[pl.BlockSpec((2, None, tm, f4), lambda i, j: (0, j, i, 0)), pl.BlockSpec((None, tm, f4), lambda i, j: (j, i, 0))],
        out_shape=[jax.ShapeDtypeStruct((2, NCHIP, t, f4), F32), jax.ShapeDtypeStruct((NCHIP, t, f4), BF)],
        scratch_shapes=[pltpu.VMEM((tm, d), BF)],
        compiler_params=_params(("parallel", "arbitrary")),
    )(h, g, w704, w704)


def ffn_down(s4, w704r, layer, idx, h):
    t, d = h.shape
    f4 = s4.shape[2]
    tm, tn = _tile(t, 512), _tile(d, 512)

    def body(s_ref, w_ref, h_ref, o_ref):
        acc = _dot(s_ref[0], w_ref[0])
        for k in range(1, NCHIP):
            acc = acc + _dot(s_ref[k], w_ref[k])
        o_ref[...] = h_ref[...] + 0.5 * acc

    return pl.pallas_call(
        body,
        name="ffn_down",
        grid=(t // tm, d // tn),
        in_specs=[pl.BlockSpec((NCHIP, tm, f4), lambda i, j: (0, i, 0)), pl.BlockSpec((NCHIP, None, f4, tn), lambda i, j: (0, layer, idx, j)),
                  pl.BlockSpec((tm, tn), lambda i, j: (i, j))],
        out_specs=pl.BlockSpec((tm, tn), lambda i, j: (i, j)),
        out_shape=jax.ShapeDtypeStruct((t, d), F32),
        compiler_params=_params(("parallel", "parallel")),
    )(s4, w704r, h)


def ffn_bwd(dho, h, g, ab, w704, w704r, layer, ig, iu, idx):
    t, d = h.shape
    f4 = w704.shape[3]
    tm = _tile(t, 512)

    def body(dho_ref, h_ref, g_ref, ab_ref, wg_ref, wu_ref, wd_ref, dh_ref, dg_ref, n_ref, do_ref, s_ref, dab_ref, acc):
        i, j = pl.program_id(0), pl.program_id(1)

        @pl.when(j == 0)
        def _():
            n_ref[...] = _rms(h_ref[...], g_ref[...]).astype(BF)
            do_ref[...] = (0.5 * dho_ref[...]).astype(BF)

        ds = _dot(do_ref[...], wd_ref[...], 1, 1)
        av, bv = ab_ref[0], ab_ref[1]
        sig = _sigmoid(av)
        sl = av * sig
        s_ref[...] = (sl * bv).astype(BF)
        da = (ds * bv * (sig * (1.0 + av * (1.0 - sig)))).astype(BF)
        db = (ds * sl).astype(BF)
        dab_ref[0] = da
        dab_ref[1] = db
        _accumulate(acc, _dot(da, wg_ref[...], 1, 1) + _dot(db, wu_ref[...], 1, 1), j == 0)

        @pl.when(j == NCHIP - 1)
        def _():
            dx, dg = _rms_bwd(h_ref[...], g_ref[...], acc[...])
            dh_ref[...] = dho_ref[...] + dx
            _accumulate(dg_ref, dg, i == 0)

    row = lambda c: pl.BlockSpec((tm, c), lambda i, j: (i, 0))
    wspec = lambda ib: pl.BlockSpec((None, None, d, f4), lambda i, j: (j, layer, ib, 0))
    ab_spec = pl.BlockSpec((2, None, tm, f4), lambda i, j: (0, j, i, 0))
    return pl.pallas_call(
        body,
        name="ffn_bwd",
        grid=(t // tm, NCHIP),
        in_specs=[row(d), row(d), pl.BlockSpec((1, d), lambda i, j: (0, 0)), ab_spec, wspec(ig), wspec(iu),
                  pl.BlockSpec((None, None, f4, d), lambda i, j: (j, layer, idx, 0))],
        out_specs=[row(d), pl.BlockSpec((1, d), lambda i, j: (0, 0)), row(d), row(d), pl.BlockSpec((None, tm, f4), lambda i, j: (j, i, 0)), ab_spec],
        out_shape=[
            jax.ShapeDtypeStruct((t, d), F32), jax.ShapeDtypeStruct((1, d), F32), jax.ShapeDtypeStruct((t, d), BF), jax.ShapeDtypeStruct((t, d), BF),
            jax.ShapeDtypeStruct((NCHIP, t, f4), BF), jax.ShapeDtypeStruct((2, NCHIP, t, f4), BF),
        ],
        scratch_shapes=[pltpu.VMEM((tm, d), F32)],
        compiler_params=_params(("arbitrary", "arbitrary")),
    )(dho, h, g, ab, w704, w704, w704r)


def ffn_wgrads(n, do, s4, dab):
    t, d = n.shape
    f4 = s4.shape[2]
    tm = _tile(d, 512)
    g_in = mmx("g_ffn_in", n, dab, grid=(2, NCHIP, d // tm), a_spec=pl.BlockSpec((t, tm), lambda wh, k, i: (0, i)),
               b_spec=pl.BlockSpec((None, None, t, f4), lambda wh, k, i: (wh, k, 0, 0)), o_spec=pl.BlockSpec((None, None, tm, f4), lambda wh, k, i: (k, wh, i, 0)),
               o_shape=(NCHIP, 2, d, f4), o_dtype=BF, ca=0, cb=0)
    g_out = mmx("g_ffn_out", s4, do, grid=(NCHIP, d // tm), a_spec=pl.BlockSpec((None, t, f4), lambda k, j: (k, 0, 0)),
                b_spec=pl.BlockSpec((t, tm), lambda k, j: (0, j)), o_spec=pl.BlockSpec((None, f4, tm), lambda k, j: (k, 0, j)),
                o_shape=(NCHIP, f4, d), o_dtype=BF, ca=0, cb=0)
    return g_in.reshape(NCHIP, 2 * d, f4), g_out


def norm_cast(h, g):
    def fn(i, nt, rows, vecs, prevs, nexts):
        return [_rms(rows[0], vecs[0])], []
    return ew(fn, [h], [g], [(h.shape[1], BF)], tm=512, name="norm_cast")[0]


def _zero_if(cond, x):
    return jnp.where(cond, jnp.zeros_like(x), x)


def conv_a_fwd(sc3, w_sc):
    d = sc3.shape[1] // 3

    def fn(i, nt, rows, vecs, prevs, nexts):
        x, pv = rows[0], _zero_if(i == 0, prevs[0])
        v = x[:, d:2 * d] * x[:, 2 * d:]
        vp = pv[:, d:2 * d] * pv[:, 2 * d:]
        return [x[:, :d] * _conv_fwd(v, vp, vecs[0])], []

    return ew(fn, [sc3], [w_sc], [(d, BF)], tm=256, name="conv_a_fwd", prev_halo=(0,))[0]


def _softplus(x):
    e = jnp.exp(-jnp.abs(x))
    return jnp.maximum(x, 0.0) + jnp.where(e < 1e-4, e - 0.5 * e * e, jnp.log(1.0 + e))


def conv_m_fwd(xbc_raw, dt_raw, w_mc, b_mc, dt_bias):
    def fn(i, nt, rows, vecs, prevs, nexts):
        pre = _conv_fwd(rows[0], _zero_if(i == 0, prevs[0]), vecs[0]) + vecs[1]
        return [pre * _sigmoid(pre), _softplus(rows[1] + vecs[2])], []

    return ew(fn, [xbc_raw, dt_raw], [w_mc, b_mc, dt_bias], [(xbc_raw.shape[1], F32), (LANE, F32)], tm=128, name="conv_m_fwd",
              prev_halo=(0,))


def conv_m_bwd1(dxbc, xbc_raw, ddt, dt_raw, w_mc, b_mc, dt_bias):
    def fn(i, nt, rows, vecs, prevs, nexts):
        pre = _conv_fwd(rows[1], _zero_if(i == 0, prevs[0]), vecs[0]) + vecs[1]
        sig = _sigmoid(pre)
        dpre = rows[0] * (sig * (1.0 + pre * (1.0 - sig)))
        ddr = rows[2] * _sigmoid(rows[3] + vecs[2])
        return [dpre, ddr], [jnp.sum(ddr, axis=0, keepdims=True)]

    return ew(fn, [dxbc, xbc_raw, ddt, dt_raw], [w_mc, b_mc, dt_bias], [(dxbc.shape[1], F32), (LANE, BF)], [(1, LANE)], tm=128,
              name="conv_m_bwd1", prev_halo=(1,))


def conv_bwd2(dpre, x, w, name):
    kk = w.shape[0]

    def fn(i, nt, rows, vecs, prevs, nexts):
        dp, xv = rows[0], rows[1]
        nx = _zero_if(i == nt - 1, nexts[0])
        pv = _zero_if(i == 0, prevs[0])
        dx = None
        dws = []
        for k in range(kk):
            term = vecs[0][k:k + 1, :] * _shift_up(dp, nx, kk - 1 - k)
            dx = term if dx is None else dx + term
            dws.append(jnp.sum(dp * _shift_down(xv, pv, kk - 1 - k), axis=0, keepdims=True))
        return [dx], [jnp.concatenate(dws, axis=0), jnp.sum(dp, axis=0, keepdims=True)]

    c = x.shape[1]
    return ew(fn, [dpre, x], [w], [(c, BF)], [(kk, c), (1, c)], tm=128, name=name, prev_halo=(1,), next_halo=(0,))


def conv_a_bwd1(dya, sc3, w_sc):
    d = sc3.shape[1] // 3

    def fn(i, nt, rows, vecs, prevs, nexts):
        x, pv = rows[1], _zero_if(i == 0, prevs[0])
        v = x[:, d:2 * d] * x[:, 2 * d:]
        vp = pv[:, d:2 * d] * pv[:, 2 * d:]
        return [rows[0] * x[:, :d], rows[0] * _conv_fwd(v, vp, vecs[0]), v], []

    return ew(fn, [dya, sc3], [w_sc], [(d, F32), (d, BF), (d, F32)], tm=256, name="conv_a_bwd1", prev_halo=(1,))


def conv_a_bwd2(dcv, v, sc3, w_sc):
    d = v.shape[1]
    kk = w_sc.shape[0]

    def fn(i, nt, rows, vecs, prevs, nexts):
        dp, vv, x = rows
        nx = _zero_if(i == nt - 1, nexts[0])
        pv = _zero_if(i == 0, prevs[0])
        dv = None
        dws = []
        for k in range(kk):
            term = vecs[0][k:k + 1, :] * _shift_up(dp, nx, kk - 1 - k)
            dv = term if dv is None else dv + term
            dws.append(jnp.sum(dp * _shift_down(vv, pv, kk - 1 - k), axis=0, keepdims=True))
        return [dv * x[:, 2 * d:], dv * x[:, d:2 * d]], [jnp.concatenate(dws, axis=0)]

    return ew(fn, [dcv, v, sc3], [w_sc], [(d, BF), (d, BF)], [(kk, d)], tm=256, name="conv_a_bwd2", prev_halo=(1,), next_halo=(0,))


def _ssd_common(xbc_ref, dt_ref, alog_ref, e_ref, w):
    ll = SSM_L
    xs = xbc_ref[:, 0:w]
    dtv = dt_ref[...]
    a_row = -jnp.exp(alog_ref[...])
    a = dtv * a_row
    row = lax.broadcasted_iota(jnp.int32, (ll, ll), 0)
    col = lax.broadcasted_iota(jnp.int32, (ll, ll), 1)
    tril = (row >= col).astype(F32)
    triu = (row <= col).astype(F32)
    acl = _dot(tril, a, precision=HI)
    acl_t = _dot(a, triu, 0, 0, precision=HI)
    e = e_ref[...]
    aclx = _dot(acl, e, precision=HI)
    dtx = _dot(dtv, e, precision=HI)
    last = aclx[ll - 1:ll, :]
    e_in = jnp.exp(aclx)
    e_end = jnp.exp(last - aclx)
    e_tot = jnp.exp(last)
    x = xs * dtx
    return dict(xs=xs, dtv=dtv, a_row=a_row, a=a, row=row, col=col, triu=triu, acl=acl, acl_t=acl_t, dtx=dtx, e_in=e_in, e_end=e_end,
                e_tot=e_tot, x=x)


def _decay(q, hh):
    diff = q["acl"][:, hh:hh + 1] - q["acl_t"][hh:hh + 1, :]
    return jnp.exp(jnp.where(q["row"] >= q["col"], diff, -jnp.inf))


def ssd_fwd(xbc, dt, z, a_log, d_exp, m_norm, e_mat):
    t = xbc.shape[0]
    w = z.shape[1]
    gn = SSM_G * SSM_N
    gw = w // SSM_G
    ll, nn = SSM_L, SSM_N
    nc = t // ll
    cw = xbc.shape[1]

    def body(xbc_ref, dt_ref, z_ref, alog_ref, dexp_ref, mn_ref, e_ref, yn_ref, y_ref, sp_ref, s_sc):
        c = pl.program_id(0)

        @pl.when(c == 0)
        def _():
            s_sc[...] = jnp.zeros_like(s_sc)

        q = _ssd_common(xbc_ref, dt_ref, alog_ref, e_ref, w)
        xb = q["x"].astype(BF)
        xsb = (q["x"] * q["e_end"]).astype(BF)
        sp = s_sc[...]
        sp_ref[0] = sp
        spb = sp.astype(BF)
        lane = lax.broadcasted_iota(jnp.int32, (ll, LANE), 1)
        for g in range(SSM_G):
            lo = g * gw
            bg = xbc_ref[:, w + g * nn:w + (g + 1) * nn].astype(BF)
            cg = xbc_ref[:, w + gn + g * nn:w + gn + (g + 1) * nn].astype(BF)
            yoff = _dot(cg, spb[:, lo:lo + gw]) * q["e_in"][:, lo:lo + gw]
            s_sc[:, lo:lo + gw] = sp[:, lo:lo + gw] * q["e_tot"][:, lo:lo + gw] + _dot(bg, xsb[:, lo:lo + gw], 0, 0)
            cb = _dot(cg, bg, 1, 1)
            for pr in range(gw // LANE):
                l0 = lo + pr * LANE
                xp = xb[:, l0:l0 + LANE]
                ys = []
                for hh in (l0 // SSM_P, l0 // SSM_P + 1):
                    wm = (cb * _decay(q, hh)).astype(BF)
                    ys.append(_dot(wm, xp))
                ydiag = jnp.where(lane < SSM_P, ys[0], ys[1])
                y_ref[:, l0:l0 + LANE] = ydiag + yoff[:, pr * LANE:(pr + 1) * LANE] + dexp_ref[:, l0:l0 + LANE] * q["xs"][:, l0:l0 + LANE]
        zv = z_ref[...].astype(F32)
        yz = y_ref[...] * (zv * _sigmoid(zv))
        for g in range(SSM_G):
            lo = g * gw
            yn_ref[:, lo:lo + gw] = _rms(yz[:, lo:lo + gw], mn_ref[:, lo:lo + gw]).astype(BF)

    vec = lambda s: pl.BlockSpec(s, lambda c: (0, 0))
    return pl.pallas_call(
        body,
        name="ssd_fwd",
        grid=(nc,),
        in_specs=[
            pl.BlockSpec((ll, cw), lambda c: (c, 0)), pl.BlockSpec((ll, LANE), lambda c: (c, 0)), pl.BlockSpec((ll, w), lambda c: (c, 0)),
            vec((1, LANE)), vec((1, w)), vec((1, w)), vec((LANE, w)),
        ],
        out_specs=[pl.BlockSpec((ll, w), lambda c: (c, 0)), pl.BlockSpec((ll, w), lambda c: (c, 0)), pl.BlockSpec((1, nn, w), lambda c: (c, 0, 0))],
        out_shape=[jax.ShapeDtypeStruct((t, w), BF), jax.ShapeDtypeStruct((t, w), F32), jax.ShapeDtypeStruct((nc, nn, w), F32)],
        scratch_shapes=[pltpu.VMEM((nn, w), F32)],
        compiler_params=_params(("arbitrary",)),
    )(xbc, dt, z, a_log, d_exp, m_norm, e_mat)


def ssd_bwd(dyn, y, z, xbc, dt, sprev, a_log, d_exp, m_norm, e_mat, et_mat):
    t = xbc.shape[0]
    w = z.shape[1]
    gn = SSM_G * SSM_N
    gw = w // SSM_G
    ll, nn = SSM_L, SSM_N
    nc = t // ll
    cw = xbc.shape[1]

    def body(dyn_ref, y_ref, z_ref, xbc_ref, dt_ref, sp_ref, alog_ref, dexp_ref, mn_ref, e_ref, et_ref,
             dz_ref, dxbc_ref, ddt_ref, dmn_ref, dd_ref, dal_ref, ds_sc, dy_sc, dx_sc):
        step = pl.program_id(0)

        @pl.when(step == 0)
        def _():
            ds_sc[...] = jnp.zeros_like(ds_sc)

        zv, yv = z_ref[...].astype(F32), y_ref[...]
        sg = _sigmoid(zv)
        sz = zv * sg
        yz = yv * sz
        dmn = []
        for g in range(SSM_G):
            lo = g * gw
            dseg, dmn_g = _rms_bwd(yz[:, lo:lo + gw], mn_ref[:, lo:lo + gw], dyn_ref[:, lo:lo + gw])
            dy_sc[:, lo:lo + gw] = dseg
            dmn.append(dmn_g)
        dmn = jnp.concatenate(dmn, axis=1)
        dyz = dy_sc[...]
        dz_ref[...] = (dyz * yv * (sg * (1.0 + zv * (1.0 - sg)))).astype(BF)
        dy = dyz * sz

        q = _ssd_common(xbc_ref, dt_ref, alog_ref, e_ref, w)
        x = q["x"]
        xb = x.astype(BF)
        xsb = (x * q["e_end"]).astype(BF)
        sp = sp_ref[0]
        spb = sp.astype(BF)
        dsn = ds_sc[...]
        dsnb = dsn.astype(BF)
        dyb = dy.astype(BF)
        lane = lax.broadcasted_iota(jnp.int32, (ll, LANE), 1)
        lane1 = lax.broadcasted_iota(jnp.int32, (1, LANE), 1)
        sub1 = lax.broadcasted_iota(jnp.int32, (LANE, 1), 0)
        dacl = jnp.zeros((ll, LANE), F32)
        dacl_t = jnp.zeros((LANE, ll), F32)
        d_ein, d_eend, d_etot = [], [], []
        for g in range(SSM_G):
            lo = g * gw
            sl = slice(lo, lo + gw)
            bg = xbc_ref[:, w + g * nn:w + (g + 1) * nn].astype(BF)
            cg = xbc_ref[:, w + gn + g * nn:w + gn + (g + 1) * nn].astype(BF)
            zg = _dot(cg, spb[:, sl])
            dzz = (dy[:, sl] * q["e_in"][:, sl]).astype(BF)
            d_ein.append(dy[:, sl] * zg)
            dcg = _dot(dzz, spb[:, sl], 1, 1)
            ds_sc[:, sl] = _dot(cg, dzz, 0, 0) + dsn[:, sl] * q["e_tot"][:, sl]
            d_etot.append(jnp.sum(dsn[:, sl] * sp[:, sl], axis=0, keepdims=True))
            dbg = _dot(xsb[:, sl], dsnb[:, sl], 1, 1)
            dxs_g = _dot(bg, dsnb[:, sl])
            d_eend.append(dxs_g * x[:, sl])
            cb = _dot(cg, bg, 1, 1)
            dcb = jnp.zeros((ll, ll), F32)
            for pr in range(gw // LANE):
                l0 = lo + pr * LANE
                xp = xb[:, l0:l0 + LANE]
                dyp = dyb[:, l0:l0 + LANE]
                dxp = []
                for hi, hh in enumerate((l0 // SSM_P, l0 // SSM_P + 1)):
                    lm = _decay(q, hh)
                    wm = (cb * lm).astype(BF)
                    in_head = (lane < SSM_P) if hi == 0 else (lane >= SSM_P)
                    dwm = _dot(jnp.where(in_head, dyp, jnp.zeros_like(dyp)), xp, 1, 1)
                    dxp.append(_dot(wm, dyp, 0, 0))
                    dlm = dwm * lm
                    dcb = dcb + dlm
                    dd = dlm * cb
                    dacl = dacl + jnp.sum(dd, axis=1, keepdims=True) * (lane1 == hh).astype(F32)
                    dacl_t = dacl_t + (sub1 == hh).astype(F32) * jnp.sum(dd, axis=0, keepdims=True)
                dx_sc[:, l0:l0 + LANE] = jnp.where(lane < SSM_P, dxp[0], dxp[1]) + dxs_g[:, pr * LANE:(pr + 1) * LANE] * q["e_end"][:, l0:l0 + LANE]
            dcbb = dcb.astype(BF)
            dxbc_ref[:, w + g * nn:w + (g + 1) * nn] = dbg + _dot(dcbb, cg, 0, 0)
            dxbc_ref[:, w + gn + g * nn:w + gn + (g + 1) * nn] = dcg + _dot(dcbb, bg)
        d_ein = jnp.concatenate(d_ein, axis=1) * q["e_in"]
        d_eend = jnp.concatenate(d_eend, axis=1) * q["e_end"]
        d_etot = jnp.concatenate(d_etot, axis=1) * q["e_tot"]
        et = et_ref[...]
        last_add = jnp.sum(d_eend, axis=0, keepdims=True) + d_etot
        last_add = _dot(jnp.broadcast_to(last_add, (HALO, w)), et, precision=HI)[0:1]
        row1 = lax.broadcasted_iota(jnp.int32, (ll, LANE), 0)
        dacl = dacl + _dot(d_ein - d_eend, et, precision=HI) + jnp.where(row1 == ll - 1, last_add, 0.0)
        da = _dot(q["triu"], dacl, precision=HI) - _dot(q["triu"], dacl_t, 1, 1, precision=HI)
        dxv = dx_sc[...]
        dxbc_ref[:, 0:w] = dexp_ref[...] * dy + dxv * q["dtx"]
        ddt_ref[...] = _dot(dxv * q["xs"], et, precision=HI) + da * q["a_row"]
        dal = jnp.sum(da * q["dtv"], axis=0, keepdims=True) * q["a_row"]
        ddv = jnp.sum(dy * q["xs"], axis=0, keepdims=True)
        ddv = _dot(jnp.broadcast_to(ddv, (HALO, w)), et, precision=HI)[0:1]
        _accumulate(dmn_ref, dmn, step == 0)
        _accumulate(dd_ref, ddv, step == 0)
        _accumulate(dal_ref, dal, step == 0)

    rev = lambda c_: pl.BlockSpec((ll, c_), lambda s: (nc - 1 - s, 0))
    vec = lambda s_: pl.BlockSpec(s_, lambda s: (0, 0))
    return pl.pallas_call(
        body,
        name="ssd_bwd",
        grid=(nc,),
        in_specs=[
            rev(w), rev(w), rev(w), rev(cw), rev(LANE), pl.BlockSpec((1, nn, w), lambda s: (nc - 1 - s, 0, 0)),
            vec((1, LANE)), vec((1, w)), vec((1, w)), vec((LANE, w)), vec((w, LANE)),
        ],
        out_specs=[rev(w), rev(cw), rev(LANE), vec((1, w)), vec((1, LANE)), vec((1, LANE))],
        out_shape=[
            jax.ShapeDtypeStruct((t, w), BF), jax.ShapeDtypeStruct((t, cw), F32), jax.ShapeDtypeStruct((t, LANE), F32),
            jax.ShapeDtypeStruct((1, w), F32), jax.ShapeDtypeStruct((1, LANE), F32), jax.ShapeDtypeStruct((1, LANE), F32),
        ],
        scratch_shapes=[pltpu.VMEM((nn, w), F32), pltpu.VMEM((ll, w), F32), pltpu.VMEM((ll, w), F32)],
        compiler_params=_params(("arbitrary",)),
    )(dyn, y, z, xbc, dt, sprev, a_log, d_exp, m_norm, e_mat, et_mat)


def _w1024_spec(d, layer, nblk, iblk):
    r = nblk * (d // NCHIP)
    return pl.BlockSpec((NCHIP, None, r, d), lambda i: (0, layer, iblk // nblk, 0))


def _whole(ref):
    v = ref[...]
    return v.reshape(v.shape[0] * v.shape[1], v.shape[2])


def mix_out_fwd(ya_in, yn, gates, h, w1024, layer):
    t, d = h.shape
    tm = _tile(t, 256)

    def body(ya_ref, yn_ref, g_ref, h_ref, wm_ref, wa_ref, wo_ref, ho_ref, oa_ref, om_ref, mg_ref):
        y_a = _dot(ya_ref[...], _whole(wa_ref))
        y_m = _dot(yn_ref[...], _whole(wm_ref))
        oa_ref[...] = y_a
        om_ref[...] = y_m
        gv = g_ref[...].astype(F32)
        mg = (_sigmoid(gv[:, :d]) * y_a + _sigmoid(gv[:, d:]) * y_m).astype(BF)
        mg_ref[...] = mg
        ho_ref[...] = h_ref[...] + _dot(mg, _whole(wo_ref))

    row = lambda c: pl.BlockSpec((tm, c), lambda i: (i, 0))
    return pl.pallas_call(
        body,
        name="mix_out_fwd",
        grid=(t // tm,),
        in_specs=[row(d), row(2 * d), row(2 * d), row(d), _w1024_spec(d, layer, 2, 0), _w1024_spec(d, layer, 1, 2), _w1024_spec(d, layer, 1, 3)],
        out_specs=[row(d), row(d), row(d), row(d)],
        out_shape=[jax.ShapeDtypeStruct((t, d), F32), jax.ShapeDtypeStruct((t, d), F32), jax.ShapeDtypeStruct((t, d), F32),
                   jax.ShapeDtypeStruct((t, d), BF)],
        compiler_params=_params(("parallel",)),
    )(ya_in, yn, gates, h, w1024, w1024, w1024)


def mix_out_bwd(dh, gates, y_a, y_m, w1024, layer):
    t, d = dh.shape
    tm = _tile(t, 256)

    def body(dh_ref, g_ref, ya_ref, ym_ref, wm_ref, wa_ref, wo_ref, dg_ref, dya_ref, dyn_ref, da_ref, dm_ref):
        dmg = _dot(dh_ref[...].astype(BF), _whole(wo_ref), 1, 1)
        gv = g_ref[...].astype(F32)
        sa, sm = _sigmoid(gv[:, :d]), _sigmoid(gv[:, d:])
        dg_ref[:, :d] = (dmg * ya_ref[...] * sa * (1.0 - sa)).astype(BF)
        dg_ref[:, d:] = (dmg * ym_ref[...] * sm * (1.0 - sm)).astype(BF)
        da = (dmg * sa).astype(BF)
        dm = (dmg * sm).astype(BF)
        da_ref[...] = da
        dm_ref[...] = dm
        dya_ref[...] = _dot(da, _whole(wa_ref), 1, 1)
        dyn_ref[...] = _dot(dm, _whole(wm_ref), 1, 1)

    row = lambda c: pl.BlockSpec((tm, c), lambda i: (i, 0))
    return pl.pallas_call(
        body,
        name="mix_out_bwd",
        grid=(t // tm,),
        in_specs=[row(d), row(2 * d), row(d), row(d), _w1024_spec(d, layer, 2, 0), _w1024_spec(d, layer, 1, 2), _w1024_spec(d, layer, 1, 3)],
        out_specs=[row(2 * d), row(d), row(2 * d), row(d), row(d)],
        out_shape=[jax.ShapeDtypeStruct((t, 2 * d), BF), jax.ShapeDtypeStruct((t, d), F32), jax.ShapeDtypeStruct((t, 2 * d), F32),
                   jax.ShapeDtypeStruct((t, d), BF), jax.ShapeDtypeStruct((t, d), BF)],
        compiler_params=_params(("parallel",)),
    )(dh, gates, y_a, y_m, w1024, w1024, w1024)


def norm_bwd_add(dh, h, g, dn):
    def fn(i, nt, rows, vecs, prevs, nexts):
        dx, dg = _rms_bwd(rows[1], vecs[0], rows[2])
        return [rows[0] + dx], [dg]
    d = h.shape[1]
    return ew(fn, [dh, h, dn], [g], [(d, F32)], [(1, d)], tm=512, name="norm_bwd_add")


def _pe(p, wpp_ref):
    pb = p.astype(BF)
    return jnp.concatenate([_dot(pb, wpp_ref[k]) for k in range(NCHIP)], axis=1)


def ple_fwd(h, g, p, w1024, wpp, layer):
    t, d = h.shape
    tm = _tile(t, 512)

    def body(h_ref, g_ref, p_ref, wg_ref, wp_ref, ho_ref):
        hv = h_ref[...]
        gate = _sigmoid(_dot(_rms(hv, g_ref[...]).astype(BF), _whole(wg_ref)))
        ho_ref[...] = hv + gate * _pe(p_ref[...], wp_ref)

    row = lambda c: pl.BlockSpec((tm, c), lambda i: (i, 0))
    wpp_spec = pl.BlockSpec((NCHIP, None) + wpp.shape[2:], lambda i: (0, layer, 0, 0))
    return pl.pallas_call(
        body,
        name="ple_fwd",
        grid=(t // tm,),
        in_specs=[row(d), pl.BlockSpec((1, d), lambda i: (0, 0)), row(p.shape[1]), _w1024_spec(d, layer, 1, 4), wpp_spec],
        out_specs=row(d),
        out_shape=jax.ShapeDtypeStruct((t, d), F32),
        compiler_params=_params(("parallel",)),
    )(h, g, p, w1024, wpp)


def ple_bwd(dho, h, g, p, w1024, wpp, layer):
    t, d = h.shape
    tm = _tile(t, 512)

    def body(dho_ref, h_ref, g_ref, p_ref, wg_ref, wp_ref, dh_ref, dg_ref, n_ref, dgp_ref, dpe_ref):
        hv, dv = h_ref[...], dho_ref[...]
        n = _rms(hv, g_ref[...]).astype(BF)
        n_ref[...] = n
        wg = _whole(wg_ref)
        gate = _sigmoid(_dot(n, wg))
        pe = _pe(p_ref[...], wp_ref)
        dpe_ref[...] = (dv * gate).astype(BF)
        dgp = (dv * pe * gate * (1.0 - gate)).astype(BF)
        dgp_ref[...] = dgp
        dx, dg = _rms_bwd(hv, g_ref[...], _dot(dgp, wg, 1, 1))
        dh_ref[...] = dv + dx
        _accumulate(dg_ref, dg, pl.program_id(0) == 0)

    row = lambda c: pl.BlockSpec((tm, c), lambda i: (i, 0))
    wpp_spec = pl.BlockSpec((NCHIP, None) + wpp.shape[2:], lambda i: (0, layer, 0, 0))
    return pl.pallas_call(
        body,
        name="ple_bwd",
        grid=(t // tm,),
        in_specs=[row(d), row(d), pl.BlockSpec((1, d), lambda i: (0, 0)), row(p.shape[1]), _w1024_spec(d, layer, 1, 4), wpp_spec],
        out_specs=[row(d), pl.BlockSpec((1, d), lambda i: (0, 0)), row(d), row(d), row(d)],
        out_shape=[jax.ShapeDtypeStruct((t, d), F32), jax.ShapeDtypeStruct((1, d), F32), jax.ShapeDtypeStruct((t, d), BF),
                   jax.ShapeDtypeStruct((t, d), BF), jax.ShapeDtypeStruct((t, d), BF)],
        compiler_params=_params(("arbitrary",)),
    )(dho, h, g, p, w1024, wpp)


def loss_bwd(h, g, target):
    d = h.shape[1]

    def fn(i, nt, rows, vecs, prevs, nexts):
        err = _rms(rows[0], vecs[0]) - rows[1]
        dx, dg = _rms_bwd(rows[0], vecs[0], err * (1.0 / d))
        return [dx], [jnp.sum(err * err, axis=0, keepdims=True) * (0.5 / d), dg]

    return ew(fn, [h, target], [g], [(d, F32)], [(1, d), (1, d)], tm=512, name="loss_bwd")


def adamw(w, g, m, v, name):
    c1, c2 = 1.0 / (1.0 - ADAM_B1 ** ADAM_STEP), 1.0 / (1.0 - ADAM_B2 ** ADAM_STEP)

    def fn(i, nt, rows, vecs, prevs, nexts):
        wv, gv, mv, vv = rows
        mn = ADAM_B1 * mv + (1.0 - ADAM_B1) * gv
        vn = ADAM_B2 * vv + (1.0 - ADAM_B2) * (gv * gv)
        delta = -ADAM_LR * ((mn * c1) / (jnp.sqrt(vn * c2) + ADAM_EPS) + ADAM_WD * wv)
        return [delta, mn, vn], []

    c = w.shape[1]
    return ew(fn, [w, g, m, v], [], [(c, F32)] * 3, tm=_row_tile(w.shape[0], c, HALO), name=name)


def _place():
    return lax.axis_index("x"), lax.axis_index("y"), lax.axis_index("c")


def _other_chips(x, y):
    return [(1 - x, y), (x, 1 - y), (1 - x, 1 - y)]


ANY = pl.BlockSpec(memory_space=pl.ANY)


def _comm_call(body, name, ins, out_shapes, n_sems, aliases=None):
    return pl.pallas_call(
        body,
        name=name,
        in_specs=[ANY] * len(ins),
        out_specs=[ANY] * len(out_shapes),
        out_shape=out_shapes,
        scratch_shapes=[pltpu.SemaphoreType.DMA((n_sems,)), pltpu.SemaphoreType.DMA((n_sems,))],
        input_output_aliases=aliases or {},
    )(*ins)


def gather_packs(packs, name):
    nt = len(packs)
    hl = packs[0].shape[0] // 2

    def body(*refs):
        ins, outs, (send_sems, recv_sems) = refs[:nt], refs[nt:2 * nt], refs[2 * nt:]
        x, y, cc = _place()
        chips = _other_chips(x, y)
        sibling = (x, y, 1 - cc)
        mine, theirs = pl.ds(cc * hl, hl), pl.ds((1 - cc) * hl, hl)
        k_me = 2 * x + y

        def copy(k, src, dst, to):
            return pltpu.make_async_remote_copy(src_ref=src, dst_ref=dst, send_sem=send_sems.at[k], recv_sem=recv_sems.at[k],
                                                device_id=to, device_id_type=MESH)

        started = []
        for ti in range(nt):
            for j, chip in enumerate(chips):
                started.append(copy(7 * ti + j, ins[ti].at[mine], outs[ti].at[k_me, mine], (*chip, cc)))
            started.append(copy(7 * ti + 6, ins[ti], outs[ti].at[k_me], sibling))
        for cp in started:
            cp.start()
        for ti in range(nt):
            for j, (px, py) in enumerate(chips):
                landed = outs[ti].at[2 * px + py, mine]
                copy(7 * ti + j, landed, landed, (px, py, cc)).wait_recv()
                fw = copy(7 * ti + 3 + j, landed, landed, sibling)
                fw.start()
                started.append(fw)
        for ti in range(nt):
            for j, (px, py) in enumerate(chips):
                landed = outs[ti].at[2 * px + py, theirs]
                copy(7 * ti + 3 + j, landed, landed, sibling).wait_recv()
            own = outs[ti].at[k_me]
            copy(7 * ti + 6, own, own, sibling).wait_recv()
        for cp in started:
            cp.wait_send()

    return _comm_call(body, name, packs, [jax.ShapeDtypeStruct((NCHIP,) + p.shape, p.dtype) for p in packs], 7 * nt)


def swap_packs(gs, name):
    nt = len(gs)
    hl = gs[0].shape[1] // 2

    def body(*refs):
        ins, outs, (send_sems, recv_sems) = refs[:nt], refs[nt:2 * nt], refs[2 * nt:]
        x, y, cc = _place()
        theirs = pl.ds((1 - cc) * hl, hl)
        cps = [pltpu.make_async_remote_copy(src_ref=ins[ti].at[:, theirs], dst_ref=outs[ti], send_sem=send_sems.at[ti], recv_sem=recv_sems.at[ti],
                                            device_id=(x, y, 1 - cc), device_id_type=MESH) for ti in range(nt)]
        for cp in cps:
            cp.start()
        for cp in cps:
            cp.wait()

    return _comm_call(body, name, gs, [jax.ShapeDtypeStruct((NCHIP, hl) + g.shape[2:], g.dtype) for g in gs], nt)


def scatter_packs(cs, name):
    nt = len(cs)

    def body(*refs):
        ins, outs, (send_sems, recv_sems) = refs[:nt], refs[nt:2 * nt], refs[2 * nt:]
        x, y, cc = _place()
        cps = []
        for ti in range(nt):
            for j, (px, py) in enumerate(_other_chips(x, y)):
                cps.append(pltpu.make_async_remote_copy(src_ref=ins[ti].at[2 * px + py], dst_ref=outs[ti].at[j], send_sem=send_sems.at[3 * ti + j],
                                                        recv_sem=recv_sems.at[3 * ti + j], device_id=(px, py, cc), device_id_type=MESH))
        for cp in cps:
            cp.start()
        for cp in cps:
            cp.wait()

    return _comm_call(body, name, cs, [jax.ShapeDtypeStruct((3,) + c_.shape[1:], c_.dtype) for c_ in cs], 3 * nt)


def join_packs(fulls, name):
    nt = len(fulls)
    hl = fulls[0].shape[0] // 2

    def body(*refs):
        ins, outs, (send_sems, recv_sems) = refs[:nt], refs[nt:2 * nt], refs[2 * nt:]
        x, y, cc = _place()
        mine = pl.ds(cc * hl, hl)
        cps = [pltpu.make_async_remote_copy(src_ref=ins[ti].at[mine], dst_ref=outs[ti].at[mine], send_sem=send_sems.at[ti], recv_sem=recv_sems.at[ti],
                                            device_id=(x, y, 1 - cc), device_id_type=MESH) for ti in range(nt)]
        for cp in cps:
            cp.start()
        for cp in cps:
            cp.wait()

    return _comm_call(body, name, fulls, [jax.ShapeDtypeStruct(f.shape, f.dtype) for f in fulls], nt, aliases={ti: ti for ti in range(nt)})


def add_sibling(g, recv, name):
    _, nl, r, c = g.shape
    hl = nl // 2
    tm = _row_tile(r, c)

    def body(g_ref, r_ref, o_ref):
        o_ref[...] = (g_ref[...].astype(F32) + r_ref[...].astype(F32)).astype(o_ref.dtype)

    blk = (None, None, tm, c)
    return pl.pallas_call(
        body,
        name=name,
        grid=(NCHIP, hl, r // tm),
        in_specs=[pl.BlockSpec(blk, lambda k, l, i: (k, lax.axis_index("c") * hl + l, i, 0)), pl.BlockSpec(blk, lambda k, l, i: (k, l, i, 0))],
        out_specs=pl.BlockSpec(blk, lambda k, l, i: (k, l, i, 0)),
        out_shape=jax.ShapeDtypeStruct(recv.shape, BF),
        compiler_params=_params(("parallel", "parallel", "parallel")),
    )(g, recv)


def add_chips(cs, got, nl, name):
    _, hl, r, c = cs.shape
    tm = _row_tile(r, c)

    def body(own_ref, got_ref, o_ref):
        o_ref[...] = own_ref[...].astype(F32) + got_ref[0].astype(F32) + got_ref[1].astype(F32) + got_ref[2].astype(F32)

    return pl.pallas_call(
        body,
        name=name,
        grid=(hl, r // tm),
        in_specs=[pl.BlockSpec((None, None, tm, c), lambda l, i: (2 * lax.axis_index("x") + lax.axis_index("y"), l, i, 0)),
                  pl.BlockSpec((3, None, tm, c), lambda l, i: (0, l, i, 0))],
        out_specs=pl.BlockSpec((None, tm, c), lambda l, i: (lax.axis_index("c") * hl + l, i, 0)),
        out_shape=jax.ShapeDtypeStruct((nl, r, c), F32),
        compiler_params=_params(("parallel", "parallel")),
    )(cs, got)


def all_gather_xy(shard, name):
    r, c = shard.shape
    hr = r // 2
    assert r % 32 == 0

    def body(x_ref, out_ref, send_sems, recv_sems, local_sem):
        x, y, cc = _place()
        chips = _other_chips(x, y)
        mine = pl.ds(pl.multiple_of(cc * hr, 16), hr)
        theirs = pl.ds(pl.multiple_of((1 - cc) * hr, 16), hr)
        k_me = 2 * x + y

        def copy(k, src, dst, to):
            return pltpu.make_async_remote_copy(src_ref=src, dst_ref=dst, send_sem=send_sems.at[k], recv_sem=recv_sems.at[k],
                                                device_id=to, device_id_type=MESH)

        own = pltpu.make_async_copy(x_ref, out_ref.at[k_me], local_sem)
        own.start()
        first = [copy(j, x_ref.at[mine], out_ref.at[k_me, mine], (*chip, cc)) for j, chip in enumerate(chips)]
        for cp in first:
            cp.start()
        passed = []
        for j, (px, py) in enumerate(chips):
            landed = out_ref.at[2 * px + py, mine]
            copy(j, landed, landed, (px, py, cc)).wait_recv()
            fw = copy(3 + j, landed, landed, (x, y, 1 - cc))
            fw.start()
            passed.append(fw)
        for j, (px, py) in enumerate(chips):
            landed = out_ref.at[2 * px + py, theirs]
            copy(3 + j, landed, landed, (x, y, 1 - cc)).wait_recv()
        for cp in first + passed:
            cp.wait_send()
        own.wait()

    return pl.pallas_call(
        body,
        name=name,
        in_specs=[ANY],
        out_specs=ANY,
        out_shape=jax.ShapeDtypeStruct((NCHIP, r, c), shard.dtype),
        scratch_shapes=[pltpu.SemaphoreType.DMA((6,)), pltpu.SemaphoreType.DMA((6,)), pltpu.SemaphoreType.DMA],
    )(shard)


def all_gather_8(block, name):
    m, c = block.shape

    def body(x_ref, out_ref, send_sems, recv_sems, local_sem):
        x, y, cc = _place()
        me, sibling = (x, y, cc), (x, y, 1 - cc)
        chips = _other_chips(x, y)

        def rows(px, py, pc):
            return out_ref.at[4 * px + 2 * py + pc]

        def copy(k, blk, to, src=None):
            return pltpu.make_async_remote_copy(src_ref=rows(*blk) if src is None else src, dst_ref=rows(*blk), send_sem=send_sems.at[k],
                                                recv_sem=recv_sems.at[k], device_id=to, device_id_type=MESH)

        mine = pltpu.make_async_copy(x_ref, rows(*me), local_sem)
        mine.start()
        first = [copy(0, me, sibling, src=x_ref)]
        first += [copy(1 + j, me, (*chip, cc), src=x_ref) for j, chip in enumerate(chips)]
        for cp in first:
            cp.start()
        passed = [copy(4 + j, (*chip, cc), sibling) for j, chip in enumerate(chips)]
        for j, chip in enumerate(chips):
            copy(1 + j, (*chip, cc), me).wait_recv()
            passed[j].start()
        copy(0, sibling, me).wait_recv()
        for j, chip in enumerate(chips):
            copy(4 + j, (*chip, 1 - cc), me).wait_recv()
        for cp in first + passed:
            cp.wait_send()
        mine.wait()

    return pl.pallas_call(
        body,
        name=name,
        in_specs=[pl.BlockSpec(memory_space=pltpu.VMEM)],
        out_specs=pl.BlockSpec(memory_space=pltpu.VMEM),
        out_shape=jax.ShapeDtypeStruct((8, m, c), block.dtype),
        scratch_shapes=[pltpu.SemaphoreType.DMA((7,)), pltpu.SemaphoreType.DMA((7,)), pltpu.SemaphoreType.DMA],
        compiler_params=pltpu.CompilerParams(vmem_limit_bytes=VMEM_LIMIT),
    )(block)


def add_parts(parts, out_dtype, name, tm=512):
    def fn(i, nt, rows, vecs, prevs, nexts):
        acc = rows[0]
        for r_ in rows[1:]:
            acc = acc + r_
        return [acc], []
    r, c = parts[0].shape
    return ew(fn, list(parts), [], [(c, out_dtype)], tm=_tile(r, tm, 16), name=name)[0]


SMALL_SHARDED = ("sc_conv_w", "m_conv_w")
SMALL_REPL = ("ffn1_norm", "mix_norm", "m_conv_b", "m_dt_bias", "m_A_log", "m_D", "m_norm", "ffn2_norm", "ple_norm", "final_norm")
BIG = ("ffn1_wg", "ffn1_wu", "ffn1_wd", "w_in", "sc_w_out", "m_w_out", "w_o", "ffn2_wg", "ffn2_wu", "ffn2_wd", "ple_w_gate", "ple_w_proj")
ORDER = ("ffn1_norm", "ffn1_wg", "ffn1_wu", "ffn1_wd", "mix_norm", "w_in", "sc_conv_w", "sc_w_out", "m_conv_w", "m_conv_b", "m_dt_bias",
         "m_A_log", "m_D", "m_norm", "m_w_out", "w_o", "ffn2_norm", "ffn2_wg", "ffn2_wu", "ffn2_wd", "ple_norm", "ple_w_gate", "ple_w_proj",
         "final_norm")


def _pack(arrs, cols, row_mult):
    flat = jnp.concatenate([a.reshape(-1) for a in arrs])
    n = flat.shape[0]
    rows = -(-n // cols)
    rows = -(-rows // row_mult) * row_mult
    return jnp.pad(flat, (0, rows * cols - n)).reshape(rows, cols)


def _unpack(flat2d, shapes):
    flat = flat2d.reshape(-1)
    out, off = [], 0
    for s in shapes:
        n = int(np.prod(s))
        out.append(flat[off:off + n].reshape(s))
        off += n
    return out


def _row_cat(arrs, dtype):
    return jnp.concatenate([a.astype(dtype) for a in arrs], axis=1)


def kernel(x, p, ffn1_norm, ffn1_wg, ffn1_wu, ffn1_wd, mix_norm, w_in, sc_conv_w, sc_w_out, m_conv_w, m_conv_b, m_dt_bias, m_A_log, m_D, m_norm, m_w_out, w_o, ffn2_norm, ffn2_wg, ffn2_wu, ffn2_wd, ple_norm, ple_w_gate, ple_w_proj, final_norm, loss_target, m_ffn1_norm, m_ffn1_wg, m_ffn1_wu, m_ffn1_wd, m_mix_norm, m_w_in, m_sc_conv_w, m_sc_w_out, m_m_conv_w, m_m_conv_b, m_m_dt_bias, m_m_A_log, m_m_D, m_m_norm, m_m_w_out, m_w_o, m_ffn2_norm, m_ffn2_wg, m_ffn2_wu, m_ffn2_wd, m_ple_norm, m_ple_w_gate, m_ple_w_proj, m_final_norm, v_ffn1_norm, v_ffn1_wg, v_ffn1_wu, v_ffn1_wd, v_mix_norm, v_w_in, v_sc_conv_w, v_sc_w_out, v_m_conv_w, v_m_conv_b, v_m_dt_bias, v_m_A_log, v_m_D, v_m_norm, v_m_w_out, v_w_o, v_ffn2_norm, v_ffn2_wg, v_ffn2_wu, v_ffn2_wd, v_ple_norm, v_ple_w_gate, v_ple_w_proj, v_final_norm):
    args = dict(locals())
    wts = {n: args[n] for n in ORDER}
    mom = {n: args["m_" + n] for n in ORDER}
    vel = {n: args["v_" + n] for n in ORDER}

    depth = ffn1_norm.shape[0]
    d = x.shape[-1]
    w = 2 * d
    hh = w // SSM_P
    cw = w + 2 * SSM_G * SSM_N
    d4 = d // NCHIP
    my_x, my_y, my_c = _place()
    k_me = 2 * my_x + my_y

    packs = [
        _row_cat([ffn1_wg, ffn1_wu, ffn2_wg, ffn2_wu], BF),
        _row_cat([ffn1_wd, ffn2_wd], BF),
        _row_cat([m_w_out, sc_w_out, w_o, ple_w_gate], BF),
        w_in.astype(BF),
        ple_w_proj.astype(BF),
    ]
    w704, w704r, w1024, win4, wpp = gather_packs(packs, "gather_weights")
    small_local = [sc_conv_w, m_conv_w]
    gathered_s = all_gather_xy(_pack(small_local, LANE, 32), "gather_conv_weights")
    per_shard_s = [_unpack(gathered_s[k], [a.shape for a in small_local]) for k in range(NCHIP)]
    sc_conv_full = jnp.concatenate([per_shard_s[k][0] for k in range(NCHIP)], axis=2)
    m_conv_full = jnp.concatenate([per_shard_s[k][1] for k in range(NCHIP)], axis=2)

    wi = jnp.transpose(win4, (1, 2, 0, 3)).reshape(depth, d, -1)
    o_z, o_xbc, o_dt, o_g = 3 * d, 5 * d, 5 * d + cw, 5 * d + cw + hh
    w_sc3, w_z, w_xbc = wi[:, :, :o_z], wi[:, :, o_z:o_xbc], wi[:, :, o_xbc:o_dt]
    w_g2 = wi[:, :, o_g:o_g + 2 * d]
    w_dt = jnp.pad(wi[:, :, o_dt:o_g], ((0, 0), (0, 0), (0, LANE - hh)))
    w_in_p = jnp.concatenate([w_sc3, w_z, w_xbc, w_g2, w_dt], axis=2)

    pad_h = lambda a: jnp.pad(a, ((0, 0), (0, LANE - hh)))
    dt_bias_p, a_log_p = pad_h(m_dt_bias), pad_h(m_A_log)
    d_exp = jnp.repeat(m_D, SSM_P, axis=1)
    e_mat = (jnp.arange(w)[None, :] // SSM_P == jnp.arange(LANE)[:, None]).astype(F32)
    et_mat = e_mat.T

    h = x[0]
    saved = []
    for i in range(depth):
        s = {}
        s["h0"] = h
        s["ab1"], s4 = ffn_up(h, ffn1_norm[i:i + 1], w704, i, 0, 1)
        h = ffn_down(s4, w704r, i, 0, h)
        s["h1"] = h
        u = norm_cast(h, mix_norm[i:i + 1])
        s["u"] = u
        s["sc3"] = mm(u, w_sc3[i], out_dtype=BF, name="proj_sc")
        s["z"] = mm(u, w_z[i], out_dtype=BF, name="proj_z")
        s["xbc_raw"] = mm(u, w_xbc[i], out_dtype=BF, name="proj_xbc")
        s["gates"] = mm(u, w_g2[i], out_dtype=BF, name="proj_gates")
        s["dt_raw"] = mm(u, w_dt[i], name="proj_dt")
        s["ya_in"] = conv_a_fwd(s["sc3"], sc_conv_full[i])
        s["xbc"], s["dt"] = conv_m_fwd(s["xbc_raw"], s["dt_raw"], m_conv_full[i], m_conv_b[i:i + 1], dt_bias_p[i:i + 1])
        s["yn"], s["y"], s["sprev"] = ssd_fwd(s["xbc"], s["dt"], s["z"], a_log_p[i:i + 1], d_exp[i:i + 1], m_norm[i:i + 1], e_mat)
        h, s["y_a"], s["y_m"], s["merged"] = mix_out_fwd(s["ya_in"], s["yn"], s["gates"], h, w1024, i)
        s["h2"] = h
        s["ab2"], s4 = ffn_up(h, ffn2_norm[i:i + 1], w704, i, 2, 3)
        h = ffn_down(s4, w704r, i, 1, h)
        s["h3"] = h
        h = ple_fwd(h, ple_norm[i:i + 1], p[i, 0], w1024, wpp, i)
        saved.append(s)

    dh, loss_lanes, g_final = loss_bwd(h, final_norm[None, :], loss_target[0])
    loss = lax.psum(jnp.sum(loss_lanes), ("x", "y", "c"))

    g704, g704r, g1024, gin, gpp = ([None] * depth for _ in range(5))
    gs = {n: [None] * depth for n in SMALL_SHARDED + SMALL_REPL if n != "final_norm"}
    for i in reversed(range(depth)):
        s = saved[i]
        dh, gs["ple_norm"][i], n3, dgp, dpe = ple_bwd(dh, s["h3"], ple_norm[i:i + 1], p[i, 0], w1024, wpp, i)
        g_pg = mm(n3, dgp, ta=True, out_dtype=BF, name="g_ple_gate", tm_cap=512, tn_cap=512)
        g_pp = mm(p[i, 0], dpe, ta=True, out_dtype=BF, name="g_ple_proj", tm_cap=512, tn_cap=512)
        gpp[i] = jnp.transpose(g_pp.reshape(g_pp.shape[0], NCHIP, d4), (1, 0, 2))
        dh, gs["ffn2_norm"][i], n2, do2, s2, dab2 = ffn_bwd(dh, s["h2"], ffn2_norm[i:i + 1], s["ab2"], w704, w704r, i, 2, 3, 1)
        g_in2, g_out2 = ffn_wgrads(n2, do2, s2, dab2)
        dgates, dya, dyn, dy_a, dy_m = mix_out_bwd(dh, s["gates"], s["y_a"], s["y_m"], w1024, i)
        g_wo = mm(s["merged"], dh, ta=True, out_dtype=BF, name="g_w_o", tm_cap=512, tn_cap=512)
        g_sco = mm(s["ya_in"], dy_a, ta=True, out_dtype=BF, name="g_sc_out", tm_cap=512, tn_cap=512)
        g_mo = mm(s["yn"], dy_m, ta=True, out_dtype=BF, name="g_m_out", tm_cap=512, tn_cap=512)
        g1024[i] = jnp.concatenate([g_mo.reshape(NCHIP, 2 * d4, d), g_sco.reshape(NCHIP, d4, d), g_wo.reshape(NCHIP, d4, d),
                                    g_pg.reshape(NCHIP, d4, d)], axis=1)
        dz, dxbc, ddt, gs["m_norm"][i], gd, gal = ssd_bwd(dyn, s["y"], s["z"], s["xbc"], s["dt"], s["sprev"], a_log_p[i:i + 1], d_exp[i:i + 1],
                                                          m_norm[i:i + 1], e_mat, et_mat)
        gs["m_D"][i], gs["m_A_log"][i] = gd[:, :hh], gal[:, :hh]
        dpre, ddt_raw, gdb = conv_m_bwd1(dxbc, s["xbc_raw"], ddt, s["dt_raw"], m_conv_full[i], m_conv_b[i:i + 1], dt_bias_p[i:i + 1])
        gs["m_dt_bias"][i] = gdb[:, :hh]
        dxbc_raw, gs["m_conv_w"][i], gs["m_conv_b"][i] = conv_bwd2(dpre, s["xbc_raw"], m_conv_full[i], "conv_m_bwd2")
        dcv, dsc_b, v = conv_a_bwd1(dya, s["sc3"], sc_conv_full[i])
        dsc_c, dsc_x, gs["sc_conv_w"][i] = conv_a_bwd2(dcv, v, s["sc3"], sc_conv_full[i])
        dproj = jnp.concatenate([dsc_b, dsc_c, dsc_x, dz, dxbc_raw, dgates, ddt_raw], axis=1)
        du = mm(dproj, w_in_p[i], tb=True, name="d_proj_in", tn_cap=512)
        gwp = mm(s["u"], dproj, ta=True, out_dtype=BF, name="g_w_in", tm_cap=512, tn_cap=1152)
        gw_cols = jnp.concatenate([gwp[:, :5 * d + cw], gwp[:, 7 * d + cw:7 * d + cw + hh], gwp[:, 5 * d + cw:7 * d + cw]], axis=1)
        gin[i] = jnp.transpose(gw_cols.reshape(d, NCHIP, -1), (1, 0, 2))
        dh, gs["mix_norm"][i] = norm_bwd_add(dh, s["h1"], mix_norm[i:i + 1], du)
        dh, gs["ffn1_norm"][i], n1, do1, s1, dab1 = ffn_bwd(dh, s["h0"], ffn1_norm[i:i + 1], s["ab1"], w704, w704r, i, 0, 1, 0)
        g_in1, g_out1 = ffn_wgrads(n1, do1, s1, dab1)
        g704[i] = jnp.concatenate([g_in1, g_in2], axis=1)
        g704r[i] = jnp.concatenate([g_out1, g_out2], axis=1)
    grad_x = dh[None]

    gpacks = [jnp.stack(g, axis=1) for g in (g704, g704r, g1024, gin, gpp)]
    from_sibling = swap_packs(gpacks, "grad_swap_halves")
    cs = [add_sibling(g, r_, "grad_add_sibling") for g, r_ in zip(gpacks, from_sibling, strict=True)]
    got = scatter_packs(cs, "grad_scatter")
    halves = [add_chips(c_, g_, depth, "grad_add_chips") for c_, g_ in zip(cs, got, strict=True)]
    r704, r704r, r1024, rin, rpp = join_packs(halves, "grad_join_halves")
    f4 = r704.shape[2]
    grads = {
        "ffn1_wg": r704[:, :d], "ffn1_wu": r704[:, d:2 * d], "ffn2_wg": r704[:, 2 * d:3 * d], "ffn2_wu": r704[:, 3 * d:],
        "ffn1_wd": r704r[:, :f4], "ffn2_wd": r704r[:, f4:],
        "m_w_out": r1024[:, :2 * d4], "sc_w_out": r1024[:, 2 * d4:3 * d4], "w_o": r1024[:, 3 * d4:4 * d4], "ple_w_gate": r1024[:, 4 * d4:],
        "w_in": rin, "ple_w_proj": rpp,
    }

    small_names = list(SMALL_SHARDED + SMALL_REPL)
    small_full = [g_final[0] if n == "final_norm" else jnp.stack(gs[n]) for n in small_names]
    small_pack = _pack(small_full, LANE, HALO)
    all8 = all_gather_8(small_pack, "gather_small_grads")
    small_sum = add_parts([all8[k] for k in range(8)], F32, "add_small_grads", tm=256)
    for n, tot in zip(small_names, _unpack(small_sum, [a.shape for a in small_full]), strict=True):
        if n in SMALL_SHARDED:
            cl = wts[n].shape[2]
            grads[n] = lax.dynamic_slice_in_dim(tot, k_me * cl, cl, axis=2)
        else:
            grads[n] = tot.reshape(wts[n].shape)

    delta, new_m, new_v = {}, {}, {}
    for n in BIG:
        shp = wts[n].shape
        two = lambda a: a.reshape(-1, shp[-1])
        dl, nm, nv = adamw(two(wts[n]), two(grads[n]), two(mom[n]), two(vel[n]), "adamw_" + "x".join(map(str, shp[1:])))
        delta[n], new_m[n], new_v[n] = dl.reshape(shp), nm.reshape(shp), nv.reshape(shp)
    sm_shapes = [wts[n].shape for n in small_names]
    pk = lambda dct: _pack([dct[n] for n in small_names], LANE, HALO)
    dl, nm, nv = adamw(pk(wts), pk(grads), pk(mom), pk(vel), "adamw_small")
    for n, a, b_, c_ in zip(small_names, _unpack(dl, sm_shapes), _unpack(nm, sm_shapes), _unpack(nv, sm_shapes), strict=True):
        delta[n], new_m[n], new_v[n] = a, b_, c_

    return (loss, grad_x, *[grads[n] for n in ORDER], *[delta[n] for n in ORDER], *[new_m[n] for n in ORDER], *[new_v[n] for n in ORDER])
```

```python
import jax
import jax.numpy as jnp
import numpy as np
from jax import lax
from jax.experimental import pallas as pl
from jax.experimental.pallas import tpu as pltpu

BF = jnp.bfloat16
F32 = jnp.float32
EPS = 1e-6
LANE = 128
HALO = 8
SSM_P = 64
SSM_N = 128
SSM_G = 4
SSM_L = 128
ADAM_LR, ADAM_B1, ADAM_B2, ADAM_EPS, ADAM_WD, ADAM_STEP = 0.001, 0.9, 0.999, 1e-08, 0.01, 10
VMEM_LIMIT = 56 * 1024 * 1024
TILE_ELEMS = 400_000
NCHIP = 4
MESH = pl.DeviceIdType.MESH
HI = lax.Precision.HIGHEST


def _tile(n, cap, mult=LANE):
    best = None
    t = mult
    while t <= min(n, cap):
        if n % t == 0:
            best = t
        t += mult
    return best if best is not None else n


def _row_tile(r, c, mult=16):
    return _tile(r, max(mult, TILE_ELEMS // c // mult * mult), mult)


def _params(sem):
    return pltpu.CompilerParams(dimension_semantics=sem, vmem_limit_bytes=VMEM_LIMIT)


def _sigmoid(x):
    return 1.0 / (1.0 + jnp.exp(-x))


def _dot(a, b, ca=1, cb=0, precision=None):
    return lax.dot_general(a, b, (((ca,), (cb,)), ((), ())), precision=precision, preferred_element_type=F32)


def _rms(x, g):
    r = lax.rsqrt(jnp.mean(x * x, axis=-1, keepdims=True) + EPS)
    return x * r * g


def _rms_bwd(x, g, dy):
    r = lax.rsqrt(jnp.mean(x * x, axis=-1, keepdims=True) + EPS)
    xh = x * r
    dxh = dy * g
    dx = r * (dxh - xh * jnp.mean(dxh * xh, axis=-1, keepdims=True))
    return dx, jnp.sum(dy * xh, axis=0, keepdims=True)


def _accumulate(ref, val, first):
    @pl.when(first)
    def _():
        ref[...] = val

    @pl.when(jnp.logical_not(first))
    def _():
        ref[...] += val


def mmx(name, a, b, *, grid, a_spec, b_spec, o_spec, o_shape, o_dtype, ca, cb, acc_shape=None, scale=None):
    nk = grid[-1] if acc_shape is not None else 1
    assert scale is None or nk == 1

    def body(a_ref, b_ref, o_ref, *acc):
        p = _dot(a_ref[...].astype(BF), b_ref[...].astype(BF), ca, cb)
        if scale is not None:
            p = p * scale
        if nk == 1:
            o_ref[...] = p.astype(o_ref.dtype)
        else:
            kk = pl.program_id(len(grid) - 1)
            _accumulate(acc[0], p, kk == 0)

            @pl.when(kk == nk - 1)
            def _():
                o_ref[...] = acc[0][...].astype(o_ref.dtype)

    sem = ("parallel",) * (len(grid) - 1) + ("arbitrary" if nk > 1 else "parallel",)
    return pl.pallas_call(
        body,
        name=name,
        grid=grid,
        in_specs=[a_spec, b_spec],
        out_specs=o_spec,
        out_shape=jax.ShapeDtypeStruct(o_shape, o_dtype),
        scratch_shapes=[pltpu.VMEM(acc_shape, F32)] if nk > 1 else [],
        compiler_params=_params(sem),
    )(a, b)


def mm(a, b, *, ta=False, tb=False, out_dtype=F32, name, tm_cap=1024, tn_cap=1024, tk_cap=4096):
    m, k = (a.shape[1], a.shape[0]) if ta else a.shape
    n = b.shape[0] if tb else b.shape[1]
    assert (b.shape[1] if tb else b.shape[0]) == k
    tm, tn, tk = _tile(m, tm_cap), _tile(n, tn_cap), _tile(k, tk_cap)
    nk = k // tk
    a_spec = pl.BlockSpec((tk, tm), lambda i, j, kk: (kk, i)) if ta else pl.BlockSpec((tm, tk), lambda i, j, kk: (i, kk))
    b_spec = pl.BlockSpec((tn, tk), lambda i, j, kk: (j, kk)) if tb else pl.BlockSpec((tk, tn), lambda i, j, kk: (kk, j))
    return mmx(name, a, b, grid=(m // tm, n // tn, nk), a_spec=a_spec, b_spec=b_spec, o_spec=pl.BlockSpec((tm, tn), lambda i, j, kk: (i, j)),
               o_shape=(m, n), o_dtype=out_dtype, ca=0 if ta else 1, cb=1 if tb else 0, acc_shape=(tm, tn) if nk > 1 else None)


def ew(fn, rows, vecs, out_rows, out_red=(), *, tm, name, prev_halo=(), next_halo=()):
    t = rows[0].shape[0]
    tm = min(tm, t)
    nt = t // tm
    assert t % tm == 0 and tm % HALO == 0
    nr, nv, npv, nnx, nor = len(rows), len(vecs), len(prev_halo), len(next_halo), len(out_rows)
    hb = tm // HALO

    def body(*refs):
        i = pl.program_id(0)
        ins = [r[...].astype(F32) for r in refs[: nr + nv + npv + nnx]]
        outs = refs[nr + nv + npv + nnx:]
        o_rows, o_red = fn(i, nt, ins[:nr], ins[nr:nr + nv], ins[nr + nv:nr + nv + npv], ins[nr + nv + npv:])
        for ref, val in zip(outs[:nor], o_rows, strict=True):
            ref[...] = val.astype(ref.dtype)
        for ref, val in zip(outs[nor:], o_red, strict=True):
            _accumulate(ref, val, i == 0)

    in_specs = [pl.BlockSpec((tm, r.shape[1]), lambda i: (i, 0)) for r in rows]
    in_specs += [pl.BlockSpec(v.shape, lambda i: (0, 0)) for v in vecs]
    in_specs += [pl.BlockSpec((HALO, rows[k].shape[1]), lambda i: (jnp.maximum(i * hb - 1, 0), 0)) for k in prev_halo]
    in_specs += [pl.BlockSpec((HALO, rows[k].shape[1]), lambda i: (jnp.minimum((i + 1) * hb, t // HALO - 1), 0)) for k in next_halo]
    out_specs = [pl.BlockSpec((tm, c), lambda i: (i, 0)) for c, _ in out_rows]
    out_specs += [pl.BlockSpec(s, lambda i: (0, 0)) for s in out_red]
    out_shape = [jax.ShapeDtypeStruct((t, c), d) for c, d in out_rows] + [jax.ShapeDtypeStruct(s, F32) for s in out_red]
    return pl.pallas_call(
        body,
        name=name,
        grid=(nt,),
        in_specs=in_specs,
        out_specs=out_specs,
        out_shape=out_shape,
        compiler_params=_params(("arbitrary",) if out_red else ("parallel",)),
    )(*rows, *vecs, *[rows[k] for k in prev_halo], *[rows[k] for k in next_halo])


def _shift_down(x, prev, j):
    if j == 0:
        return x
    r = pltpu.roll(x, j, 0)
    rh = pltpu.roll(prev, j, 0)
    row = lax.broadcasted_iota(jnp.int32, (HALO, x.shape[1]), 0)
    head = jnp.where(row < j, rh, r[:HALO])
    return jnp.concatenate([head, r[HALO:]], axis=0)


def _shift_up(x, nxt, j):
    if j == 0:
        return x
    n = x.shape[0]
    r = pltpu.roll(x, n - j, 0)
    rh = pltpu.roll(nxt, HALO - j, 0)
    row = lax.broadcasted_iota(jnp.int32, (HALO, x.shape[1]), 0)
    tail = jnp.where(row >= HALO - j, rh, r[n - HALO:])
    return jnp.concatenate([r[: n - HALO], tail], axis=0)


def _conv_fwd(x, prev, w):
    kk = w.shape[0]
    acc = None
    for k in range(kk):
        term = w[k:k + 1, :] * _shift_down(x, prev, kk - 1 - k)
        acc = term if acc is None else acc + term
    return acc


def ffn_up(h, g, w704, layer, ig, iu):
    t, d = h.shape
    f4 = w704.shape[3]
    tm = _tile(t, 1024)

    def body(h_ref, g_ref, wg_ref, wu_ref, ab_ref, s_ref, n_ref):
        @pl.when(pl.program_id(1) == 0)
        def _():
            n_ref[...] = _rms(h_ref[...], g_ref[...]).astype(BF)

        n = n_ref[...]
        a = _dot(n, wg_ref[...])
        b = _dot(n, wu_ref[...])
        ab_ref[0] = a.astype(BF)
        ab_ref[1] = b.astype(BF)
        s_ref[...] = (a * _sigmoid(a) * b).astype(BF)

    wspec = lambda ib: pl.BlockSpec((None, None, d, f4), lambda i, j: (j, layer, ib, 0))
    return pl.pallas_call(
        body,
        name="ffn_up",
        grid=(t // tm, NCHIP),
        in_specs=[pl.BlockSpec((tm, d), lambda i, j: (i, 0)), pl.BlockSpec((1, d), lambda i, j: (0, 0)), wspec(ig), wspec(iu)],
        out_specs=[pl.BlockSpec((2, None, tm, f4), lambda i, j: (0, j, i, 0)), pl.BlockSpec((None, tm, f4), lambda i, j: (j, i, 0)),
                   pl.BlockSpec((tm, d), lambda i, j: (i, 0))],
        out_shape=[jax.ShapeDtypeStruct((2, NCHIP, t, f4), BF), jax.ShapeDtypeStruct((NCHIP, t, f4), BF), jax.ShapeDtypeStruct((t, d), BF)],
        compiler_params=_params(("parallel", "arbitrary")),
    )(h, g, w704, w704)


def ffn_down(s4, w704r, layer, idx, h):
    t, d = h.shape
    f4 = s4.shape[2]
    tm, tn = _tile(t, 512), d

    def body(s_ref, w_ref, h_ref, o_ref):
        acc = _dot(s_ref[0], w_ref[0])
        for k in range(1, NCHIP):
            acc = acc + _dot(s_ref[k], w_ref[k])
        o_ref[...] = h_ref[...] + 0.5 * acc

    return pl.pallas_call(
        body,
        name="ffn_down",
        grid=(t // tm, d // tn),
        in_specs=[pl.BlockSpec((NCHIP, tm, f4), lambda i, j: (0, i, 0)), pl.BlockSpec((NCHIP, None, f4, tn), lambda i, j: (0, layer, idx, j)),
                  pl.BlockSpec((tm, tn), lambda i, j: (i, j))],
        out_specs=pl.BlockSpec((tm, tn), lambda i, j: (i, j)),
        out_shape=jax.ShapeDtypeStruct((t, d), F32),
        compiler_params=_params(("parallel", "parallel")),
    )(s4, w704r, h)


def ffn_bwd(dho, ab, w704, w704r, layer, ig, iu, idx):
    t, d = dho.shape
    f4 = w704.shape[3]
    tm = _tile(t, 1024)

    def body(dho_ref, ab_ref, wg_ref, wu_ref, wd_ref, dn_ref, s_ref, dab_ref, do_sc):
        j = pl.program_id(1)

        @pl.when(j == 0)
        def _():
            do_sc[...] = (0.5 * dho_ref[...]).astype(BF)

        ds = _dot(do_sc[...], wd_ref[...], 1, 1)
        av, bv = ab_ref[0].astype(F32), ab_ref[1].astype(F32)
        sig = _sigmoid(av)
        sl = av * sig
        s_ref[...] = (sl * bv).astype(BF)
        da = (ds * bv * (sig * (1.0 + av * (1.0 - sig)))).astype(BF)
        db = (ds * sl).astype(BF)
        dab_ref[0] = da
        dab_ref[1] = db
        _accumulate(dn_ref, _dot(da, wg_ref[...], 1, 1) + _dot(db, wu_ref[...], 1, 1), j == 0)

    row = lambda c: pl.BlockSpec((tm, c), lambda i, j: (i, 0))
    wspec = lambda ib: pl.BlockSpec((None, None, d, f4), lambda i, j: (j, layer, ib, 0))
    ab_spec = pl.BlockSpec((2, None, tm, f4), lambda i, j: (0, j, i, 0))
    return pl.pallas_call(
        body,
        name="ffn_bwd",
        grid=(t // tm, NCHIP),
        in_specs=[row(d), ab_spec, wspec(ig), wspec(iu), pl.BlockSpec((None, None, f4, d), lambda i, j: (j, layer, idx, 0))],
        out_specs=[row(d), pl.BlockSpec((None, tm, f4), lambda i, j: (j, i, 0)), ab_spec],
        out_shape=[jax.ShapeDtypeStruct((t, d), F32), jax.ShapeDtypeStruct((NCHIP, t, f4), BF), jax.ShapeDtypeStruct((2, NCHIP, t, f4), BF)],
        scratch_shapes=[pltpu.VMEM((tm, d), BF)],
        compiler_params=_params(("parallel", "arbitrary")),
    )(dho, ab, w704, w704, w704r)


def ffn_wgrads(n, dho, s4, dab):
    t, d = n.shape
    f4 = s4.shape[2]
    tm = _tile(d, 512)
    g_in = mmx("g_ffn_in", n, dab, grid=(2, NCHIP, d // tm), a_spec=pl.BlockSpec((t, tm), lambda wh, k, i: (0, i)),
               b_spec=pl.BlockSpec((None, None, t, f4), lambda wh, k, i: (wh, k, 0, 0)), o_spec=pl.BlockSpec((None, None, tm, f4), lambda wh, k, i: (k, wh, i, 0)),
               o_shape=(NCHIP, 2, d, f4), o_dtype=BF, ca=0, cb=0)
    g_out = mmx("g_ffn_out", s4, dho, grid=(NCHIP, d // tm), a_spec=pl.BlockSpec((None, t, f4), lambda k, j: (k, 0, 0)),
                b_spec=pl.BlockSpec((t, tm), lambda k, j: (0, j)), o_spec=pl.BlockSpec((None, f4, tm), lambda k, j: (k, 0, j)),
                o_shape=(NCHIP, f4, d), o_dtype=BF, ca=0, cb=0, scale=0.5)
    return g_in.reshape(NCHIP, 2 * d, f4), g_out


def norm_cast(h, g):
    def fn(i, nt, rows, vecs, prevs, nexts):
        return [_rms(rows[0], vecs[0])], []
    return ew(fn, [h], [g], [(h.shape[1], BF)], tm=512, name="norm_cast")[0]


def _zero_if(cond, x):
    return jnp.where(cond, jnp.zeros_like(x), x)


def conv_a_fwd(sc3, w_sc):
    d = sc3.shape[1] // 3

    def fn(i, nt, rows, vecs, prevs, nexts):
        x, pv = rows[0], _zero_if(i == 0, prevs[0])
        v = x[:, d:2 * d] * x[:, 2 * d:]
        vp = pv[:, d:2 * d] * pv[:, 2 * d:]
        return [x[:, :d] * _conv_fwd(v, vp, vecs[0])], []

    return ew(fn, [sc3], [w_sc], [(d, BF)], tm=256, name="conv_a_fwd", prev_halo=(0,))[0]


def _softplus(x):
    e = jnp.exp(-jnp.abs(x))
    return jnp.maximum(x, 0.0) + jnp.where(e < 1e-4, e - 0.5 * e * e, jnp.log(1.0 + e))


def conv_m_fwd(xbc_raw, dt_raw, w_mc, b_mc, dt_bias):
    def fn(i, nt, rows, vecs, prevs, nexts):
        pre = _conv_fwd(rows[0], _zero_if(i == 0, prevs[0]), vecs[0]) + vecs[1]
        return [pre * _sigmoid(pre), _softplus(rows[1] + vecs[2])], []

    return ew(fn, [xbc_raw, dt_raw], [w_mc, b_mc, dt_bias], [(xbc_raw.shape[1], F32), (LANE, F32)], tm=256, name="conv_m_fwd",
              prev_halo=(0,))


def conv_m_bwd1(dxbc, xbc_raw, ddt, dt_raw, w_mc, b_mc, dt_bias):
    def fn(i, nt, rows, vecs, prevs, nexts):
        pre = _conv_fwd(rows[1], _zero_if(i == 0, prevs[0]), vecs[0]) + vecs[1]
        sig = _sigmoid(pre)
        dpre = rows[0] * (sig * (1.0 + pre * (1.0 - sig)))
        ddr = rows[2] * _sigmoid(rows[3] + vecs[2])
        return [dpre, ddr], [jnp.sum(ddr, axis=0, keepdims=True)]

    return ew(fn, [dxbc, xbc_raw, ddt, dt_raw], [w_mc, b_mc, dt_bias], [(dxbc.shape[1], F32), (LANE, BF)], [(1, LANE)], tm=256,
              name="conv_m_bwd1", prev_halo=(1,))


def conv_bwd2(dpre, x, w, name):
    kk = w.shape[0]

    def fn(i, nt, rows, vecs, prevs, nexts):
        dp, xv = rows[0], rows[1]
        nx = _zero_if(i == nt - 1, nexts[0])
        pv = _zero_if(i == 0, prevs[0])
        dx = None
        dws = []
        for k in range(kk):
            term = vecs[0][k:k + 1, :] * _shift_up(dp, nx, kk - 1 - k)
            dx = term if dx is None else dx + term
            dws.append(jnp.sum(dp * _shift_down(xv, pv, kk - 1 - k), axis=0, keepdims=True))
        return [dx], [jnp.concatenate(dws, axis=0), jnp.sum(dp, axis=0, keepdims=True)]

    c = x.shape[1]
    return ew(fn, [dpre, x], [w], [(c, BF)], [(kk, c), (1, c)], tm=256, name=name, prev_halo=(1,), next_halo=(0,))


def conv_a_bwd1(dya, sc3, w_sc):
    d = sc3.shape[1] // 3

    def fn(i, nt, rows, vecs, prevs, nexts):
        x, pv = rows[1], _zero_if(i == 0, prevs[0])
        v = x[:, d:2 * d] * x[:, 2 * d:]
        vp = pv[:, d:2 * d] * pv[:, 2 * d:]
        return [rows[0] * x[:, :d], rows[0] * _conv_fwd(v, vp, vecs[0]), v], []

    return ew(fn, [dya, sc3], [w_sc], [(d, F32), (d, BF), (d, F32)], tm=256, name="conv_a_bwd1", prev_halo=(1,))


def conv_a_bwd2(dcv, v, sc3, w_sc):
    d = v.shape[1]
    kk = w_sc.shape[0]

    def fn(i, nt, rows, vecs, prevs, nexts):
        dp, vv, x = rows
        nx = _zero_if(i == nt - 1, nexts[0])
        pv = _zero_if(i == 0, prevs[0])
        dv = None
        dws = []
        for k in range(kk):
            term = vecs[0][k:k + 1, :] * _shift_up(dp, nx, kk - 1 - k)
            dv = term if dv is None else dv + term
            dws.append(jnp.sum(dp * _shift_down(vv, pv, kk - 1 - k), axis=0, keepdims=True))
        return [dv * x[:, 2 * d:], dv * x[:, d:2 * d]], [jnp.concatenate(dws, axis=0)]

    return ew(fn, [dcv, v, sc3], [w_sc], [(d, BF), (d, BF)], [(kk, d)], tm=256, name="conv_a_bwd2", prev_halo=(1,), next_halo=(0,))


def _xdot(a, b, passes, split_lhs, ca=1, cb=0):
    parts, r = [], (a if split_lhs else b)
    for _ in range(passes):
        piece = r.astype(BF)
        parts.append(piece)
        r = r - piece.astype(F32)
    other = (b if split_lhs else a).astype(BF)
    acc = None
    for piece in parts:
        term = _dot(piece, other, ca, cb) if split_lhs else _dot(other, piece, ca, cb)
        acc = term if acc is None else acc + term
    return acc


def _ssd_common(xbc_ref, dt_ref, alog_ref, e_ref, w):
    ll = SSM_L
    xs = xbc_ref[:, 0:w]
    dtv = dt_ref[...]
    a_row = -jnp.exp(alog_ref[...])
    a = dtv * a_row
    row = lax.broadcasted_iota(jnp.int32, (ll, ll), 0)
    col = lax.broadcasted_iota(jnp.int32, (ll, ll), 1)
    tril = (row >= col).astype(F32)
    triu = (row <= col).astype(F32)
    acl = _xdot(tril, a, 3, False)
    acl_t = _xdot(a, triu, 3, True, 0, 0)
    e = e_ref[...]
    aclx = _xdot(acl, e, 3, True)
    dtx = _xdot(dtv, e, 2, True)
    last = aclx[ll - 1:ll, :]
    e_in = jnp.exp(aclx)
    e_end = jnp.exp(last - aclx)
    e_tot = jnp.exp(last)
    x = xs * dtx
    return dict(xs=xs, dtv=dtv, a_row=a_row, a=a, row=row, col=col, triu=triu, acl=acl, acl_t=acl_t, dtx=dtx, e_in=e_in, e_end=e_end,
                e_tot=e_tot, x=x)


def _decay(q, hh):
    diff = q["acl"][:, hh:hh + 1] - q["acl_t"][hh:hh + 1, :]
    return jnp.exp(jnp.where(q["row"] >= q["col"], diff, -jnp.inf))


def ssd_fwd(xbc, dt, z, a_log, d_exp, m_norm, e_mat):
    t = xbc.shape[0]
    w = z.shape[1]
    gn = SSM_G * SSM_N
    gw = w // SSM_G
    ll, nn = SSM_L, SSM_N
    nc = t // ll
    cw = xbc.shape[1]

    def body(xbc_ref, dt_ref, z_ref, alog_ref, dexp_ref, mn_ref, e_ref, yn_ref, y_ref, sp_ref, s_sc):
        c = pl.program_id(0)

        @pl.when(c == 0)
        def _():
            s_sc[...] = jnp.zeros_like(s_sc)

        q = _ssd_common(xbc_ref, dt_ref, alog_ref, e_ref, w)
        xb = q["x"].astype(BF)
        xsb = (q["x"] * q["e_end"]).astype(BF)
        sp = s_sc[...]
        sp_ref[0] = sp
        spb = sp.astype(BF)
        lane = lax.broadcasted_iota(jnp.int32, (ll, LANE), 1)
        for g in range(SSM_G):
            lo = g * gw
            bg = xbc_ref[:, w + g * nn:w + (g + 1) * nn].astype(BF)
            cg = xbc_ref[:, w + gn + g * nn:w + gn + (g + 1) * nn].astype(BF)
            yoff = _dot(cg, spb[:, lo:lo + gw]) * q["e_in"][:, lo:lo + gw]
            s_sc[:, lo:lo + gw] = sp[:, lo:lo + gw] * q["e_tot"][:, lo:lo + gw] + _dot(bg, xsb[:, lo:lo + gw], 0, 0)
            cb = _dot(cg, bg, 1, 1)
            for pr in range(gw // LANE):
                l0 = lo + pr * LANE
                xp = xb[:, l0:l0 + LANE]
                ys = []
                for hh in (l0 // SSM_P, l0 // SSM_P + 1):
                    wm = (cb * _decay(q, hh)).astype(BF)
                    ys.append(_dot(wm, xp))
                ydiag = jnp.where(lane < SSM_P, ys[0], ys[1])
                y_ref[:, l0:l0 + LANE] = ydiag + yoff[:, pr * LANE:(pr + 1) * LANE] + dexp_ref[:, l0:l0 + LANE] * q["xs"][:, l0:l0 + LANE]
        zv = z_ref[...].astype(F32)
        yz = y_ref[...] * (zv * _sigmoid(zv))
        for g in range(SSM_G):
            lo = g * gw
            yn_ref[:, lo:lo + gw] = _rms(yz[:, lo:lo + gw], mn_ref[:, lo:lo + gw]).astype(BF)

    vec = lambda s: pl.BlockSpec(s, lambda c: (0, 0))
    return pl.pallas_call(
        body,
        name="ssd_fwd",
        grid=(nc,),
        in_specs=[
            pl.BlockSpec((ll, cw), lambda c: (c, 0)), pl.BlockSpec((ll, LANE), lambda c: (c, 0)), pl.BlockSpec((ll, w), lambda c: (c, 0)),
            vec((1, LANE)), vec((1, w)), vec((1, w)), vec((LANE, w)),
        ],
        out_specs=[pl.BlockSpec((ll, w), lambda c: (c, 0)), pl.BlockSpec((ll, w), lambda c: (c, 0)), pl.BlockSpec((1, nn, w), lambda c: (c, 0, 0))],
        out_shape=[jax.ShapeDtypeStruct((t, w), BF), jax.ShapeDtypeStruct((t, w), F32), jax.ShapeDtypeStruct((nc, nn, w), F32)],
        scratch_shapes=[pltpu.VMEM((nn, w), F32)],
        compiler_params=_params(("arbitrary",)),
    )(xbc, dt, z, a_log, d_exp, m_norm, e_mat)


def ssd_bwd(dyn, y, z, xbc, dt, sprev, a_log, d_exp, m_norm, e_mat, et_mat):
    t = xbc.shape[0]
    w = z.shape[1]
    gn = SSM_G * SSM_N
    gw = w // SSM_G
    ll, nn = SSM_L, SSM_N
    nc = t // ll
    cw = xbc.shape[1]

    def body(dyn_ref, y_ref, z_ref, xbc_ref, dt_ref, sp_ref, alog_ref, dexp_ref, mn_ref, e_ref, et_ref,
             dz_ref, dxbc_ref, ddt_ref, dmn_ref, dd_ref, dal_ref, ds_sc, dy_sc, dx_sc):
        step = pl.program_id(0)

        @pl.when(step == 0)
        def _():
            ds_sc[...] = jnp.zeros_like(ds_sc)

        zv, yv = z_ref[...].astype(F32), y_ref[...]
        sg = _sigmoid(zv)
        sz = zv * sg
        yz = yv * sz
        dmn = []
        for g in range(SSM_G):
            lo = g * gw
            dseg, dmn_g = _rms_bwd(yz[:, lo:lo + gw], mn_ref[:, lo:lo + gw], dyn_ref[:, lo:lo + gw])
            dy_sc[:, lo:lo + gw] = dseg
            dmn.append(dmn_g)
        dmn = jnp.concatenate(dmn, axis=1)
        dyz = dy_sc[...]
        dz_ref[...] = (dyz * yv * (sg * (1.0 + zv * (1.0 - sg)))).astype(BF)
        dy = dyz * sz

        q = _ssd_common(xbc_ref, dt_ref, alog_ref, e_ref, w)
        x = q["x"]
        xb = x.astype(BF)
        xsb = (x * q["e_end"]).astype(BF)
        sp = sp_ref[0]
        spb = sp.astype(BF)
        dsn = ds_sc[...]
        dsnb = dsn.astype(BF)
        dyb = dy.astype(BF)
        lane = lax.broadcasted_iota(jnp.int32, (ll, LANE), 1)
        lane1 = lax.broadcasted_iota(jnp.int32, (1, LANE), 1)
        sub1 = lax.broadcasted_iota(jnp.int32, (LANE, 1), 0)
        dacl = jnp.zeros((ll, LANE), F32)
        dacl_t = jnp.zeros((LANE, ll), F32)
        d_ein, d_eend, d_etot = [], [], []
        for g in range(SSM_G):
            lo = g * gw
            sl = slice(lo, lo + gw)
            bg = xbc_ref[:, w + g * nn:w + (g + 1) * nn].astype(BF)
            cg = xbc_ref[:, w + gn + g * nn:w + gn + (g + 1) * nn].astype(BF)
            zg = _dot(cg, spb[:, sl])
            dzz = (dy[:, sl] * q["e_in"][:, sl]).astype(BF)
            d_ein.append(dy[:, sl] * zg)
            dcg = _dot(dzz, spb[:, sl], 1, 1)
            ds_sc[:, sl] = _dot(cg, dzz, 0, 0) + dsn[:, sl] * q["e_tot"][:, sl]
            d_etot.append(jnp.sum(dsn[:, sl] * sp[:, sl], axis=0, keepdims=True))
            dbg = _dot(xsb[:, sl], dsnb[:, sl], 1, 1)
            dxs_g = _dot(bg, dsnb[:, sl])
            d_eend.append(dxs_g * x[:, sl])
            cb = _dot(cg, bg, 1, 1)
            dcb = jnp.zeros((ll, ll), F32)
            for pr in range(gw // LANE):
                l0 = lo + pr * LANE
                xp = xb[:, l0:l0 + LANE]
                dyp = dyb[:, l0:l0 + LANE]
                dxp = []
                for hi, hh in enumerate((l0 // SSM_P, l0 // SSM_P + 1)):
                    lm = _decay(q, hh)
                    wm = (cb * lm).astype(BF)
                    in_head = (lane < SSM_P) if hi == 0 else (lane >= SSM_P)
                    dwm = _dot(jnp.where(in_head, dyp, jnp.zeros_like(dyp)), xp, 1, 1)
                    dxp.append(_dot(wm, dyp, 0, 0))
                    dlm = dwm * lm
                    dcb = dcb + dlm
                    dd = dlm * cb
                    dacl = dacl + jnp.sum(dd, axis=1, keepdims=True) * (lane1 == hh).astype(F32)
                    dacl_t = dacl_t + (sub1 == hh).astype(F32) * jnp.sum(dd, axis=0, keepdims=True)
                dx_sc[:, l0:l0 + LANE] = jnp.where(lane < SSM_P, dxp[0], dxp[1]) + dxs_g[:, pr * LANE:(pr + 1) * LANE] * q["e_end"][:, l0:l0 + LANE]
            dcbb = dcb.astype(BF)
            dxbc_ref[:, w + g * nn:w + (g + 1) * nn] = dbg + _dot(dcbb, cg, 0, 0)
            dxbc_ref[:, w + gn + g * nn:w + gn + (g + 1) * nn] = dcg + _dot(dcbb, bg)
        d_ein = jnp.concatenate(d_ein, axis=1) * q["e_in"]
        d_eend = jnp.concatenate(d_eend, axis=1) * q["e_end"]
        d_etot = jnp.concatenate(d_etot, axis=1) * q["e_tot"]
        et = et_ref[...]
        last_add = jnp.sum(d_eend, axis=0, keepdims=True) + d_etot
        last_add = _xdot(jnp.broadcast_to(last_add, (HALO, w)), et, 2, True)[0:1]
        row1 = lax.broadcasted_iota(jnp.int32, (ll, LANE), 0)
        dacl = dacl + _xdot(d_ein - d_eend, et, 2, True) + jnp.where(row1 == ll - 1, last_add, 0.0)
        da = _xdot(q["triu"], dacl, 2, False) - _xdot(q["triu"], dacl_t, 2, False, 1, 1)
        dxv = dx_sc[...]
        dxbc_ref[:, 0:w] = dexp_ref[...] * dy + dxv * q["dtx"]
        ddt_ref[...] = _xdot(dxv * q["xs"], et, 2, True) + da * q["a_row"]
        dal = jnp.sum(da * q["dtv"], axis=0, keepdims=True) * q["a_row"]
        ddv = jnp.sum(dy * q["xs"], axis=0, keepdims=True)
        ddv = _xdot(jnp.broadcast_to(ddv, (HALO, w)), et, 2, True)[0:1]
        _accumulate(dmn_ref, dmn, step == 0)
        _accumulate(dd_ref, ddv, step == 0)
        _accumulate(dal_ref, dal, step == 0)

    rev = lambda c_: pl.BlockSpec((ll, c_), lambda s: (nc - 1 - s, 0))
    vec = lambda s_: pl.BlockSpec(s_, lambda s: (0, 0))
    return pl.pallas_call(
        body,
        name="ssd_bwd",
        grid=(nc,),
        in_specs=[
            rev(w), rev(w), rev(w), rev(cw), rev(LANE), pl.BlockSpec((1, nn, w), lambda s: (nc - 1 - s, 0, 0)),
            vec((1, LANE)), vec((1, w)), vec((1, w)), vec((LANE, w)), vec((w, LANE)),
        ],
        out_specs=[rev(w), rev(cw), rev(LANE), vec((1, w)), vec((1, LANE)), vec((1, LANE))],
        out_shape=[
            jax.ShapeDtypeStruct((t, w), BF), jax.ShapeDtypeStruct((t, cw), F32), jax.ShapeDtypeStruct((t, LANE), F32),
            jax.ShapeDtypeStruct((1, w), F32), jax.ShapeDtypeStruct((1, LANE), F32), jax.ShapeDtypeStruct((1, LANE), F32),
        ],
        scratch_shapes=[pltpu.VMEM((nn, w), F32), pltpu.VMEM((ll, w), F32), pltpu.VMEM((ll, w), F32)],
        compiler_params=_params(("arbitrary",)),
    )(dyn, y, z, xbc, dt, sprev, a_log, d_exp, m_norm, e_mat, et_mat)


def _w1024_spec(d, layer, nblk, iblk):
    r = nblk * (d // NCHIP)
    return pl.BlockSpec((NCHIP, None, r, d), lambda i: (0, layer, iblk // nblk, 0))


def _whole(ref):
    v = ref[...]
    return v.reshape(v.shape[0] * v.shape[1], v.shape[2])


def mix_out_fwd(ya_in, yn, gates, h, w1024, layer):
    t, d = h.shape
    tm = _tile(t, 256)

    def body(ya_ref, yn_ref, g_ref, h_ref, wm_ref, wa_ref, wo_ref, ho_ref, oa_ref, om_ref, mg_ref):
        y_a = _dot(ya_ref[...], _whole(wa_ref))
        y_m = _dot(yn_ref[...], _whole(wm_ref))
        oa_ref[...] = y_a
        om_ref[...] = y_m
        gv = g_ref[...].astype(F32)
        mg = (_sigmoid(gv[:, :d]) * y_a + _sigmoid(gv[:, d:]) * y_m).astype(BF)
        mg_ref[...] = mg
        ho_ref[...] = h_ref[...] + _dot(mg, _whole(wo_ref))

    row = lambda c: pl.BlockSpec((tm, c), lambda i: (i, 0))
    return pl.pallas_call(
        body,
        name="mix_out_fwd",
        grid=(t // tm,),
        in_specs=[row(d), row(2 * d), row(2 * d), row(d), _w1024_spec(d, layer, 2, 0), _w1024_spec(d, layer, 1, 2), _w1024_spec(d, layer, 1, 3)],
        out_specs=[row(d), row(d), row(d), row(d)],
        out_shape=[jax.ShapeDtypeStruct((t, d), F32), jax.ShapeDtypeStruct((t, d), F32), jax.ShapeDtypeStruct((t, d), F32),
                   jax.ShapeDtypeStruct((t, d), BF)],
        compiler_params=_params(("parallel",)),
    )(ya_in, yn, gates, h, w1024, w1024, w1024)


def mix_out_bwd(dh, gates, y_a, y_m, w1024, layer):
    t, d = dh.shape
    tm = _tile(t, 256)

    def body(dh_ref, g_ref, ya_ref, ym_ref, wm_ref, wa_ref, wo_ref, dg_ref, dya_ref, dyn_ref, da_ref, dm_ref):
        dmg = _dot(dh_ref[...].astype(BF), _whole(wo_ref), 1, 1)
        gv = g_ref[...].astype(F32)
        sa, sm = _sigmoid(gv[:, :d]), _sigmoid(gv[:, d:])
        dg_ref[:, :d] = (dmg * ya_ref[...] * sa * (1.0 - sa)).astype(BF)
        dg_ref[:, d:] = (dmg * ym_ref[...] * sm * (1.0 - sm)).astype(BF)
        da = (dmg * sa).astype(BF)
        dm = (dmg * sm).astype(BF)
        da_ref[...] = da
        dm_ref[...] = dm
        dya_ref[...] = _dot(da, _whole(wa_ref), 1, 1)
        dyn_ref[...] = _dot(dm, _whole(wm_ref), 1, 1)

    row = lambda c: pl.BlockSpec((tm, c), lambda i: (i, 0))
    return pl.pallas_call(
        body,
        name="mix_out_bwd",
        grid=(t // tm,),
        in_specs=[row(d), row(2 * d), row(d), row(d), _w1024_spec(d, layer, 2, 0), _w1024_spec(d, layer, 1, 2), _w1024_spec(d, layer, 1, 3)],
        out_specs=[row(2 * d), row(d), row(2 * d), row(d), row(d)],
        out_shape=[jax.ShapeDtypeStruct((t, 2 * d), BF), jax.ShapeDtypeStruct((t, d), F32), jax.ShapeDtypeStruct((t, 2 * d), F32),
                   jax.ShapeDtypeStruct((t, d), BF), jax.ShapeDtypeStruct((t, d), BF)],
        compiler_params=_params(("parallel",)),
    )(dh, gates, y_a, y_m, w1024, w1024, w1024)


def norm_bwd_add(dh, h, g, dn):
    def fn(i, nt, rows, vecs, prevs, nexts):
        dx, dg = _rms_bwd(rows[1], vecs[0], rows[2])
        return [rows[0] + dx], [dg]
    d = h.shape[1]
    return ew(fn, [dh, h, dn], [g], [(d, F32)], [(1, d)], tm=512, name="norm_bwd_add")


def _pe(p, wpp_ref):
    pb = p.astype(BF)
    return jnp.concatenate([_dot(pb, wpp_ref[k]) for k in range(NCHIP)], axis=1)


def ple_fwd(h, g, p, w1024, wpp, layer):
    t, d = h.shape
    tm = _tile(t, 512)

    def body(h_ref, g_ref, p_ref, wg_ref, wp_ref, ho_ref):
        hv = h_ref[...]
        gate = _sigmoid(_dot(_rms(hv, g_ref[...]).astype(BF), _whole(wg_ref)))
        ho_ref[...] = hv + gate * _pe(p_ref[...], wp_ref)

    row = lambda c: pl.BlockSpec((tm, c), lambda i: (i, 0))
    wpp_spec = pl.BlockSpec((NCHIP, None) + wpp.shape[2:], lambda i: (0, layer, 0, 0))
    return pl.pallas_call(
        body,
        name="ple_fwd",
        grid=(t // tm,),
        in_specs=[row(d), pl.BlockSpec((1, d), lambda i: (0, 0)), row(p.shape[1]), _w1024_spec(d, layer, 1, 4), wpp_spec],
        out_specs=row(d),
        out_shape=jax.ShapeDtypeStruct((t, d), F32),
        compiler_params=_params(("parallel",)),
    )(h, g, p, w1024, wpp)


def ple_bwd(dho, h, g, p, w1024, wpp, layer):
    t, d = h.shape
    tm = _tile(t, 512)

    def body(dho_ref, h_ref, g_ref, p_ref, wg_ref, wp_ref, dh_ref, dg_ref, n_ref, dgp_ref, dpe_ref):
        hv, dv = h_ref[...], dho_ref[...]
        n = _rms(hv, g_ref[...]).astype(BF)
        n_ref[...] = n
        wg = _whole(wg_ref)
        gate = _sigmoid(_dot(n, wg))
        pe = _pe(p_ref[...], wp_ref)
        dpe_ref[...] = (dv * gate).astype(BF)
        dgp = (dv * pe * gate * (1.0 - gate)).astype(BF)
        dgp_ref[...] = dgp
        dx, dg = _rms_bwd(hv, g_ref[...], _dot(dgp, wg, 1, 1))
        dh_ref[...] = dv + dx
        _accumulate(dg_ref, dg, pl.program_id(0) == 0)

    row = lambda c: pl.BlockSpec((tm, c), lambda i: (i, 0))
    wpp_spec = pl.BlockSpec((NCHIP, None) + wpp.shape[2:], lambda i: (0, layer, 0, 0))
    return pl.pallas_call(
        body,
        name="ple_bwd",
        grid=(t // tm,),
        in_specs=[row(d), row(d), pl.BlockSpec((1, d), lambda i: (0, 0)), row(p.shape[1]), _w1024_spec(d, layer, 1, 4), wpp_spec],
        out_specs=[row(d), pl.BlockSpec((1, d), lambda i: (0, 0)), row(d), row(d), row(d)],
        out_shape=[jax.ShapeDtypeStruct((t, d), F32), jax.ShapeDtypeStruct((1, d), F32), jax.ShapeDtypeStruct((t, d), BF),
                   jax.ShapeDtypeStruct((t, d), BF), jax.ShapeDtypeStruct((t, d), BF)],
        compiler_params=_params(("arbitrary",)),
    )(dho, h, g, p, w1024, wpp)


def loss_bwd(h, g, target):
    d = h.shape[1]

    def fn(i, nt, rows, vecs, prevs, nexts):
        err = _rms(rows[0], vecs[0]) - rows[1]
        dx, dg = _rms_bwd(rows[0], vecs[0], err * (1.0 / d))
        return [dx], [jnp.sum(err * err, axis=0, keepdims=True) * (0.5 / d), dg]

    return ew(fn, [h, target], [g], [(d, F32)], [(1, d), (1, d)], tm=512, name="loss_bwd")


def adamw(w, g, m, v, name):
    c1, c2 = 1.0 / (1.0 - ADAM_B1 ** ADAM_STEP), 1.0 / (1.0 - ADAM_B2 ** ADAM_STEP)

    def fn(i, nt, rows, vecs, prevs, nexts):
        wv, gv, mv, vv = rows
        mn = ADAM_B1 * mv + (1.0 - ADAM_B1) * gv
        vn = ADAM_B2 * vv + (1.0 - ADAM_B2) * (gv * gv)
        delta = -ADAM_LR * ((mn * c1) / (jnp.sqrt(vn * c2) + ADAM_EPS) + ADAM_WD * wv)
        return [delta, mn, vn], []

    c = w.shape[1]
    return ew(fn, [w, g, m, v], [], [(c, F32)] * 3, tm=_row_tile(w.shape[0], c, HALO), name=name)


def _place():
    return lax.axis_index("x"), lax.axis_index("y"), lax.axis_index("c")


def _other_chips(x, y):
    return [(1 - x, y), (x, 1 - y), (1 - x, 1 - y)]


ANY = pl.BlockSpec(memory_space=pl.ANY)


def _comm_call(body, name, ins, out_shapes, n_sems, aliases=None):
    return pl.pallas_call(
        body,
        name=name,
        in_specs=[ANY] * len(ins),
        out_specs=[ANY] * len(out_shapes),
        out_shape=out_shapes,
        scratch_shapes=[pltpu.SemaphoreType.DMA((n_sems,)), pltpu.SemaphoreType.DMA((n_sems,))],
        input_output_aliases=aliases or {},
    )(*ins)


def gather_packs(packs, name):
    nt = len(packs)
    hl = packs[0].shape[0] // 2

    def body(*refs):
        ins, outs, (send_sems, recv_sems) = refs[:nt], refs[nt:2 * nt], refs[2 * nt:]
        x, y, cc = _place()
        chips = _other_chips(x, y)
        sibling = (x, y, 1 - cc)
        mine, theirs = pl.ds(cc * hl, hl), pl.ds((1 - cc) * hl, hl)
        k_me = 2 * x + y

        def copy(k, src, dst, to):
            return pltpu.make_async_remote_copy(src_ref=src, dst_ref=dst, send_sem=send_sems.at[k], recv_sem=recv_sems.at[k],
                                                device_id=to, device_id_type=MESH)

        started = []
        for ti in range(nt):
            for j, chip in enumerate(chips):
                started.append(copy(7 * ti + j, ins[ti].at[mine], outs[ti].at[k_me, mine], (*chip, cc)))
            started.append(copy(7 * ti + 6, ins[ti], outs[ti].at[k_me], sibling))
        for cp in started:
            cp.start()
        for ti in range(nt):
            for j, (px, py) in enumerate(chips):
                landed = outs[ti].at[2 * px + py, mine]
                copy(7 * ti + j, landed, landed, (px, py, cc)).wait_recv()
                fw = copy(7 * ti + 3 + j, landed, landed, sibling)
                fw.start()
                started.append(fw)
        for ti in range(nt):
            for j, (px, py) in enumerate(chips):
                landed = outs[ti].at[2 * px + py, theirs]
                copy(7 * ti + 3 + j, landed, landed, sibling).wait_recv()
            own = outs[ti].at[k_me]
            copy(7 * ti + 6, own, own, sibling).wait_recv()
        for cp in started:
            cp.wait_send()

    return _comm_call(body, name, packs, [jax.ShapeDtypeStruct((NCHIP,) + p.shape, p.dtype) for p in packs], 7 * nt)


def swap_packs(gs, name):
    nt = len(gs)
    hl = gs[0].shape[1] // 2

    def body(*refs):
        ins, outs, (send_sems, recv_sems) = refs[:nt], refs[nt:2 * nt], refs[2 * nt:]
        x, y, cc = _place()
        theirs = pl.ds((1 - cc) * hl, hl)
        cps = [pltpu.make_async_remote_copy(src_ref=ins[ti].at[:, theirs], dst_ref=outs[ti], send_sem=send_sems.at[ti], recv_sem=recv_sems.at[ti],
                                            device_id=(x, y, 1 - cc), device_id_type=MESH) for ti in range(nt)]
        for cp in cps:
            cp.start()
        for cp in cps:
            cp.wait()

    return _comm_call(body, name, gs, [jax.ShapeDtypeStruct((NCHIP, hl) + g.shape[2:], g.dtype) for g in gs], nt)


def scatter_packs(cs, name):
    nt = len(cs)

    def body(*refs):
        ins, outs, (send_sems, recv_sems) = refs[:nt], refs[nt:2 * nt], refs[2 * nt:]
        x, y, cc = _place()
        cps = []
        for ti in range(nt):
            for j, (px, py) in enumerate(_other_chips(x, y)):
                cps.append(pltpu.make_async_remote_copy(src_ref=ins[ti].at[2 * px + py], dst_ref=outs[ti].at[j], send_sem=send_sems.at[3 * ti + j],
                                                        recv_sem=recv_sems.at[3 * ti + j], device_id=(px, py, cc), device_id_type=MESH))
        for cp in cps:
            cp.start()
        for cp in cps:
            cp.wait()

    return _comm_call(body, name, cs, [jax.ShapeDtypeStruct((3,) + c_.shape[1:], c_.dtype) for c_ in cs], 3 * nt)


def join_packs(fulls, name):
    nt = len(fulls)
    hl = fulls[0].shape[0] // 2

    def body(*refs):
        ins, outs, (send_sems, recv_sems) = refs[:nt], refs[nt:2 * nt], refs[2 * nt:]
        x, y, cc = _place()
        mine = pl.ds(cc * hl, hl)
        cps = [pltpu.make_async_remote_copy(src_ref=ins[ti].at[mine], dst_ref=outs[ti].at[mine], send_sem=send_sems.at[ti], recv_sem=recv_sems.at[ti],
                                            device_id=(x, y, 1 - cc), device_id_type=MESH) for ti in range(nt)]
        for cp in cps:
            cp.start()
        for cp in cps:
            cp.wait()

    return _comm_call(body, name, fulls, [jax.ShapeDtypeStruct(f.shape, f.dtype) for f in fulls], nt, aliases={ti: ti for ti in range(nt)})


def add_sibling(g, recv, name):
    _, nl, r, c = g.shape
    hl = nl // 2
    tm = _row_tile(r, c)

    def body(g_ref, r_ref, o_ref):
        o_ref[...] = (g_ref[...].astype(F32) + r_ref[...].astype(F32)).astype(o_ref.dtype)

    blk = (None, None, tm, c)
    return pl.pallas_call(
        body,
        name=name,
        grid=(NCHIP, hl, r // tm),
        in_specs=[pl.BlockSpec(blk, lambda k, l, i: (k, lax.axis_index("c") * hl + l, i, 0)), pl.BlockSpec(blk, lambda k, l, i: (k, l, i, 0))],
        out_specs=pl.BlockSpec(blk, lambda k, l, i: (k, l, i, 0)),
        out_shape=jax.ShapeDtypeStruct(recv.shape, BF),
        compiler_params=_params(("parallel", "parallel", "parallel")),
    )(g, recv)


def add_chips(cs, got, nl, name):
    _, hl, r, c = cs.shape
    tm = _row_tile(r, c)

    def body(own_ref, got_ref, o_ref):
        o_ref[...] = own_ref[...].astype(F32) + got_ref[0].astype(F32) + got_ref[1].astype(F32) + got_ref[2].astype(F32)

    return pl.pallas_call(
        body,
        name=name,
        grid=(hl, r // tm),
        in_specs=[pl.BlockSpec((None, None, tm, c), lambda l, i: (2 * lax.axis_index("x") + lax.axis_index("y"), l, i, 0)),
                  pl.BlockSpec((3, None, tm, c), lambda l, i: (0, l, i, 0))],
        out_specs=pl.BlockSpec((None, tm, c), lambda l, i: (lax.axis_index("c") * hl + l, i, 0)),
        out_shape=jax.ShapeDtypeStruct((nl, r, c), F32),
        compiler_params=_params(("parallel", "parallel")),
    )(cs, got)


def all_gather_xy(shard, name):
    r, c = shard.shape
    hr = r // 2
    assert r % 32 == 0

    def body(x_ref, out_ref, send_sems, recv_sems, local_sem):
        x, y, cc = _place()
        chips = _other_chips(x, y)
        mine = pl.ds(pl.multiple_of(cc * hr, 16), hr)
        theirs = pl.ds(pl.multiple_of((1 - cc) * hr, 16), hr)
        k_me = 2 * x + y

        def copy(k, src, dst, to):
            return pltpu.make_async_remote_copy(src_ref=src, dst_ref=dst, send_sem=send_sems.at[k], recv_sem=recv_sems.at[k],
                                                device_id=to, device_id_type=MESH)

        own = pltpu.make_async_copy(x_ref, out_ref.at[k_me], local_sem)
        own.start()
        first = [copy(j, x_ref.at[mine], out_ref.at[k_me, mine], (*chip, cc)) for j, chip in enumerate(chips)]
        for cp in first:
            cp.start()
        passed = []
        for j, (px, py) in enumerate(chips):
            landed = out_ref.at[2 * px + py, mine]
            copy(j, landed, landed, (px, py, cc)).wait_recv()
            fw = copy(3 + j, landed, landed, (x, y, 1 - cc))
            fw.start()
            passed.append(fw)
        for j, (px, py) in enumerate(chips):
            landed = out_ref.at[2 * px + py, theirs]
            copy(3 + j, landed, landed, (x, y, 1 - cc)).wait_recv()
        for cp in first + passed:
            cp.wait_send()
        own.wait()

    return pl.pallas_call(
        body,
        name=name,
        in_specs=[ANY],
        out_specs=ANY,
        out_shape=jax.ShapeDtypeStruct((NCHIP, r, c), shard.dtype),
        scratch_shapes=[pltpu.SemaphoreType.DMA((6,)), pltpu.SemaphoreType.DMA((6,)), pltpu.SemaphoreType.DMA],
    )(shard)


def all_gather_8(block, name):
    m, c = block.shape

    def body(x_ref, out_ref, send_sems, recv_sems, local_sem):
        x, y, cc = _place()
        me, sibling = (x, y, cc), (x, y, 1 - cc)
        chips = _other_chips(x, y)

        def rows(px, py, pc):
            return out_ref.at[4 * px + 2 * py + pc]

        def copy(k, blk, to, src=None):
            return pltpu.make_async_remote_copy(src_ref=rows(*blk) if src is None else src, dst_ref=rows(*blk), send_sem=send_sems.at[k],
                                                recv_sem=recv_sems.at[k], device_id=to, device_id_type=MESH)

        mine = pltpu.make_async_copy(x_ref, rows(*me), local_sem)
        mine.start()
        first = [copy(0, me, sibling, src=x_ref)]
        first += [copy(1 + j, me, (*chip, cc), src=x_ref) for j, chip in enumerate(chips)]
        for cp in first:
            cp.start()
        passed = [copy(4 + j, (*chip, cc), sibling) for j, chip in enumerate(chips)]
        for j, chip in enumerate(chips):
            copy(1 + j, (*chip, cc), me).wait_recv()
            passed[j].start()
        copy(0, sibling, me).wait_recv()
        for j, chip in enumerate(chips):
            copy(4 + j, (*chip, 1 - cc), me).wait_recv()
        for cp in first + passed:
            cp.wait_send()
        mine.wait()

    return pl.pallas_call(
        body,
        name=name,
        in_specs=[pl.BlockSpec(memory_space=pltpu.VMEM)],
        out_specs=pl.BlockSpec(memory_space=pltpu.VMEM),
        out_shape=jax.ShapeDtypeStruct((8, m, c), block.dtype),
        scratch_shapes=[pltpu.SemaphoreType.DMA((7,)), pltpu.SemaphoreType.DMA((7,)), pltpu.SemaphoreType.DMA],
        compiler_params=pltpu.CompilerParams(vmem_limit_bytes=VMEM_LIMIT),
    )(block)


def add_parts(parts, out_dtype, name, tm=512):
    def fn(i, nt, rows, vecs, prevs, nexts):
        acc = rows[0]
        for r_ in rows[1:]:
            acc = acc + r_
        return [acc], []
    r, c = parts[0].shape
    return ew(fn, list(parts), [], [(c, out_dtype)], tm=_tile(r, tm, 16), name=name)[0]


SMALL_SHARDED = ("sc_conv_w", "m_conv_w")
SMALL_REPL = ("ffn1_norm", "mix_norm", "m_conv_b", "m_dt_bias", "m_A_log", "m_D", "m_norm", "ffn2_norm", "ple_norm", "final_norm")
BIG = ("ffn1_wg", "ffn1_wu", "ffn1_wd", "w_in", "sc_w_out", "m_w_out", "w_o", "ffn2_wg", "ffn2_wu", "ffn2_wd", "ple_w_gate", "ple_w_proj")
ORDER = ("ffn1_norm", "ffn1_wg", "ffn1_wu", "ffn1_wd", "mix_norm", "w_in", "sc_conv_w", "sc_w_out", "m_conv_w", "m_conv_b", "m_dt_bias",
         "m_A_log", "m_D", "m_norm", "m_w_out", "w_o", "ffn2_norm", "ffn2_wg", "ffn2_wu", "ffn2_wd", "ple_norm", "ple_w_gate", "ple_w_proj",
         "final_norm")


def _pack(arrs, cols, row_mult):
    flat = jnp.concatenate([a.reshape(-1) for a in arrs])
    n = flat.shape[0]
    rows = -(-n // cols)
    rows = -(-rows // row_mult) * row_mult
    return jnp.pad(flat, (0, rows * cols - n)).reshape(rows, cols)


def _unpack(flat2d, shapes):
    flat = flat2d.reshape(-1)
    out, off = [], 0
    for s in shapes:
        n = int(np.prod(s))
        out.append(flat[off:off + n].reshape(s))
        off += n
    return out


def _row_cat(arrs, dtype):
    return jnp.concatenate([a.astype(dtype) for a in arrs], axis=1)


def kernel(x, p, ffn1_norm, ffn1_wg, ffn1_wu, ffn1_wd, mix_norm, w_in, sc_conv_w, sc_w_out, m_conv_w, m_conv_b, m_dt_bias, m_A_log, m_D, m_norm, m_w_out, w_o, ffn2_norm, ffn2_wg, ffn2_wu, ffn2_wd, ple_norm, ple_w_gate, ple_w_proj, final_norm, loss_target, m_ffn1_norm, m_ffn1_wg, m_ffn1_wu, m_ffn1_wd, m_mix_norm, m_w_in, m_sc_conv_w, m_sc_w_out, m_m_conv_w, m_m_conv_b, m_m_dt_bias, m_m_A_log, m_m_D, m_m_norm, m_m_w_out, m_w_o, m_ffn2_norm, m_ffn2_wg, m_ffn2_wu, m_ffn2_wd, m_ple_norm, m_ple_w_gate, m_ple_w_proj, m_final_norm, v_ffn1_norm, v_ffn1_wg, v_ffn1_wu, v_ffn1_wd, v_mix_norm, v_w_in, v_sc_conv_w, v_sc_w_out, v_m_conv_w, v_m_conv_b, v_m_dt_bias, v_m_A_log, v_m_D, v_m_norm, v_m_w_out, v_w_o, v_ffn2_norm, v_ffn2_wg, v_ffn2_wu, v_ffn2_wd, v_ple_norm, v_ple_w_gate, v_ple_w_proj, v_final_norm):
    args = dict(locals())
    wts = {n: args[n] for n in ORDER}
    mom = {n: args["m_" + n] for n in ORDER}
    vel = {n: args["v_" + n] for n in ORDER}

    depth = ffn1_norm.shape[0]
    d = x.shape[-1]
    w = 2 * d
    hh = w // SSM_P
    cw = w + 2 * SSM_G * SSM_N
    d4 = d // NCHIP
    my_x, my_y, my_c = _place()
    k_me = 2 * my_x + my_y

    packs = [
        _row_cat([ffn1_wg, ffn1_wu, ffn2_wg, ffn2_wu], BF),
        _row_cat([ffn1_wd, ffn2_wd], BF),
        _row_cat([m_w_out, sc_w_out, w_o, ple_w_gate], BF),
        w_in.astype(BF),
        ple_w_proj.astype(BF),
    ]
    w704, w704r, w1024, win4, wpp = gather_packs(packs, "gather_weights")
    small_local = [sc_conv_w, m_conv_w]
    gathered_s = all_gather_xy(_pack(small_local, LANE, 32), "gather_conv_weights")
    per_shard_s = [_unpack(gathered_s[k], [a.shape for a in small_local]) for k in range(NCHIP)]
    sc_conv_full = jnp.concatenate([per_shard_s[k][0] for k in range(NCHIP)], axis=2)
    m_conv_full = jnp.concatenate([per_shard_s[k][1] for k in range(NCHIP)], axis=2)

    wi = jnp.transpose(win4, (1, 2, 0, 3)).reshape(depth, d, -1)
    o_z, o_xbc, o_dt, o_g = 3 * d, 5 * d, 5 * d + cw, 5 * d + cw + hh
    w_sc3, w_z, w_xbc = wi[:, :, :o_z], wi[:, :, o_z:o_xbc], wi[:, :, o_xbc:o_dt]
    w_g2 = wi[:, :, o_g:o_g + 2 * d]
    w_dt = jnp.pad(wi[:, :, o_dt:o_g], ((0, 0), (0, 0), (0, LANE - hh)))
    w_in_p = jnp.concatenate([w_sc3, w_z, w_xbc, w_g2, w_dt], axis=2)

    pad_h = lambda a: jnp.pad(a, ((0, 0), (0, LANE - hh)))
    dt_bias_p, a_log_p = pad_h(m_dt_bias), pad_h(m_A_log)
    d_exp = jnp.repeat(m_D, SSM_P, axis=1)
    e_mat = (jnp.arange(w)[None, :] // SSM_P == jnp.arange(LANE)[:, None]).astype(F32)
    et_mat = e_mat.T

    h = x[0]
    saved = []
    for i in range(depth):
        s = {}
        s["h0"] = h
        s["ab1"], s4, s["n1"] = ffn_up(h, ffn1_norm[i:i + 1], w704, i, 0, 1)
        h = ffn_down(s4, w704r, i, 0, h)
        s["h1"] = h
        u = norm_cast(h, mix_norm[i:i + 1])
        s["u"] = u
        s["sc3"] = mm(u, w_sc3[i], out_dtype=BF, name="proj_sc")
        s["z"] = mm(u, w_z[i], out_dtype=BF, name="proj_z")
        s["xbc_raw"] = mm(u, w_xbc[i], out_dtype=BF, name="proj_xbc")
        s["gates"] = mm(u, w_g2[i], out_dtype=BF, name="proj_gates")
        s["dt_raw"] = mm(u, w_dt[i], name="proj_dt")
        s["ya_in"] = conv_a_fwd(s["sc3"], sc_conv_full[i])
        s["xbc"], s["dt"] = conv_m_fwd(s["xbc_raw"], s["dt_raw"], m_conv_full[i], m_conv_b[i:i + 1], dt_bias_p[i:i + 1])
        s["yn"], s["y"], s["sprev"] = ssd_fwd(s["xbc"], s["dt"], s["z"], a_log_p[i:i + 1], d_exp[i:i + 1], m_norm[i:i + 1], e_mat)
        h, s["y_a"], s["y_m"], s["merged"] = mix_out_fwd(s["ya_in"], s["yn"], s["gates"], h, w1024, i)
        s["h2"] = h
        s["ab2"], s4, s["n2"] = ffn_up(h, ffn2_norm[i:i + 1], w704, i, 2, 3)
        h = ffn_down(s4, w704r, i, 1, h)
        s["h3"] = h
        h = ple_fwd(h, ple_norm[i:i + 1], p[i, 0], w1024, wpp, i)
        saved.append(s)

    dh, loss_lanes, g_final = loss_bwd(h, final_norm[None, :], loss_target[0])
    loss = lax.psum(jnp.sum(loss_lanes), ("x", "y", "c"))

    g704, g704r, g1024, gin, gpp = ([None] * depth for _ in range(5))
    gs = {n: [None] * depth for n in SMALL_SHARDED + SMALL_REPL if n != "final_norm"}
    for i in reversed(range(depth)):
        s = saved[i]
        dh, gs["ple_norm"][i], n3, dgp, dpe = ple_bwd(dh, s["h3"], ple_norm[i:i + 1], p[i, 0], w1024, wpp, i)
        g_pg = mm(n3, dgp, ta=True, out_dtype=BF, name="g_ple_gate", tm_cap=512, tn_cap=512)
        g_pp = mm(p[i, 0], dpe, ta=True, out_dtype=BF, name="g_ple_proj", tm_cap=512, tn_cap=512)
        gpp[i] = jnp.transpose(g_pp.reshape(g_pp.shape[0], NCHIP, d4), (1, 0, 2))
        dn2, s2, dab2 = ffn_bwd(dh, s["ab2"], w704, w704r, i, 2, 3, 1)
        g_in2, g_out2 = ffn_wgrads(s["n2"], dh, s2, dab2)
        dh, gs["ffn2_norm"][i] = norm_bwd_add(dh, s["h2"], ffn2_norm[i:i + 1], dn2)
        dgates, dya, dyn, dy_a, dy_m = mix_out_bwd(dh, s["gates"], s["y_a"], s["y_m"], w1024, i)
        g_wo = mm(s["merged"], dh, ta=True, out_dtype=BF, name="g_w_o", tm_cap=512, tn_cap=512)
        g_sco = mm(s["ya_in"], dy_a, ta=True, out_dtype=BF, name="g_sc_out", tm_cap=512, tn_cap=512)
        g_mo = mm(s["yn"], dy_m, ta=True, out_dtype=BF, name="g_m_out", tm_cap=512, tn_cap=512)
        g1024[i] = jnp.concatenate([g_mo.reshape(NCHIP, 2 * d4, d), g_sco.reshape(NCHIP, d4, d), g_wo.reshape(NCHIP, d4, d),
                                    g_pg.reshape(NCHIP, d4, d)], axis=1)
        dz, dxbc, ddt, gs["m_norm"][i], gd, gal = ssd_bwd(dyn, s["y"], s["z"], s["xbc"], s["dt"], s["sprev"], a_log_p[i:i + 1], d_exp[i:i + 1],
                                                          m_norm[i:i + 1], e_mat, et_mat)
        gs["m_D"][i], gs["m_A_log"][i] = gd[:, :hh], gal[:, :hh]
        dpre, ddt_raw, gdb = conv_m_bwd1(dxbc, s["xbc_raw"], ddt, s["dt_raw"], m_conv_full[i], m_conv_b[i:i + 1], dt_bias_p[i:i + 1])
        gs["m_dt_bias"][i] = gdb[:, :hh]
        dxbc_raw, gs["m_conv_w"][i], gs["m_conv_b"][i] = conv_bwd2(dpre, s["xbc_raw"], m_conv_full[i], "conv_m_bwd2")
        dcv, dsc_b, v = conv_a_bwd1(dya, s["sc3"], sc_conv_full[i])
        dsc_c, dsc_x, gs["sc_conv_w"][i] = conv_a_bwd2(dcv, v, s["sc3"], sc_conv_full[i])
        dproj = jnp.concatenate([dsc_b, dsc_c, dsc_x, dz, dxbc_raw, dgates, ddt_raw], axis=1)
        du = mm(dproj, w_in_p[i], tb=True, name="d_proj_in", tn_cap=512)
        gwp = mm(s["u"], dproj, ta=True, out_dtype=BF, name="g_w_in", tm_cap=512, tn_cap=1152)
        gw_cols = jnp.concatenate([gwp[:, :5 * d + cw], gwp[:, 7 * d + cw:7 * d + cw + hh], gwp[:, 5 * d + cw:7 * d + cw]], axis=1)
        gin[i] = jnp.transpose(gw_cols.reshape(d, NCHIP, -1), (1, 0, 2))
        dh, gs["mix_norm"][i] = norm_bwd_add(dh, s["h1"], mix_norm[i:i + 1], du)
        dn1, s1, dab1 = ffn_bwd(dh, s["ab1"], w704, w704r, i, 0, 1, 0)
        g_in1, g_out1 = ffn_wgrads(s["n1"], dh, s1, dab1)
        dh, gs["ffn1_norm"][i] = norm_bwd_add(dh, s["h0"], ffn1_norm[i:i + 1], dn1)
        g704[i] = jnp.concatenate([g_in1, g_in2], axis=1)
        g704r[i] = jnp.concatenate([g_out1, g_out2], axis=1)
    grad_x = dh[None]

    gpacks = [jnp.stack(g, axis=1) for g in (g704, g704r, g1024, gin, gpp)]
    from_sibling = swap_packs(gpacks, "grad_swap_halves")
    cs = [add_sibling(g, r_, "grad_add_sibling") for g, r_ in zip(gpacks, from_sibling, strict=True)]
    got = scatter_packs(cs, "grad_scatter")
    halves = [add_chips(c_, g_, depth, "grad_add_chips") for c_, g_ in zip(cs, got, strict=True)]
    r704, r704r, r1024, rin, rpp = join_packs(halves, "grad_join_halves")
    f4 = r704.shape[2]
    grads = {
        "ffn1_wg": r704[:, :d], "ffn1_wu": r704[:, d:2 * d], "ffn2_wg": r704[:, 2 * d:3 * d], "ffn2_wu": r704[:, 3 * d:],
        "ffn1_wd": r704r[:, :f4], "ffn2_wd": r704r[:, f4:],
        "m_w_out": r1024[:, :2 * d4], "sc_w_out": r1024[:, 2 * d4:3 * d4], "w_o": r1024[:, 3 * d4:4 * d4], "ple_w_gate": r1024[:, 4 * d4:],
        "w_in": rin, "ple_w_proj": rpp,
    }

    small_names = list(SMALL_SHARDED + SMALL_REPL)
    small_full = [g_final[0] if n == "final_norm" else jnp.stack(gs[n]) for n in small_names]
    small_pack = _pack(small_full, LANE, HALO)
    all8 = all_gather_8(small_pack, "gather_small_grads")
    small_sum = add_parts([all8[k] for k in range(8)], F32, "add_small_grads", tm=256)
    for n, tot in zip(small_names, _unpack(small_sum, [a.shape for a in small_full]), strict=True):
        if n in SMALL_SHARDED:
            cl = wts[n].shape[2]
            grads[n] = lax.dynamic_slice_in_dim(tot, k_me * cl, cl, axis=2)
        else:
            grads[n] = tot.reshape(wts[n].shape)

    delta, new_m, new_v = {}, {}, {}
    for n in BIG:
        shp = wts[n].shape
        two = lambda a: a.reshape(-1, shp[-1])
        dl, nm, nv = adamw(two(wts[n]), two(grads[n]), two(mom[n]), two(vel[n]), "adamw_" + "x".join(map(str, shp[1:])))
        delta[n], new_m[n], new_v[n] = dl.reshape(shp), nm.reshape(shp), nv.reshape(shp)
    sm_shapes = [wts[n].shape for n in small_names]
    pk = lambda dct: _pack([dct[n] for n in small_names], LANE, HALO)
    dl, nm, nv = adamw(pk(wts), pk(grads), pk(mom), pk(vel), "adamw_small")
    for n, a, b_, c_ in zip(small_names, _unpack(dl, sm_shapes), _unpack(nm, sm_shapes), _unpack(nv, sm_shapes), strict=True):
        delta[n], new_m[n], new_v[n] = a, b_, c_

    return (loss, grad_x, *[grads[n] for n in ORDER], *[delta[n] for n in ORDER], *[new_m[n] for n in ORDER], *[new_v[n] for n in ORDER])
```

```python
import jax
import jax.numpy as jnp
import numpy as np
from jax import lax
from jax.experimental import pallas as pl
from jax.experimental.pallas import tpu as pltpu

BF = jnp.bfloat16
F32 = jnp.float32
EPS = 1e-6
LANE = 128
HALO = 8
SSM_P = 64
SSM_N = 128
SSM_G = 4
SSM_L = 128
ADAM_LR, ADAM_B1, ADAM_B2, ADAM_EPS, ADAM_WD, ADAM_STEP = 0.001, 0.9, 0.999, 1e-08, 0.01, 10
VMEM_LIMIT = 56 * 1024 * 1024
TILE_ELEMS = 400_000
NCHIP = 4
FFN_SUB = 256
MESH = pl.DeviceIdType.MESH
HI = lax.Precision.HIGHEST


def _tile(n, cap, mult=LANE):
    best = None
    t = mult
    while t <= min(n, cap):
        if n % t == 0:
            best = t
        t += mult
    return best if best is not None else n


def _row_tile(r, c, mult=16):
    return _tile(r, max(mult, TILE_ELEMS // c // mult * mult), mult)


def _tile2(r, c, mult=16):
    tm = _row_tile(r, c, mult)
    tc = c if tm * c <= TILE_ELEMS else _tile(c, max(LANE, TILE_ELEMS // tm // LANE * LANE))
    return tm, tc


def _params(sem):
    return pltpu.CompilerParams(dimension_semantics=sem, vmem_limit_bytes=VMEM_LIMIT)


def _sigmoid(x):
    return 1.0 / (1.0 + jnp.exp(-x))


def _dot(a, b, ca=1, cb=0, precision=None):
    return lax.dot_general(a, b, (((ca,), (cb,)), ((), ())), precision=precision, preferred_element_type=F32)


def _rms(x, g):
    r = lax.rsqrt(jnp.mean(x * x, axis=-1, keepdims=True) + EPS)
    return x * r * g


def _rms_bwd(x, g, dy):
    r = lax.rsqrt(jnp.mean(x * x, axis=-1, keepdims=True) + EPS)
    xh = x * r
    dxh = dy * g
    dx = r * (dxh - xh * jnp.mean(dxh * xh, axis=-1, keepdims=True))
    return dx, jnp.sum(dy * xh, axis=0, keepdims=True)


def _accumulate(ref, val, first):
    @pl.when(first)
    def _():
        ref[...] = val

    @pl.when(jnp.logical_not(first))
    def _():
        ref[...] += val


def mmx(name, a, b, *, grid, a_spec, b_spec, o_spec, o_shape, o_dtype, ca, cb, acc_shape=None, scale=None):
    nk = grid[-1] if acc_shape is not None else 1
    assert scale is None or nk == 1

    def body(a_ref, b_ref, o_ref, *acc):
        p = _dot(a_ref[...].astype(BF), b_ref[...].astype(BF), ca, cb)
        if scale is not None:
            p = p * scale
        if nk == 1:
            o_ref[...] = p.astype(o_ref.dtype)
        else:
            kk = pl.program_id(len(grid) - 1)
            _accumulate(acc[0], p, kk == 0)

            @pl.when(kk == nk - 1)
            def _():
                o_ref[...] = acc[0][...].astype(o_ref.dtype)

    sem = ("parallel",) * (len(grid) - 1) + ("arbitrary" if nk > 1 else "parallel",)
    return pl.pallas_call(
        body,
        name=name,
        grid=grid,
        in_specs=[a_spec, b_spec],
        out_specs=o_spec,
        out_shape=jax.ShapeDtypeStruct(o_shape, o_dtype),
        scratch_shapes=[pltpu.VMEM(acc_shape, F32)] if nk > 1 else [],
        compiler_params=_params(sem),
    )(a, b)


def mm(a, b, *, ta=False, tb=False, out_dtype=F32, name, tm_cap=1024, tn_cap=1024, tk_cap=4096):
    m, k = (a.shape[1], a.shape[0]) if ta else a.shape
    n = b.shape[0] if tb else b.shape[1]
    assert (b.shape[1] if tb else b.shape[0]) == k
    tm, tn, tk = _tile(m, tm_cap), _tile(n, tn_cap), _tile(k, tk_cap)
    nk = k // tk
    a_spec = pl.BlockSpec((tk, tm), lambda i, j, kk: (kk, i)) if ta else pl.BlockSpec((tm, tk), lambda i, j, kk: (i, kk))
    b_spec = pl.BlockSpec((tn, tk), lambda i, j, kk: (j, kk)) if tb else pl.BlockSpec((tk, tn), lambda i, j, kk: (kk, j))
    return mmx(name, a, b, grid=(m // tm, n // tn, nk), a_spec=a_spec, b_spec=b_spec, o_spec=pl.BlockSpec((tm, tn), lambda i, j, kk: (i, j)),
               o_shape=(m, n), o_dtype=out_dtype, ca=0 if ta else 1, cb=1 if tb else 0, acc_shape=(tm, tn) if nk > 1 else None)


def ew(fn, rows, vecs, out_rows, out_red=(), *, tm, name, prev_halo=(), next_halo=()):
    t = rows[0].shape[0]
    tm = min(tm, t)
    nt = t // tm
    assert t % tm == 0 and (tm % HALO == 0 or (tm == t and not prev_halo and not next_halo))
    nr, nv, npv, nnx, nor = len(rows), len(vecs), len(prev_halo), len(next_halo), len(out_rows)
    hb = tm // HALO

    def body(*refs):
        i = pl.program_id(0)
        ins = [r[...].astype(F32) for r in refs[: nr + nv + npv + nnx]]
        outs = refs[nr + nv + npv + nnx:]
        o_rows, o_red = fn(i, nt, ins[:nr], ins[nr:nr + nv], ins[nr + nv:nr + nv + npv], ins[nr + nv + npv:])
        for ref, val in zip(outs[:nor], o_rows, strict=True):
            ref[...] = val.astype(ref.dtype)
        for ref, val in zip(outs[nor:], o_red, strict=True):
            _accumulate(ref, val, i == 0)

    in_specs = [pl.BlockSpec((tm, r.shape[1]), lambda i: (i, 0)) for r in rows]
    in_specs += [pl.BlockSpec(v.shape, lambda i: (0, 0)) for v in vecs]
    in_specs += [pl.BlockSpec((HALO, rows[k].shape[1]), lambda i: (jnp.maximum(i * hb - 1, 0), 0)) for k in prev_halo]
    in_specs += [pl.BlockSpec((HALO, rows[k].shape[1]), lambda i: (jnp.minimum((i + 1) * hb, t // HALO - 1), 0)) for k in next_halo]
    out_specs = [pl.BlockSpec((tm, c), lambda i: (i, 0)) for c, _ in out_rows]
    out_specs += [pl.BlockSpec(s, lambda i: (0, 0)) for s in out_red]
    out_shape = [jax.ShapeDtypeStruct((t, c), d) for c, d in out_rows] + [jax.ShapeDtypeStruct(s, F32) for s in out_red]
    return pl.pallas_call(
        body,
        name=name,
        grid=(nt,),
        in_specs=in_specs,
        out_specs=out_specs,
        out_shape=out_shape,
        compiler_params=_params(("arbitrary",) if out_red else ("parallel",)),
    )(*rows, *vecs, *[rows[k] for k in prev_halo], *[rows[k] for k in next_halo])


def _shift_down(x, prev, j):
    if j == 0:
        return x
    r = pltpu.roll(x, j, 0)
    rh = pltpu.roll(prev, j, 0)
    row = lax.broadcasted_iota(jnp.int32, (HALO, x.shape[1]), 0)
    head = jnp.where(row < j, rh, r[:HALO])
    return jnp.concatenate([head, r[HALO:]], axis=0)


def _shift_up(x, nxt, j):
    if j == 0:
        return x
    n = x.shape[0]
    r = pltpu.roll(x, n - j, 0)
    rh = pltpu.roll(nxt, HALO - j, 0)
    row = lax.broadcasted_iota(jnp.int32, (HALO, x.shape[1]), 0)
    tail = jnp.where(row >= HALO - j, rh, r[n - HALO:])
    return jnp.concatenate([r[: n - HALO], tail], axis=0)


def _conv_fwd(x, prev, w):
    kk = w.shape[0]
    acc = None
    for k in range(kk):
        term = w[k:k + 1, :] * _shift_down(x, prev, kk - 1 - k)
        acc = term if acc is None else acc + term
    return acc


def ffn_up(h, g, wf, layer, ffn):
    t, d = h.shape
    f4 = wf.shape[2] // 6
    tm = _tile(t, 1024)
    sub = _tile(tm, FFN_SUB, 16)

    def body(h_ref, g_ref, wg_ref, wu_ref, ab_ref, s_ref, n_ref):
        @pl.when(pl.program_id(1) == 0)
        def _():
            n_ref[...] = _rms(h_ref[...], g_ref[...]).astype(BF)

        for r in range(tm // sub):
            rows = slice(r * sub, (r + 1) * sub)
            n = n_ref[rows, :]
            a = _dot(n, wg_ref[...], 1, 1)
            b = _dot(n, wu_ref[...], 1, 1)
            ab_ref[0, rows, :] = a.astype(BF)
            ab_ref[1, rows, :] = b.astype(BF)
            s_ref[rows, :] = (a * _sigmoid(a) * b).astype(BF)

    wspec = lambda ib: pl.BlockSpec((None, None, f4, d), lambda i, j: (j, layer, ib, 0))
    return pl.pallas_call(
        body,
        name="ffn_up",
        grid=(t // tm, NCHIP),
        in_specs=[pl.BlockSpec((tm, d), lambda i, j: (i, 0)), pl.BlockSpec((1, d), lambda i, j: (0, 0)), wspec(3 * ffn), wspec(3 * ffn + 1)],
        out_specs=[pl.BlockSpec((2, None, tm, f4), lambda i, j: (0, j, i, 0)), pl.BlockSpec((None, tm, f4), lambda i, j: (j, i, 0)),
                   pl.BlockSpec((tm, d), lambda i, j: (i, 0))],
        out_shape=[jax.ShapeDtypeStruct((2, NCHIP, t, f4), BF), jax.ShapeDtypeStruct((NCHIP, t, f4), BF), jax.ShapeDtypeStruct((t, d), BF)],
        compiler_params=_params(("parallel", "arbitrary")),
    )(h, g, wf, wf)


def ffn_down(s4, wf, layer, ffn, h):
    t, d = h.shape
    f4 = s4.shape[2]
    tm = _tile(t, 512)

    def body(s_ref, w_ref, h_ref, o_ref):
        acc = _dot(s_ref[0], w_ref[0])
        for k in range(1, NCHIP):
            acc = acc + _dot(s_ref[k], w_ref[k])
        o_ref[...] = h_ref[...] + 0.5 * acc

    return pl.pallas_call(
        body,
        name="ffn_down",
        grid=(t // tm,),
        in_specs=[pl.BlockSpec((NCHIP, tm, f4), lambda i: (0, i, 0)), pl.BlockSpec((NCHIP, None, f4, d), lambda i: (0, layer, 3 * ffn + 2, 0)),
                  pl.BlockSpec((tm, d), lambda i: (i, 0))],
        out_specs=pl.BlockSpec((tm, d), lambda i: (i, 0)),
        out_shape=jax.ShapeDtypeStruct((t, d), F32),
        compiler_params=_params(("parallel",)),
    )(s4, wf, h)


def ffn_bwd(dho, ab, wf, layer, ffn):
    t, d = dho.shape
    f4 = wf.shape[2] // 6
    tm = _tile(t, 1024)
    sub = _tile(tm, FFN_SUB, 16)

    def body(dho_ref, ab_ref, wg_ref, wu_ref, wd_ref, dn_ref, s_ref, dab_ref, do_sc):
        j = pl.program_id(1)

        @pl.when(j == 0)
        def _():
            do_sc[...] = (0.5 * dho_ref[...]).astype(BF)
            dn_ref[...] = jnp.zeros_like(dn_ref)

        for r in range(tm // sub):
            rows = slice(r * sub, (r + 1) * sub)
            ds = _dot(do_sc[rows, :], wd_ref[...], 1, 1)
            av, bv = ab_ref[0, rows, :].astype(F32), ab_ref[1, rows, :].astype(F32)
            sig = _sigmoid(av)
            sl = av * sig
            s_ref[rows, :] = (sl * bv).astype(BF)
            da = (ds * bv * (sig * (1.0 + av * (1.0 - sig)))).astype(BF)
            db = (ds * sl).astype(BF)
            dab_ref[0, rows, :] = da
            dab_ref[1, rows, :] = db
            dn_ref[rows, :] += _dot(da, wg_ref[...]) + _dot(db, wu_ref[...])

    row = lambda c: pl.BlockSpec((tm, c), lambda i, j: (i, 0))
    wspec = lambda ib: pl.BlockSpec((None, None, f4, d), lambda i, j: (j, layer, ib, 0))
    ab_spec = pl.BlockSpec((2, None, tm, f4), lambda i, j: (0, j, i, 0))
    return pl.pallas_call(
        body,
        name="ffn_bwd",
        grid=(t // tm, NCHIP),
        in_specs=[row(d), ab_spec, wspec(3 * ffn), wspec(3 * ffn + 1), wspec(3 * ffn + 2)],
        out_specs=[row(d), pl.BlockSpec((None, tm, f4), lambda i, j: (j, i, 0)), ab_spec],
        out_shape=[jax.ShapeDtypeStruct((t, d), F32), jax.ShapeDtypeStruct((NCHIP, t, f4), BF), jax.ShapeDtypeStruct((2, NCHIP, t, f4), BF)],
        scratch_shapes=[pltpu.VMEM((tm, d), BF)],
        compiler_params=_params(("parallel", "arbitrary")),
    )(dho, ab, wf, wf, wf)


def ffn_wgrads(n, dho, s4, dab):
    t, d = n.shape
    f4 = s4.shape[2]
    tn = _tile(d, 512)
    g_in = mmx("g_ffn_in", dab, n, grid=(2, NCHIP, d // tn), a_spec=pl.BlockSpec((None, None, t, f4), lambda wh, k, j: (wh, k, 0, 0)),
               b_spec=pl.BlockSpec((t, tn), lambda wh, k, j: (0, j)), o_spec=pl.BlockSpec((None, None, f4, tn), lambda wh, k, j: (k, wh, 0, j)),
               o_shape=(NCHIP, 2, f4, d), o_dtype=BF, ca=0, cb=0)
    g_out = mmx("g_ffn_out", s4, dho, grid=(NCHIP, d // tn), a_spec=pl.BlockSpec((None, t, f4), lambda k, j: (k, 0, 0)),
                b_spec=pl.BlockSpec((t, tn), lambda k, j: (0, j)), o_spec=pl.BlockSpec((None, f4, tn), lambda k, j: (k, 0, j)),
                o_shape=(NCHIP, f4, d), o_dtype=BF, ca=0, cb=0, scale=0.5)
    return jnp.concatenate([g_in.reshape(NCHIP, 2 * f4, d), g_out], axis=1)


def norm_cast(h, g):
    def fn(i, nt, rows, vecs, prevs, nexts):
        return [_rms(rows[0], vecs[0])], []
    return ew(fn, [h], [g], [(h.shape[1], BF)], tm=512, name="norm_cast")[0]


def _zero_if(cond, x):
    return jnp.where(cond, jnp.zeros_like(x), x)


def conv_a_fwd(sc3, w_sc):
    d = sc3.shape[1] // 3

    def fn(i, nt, rows, vecs, prevs, nexts):
        x, pv = rows[0], _zero_if(i == 0, prevs[0])
        v = x[:, d:2 * d] * x[:, 2 * d:]
        vp = pv[:, d:2 * d] * pv[:, 2 * d:]
        return [x[:, :d] * _conv_fwd(v, vp, vecs[0])], []

    return ew(fn, [sc3], [w_sc], [(d, BF)], tm=256, name="conv_a_fwd", prev_halo=(0,))[0]


def _softplus(x):
    e = jnp.exp(-jnp.abs(x))
    return jnp.maximum(x, 0.0) + jnp.where(e < 1e-4, e - 0.5 * e * e, jnp.log(1.0 + e))


def conv_m_fwd(xbc_raw, dt_raw, w_mc, b_mc, dt_bias):
    def fn(i, nt, rows, vecs, prevs, nexts):
        pre = _conv_fwd(rows[0], _zero_if(i == 0, prevs[0]), vecs[0]) + vecs[1]
        return [pre * _sigmoid(pre), _softplus(rows[1] + vecs[2])], []

    return ew(fn, [xbc_raw, dt_raw], [w_mc, b_mc, dt_bias], [(xbc_raw.shape[1], F32), (LANE, F32)], tm=256, name="conv_m_fwd",
              prev_halo=(0,))


def conv_m_bwd1(dxbc, xbc_raw, ddt, dt_raw, w_mc, b_mc, dt_bias):
    def fn(i, nt, rows, vecs, prevs, nexts):
        pre = _conv_fwd(rows[1], _zero_if(i == 0, prevs[0]), vecs[0]) + vecs[1]
        sig = _sigmoid(pre)
        dpre = rows[0] * (sig * (1.0 + pre * (1.0 - sig)))
        ddr = rows[2] * _sigmoid(rows[3] + vecs[2])
        return [dpre, ddr], [jnp.sum(ddr, axis=0, keepdims=True)]

    return ew(fn, [dxbc, xbc_raw, ddt, dt_raw], [w_mc, b_mc, dt_bias], [(dxbc.shape[1], F32), (LANE, BF)], [(1, LANE)], tm=256,
              name="conv_m_bwd1", prev_halo=(1,))


def conv_bwd2(dpre, x, w, name):
    kk = w.shape[0]

    def fn(i, nt, rows, vecs, prevs, nexts):
        dp, xv = rows[0], rows[1]
        nx = _zero_if(i == nt - 1, nexts[0])
        pv = _zero_if(i == 0, prevs[0])
        dx = None
        dws = []
        for k in range(kk):
            term = vecs[0][k:k + 1, :] * _shift_up(dp, nx, kk - 1 - k)
            dx = term if dx is None else dx + term
            dws.append(jnp.sum(dp * _shift_down(xv, pv, kk - 1 - k), axis=0, keepdims=True))
        return [dx], [jnp.concatenate(dws, axis=0), jnp.sum(dp, axis=0, keepdims=True)]

    c = x.shape[1]
    return ew(fn, [dpre, x], [w], [(c, BF)], [(kk, c), (1, c)], tm=256, name=name, prev_halo=(1,), next_halo=(0,))


def conv_a_bwd1(dya, sc3, w_sc):
    d = sc3.shape[1] // 3

    def fn(i, nt, rows, vecs, prevs, nexts):
        x, pv = rows[1], _zero_if(i == 0, prevs[0])
        v = x[:, d:2 * d] * x[:, 2 * d:]
        vp = pv[:, d:2 * d] * pv[:, 2 * d:]
        return [rows[0] * x[:, :d], rows[0] * _conv_fwd(v, vp, vecs[0]), v], []

    return ew(fn, [dya, sc3], [w_sc], [(d, F32), (d, BF), (d, F32)], tm=256, name="conv_a_bwd1", prev_halo=(1,))


def conv_a_bwd2(dcv, v, sc3, w_sc):
    d = v.shape[1]
    kk = w_sc.shape[0]

    def fn(i, nt, rows, vecs, prevs, nexts):
        dp, vv, x = rows
        nx = _zero_if(i == nt - 1, nexts[0])
        pv = _zero_if(i == 0, prevs[0])
        dv = None
        dws = []
        for k in range(kk):
            term = vecs[0][k:k + 1, :] * _shift_up(dp, nx, kk - 1 - k)
            dv = term if dv is None else dv + term
            dws.append(jnp.sum(dp * _shift_down(vv, pv, kk - 1 - k), axis=0, keepdims=True))
        return [dv * x[:, 2 * d:], dv * x[:, d:2 * d]], [jnp.concatenate(dws, axis=0)]

    return ew(fn, [dcv, v, sc3], [w_sc], [(d, BF), (d, BF)], [(kk, d)], tm=256, name="conv_a_bwd2", prev_halo=(1,), next_halo=(0,))


def _xdot(a, b, passes, split_lhs, ca=1, cb=0):
    parts, r = [], (a if split_lhs else b)
    for _ in range(passes):
        piece = r.astype(BF)
        parts.append(piece)
        r = r - piece.astype(F32)
    other = (b if split_lhs else a).astype(BF)
    acc = None
    for piece in parts:
        term = _dot(piece, other, ca, cb) if split_lhs else _dot(other, piece, ca, cb)
        acc = term if acc is None else acc + term
    return acc


def _ssd_common(xbc_ref, dt_ref, alog_ref, e_ref, w):
    ll = SSM_L
    xs = xbc_ref[:, 0:w]
    dtv = dt_ref[...]
    a_row = -jnp.exp(alog_ref[...])
    a = dtv * a_row
    row = lax.broadcasted_iota(jnp.int32, (ll, ll), 0)
    col = lax.broadcasted_iota(jnp.int32, (ll, ll), 1)
    tril = (row >= col).astype(F32)
    triu = (row <= col).astype(F32)
    acl = _xdot(tril, a, 3, False)
    acl_t = _xdot(a, triu, 3, True, 0, 0)
    e = e_ref[...]
    aclx = _xdot(acl, e, 3, True)
    dtx = _xdot(dtv, e, 2, True)
    last = aclx[ll - 1:ll, :]
    e_in = jnp.exp(aclx)
    e_end = jnp.exp(last - aclx)
    e_tot = jnp.exp(last)
    x = xs * dtx
    return dict(xs=xs, dtv=dtv, a_row=a_row, a=a, row=row, col=col, triu=triu, acl=acl, acl_t=acl_t, dtx=dtx, e_in=e_in, e_end=e_end,
                e_tot=e_tot, x=x)


def _decay(q, hh):
    diff = q["acl"][:, hh:hh + 1] - q["acl_t"][hh:hh + 1, :]
    return jnp.exp(jnp.where(q["row"] >= q["col"], diff, -jnp.inf))


def ssd_fwd(xbc, dt, z, a_log, d_exp, m_norm, e_mat):
    t = xbc.shape[0]
    w = z.shape[1]
    gn = SSM_G * SSM_N
    gw = w // SSM_G
    ll, nn = SSM_L, SSM_N
    nc = t // ll
    cw = xbc.shape[1]

    def body(xbc_ref, dt_ref, z_ref, alog_ref, dexp_ref, mn_ref, e_ref, yn_ref, y_ref, sp_ref, s_sc):
        c = pl.program_id(0)

        @pl.when(c == 0)
        def _():
            s_sc[...] = jnp.zeros_like(s_sc)

        q = _ssd_common(xbc_ref, dt_ref, alog_ref, e_ref, w)
        xb = q["x"].astype(BF)
        xsb = (q["x"] * q["e_end"]).astype(BF)
        sp = s_sc[...]
        sp_ref[0] = sp
        spb = sp.astype(BF)
        lane = lax.broadcasted_iota(jnp.int32, (ll, LANE), 1)
        for g in range(SSM_G):
            lo = g * gw
            bg = xbc_ref[:, w + g * nn:w + (g + 1) * nn].astype(BF)
            cg = xbc_ref[:, w + gn + g * nn:w + gn + (g + 1) * nn].astype(BF)
            yoff = _dot(cg, spb[:, lo:lo + gw]) * q["e_in"][:, lo:lo + gw]
            s_sc[:, lo:lo + gw] = sp[:, lo:lo + gw] * q["e_tot"][:, lo:lo + gw] + _dot(bg, xsb[:, lo:lo + gw], 0, 0)
            cb = _dot(cg, bg, 1, 1)
            for pr in range(gw // LANE):
                l0 = lo + pr * LANE
                xp = xb[:, l0:l0 + LANE]
                ys = []
                for hh in (l0 // SSM_P, l0 // SSM_P + 1):
                    wm = (cb * _decay(q, hh)).astype(BF)
                    ys.append(_dot(wm, xp))
                ydiag = jnp.where(lane < SSM_P, ys[0], ys[1])
                y_ref[:, l0:l0 + LANE] = ydiag + yoff[:, pr * LANE:(pr + 1) * LANE] + dexp_ref[:, l0:l0 + LANE] * q["xs"][:, l0:l0 + LANE]
        zv = z_ref[...].astype(F32)
        yz = y_ref[...] * (zv * _sigmoid(zv))
        for g in range(SSM_G):
            lo = g * gw
            yn_ref[:, lo:lo + gw] = _rms(yz[:, lo:lo + gw], mn_ref[:, lo:lo + gw]).astype(BF)

    vec = lambda s: pl.BlockSpec(s, lambda c: (0, 0))
    return pl.pallas_call(
        body,
        name="ssd_fwd",
        grid=(nc,),
        in_specs=[
            pl.BlockSpec((ll, cw), lambda c: (c, 0)), pl.BlockSpec((ll, LANE), lambda c: (c, 0)), pl.BlockSpec((ll, w), lambda c: (c, 0)),
            vec((1, LANE)), vec((1, w)), vec((1, w)), vec((LANE, w)),
        ],
        out_specs=[pl.BlockSpec((ll, w), lambda c: (c, 0)), pl.BlockSpec((ll, w), lambda c: (c, 0)), pl.BlockSpec((1, nn, w), lambda c: (c, 0, 0))],
        out_shape=[jax.ShapeDtypeStruct((t, w), BF), jax.ShapeDtypeStruct((t, w), F32), jax.ShapeDtypeStruct((nc, nn, w), F32)],
        scratch_shapes=[pltpu.VMEM((nn, w), F32)],
        compiler_params=_params(("arbitrary",)),
    )(xbc, dt, z, a_log, d_exp, m_norm, e_mat)


def ssd_bwd(dyn, y, z, xbc, dt, sprev, a_log, d_exp, m_norm, e_mat, et_mat):
    t = xbc.shape[0]
    w = z.shape[1]
    gn = SSM_G * SSM_N
    gw = w // SSM_G
    ll, nn = SSM_L, SSM_N
    nc = t // ll
    cw = xbc.shape[1]

    def body(dyn_ref, y_ref, z_ref, xbc_ref, dt_ref, sp_ref, alog_ref, dexp_ref, mn_ref, e_ref, et_ref,
             dz_ref, dxbc_ref, ddt_ref, dmn_ref, dd_ref, dal_ref, ds_sc, dy_sc, dx_sc):
        step = pl.program_id(0)

        @pl.when(step == 0)
        def _():
            ds_sc[...] = jnp.zeros_like(ds_sc)

        zv, yv = z_ref[...].astype(F32), y_ref[...]
        sg = _sigmoid(zv)
        sz = zv * sg
        yz = yv * sz
        dmn = []
        for g in range(SSM_G):
            lo = g * gw
            dseg, dmn_g = _rms_bwd(yz[:, lo:lo + gw], mn_ref[:, lo:lo + gw], dyn_ref[:, lo:lo + gw])
            dy_sc[:, lo:lo + gw] = dseg
            dmn.append(dmn_g)
        dmn = jnp.concatenate(dmn, axis=1)
        dyz = dy_sc[...]
        dz_ref[...] = (dyz * yv * (sg * (1.0 + zv * (1.0 - sg)))).astype(BF)
        dy = dyz * sz

        q = _ssd_common(xbc_ref, dt_ref, alog_ref, e_ref, w)
        x = q["x"]
        xb = x.astype(BF)
        xsb = (x * q["e_end"]).astype(BF)
        sp = sp_ref[0]
        spb = sp.astype(BF)
        dsn = ds_sc[...]
        dsnb = dsn.astype(BF)
        dyb = dy.astype(BF)
        lane = lax.broadcasted_iota(jnp.int32, (ll, LANE), 1)
        lane1 = lax.broadcasted_iota(jnp.int32, (1, LANE), 1)
        sub1 = lax.broadcasted_iota(jnp.int32, (LANE, 1), 0)
        dacl = jnp.zeros((ll, LANE), F32)
        dacl_t = jnp.zeros((LANE, ll), F32)
        d_ein, d_eend, d_etot = [], [], []
        for g in range(SSM_G):
            lo = g * gw
            sl = slice(lo, lo + gw)
            bg = xbc_ref[:, w + g * nn:w + (g + 1) * nn].astype(BF)
            cg = xbc_ref[:, w + gn + g * nn:w + gn + (g + 1) * nn].astype(BF)
            zg = _dot(cg, spb[:, sl])
            dzz = (dy[:, sl] * q["e_in"][:, sl]).astype(BF)
            d_ein.append(dy[:, sl] * zg)
            dcg = _dot(dzz, spb[:, sl], 1, 1)
            ds_sc[:, sl] = _dot(cg, dzz, 0, 0) + dsn[:, sl] * q["e_tot"][:, sl]
            d_etot.append(jnp.sum(dsn[:, sl] * sp[:, sl], axis=0, keepdims=True))
            dbg = _dot(xsb[:, sl], dsnb[:, sl], 1, 1)
            dxs_g = _dot(bg, dsnb[:, sl])
            d_eend.append(dxs_g * x[:, sl])
            cb = _dot(cg, bg, 1, 1)
            dcb = jnp.zeros((ll, ll), F32)
            for pr in range(gw // LANE):
                l0 = lo + pr * LANE
                xp = xb[:, l0:l0 + LANE]
                dyp = dyb[:, l0:l0 + LANE]
                dxp = []
                for hi, hh in enumerate((l0 // SSM_P, l0 // SSM_P + 1)):
                    lm = _decay(q, hh)
                    wm = (cb * lm).astype(BF)
                    in_head = (lane < SSM_P) if hi == 0 else (lane >= SSM_P)
                    dwm = _dot(jnp.where(in_head, dyp, jnp.zeros_like(dyp)), xp, 1, 1)
                    dxp.append(_dot(wm, dyp, 0, 0))
                    dlm = dwm * lm
                    dcb = dcb + dlm
                    dd = dlm * cb
                    dacl = dacl + jnp.sum(dd, axis=1, keepdims=True) * (lane1 == hh).astype(F32)
                    dacl_t = dacl_t + (sub1 == hh).astype(F32) * jnp.sum(dd, axis=0, keepdims=True)
                dx_sc[:, l0:l0 + LANE] = jnp.where(lane < SSM_P, dxp[0], dxp[1]) + dxs_g[:, pr * LANE:(pr + 1) * LANE] * q["e_end"][:, l0:l0 + LANE]
            dcbb = dcb.astype(BF)
            dxbc_ref[:, w + g * nn:w + (g + 1) * nn] = dbg + _dot(dcbb, cg, 0, 0)
            dxbc_ref[:, w + gn + g * nn:w + gn + (g + 1) * nn] = dcg + _dot(dcbb, bg)
        d_ein = jnp.concatenate(d_ein, axis=1) * q["e_in"]
        d_eend = jnp.concatenate(d_eend, axis=1) * q["e_end"]
        d_etot = jnp.concatenate(d_etot, axis=1) * q["e_tot"]
        et = et_ref[...]
        last_add = jnp.sum(d_eend, axis=0, keepdims=True) + d_etot
        last_add = _xdot(jnp.broadcast_to(last_add, (HALO, w)), et, 2, True)[0:1]
        row1 = lax.broadcasted_iota(jnp.int32, (ll, LANE), 0)
        dacl = dacl + _xdot(d_ein - d_eend, et, 2, True) + jnp.where(row1 == ll - 1, last_add, 0.0)
        da = _xdot(q["triu"], dacl, 2, False) - _xdot(q["triu"], dacl_t, 2, False, 1, 1)
        dxv = dx_sc[...]
        dxbc_ref[:, 0:w] = dexp_ref[...] * dy + dxv * q["dtx"]
        ddt_ref[...] = _xdot(dxv * q["xs"], et, 2, True) + da * q["a_row"]
        dal = jnp.sum(da * q["dtv"], axis=0, keepdims=True) * q["a_row"]
        ddv = jnp.sum(dy * q["xs"], axis=0, keepdims=True)
        ddv = _xdot(jnp.broadcast_to(ddv, (HALO, w)), et, 2, True)[0:1]
        _accumulate(dmn_ref, dmn, step == 0)
        _accumulate(dd_ref, ddv, step == 0)
        _accumulate(dal_ref, dal, step == 0)

    rev = lambda c_: pl.BlockSpec((ll, c_), lambda s: (nc - 1 - s, 0))
    vec = lambda s_: pl.BlockSpec(s_, lambda s: (0, 0))
    return pl.pallas_call(
        body,
        name="ssd_bwd",
        grid=(nc,),
        in_specs=[
            rev(w), rev(w), rev(w), rev(cw), rev(LANE), pl.BlockSpec((1, nn, w), lambda s: (nc - 1 - s, 0, 0)),
            vec((1, LANE)), vec((1, w)), vec((1, w)), vec((LANE, w)), vec((w, LANE)),
        ],
        out_specs=[rev(w), rev(cw), rev(LANE), vec((1, w)), vec((1, LANE)), vec((1, LANE))],
        out_shape=[
            jax.ShapeDtypeStruct((t, w), BF), jax.ShapeDtypeStruct((t, cw), F32), jax.ShapeDtypeStruct((t, LANE), F32),
            jax.ShapeDtypeStruct((1, w), F32), jax.ShapeDtypeStruct((1, LANE), F32), jax.ShapeDtypeStruct((1, LANE), F32),
        ],
        scratch_shapes=[pltpu.VMEM((nn, w), F32), pltpu.VMEM((ll, w), F32), pltpu.VMEM((ll, w), F32)],
        compiler_params=_params(("arbitrary",)),
    )(dyn, y, z, xbc, dt, sprev, a_log, d_exp, m_norm, e_mat, et_mat)


def _w1024_spec(d, layer, nblk, iblk):
    r = nblk * (d // NCHIP)
    return pl.BlockSpec((NCHIP, None, r, d), lambda i: (0, layer, iblk // nblk, 0))


def _whole(ref):
    v = ref[...]
    return v.reshape(v.shape[0] * v.shape[1], v.shape[2])


def mix_out_fwd(ya_in, yn, gates, h, w1024, layer):
    t, d = h.shape
    tm = _tile(t, 256)

    def body(ya_ref, yn_ref, g_ref, h_ref, wm_ref, wa_ref, wo_ref, ho_ref, oa_ref, om_ref, mg_ref):
        y_a = _dot(ya_ref[...], _whole(wa_ref))
        y_m = _dot(yn_ref[...], _whole(wm_ref))
        oa_ref[...] = y_a
        om_ref[...] = y_m
        gv = g_ref[...].astype(F32)
        mg = (_sigmoid(gv[:, :d]) * y_a + _sigmoid(gv[:, d:]) * y_m).astype(BF)
        mg_ref[...] = mg
        ho_ref[...] = h_ref[...] + _dot(mg, _whole(wo_ref))

    row = lambda c: pl.BlockSpec((tm, c), lambda i: (i, 0))
    return pl.pallas_call(
        body,
        name="mix_out_fwd",
        grid=(t // tm,),
        in_specs=[row(d), row(2 * d), row(2 * d), row(d), _w1024_spec(d, layer, 2, 0), _w1024_spec(d, layer, 1, 2), _w1024_spec(d, layer, 1, 3)],
        out_specs=[row(d), row(d), row(d), row(d)],
        out_shape=[jax.ShapeDtypeStruct((t, d), F32), jax.ShapeDtypeStruct((t, d), F32), jax.ShapeDtypeStruct((t, d), F32),
                   jax.ShapeDtypeStruct((t, d), BF)],
        compiler_params=_params(("parallel",)),
    )(ya_in, yn, gates, h, w1024, w1024, w1024)


def mix_out_bwd(dh, gates, y_a, y_m, w1024, layer):
    t, d = dh.shape
    tm = _tile(t, 256)

    def body(dh_ref, g_ref, ya_ref, ym_ref, wm_ref, wa_ref, wo_ref, dg_ref, dya_ref, dyn_ref, da_ref, dm_ref):
        dmg = _dot(dh_ref[...].astype(BF), _whole(wo_ref), 1, 1)
        gv = g_ref[...].astype(F32)
        sa, sm = _sigmoid(gv[:, :d]), _sigmoid(gv[:, d:])
        dg_ref[:, :d] = (dmg * ya_ref[...] * sa * (1.0 - sa)).astype(BF)
        dg_ref[:, d:] = (dmg * ym_ref[...] * sm * (1.0 - sm)).astype(BF)
        da = (dmg * sa).astype(BF)
        dm = (dmg * sm).astype(BF)
        da_ref[...] = da
        dm_ref[...] = dm
        dya_ref[...] = _dot(da, _whole(wa_ref), 1, 1)
        dyn_ref[...] = _dot(dm, _whole(wm_ref), 1, 1)

    row = lambda c: pl.BlockSpec((tm, c), lambda i: (i, 0))
    return pl.pallas_call(
        body,
        name="mix_out_bwd",
        grid=(t // tm,),
        in_specs=[row(d), row(2 * d), row(d), row(d), _w1024_spec(d, layer, 2, 0), _w1024_spec(d, layer, 1, 2), _w1024_spec(d, layer, 1, 3)],
        out_specs=[row(2 * d), row(d), row(2 * d), row(d), row(d)],
        out_shape=[jax.ShapeDtypeStruct((t, 2 * d), BF), jax.ShapeDtypeStruct((t, d), F32), jax.ShapeDtypeStruct((t, 2 * d), F32),
                   jax.ShapeDtypeStruct((t, d), BF), jax.ShapeDtypeStruct((t, d), BF)],
        compiler_params=_params(("parallel",)),
    )(dh, gates, y_a, y_m, w1024, w1024, w1024)


def norm_bwd_add(dh, h, g, dn):
    def fn(i, nt, rows, vecs, prevs, nexts):
        dx, dg = _rms_bwd(rows[1], vecs[0], rows[2])
        return [rows[0] + dx], [dg]
    d = h.shape[1]
    return ew(fn, [dh, h, dn], [g], [(d, F32)], [(1, d)], tm=512, name="norm_bwd_add")


def _pe(p, wpp_ref):
    pb = p.astype(BF)
    return jnp.concatenate([_dot(pb, wpp_ref[k]) for k in range(NCHIP)], axis=1)


def ple_fwd(h, g, p, w1024, wpp, layer):
    t, d = h.shape
    tm = _tile(t, 512)

    def body(h_ref, g_ref, p_ref, wg_ref, wp_ref, ho_ref):
        hv = h_ref[...]
        gate = _sigmoid(_dot(_rms(hv, g_ref[...]).astype(BF), _whole(wg_ref)))
        ho_ref[...] = hv + gate * _pe(p_ref[...], wp_ref)

    row = lambda c: pl.BlockSpec((tm, c), lambda i: (i, 0))
    wpp_spec = pl.BlockSpec((NCHIP, None) + wpp.shape[2:], lambda i: (0, layer, 0, 0))
    return pl.pallas_call(
        body,
        name="ple_fwd",
        grid=(t // tm,),
        in_specs=[row(d), pl.BlockSpec((1, d), lambda i: (0, 0)), row(p.shape[1]), _w1024_spec(d, layer, 1, 4), wpp_spec],
        out_specs=row(d),
        out_shape=jax.ShapeDtypeStruct((t, d), F32),
        compiler_params=_params(("parallel",)),
    )(h, g, p, w1024, wpp)


def ple_bwd(dho, h, g, p, w1024, wpp, layer):
    t, d = h.shape
    tm = _tile(t, 512)

    def body(dho_ref, h_ref, g_ref, p_ref, wg_ref, wp_ref, dh_ref, dg_ref, n_ref, dgp_ref, dpe_ref):
        hv, dv = h_ref[...], dho_ref[...]
        n = _rms(hv, g_ref[...]).astype(BF)
        n_ref[...] = n
        wg = _whole(wg_ref)
        gate = _sigmoid(_dot(n, wg))
        pe = _pe(p_ref[...], wp_ref)
        dpe_ref[...] = (dv * gate).astype(BF)
        dgp = (dv * pe * gate * (1.0 - gate)).astype(BF)
        dgp_ref[...] = dgp
        dx, dg = _rms_bwd(hv, g_ref[...], _dot(dgp, wg, 1, 1))
        dh_ref[...] = dv + dx
        _accumulate(dg_ref, dg, pl.program_id(0) == 0)

    row = lambda c: pl.BlockSpec((tm, c), lambda i: (i, 0))
    wpp_spec = pl.BlockSpec((NCHIP, None) + wpp.shape[2:], lambda i: (0, layer, 0, 0))
    return pl.pallas_call(
        body,
        name="ple_bwd",
        grid=(t // tm,),
        in_specs=[row(d), row(d), pl.BlockSpec((1, d), lambda i: (0, 0)), row(p.shape[1]), _w1024_spec(d, layer, 1, 4), wpp_spec],
        out_specs=[row(d), pl.BlockSpec((1, d), lambda i: (0, 0)), row(d), row(d), row(d)],
        out_shape=[jax.ShapeDtypeStruct((t, d), F32), jax.ShapeDtypeStruct((1, d), F32), jax.ShapeDtypeStruct((t, d), BF),
                   jax.ShapeDtypeStruct((t, d), BF), jax.ShapeDtypeStruct((t, d), BF)],
        compiler_params=_params(("arbitrary",)),
    )(dho, h, g, p, w1024, wpp)


def loss_bwd(h, g, target):
    d = h.shape[1]

    def fn(i, nt, rows, vecs, prevs, nexts):
        err = _rms(rows[0], vecs[0]) - rows[1]
        dx, dg = _rms_bwd(rows[0], vecs[0], err * (1.0 / d))
        return [dx], [jnp.sum(err * err, axis=0, keepdims=True) * (0.5 / d), dg]

    return ew(fn, [h, target], [g], [(d, F32)], [(1, d), (1, d)], tm=512, name="loss_bwd")


def adamw(w, g, m, v, name):
    c1, c2 = 1.0 / (1.0 - ADAM_B1 ** ADAM_STEP), 1.0 / (1.0 - ADAM_B2 ** ADAM_STEP)

    def fn(i, nt, rows, vecs, prevs, nexts):
        wv, gv, mv, vv = rows
        mn = ADAM_B1 * mv + (1.0 - ADAM_B1) * gv
        vn = ADAM_B2 * vv + (1.0 - ADAM_B2) * (gv * gv)
        delta = -ADAM_LR * ((mn * c1) / (jnp.sqrt(vn * c2) + ADAM_EPS) + ADAM_WD * wv)
        return [delta, mn, vn], []

    c = w.shape[1]
    return ew(fn, [w, g, m, v], [], [(c, F32)] * 3, tm=_row_tile(w.shape[0], c, HALO), name=name)


def _place():
    return lax.axis_index("x"), lax.axis_index("y"), lax.axis_index("c")


def _other_chips(x, y):
    return [(1 - x, y), (x, 1 - y), (1 - x, 1 - y)]


ANY = pl.BlockSpec(memory_space=pl.ANY)


def _comm_call(body, name, ins, out_shapes, n_sems, aliases=None):
    return pl.pallas_call(
        body,
        name=name,
        in_specs=[ANY] * len(ins),
        out_specs=[ANY] * len(out_shapes),
        out_shape=out_shapes,
        scratch_shapes=[pltpu.SemaphoreType.DMA((n_sems,)), pltpu.SemaphoreType.DMA((n_sems,))],
        input_output_aliases=aliases or {},
    )(*ins)


def gather_packs(packs, layer_major, name):
    nt = len(packs)
    nl = packs[0].shape[0]
    hl = nl // 2

    def body(*refs):
        ins, outs, (send_sems, recv_sems) = refs[:nt], refs[nt:2 * nt], refs[2 * nt:]
        x, y, cc = _place()
        chips = _other_chips(x, y)
        sibling = (x, y, 1 - cc)
        mine, theirs, whole = pl.ds(cc * hl, hl), pl.ds((1 - cc) * hl, hl), pl.ds(0, nl)
        k_me = 2 * x + y

        def block(ti, k, layers):
            return outs[ti].at[layers, k] if layer_major[ti] else outs[ti].at[k, layers]

        def copy(k, src, dst, to):
            return pltpu.make_async_remote_copy(src_ref=src, dst_ref=dst, send_sem=send_sems.at[k], recv_sem=recv_sems.at[k],
                                                device_id=to, device_id_type=MESH)

        started = []
        for ti in range(nt):
            for j, chip in enumerate(chips):
                started.append(copy(7 * ti + j, ins[ti].at[mine], block(ti, k_me, mine), (*chip, cc)))
            started.append(copy(7 * ti + 6, ins[ti], block(ti, k_me, whole), sibling))
        for cp in started:
            cp.start()
        for ti in range(nt):
            for j, (px, py) in enumerate(chips):
                landed = block(ti, 2 * px + py, mine)
                copy(7 * ti + j, landed, landed, (px, py, cc)).wait_recv()
                fw = copy(7 * ti + 3 + j, landed, landed, sibling)
                fw.start()
                started.append(fw)
        for ti in range(nt):
            for j, (px, py) in enumerate(chips):
                landed = block(ti, 2 * px + py, theirs)
                copy(7 * ti + 3 + j, landed, landed, sibling).wait_recv()
            own = block(ti, k_me, whole)
            copy(7 * ti + 6, own, own, sibling).wait_recv()
        for cp in started:
            cp.wait_send()

    shapes = [jax.ShapeDtypeStruct((nl, NCHIP) + p.shape[1:] if lm else (NCHIP,) + p.shape, p.dtype) for p, lm in zip(packs, layer_major, strict=True)]
    return _comm_call(body, name, packs, shapes, 7 * nt)


def swap_packs(gs, name):
    nt = len(gs)
    hl = gs[0].shape[1] // 2

    def body(*refs):
        ins, outs, (send_sems, recv_sems) = refs[:nt], refs[nt:2 * nt], refs[2 * nt:]
        x, y, cc = _place()
        theirs = pl.ds((1 - cc) * hl, hl)
        cps = [pltpu.make_async_remote_copy(src_ref=ins[ti].at[:, theirs], dst_ref=outs[ti], send_sem=send_sems.at[ti], recv_sem=recv_sems.at[ti],
                                            device_id=(x, y, 1 - cc), device_id_type=MESH) for ti in range(nt)]
        for cp in cps:
            cp.start()
        for cp in cps:
            cp.wait()

    return _comm_call(body, name, gs, [jax.ShapeDtypeStruct((NCHIP, hl) + g.shape[2:], g.dtype) for g in gs], nt)


def scatter_packs(cs, name):
    nt = len(cs)

    def body(*refs):
        ins, outs, (send_sems, recv_sems) = refs[:nt], refs[nt:2 * nt], refs[2 * nt:]
        x, y, cc = _place()
        cps = []
        for ti in range(nt):
            for j, (px, py) in enumerate(_other_chips(x, y)):
                cps.append(pltpu.make_async_remote_copy(src_ref=ins[ti].at[2 * px + py], dst_ref=outs[ti].at[j], send_sem=send_sems.at[3 * ti + j],
                                                        recv_sem=recv_sems.at[3 * ti + j], device_id=(px, py, cc), device_id_type=MESH))
        for cp in cps:
            cp.start()
        for cp in cps:
            cp.wait()

    return _comm_call(body, name, cs, [jax.ShapeDtypeStruct((3,) + c_.shape[1:], c_.dtype) for c_ in cs], 3 * nt)


def join_packs(fulls, name):
    nt = len(fulls)
    hl = fulls[0].shape[0] // 2

    def body(*refs):
        ins, outs, (send_sems, recv_sems) = refs[:nt], refs[nt:2 * nt], refs[2 * nt:]
        x, y, cc = _place()
        mine = pl.ds(cc * hl, hl)
        cps = [pltpu.make_async_remote_copy(src_ref=ins[ti].at[mine], dst_ref=outs[ti].at[mine], send_sem=send_sems.at[ti], recv_sem=recv_sems.at[ti],
                                            device_id=(x, y, 1 - cc), device_id_type=MESH) for ti in range(nt)]
        for cp in cps:
            cp.start()
        for cp in cps:
            cp.wait()

    return _comm_call(body, name, fulls, [jax.ShapeDtypeStruct(f.shape, f.dtype) for f in fulls], nt, aliases={ti: ti for ti in range(nt)})


def add_sibling(g, recv, name):
    _, nl, r, c = g.shape
    hl = nl // 2
    tm, tc = _tile2(r, c)

    def body(g_ref, r_ref, o_ref):
        o_ref[...] = (g_ref[...].astype(F32) + r_ref[...].astype(F32)).astype(o_ref.dtype)

    blk = (None, None, tm, tc)
    return pl.pallas_call(
        body,
        name=name,
        grid=(NCHIP, hl, r // tm, c // tc),
        in_specs=[pl.BlockSpec(blk, lambda k, l, i, j: (k, lax.axis_index("c") * hl + l, i, j)), pl.BlockSpec(blk, lambda k, l, i, j: (k, l, i, j))],
        out_specs=pl.BlockSpec(blk, lambda k, l, i, j: (k, l, i, j)),
        out_shape=jax.ShapeDtypeStruct(recv.shape, BF),
        compiler_params=_params(("parallel",) * 4),
    )(g, recv)


def add_chips(cs, got, nl, name):
    _, hl, r, c = cs.shape
    tm, tc = _tile2(r, c)

    def body(own_ref, got_ref, o_ref):
        o_ref[...] = own_ref[...].astype(F32) + got_ref[0].astype(F32) + got_ref[1].astype(F32) + got_ref[2].astype(F32)

    return pl.pallas_call(
        body,
        name=name,
        grid=(hl, r // tm, c // tc),
        in_specs=[pl.BlockSpec((None, None, tm, tc), lambda l, i, j: (2 * lax.axis_index("x") + lax.axis_index("y"), l, i, j)),
                  pl.BlockSpec((3, None, tm, tc), lambda l, i, j: (0, l, i, j))],
        out_specs=pl.BlockSpec((None, tm, tc), lambda l, i, j: (lax.axis_index("c") * hl + l, i, j)),
        out_shape=jax.ShapeDtypeStruct((nl, r, c), F32),
        compiler_params=_params(("parallel",) * 3),
    )(cs, got)


def all_gather_xy(shard, name):
    r, c = shard.shape
    hr = r // 2
    assert r % 32 == 0

    def body(x_ref, out_ref, send_sems, recv_sems, local_sem):
        x, y, cc = _place()
        chips = _other_chips(x, y)
        mine = pl.ds(pl.multiple_of(cc * hr, 16), hr)
        theirs = pl.ds(pl.multiple_of((1 - cc) * hr, 16), hr)
        k_me = 2 * x + y

        def copy(k, src, dst, to):
            return pltpu.make_async_remote_copy(src_ref=src, dst_ref=dst, send_sem=send_sems.at[k], recv_sem=recv_sems.at[k],
                                                device_id=to, device_id_type=MESH)

        own = pltpu.make_async_copy(x_ref, out_ref.at[k_me], local_sem)
        own.start()
        first = [copy(j, x_ref.at[mine], out_ref.at[k_me, mine], (*chip, cc)) for j, chip in enumerate(chips)]
        for cp in first:
            cp.start()
        passed = []
        for j, (px, py) in enumerate(chips):
            landed = out_ref.at[2 * px + py, mine]
            copy(j, landed, landed, (px, py, cc)).wait_recv()
            fw = copy(3 + j, landed, landed, (x, y, 1 - cc))
            fw.start()
            passed.append(fw)
        for j, (px, py) in enumerate(chips):
            landed = out_ref.at[2 * px + py, theirs]
            copy(3 + j, landed, landed, (x, y, 1 - cc)).wait_recv()
        for cp in first + passed:
            cp.wait_send()
        own.wait()

    return pl.pallas_call(
        body,
        name=name,
        in_specs=[ANY],
        out_specs=ANY,
        out_shape=jax.ShapeDtypeStruct((NCHIP, r, c), shard.dtype),
        scratch_shapes=[pltpu.SemaphoreType.DMA((6,)), pltpu.SemaphoreType.DMA((6,)), pltpu.SemaphoreType.DMA],
    )(shard)


def all_gather_8(block, name):
    m, c = block.shape

    def body(x_ref, out_ref, send_sems, recv_sems, local_sem):
        x, y, cc = _place()
        me, sibling = (x, y, cc), (x, y, 1 - cc)
        chips = _other_chips(x, y)

        def rows(px, py, pc):
            return out_ref.at[4 * px + 2 * py + pc]

        def copy(k, blk, to, src=None):
            return pltpu.make_async_remote_copy(src_ref=rows(*blk) if src is None else src, dst_ref=rows(*blk), send_sem=send_sems.at[k],
                                                recv_sem=recv_sems.at[k], device_id=to, device_id_type=MESH)

        mine = pltpu.make_async_copy(x_ref, rows(*me), local_sem)
        mine.start()
        first = [copy(0, me, sibling, src=x_ref)]
        first += [copy(1 + j, me, (*chip, cc), src=x_ref) for j, chip in enumerate(chips)]
        for cp in first:
            cp.start()
        passed = [copy(4 + j, (*chip, cc), sibling) for j, chip in enumerate(chips)]
        for j, chip in enumerate(chips):
            copy(1 + j, (*chip, cc), me).wait_recv()
            passed[j].start()
        copy(0, sibling, me).wait_recv()
        for j, chip in enumerate(chips):
            copy(4 + j, (*chip, 1 - cc), me).wait_recv()
        for cp in first + passed:
            cp.wait_send()
        mine.wait()

    return pl.pallas_call(
        body,
        name=name,
        in_specs=[pl.BlockSpec(memory_space=pltpu.VMEM)],
        out_specs=pl.BlockSpec(memory_space=pltpu.VMEM),
        out_shape=jax.ShapeDtypeStruct((8, m, c), block.dtype),
        scratch_shapes=[pltpu.SemaphoreType.DMA((7,)), pltpu.SemaphoreType.DMA((7,)), pltpu.SemaphoreType.DMA],
        compiler_params=pltpu.CompilerParams(vmem_limit_bytes=VMEM_LIMIT),
    )(block)


def add_parts(parts, out_dtype, name, tm=512):
    def fn(i, nt, rows, vecs, prevs, nexts):
        acc = rows[0]
        for r_ in rows[1:]:
            acc = acc + r_
        return [acc], []
    r, c = parts[0].shape
    return ew(fn, list(parts), [], [(c, out_dtype)], tm=_tile(r, tm, 16), name=name)[0]


SMALL_SHARDED = ("sc_conv_w", "m_conv_w")
SMALL_REPL = ("ffn1_norm", "mix_norm", "m_conv_b", "m_dt_bias", "m_A_log", "m_D", "m_norm", "ffn2_norm", "ple_norm", "final_norm")
BIG = ("ffn1_wg", "ffn1_wu", "ffn1_wd", "w_in", "sc_w_out", "m_w_out", "w_o", "ffn2_wg", "ffn2_wu", "ffn2_wd", "ple_w_gate", "ple_w_proj")
TRANSPOSED = ("ffn1_wg", "ffn1_wu", "ffn2_wg", "ffn2_wu", "w_in")
ORDER = ("ffn1_norm", "ffn1_wg", "ffn1_wu", "ffn1_wd", "mix_norm", "w_in", "sc_conv_w", "sc_w_out", "m_conv_w", "m_conv_b", "m_dt_bias",
         "m_A_log", "m_D", "m_norm", "m_w_out", "w_o", "ffn2_norm", "ffn2_wg", "ffn2_wu", "ffn2_wd", "ple_norm", "ple_w_gate", "ple_w_proj",
         "final_norm")


def _pack(arrs, cols, row_mult):
    flat = jnp.concatenate([a.reshape(-1) for a in arrs])
    n = flat.shape[0]
    rows = -(-n // cols)
    rows = -(-rows // row_mult) * row_mult
    return jnp.pad(flat, (0, rows * cols - n)).reshape(rows, cols)


def _unpack(flat2d, shapes):
    flat = flat2d.reshape(-1)
    out, off = [], 0
    for s in shapes:
        n = int(np.prod(s))
        out.append(flat[off:off + n].reshape(s))
        off += n
    return out


def _row_cat(arrs, dtype):
    return jnp.concatenate([a.astype(dtype) for a in arrs], axis=1)


def kernel(x, p, ffn1_norm, ffn1_wg, ffn1_wu, ffn1_wd, mix_norm, w_in, sc_conv_w, sc_w_out, m_conv_w, m_conv_b, m_dt_bias, m_A_log, m_D, m_norm, m_w_out, w_o, ffn2_norm, ffn2_wg, ffn2_wu, ffn2_wd, ple_norm, ple_w_gate, ple_w_proj, final_norm, loss_target, m_ffn1_norm, m_ffn1_wg, m_ffn1_wu, m_ffn1_wd, m_mix_norm, m_w_in, m_sc_conv_w, m_sc_w_out, m_m_conv_w, m_m_conv_b, m_m_dt_bias, m_m_A_log, m_m_D, m_m_norm, m_m_w_out, m_w_o, m_ffn2_norm, m_ffn2_wg, m_ffn2_wu, m_ffn2_wd, m_ple_norm, m_ple_w_gate, m_ple_w_proj, m_final_norm, v_ffn1_norm, v_ffn1_wg, v_ffn1_wu, v_ffn1_wd, v_mix_norm, v_w_in, v_sc_conv_w, v_sc_w_out, v_m_conv_w, v_m_conv_b, v_m_dt_bias, v_m_A_log, v_m_D, v_m_norm, v_m_w_out, v_w_o, v_ffn2_norm, v_ffn2_wg, v_ffn2_wu, v_ffn2_wd, v_ple_norm, v_ple_w_gate, v_ple_w_proj, v_final_norm):
    args = dict(locals())
    wts = {n: args[n] for n in ORDER}
    mom = {n: args["m_" + n] for n in ORDER}
    vel = {n: args["v_" + n] for n in ORDER}

    depth = ffn1_norm.shape[0]
    d = x.shape[-1]
    w = 2 * d
    hh = w // SSM_P
    cw = w + 2 * SSM_G * SSM_N
    d4 = d // NCHIP
    my_x, my_y, my_c = _place()
    k_me = 2 * my_x + my_y

    tr = lambda a: jnp.swapaxes(a, 1, 2)
    packs = [
        _row_cat([tr(ffn1_wg), tr(ffn1_wu), ffn1_wd, tr(ffn2_wg), tr(ffn2_wu), ffn2_wd], BF),
        _row_cat([m_w_out, sc_w_out, w_o, ple_w_gate], BF),
        tr(w_in).astype(BF),
        ple_w_proj.astype(BF),
    ]
    wf, w1024, win4, wpp = gather_packs(packs, (False, False, True, False), "gather_weights")
    small_local = [sc_conv_w, m_conv_w]
    gathered_s = all_gather_xy(_pack(small_local, LANE, 32), "gather_conv_weights")
    per_shard_s = [_unpack(gathered_s[k], [a.shape for a in small_local]) for k in range(NCHIP)]
    sc_conv_full = jnp.concatenate([per_shard_s[k][0] for k in range(NCHIP)], axis=2)
    m_conv_full = jnp.concatenate([per_shard_s[k][1] for k in range(NCHIP)], axis=2)

    wi = win4.reshape(depth, -1, d)
    o_z, o_xbc, o_dt, o_g = 3 * d, 5 * d, 5 * d + cw, 5 * d + cw + hh
    w_sc3, w_z, w_xbc = wi[:, :o_z], wi[:, o_z:o_xbc], wi[:, o_xbc:o_dt]
    w_g2 = wi[:, o_g:o_g + 2 * d]
    w_dt = jnp.pad(wi[:, o_dt:o_g], ((0, 0), (0, LANE - hh), (0, 0)))
    w_in_p = jnp.concatenate([w_sc3, w_z, w_xbc, w_g2, w_dt], axis=1)

    pad_h = lambda a: jnp.pad(a, ((0, 0), (0, LANE - hh)))
    dt_bias_p, a_log_p = pad_h(m_dt_bias), pad_h(m_A_log)
    d_exp = jnp.repeat(m_D, SSM_P, axis=1)
    e_mat = (jnp.arange(w)[None, :] // SSM_P == jnp.arange(LANE)[:, None]).astype(F32)
    et_mat = e_mat.T

    h = x[0]
    saved = []
    for i in range(depth):
        s = {}
        s["h0"] = h
        s["ab1"], s4, s["n1"] = ffn_up(h, ffn1_norm[i:i + 1], wf, i, 0)
        h = ffn_down(s4, wf, i, 0, h)
        s["h1"] = h
        u = norm_cast(h, mix_norm[i:i + 1])
        s["u"] = u
        s["sc3"] = mm(u, w_sc3[i], tb=True, out_dtype=BF, name="proj_sc")
        s["z"] = mm(u, w_z[i], tb=True, out_dtype=BF, name="proj_z")
        s["xbc_raw"] = mm(u, w_xbc[i], tb=True, out_dtype=BF, name="proj_xbc")
        s["gates"] = mm(u, w_g2[i], tb=True, out_dtype=BF, name="proj_gates")
        s["dt_raw"] = mm(u, w_dt[i], tb=True, name="proj_dt")
        s["ya_in"] = conv_a_fwd(s["sc3"], sc_conv_full[i])
        s["xbc"], s["dt"] = conv_m_fwd(s["xbc_raw"], s["dt_raw"], m_conv_full[i], m_conv_b[i:i + 1], dt_bias_p[i:i + 1])
        s["yn"], s["y"], s["sprev"] = ssd_fwd(s["xbc"], s["dt"], s["z"], a_log_p[i:i + 1], d_exp[i:i + 1], m_norm[i:i + 1], e_mat)
        h, s["y_a"], s["y_m"], s["merged"] = mix_out_fwd(s["ya_in"], s["yn"], s["gates"], h, w1024, i)
        s["h2"] = h
        s["ab2"], s4, s["n2"] = ffn_up(h, ffn2_norm[i:i + 1], wf, i, 1)
        h = ffn_down(s4, wf, i, 1, h)
        s["h3"] = h
        h = ple_fwd(h, ple_norm[i:i + 1], p[i, 0], w1024, wpp, i)
        saved.append(s)

    dh, loss_lanes, g_final = loss_bwd(h, final_norm[None, :], loss_target[0])
    loss = lax.psum(jnp.sum(loss_lanes), ("x", "y", "c"))

    gf, g1024, gin, gpp = ([None] * depth for _ in range(4))
    gs = {n: [None] * depth for n in SMALL_SHARDED + SMALL_REPL if n != "final_norm"}
    for i in reversed(range(depth)):
        s = saved[i]
        dh, gs["ple_norm"][i], n3, dgp, dpe = ple_bwd(dh, s["h3"], ple_norm[i:i + 1], p[i, 0], w1024, wpp, i)
        g_pg = mm(n3, dgp, ta=True, out_dtype=BF, name="g_ple_gate", tm_cap=512, tn_cap=512)
        g_pp = mm(p[i, 0], dpe, ta=True, out_dtype=BF, name="g_ple_proj", tm_cap=512, tn_cap=512)
        gpp[i] = jnp.transpose(g_pp.reshape(g_pp.shape[0], NCHIP, d4), (1, 0, 2))
        dn2, s2, dab2 = ffn_bwd(dh, s["ab2"], wf, i, 1)
        g_ffn2 = ffn_wgrads(s["n2"], dh, s2, dab2)
        dh, gs["ffn2_norm"][i] = norm_bwd_add(dh, s["h2"], ffn2_norm[i:i + 1], dn2)
        dgates, dya, dyn, dy_a, dy_m = mix_out_bwd(dh, s["gates"], s["y_a"], s["y_m"], w1024, i)
        g_wo = mm(s["merged"], dh, ta=True, out_dtype=BF, name="g_w_o", tm_cap=512, tn_cap=512)
        g_sco = mm(s["ya_in"], dy_a, ta=True, out_dtype=BF, name="g_sc_out", tm_cap=512, tn_cap=512)
        g_mo = mm(s["yn"], dy_m, ta=True, out_dtype=BF, name="g_m_out", tm_cap=512, tn_cap=512)
        g1024[i] = jnp.concatenate([g_mo.reshape(NCHIP, 2 * d4, d), g_sco.reshape(NCHIP, d4, d), g_wo.reshape(NCHIP, d4, d),
                                    g_pg.reshape(NCHIP, d4, d)], axis=1)
        dz, dxbc, ddt, gs["m_norm"][i], gd, gal = ssd_bwd(dyn, s["y"], s["z"], s["xbc"], s["dt"], s["sprev"], a_log_p[i:i + 1], d_exp[i:i + 1],
                                                          m_norm[i:i + 1], e_mat, et_mat)
        gs["m_D"][i], gs["m_A_log"][i] = gd[:, :hh], gal[:, :hh]
        dpre, ddt_raw, gdb = conv_m_bwd1(dxbc, s["xbc_raw"], ddt, s["dt_raw"], m_conv_full[i], m_conv_b[i:i + 1], dt_bias_p[i:i + 1])
        gs["m_dt_bias"][i] = gdb[:, :hh]
        dxbc_raw, gs["m_conv_w"][i], gs["m_conv_b"][i] = conv_bwd2(dpre, s["xbc_raw"], m_conv_full[i], "conv_m_bwd2")
        dcv, dsc_b, v = conv_a_bwd1(dya, s["sc3"], sc_conv_full[i])
        dsc_c, dsc_x, gs["sc_conv_w"][i] = conv_a_bwd2(dcv, v, s["sc3"], sc_conv_full[i])
        dproj = jnp.concatenate([dsc_b, dsc_c, dsc_x, dz, dxbc_raw, dgates, ddt_raw], axis=1)
        du = mm(dproj, w_in_p[i], name="d_proj_in", tn_cap=512)
        gwp = mm(dproj, s["u"], ta=True, out_dtype=BF, name="g_w_in", tm_cap=1152, tn_cap=512)
        gw_rows = jnp.concatenate([gwp[:5 * d + cw], gwp[7 * d + cw:7 * d + cw + hh], gwp[5 * d + cw:7 * d + cw]], axis=0)
        gin[i] = gw_rows.reshape(NCHIP, -1, d)
        dh, gs["mix_norm"][i] = norm_bwd_add(dh, s["h1"], mix_norm[i:i + 1], du)
        dn1, s1, dab1 = ffn_bwd(dh, s["ab1"], wf, i, 0)
        g_ffn1 = ffn_wgrads(s["n1"], dh, s1, dab1)
        dh, gs["ffn1_norm"][i] = norm_bwd_add(dh, s["h0"], ffn1_norm[i:i + 1], dn1)
        gf[i] = jnp.concatenate([g_ffn1, g_ffn2], axis=1)
    grad_x = dh[None]

    gpacks = [jnp.stack(g, axis=1) for g in (gf, g1024, gin, gpp)]
    from_sibling = swap_packs(gpacks, "grad_swap_halves")
    cs = [add_sibling(g, r_, "grad_add_sibling") for g, r_ in zip(gpacks, from_sibling, strict=True)]
    got = scatter_packs(cs, "grad_scatter")
    halves = [add_chips(c_, g_, depth, "grad_add_chips") for c_, g_ in zip(cs, got, strict=True)]
    rf, r1024, rin, rpp = join_packs(halves, "grad_join_halves")
    f4 = rf.shape[1] // 6
    ffn_rows = lambda j: rf[:, j * f4:(j + 1) * f4]
    grads = {
        "ffn1_wg": ffn_rows(0), "ffn1_wu": ffn_rows(1), "ffn1_wd": ffn_rows(2), "ffn2_wg": ffn_rows(3), "ffn2_wu": ffn_rows(4), "ffn2_wd": ffn_rows(5),
        "m_w_out": r1024[:, :2 * d4], "sc_w_out": r1024[:, 2 * d4:3 * d4], "w_o": r1024[:, 3 * d4:4 * d4], "ple_w_gate": r1024[:, 4 * d4:],
        "w_in": rin, "ple_w_proj": rpp,
    }

    small_names = list(SMALL_SHARDED + SMALL_REPL)
    small_full = [g_final[0] if n == "final_norm" else jnp.stack(gs[n]) for n in small_names]
    small_pack = _pack(small_full, LANE, HALO)
    all8 = all_gather_8(small_pack, "gather_small_grads")
    small_sum = add_parts([all8[k] for k in range(8)], F32, "add_small_grads", tm=256)
    for n, tot in zip(small_names, _unpack(small_sum, [a.shape for a in small_full]), strict=True):
        if n in SMALL_SHARDED:
            cl = wts[n].shape[2]
            grads[n] = lax.dynamic_slice_in_dim(tot, k_me * cl, cl, axis=2)
        else:
            grads[n] = tot.reshape(wts[n].shape)

    delta, new_m, new_v = {}, {}, {}
    for n in BIG:
        view = tr if n in TRANSPOSED else (lambda a: a)
        shp = grads[n].shape
        two = lambda a: a.reshape(-1, shp[-1])
        dl, nm, nv = adamw(two(view(wts[n])), two(grads[n]), two(view(mom[n])), two(view(vel[n])), "adamw_" + "x".join(map(str, shp[1:])))
        grads[n], delta[n], new_m[n], new_v[n] = view(grads[n]), view(dl.reshape(shp)), view(nm.reshape(shp)), view(nv.reshape(shp))
    sm_shapes = [wts[n].shape for n in small_names]
    pk = lambda dct: _pack([dct[n] for n in small_names], LANE, HALO)
    dl, nm, nv = adamw(pk(wts), pk(grads), pk(mom), pk(vel), "adamw_small")
    for n, a, b_, c_ in zip(small_names, _unpack(dl, sm_shapes), _unpack(nm, sm_shapes), _unpack(nv, sm_shapes), strict=True):
        delta[n], new_m[n], new_v[n] = a, b_, c_

    return (loss, grad_x, *[grads[n] for n in ORDER], *[delta[n] for n in ORDER], *[new_m[n] for n in ORDER], *[new_v[n] for n in ORDER])
```

```python
import jax
import jax.numpy as jnp
import numpy as np
from jax import lax
from jax.experimental import pallas as pl
from jax.experimental.pallas import tpu as pltpu

BF = jnp.bfloat16
F32 = jnp.float32
EPS = 1e-6
LANE = 128
HALO = 8
SSM_P = 64
SSM_N = 128
SSM_G = 4
SSM_L = 128
ADAM_LR, ADAM_B1, ADAM_B2, ADAM_EPS, ADAM_WD, ADAM_STEP = 0.001, 0.9, 0.999, 1e-08, 0.01, 10
VMEM_LIMIT = 56 * 1024 * 1024
TILE_ELEMS = 400_000
NCHIP = 4
FFN_SUB = 256
MESH = pl.DeviceIdType.MESH
HI = lax.Precision.HIGHEST


def _tile(n, cap, mult=LANE):
    best = None
    t = mult
    while t <= min(n, cap):
        if n % t == 0:
            best = t
        t += mult
    return best if best is not None else n


def _row_tile(r, c, mult=16):
    return _tile(r, max(mult, TILE_ELEMS // c // mult * mult), mult)


def _tile2(r, c, mult=16):
    tm = _row_tile(r, c, mult)
    tc = c if tm * c <= TILE_ELEMS else _tile(c, max(LANE, TILE_ELEMS // tm // LANE * LANE))
    return tm, tc


def _params(sem):
    return pltpu.CompilerParams(dimension_semantics=sem, vmem_limit_bytes=VMEM_LIMIT)


def _sigmoid(x):
    return 1.0 / (1.0 + jnp.exp(-x))


def _dot(a, b, ca=1, cb=0, precision=None):
    return lax.dot_general(a, b, (((ca,), (cb,)), ((), ())), precision=precision, preferred_element_type=F32)


def _rms(x, g):
    r = lax.rsqrt(jnp.mean(x * x, axis=-1, keepdims=True) + EPS)
    return x * r * g


def _rms_bwd(x, g, dy):
    r = lax.rsqrt(jnp.mean(x * x, axis=-1, keepdims=True) + EPS)
    xh = x * r
    dxh = dy * g
    dx = r * (dxh - xh * jnp.mean(dxh * xh, axis=-1, keepdims=True))
    return dx, jnp.sum(dy * xh, axis=0, keepdims=True)


def _accumulate(ref, val, first):
    @pl.when(first)
    def _():
        ref[...] = val

    @pl.when(jnp.logical_not(first))
    def _():
        ref[...] += val


class Rider:
    def __init__(self, ins, out_shapes, n_sems, copies):
        self.ins, self.out_shapes, self.n_sems, self.copies = list(ins), list(out_shapes), n_sems, copies


def host_call(body, *, name, grid, in_specs, out_specs, out_shape, scratch_shapes, operands, rider=None):
    n_in, n_out = len(in_specs), len(out_specs)
    if rider is None:
        outs = pl.pallas_call(body, name=name, grid=grid, in_specs=in_specs, out_specs=out_specs, out_shape=out_shape,
                              scratch_shapes=scratch_shapes, compiler_params=_params(("arbitrary",) * len(grid)))(*operands)
        return list(outs), []
    ri, ro = len(rider.ins), len(rider.out_shapes)

    def hosted(*refs):
        ins, r_ins = refs[:n_in], refs[n_in:n_in + ri]
        outs, r_outs = refs[n_in + ri:n_in + ri + n_out], refs[n_in + ri + n_out:n_in + ri + n_out + ro]
        scratch, (send_sems, recv_sems) = refs[n_in + ri + n_out + ro:-2], refs[-2:]
        first, last = None, None
        for ax, n in enumerate(grid):
            pid = pl.program_id(ax)
            first = (pid == 0) if first is None else jnp.logical_and(first, pid == 0)
            last = (pid == n - 1) if last is None else jnp.logical_and(last, pid == n - 1)

        @pl.when(first)
        def _():
            for cp in rider.copies(r_ins, r_outs, send_sems, recv_sems):
                cp.start()

        body(*ins, *outs, *scratch)

        @pl.when(last)
        def _():
            for cp in rider.copies(r_ins, r_outs, send_sems, recv_sems):
                cp.wait()

    outs = pl.pallas_call(
        hosted,
        name=name,
        grid=grid,
        in_specs=list(in_specs) + [ANY] * ri,
        out_specs=list(out_specs) + [ANY] * ro,
        out_shape=list(out_shape) + rider.out_shapes,
        scratch_shapes=list(scratch_shapes) + [pltpu.SemaphoreType.DMA((rider.n_sems,)), pltpu.SemaphoreType.DMA((rider.n_sems,))],
        compiler_params=_params(("arbitrary",) * len(grid)),
    )(*operands, *rider.ins)
    return list(outs[:n_out]), list(outs[n_out:])


def mmx(name, a, b, *, grid, a_spec, b_spec, o_spec, o_shape, o_dtype, ca, cb, acc_shape=None, scale=None, rider=None):
    nk = grid[-1] if acc_shape is not None else 1
    assert scale is None or nk == 1

    def body(a_ref, b_ref, o_ref, *acc):
        p = _dot(a_ref[...].astype(BF), b_ref[...].astype(BF), ca, cb)
        if scale is not None:
            p = p * scale
        if nk == 1:
            o_ref[...] = p.astype(o_ref.dtype)
        else:
            kk = pl.program_id(len(grid) - 1)
            _accumulate(acc[0], p, kk == 0)

            @pl.when(kk == nk - 1)
            def _():
                o_ref[...] = acc[0][...].astype(o_ref.dtype)

    if rider is not None:
        (out,), r_outs = host_call(body, name=name, grid=grid, in_specs=[a_spec, b_spec], out_specs=[o_spec], out_shape=[jax.ShapeDtypeStruct(o_shape, o_dtype)],
                                   scratch_shapes=[pltpu.VMEM(acc_shape, F32)] if nk > 1 else [], operands=(a, b), rider=rider)
        return out, r_outs
    sem = ("parallel",) * (len(grid) - 1) + ("arbitrary" if nk > 1 else "parallel",)
    return pl.pallas_call(
        body,
        name=name,
        grid=grid,
        in_specs=[a_spec, b_spec],
        out_specs=o_spec,
        out_shape=jax.ShapeDtypeStruct(o_shape, o_dtype),
        scratch_shapes=[pltpu.VMEM(acc_shape, F32)] if nk > 1 else [],
        compiler_params=_params(sem),
    )(a, b)


def mm(a, b, *, ta=False, tb=False, out_dtype=F32, name, tm_cap=1024, tn_cap=1024, tk_cap=4096, rider=None):
    m, k = (a.shape[1], a.shape[0]) if ta else a.shape
    n = b.shape[0] if tb else b.shape[1]
    assert (b.shape[1] if tb else b.shape[0]) == k
    tm, tn, tk = _tile(m, tm_cap), _tile(n, tn_cap), _tile(k, tk_cap)
    nk = k // tk
    a_spec = pl.BlockSpec((tk, tm), lambda i, j, kk: (kk, i)) if ta else pl.BlockSpec((tm, tk), lambda i, j, kk: (i, kk))
    b_spec = pl.BlockSpec((tn, tk), lambda i, j, kk: (j, kk)) if tb else pl.BlockSpec((tk, tn), lambda i, j, kk: (kk, j))
    return mmx(name, a, b, grid=(m // tm, n // tn, nk), a_spec=a_spec, b_spec=b_spec, o_spec=pl.BlockSpec((tm, tn), lambda i, j, kk: (i, j)),
               o_shape=(m, n), o_dtype=out_dtype, ca=0 if ta else 1, cb=1 if tb else 0, acc_shape=(tm, tn) if nk > 1 else None, rider=rider)


def ew(fn, rows, vecs, out_rows, out_red=(), *, tm, name, prev_halo=(), next_halo=()):
    t = rows[0].shape[0]
    tm = min(tm, t)
    nt = t // tm
    assert t % tm == 0 and (tm % HALO == 0 or (tm == t and not prev_halo and not next_halo))
    nr, nv, npv, nnx, nor = len(rows), len(vecs), len(prev_halo), len(next_halo), len(out_rows)
    hb = tm // HALO

    def body(*refs):
        i = pl.program_id(0)
        ins = [r[...].astype(F32) for r in refs[: nr + nv + npv + nnx]]
        outs = refs[nr + nv + npv + nnx:]
        o_rows, o_red = fn(i, nt, ins[:nr], ins[nr:nr + nv], ins[nr + nv:nr + nv + npv], ins[nr + nv + npv:])
        for ref, val in zip(outs[:nor], o_rows, strict=True):
            ref[...] = val.astype(ref.dtype)
        for ref, val in zip(outs[nor:], o_red, strict=True):
            _accumulate(ref, val, i == 0)

    in_specs = [pl.BlockSpec((tm, r.shape[1]), lambda i: (i, 0)) for r in rows]
    in_specs += [pl.BlockSpec(v.shape, lambda i: (0, 0)) for v in vecs]
    in_specs += [pl.BlockSpec((HALO, rows[k].shape[1]), lambda i: (jnp.maximum(i * hb - 1, 0), 0)) for k in prev_halo]
    in_specs += [pl.BlockSpec((HALO, rows[k].shape[1]), lambda i: (jnp.minimum((i + 1) * hb, t // HALO - 1), 0)) for k in next_halo]
    out_specs = [pl.BlockSpec((tm, c), lambda i: (i, 0)) for c, _ in out_rows]
    out_specs += [pl.BlockSpec(s, lambda i: (0, 0)) for s in out_red]
    out_shape = [jax.ShapeDtypeStruct((t, c), d) for c, d in out_rows] + [jax.ShapeDtypeStruct(s, F32) for s in out_red]
    return pl.pallas_call(
        body,
        name=name,
        grid=(nt,),
        in_specs=in_specs,
        out_specs=out_specs,
        out_shape=out_shape,
        compiler_params=_params(("arbitrary",) if out_red else ("parallel",)),
    )(*rows, *vecs, *[rows[k] for k in prev_halo], *[rows[k] for k in next_halo])


def _shift_down(x, prev, j):
    if j == 0:
        return x
    r = pltpu.roll(x, j, 0)
    rh = pltpu.roll(prev, j, 0)
    row = lax.broadcasted_iota(jnp.int32, (HALO, x.shape[1]), 0)
    head = jnp.where(row < j, rh, r[:HALO])
    return jnp.concatenate([head, r[HALO:]], axis=0)


def _shift_up(x, nxt, j):
    if j == 0:
        return x
    n = x.shape[0]
    r = pltpu.roll(x, n - j, 0)
    rh = pltpu.roll(nxt, HALO - j, 0)
    row = lax.broadcasted_iota(jnp.int32, (HALO, x.shape[1]), 0)
    tail = jnp.where(row >= HALO - j, rh, r[n - HALO:])
    return jnp.concatenate([r[: n - HALO], tail], axis=0)


def _conv_fwd(x, prev, w):
    kk = w.shape[0]
    acc = None
    for k in range(kk):
        term = w[k:k + 1, :] * _shift_down(x, prev, kk - 1 - k)
        acc = term if acc is None else acc + term
    return acc


def ffn_up(h, g, wf, layer, ffn):
    t, d = h.shape
    f4 = wf.shape[2] // 6
    tm = _tile(t, 1024)
    sub = _tile(tm, FFN_SUB, 16)

    def body(h_ref, g_ref, wg_ref, wu_ref, ab_ref, s_ref, n_ref):
        @pl.when(pl.program_id(1) == 0)
        def _():
            n_ref[...] = _rms(h_ref[...], g_ref[...]).astype(BF)

        for r in range(tm // sub):
            rows = slice(r * sub, (r + 1) * sub)
            n = n_ref[rows, :]
            a = _dot(n, wg_ref[...], 1, 1)
            b = _dot(n, wu_ref[...], 1, 1)
            ab_ref[0, rows, :] = a.astype(BF)
            ab_ref[1, rows, :] = b.astype(BF)
            s_ref[rows, :] = (a * _sigmoid(a) * b).astype(BF)

    wspec = lambda ib: pl.BlockSpec((None, None, f4, d), lambda i, j: (j, layer, ib, 0))
    return pl.pallas_call(
        body,
        name="ffn_up",
        grid=(t // tm, NCHIP),
        in_specs=[pl.BlockSpec((tm, d), lambda i, j: (i, 0)), pl.BlockSpec((1, d), lambda i, j: (0, 0)), wspec(3 * ffn), wspec(3 * ffn + 1)],
        out_specs=[pl.BlockSpec((2, None, tm, f4), lambda i, j: (0, j, i, 0)), pl.BlockSpec((None, tm, f4), lambda i, j: (j, i, 0)),
                   pl.BlockSpec((tm, d), lambda i, j: (i, 0))],
        out_shape=[jax.ShapeDtypeStruct((2, NCHIP, t, f4), BF), jax.ShapeDtypeStruct((NCHIP, t, f4), BF), jax.ShapeDtypeStruct((t, d), BF)],
        compiler_params=_params(("parallel", "arbitrary")),
    )(h, g, wf, wf)


def ffn_down(s4, wf, layer, ffn, h):
    t, d = h.shape
    f4 = s4.shape[2]
    tm = _tile(t, 512)

    def body(s_ref, w_ref, h_ref, o_ref):
        acc = _dot(s_ref[0], w_ref[0])
        for k in range(1, NCHIP):
            acc = acc + _dot(s_ref[k], w_ref[k])
        o_ref[...] = h_ref[...] + 0.5 * acc

    return pl.pallas_call(
        body,
        name="ffn_down",
        grid=(t // tm,),
        in_specs=[pl.BlockSpec((NCHIP, tm, f4), lambda i: (0, i, 0)), pl.BlockSpec((NCHIP, None, f4, d), lambda i: (0, layer, 3 * ffn + 2, 0)),
                  pl.BlockSpec((tm, d), lambda i: (i, 0))],
        out_specs=pl.BlockSpec((tm, d), lambda i: (i, 0)),
        out_shape=jax.ShapeDtypeStruct((t, d), F32),
        compiler_params=_params(("parallel",)),
    )(s4, wf, h)


def ffn_bwd(dho, ab, wf, layer, ffn):
    t, d = dho.shape
    f4 = wf.shape[2] // 6
    tm = _tile(t, 1024)
    sub = _tile(tm, FFN_SUB, 16)

    def body(dho_ref, ab_ref, wg_ref, wu_ref, wd_ref, dn_ref, s_ref, dab_ref, do_sc):
        j = pl.program_id(1)

        @pl.when(j == 0)
        def _():
            do_sc[...] = (0.5 * dho_ref[...]).astype(BF)
            dn_ref[...] = jnp.zeros_like(dn_ref)

        for r in range(tm // sub):
            rows = slice(r * sub, (r + 1) * sub)
            ds = _dot(do_sc[rows, :], wd_ref[...], 1, 1)
            av, bv = ab_ref[0, rows, :].astype(F32), ab_ref[1, rows, :].astype(F32)
            sig = _sigmoid(av)
            sl = av * sig
            s_ref[rows, :] = (sl * bv).astype(BF)
            da = (ds * bv * (sig * (1.0 + av * (1.0 - sig)))).astype(BF)
            db = (ds * sl).astype(BF)
            dab_ref[0, rows, :] = da
            dab_ref[1, rows, :] = db
            dn_ref[rows, :] += _dot(da, wg_ref[...]) + _dot(db, wu_ref[...])

    row = lambda c: pl.BlockSpec((tm, c), lambda i, j: (i, 0))
    wspec = lambda ib: pl.BlockSpec((None, None, f4, d), lambda i, j: (j, layer, ib, 0))
    ab_spec = pl.BlockSpec((2, None, tm, f4), lambda i, j: (0, j, i, 0))
    return pl.pallas_call(
        body,
        name="ffn_bwd",
        grid=(t // tm, NCHIP),
        in_specs=[row(d), ab_spec, wspec(3 * ffn), wspec(3 * ffn + 1), wspec(3 * ffn + 2)],
        out_specs=[row(d), pl.BlockSpec((None, tm, f4), lambda i, j: (j, i, 0)), ab_spec],
        out_shape=[jax.ShapeDtypeStruct((t, d), F32), jax.ShapeDtypeStruct((NCHIP, t, f4), BF), jax.ShapeDtypeStruct((2, NCHIP, t, f4), BF)],
        scratch_shapes=[pltpu.VMEM((tm, d), BF)],
        compiler_params=_params(("parallel", "arbitrary")),
    )(dho, ab, wf, wf, wf)


def ffn_wgrads(n, dho, s4, dab):
    t, d = n.shape
    f4 = s4.shape[2]
    tn = _tile(d, 512)
    g_in = mmx("g_ffn_in", dab, n, grid=(2, NCHIP, d // tn), a_spec=pl.BlockSpec((None, None, t, f4), lambda wh, k, j: (wh, k, 0, 0)),
               b_spec=pl.BlockSpec((t, tn), lambda wh, k, j: (0, j)), o_spec=pl.BlockSpec((None, None, f4, tn), lambda wh, k, j: (k, wh, 0, j)),
               o_shape=(NCHIP, 2, f4, d), o_dtype=BF, ca=0, cb=0)
    g_out = mmx("g_ffn_out", s4, dho, grid=(NCHIP, d // tn), a_spec=pl.BlockSpec((None, t, f4), lambda k, j: (k, 0, 0)),
                b_spec=pl.BlockSpec((t, tn), lambda k, j: (0, j)), o_spec=pl.BlockSpec((None, f4, tn), lambda k, j: (k, 0, j)),
                o_shape=(NCHIP, f4, d), o_dtype=BF, ca=0, cb=0, scale=0.5)
    return jnp.concatenate([g_in.reshape(NCHIP, 2 * f4, d), g_out], axis=1)


def norm_cast(h, g):
    def fn(i, nt, rows, vecs, prevs, nexts):
        return [_rms(rows[0], vecs[0])], []
    return ew(fn, [h], [g], [(h.shape[1], BF)], tm=512, name="norm_cast")[0]


def _zero_if(cond, x):
    return jnp.where(cond, jnp.zeros_like(x), x)


def conv_a_fwd(sc3, w_sc):
    d = sc3.shape[1] // 3

    def fn(i, nt, rows, vecs, prevs, nexts):
        x, pv = rows[0], _zero_if(i == 0, prevs[0])
        v = x[:, d:2 * d] * x[:, 2 * d:]
        vp = pv[:, d:2 * d] * pv[:, 2 * d:]
        return [x[:, :d] * _conv_fwd(v, vp, vecs[0])], []

    return ew(fn, [sc3], [w_sc], [(d, BF)], tm=256, name="conv_a_fwd", prev_halo=(0,))[0]


def _softplus(x):
    e = jnp.exp(-jnp.abs(x))
    return jnp.maximum(x, 0.0) + jnp.where(e < 1e-4, e - 0.5 * e * e, jnp.log(1.0 + e))


def conv_m_fwd(xbc_raw, dt_raw, w_mc, b_mc, dt_bias):
    def fn(i, nt, rows, vecs, prevs, nexts):
        pre = _conv_fwd(rows[0], _zero_if(i == 0, prevs[0]), vecs[0]) + vecs[1]
        return [pre * _sigmoid(pre), _softplus(rows[1] + vecs[2])], []

    return ew(fn, [xbc_raw, dt_raw], [w_mc, b_mc, dt_bias], [(xbc_raw.shape[1], F32), (LANE, F32)], tm=256, name="conv_m_fwd",
              prev_halo=(0,))


def conv_m_bwd1(dxbc, xbc_raw, ddt, dt_raw, w_mc, b_mc, dt_bias):
    def fn(i, nt, rows, vecs, prevs, nexts):
        pre = _conv_fwd(rows[1], _zero_if(i == 0, prevs[0]), vecs[0]) + vecs[1]
        sig = _sigmoid(pre)
        dpre = rows[0] * (sig * (1.0 + pre * (1.0 - sig)))
        ddr = rows[2] * _sigmoid(rows[3] + vecs[2])
        return [dpre, ddr], [jnp.sum(ddr, axis=0, keepdims=True)]

    return ew(fn, [dxbc, xbc_raw, ddt, dt_raw], [w_mc, b_mc, dt_bias], [(dxbc.shape[1], F32), (LANE, BF)], [(1, LANE)], tm=256,
              name="conv_m_bwd1", prev_halo=(1,))


def conv_bwd2(dpre, x, w, name):
    kk = w.shape[0]

    def fn(i, nt, rows, vecs, prevs, nexts):
        dp, xv = rows[0], rows[1]
        nx = _zero_if(i == nt - 1, nexts[0])
        pv = _zero_if(i == 0, prevs[0])
        dx = None
        dws = []
        for k in range(kk):
            term = vecs[0][k:k + 1, :] * _shift_up(dp, nx, kk - 1 - k)
            dx = term if dx is None else dx + term
            dws.append(jnp.sum(dp * _shift_down(xv, pv, kk - 1 - k), axis=0, keepdims=True))
        return [dx], [jnp.concatenate(dws, axis=0), jnp.sum(dp, axis=0, keepdims=True)]

    c = x.shape[1]
    return ew(fn, [dpre, x], [w], [(c, BF)], [(kk, c), (1, c)], tm=256, name=name, prev_halo=(1,), next_halo=(0,))


def conv_a_bwd1(dya, sc3, w_sc):
    d = sc3.shape[1] // 3

    def fn(i, nt, rows, vecs, prevs, nexts):
        x, pv = rows[1], _zero_if(i == 0, prevs[0])
        v = x[:, d:2 * d] * x[:, 2 * d:]
        vp = pv[:, d:2 * d] * pv[:, 2 * d:]
        return [rows[0] * x[:, :d], rows[0] * _conv_fwd(v, vp, vecs[0]), v], []

    return ew(fn, [dya, sc3], [w_sc], [(d, F32), (d, BF), (d, F32)], tm=256, name="conv_a_bwd1", prev_halo=(1,))


def conv_a_bwd2(dcv, v, sc3, w_sc):
    d = v.shape[1]
    kk = w_sc.shape[0]

    def fn(i, nt, rows, vecs, prevs, nexts):
        dp, vv, x = rows
        nx = _zero_if(i == nt - 1, nexts[0])
        pv = _zero_if(i == 0, prevs[0])
        dv = None
        dws = []
        for k in range(kk):
            term = vecs[0][k:k + 1, :] * _shift_up(dp, nx, kk - 1 - k)
            dv = term if dv is None else dv + term
            dws.append(jnp.sum(dp * _shift_down(vv, pv, kk - 1 - k), axis=0, keepdims=True))
        return [dv * x[:, 2 * d:], dv * x[:, d:2 * d]], [jnp.concatenate(dws, axis=0)]

    return ew(fn, [dcv, v, sc3], [w_sc], [(d, BF), (d, BF)], [(kk, d)], tm=256, name="conv_a_bwd2", prev_halo=(1,), next_halo=(0,))


def _xdot(a, b, passes, split_lhs, ca=1, cb=0):
    parts, r = [], (a if split_lhs else b)
    for _ in range(passes):
        piece = r.astype(BF)
        parts.append(piece)
        r = r - piece.astype(F32)
    other = (b if split_lhs else a).astype(BF)
    acc = None
    for piece in parts:
        term = _dot(piece, other, ca, cb) if split_lhs else _dot(other, piece, ca, cb)
        acc = term if acc is None else acc + term
    return acc


def _ssd_common(xbc_ref, dt_ref, alog_ref, e_ref, w):
    ll = SSM_L
    xs = xbc_ref[:, 0:w]
    dtv = dt_ref[...]
    a_row = -jnp.exp(alog_ref[...])
    a = dtv * a_row
    row = lax.broadcasted_iota(jnp.int32, (ll, ll), 0)
    col = lax.broadcasted_iota(jnp.int32, (ll, ll), 1)
    tril = (row >= col).astype(F32)
    triu = (row <= col).astype(F32)
    acl = _xdot(tril, a, 3, False)
    acl_t = _xdot(a, triu, 3, True, 0, 0)
    e = e_ref[...]
    aclx = _xdot(acl, e, 3, True)
    dtx = _xdot(dtv, e, 2, True)
    last = aclx[ll - 1:ll, :]
    e_in = jnp.exp(aclx)
    e_end = jnp.exp(last - aclx)
    e_tot = jnp.exp(last)
    x = xs * dtx
    return dict(xs=xs, dtv=dtv, a_row=a_row, a=a, row=row, col=col, triu=triu, acl=acl, acl_t=acl_t, dtx=dtx, e_in=e_in, e_end=e_end,
                e_tot=e_tot, x=x)


def _decay(q, hh):
    diff = q["acl"][:, hh:hh + 1] - q["acl_t"][hh:hh + 1, :]
    return jnp.exp(jnp.where(q["row"] >= q["col"], diff, -jnp.inf))


def ssd_fwd(xbc, dt, z, a_log, d_exp, m_norm, e_mat):
    t = xbc.shape[0]
    w = z.shape[1]
    gn = SSM_G * SSM_N
    gw = w // SSM_G
    ll, nn = SSM_L, SSM_N
    nc = t // ll
    cw = xbc.shape[1]

    def body(xbc_ref, dt_ref, z_ref, alog_ref, dexp_ref, mn_ref, e_ref, yn_ref, y_ref, sp_ref, s_sc):
        c = pl.program_id(0)

        @pl.when(c == 0)
        def _():
            s_sc[...] = jnp.zeros_like(s_sc)

        q = _ssd_common(xbc_ref, dt_ref, alog_ref, e_ref, w)
        xb = q["x"].astype(BF)
        xsb = (q["x"] * q["e_end"]).astype(BF)
        sp = s_sc[...]
        sp_ref[0] = sp
        spb = sp.astype(BF)
        lane = lax.broadcasted_iota(jnp.int32, (ll, LANE), 1)
        for g in range(SSM_G):
            lo = g * gw
            bg = xbc_ref[:, w + g * nn:w + (g + 1) * nn].astype(BF)
            cg = xbc_ref[:, w + gn + g * nn:w + gn + (g + 1) * nn].astype(BF)
            yoff = _dot(cg, spb[:, lo:lo + gw]) * q["e_in"][:, lo:lo + gw]
            s_sc[:, lo:lo + gw] = sp[:, lo:lo + gw] * q["e_tot"][:, lo:lo + gw] + _dot(bg, xsb[:, lo:lo + gw], 0, 0)
            cb = _dot(cg, bg, 1, 1)
            for pr in range(gw // LANE):
                l0 = lo + pr * LANE
                xp = xb[:, l0:l0 + LANE]
                ys = []
                for hh in (l0 // SSM_P, l0 // SSM_P + 1):
                    wm = (cb * _decay(q, hh)).astype(BF)
                    ys.append(_dot(wm, xp))
                ydiag = jnp.where(lane < SSM_P, ys[0], ys[1])
                y_ref[:, l0:l0 + LANE] = ydiag + yoff[:, pr * LANE:(pr + 1) * LANE] + dexp_ref[:, l0:l0 + LANE] * q["xs"][:, l0:l0 + LANE]
        zv = z_ref[...].astype(F32)
        yz = y_ref[...] * (zv * _sigmoid(zv))
        for g in range(SSM_G):
            lo = g * gw
            yn_ref[:, lo:lo + gw] = _rms(yz[:, lo:lo + gw], mn_ref[:, lo:lo + gw]).astype(BF)

    vec = lambda s: pl.BlockSpec(s, lambda c: (0, 0))
    return pl.pallas_call(
        body,
        name="ssd_fwd",
        grid=(nc,),
        in_specs=[
            pl.BlockSpec((ll, cw), lambda c: (c, 0)), pl.BlockSpec((ll, LANE), lambda c: (c, 0)), pl.BlockSpec((ll, w), lambda c: (c, 0)),
            vec((1, LANE)), vec((1, w)), vec((1, w)), vec((LANE, w)),
        ],
        out_specs=[pl.BlockSpec((ll, w), lambda c: (c, 0)), pl.BlockSpec((ll, w), lambda c: (c, 0)), pl.BlockSpec((1, nn, w), lambda c: (c, 0, 0))],
        out_shape=[jax.ShapeDtypeStruct((t, w), BF), jax.ShapeDtypeStruct((t, w), F32), jax.ShapeDtypeStruct((nc, nn, w), F32)],
        scratch_shapes=[pltpu.VMEM((nn, w), F32)],
        compiler_params=_params(("arbitrary",)),
    )(xbc, dt, z, a_log, d_exp, m_norm, e_mat)


def ssd_bwd(dyn, y, z, xbc, dt, sprev, a_log, d_exp, m_norm, e_mat, et_mat, rider=None):
    t = xbc.shape[0]
    w = z.shape[1]
    gn = SSM_G * SSM_N
    gw = w // SSM_G
    ll, nn = SSM_L, SSM_N
    nc = t // ll
    cw = xbc.shape[1]

    def body(dyn_ref, y_ref, z_ref, xbc_ref, dt_ref, sp_ref, alog_ref, dexp_ref, mn_ref, e_ref, et_ref,
             dz_ref, dxbc_ref, ddt_ref, dmn_ref, dd_ref, dal_ref, ds_sc, dy_sc, dx_sc):
        step = pl.program_id(0)

        @pl.when(step == 0)
        def _():
            ds_sc[...] = jnp.zeros_like(ds_sc)

        zv, yv = z_ref[...].astype(F32), y_ref[...]
        sg = _sigmoid(zv)
        sz = zv * sg
        yz = yv * sz
        dmn = []
        for g in range(SSM_G):
            lo = g * gw
            dseg, dmn_g = _rms_bwd(yz[:, lo:lo + gw], mn_ref[:, lo:lo + gw], dyn_ref[:, lo:lo + gw])
            dy_sc[:, lo:lo + gw] = dseg
            dmn.append(dmn_g)
        dmn = jnp.concatenate(dmn, axis=1)
        dyz = dy_sc[...]
        dz_ref[...] = (dyz * yv * (sg * (1.0 + zv * (1.0 - sg)))).astype(BF)
        dy = dyz * sz

        q = _ssd_common(xbc_ref, dt_ref, alog_ref, e_ref, w)
        x = q["x"]
        xb = x.astype(BF)
        xsb = (x * q["e_end"]).astype(BF)
        sp = sp_ref[0]
        spb = sp.astype(BF)
        dsn = ds_sc[...]
        dsnb = dsn.astype(BF)
        dyb = dy.astype(BF)
        lane = lax.broadcasted_iota(jnp.int32, (ll, LANE), 1)
        lane1 = lax.broadcasted_iota(jnp.int32, (1, LANE), 1)
        sub1 = lax.broadcasted_iota(jnp.int32, (LANE, 1), 0)
        dacl = jnp.zeros((ll, LANE), F32)
        dacl_t = jnp.zeros((LANE, ll), F32)
        d_ein, d_eend, d_etot = [], [], []
        for g in range(SSM_G):
            lo = g * gw
            sl = slice(lo, lo + gw)
            bg = xbc_ref[:, w + g * nn:w + (g + 1) * nn].astype(BF)
            cg = xbc_ref[:, w + gn + g * nn:w + gn + (g + 1) * nn].astype(BF)
            zg = _dot(cg, spb[:, sl])
            dzz = (dy[:, sl] * q["e_in"][:, sl]).astype(BF)
            d_ein.append(dy[:, sl] * zg)
            dcg = _dot(dzz, spb[:, sl], 1, 1)
            ds_sc[:, sl] = _dot(cg, dzz, 0, 0) + dsn[:, sl] * q["e_tot"][:, sl]
            d_etot.append(jnp.sum(dsn[:, sl] * sp[:, sl], axis=0, keepdims=True))
            dbg = _dot(xsb[:, sl], dsnb[:, sl], 1, 1)
            dxs_g = _dot(bg, dsnb[:, sl])
            d_eend.append(dxs_g * x[:, sl])
            cb = _dot(cg, bg, 1, 1)
            dcb = jnp.zeros((ll, ll), F32)
            for pr in range(gw // LANE):
                l0 = lo + pr * LANE
                xp = xb[:, l0:l0 + LANE]
                dyp = dyb[:, l0:l0 + LANE]
                dxp = []
                for hi, hh in enumerate((l0 // SSM_P, l0 // SSM_P + 1)):
                    lm = _decay(q, hh)
                    wm = (cb * lm).astype(BF)
                    in_head = (lane < SSM_P) if hi == 0 else (lane >= SSM_P)
                    dwm = _dot(jnp.where(in_head, dyp, jnp.zeros_like(dyp)), xp, 1, 1)
                    dxp.append(_dot(wm, dyp, 0, 0))
                    dlm = dwm * lm
                    dcb = dcb + dlm
                    dd = dlm * cb
                    dacl = dacl + jnp.sum(dd, axis=1, keepdims=True) * (lane1 == hh).astype(F32)
                    dacl_t = dacl_t + (sub1 == hh).astype(F32) * jnp.sum(dd, axis=0, keepdims=True)
                dx_sc[:, l0:l0 + LANE] = jnp.where(lane < SSM_P, dxp[0], dxp[1]) + dxs_g[:, pr * LANE:(pr + 1) * LANE] * q["e_end"][:, l0:l0 + LANE]
            dcbb = dcb.astype(BF)
            dxbc_ref[:, w + g * nn:w + (g + 1) * nn] = dbg + _dot(dcbb, cg, 0, 0)
            dxbc_ref[:, w + gn + g * nn:w + gn + (g + 1) * nn] = dcg + _dot(dcbb, bg)
        d_ein = jnp.concatenate(d_ein, axis=1) * q["e_in"]
        d_eend = jnp.concatenate(d_eend, axis=1) * q["e_end"]
        d_etot = jnp.concatenate(d_etot, axis=1) * q["e_tot"]
        et = et_ref[...]
        last_add = jnp.sum(d_eend, axis=0, keepdims=True) + d_etot
        last_add = _xdot(jnp.broadcast_to(last_add, (HALO, w)), et, 2, True)[0:1]
        row1 = lax.broadcasted_iota(jnp.int32, (ll, LANE), 0)
        dacl = dacl + _xdot(d_ein - d_eend, et, 2, True) + jnp.where(row1 == ll - 1, last_add, 0.0)
        da = _xdot(q["triu"], dacl, 2, False) - _xdot(q["triu"], dacl_t, 2, False, 1, 1)
        dxv = dx_sc[...]
        dxbc_ref[:, 0:w] = dexp_ref[...] * dy + dxv * q["dtx"]
        ddt_ref[...] = _xdot(dxv * q["xs"], et, 2, True) + da * q["a_row"]
        dal = jnp.sum(da * q["dtv"], axis=0, keepdims=True) * q["a_row"]
        ddv = jnp.sum(dy * q["xs"], axis=0, keepdims=True)
        ddv = _xdot(jnp.broadcast_to(ddv, (HALO, w)), et, 2, True)[0:1]
        _accumulate(dmn_ref, dmn, step == 0)
        _accumulate(dd_ref, ddv, step == 0)
        _accumulate(dal_ref, dal, step == 0)

    rev = lambda c_: pl.BlockSpec((ll, c_), lambda s: (nc - 1 - s, 0))
    vec = lambda s_: pl.BlockSpec(s_, lambda s: (0, 0))
    return host_call(
        body,
        name="ssd_bwd",
        grid=(nc,),
        in_specs=[
            rev(w), rev(w), rev(w), rev(cw), rev(LANE), pl.BlockSpec((1, nn, w), lambda s: (nc - 1 - s, 0, 0)),
            vec((1, LANE)), vec((1, w)), vec((1, w)), vec((LANE, w)), vec((w, LANE)),
        ],
        out_specs=[rev(w), rev(cw), rev(LANE), vec((1, w)), vec((1, LANE)), vec((1, LANE))],
        out_shape=[
            jax.ShapeDtypeStruct((t, w), BF), jax.ShapeDtypeStruct((t, cw), F32), jax.ShapeDtypeStruct((t, LANE), F32),
            jax.ShapeDtypeStruct((1, w), F32), jax.ShapeDtypeStruct((1, LANE), F32), jax.ShapeDtypeStruct((1, LANE), F32),
        ],
        scratch_shapes=[pltpu.VMEM((nn, w), F32), pltpu.VMEM((ll, w), F32), pltpu.VMEM((ll, w), F32)],
        operands=(dyn, y, z, xbc, dt, sprev, a_log, d_exp, m_norm, e_mat, et_mat),
        rider=rider,
    )


def _w1024_spec(d, layer, nblk, iblk):
    r = nblk * (d // NCHIP)
    return pl.BlockSpec((NCHIP, None, r, d), lambda i: (0, layer, iblk // nblk, 0))


def _whole(ref):
    v = ref[...]
    return v.reshape(v.shape[0] * v.shape[1], v.shape[2])


def mix_out_fwd(ya_in, yn, gates, h, w1024, layer):
    t, d = h.shape
    tm = _tile(t, 256)

    def body(ya_ref, yn_ref, g_ref, h_ref, wm_ref, wa_ref, wo_ref, ho_ref, oa_ref, om_ref, mg_ref):
        y_a = _dot(ya_ref[...], _whole(wa_ref))
        y_m = _dot(yn_ref[...], _whole(wm_ref))
        oa_ref[...] = y_a
        om_ref[...] = y_m
        gv = g_ref[...].astype(F32)
        mg = (_sigmoid(gv[:, :d]) * y_a + _sigmoid(gv[:, d:]) * y_m).astype(BF)
        mg_ref[...] = mg
        ho_ref[...] = h_ref[...] + _dot(mg, _whole(wo_ref))

    row = lambda c: pl.BlockSpec((tm, c), lambda i: (i, 0))
    return pl.pallas_call(
        body,
        name="mix_out_fwd",
        grid=(t // tm,),
        in_specs=[row(d), row(2 * d), row(2 * d), row(d), _w1024_spec(d, layer, 2, 0), _w1024_spec(d, layer, 1, 2), _w1024_spec(d, layer, 1, 3)],
        out_specs=[row(d), row(d), row(d), row(d)],
        out_shape=[jax.ShapeDtypeStruct((t, d), F32), jax.ShapeDtypeStruct((t, d), F32), jax.ShapeDtypeStruct((t, d), F32),
                   jax.ShapeDtypeStruct((t, d), BF)],
        compiler_params=_params(("parallel",)),
    )(ya_in, yn, gates, h, w1024, w1024, w1024)


def mix_out_bwd(dh, gates, y_a, y_m, w1024, layer):
    t, d = dh.shape
    tm = _tile(t, 256)

    def body(dh_ref, g_ref, ya_ref, ym_ref, wm_ref, wa_ref, wo_ref, dg_ref, dya_ref, dyn_ref, da_ref, dm_ref):
        dmg = _dot(dh_ref[...].astype(BF), _whole(wo_ref), 1, 1)
        gv = g_ref[...].astype(F32)
        sa, sm = _sigmoid(gv[:, :d]), _sigmoid(gv[:, d:])
        dg_ref[:, :d] = (dmg * ya_ref[...] * sa * (1.0 - sa)).astype(BF)
        dg_ref[:, d:] = (dmg * ym_ref[...] * sm * (1.0 - sm)).astype(BF)
        da = (dmg * sa).astype(BF)
        dm = (dmg * sm).astype(BF)
        da_ref[...] = da
        dm_ref[...] = dm
        dya_ref[...] = _dot(da, _whole(wa_ref), 1, 1)
        dyn_ref[...] = _dot(dm, _whole(wm_ref), 1, 1)

    row = lambda c: pl.BlockSpec((tm, c), lambda i: (i, 0))
    return pl.pallas_call(
        body,
        name="mix_out_bwd",
        grid=(t // tm,),
        in_specs=[row(d), row(2 * d), row(d), row(d), _w1024_spec(d, layer, 2, 0), _w1024_spec(d, layer, 1, 2), _w1024_spec(d, layer, 1, 3)],
        out_specs=[row(2 * d), row(d), row(2 * d), row(d), row(d)],
        out_shape=[jax.ShapeDtypeStruct((t, 2 * d), BF), jax.ShapeDtypeStruct((t, d), F32), jax.ShapeDtypeStruct((t, 2 * d), F32),
                   jax.ShapeDtypeStruct((t, d), BF), jax.ShapeDtypeStruct((t, d), BF)],
        compiler_params=_params(("parallel",)),
    )(dh, gates, y_a, y_m, w1024, w1024, w1024)


def norm_bwd_add(dh, h, g, dn):
    def fn(i, nt, rows, vecs, prevs, nexts):
        dx, dg = _rms_bwd(rows[1], vecs[0], rows[2])
        return [rows[0] + dx], [dg]
    d = h.shape[1]
    return ew(fn, [dh, h, dn], [g], [(d, F32)], [(1, d)], tm=512, name="norm_bwd_add")


def _pe(p, wpp_ref):
    pb = p.astype(BF)
    return jnp.concatenate([_dot(pb, wpp_ref[k]) for k in range(NCHIP)], axis=1)


def ple_fwd(h, g, p, w1024, wpp, layer):
    t, d = h.shape
    tm = _tile(t, 512)

    def body(h_ref, g_ref, p_ref, wg_ref, wp_ref, ho_ref):
        hv = h_ref[...]
        gate = _sigmoid(_dot(_rms(hv, g_ref[...]).astype(BF), _whole(wg_ref)))
        ho_ref[...] = hv + gate * _pe(p_ref[...], wp_ref)

    row = lambda c: pl.BlockSpec((tm, c), lambda i: (i, 0))
    wpp_spec = pl.BlockSpec((NCHIP, None) + wpp.shape[2:], lambda i: (0, layer, 0, 0))
    return pl.pallas_call(
        body,
        name="ple_fwd",
        grid=(t // tm,),
        in_specs=[row(d), pl.BlockSpec((1, d), lambda i: (0, 0)), row(p.shape[1]), _w1024_spec(d, layer, 1, 4), wpp_spec],
        out_specs=row(d),
        out_shape=jax.ShapeDtypeStruct((t, d), F32),
        compiler_params=_params(("parallel",)),
    )(h, g, p, w1024, wpp)


def ple_bwd(dho, h, g, p, w1024, wpp, layer):
    t, d = h.shape
    tm = _tile(t, 512)

    def body(dho_ref, h_ref, g_ref, p_ref, wg_ref, wp_ref, dh_ref, dg_ref, n_ref, dgp_ref, dpe_ref):
        hv, dv = h_ref[...], dho_ref[...]
        n = _rms(hv, g_ref[...]).astype(BF)
        n_ref[...] = n
        wg = _whole(wg_ref)
        gate = _sigmoid(_dot(n, wg))
        pe = _pe(p_ref[...], wp_ref)
        dpe_ref[...] = (dv * gate).astype(BF)
        dgp = (dv * pe * gate * (1.0 - gate)).astype(BF)
        dgp_ref[...] = dgp
        dx, dg = _rms_bwd(hv, g_ref[...], _dot(dgp, wg, 1, 1))
        dh_ref[...] = dv + dx
        _accumulate(dg_ref, dg, pl.program_id(0) == 0)

    row = lambda c: pl.BlockSpec((tm, c), lambda i: (i, 0))
    wpp_spec = pl.BlockSpec((NCHIP, None) + wpp.shape[2:], lambda i: (0, layer, 0, 0))
    return pl.pallas_call(
        body,
        name="ple_bwd",
        grid=(t // tm,),
        in_specs=[row(d), row(d), pl.BlockSpec((1, d), lambda i: (0, 0)), row(p.shape[1]), _w1024_spec(d, layer, 1, 4), wpp_spec],
        out_specs=[row(d), pl.BlockSpec((1, d), lambda i: (0, 0)), row(d), row(d), row(d)],
        out_shape=[jax.ShapeDtypeStruct((t, d), F32), jax.ShapeDtypeStruct((1, d), F32), jax.ShapeDtypeStruct((t, d), BF),
                   jax.ShapeDtypeStruct((t, d), BF), jax.ShapeDtypeStruct((t, d), BF)],
        compiler_params=_params(("arbitrary",)),
    )(dho, h, g, p, w1024, wpp)


def loss_bwd(h, g, target):
    d = h.shape[1]

    def fn(i, nt, rows, vecs, prevs, nexts):
        err = _rms(rows[0], vecs[0]) - rows[1]
        dx, dg = _rms_bwd(rows[0], vecs[0], err * (1.0 / d))
        return [dx], [jnp.sum(err * err, axis=0, keepdims=True) * (0.5 / d), dg]

    return ew(fn, [h, target], [g], [(d, F32)], [(1, d), (1, d)], tm=512, name="loss_bwd")


def adamw(w, g, m, v, name):
    c1, c2 = 1.0 / (1.0 - ADAM_B1 ** ADAM_STEP), 1.0 / (1.0 - ADAM_B2 ** ADAM_STEP)

    def fn(i, nt, rows, vecs, prevs, nexts):
        wv, gv, mv, vv = rows
        mn = ADAM_B1 * mv + (1.0 - ADAM_B1) * gv
        vn = ADAM_B2 * vv + (1.0 - ADAM_B2) * (gv * gv)
        delta = -ADAM_LR * ((mn * c1) / (jnp.sqrt(vn * c2) + ADAM_EPS) + ADAM_WD * wv)
        return [delta, mn, vn], []

    c = w.shape[1]
    return ew(fn, [w, g, m, v], [], [(c, F32)] * 3, tm=_row_tile(w.shape[0], c, HALO), name=name)


def _place():
    return lax.axis_index("x"), lax.axis_index("y"), lax.axis_index("c")


def _other_chips(x, y):
    return [(1 - x, y), (x, 1 - y), (1 - x, 1 - y)]


ANY = pl.BlockSpec(memory_space=pl.ANY)


def _comm_call(body, name, ins, out_shapes, n_sems, aliases=None):
    return pl.pallas_call(
        body,
        name=name,
        in_specs=[ANY] * len(ins),
        out_specs=[ANY] * len(out_shapes),
        out_shape=out_shapes,
        scratch_shapes=[pltpu.SemaphoreType.DMA((n_sems,)), pltpu.SemaphoreType.DMA((n_sems,))],
        input_output_aliases=aliases or {},
    )(*ins)


def gather_packs(packs, layer_major, name):
    nt = len(packs)
    nl = packs[0].shape[0]
    hl = nl // 2

    def body(*refs):
        ins, outs, (send_sems, recv_sems) = refs[:nt], refs[nt:2 * nt], refs[2 * nt:]
        x, y, cc = _place()
        chips = _other_chips(x, y)
        sibling = (x, y, 1 - cc)
        mine, theirs, whole = pl.ds(cc * hl, hl), pl.ds((1 - cc) * hl, hl), pl.ds(0, nl)
        k_me = 2 * x + y

        def block(ti, k, layers):
            return outs[ti].at[layers, k] if layer_major[ti] else outs[ti].at[k, layers]

        def copy(k, src, dst, to):
            return pltpu.make_async_remote_copy(src_ref=src, dst_ref=dst, send_sem=send_sems.at[k], recv_sem=recv_sems.at[k],
                                                device_id=to, device_id_type=MESH)

        started = []
        for ti in range(nt):
            for j, chip in enumerate(chips):
                started.append(copy(7 * ti + j, ins[ti].at[mine], block(ti, k_me, mine), (*chip, cc)))
            started.append(copy(7 * ti + 6, ins[ti], block(ti, k_me, whole), sibling))
        for cp in started:
            cp.start()
        for ti in range(nt):
            for j, (px, py) in enumerate(chips):
                landed = block(ti, 2 * px + py, mine)
                copy(7 * ti + j, landed, landed, (px, py, cc)).wait_recv()
                fw = copy(7 * ti + 3 + j, landed, landed, sibling)
                fw.start()
                started.append(fw)
        for ti in range(nt):
            for j, (px, py) in enumerate(chips):
                landed = block(ti, 2 * px + py, theirs)
                copy(7 * ti + 3 + j, landed, landed, sibling).wait_recv()
            own = block(ti, k_me, whole)
            copy(7 * ti + 6, own, own, sibling).wait_recv()
        for cp in started:
            cp.wait_send()

    shapes = [jax.ShapeDtypeStruct((nl, NCHIP) + p.shape[1:] if lm else (NCHIP,) + p.shape, p.dtype) for p, lm in zip(packs, layer_major, strict=True)]
    return _comm_call(body, name, packs, shapes, 7 * nt)


def swap_packs(gs, name):
    nt = len(gs)
    hl = gs[0].shape[1] // 2

    def body(*refs):
        ins, outs, (send_sems, recv_sems) = refs[:nt], refs[nt:2 * nt], refs[2 * nt:]
        x, y, cc = _place()
        theirs = pl.ds((1 - cc) * hl, hl)
        cps = [pltpu.make_async_remote_copy(src_ref=ins[ti].at[:, theirs], dst_ref=outs[ti], send_sem=send_sems.at[ti], recv_sem=recv_sems.at[ti],
                                            device_id=(x, y, 1 - cc), device_id_type=MESH) for ti in range(nt)]
        for cp in cps:
            cp.start()
        for cp in cps:
            cp.wait()

    return _comm_call(body, name, gs, [jax.ShapeDtypeStruct((NCHIP, hl) + g.shape[2:], g.dtype) for g in gs], nt)


def scatter_packs(cs, name):
    rider = scatter_rider(cs)

    def body(*refs):
        nt = len(cs)
        cps = rider.copies(refs[:nt], refs[nt:2 * nt], *refs[2 * nt:])
        for cp in cps:
            cp.start()
        for cp in cps:
            cp.wait()

    return _comm_call(body, name, cs, rider.out_shapes, rider.n_sems)


def scatter_rider(cs):
    nt = len(cs)

    def copies(ins, outs, send_sems, recv_sems):
        x, y, cc = _place()
        cps = []
        for ti in range(nt):
            for j, (px, py) in enumerate(_other_chips(x, y)):
                cps.append(pltpu.make_async_remote_copy(src_ref=ins[ti].at[2 * px + py], dst_ref=outs[ti].at[j], send_sem=send_sems.at[3 * ti + j],
                                                        recv_sem=recv_sems.at[3 * ti + j], device_id=(px, py, cc), device_id_type=MESH))
        return cps

    return Rider(cs, [jax.ShapeDtypeStruct((3,) + c_.shape[1:], c_.dtype) for c_ in cs], 3 * nt, copies)


def join_packs(fulls, name):
    nt = len(fulls)
    hl = fulls[0].shape[0] // 2

    def body(*refs):
        ins, outs, (send_sems, recv_sems) = refs[:nt], refs[nt:2 * nt], refs[2 * nt:]
        x, y, cc = _place()
        mine = pl.ds(cc * hl, hl)
        cps = [pltpu.make_async_remote_copy(src_ref=ins[ti].at[mine], dst_ref=outs[ti].at[mine], send_sem=send_sems.at[ti], recv_sem=recv_sems.at[ti],
                                            device_id=(x, y, 1 - cc), device_id_type=MESH) for ti in range(nt)]
        for cp in cps:
            cp.start()
        for cp in cps:
            cp.wait()

    return _comm_call(body, name, fulls, [jax.ShapeDtypeStruct(f.shape, f.dtype) for f in fulls], nt, aliases={ti: ti for ti in range(nt)})


def add_sibling(g, recv, name):
    _, nl, r, c = g.shape
    hl = nl // 2
    tm, tc = _tile2(r, c)

    def body(g_ref, r_ref, o_ref):
        o_ref[...] = (g_ref[...].astype(F32) + r_ref[...].astype(F32)).astype(o_ref.dtype)

    blk = (None, None, tm, tc)
    return pl.pallas_call(
        body,
        name=name,
        grid=(NCHIP, hl, r // tm, c // tc),
        in_specs=[pl.BlockSpec(blk, lambda k, l, i, j: (k, lax.axis_index("c") * hl + l, i, j)), pl.BlockSpec(blk, lambda k, l, i, j: (k, l, i, j))],
        out_specs=pl.BlockSpec(blk, lambda k, l, i, j: (k, l, i, j)),
        out_shape=jax.ShapeDtypeStruct(recv.shape, BF),
        compiler_params=_params(("parallel",) * 4),
    )(g, recv)


def add_chips(cs, got, nl, name):
    _, hl, r, c = cs.shape
    tm, tc = _tile2(r, c)

    def body(own_ref, got_ref, o_ref):
        o_ref[...] = own_ref[...].astype(F32) + got_ref[0].astype(F32) + got_ref[1].astype(F32) + got_ref[2].astype(F32)

    return pl.pallas_call(
        body,
        name=name,
        grid=(hl, r // tm, c // tc),
        in_specs=[pl.BlockSpec((None, None, tm, tc), lambda l, i, j: (2 * lax.axis_index("x") + lax.axis_index("y"), l, i, j)),
                  pl.BlockSpec((3, None, tm, tc), lambda l, i, j: (0, l, i, j))],
        out_specs=pl.BlockSpec((None, tm, tc), lambda l, i, j: (lax.axis_index("c") * hl + l, i, j)),
        out_shape=jax.ShapeDtypeStruct((nl, r, c), F32),
        compiler_params=_params(("parallel",) * 3),
    )(cs, got)


def all_gather_xy(shard, name):
    r, c = shard.shape
    hr = r // 2
    assert r % 32 == 0

    def body(x_ref, out_ref, send_sems, recv_sems, local_sem):
        x, y, cc = _place()
        chips = _other_chips(x, y)
        mine = pl.ds(pl.multiple_of(cc * hr, 16), hr)
        theirs = pl.ds(pl.multiple_of((1 - cc) * hr, 16), hr)
        k_me = 2 * x + y

        def copy(k, src, dst, to):
            return pltpu.make_async_remote_copy(src_ref=src, dst_ref=dst, send_sem=send_sems.at[k], recv_sem=recv_sems.at[k],
                                                device_id=to, device_id_type=MESH)

        own = pltpu.make_async_copy(x_ref, out_ref.at[k_me], local_sem)
        own.start()
        first = [copy(j, x_ref.at[mine], out_ref.at[k_me, mine], (*chip, cc)) for j, chip in enumerate(chips)]
        for cp in first:
            cp.start()
        passed = []
        for j, (px, py) in enumerate(chips):
            landed = out_ref.at[2 * px + py, mine]
            copy(j, landed, landed, (px, py, cc)).wait_recv()
            fw = copy(3 + j, landed, landed, (x, y, 1 - cc))
            fw.start()
            passed.append(fw)
        for j, (px, py) in enumerate(chips):
            landed = out_ref.at[2 * px + py, theirs]
            copy(3 + j, landed, landed, (x, y, 1 - cc)).wait_recv()
        for cp in first + passed:
            cp.wait_send()
        own.wait()

    return pl.pallas_call(
        body,
        name=name,
        in_specs=[ANY],
        out_specs=ANY,
        out_shape=jax.ShapeDtypeStruct((NCHIP, r, c), shard.dtype),
        scratch_shapes=[pltpu.SemaphoreType.DMA((6,)), pltpu.SemaphoreType.DMA((6,)), pltpu.SemaphoreType.DMA],
    )(shard)


def all_gather_8(block, name):
    m, c = block.shape

    def body(x_ref, out_ref, send_sems, recv_sems, local_sem):
        x, y, cc = _place()
        me, sibling = (x, y, cc), (x, y, 1 - cc)
        chips = _other_chips(x, y)

        def rows(px, py, pc):
            return out_ref.at[4 * px + 2 * py + pc]

        def copy(k, blk, to, src=None):
            return pltpu.make_async_remote_copy(src_ref=rows(*blk) if src is None else src, dst_ref=rows(*blk), send_sem=send_sems.at[k],
                                                recv_sem=recv_sems.at[k], device_id=to, device_id_type=MESH)

        mine = pltpu.make_async_copy(x_ref, rows(*me), local_sem)
        mine.start()
        first = [copy(0, me, sibling, src=x_ref)]
        first += [copy(1 + j, me, (*chip, cc), src=x_ref) for j, chip in enumerate(chips)]
        for cp in first:
            cp.start()
        passed = [copy(4 + j, (*chip, cc), sibling) for j, chip in enumerate(chips)]
        for j, chip in enumerate(chips):
            copy(1 + j, (*chip, cc), me).wait_recv()
            passed[j].start()
        copy(0, sibling, me).wait_recv()
        for j, chip in enumerate(chips):
            copy(4 + j, (*chip, 1 - cc), me).wait_recv()
        for cp in first + passed:
            cp.wait_send()
        mine.wait()

    return pl.pallas_call(
        body,
        name=name,
        in_specs=[pl.BlockSpec(memory_space=pltpu.VMEM)],
        out_specs=pl.BlockSpec(memory_space=pltpu.VMEM),
        out_shape=jax.ShapeDtypeStruct((8, m, c), block.dtype),
        scratch_shapes=[pltpu.SemaphoreType.DMA((7,)), pltpu.SemaphoreType.DMA((7,)), pltpu.SemaphoreType.DMA],
        compiler_params=pltpu.CompilerParams(vmem_limit_bytes=VMEM_LIMIT),
    )(block)


def add_parts(parts, out_dtype, name, tm=512):
    def fn(i, nt, rows, vecs, prevs, nexts):
        acc = rows[0]
        for r_ in rows[1:]:
            acc = acc + r_
        return [acc], []
    r, c = parts[0].shape
    return ew(fn, list(parts), [], [(c, out_dtype)], tm=_tile(r, tm, 16), name=name)[0]


SMALL_SHARDED = ("sc_conv_w", "m_conv_w")
SMALL_REPL = ("ffn1_norm", "mix_norm", "m_conv_b", "m_dt_bias", "m_A_log", "m_D", "m_norm", "ffn2_norm", "ple_norm", "final_norm")
BIG = ("ffn1_wg", "ffn1_wu", "ffn1_wd", "w_in", "sc_w_out", "m_w_out", "w_o", "ffn2_wg", "ffn2_wu", "ffn2_wd", "ple_w_gate", "ple_w_proj")
TRANSPOSED = ("ffn1_wg", "ffn1_wu", "ffn2_wg", "ffn2_wu", "w_in")
ORDER = ("ffn1_norm", "ffn1_wg", "ffn1_wu", "ffn1_wd", "mix_norm", "w_in", "sc_conv_w", "sc_w_out", "m_conv_w", "m_conv_b", "m_dt_bias",
         "m_A_log", "m_D", "m_norm", "m_w_out", "w_o", "ffn2_norm", "ffn2_wg", "ffn2_wu", "ffn2_wd", "ple_norm", "ple_w_gate", "ple_w_proj",
         "final_norm")


def _pack(arrs, cols, row_mult):
    flat = jnp.concatenate([a.reshape(-1) for a in arrs])
    n = flat.shape[0]
    rows = -(-n // cols)
    rows = -(-rows // row_mult) * row_mult
    return jnp.pad(flat, (0, rows * cols - n)).reshape(rows, cols)


def _unpack(flat2d, shapes):
    flat = flat2d.reshape(-1)
    out, off = [], 0
    for s in shapes:
        n = int(np.prod(s))
        out.append(flat[off:off + n].reshape(s))
        off += n
    return out


def _row_cat(arrs, dtype):
    return jnp.concatenate([a.astype(dtype) for a in arrs], axis=1)


def kernel(x, p, ffn1_norm, ffn1_wg, ffn1_wu, ffn1_wd, mix_norm, w_in, sc_conv_w, sc_w_out, m_conv_w, m_conv_b, m_dt_bias, m_A_log, m_D, m_norm, m_w_out, w_o, ffn2_norm, ffn2_wg, ffn2_wu, ffn2_wd, ple_norm, ple_w_gate, ple_w_proj, final_norm, loss_target, m_ffn1_norm, m_ffn1_wg, m_ffn1_wu, m_ffn1_wd, m_mix_norm, m_w_in, m_sc_conv_w, m_sc_w_out, m_m_conv_w, m_m_conv_b, m_m_dt_bias, m_m_A_log, m_m_D, m_m_norm, m_m_w_out, m_w_o, m_ffn2_norm, m_ffn2_wg, m_ffn2_wu, m_ffn2_wd, m_ple_norm, m_ple_w_gate, m_ple_w_proj, m_final_norm, v_ffn1_norm, v_ffn1_wg, v_ffn1_wu, v_ffn1_wd, v_mix_norm, v_w_in, v_sc_conv_w, v_sc_w_out, v_m_conv_w, v_m_conv_b, v_m_dt_bias, v_m_A_log, v_m_D, v_m_norm, v_m_w_out, v_w_o, v_ffn2_norm, v_ffn2_wg, v_ffn2_wu, v_ffn2_wd, v_ple_norm, v_ple_w_gate, v_ple_w_proj, v_final_norm):
    args = dict(locals())
    wts = {n: args[n] for n in ORDER}
    mom = {n: args["m_" + n] for n in ORDER}
    vel = {n: args["v_" + n] for n in ORDER}

    depth = ffn1_norm.shape[0]
    d = x.shape[-1]
    w = 2 * d
    hh = w // SSM_P
    cw = w + 2 * SSM_G * SSM_N
    d4 = d // NCHIP
    my_x, my_y, my_c = _place()
    k_me = 2 * my_x + my_y

    tr = lambda a: jnp.swapaxes(a, 1, 2)
    packs = [
        _row_cat([tr(ffn1_wg), tr(ffn1_wu), ffn1_wd, tr(ffn2_wg), tr(ffn2_wu), ffn2_wd], BF),
        _row_cat([m_w_out, sc_w_out, w_o, ple_w_gate], BF),
        tr(w_in).astype(BF),
        ple_w_proj.astype(BF),
    ]
    wf, w1024, win4, wpp = gather_packs(packs, (False, False, True, False), "gather_weights")
    small_local = [sc_conv_w, m_conv_w]
    gathered_s = all_gather_xy(_pack(small_local, LANE, 32), "gather_conv_weights")
    per_shard_s = [_unpack(gathered_s[k], [a.shape for a in small_local]) for k in range(NCHIP)]
    sc_conv_full = jnp.concatenate([per_shard_s[k][0] for k in range(NCHIP)], axis=2)
    m_conv_full = jnp.concatenate([per_shard_s[k][1] for k in range(NCHIP)], axis=2)

    wi = win4.reshape(depth, -1, d)
    o_z, o_xbc, o_dt, o_g = 3 * d, 5 * d, 5 * d + cw, 5 * d + cw + hh
    w_sc3, w_z, w_xbc = wi[:, :o_z], wi[:, o_z:o_xbc], wi[:, o_xbc:o_dt]
    w_g2 = wi[:, o_g:o_g + 2 * d]
    w_dt = jnp.pad(wi[:, o_dt:o_g], ((0, 0), (0, LANE - hh), (0, 0)))
    w_in_p = jnp.concatenate([w_sc3, w_z, w_xbc, w_g2, w_dt], axis=1)

    pad_h = lambda a: jnp.pad(a, ((0, 0), (0, LANE - hh)))
    dt_bias_p, a_log_p = pad_h(m_dt_bias), pad_h(m_A_log)
    d_exp = jnp.repeat(m_D, SSM_P, axis=1)
    e_mat = (jnp.arange(w)[None, :] // SSM_P == jnp.arange(LANE)[:, None]).astype(F32)
    et_mat = e_mat.T

    h = x[0]
    saved = []
    for i in range(depth):
        s = {}
        s["h0"] = h
        s["ab1"], s4, s["n1"] = ffn_up(h, ffn1_norm[i:i + 1], wf, i, 0)
        h = ffn_down(s4, wf, i, 0, h)
        s["h1"] = h
        u = norm_cast(h, mix_norm[i:i + 1])
        s["u"] = u
        s["sc3"] = mm(u, w_sc3[i], tb=True, out_dtype=BF, name="proj_sc")
        s["z"] = mm(u, w_z[i], tb=True, out_dtype=BF, name="proj_z")
        s["xbc_raw"] = mm(u, w_xbc[i], tb=True, out_dtype=BF, name="proj_xbc")
        s["gates"] = mm(u, w_g2[i], tb=True, out_dtype=BF, name="proj_gates")
        s["dt_raw"] = mm(u, w_dt[i], tb=True, name="proj_dt")
        s["ya_in"] = conv_a_fwd(s["sc3"], sc_conv_full[i])
        s["xbc"], s["dt"] = conv_m_fwd(s["xbc_raw"], s["dt_raw"], m_conv_full[i], m_conv_b[i:i + 1], dt_bias_p[i:i + 1])
        s["yn"], s["y"], s["sprev"] = ssd_fwd(s["xbc"], s["dt"], s["z"], a_log_p[i:i + 1], d_exp[i:i + 1], m_norm[i:i + 1], e_mat)
        h, s["y_a"], s["y_m"], s["merged"] = mix_out_fwd(s["ya_in"], s["yn"], s["gates"], h, w1024, i)
        s["h2"] = h
        s["ab2"], s4, s["n2"] = ffn_up(h, ffn2_norm[i:i + 1], wf, i, 1)
        h = ffn_down(s4, wf, i, 1, h)
        s["h3"] = h
        h = ple_fwd(h, ple_norm[i:i + 1], p[i, 0], w1024, wpp, i)
        saved.append(s)

    dh, loss_lanes, g_final = loss_bwd(h, final_norm[None, :], loss_target[0])
    loss = lax.psum(jnp.sum(loss_lanes), ("x", "y", "c"))

    def finish_reduce(cs, got):
        halves = [add_chips(c_, g_, 2, "grad_add_chips") for c_, g_ in zip(cs, got, strict=True)]
        return [f.reshape(-1, f.shape[2]) for f in join_packs(halves, "grad_join_halves")]

    pending, reduced = None, [None] * depth
    gs = {n: [None] * depth for n in SMALL_SHARDED + SMALL_REPL if n != "final_norm"}
    for i in reversed(range(depth)):
        s = saved[i]
        dh, gs["ple_norm"][i], n3, dgp, dpe = ple_bwd(dh, s["h3"], ple_norm[i:i + 1], p[i, 0], w1024, wpp, i)
        g_pg = mm(n3, dgp, ta=True, out_dtype=BF, name="g_ple_gate", tm_cap=512, tn_cap=512)
        g_pp = mm(p[i, 0], dpe, ta=True, out_dtype=BF, name="g_ple_proj", tm_cap=512, tn_cap=512)
        g_pp = jnp.transpose(g_pp.reshape(g_pp.shape[0], NCHIP, d4), (1, 0, 2))
        dn2, s2, dab2 = ffn_bwd(dh, s["ab2"], wf, i, 1)
        g_ffn2 = ffn_wgrads(s["n2"], dh, s2, dab2)
        dh, gs["ffn2_norm"][i] = norm_bwd_add(dh, s["h2"], ffn2_norm[i:i + 1], dn2)
        dgates, dya, dyn, dy_a, dy_m = mix_out_bwd(dh, s["gates"], s["y_a"], s["y_m"], w1024, i)
        g_wo = mm(s["merged"], dh, ta=True, out_dtype=BF, name="g_w_o", tm_cap=512, tn_cap=512)
        g_sco = mm(s["ya_in"], dy_a, ta=True, out_dtype=BF, name="g_sc_out", tm_cap=512, tn_cap=512)
        g_mo = mm(s["yn"], dy_m, ta=True, out_dtype=BF, name="g_m_out", tm_cap=512, tn_cap=512)
        g_1024 = jnp.concatenate([g_mo.reshape(NCHIP, 2 * d4, d), g_sco.reshape(NCHIP, d4, d), g_wo.reshape(NCHIP, d4, d),
                                  g_pg.reshape(NCHIP, d4, d)], axis=1)
        (dz, dxbc, ddt, gs["m_norm"][i], gd, gal), got_a = ssd_bwd(dyn, s["y"], s["z"], s["xbc"], s["dt"], s["sprev"], a_log_p[i:i + 1], d_exp[i:i + 1],
                                                                   m_norm[i:i + 1], e_mat, et_mat, rider=scatter_rider(pending[:1]) if pending else None)
        gs["m_D"][i], gs["m_A_log"][i] = gd[:, :hh], gal[:, :hh]
        dpre, ddt_raw, gdb = conv_m_bwd1(dxbc, s["xbc_raw"], ddt, s["dt_raw"], m_conv_full[i], m_conv_b[i:i + 1], dt_bias_p[i:i + 1])
        gs["m_dt_bias"][i] = gdb[:, :hh]
        dxbc_raw, gs["m_conv_w"][i], gs["m_conv_b"][i] = conv_bwd2(dpre, s["xbc_raw"], m_conv_full[i], "conv_m_bwd2")
        dcv, dsc_b, v = conv_a_bwd1(dya, s["sc3"], sc_conv_full[i])
        dsc_c, dsc_x, gs["sc_conv_w"][i] = conv_a_bwd2(dcv, v, s["sc3"], sc_conv_full[i])
        dproj = jnp.concatenate([dsc_b, dsc_c, dsc_x, dz, dxbc_raw, dgates, ddt_raw], axis=1)
        if pending:
            du, got_b = mm(dproj, w_in_p[i], name="d_proj_in", tn_cap=512, rider=scatter_rider(pending[1:]))
            reduced[i + 1] = finish_reduce(pending, got_a + got_b)
        else:
            du = mm(dproj, w_in_p[i], name="d_proj_in", tn_cap=512)
        gwp = mm(dproj, s["u"], ta=True, out_dtype=BF, name="g_w_in", tm_cap=1152, tn_cap=512)
        gw_rows = jnp.concatenate([gwp[:5 * d + cw], gwp[7 * d + cw:7 * d + cw + hh], gwp[5 * d + cw:7 * d + cw]], axis=0)
        g_in = gw_rows.reshape(NCHIP, -1, d)
        dh, gs["mix_norm"][i] = norm_bwd_add(dh, s["h1"], mix_norm[i:i + 1], du)
        dn1, s1, dab1 = ffn_bwd(dh, s["ab1"], wf, i, 0)
        g_ffn1 = ffn_wgrads(s["n1"], dh, s1, dab1)
        dh, gs["ffn1_norm"][i] = norm_bwd_add(dh, s["h0"], ffn1_norm[i:i + 1], dn1)
        g_layer = [jnp.concatenate([g_ffn1, g_ffn2], axis=1), g_1024, g_in, g_pp]
        g_layer = [g.reshape(NCHIP, 2, g.shape[1] // 2, g.shape[2]) for g in g_layer]
        from_sibling = swap_packs(g_layer, "grad_swap_halves")
        pending = [add_sibling(g, r_, "grad_add_sibling") for g, r_ in zip(g_layer, from_sibling, strict=True)]
    reduced[0] = finish_reduce(pending, scatter_packs(pending, "grad_scatter"))
    grad_x = dh[None]

    rf, r1024, rin, rpp = (jnp.stack([reduced[l][j] for l in range(depth)]) for j in range(4))
    f4 = rf.shape[1] // 6
    ffn_rows = lambda j: rf[:, j * f4:(j + 1) * f4]
    grads = {
        "ffn1_wg": ffn_rows(0), "ffn1_wu": ffn_rows(1), "ffn1_wd": ffn_rows(2), "ffn2_wg": ffn_rows(3), "ffn2_wu": ffn_rows(4), "ffn2_wd": ffn_rows(5),
        "m_w_out": r1024[:, :2 * d4], "sc_w_out": r1024[:, 2 * d4:3 * d4], "w_o": r1024[:, 3 * d4:4 * d4], "ple_w_gate": r1024[:, 4 * d4:],
        "w_in": rin, "ple_w_proj": rpp,
    }

    small_names = list(SMALL_SHARDED + SMALL_REPL)
    small_full = [g_final[0] if n == "final_norm" else jnp.stack(gs[n]) for n in small_names]
    small_pack = _pack(small_full, LANE, HALO)
    all8 = all_gather_8(small_pack, "gather_small_grads")
    small_sum = add_parts([all8[k] for k in range(8)], F32, "add_small_grads", tm=256)
    for n, tot in zip(small_names, _unpack(small_sum, [a.shape for a in small_full]), strict=True):
        if n in SMALL_SHARDED:
            cl = wts[n].shape[2]
            grads[n] = lax.dynamic_slice_in_dim(tot, k_me * cl, cl, axis=2)
        else:
            grads[n] = tot.reshape(wts[n].shape)

    delta, new_m, new_v = {}, {}, {}
    for n in BIG:
        view = tr if n in TRANSPOSED else (lambda a: a)
        shp = grads[n].shape
        two = lambda a: a.reshape(-1, shp[-1])
        dl, nm, nv = adamw(two(view(wts[n])), two(grads[n]), two(view(mom[n])), two(view(vel[n])), "adamw_" + "x".join(map(str, shp[1:])))
        grads[n], delta[n], new_m[n], new_v[n] = view(grads[n]), view(dl.reshape(shp)), view(nm.reshape(shp)), view(nv.reshape(shp))
    sm_shapes = [wts[n].shape for n in small_names]
    pk = lambda dct: _pack([dct[n] for n in small_names], LANE, HALO)
    dl, nm, nv = adamw(pk(wts), pk(grads), pk(mom), pk(vel), "adamw_small")
    for n, a, b_, c_ in zip(small_names, _unpack(dl, sm_shapes), _unpack(nm, sm_shapes), _unpack(nv, sm_shapes), strict=True):
        delta[n], new_m[n], new_v[n] = a, b_, c_

    return (loss, grad_x, *[grads[n] for n in ORDER], *[delta[n] for n in ORDER], *[new_m[n] for n in ORDER], *[new_v[n] for n in ORDER])
```

```python
import jax
import jax.numpy as jnp
import numpy as np
from jax import lax
from jax.experimental import pallas as pl
from jax.experimental.pallas import tpu as pltpu

BF = jnp.bfloat16
F32 = jnp.float32
EPS = 1e-6
LANE = 128
HALO = 8
SSM_P = 64
SSM_N = 128
SSM_G = 4
SSM_L = 128
ADAM_LR, ADAM_B1, ADAM_B2, ADAM_EPS, ADAM_WD, ADAM_STEP = 0.001, 0.9, 0.999, 1e-08, 0.01, 10
VMEM_LIMIT = 56 * 1024 * 1024
TILE_ELEMS = 400_000
NCHIP = 4
FFN_SUB = 256
MESH = pl.DeviceIdType.MESH
HI = lax.Precision.HIGHEST


def _tile(n, cap, mult=LANE):
    best = None
    t = mult
    while t <= min(n, cap):
        if n % t == 0:
            best = t
        t += mult
    return best if best is not None else n


def _row_tile(r, c, mult=16):
    return _tile(r, max(mult, TILE_ELEMS // c // mult * mult), mult)


def _tile2(r, c, mult=16):
    tm = _row_tile(r, c, mult)
    tc = c if tm * c <= TILE_ELEMS else _tile(c, max(LANE, TILE_ELEMS // tm // LANE * LANE))
    return tm, tc


def _params(sem):
    return pltpu.CompilerParams(dimension_semantics=sem, vmem_limit_bytes=VMEM_LIMIT)


def _sigmoid(x):
    return 1.0 / (1.0 + jnp.exp(-x))


def _dot(a, b, ca=1, cb=0, precision=None):
    return lax.dot_general(a, b, (((ca,), (cb,)), ((), ())), precision=precision, preferred_element_type=F32)


def _rms(x, g):
    r = lax.rsqrt(jnp.mean(x * x, axis=-1, keepdims=True) + EPS)
    return x * r * g


def _rms_bwd(x, g, dy):
    r = lax.rsqrt(jnp.mean(x * x, axis=-1, keepdims=True) + EPS)
    xh = x * r
    dxh = dy * g
    dx = r * (dxh - xh * jnp.mean(dxh * xh, axis=-1, keepdims=True))
    return dx, jnp.sum(dy * xh, axis=0, keepdims=True)


def _accumulate(ref, val, first):
    @pl.when(first)
    def _():
        ref[...] = val

    @pl.when(jnp.logical_not(first))
    def _():
        ref[...] += val


RIDER_MID = 0.6


class Rider:
    def __init__(self, ins, out_shapes, n_sems, start, finish, mid=None):
        self.ins, self.out_shapes, self.n_sems, self.start, self.mid, self.finish = list(ins), list(out_shapes), n_sems, start, mid, finish

    def standalone(self, name):
        ni, no = len(self.ins), len(self.out_shapes)

        def body(*refs):
            parts = (refs[:ni], refs[ni:ni + no], *refs[ni + no:])
            self.start(*parts)
            if self.mid is not None:
                self.mid(*parts)
            self.finish(*parts)

        return _comm_call(body, name, self.ins, self.out_shapes, self.n_sems)


def host_call(body, *, name, grid, in_specs, out_specs, out_shape, scratch_shapes, operands, rider=None):
    n_in, n_out = len(in_specs), len(out_specs)
    if rider is None:
        outs = pl.pallas_call(body, name=name, grid=grid, in_specs=in_specs, out_specs=out_specs, out_shape=out_shape,
                              scratch_shapes=scratch_shapes, compiler_params=_params(("arbitrary",) * len(grid)))(*operands)
        return list(outs), []
    ri, ro = len(rider.ins), len(rider.out_shapes)

    def hosted(*refs):
        ins, r_ins = refs[:n_in], refs[n_in:n_in + ri]
        outs, r_outs = refs[n_in + ri:n_in + ri + n_out], refs[n_in + ri + n_out:n_in + ri + n_out + ro]
        scratch, (send_sems, recv_sems) = refs[n_in + ri + n_out + ro:-2], refs[-2:]
        step, total = 0, 1
        for ax, n in enumerate(grid):
            step = step * n + pl.program_id(ax)
            total *= n
        parts = (r_ins, r_outs, send_sems, recv_sems)

        @pl.when(step == 0)
        def _():
            rider.start(*parts)

        if rider.mid is not None:
            @pl.when(step == min(total - 1, int(total * RIDER_MID)))
            def _():
                rider.mid(*parts)

        body(*ins, *outs, *scratch)

        @pl.when(step == total - 1)
        def _():
            rider.finish(*parts)

    outs = pl.pallas_call(
        hosted,
        name=name,
        grid=grid,
        in_specs=list(in_specs) + [ANY] * ri,
        out_specs=list(out_specs) + [ANY] * ro,
        out_shape=list(out_shape) + rider.out_shapes,
        scratch_shapes=list(scratch_shapes) + [pltpu.SemaphoreType.DMA((rider.n_sems,)), pltpu.SemaphoreType.DMA((rider.n_sems,))],
        compiler_params=_params(("arbitrary",) * len(grid)),
    )(*operands, *rider.ins)
    return list(outs[:n_out]), list(outs[n_out:])


def mmx(name, a, b, *, grid, a_spec, b_spec, o_spec, o_shape, o_dtype, ca, cb, acc_shape=None, scale=None, rider=None):
    nk = grid[-1] if acc_shape is not None else 1
    assert scale is None or nk == 1

    def body(a_ref, b_ref, o_ref, *acc):
        p = _dot(a_ref[...].astype(BF), b_ref[...].astype(BF), ca, cb)
        if scale is not None:
            p = p * scale
        if nk == 1:
            o_ref[...] = p.astype(o_ref.dtype)
        else:
            kk = pl.program_id(len(grid) - 1)
            _accumulate(acc[0], p, kk == 0)

            @pl.when(kk == nk - 1)
            def _():
                o_ref[...] = acc[0][...].astype(o_ref.dtype)

    if rider is not None:
        (out,), r_outs = host_call(body, name=name, grid=grid, in_specs=[a_spec, b_spec], out_specs=[o_spec], out_shape=[jax.ShapeDtypeStruct(o_shape, o_dtype)],
                                   scratch_shapes=[pltpu.VMEM(acc_shape, F32)] if nk > 1 else [], operands=(a, b), rider=rider)
        return out, r_outs
    sem = ("parallel",) * (len(grid) - 1) + ("arbitrary" if nk > 1 else "parallel",)
    return pl.pallas_call(
        body,
        name=name,
        grid=grid,
        in_specs=[a_spec, b_spec],
        out_specs=o_spec,
        out_shape=jax.ShapeDtypeStruct(o_shape, o_dtype),
        scratch_shapes=[pltpu.VMEM(acc_shape, F32)] if nk > 1 else [],
        compiler_params=_params(sem),
    )(a, b)


def mm(a, b, *, ta=False, tb=False, out_dtype=F32, name, tm_cap=1024, tn_cap=1024, tk_cap=4096, rider=None):
    m, k = (a.shape[1], a.shape[0]) if ta else a.shape
    n = b.shape[0] if tb else b.shape[1]
    assert (b.shape[1] if tb else b.shape[0]) == k
    tm, tn, tk = _tile(m, tm_cap), _tile(n, tn_cap), _tile(k, tk_cap)
    nk = k // tk
    a_spec = pl.BlockSpec((tk, tm), lambda i, j, kk: (kk, i)) if ta else pl.BlockSpec((tm, tk), lambda i, j, kk: (i, kk))
    b_spec = pl.BlockSpec((tn, tk), lambda i, j, kk: (j, kk)) if tb else pl.BlockSpec((tk, tn), lambda i, j, kk: (kk, j))
    return mmx(name, a, b, grid=(m // tm, n // tn, nk), a_spec=a_spec, b_spec=b_spec, o_spec=pl.BlockSpec((tm, tn), lambda i, j, kk: (i, j)),
               o_shape=(m, n), o_dtype=out_dtype, ca=0 if ta else 1, cb=1 if tb else 0, acc_shape=(tm, tn) if nk > 1 else None, rider=rider)


def ew(fn, rows, vecs, out_rows, out_red=(), *, tm, name, prev_halo=(), next_halo=()):
    t = rows[0].shape[0]
    tm = min(tm, t)
    nt = t // tm
    assert t % tm == 0 and (tm % HALO == 0 or (tm == t and not prev_halo and not next_halo))
    nr, nv, npv, nnx, nor = len(rows), len(vecs), len(prev_halo), len(next_halo), len(out_rows)
    hb = tm // HALO

    def body(*refs):
        i = pl.program_id(0)
        ins = [r[...].astype(F32) for r in refs[: nr + nv + npv + nnx]]
        outs = refs[nr + nv + npv + nnx:]
        o_rows, o_red = fn(i, nt, ins[:nr], ins[nr:nr + nv], ins[nr + nv:nr + nv + npv], ins[nr + nv + npv:])
        for ref, val in zip(outs[:nor], o_rows, strict=True):
            ref[...] = val.astype(ref.dtype)
        for ref, val in zip(outs[nor:], o_red, strict=True):
            _accumulate(ref, val, i == 0)

    in_specs = [pl.BlockSpec((tm, r.shape[1]), lambda i: (i, 0)) for r in rows]
    in_specs += [pl.BlockSpec(v.shape, lambda i: (0, 0)) for v in vecs]
    in_specs += [pl.BlockSpec((HALO, rows[k].shape[1]), lambda i: (jnp.maximum(i * hb - 1, 0), 0)) for k in prev_halo]
    in_specs += [pl.BlockSpec((HALO, rows[k].shape[1]), lambda i: (jnp.minimum((i + 1) * hb, t // HALO - 1), 0)) for k in next_halo]
    out_specs = [pl.BlockSpec((tm, c), lambda i: (i, 0)) for c, _ in out_rows]
    out_specs += [pl.BlockSpec(s, lambda i: (0, 0)) for s in out_red]
    out_shape = [jax.ShapeDtypeStruct((t, c), d) for c, d in out_rows] + [jax.ShapeDtypeStruct(s, F32) for s in out_red]
    return pl.pallas_call(
        body,
        name=name,
        grid=(nt,),
        in_specs=in_specs,
        out_specs=out_specs,
        out_shape=out_shape,
        compiler_params=_params(("arbitrary",) if out_red else ("parallel",)),
    )(*rows, *vecs, *[rows[k] for k in prev_halo], *[rows[k] for k in next_halo])


def _shift_down(x, prev, j):
    if j == 0:
        return x
    r = pltpu.roll(x, j, 0)
    rh = pltpu.roll(prev, j, 0)
    row = lax.broadcasted_iota(jnp.int32, (HALO, x.shape[1]), 0)
    head = jnp.where(row < j, rh, r[:HALO])
    return jnp.concatenate([head, r[HALO:]], axis=0)


def _shift_up(x, nxt, j):
    if j == 0:
        return x
    n = x.shape[0]
    r = pltpu.roll(x, n - j, 0)
    rh = pltpu.roll(nxt, HALO - j, 0)
    row = lax.broadcasted_iota(jnp.int32, (HALO, x.shape[1]), 0)
    tail = jnp.where(row >= HALO - j, rh, r[n - HALO:])
    return jnp.concatenate([r[: n - HALO], tail], axis=0)


def _conv_fwd(x, prev, w):
    kk = w.shape[0]
    acc = None
    for k in range(kk):
        term = w[k:k + 1, :] * _shift_down(x, prev, kk - 1 - k)
        acc = term if acc is None else acc + term
    return acc


def ffn_up(h, g, wf, rider=None):
    t, d = h.shape
    f4 = wf.shape[1] // 3
    tm = _tile(t, 1024)
    sub = _tile(tm, FFN_SUB, 16)

    def body(h_ref, g_ref, wg_ref, wu_ref, ab_ref, s_ref, n_ref):
        @pl.when(pl.program_id(1) == 0)
        def _():
            n_ref[...] = _rms(h_ref[...], g_ref[...]).astype(BF)

        for r in range(tm // sub):
            rows = slice(r * sub, (r + 1) * sub)
            n = n_ref[rows, :]
            a = _dot(n, wg_ref[...], 1, 1)
            b = _dot(n, wu_ref[...], 1, 1)
            ab_ref[0, rows, :] = a.astype(BF)
            ab_ref[1, rows, :] = b.astype(BF)
            s_ref[rows, :] = (a * _sigmoid(a) * b).astype(BF)

    wspec = lambda ib: pl.BlockSpec((None, f4, d), lambda i, j: (j, ib, 0))
    return host_call(
        body,
        name="ffn_up",
        grid=(t // tm, NCHIP),
        in_specs=[pl.BlockSpec((tm, d), lambda i, j: (i, 0)), pl.BlockSpec((1, d), lambda i, j: (0, 0)), wspec(0), wspec(1)],
        out_specs=[pl.BlockSpec((2, None, tm, f4), lambda i, j: (0, j, i, 0)), pl.BlockSpec((None, tm, f4), lambda i, j: (j, i, 0)),
                   pl.BlockSpec((tm, d), lambda i, j: (i, 0))],
        out_shape=[jax.ShapeDtypeStruct((2, NCHIP, t, f4), BF), jax.ShapeDtypeStruct((NCHIP, t, f4), BF), jax.ShapeDtypeStruct((t, d), BF)],
        scratch_shapes=[],
        operands=(h, g, wf, wf),
        rider=rider,
    )


def ffn_down(s4, wf, h, rider=None):
    t, d = h.shape
    f4 = s4.shape[2]
    tm = _tile(t, 512)

    def body(s_ref, w_ref, h_ref, o_ref):
        acc = _dot(s_ref[0], w_ref[0])
        for k in range(1, NCHIP):
            acc = acc + _dot(s_ref[k], w_ref[k])
        o_ref[...] = h_ref[...] + 0.5 * acc

    (out,), r_outs = host_call(
        body,
        name="ffn_down",
        grid=(t // tm,),
        in_specs=[pl.BlockSpec((NCHIP, tm, f4), lambda i: (0, i, 0)), pl.BlockSpec((NCHIP, f4, d), lambda i: (0, 2, 0)), pl.BlockSpec((tm, d), lambda i: (i, 0))],
        out_specs=[pl.BlockSpec((tm, d), lambda i: (i, 0))],
        out_shape=[jax.ShapeDtypeStruct((t, d), F32)],
        scratch_shapes=[],
        operands=(s4, wf, h),
        rider=rider,
    )
    return out, r_outs


def ffn_bwd(dho, ab, wf):
    t, d = dho.shape
    f4 = wf.shape[1] // 3
    tm = _tile(t, 1024)
    sub = _tile(tm, FFN_SUB, 16)

    def body(dho_ref, ab_ref, wg_ref, wu_ref, wd_ref, dn_ref, s_ref, dab_ref, do_sc):
        j = pl.program_id(1)

        @pl.when(j == 0)
        def _():
            do_sc[...] = (0.5 * dho_ref[...]).astype(BF)
            dn_ref[...] = jnp.zeros_like(dn_ref)

        for r in range(tm // sub):
            rows = slice(r * sub, (r + 1) * sub)
            ds = _dot(do_sc[rows, :], wd_ref[...], 1, 1)
            av, bv = ab_ref[0, rows, :].astype(F32), ab_ref[1, rows, :].astype(F32)
            sig = _sigmoid(av)
            sl = av * sig
            s_ref[rows, :] = (sl * bv).astype(BF)
            da = (ds * bv * (sig * (1.0 + av * (1.0 - sig)))).astype(BF)
            db = (ds * sl).astype(BF)
            dab_ref[0, rows, :] = da
            dab_ref[1, rows, :] = db
            dn_ref[rows, :] += _dot(da, wg_ref[...]) + _dot(db, wu_ref[...])

    row = lambda c: pl.BlockSpec((tm, c), lambda i, j: (i, 0))
    wspec = lambda ib: pl.BlockSpec((None, f4, d), lambda i, j: (j, ib, 0))
    ab_spec = pl.BlockSpec((2, None, tm, f4), lambda i, j: (0, j, i, 0))
    return pl.pallas_call(
        body,
        name="ffn_bwd",
        grid=(t // tm, NCHIP),
        in_specs=[row(d), ab_spec, wspec(0), wspec(1), wspec(2)],
        out_specs=[row(d), pl.BlockSpec((None, tm, f4), lambda i, j: (j, i, 0)), ab_spec],
        out_shape=[jax.ShapeDtypeStruct((t, d), F32), jax.ShapeDtypeStruct((NCHIP, t, f4), BF), jax.ShapeDtypeStruct((2, NCHIP, t, f4), BF)],
        scratch_shapes=[pltpu.VMEM((tm, d), BF)],
        compiler_params=_params(("parallel", "arbitrary")),
    )(dho, ab, wf, wf, wf)


def ffn_wgrads(n, dho, s4, dab):
    t, d = n.shape
    f4 = s4.shape[2]
    tn = _tile(d, 512)
    g_in = mmx("g_ffn_in", dab, n, grid=(2, NCHIP, d // tn), a_spec=pl.BlockSpec((None, None, t, f4), lambda wh, k, j: (wh, k, 0, 0)),
               b_spec=pl.BlockSpec((t, tn), lambda wh, k, j: (0, j)), o_spec=pl.BlockSpec((None, None, f4, tn), lambda wh, k, j: (k, wh, 0, j)),
               o_shape=(NCHIP, 2, f4, d), o_dtype=BF, ca=0, cb=0)
    g_out = mmx("g_ffn_out", s4, dho, grid=(NCHIP, d // tn), a_spec=pl.BlockSpec((None, t, f4), lambda k, j: (k, 0, 0)),
                b_spec=pl.BlockSpec((t, tn), lambda k, j: (0, j)), o_spec=pl.BlockSpec((None, f4, tn), lambda k, j: (k, 0, j)),
                o_shape=(NCHIP, f4, d), o_dtype=BF, ca=0, cb=0, scale=0.5)
    return jnp.concatenate([g_in.reshape(NCHIP, 2 * f4, d), g_out], axis=1)


def norm_cast(h, g):
    def fn(i, nt, rows, vecs, prevs, nexts):
        return [_rms(rows[0], vecs[0])], []
    return ew(fn, [h], [g], [(h.shape[1], BF)], tm=512, name="norm_cast")[0]


def _zero_if(cond, x):
    return jnp.where(cond, jnp.zeros_like(x), x)


def conv_a_fwd(sc3, w_sc):
    d = sc3.shape[1] // 3

    def fn(i, nt, rows, vecs, prevs, nexts):
        x, pv = rows[0], _zero_if(i == 0, prevs[0])
        v = x[:, d:2 * d] * x[:, 2 * d:]
        vp = pv[:, d:2 * d] * pv[:, 2 * d:]
        return [x[:, :d] * _conv_fwd(v, vp, vecs[0])], []

    return ew(fn, [sc3], [w_sc], [(d, BF)], tm=256, name="conv_a_fwd", prev_halo=(0,))[0]


def _softplus(x):
    e = jnp.exp(-jnp.abs(x))
    return jnp.maximum(x, 0.0) + jnp.where(e < 1e-4, e - 0.5 * e * e, jnp.log(1.0 + e))


def conv_m_fwd(xbc_raw, dt_raw, w_mc, b_mc, dt_bias):
    def fn(i, nt, rows, vecs, prevs, nexts):
        pre = _conv_fwd(rows[0], _zero_if(i == 0, prevs[0]), vecs[0]) + vecs[1]
        return [pre * _sigmoid(pre), _softplus(rows[1] + vecs[2])], []

    return ew(fn, [xbc_raw, dt_raw], [w_mc, b_mc, dt_bias], [(xbc_raw.shape[1], F32), (LANE, F32)], tm=256, name="conv_m_fwd",
              prev_halo=(0,))


def conv_m_bwd1(dxbc, xbc_raw, ddt, dt_raw, w_mc, b_mc, dt_bias):
    def fn(i, nt, rows, vecs, prevs, nexts):
        pre = _conv_fwd(rows[1], _zero_if(i == 0, prevs[0]), vecs[0]) + vecs[1]
        sig = _sigmoid(pre)
        dpre = rows[0] * (sig * (1.0 + pre * (1.0 - sig)))
        ddr = rows[2] * _sigmoid(rows[3] + vecs[2])
        return [dpre, ddr], [jnp.sum(ddr, axis=0, keepdims=True)]

    return ew(fn, [dxbc, xbc_raw, ddt, dt_raw], [w_mc, b_mc, dt_bias], [(dxbc.shape[1], F32), (LANE, BF)], [(1, LANE)], tm=256,
              name="conv_m_bwd1", prev_halo=(1,))


def conv_bwd2(dpre, x, w, name):
    kk = w.shape[0]

    def fn(i, nt, rows, vecs, prevs, nexts):
        dp, xv = rows[0], rows[1]
        nx = _zero_if(i == nt - 1, nexts[0])
        pv = _zero_if(i == 0, prevs[0])
        dx = None
        dws = []
        for k in range(kk):
            term = vecs[0][k:k + 1, :] * _shift_up(dp, nx, kk - 1 - k)
            dx = term if dx is None else dx + term
            dws.append(jnp.sum(dp * _shift_down(xv, pv, kk - 1 - k), axis=0, keepdims=True))
        return [dx], [jnp.concatenate(dws, axis=0), jnp.sum(dp, axis=0, keepdims=True)]

    c = x.shape[1]
    return ew(fn, [dpre, x], [w], [(c, BF)], [(kk, c), (1, c)], tm=256, name=name, prev_halo=(1,), next_halo=(0,))


def conv_a_bwd1(dya, sc3, w_sc):
    d = sc3.shape[1] // 3

    def fn(i, nt, rows, vecs, prevs, nexts):
        x, pv = rows[1], _zero_if(i == 0, prevs[0])
        v = x[:, d:2 * d] * x[:, 2 * d:]
        vp = pv[:, d:2 * d] * pv[:, 2 * d:]
        return [rows[0] * x[:, :d], rows[0] * _conv_fwd(v, vp, vecs[0]), v], []

    return ew(fn, [dya, sc3], [w_sc], [(d, F32), (d, BF), (d, F32)], tm=256, name="conv_a_bwd1", prev_halo=(1,))


def conv_a_bwd2(dcv, v, sc3, w_sc):
    d = v.shape[1]
    kk = w_sc.shape[0]

    def fn(i, nt, rows, vecs, prevs, nexts):
        dp, vv, x = rows
        nx = _zero_if(i == nt - 1, nexts[0])
        pv = _zero_if(i == 0, prevs[0])
        dv = None
        dws = []
        for k in range(kk):
            term = vecs[0][k:k + 1, :] * _shift_up(dp, nx, kk - 1 - k)
            dv = term if dv is None else dv + term
            dws.append(jnp.sum(dp * _shift_down(vv, pv, kk - 1 - k), axis=0, keepdims=True))
        return [dv * x[:, 2 * d:], dv * x[:, d:2 * d]], [jnp.concatenate(dws, axis=0)]

    return ew(fn, [dcv, v, sc3], [w_sc], [(d, BF), (d, BF)], [(kk, d)], tm=256, name="conv_a_bwd2", prev_halo=(1,), next_halo=(0,))


def _xdot(a, b, passes, split_lhs, ca=1, cb=0):
    parts, r = [], (a if split_lhs else b)
    for _ in range(passes):
        piece = r.astype(BF)
        parts.append(piece)
        r = r - piece.astype(F32)
    other = (b if split_lhs else a).astype(BF)
    acc = None
    for piece in parts:
        term = _dot(piece, other, ca, cb) if split_lhs else _dot(other, piece, ca, cb)
        acc = term if acc is None else acc + term
    return acc


def _ssd_common(xbc_ref, dt_ref, alog_ref, e_ref, w):
    ll = SSM_L
    xs = xbc_ref[:, 0:w]
    dtv = dt_ref[...]
    a_row = -jnp.exp(alog_ref[...])
    a = dtv * a_row
    row = lax.broadcasted_iota(jnp.int32, (ll, ll), 0)
    col = lax.broadcasted_iota(jnp.int32, (ll, ll), 1)
    tril = (row >= col).astype(F32)
    triu = (row <= col).astype(F32)
    acl = _xdot(tril, a, 3, False)
    acl_t = _xdot(a, triu, 3, True, 0, 0)
    e = e_ref[...]
    aclx = _xdot(acl, e, 3, True)
    dtx = _xdot(dtv, e, 2, True)
    last = aclx[ll - 1:ll, :]
    e_in = jnp.exp(aclx)
    e_end = jnp.exp(last - aclx)
    e_tot = jnp.exp(last)
    x = xs * dtx
    return dict(xs=xs, dtv=dtv, a_row=a_row, a=a, row=row, col=col, triu=triu, acl=acl, acl_t=acl_t, dtx=dtx, e_in=e_in, e_end=e_end,
                e_tot=e_tot, x=x)


def _decay(q, hh):
    diff = q["acl"][:, hh:hh + 1] - q["acl_t"][hh:hh + 1, :]
    return jnp.exp(jnp.where(q["row"] >= q["col"], diff, -jnp.inf))


def ssd_fwd(xbc, dt, z, a_log, d_exp, m_norm, e_mat, rider=None):
    t = xbc.shape[0]
    w = z.shape[1]
    gn = SSM_G * SSM_N
    gw = w // SSM_G
    ll, nn = SSM_L, SSM_N
    nc = t // ll
    cw = xbc.shape[1]

    def body(xbc_ref, dt_ref, z_ref, alog_ref, dexp_ref, mn_ref, e_ref, yn_ref, y_ref, sp_ref, s_sc):
        c = pl.program_id(0)

        @pl.when(c == 0)
        def _():
            s_sc[...] = jnp.zeros_like(s_sc)

        q = _ssd_common(xbc_ref, dt_ref, alog_ref, e_ref, w)
        xb = q["x"].astype(BF)
        xsb = (q["x"] * q["e_end"]).astype(BF)
        sp = s_sc[...]
        sp_ref[0] = sp
        spb = sp.astype(BF)
        lane = lax.broadcasted_iota(jnp.int32, (ll, LANE), 1)
        for g in range(SSM_G):
            lo = g * gw
            bg = xbc_ref[:, w + g * nn:w + (g + 1) * nn].astype(BF)
            cg = xbc_ref[:, w + gn + g * nn:w + gn + (g + 1) * nn].astype(BF)
            yoff = _dot(cg, spb[:, lo:lo + gw]) * q["e_in"][:, lo:lo + gw]
            s_sc[:, lo:lo + gw] = sp[:, lo:lo + gw] * q["e_tot"][:, lo:lo + gw] + _dot(bg, xsb[:, lo:lo + gw], 0, 0)
            cb = _dot(cg, bg, 1, 1)
            for pr in range(gw // LANE):
                l0 = lo + pr * LANE
                xp = xb[:, l0:l0 + LANE]
                ys = []
                for hh in (l0 // SSM_P, l0 // SSM_P + 1):
                    wm = (cb * _decay(q, hh)).astype(BF)
                    ys.append(_dot(wm, xp))
                ydiag = jnp.where(lane < SSM_P, ys[0], ys[1])
                y_ref[:, l0:l0 + LANE] = ydiag + yoff[:, pr * LANE:(pr + 1) * LANE] + dexp_ref[:, l0:l0 + LANE] * q["xs"][:, l0:l0 + LANE]
        zv = z_ref[...].astype(F32)
        yz = y_ref[...] * (zv * _sigmoid(zv))
        for g in range(SSM_G):
            lo = g * gw
            yn_ref[:, lo:lo + gw] = _rms(yz[:, lo:lo + gw], mn_ref[:, lo:lo + gw]).astype(BF)

    vec = lambda s: pl.BlockSpec(s, lambda c: (0, 0))
    return host_call(
        body,
        name="ssd_fwd",
        grid=(nc,),
        in_specs=[
            pl.BlockSpec((ll, cw), lambda c: (c, 0)), pl.BlockSpec((ll, LANE), lambda c: (c, 0)), pl.BlockSpec((ll, w), lambda c: (c, 0)),
            vec((1, LANE)), vec((1, w)), vec((1, w)), vec((LANE, w)),
        ],
        out_specs=[pl.BlockSpec((ll, w), lambda c: (c, 0)), pl.BlockSpec((ll, w), lambda c: (c, 0)), pl.BlockSpec((1, nn, w), lambda c: (c, 0, 0))],
        out_shape=[jax.ShapeDtypeStruct((t, w), BF), jax.ShapeDtypeStruct((t, w), F32), jax.ShapeDtypeStruct((nc, nn, w), F32)],
        scratch_shapes=[pltpu.VMEM((nn, w), F32)],
        operands=(xbc, dt, z, a_log, d_exp, m_norm, e_mat),
        rider=rider,
    )


def ssd_bwd(dyn, y, z, xbc, dt, sprev, a_log, d_exp, m_norm, e_mat, et_mat, rider=None):
    t = xbc.shape[0]
    w = z.shape[1]
    gn = SSM_G * SSM_N
    gw = w // SSM_G
    ll, nn = SSM_L, SSM_N
    nc = t // ll
    cw = xbc.shape[1]

    def body(dyn_ref, y_ref, z_ref, xbc_ref, dt_ref, sp_ref, alog_ref, dexp_ref, mn_ref, e_ref, et_ref,
             dz_ref, dxbc_ref, ddt_ref, dmn_ref, dd_ref, dal_ref, ds_sc, dy_sc, dx_sc):
        step = pl.program_id(0)

        @pl.when(step == 0)
        def _():
            ds_sc[...] = jnp.zeros_like(ds_sc)

        zv, yv = z_ref[...].astype(F32), y_ref[...]
        sg = _sigmoid(zv)
        sz = zv * sg
        yz = yv * sz
        dmn = []
        for g in range(SSM_G):
            lo = g * gw
            dseg, dmn_g = _rms_bwd(yz[:, lo:lo + gw], mn_ref[:, lo:lo + gw], dyn_ref[:, lo:lo + gw])
            dy_sc[:, lo:lo + gw] = dseg
            dmn.append(dmn_g)
        dmn = jnp.concatenate(dmn, axis=1)
        dyz = dy_sc[...]
        dz_ref[...] = (dyz * yv * (sg * (1.0 + zv * (1.0 - sg)))).astype(BF)
        dy = dyz * sz

        q = _ssd_common(xbc_ref, dt_ref, alog_ref, e_ref, w)
        x = q["x"]
        xb = x.astype(BF)
        xsb = (x * q["e_end"]).astype(BF)
        sp = sp_ref[0]
        spb = sp.astype(BF)
        dsn = ds_sc[...]
        dsnb = dsn.astype(BF)
        dyb = dy.astype(BF)
        lane = lax.broadcasted_iota(jnp.int32, (ll, LANE), 1)
        lane1 = lax.broadcasted_iota(jnp.int32, (1, LANE), 1)
        sub1 = lax.broadcasted_iota(jnp.int32, (LANE, 1), 0)
        dacl = jnp.zeros((ll, LANE), F32)
        dacl_t = jnp.zeros((LANE, ll), F32)
        d_ein, d_eend, d_etot = [], [], []
        for g in range(SSM_G):
            lo = g * gw
            sl = slice(lo, lo + gw)
            bg = xbc_ref[:, w + g * nn:w + (g + 1) * nn].astype(BF)
            cg = xbc_ref[:, w + gn + g * nn:w + gn + (g + 1) * nn].astype(BF)
            zg = _dot(cg, spb[:, sl])
            dzz = (dy[:, sl] * q["e_in"][:, sl]).astype(BF)
            d_ein.append(dy[:, sl] * zg)
            dcg = _dot(dzz, spb[:, sl], 1, 1)
            ds_sc[:, sl] = _dot(cg, dzz, 0, 0) + dsn[:, sl] * q["e_tot"][:, sl]
            d_etot.append(jnp.sum(dsn[:, sl] * sp[:, sl], axis=0, keepdims=True))
            dbg = _dot(xsb[:, sl], dsnb[:, sl], 1, 1)
            dxs_g = _dot(bg, dsnb[:, sl])
            d_eend.append(dxs_g * x[:, sl])
            cb = _dot(cg, bg, 1, 1)
            dcb = jnp.zeros((ll, ll), F32)
            for pr in range(gw // LANE):
                l0 = lo + pr * LANE
                xp = xb[:, l0:l0 + LANE]
                dyp = dyb[:, l0:l0 + LANE]
                dxp = []
                for hi, hh in enumerate((l0 // SSM_P, l0 // SSM_P + 1)):
                    lm = _decay(q, hh)
                    wm = (cb * lm).astype(BF)
                    in_head = (lane < SSM_P) if hi == 0 else (lane >= SSM_P)
                    dwm = _dot(jnp.where(in_head, dyp, jnp.zeros_like(dyp)), xp, 1, 1)
                    dxp.append(_dot(wm, dyp, 0, 0))
                    dlm = dwm * lm
                    dcb = dcb + dlm
                    dd = dlm * cb
                    dacl = dacl + jnp.sum(dd, axis=1, keepdims=True) * (lane1 == hh).astype(F32)
                    dacl_t = dacl_t + (sub1 == hh).astype(F32) * jnp.sum(dd, axis=0, keepdims=True)
                dx_sc[:, l0:l0 + LANE] = jnp.where(lane < SSM_P, dxp[0], dxp[1]) + dxs_g[:, pr * LANE:(pr + 1) * LANE] * q["e_end"][:, l0:l0 + LANE]
            dcbb = dcb.astype(BF)
            dxbc_ref[:, w + g * nn:w + (g + 1) * nn] = dbg + _dot(dcbb, cg, 0, 0)
            dxbc_ref[:, w + gn + g * nn:w + gn + (g + 1) * nn] = dcg + _dot(dcbb, bg)
        d_ein = jnp.concatenate(d_ein, axis=1) * q["e_in"]
        d_eend = jnp.concatenate(d_eend, axis=1) * q["e_end"]
        d_etot = jnp.concatenate(d_etot, axis=1) * q["e_tot"]
        et = et_ref[...]
        last_add = jnp.sum(d_eend, axis=0, keepdims=True) + d_etot
        last_add = _xdot(jnp.broadcast_to(last_add, (HALO, w)), et, 2, True)[0:1]
        row1 = lax.broadcasted_iota(jnp.int32, (ll, LANE), 0)
        dacl = dacl + _xdot(d_ein - d_eend, et, 2, True) + jnp.where(row1 == ll - 1, last_add, 0.0)
        da = _xdot(q["triu"], dacl, 2, False) - _xdot(q["triu"], dacl_t, 2, False, 1, 1)
        dxv = dx_sc[...]
        dxbc_ref[:, 0:w] = dexp_ref[...] * dy + dxv * q["dtx"]
        ddt_ref[...] = _xdot(dxv * q["xs"], et, 2, True) + da * q["a_row"]
        dal = jnp.sum(da * q["dtv"], axis=0, keepdims=True) * q["a_row"]
        ddv = jnp.sum(dy * q["xs"], axis=0, keepdims=True)
        ddv = _xdot(jnp.broadcast_to(ddv, (HALO, w)), et, 2, True)[0:1]
        _accumulate(dmn_ref, dmn, step == 0)
        _accumulate(dd_ref, ddv, step == 0)
        _accumulate(dal_ref, dal, step == 0)

    rev = lambda c_: pl.BlockSpec((ll, c_), lambda s: (nc - 1 - s, 0))
    vec = lambda s_: pl.BlockSpec(s_, lambda s: (0, 0))
    return host_call(
        body,
        name="ssd_bwd",
        grid=(nc,),
        in_specs=[
            rev(w), rev(w), rev(w), rev(cw), rev(LANE), pl.BlockSpec((1, nn, w), lambda s: (nc - 1 - s, 0, 0)),
            vec((1, LANE)), vec((1, w)), vec((1, w)), vec((LANE, w)), vec((w, LANE)),
        ],
        out_specs=[rev(w), rev(cw), rev(LANE), vec((1, w)), vec((1, LANE)), vec((1, LANE))],
        out_shape=[
            jax.ShapeDtypeStruct((t, w), BF), jax.ShapeDtypeStruct((t, cw), F32), jax.ShapeDtypeStruct((t, LANE), F32),
            jax.ShapeDtypeStruct((1, w), F32), jax.ShapeDtypeStruct((1, LANE), F32), jax.ShapeDtypeStruct((1, LANE), F32),
        ],
        scratch_shapes=[pltpu.VMEM((nn, w), F32), pltpu.VMEM((ll, w), F32), pltpu.VMEM((ll, w), F32)],
        operands=(dyn, y, z, xbc, dt, sprev, a_log, d_exp, m_norm, e_mat, et_mat),
        rider=rider,
    )


def _w1024_spec(d, nblk, iblk):
    r = nblk * (d // NCHIP)
    return pl.BlockSpec((NCHIP, r, d), lambda i: (0, iblk // nblk, 0))


def _whole(ref):
    v = ref[...]
    return v.reshape(v.shape[0] * v.shape[1], v.shape[2])


def mix_out_fwd(ya_in, yn, gates, h, w1024, rider=None):
    t, d = h.shape
    tm = _tile(t, 256)

    def body(ya_ref, yn_ref, g_ref, h_ref, wm_ref, wa_ref, wo_ref, ho_ref, oa_ref, om_ref, mg_ref):
        y_a = _dot(ya_ref[...], _whole(wa_ref))
        y_m = _dot(yn_ref[...], _whole(wm_ref))
        oa_ref[...] = y_a
        om_ref[...] = y_m
        gv = g_ref[...].astype(F32)
        mg = (_sigmoid(gv[:, :d]) * y_a + _sigmoid(gv[:, d:]) * y_m).astype(BF)
        mg_ref[...] = mg
        ho_ref[...] = h_ref[...] + _dot(mg, _whole(wo_ref))

    row = lambda c: pl.BlockSpec((tm, c), lambda i: (i, 0))
    return host_call(
        body,
        name="mix_out_fwd",
        grid=(t // tm,),
        in_specs=[row(d), row(2 * d), row(2 * d), row(d), _w1024_spec(d, 2, 0), _w1024_spec(d, 1, 2), _w1024_spec(d, 1, 3)],
        out_specs=[row(d), row(d), row(d), row(d)],
        out_shape=[jax.ShapeDtypeStruct((t, d), F32), jax.ShapeDtypeStruct((t, d), F32), jax.ShapeDtypeStruct((t, d), F32),
                   jax.ShapeDtypeStruct((t, d), BF)],
        scratch_shapes=[],
        operands=(ya_in, yn, gates, h, w1024, w1024, w1024),
        rider=rider,
    )


def mix_out_bwd(dh, gates, y_a, y_m, w1024):
    t, d = dh.shape
    tm = _tile(t, 256)

    def body(dh_ref, g_ref, ya_ref, ym_ref, wm_ref, wa_ref, wo_ref, dg_ref, dya_ref, dyn_ref, da_ref, dm_ref):
        dmg = _dot(dh_ref[...].astype(BF), _whole(wo_ref), 1, 1)
        gv = g_ref[...].astype(F32)
        sa, sm = _sigmoid(gv[:, :d]), _sigmoid(gv[:, d:])
        dg_ref[:, :d] = (dmg * ya_ref[...] * sa * (1.0 - sa)).astype(BF)
        dg_ref[:, d:] = (dmg * ym_ref[...] * sm * (1.0 - sm)).astype(BF)
        da = (dmg * sa).astype(BF)
        dm = (dmg * sm).astype(BF)
        da_ref[...] = da
        dm_ref[...] = dm
        dya_ref[...] = _dot(da, _whole(wa_ref), 1, 1)
        dyn_ref[...] = _dot(dm, _whole(wm_ref), 1, 1)

    row = lambda c: pl.BlockSpec((tm, c), lambda i: (i, 0))
    return pl.pallas_call(
        body,
        name="mix_out_bwd",
        grid=(t // tm,),
        in_specs=[row(d), row(2 * d), row(d), row(d), _w1024_spec(d, 2, 0), _w1024_spec(d, 1, 2), _w1024_spec(d, 1, 3)],
        out_specs=[row(2 * d), row(d), row(2 * d), row(d), row(d)],
        out_shape=[jax.ShapeDtypeStruct((t, 2 * d), BF), jax.ShapeDtypeStruct((t, d), F32), jax.ShapeDtypeStruct((t, 2 * d), F32),
                   jax.ShapeDtypeStruct((t, d), BF), jax.ShapeDtypeStruct((t, d), BF)],
        compiler_params=_params(("parallel",)),
    )(dh, gates, y_a, y_m, w1024, w1024, w1024)


def norm_bwd_add(dh, h, g, dn):
    def fn(i, nt, rows, vecs, prevs, nexts):
        dx, dg = _rms_bwd(rows[1], vecs[0], rows[2])
        return [rows[0] + dx], [dg]
    d = h.shape[1]
    return ew(fn, [dh, h, dn], [g], [(d, F32)], [(1, d)], tm=512, name="norm_bwd_add")


def _pe(p, wpp_ref):
    pb = p.astype(BF)
    return jnp.concatenate([_dot(pb, wpp_ref[k]) for k in range(NCHIP)], axis=1)


def ple_fwd(h, g, p, w1024, wpp):
    t, d = h.shape
    tm = _tile(t, 512)

    def body(h_ref, g_ref, p_ref, wg_ref, wp_ref, ho_ref):
        hv = h_ref[...]
        gate = _sigmoid(_dot(_rms(hv, g_ref[...]).astype(BF), _whole(wg_ref)))
        ho_ref[...] = hv + gate * _pe(p_ref[...], wp_ref)

    row = lambda c: pl.BlockSpec((tm, c), lambda i: (i, 0))
    wpp_spec = pl.BlockSpec(wpp.shape, lambda i: (0, 0, 0))
    return pl.pallas_call(
        body,
        name="ple_fwd",
        grid=(t // tm,),
        in_specs=[row(d), pl.BlockSpec((1, d), lambda i: (0, 0)), row(p.shape[1]), _w1024_spec(d, 1, 4), wpp_spec],
        out_specs=row(d),
        out_shape=jax.ShapeDtypeStruct((t, d), F32),
        compiler_params=_params(("parallel",)),
    )(h, g, p, w1024, wpp)


def ple_bwd(dho, h, g, p, w1024, wpp):
    t, d = h.shape
    tm = _tile(t, 512)

    def body(dho_ref, h_ref, g_ref, p_ref, wg_ref, wp_ref, dh_ref, dg_ref, n_ref, dgp_ref, dpe_ref):
        hv, dv = h_ref[...], dho_ref[...]
        n = _rms(hv, g_ref[...]).astype(BF)
        n_ref[...] = n
        wg = _whole(wg_ref)
        gate = _sigmoid(_dot(n, wg))
        pe = _pe(p_ref[...], wp_ref)
        dpe_ref[...] = (dv * gate).astype(BF)
        dgp = (dv * pe * gate * (1.0 - gate)).astype(BF)
        dgp_ref[...] = dgp
        dx, dg = _rms_bwd(hv, g_ref[...], _dot(dgp, wg, 1, 1))
        dh_ref[...] = dv + dx
        _accumulate(dg_ref, dg, pl.program_id(0) == 0)

    row = lambda c: pl.BlockSpec((tm, c), lambda i: (i, 0))
    wpp_spec = pl.BlockSpec(wpp.shape, lambda i: (0, 0, 0))
    return pl.pallas_call(
        body,
        name="ple_bwd",
        grid=(t // tm,),
        in_specs=[row(d), row(d), pl.BlockSpec((1, d), lambda i: (0, 0)), row(p.shape[1]), _w1024_spec(d, 1, 4), wpp_spec],
        out_specs=[row(d), pl.BlockSpec((1, d), lambda i: (0, 0)), row(d), row(d), row(d)],
        out_shape=[jax.ShapeDtypeStruct((t, d), F32), jax.ShapeDtypeStruct((1, d), F32), jax.ShapeDtypeStruct((t, d), BF),
                   jax.ShapeDtypeStruct((t, d), BF), jax.ShapeDtypeStruct((t, d), BF)],
        compiler_params=_params(("arbitrary",)),
    )(dho, h, g, p, w1024, wpp)


def loss_bwd(h, g, target):
    d = h.shape[1]

    def fn(i, nt, rows, vecs, prevs, nexts):
        err = _rms(rows[0], vecs[0]) - rows[1]
        dx, dg = _rms_bwd(rows[0], vecs[0], err * (1.0 / d))
        return [dx], [jnp.sum(err * err, axis=0, keepdims=True) * (0.5 / d), dg]

    return ew(fn, [h, target], [g], [(d, F32)], [(1, d), (1, d)], tm=512, name="loss_bwd")


def adamw(w, g, m, v, name):
    c1, c2 = 1.0 / (1.0 - ADAM_B1 ** ADAM_STEP), 1.0 / (1.0 - ADAM_B2 ** ADAM_STEP)

    def fn(i, nt, rows, vecs, prevs, nexts):
        wv, gv, mv, vv = rows
        mn = ADAM_B1 * mv + (1.0 - ADAM_B1) * gv
        vn = ADAM_B2 * vv + (1.0 - ADAM_B2) * (gv * gv)
        delta = -ADAM_LR * ((mn * c1) / (jnp.sqrt(vn * c2) + ADAM_EPS) + ADAM_WD * wv)
        return [delta, mn, vn], []

    c = w.shape[1]
    return ew(fn, [w, g, m, v], [], [(c, F32)] * 3, tm=_row_tile(w.shape[0], c, HALO), name=name)


def _place():
    return lax.axis_index("x"), lax.axis_index("y"), lax.axis_index("c")


def _other_chips(x, y):
    return [(1 - x, y), (x, 1 - y), (1 - x, 1 - y)]


ANY = pl.BlockSpec(memory_space=pl.ANY)


def _comm_call(body, name, ins, out_shapes, n_sems, aliases=None):
    return pl.pallas_call(
        body,
        name=name,
        in_specs=[ANY] * len(ins),
        out_specs=[ANY] * len(out_shapes),
        out_shape=out_shapes,
        scratch_shapes=[pltpu.SemaphoreType.DMA((n_sems,)), pltpu.SemaphoreType.DMA((n_sems,))],
        input_output_aliases=aliases or {},
    )(*ins)


def gather_rider(packs):
    nt = len(packs)

    def pieces(ins, outs, send_sems, recv_sems):
        x, y, cc = _place()
        chips = _other_chips(x, y)
        sibling = (x, y, 1 - cc)
        k_me = 2 * x + y

        def copy(k, src, dst, to):
            return pltpu.make_async_remote_copy(src_ref=src, dst_ref=dst, send_sem=send_sems.at[k], recv_sem=recv_sems.at[k],
                                                device_id=to, device_id_type=MESH)

        sends, forwards, arrivals = [], [], []
        for ti in range(nt):
            for j, (px, py) in enumerate(chips):
                sends.append(copy(7 * ti + j, ins[ti].at[cc], outs[ti].at[k_me, cc], (px, py, cc)))
                landed = outs[ti].at[2 * px + py, cc]
                forwards.append((copy(7 * ti + j, landed, landed, (px, py, cc)), copy(7 * ti + 3 + j, landed, landed, sibling)))
                passed = outs[ti].at[2 * px + py, 1 - cc]
                arrivals.append(copy(7 * ti + 3 + j, passed, passed, sibling))
            sends.append(copy(7 * ti + 6, ins[ti], outs[ti].at[k_me], sibling))
            own = outs[ti].at[k_me]
            arrivals.append(copy(7 * ti + 6, own, own, sibling))
        return sends, forwards, arrivals

    def start(*parts):
        for cp in pieces(*parts)[0]:
            cp.start()

    def mid(*parts):
        for landed, forward in pieces(*parts)[1]:
            landed.wait_recv()
            forward.start()

    def finish(*parts):
        sends, forwards, arrivals = pieces(*parts)
        for cp in arrivals:
            cp.wait_recv()
        for cp in sends + [f for _, f in forwards]:
            cp.wait_send()

    return Rider(packs, [jax.ShapeDtypeStruct((NCHIP,) + p.shape, p.dtype) for p in packs], 7 * nt, start, finish, mid)


def swap_packs(gs, name):
    nt = len(gs)
    hl = gs[0].shape[1] // 2

    def body(*refs):
        ins, outs, (send_sems, recv_sems) = refs[:nt], refs[nt:2 * nt], refs[2 * nt:]
        x, y, cc = _place()
        theirs = pl.ds((1 - cc) * hl, hl)
        cps = [pltpu.make_async_remote_copy(src_ref=ins[ti].at[:, theirs], dst_ref=outs[ti], send_sem=send_sems.at[ti], recv_sem=recv_sems.at[ti],
                                            device_id=(x, y, 1 - cc), device_id_type=MESH) for ti in range(nt)]
        for cp in cps:
            cp.start()
        for cp in cps:
            cp.wait()

    return _comm_call(body, name, gs, [jax.ShapeDtypeStruct((NCHIP, hl) + g.shape[2:], g.dtype) for g in gs], nt)


def scatter_packs(cs, name):
    return scatter_rider(cs).standalone(name)


def scatter_rider(cs):
    nt = len(cs)

    def copies(ins, outs, send_sems, recv_sems):
        x, y, cc = _place()
        cps = []
        for ti in range(nt):
            for j, (px, py) in enumerate(_other_chips(x, y)):
                cps.append(pltpu.make_async_remote_copy(src_ref=ins[ti].at[2 * px + py], dst_ref=outs[ti].at[j], send_sem=send_sems.at[3 * ti + j],
                                                        recv_sem=recv_sems.at[3 * ti + j], device_id=(px, py, cc), device_id_type=MESH))
        return cps

    def start(*parts):
        for cp in copies(*parts):
            cp.start()

    def finish(*parts):
        for cp in copies(*parts):
            cp.wait()

    return Rider(cs, [jax.ShapeDtypeStruct((3,) + c_.shape[1:], c_.dtype) for c_ in cs], 3 * nt, start, finish)


def join_packs(fulls, name):
    nt = len(fulls)
    hl = fulls[0].shape[0] // 2

    def body(*refs):
        ins, outs, (send_sems, recv_sems) = refs[:nt], refs[nt:2 * nt], refs[2 * nt:]
        x, y, cc = _place()
        mine = pl.ds(cc * hl, hl)
        cps = [pltpu.make_async_remote_copy(src_ref=ins[ti].at[mine], dst_ref=outs[ti].at[mine], send_sem=send_sems.at[ti], recv_sem=recv_sems.at[ti],
                                            device_id=(x, y, 1 - cc), device_id_type=MESH) for ti in range(nt)]
        for cp in cps:
            cp.start()
        for cp in cps:
            cp.wait()

    return _comm_call(body, name, fulls, [jax.ShapeDtypeStruct(f.shape, f.dtype) for f in fulls], nt, aliases={ti: ti for ti in range(nt)})


def add_sibling(g, recv, name):
    _, nl, r, c = g.shape
    hl = nl // 2
    tm, tc = _tile2(r, c)

    def body(g_ref, r_ref, o_ref):
        o_ref[...] = (g_ref[...].astype(F32) + r_ref[...].astype(F32)).astype(o_ref.dtype)

    blk = (None, None, tm, tc)
    return pl.pallas_call(
        body,
        name=name,
        grid=(NCHIP, hl, r // tm, c // tc),
        in_specs=[pl.BlockSpec(blk, lambda k, l, i, j: (k, lax.axis_index("c") * hl + l, i, j)), pl.BlockSpec(blk, lambda k, l, i, j: (k, l, i, j))],
        out_specs=pl.BlockSpec(blk, lambda k, l, i, j: (k, l, i, j)),
        out_shape=jax.ShapeDtypeStruct(recv.shape, BF),
        compiler_params=_params(("parallel",) * 4),
    )(g, recv)


def add_chips(cs, got, nl, name):
    _, hl, r, c = cs.shape
    tm, tc = _tile2(r, c)

    def body(own_ref, got_ref, o_ref):
        o_ref[...] = own_ref[...].astype(F32) + got_ref[0].astype(F32) + got_ref[1].astype(F32) + got_ref[2].astype(F32)

    return pl.pallas_call(
        body,
        name=name,
        grid=(hl, r // tm, c // tc),
        in_specs=[pl.BlockSpec((None, None, tm, tc), lambda l, i, j: (2 * lax.axis_index("x") + lax.axis_index("y"), l, i, j)),
                  pl.BlockSpec((3, None, tm, tc), lambda l, i, j: (0, l, i, j))],
        out_specs=pl.BlockSpec((None, tm, tc), lambda l, i, j: (lax.axis_index("c") * hl + l, i, j)),
        out_shape=jax.ShapeDtypeStruct((nl, r, c), F32),
        compiler_params=_params(("parallel",) * 3),
    )(cs, got)


def all_gather_xy(shard, name):
    r, c = shard.shape
    hr = r // 2
    assert r % 32 == 0

    def body(x_ref, out_ref, send_sems, recv_sems, local_sem):
        x, y, cc = _place()
        chips = _other_chips(x, y)
        mine = pl.ds(pl.multiple_of(cc * hr, 16), hr)
        theirs = pl.ds(pl.multiple_of((1 - cc) * hr, 16), hr)
        k_me = 2 * x + y

        def copy(k, src, dst, to):
            return pltpu.make_async_remote_copy(src_ref=src, dst_ref=dst, send_sem=send_sems.at[k], recv_sem=recv_sems.at[k],
                                                device_id=to, device_id_type=MESH)

        own = pltpu.make_async_copy(x_ref, out_ref.at[k_me], local_sem)
        own.start()
        first = [copy(j, x_ref.at[mine], out_ref.at[k_me, mine], (*chip, cc)) for j, chip in enumerate(chips)]
        for cp in first:
            cp.start()
        passed = []
        for j, (px, py) in enumerate(chips):
            landed = out_ref.at[2 * px + py, mine]
            copy(j, landed, landed, (px, py, cc)).wait_recv()
            fw = copy(3 + j, landed, landed, (x, y, 1 - cc))
            fw.start()
            passed.append(fw)
        for j, (px, py) in enumerate(chips):
            landed = out_ref.at[2 * px + py, theirs]
            copy(3 + j, landed, landed, (x, y, 1 - cc)).wait_recv()
        for cp in first + passed:
            cp.wait_send()
        own.wait()

    return pl.pallas_call(
        body,
        name=name,
        in_specs=[ANY],
        out_specs=ANY,
        out_shape=jax.ShapeDtypeStruct((NCHIP, r, c), shard.dtype),
        scratch_shapes=[pltpu.SemaphoreType.DMA((6,)), pltpu.SemaphoreType.DMA((6,)), pltpu.SemaphoreType.DMA],
    )(shard)


def all_gather_8(block, name):
    m, c = block.shape

    def body(x_ref, out_ref, send_sems, recv_sems, local_sem):
        x, y, cc = _place()
        me, sibling = (x, y, cc), (x, y, 1 - cc)
        chips = _other_chips(x, y)

        def rows(px, py, pc):
            return out_ref.at[4 * px + 2 * py + pc]

        def copy(k, blk, to, src=None):
            return pltpu.make_async_remote_copy(src_ref=rows(*blk) if src is None else src, dst_ref=rows(*blk), send_sem=send_sems.at[k],
                                                recv_sem=recv_sems.at[k], device_id=to, device_id_type=MESH)

        mine = pltpu.make_async_copy(x_ref, rows(*me), local_sem)
        mine.start()
        first = [copy(0, me, sibling, src=x_ref)]
        first += [copy(1 + j, me, (*chip, cc), src=x_ref) for j, chip in enumerate(chips)]
        for cp in first:
            cp.start()
        passed = [copy(4 + j, (*chip, cc), sibling) for j, chip in enumerate(chips)]
        for j, chip in enumerate(chips):
            copy(1 + j, (*chip, cc), me).wait_recv()
            passed[j].start()
        copy(0, sibling, me).wait_recv()
        for j, chip in enumerate(chips):
            copy(4 + j, (*chip, 1 - cc), me).wait_recv()
        for cp in first + passed:
            cp.wait_send()
        mine.wait()

    return pl.pallas_call(
        body,
        name=name,
        in_specs=[pl.BlockSpec(memory_space=pltpu.VMEM)],
        out_specs=pl.BlockSpec(memory_space=pltpu.VMEM),
        out_shape=jax.ShapeDtypeStruct((8, m, c), block.dtype),
        scratch_shapes=[pltpu.SemaphoreType.DMA((7,)), pltpu.SemaphoreType.DMA((7,)), pltpu.SemaphoreType.DMA],
        compiler_params=pltpu.CompilerParams(vmem_limit_bytes=VMEM_LIMIT),
    )(block)


def add_parts(parts, out_dtype, name, tm=512):
    def fn(i, nt, rows, vecs, prevs, nexts):
        acc = rows[0]
        for r_ in rows[1:]:
            acc = acc + r_
        return [acc], []
    r, c = parts[0].shape
    return ew(fn, list(parts), [], [(c, out_dtype)], tm=_tile(r, tm, 16), name=name)[0]


SMALL_SHARDED = ("sc_conv_w", "m_conv_w")
SMALL_REPL = ("ffn1_norm", "mix_norm", "m_conv_b", "m_dt_bias", "m_A_log", "m_D", "m_norm", "ffn2_norm", "ple_norm", "final_norm")
BIG = ("ffn1_wg", "ffn1_wu", "ffn1_wd", "w_in", "sc_w_out", "m_w_out", "w_o", "ffn2_wg", "ffn2_wu", "ffn2_wd", "ple_w_gate", "ple_w_proj")
TRANSPOSED = ("ffn1_wg", "ffn1_wu", "ffn2_wg", "ffn2_wu", "w_in")
ORDER = ("ffn1_norm", "ffn1_wg", "ffn1_wu", "ffn1_wd", "mix_norm", "w_in", "sc_conv_w", "sc_w_out", "m_conv_w", "m_conv_b", "m_dt_bias",
         "m_A_log", "m_D", "m_norm", "m_w_out", "w_o", "ffn2_norm", "ffn2_wg", "ffn2_wu", "ffn2_wd", "ple_norm", "ple_w_gate", "ple_w_proj",
         "final_norm")


def _pack(arrs, cols, row_mult):
    flat = jnp.concatenate([a.reshape(-1) for a in arrs])
    n = flat.shape[0]
    rows = -(-n // cols)
    rows = -(-rows // row_mult) * row_mult
    return jnp.pad(flat, (0, rows * cols - n)).reshape(rows, cols)


def _unpack(flat2d, shapes):
    flat = flat2d.reshape(-1)
    out, off = [], 0
    for s in shapes:
        n = int(np.prod(s))
        out.append(flat[off:off + n].reshape(s))
        off += n
    return out


def _row_cat(arrs, dtype):
    return jnp.concatenate([a.astype(dtype) for a in arrs], axis=1)


def kernel(x, p, ffn1_norm, ffn1_wg, ffn1_wu, ffn1_wd, mix_norm, w_in, sc_conv_w, sc_w_out, m_conv_w, m_conv_b, m_dt_bias, m_A_log, m_D, m_norm, m_w_out, w_o, ffn2_norm, ffn2_wg, ffn2_wu, ffn2_wd, ple_norm, ple_w_gate, ple_w_proj, final_norm, loss_target, m_ffn1_norm, m_ffn1_wg, m_ffn1_wu, m_ffn1_wd, m_mix_norm, m_w_in, m_sc_conv_w, m_sc_w_out, m_m_conv_w, m_m_conv_b, m_m_dt_bias, m_m_A_log, m_m_D, m_m_norm, m_m_w_out, m_w_o, m_ffn2_norm, m_ffn2_wg, m_ffn2_wu, m_ffn2_wd, m_ple_norm, m_ple_w_gate, m_ple_w_proj, m_final_norm, v_ffn1_norm, v_ffn1_wg, v_ffn1_wu, v_ffn1_wd, v_mix_norm, v_w_in, v_sc_conv_w, v_sc_w_out, v_m_conv_w, v_m_conv_b, v_m_dt_bias, v_m_A_log, v_m_D, v_m_norm, v_m_w_out, v_w_o, v_ffn2_norm, v_ffn2_wg, v_ffn2_wu, v_ffn2_wd, v_ple_norm, v_ple_w_gate, v_ple_w_proj, v_final_norm):
    args = dict(locals())
    wts = {n: args[n] for n in ORDER}
    mom = {n: args["m_" + n] for n in ORDER}
    vel = {n: args["v_" + n] for n in ORDER}

    depth = ffn1_norm.shape[0]
    d = x.shape[-1]
    w = 2 * d
    hh = w // SSM_P
    cw = w + 2 * SSM_G * SSM_N
    d4 = d // NCHIP
    my_x, my_y, my_c = _place()
    k_me = 2 * my_x + my_y

    tr = lambda a: jnp.swapaxes(a, 1, 2)
    ffn_t = [_row_cat([tr(wg_), tr(wu_), wd_], BF) for wg_, wu_, wd_ in ((ffn1_wg, ffn1_wu, ffn1_wd), (ffn2_wg, ffn2_wu, ffn2_wd))]
    w1024_l = _row_cat([m_w_out, sc_w_out, w_o, ple_w_gate], BF)
    win_l, wpp_l = tr(w_in).astype(BF), ple_w_proj.astype(BF)
    halves = lambda a: a.reshape(2, a.shape[0] // 2, a.shape[1])
    whole = lambda g: g.reshape(NCHIP, g.shape[2] * 2, g.shape[3])

    def pieces(l):
        return {"small": [halves(w1024_l[l]), halves(wpp_l[l])], "win": [halves(win_l[l])], "f1": [halves(ffn_t[0][l])], "f2": [halves(ffn_t[1][l])]}

    small_local = [sc_conv_w, m_conv_w]
    gathered_s = all_gather_xy(_pack(small_local, LANE, 32), "gather_conv_weights")
    per_shard_s = [_unpack(gathered_s[k], [a.shape for a in small_local]) for k in range(NCHIP)]
    sc_conv_full = jnp.concatenate([per_shard_s[k][0] for k in range(NCHIP)], axis=2)
    m_conv_full = jnp.concatenate([per_shard_s[k][1] for k in range(NCHIP)], axis=2)

    pad_h = lambda a: jnp.pad(a, ((0, 0), (0, LANE - hh)))
    dt_bias_p, a_log_p = pad_h(m_dt_bias), pad_h(m_A_log)
    d_exp = jnp.repeat(m_D, SSM_P, axis=1)
    e_mat = (jnp.arange(w)[None, :] // SSM_P == jnp.arange(LANE)[:, None]).astype(F32)
    et_mat = e_mat.T
    o_z, o_xbc, o_dt, o_g = 3 * d, 5 * d, 5 * d + cw, 5 * d + cw + hh

    def layer_weights(got):
        wt = {"w1024": whole(got["small"][0]), "wpp": whole(got["small"][1]), "f1": whole(got["f1"][0]), "f2": whole(got["f2"][0])}
        wi = whole(got["win"][0]).reshape(-1, d)
        wt["sc3"], wt["z"], wt["xbc"], wt["g2"] = wi[:o_z], wi[o_z:o_xbc], wi[o_xbc:o_dt], wi[o_g:o_g + 2 * d]
        wt["dt"] = jnp.pad(wi[o_dt:o_g], ((0, LANE - hh), (0, 0)))
        wt["in_p"] = jnp.concatenate([wt["sc3"], wt["z"], wt["xbc"], wt["g2"], wt["dt"]], axis=0)
        return wt

    first = pieces(0)
    order = ("small", "win", "f1", "f2")
    flat = gather_rider([a for k in order for a in first[k]]).standalone("gather_weights")
    got, pos = {}, 0
    for k in order:
        got[k] = flat[pos:pos + len(first[k])]
        pos += len(first[k])
    wts_l = [layer_weights(got)]

    h = x[0]
    saved = []
    for i in range(depth):
        s, wt = {}, wts_l[i]
        nxt = pieces(i + 1) if i + 1 < depth else None
        ride = lambda k: gather_rider(nxt[k]) if nxt else None
        got = {}
        s["h0"] = h
        (s["ab1"], s4, s["n1"]), got["small"] = ffn_up(h, ffn1_norm[i:i + 1], wt["f1"], rider=ride("small"))
        h, _ = ffn_down(s4, wt["f1"], h)
        s["h1"] = h
        u = norm_cast(h, mix_norm[i:i + 1])
        s["u"] = u
        s["sc3"] = mm(u, wt["sc3"], tb=True, out_dtype=BF, name="proj_sc")
        s["z"] = mm(u, wt["z"], tb=True, out_dtype=BF, name="proj_z")
        s["xbc_raw"] = mm(u, wt["xbc"], tb=True, out_dtype=BF, name="proj_xbc")
        s["gates"] = mm(u, wt["g2"], tb=True, out_dtype=BF, name="proj_gates")
        s["dt_raw"] = mm(u, wt["dt"], tb=True, name="proj_dt")
        s["ya_in"] = conv_a_fwd(s["sc3"], sc_conv_full[i])
        s["xbc"], s["dt"] = conv_m_fwd(s["xbc_raw"], s["dt_raw"], m_conv_full[i], m_conv_b[i:i + 1], dt_bias_p[i:i + 1])
        (s["yn"], s["y"], s["sprev"]), got["win"] = ssd_fwd(s["xbc"], s["dt"], s["z"], a_log_p[i:i + 1], d_exp[i:i + 1], m_norm[i:i + 1], e_mat,
                                                            rider=ride("win"))
        (h, s["y_a"], s["y_m"], s["merged"]), got["f1"] = mix_out_fwd(s["ya_in"], s["yn"], s["gates"], h, wt["w1024"], rider=ride("f1"))
        s["h2"] = h
        (s["ab2"], s4, s["n2"]), got["f2"] = ffn_up(h, ffn2_norm[i:i + 1], wt["f2"], rider=ride("f2"))
        h, _ = ffn_down(s4, wt["f2"], h)
        s["h3"] = h
        h = ple_fwd(h, ple_norm[i:i + 1], p[i, 0], wt["w1024"], wt["wpp"])
        saved.append(s)
        if nxt:
            wts_l.append(layer_weights(got))

    dh, loss_lanes, g_final = loss_bwd(h, final_norm[None, :], loss_target[0])
    loss = lax.psum(jnp.sum(loss_lanes), ("x", "y", "c"))

    def finish_reduce(cs, got):
        halves = [add_chips(c_, g_, 2, "grad_add_chips") for c_, g_ in zip(cs, got, strict=True)]
        return [f.reshape(-1, f.shape[2]) for f in join_packs(halves, "grad_join_halves")]

    pending, reduced = None, [None] * depth
    gs = {n: [None] * depth for n in SMALL_SHARDED + SMALL_REPL if n != "final_norm"}
    for i in reversed(range(depth)):
        s = saved[i]
        wt = wts_l[i]
        dh, gs["ple_norm"][i], n3, dgp, dpe = ple_bwd(dh, s["h3"], ple_norm[i:i + 1], p[i, 0], wt["w1024"], wt["wpp"])
        g_pg = mm(n3, dgp, ta=True, out_dtype=BF, name="g_ple_gate", tm_cap=512, tn_cap=512)
        g_pp = mm(p[i, 0], dpe, ta=True, out_dtype=BF, name="g_ple_proj", tm_cap=512, tn_cap=512)
        g_pp = jnp.transpose(g_pp.reshape(g_pp.shape[0], NCHIP, d4), (1, 0, 2))
        dn2, s2, dab2 = ffn_bwd(dh, s["ab2"], wt["f2"])
        g_ffn2 = ffn_wgrads(s["n2"], dh, s2, dab2)
        dh, gs["ffn2_norm"][i] = norm_bwd_add(dh, s["h2"], ffn2_norm[i:i + 1], dn2)
        dgates, dya, dyn, dy_a, dy_m = mix_out_bwd(dh, s["gates"], s["y_a"], s["y_m"], wt["w1024"])
        g_wo = mm(s["merged"], dh, ta=True, out_dtype=BF, name="g_w_o", tm_cap=512, tn_cap=512)
        g_sco = mm(s["ya_in"], dy_a, ta=True, out_dtype=BF, name="g_sc_out", tm_cap=512, tn_cap=512)
        g_mo = mm(s["yn"], dy_m, ta=True, out_dtype=BF, name="g_m_out", tm_cap=512, tn_cap=512)
        g_1024 = jnp.concatenate([g_mo.reshape(NCHIP, 2 * d4, d), g_sco.reshape(NCHIP, d4, d), g_wo.reshape(NCHIP, d4, d),
                                  g_pg.reshape(NCHIP, d4, d)], axis=1)
        (dz, dxbc, ddt, gs["m_norm"][i], gd, gal), got_a = ssd_bwd(dyn, s["y"], s["z"], s["xbc"], s["dt"], s["sprev"], a_log_p[i:i + 1], d_exp[i:i + 1],
                                                                   m_norm[i:i + 1], e_mat, et_mat, rider=scatter_rider(pending[:1]) if pending else None)
        gs["m_D"][i], gs["m_A_log"][i] = gd[:, :hh], gal[:, :hh]
        dpre, ddt_raw, gdb = conv_m_bwd1(dxbc, s["xbc_raw"], ddt, s["dt_raw"], m_conv_full[i], m_conv_b[i:i + 1], dt_bias_p[i:i + 1])
        gs["m_dt_bias"][i] = gdb[:, :hh]
        dxbc_raw, gs["m_conv_w"][i], gs["m_conv_b"][i] = conv_bwd2(dpre, s["xbc_raw"], m_conv_full[i], "conv_m_bwd2")
        dcv, dsc_b, v = conv_a_bwd1(dya, s["sc3"], sc_conv_full[i])
        dsc_c, dsc_x, gs["sc_conv_w"][i] = conv_a_bwd2(dcv, v, s["sc3"], sc_conv_full[i])
        dproj = jnp.concatenate([dsc_b, dsc_c, dsc_x, dz, dxbc_raw, dgates, ddt_raw], axis=1)
        if pending:
            du, got_b = mm(dproj, wt["in_p"], name="d_proj_in", tn_cap=512, rider=scatter_rider(pending[2:3]))
            gwp, got_c = mm(dproj, s["u"], ta=True, out_dtype=BF, name="g_w_in", tm_cap=1152, tn_cap=512, rider=scatter_rider(pending[1:2] + pending[3:]))
            reduced[i + 1] = finish_reduce(pending, [got_a[0], got_c[0], got_b[0], got_c[1]])
        else:
            du = mm(dproj, wt["in_p"], name="d_proj_in", tn_cap=512)
            gwp = mm(dproj, s["u"], ta=True, out_dtype=BF, name="g_w_in", tm_cap=1152, tn_cap=512)
        gw_rows = jnp.concatenate([gwp[:5 * d + cw], gwp[7 * d + cw:7 * d + cw + hh], gwp[5 * d + cw:7 * d + cw]], axis=0)
        g_in = gw_rows.reshape(NCHIP, -1, d)
        dh, gs["mix_norm"][i] = norm_bwd_add(dh, s["h1"], mix_norm[i:i + 1], du)
        dn1, s1, dab1 = ffn_bwd(dh, s["ab1"], wt["f1"])
        g_ffn1 = ffn_wgrads(s["n1"], dh, s1, dab1)
        dh, gs["ffn1_norm"][i] = norm_bwd_add(dh, s["h0"], ffn1_norm[i:i + 1], dn1)
        g_layer = [jnp.concatenate([g_ffn1, g_ffn2], axis=1), g_1024, g_in, g_pp]
        g_layer = [g.reshape(NCHIP, 2, g.shape[1] // 2, g.shape[2]) for g in g_layer]
        from_sibling = swap_packs(g_layer, "grad_swap_halves")
        pending = [add_sibling(g, r_, "grad_add_sibling") for g, r_ in zip(g_layer, from_sibling, strict=True)]
    reduced[0] = finish_reduce(pending, scatter_packs(pending, "grad_scatter"))
    grad_x = dh[None]

    rf, r1024, rin, rpp = (jnp.stack([reduced[l][j] for l in range(depth)]) for j in range(4))
    f4 = rf.shape[1] // 6
    ffn_rows = lambda j: rf[:, j * f4:(j + 1) * f4]
    grads = {
        "ffn1_wg": ffn_rows(0), "ffn1_wu": ffn_rows(1), "ffn1_wd": ffn_rows(2), "ffn2_wg": ffn_rows(3), "ffn2_wu": ffn_rows(4), "ffn2_wd": ffn_rows(5),
        "m_w_out": r1024[:, :2 * d4], "sc_w_out": r1024[:, 2 * d4:3 * d4], "w_o": r1024[:, 3 * d4:4 * d4], "ple_w_gate": r1024[:, 4 * d4:],
        "w_in": rin, "ple_w_proj": rpp,
    }

    small_names = list(SMALL_SHARDED + SMALL_REPL)
    small_full = [g_final[0] if n == "final_norm" else jnp.stack(gs[n]) for n in small_names]
    small_pack = _pack(small_full, LANE, HALO)
    all8 = all_gather_8(small_pack, "gather_small_grads")
    small_sum = add_parts([all8[k] for k in range(8)], F32, "add_small_grads", tm=256)
    for n, tot in zip(small_names, _unpack(small_sum, [a.shape for a in small_full]), strict=True):
        if n in SMALL_SHARDED:
            cl = wts[n].shape[2]
            grads[n] = lax.dynamic_slice_in_dim(tot, k_me * cl, cl, axis=2)
        else:
            grads[n] = tot.reshape(wts[n].shape)

    delta, new_m, new_v = {}, {}, {}
    for n in BIG:
        view = tr if n in TRANSPOSED else (lambda a: a)
        shp = grads[n].shape
        two = lambda a: a.reshape(-1, shp[-1])
        dl, nm, nv = adamw(two(view(wts[n])), two(grads[n]), two(view(mom[n])), two(view(vel[n])), "adamw_" + "x".join(map(str, shp[1:])))
        grads[n], delta[n], new_m[n], new_v[n] = view(grads[n]), view(dl.reshape(shp)), view(nm.reshape(shp)), view(nv.reshape(shp))
    sm_shapes = [wts[n].shape for n in small_names]
    pk = lambda dct: _pack([dct[n] for n in small_names], LANE, HALO)
    dl, nm, nv = adamw(pk(wts), pk(grads), pk(mom), pk(vel), "adamw_small")
    for n, a, b_, c_ in zip(small_names, _unpack(dl, sm_shapes), _unpack(nm, sm_shapes), _unpack(nv, sm_shapes), strict=True):
        delta[n], new_m[n], new_v[n] = a, b_, c_

    return (loss, grad_x, *[grads[n] for n in ORDER], *[delta[n] for n in ORDER], *[new_m[n] for n in ORDER], *[new_v[n] for n in ORDER])
```

```python
import jax
import jax.numpy as jnp
import numpy as np
from jax import lax
from jax.experimental import pallas as pl
from jax.experimental.pallas import tpu as pltpu

BF = jnp.bfloat16
F32 = jnp.float32
EPS = 1e-6
LANE = 128
HALO = 8
SSM_P = 64
SSM_N = 128
SSM_G = 4
SSM_L = 128
ADAM_LR, ADAM_B1, ADAM_B2, ADAM_EPS, ADAM_WD, ADAM_STEP = 0.001, 0.9, 0.999, 1e-08, 0.01, 10
VMEM_LIMIT = 56 * 1024 * 1024
TILE_ELEMS = 400_000
NCHIP = 4
FFN_SUB = 256
MESH = pl.DeviceIdType.MESH
HI = lax.Precision.HIGHEST


def _tile(n, cap, mult=LANE):
    best = None
    t = mult
    while t <= min(n, cap):
        if n % t == 0:
            best = t
        t += mult
    return best if best is not None else n


def _row_tile(r, c, mult=16):
    return _tile(r, max(mult, TILE_ELEMS // c // mult * mult), mult)


def _tile2(r, c, mult=16):
    tm = _row_tile(r, c, mult)
    tc = c if tm * c <= TILE_ELEMS else _tile(c, max(LANE, TILE_ELEMS // tm // LANE * LANE))
    return tm, tc


def _params(sem):
    return pltpu.CompilerParams(dimension_semantics=sem, vmem_limit_bytes=VMEM_LIMIT)


def _sigmoid(x):
    return 1.0 / (1.0 + jnp.exp(-x))


def _dot(a, b, ca=1, cb=0, precision=None):
    return lax.dot_general(a, b, (((ca,), (cb,)), ((), ())), precision=precision, preferred_element_type=F32)


def _rms(x, g):
    r = lax.rsqrt(jnp.mean(x * x, axis=-1, keepdims=True) + EPS)
    return x * r * g


def _rms_bwd(x, g, dy):
    r = lax.rsqrt(jnp.mean(x * x, axis=-1, keepdims=True) + EPS)
    xh = x * r
    dxh = dy * g
    dx = r * (dxh - xh * jnp.mean(dxh * xh, axis=-1, keepdims=True))
    return dx, jnp.sum(dy * xh, axis=0, keepdims=True)


def _accumulate(ref, val, first):
    @pl.when(first)
    def _():
        ref[...] = val

    @pl.when(jnp.logical_not(first))
    def _():
        ref[...] += val


RIDER_MID = 1.0


class Rider:
    def __init__(self, ins, out_shapes, n_sems, start, finish, mid=None):
        self.ins, self.out_shapes, self.n_sems, self.start, self.mid, self.finish = list(ins), list(out_shapes), n_sems, start, mid, finish

    def standalone(self, name):
        ni, no = len(self.ins), len(self.out_shapes)

        def body(*refs):
            parts = (refs[:ni], refs[ni:ni + no], *refs[ni + no:])
            self.start(*parts)
            if self.mid is not None:
                self.mid(*parts)
            self.finish(*parts)

        return _comm_call(body, name, self.ins, self.out_shapes, self.n_sems)


def host_call(body, *, name, grid, in_specs, out_specs, out_shape, scratch_shapes, operands, rider=None):
    n_in, n_out = len(in_specs), len(out_specs)
    if rider is None:
        outs = pl.pallas_call(body, name=name, grid=grid, in_specs=in_specs, out_specs=out_specs, out_shape=out_shape,
                              scratch_shapes=scratch_shapes, compiler_params=_params(("arbitrary",) * len(grid)))(*operands)
        return list(outs), []
    ri, ro = len(rider.ins), len(rider.out_shapes)

    def hosted(*refs):
        ins, r_ins = refs[:n_in], refs[n_in:n_in + ri]
        outs, r_outs = refs[n_in + ri:n_in + ri + n_out], refs[n_in + ri + n_out:n_in + ri + n_out + ro]
        scratch, (send_sems, recv_sems) = refs[n_in + ri + n_out + ro:-2], refs[-2:]
        step, total = 0, 1
        for ax, n in enumerate(grid):
            step = step * n + pl.program_id(ax)
            total *= n
        parts = (r_ins, r_outs, send_sems, recv_sems)

        @pl.when(step == 0)
        def _():
            rider.start(*parts)

        if rider.mid is not None:
            @pl.when(step == min(total - 1, int(total * RIDER_MID)))
            def _():
                rider.mid(*parts)

        body(*ins, *outs, *scratch)

        @pl.when(step == total - 1)
        def _():
            rider.finish(*parts)

    outs = pl.pallas_call(
        hosted,
        name=name,
        grid=grid,
        in_specs=list(in_specs) + [ANY] * ri,
        out_specs=list(out_specs) + [ANY] * ro,
        out_shape=list(out_shape) + rider.out_shapes,
        scratch_shapes=list(scratch_shapes) + [pltpu.SemaphoreType.DMA((rider.n_sems,)), pltpu.SemaphoreType.DMA((rider.n_sems,))],
        compiler_params=_params(("arbitrary",) * len(grid)),
    )(*operands, *rider.ins)
    return list(outs[:n_out]), list(outs[n_out:])


def mmx(name, a, b, *, grid, a_spec, b_spec, o_spec, o_shape, o_dtype, ca, cb, acc_shape=None, scale=None, rider=None):
    nk = grid[-1] if acc_shape is not None else 1
    assert scale is None or nk == 1

    def body(a_ref, b_ref, o_ref, *acc):
        p = _dot(a_ref[...].astype(BF), b_ref[...].astype(BF), ca, cb)
        if scale is not None:
            p = p * scale
        if nk == 1:
            o_ref[...] = p.astype(o_ref.dtype)
        else:
            kk = pl.program_id(len(grid) - 1)
            _accumulate(acc[0], p, kk == 0)

            @pl.when(kk == nk - 1)
            def _():
                o_ref[...] = acc[0][...].astype(o_ref.dtype)

    if rider is not None:
        (out,), r_outs = host_call(body, name=name, grid=grid, in_specs=[a_spec, b_spec], out_specs=[o_spec], out_shape=[jax.ShapeDtypeStruct(o_shape, o_dtype)],
                                   scratch_shapes=[pltpu.VMEM(acc_shape, F32)] if nk > 1 else [], operands=(a, b), rider=rider)
        return out, r_outs
    sem = ("parallel",) * (len(grid) - 1) + ("arbitrary" if nk > 1 else "parallel",)
    return pl.pallas_call(
        body,
        name=name,
        grid=grid,
        in_specs=[a_spec, b_spec],
        out_specs=o_spec,
        out_shape=jax.ShapeDtypeStruct(o_shape, o_dtype),
        scratch_shapes=[pltpu.VMEM(acc_shape, F32)] if nk > 1 else [],
        compiler_params=_params(sem),
    )(a, b)


def mm(a, b, *, ta=False, tb=False, out_dtype=F32, name, tm_cap=1024, tn_cap=1024, tk_cap=4096, rider=None):
    m, k = (a.shape[1], a.shape[0]) if ta else a.shape
    n = b.shape[0] if tb else b.shape[1]
    assert (b.shape[1] if tb else b.shape[0]) == k
    tm, tn, tk = _tile(m, tm_cap), _tile(n, tn_cap), _tile(k, tk_cap)
    nk = k // tk
    a_spec = pl.BlockSpec((tk, tm), lambda i, j, kk: (kk, i)) if ta else pl.BlockSpec((tm, tk), lambda i, j, kk: (i, kk))
    b_spec = pl.BlockSpec((tn, tk), lambda i, j, kk: (j, kk)) if tb else pl.BlockSpec((tk, tn), lambda i, j, kk: (kk, j))
    return mmx(name, a, b, grid=(m // tm, n // tn, nk), a_spec=a_spec, b_spec=b_spec, o_spec=pl.BlockSpec((tm, tn), lambda i, j, kk: (i, j)),
               o_shape=(m, n), o_dtype=out_dtype, ca=0 if ta else 1, cb=1 if tb else 0, acc_shape=(tm, tn) if nk > 1 else None, rider=rider)


def ew(fn, rows, vecs, out_rows, out_red=(), *, tm, name, prev_halo=(), next_halo=()):
    t = rows[0].shape[0]
    tm = min(tm, t)
    nt = t // tm
    assert t % tm == 0 and (tm % HALO == 0 or (tm == t and not prev_halo and not next_halo))
    nr, nv, npv, nnx, nor = len(rows), len(vecs), len(prev_halo), len(next_halo), len(out_rows)
    hb = tm // HALO

    def body(*refs):
        i = pl.program_id(0)
        ins = [r[...].astype(F32) for r in refs[: nr + nv + npv + nnx]]
        outs = refs[nr + nv + npv + nnx:]
        o_rows, o_red = fn(i, nt, ins[:nr], ins[nr:nr + nv], ins[nr + nv:nr + nv + npv], ins[nr + nv + npv:])
        for ref, val in zip(outs[:nor], o_rows, strict=True):
            ref[...] = val.astype(ref.dtype)
        for ref, val in zip(outs[nor:], o_red, strict=True):
            _accumulate(ref, val, i == 0)

    in_specs = [pl.BlockSpec((tm, r.shape[1]), lambda i: (i, 0)) for r in rows]
    in_specs += [pl.BlockSpec(v.shape, lambda i: (0, 0)) for v in vecs]
    in_specs += [pl.BlockSpec((HALO, rows[k].shape[1]), lambda i: (jnp.maximum(i * hb - 1, 0), 0)) for k in prev_halo]
    in_specs += [pl.BlockSpec((HALO, rows[k].shape[1]), lambda i: (jnp.minimum((i + 1) * hb, t // HALO - 1), 0)) for k in next_halo]
    out_specs = [pl.BlockSpec((tm, c), lambda i: (i, 0)) for c, _ in out_rows]
    out_specs += [pl.BlockSpec(s, lambda i: (0, 0)) for s in out_red]
    out_shape = [jax.ShapeDtypeStruct((t, c), d) for c, d in out_rows] + [jax.ShapeDtypeStruct(s, F32) for s in out_red]
    return pl.pallas_call(
        body,
        name=name,
        grid=(nt,),
        in_specs=in_specs,
        out_specs=out_specs,
        out_shape=out_shape,
        compiler_params=_params(("arbitrary",) if out_red else ("parallel",)),
    )(*rows, *vecs, *[rows[k] for k in prev_halo], *[rows[k] for k in next_halo])


def _shift_down(x, prev, j):
    if j == 0:
        return x
    r = pltpu.roll(x, j, 0)
    rh = pltpu.roll(prev, j, 0)
    row = lax.broadcasted_iota(jnp.int32, (HALO, x.shape[1]), 0)
    head = jnp.where(row < j, rh, r[:HALO])
    return jnp.concatenate([head, r[HALO:]], axis=0)


def _shift_up(x, nxt, j):
    if j == 0:
        return x
    n = x.shape[0]
    r = pltpu.roll(x, n - j, 0)
    rh = pltpu.roll(nxt, HALO - j, 0)
    row = lax.broadcasted_iota(jnp.int32, (HALO, x.shape[1]), 0)
    tail = jnp.where(row >= HALO - j, rh, r[n - HALO:])
    return jnp.concatenate([r[: n - HALO], tail], axis=0)


def _conv_fwd(x, prev, w):
    kk = w.shape[0]
    acc = None
    for k in range(kk):
        term = w[k:k + 1, :] * _shift_down(x, prev, kk - 1 - k)
        acc = term if acc is None else acc + term
    return acc


def ffn_up(h, g, wf, rider=None):
    t, d = h.shape
    f4 = wf.shape[1] // 2
    tm = _tile(t, 1024)
    sub = _tile(tm, FFN_SUB, 16)

    def body(h_ref, g_ref, wg_ref, wu_ref, ab_ref, s_ref, n_ref):
        @pl.when(pl.program_id(1) == 0)
        def _():
            n_ref[...] = _rms(h_ref[...], g_ref[...]).astype(BF)

        for r in range(tm // sub):
            rows = slice(r * sub, (r + 1) * sub)
            n = n_ref[rows, :]
            a = _dot(n, wg_ref[...], 1, 1)
            b = _dot(n, wu_ref[...], 1, 1)
            ab_ref[0, rows, :] = a.astype(BF)
            ab_ref[1, rows, :] = b.astype(BF)
            s_ref[rows, :] = (a * _sigmoid(a) * b).astype(BF)

    wspec = lambda ib: pl.BlockSpec((None, f4, d), lambda i, j: (j, ib, 0))
    return host_call(
        body,
        name="ffn_up",
        grid=(t // tm, NCHIP),
        in_specs=[pl.BlockSpec((tm, d), lambda i, j: (i, 0)), pl.BlockSpec((1, d), lambda i, j: (0, 0)), wspec(0), wspec(1)],
        out_specs=[pl.BlockSpec((2, None, tm, f4), lambda i, j: (0, j, i, 0)), pl.BlockSpec((None, tm, f4), lambda i, j: (j, i, 0)),
                   pl.BlockSpec((tm, d), lambda i, j: (i, 0))],
        out_shape=[jax.ShapeDtypeStruct((2, NCHIP, t, f4), BF), jax.ShapeDtypeStruct((NCHIP, t, f4), BF), jax.ShapeDtypeStruct((t, d), BF)],
        scratch_shapes=[],
        operands=(h, g, wf, wf),
        rider=rider,
    )


def ffn_down(s4, wf, h, rider=None):
    t, d = h.shape
    f4 = s4.shape[2]
    tm = _tile(t, 512)

    def body(s_ref, w_ref, h_ref, o_ref):
        acc = _dot(s_ref[0], w_ref[0])
        for k in range(1, NCHIP):
            acc = acc + _dot(s_ref[k], w_ref[k])
        o_ref[...] = h_ref[...] + 0.5 * acc

    (out,), r_outs = host_call(
        body,
        name="ffn_down",
        grid=(t // tm,),
        in_specs=[pl.BlockSpec((NCHIP, tm, f4), lambda i: (0, i, 0)), pl.BlockSpec((NCHIP, f4, d), lambda i: (0, 0, 0)), pl.BlockSpec((tm, d), lambda i: (i, 0))],
        out_specs=[pl.BlockSpec((tm, d), lambda i: (i, 0))],
        out_shape=[jax.ShapeDtypeStruct((t, d), F32)],
        scratch_shapes=[],
        operands=(s4, wf, h),
        rider=rider,
    )
    return out, r_outs


def ffn_bwd(dho, ab, wf, wd):
    t, d = dho.shape
    f4 = wd.shape[1]
    tm = _tile(t, 1024)
    sub = _tile(tm, FFN_SUB, 16)

    def body(dho_ref, ab_ref, wg_ref, wu_ref, wd_ref, dn_ref, s_ref, dab_ref, do_sc):
        j = pl.program_id(1)

        @pl.when(j == 0)
        def _():
            do_sc[...] = (0.5 * dho_ref[...]).astype(BF)
            dn_ref[...] = jnp.zeros_like(dn_ref)

        for r in range(tm // sub):
            rows = slice(r * sub, (r + 1) * sub)
            ds = _dot(do_sc[rows, :], wd_ref[...], 1, 1)
            av, bv = ab_ref[0, rows, :].astype(F32), ab_ref[1, rows, :].astype(F32)
            sig = _sigmoid(av)
            sl = av * sig
            s_ref[rows, :] = (sl * bv).astype(BF)
            da = (ds * bv * (sig * (1.0 + av * (1.0 - sig)))).astype(BF)
            db = (ds * sl).astype(BF)
            dab_ref[0, rows, :] = da
            dab_ref[1, rows, :] = db
            dn_ref[rows, :] += _dot(da, wg_ref[...]) + _dot(db, wu_ref[...])

    row = lambda c: pl.BlockSpec((tm, c), lambda i, j: (i, 0))
    wspec = lambda ib: pl.BlockSpec((None, f4, d), lambda i, j: (j, ib, 0))
    ab_spec = pl.BlockSpec((2, None, tm, f4), lambda i, j: (0, j, i, 0))
    return pl.pallas_call(
        body,
        name="ffn_bwd",
        grid=(t // tm, NCHIP),
        in_specs=[row(d), ab_spec, wspec(0), wspec(1), wspec(0)],
        out_specs=[row(d), pl.BlockSpec((None, tm, f4), lambda i, j: (j, i, 0)), ab_spec],
        out_shape=[jax.ShapeDtypeStruct((t, d), F32), jax.ShapeDtypeStruct((NCHIP, t, f4), BF), jax.ShapeDtypeStruct((2, NCHIP, t, f4), BF)],
        scratch_shapes=[pltpu.VMEM((tm, d), BF)],
        compiler_params=_params(("parallel", "arbitrary")),
    )(dho, ab, wf, wf, wd)


def ffn_wgrads(n, dho, s4, dab):
    t, d = n.shape
    f4 = s4.shape[2]
    tn = _tile(d, 512)
    g_in = mmx("g_ffn_in", dab, n, grid=(2, NCHIP, d // tn), a_spec=pl.BlockSpec((None, None, t, f4), lambda wh, k, j: (wh, k, 0, 0)),
               b_spec=pl.BlockSpec((t, tn), lambda wh, k, j: (0, j)), o_spec=pl.BlockSpec((None, None, f4, tn), lambda wh, k, j: (k, wh, 0, j)),
               o_shape=(NCHIP, 2, f4, d), o_dtype=BF, ca=0, cb=0)
    g_out = mmx("g_ffn_out", s4, dho, grid=(NCHIP, d // tn), a_spec=pl.BlockSpec((None, t, f4), lambda k, j: (k, 0, 0)),
                b_spec=pl.BlockSpec((t, tn), lambda k, j: (0, j)), o_spec=pl.BlockSpec((None, f4, tn), lambda k, j: (k, 0, j)),
                o_shape=(NCHIP, f4, d), o_dtype=BF, ca=0, cb=0, scale=0.5)
    return jnp.concatenate([g_in.reshape(NCHIP, 2 * f4, d), g_out], axis=1)


def norm_cast(h, g):
    def fn(i, nt, rows, vecs, prevs, nexts):
        return [_rms(rows[0], vecs[0])], []
    return ew(fn, [h], [g], [(h.shape[1], BF)], tm=512, name="norm_cast")[0]


def _zero_if(cond, x):
    return jnp.where(cond, jnp.zeros_like(x), x)


def conv_a_fwd(sc3, w_sc):
    d = sc3.shape[1] // 3

    def fn(i, nt, rows, vecs, prevs, nexts):
        x, pv = rows[0], _zero_if(i == 0, prevs[0])
        v = x[:, d:2 * d] * x[:, 2 * d:]
        vp = pv[:, d:2 * d] * pv[:, 2 * d:]
        return [x[:, :d] * _conv_fwd(v, vp, vecs[0])], []

    return ew(fn, [sc3], [w_sc], [(d, BF)], tm=256, name="conv_a_fwd", prev_halo=(0,))[0]


def _softplus(x):
    e = jnp.exp(-jnp.abs(x))
    return jnp.maximum(x, 0.0) + jnp.where(e < 1e-4, e - 0.5 * e * e, jnp.log(1.0 + e))


def conv_m_fwd(xbc_raw, dt_raw, w_mc, b_mc, dt_bias):
    def fn(i, nt, rows, vecs, prevs, nexts):
        pre = _conv_fwd(rows[0], _zero_if(i == 0, prevs[0]), vecs[0]) + vecs[1]
        return [pre * _sigmoid(pre), _softplus(rows[1] + vecs[2])], []

    return ew(fn, [xbc_raw, dt_raw], [w_mc, b_mc, dt_bias], [(xbc_raw.shape[1], F32), (LANE, F32)], tm=256, name="conv_m_fwd",
              prev_halo=(0,))


def conv_m_bwd1(dxbc, xbc_raw, ddt, dt_raw, w_mc, b_mc, dt_bias):
    def fn(i, nt, rows, vecs, prevs, nexts):
        pre = _conv_fwd(rows[1], _zero_if(i == 0, prevs[0]), vecs[0]) + vecs[1]
        sig = _sigmoid(pre)
        dpre = rows[0] * (sig * (1.0 + pre * (1.0 - sig)))
        ddr = rows[2] * _sigmoid(rows[3] + vecs[2])
        return [dpre, ddr], [jnp.sum(ddr, axis=0, keepdims=True)]

    return ew(fn, [dxbc, xbc_raw, ddt, dt_raw], [w_mc, b_mc, dt_bias], [(dxbc.shape[1], F32), (LANE, BF)], [(1, LANE)], tm=256,
              name="conv_m_bwd1", prev_halo=(1,))


def conv_bwd2(dpre, x, w, name):
    kk = w.shape[0]

    def fn(i, nt, rows, vecs, prevs, nexts):
        dp, xv = rows[0], rows[1]
        nx = _zero_if(i == nt - 1, nexts[0])
        pv = _zero_if(i == 0, prevs[0])
        dx = None
        dws = []
        for k in range(kk):
            term = vecs[0][k:k + 1, :] * _shift_up(dp, nx, kk - 1 - k)
            dx = term if dx is None else dx + term
            dws.append(jnp.sum(dp * _shift_down(xv, pv, kk - 1 - k), axis=0, keepdims=True))
        return [dx], [jnp.concatenate(dws, axis=0), jnp.sum(dp, axis=0, keepdims=True)]

    c = x.shape[1]
    return ew(fn, [dpre, x], [w], [(c, BF)], [(kk, c), (1, c)], tm=256, name=name, prev_halo=(1,), next_halo=(0,))


def conv_a_bwd1(dya, sc3, w_sc):
    d = sc3.shape[1] // 3

    def fn(i, nt, rows, vecs, prevs, nexts):
        x, pv = rows[1], _zero_if(i == 0, prevs[0])
        v = x[:, d:2 * d] * x[:, 2 * d:]
        vp = pv[:, d:2 * d] * pv[:, 2 * d:]
        return [rows[0] * x[:, :d], rows[0] * _conv_fwd(v, vp, vecs[0]), v], []

    return ew(fn, [dya, sc3], [w_sc], [(d, F32), (d, BF), (d, F32)], tm=256, name="conv_a_bwd1", prev_halo=(1,))


def conv_a_bwd2(dcv, v, sc3, w_sc):
    d = v.shape[1]
    kk = w_sc.shape[0]

    def fn(i, nt, rows, vecs, prevs, nexts):
        dp, vv, x = rows
        nx = _zero_if(i == nt - 1, nexts[0])
        pv = _zero_if(i == 0, prevs[0])
        dv = None
        dws = []
        for k in range(kk):
            term = vecs[0][k:k + 1, :] * _shift_up(dp, nx, kk - 1 - k)
            dv = term if dv is None else dv + term
            dws.append(jnp.sum(dp * _shift_down(vv, pv, kk - 1 - k), axis=0, keepdims=True))
        return [dv * x[:, 2 * d:], dv * x[:, d:2 * d]], [jnp.concatenate(dws, axis=0)]

    return ew(fn, [dcv, v, sc3], [w_sc], [(d, BF), (d, BF)], [(kk, d)], tm=256, name="conv_a_bwd2", prev_halo=(1,), next_halo=(0,))


def _xdot(a, b, passes, split_lhs, ca=1, cb=0):
    parts, r = [], (a if split_lhs else b)
    for _ in range(passes):
        piece = r.astype(BF)
        parts.append(piece)
        r = r - piece.astype(F32)
    other = (b if split_lhs else a).astype(BF)
    acc = None
    for piece in parts:
        term = _dot(piece, other, ca, cb) if split_lhs else _dot(other, piece, ca, cb)
        acc = term if acc is None else acc + term
    return acc


def _ssd_common(xbc_ref, dt_ref, alog_ref, e_ref, w):
    ll = SSM_L
    xs = xbc_ref[:, 0:w]
    dtv = dt_ref[...]
    a_row = -jnp.exp(alog_ref[...])
    a = dtv * a_row
    row = lax.broadcasted_iota(jnp.int32, (ll, ll), 0)
    col = lax.broadcasted_iota(jnp.int32, (ll, ll), 1)
    tril = (row >= col).astype(F32)
    triu = (row <= col).astype(F32)
    acl = _xdot(tril, a, 3, False)
    acl_t = _xdot(a, triu, 3, True, 0, 0)
    e = e_ref[...]
    aclx = _xdot(acl, e, 3, True)
    dtx = _xdot(dtv, e, 2, True)
    last = aclx[ll - 1:ll, :]
    e_in = jnp.exp(aclx)
    e_end = jnp.exp(last - aclx)
    e_tot = jnp.exp(last)
    x = xs * dtx
    return dict(xs=xs, dtv=dtv, a_row=a_row, a=a, row=row, col=col, triu=triu, acl=acl, acl_t=acl_t, dtx=dtx, e_in=e_in, e_end=e_end,
                e_tot=e_tot, x=x)


def _decay(q, hh):
    diff = q["acl"][:, hh:hh + 1] - q["acl_t"][hh:hh + 1, :]
    return jnp.exp(jnp.where(q["row"] >= q["col"], diff, -jnp.inf))


def ssd_fwd(xbc, dt, z, a_log, d_exp, m_norm, e_mat, rider=None):
    t = xbc.shape[0]
    w = z.shape[1]
    gn = SSM_G * SSM_N
    gw = w // SSM_G
    ll, nn = SSM_L, SSM_N
    nc = t // ll
    cw = xbc.shape[1]

    def body(xbc_ref, dt_ref, z_ref, alog_ref, dexp_ref, mn_ref, e_ref, yn_ref, y_ref, sp_ref, s_sc):
        c = pl.program_id(0)

        @pl.when(c == 0)
        def _():
            s_sc[...] = jnp.zeros_like(s_sc)

        q = _ssd_common(xbc_ref, dt_ref, alog_ref, e_ref, w)
        xb = q["x"].astype(BF)
        xsb = (q["x"] * q["e_end"]).astype(BF)
        sp = s_sc[...]
        sp_ref[0] = sp
        spb = sp.astype(BF)
        lane = lax.broadcasted_iota(jnp.int32, (ll, LANE), 1)
        for g in range(SSM_G):
            lo = g * gw
            bg = xbc_ref[:, w + g * nn:w + (g + 1) * nn].astype(BF)
            cg = xbc_ref[:, w + gn + g * nn:w + gn + (g + 1) * nn].astype(BF)
            yoff = _dot(cg, spb[:, lo:lo + gw]) * q["e_in"][:, lo:lo + gw]
            s_sc[:, lo:lo + gw] = sp[:, lo:lo + gw] * q["e_tot"][:, lo:lo + gw] + _dot(bg, xsb[:, lo:lo + gw], 0, 0)
            cb = _dot(cg, bg, 1, 1)
            for pr in range(gw // LANE):
                l0 = lo + pr * LANE
                xp = xb[:, l0:l0 + LANE]
                ys = []
                for hh in (l0 // SSM_P, l0 // SSM_P + 1):
                    wm = (cb * _decay(q, hh)).astype(BF)
                    ys.append(_dot(wm, xp))
                ydiag = jnp.where(lane < SSM_P, ys[0], ys[1])
                y_ref[:, l0:l0 + LANE] = ydiag + yoff[:, pr * LANE:(pr + 1) * LANE] + dexp_ref[:, l0:l0 + LANE] * q["xs"][:, l0:l0 + LANE]
        zv = z_ref[...].astype(F32)
        yz = y_ref[...] * (zv * _sigmoid(zv))
        for g in range(SSM_G):
            lo = g * gw
            yn_ref[:, lo:lo + gw] = _rms(yz[:, lo:lo + gw], mn_ref[:, lo:lo + gw]).astype(BF)

    vec = lambda s: pl.BlockSpec(s, lambda c: (0, 0))
    return host_call(
        body,
        name="ssd_fwd",
        grid=(nc,),
        in_specs=[
            pl.BlockSpec((ll, cw), lambda c: (c, 0)), pl.BlockSpec((ll, LANE), lambda c: (c, 0)), pl.BlockSpec((ll, w), lambda c: (c, 0)),
            vec((1, LANE)), vec((1, w)), vec((1, w)), vec((LANE, w)),
        ],
        out_specs=[pl.BlockSpec((ll, w), lambda c: (c, 0)), pl.BlockSpec((ll, w), lambda c: (c, 0)), pl.BlockSpec((1, nn, w), lambda c: (c, 0, 0))],
        out_shape=[jax.ShapeDtypeStruct((t, w), BF), jax.ShapeDtypeStruct((t, w), F32), jax.ShapeDtypeStruct((nc, nn, w), F32)],
        scratch_shapes=[pltpu.VMEM((nn, w), F32)],
        operands=(xbc, dt, z, a_log, d_exp, m_norm, e_mat),
        rider=rider,
    )


def ssd_bwd(dyn, y, z, xbc, dt, sprev, a_log, d_exp, m_norm, e_mat, et_mat, rider=None):
    t = xbc.shape[0]
    w = z.shape[1]
    gn = SSM_G * SSM_N
    gw = w // SSM_G
    ll, nn = SSM_L, SSM_N
    nc = t // ll
    cw = xbc.shape[1]

    def body(dyn_ref, y_ref, z_ref, xbc_ref, dt_ref, sp_ref, alog_ref, dexp_ref, mn_ref, e_ref, et_ref,
             dz_ref, dxbc_ref, ddt_ref, dmn_ref, dd_ref, dal_ref, ds_sc, dy_sc, dx_sc):
        step = pl.program_id(0)

        @pl.when(step == 0)
        def _():
            ds_sc[...] = jnp.zeros_like(ds_sc)

        zv, yv = z_ref[...].astype(F32), y_ref[...]
        sg = _sigmoid(zv)
        sz = zv * sg
        yz = yv * sz
        dmn = []
        for g in range(SSM_G):
            lo = g * gw
            dseg, dmn_g = _rms_bwd(yz[:, lo:lo + gw], mn_ref[:, lo:lo + gw], dyn_ref[:, lo:lo + gw])
            dy_sc[:, lo:lo + gw] = dseg
            dmn.append(dmn_g)
        dmn = jnp.concatenate(dmn, axis=1)
        dyz = dy_sc[...]
        dz_ref[...] = (dyz * yv * (sg * (1.0 + zv * (1.0 - sg)))).astype(BF)
        dy = dyz * sz

        q = _ssd_common(xbc_ref, dt_ref, alog_ref, e_ref, w)
        x = q["x"]
        xb = x.astype(BF)
        xsb = (x * q["e_end"]).astype(BF)
        sp = sp_ref[0]
        spb = sp.astype(BF)
        dsn = ds_sc[...]
        dsnb = dsn.astype(BF)
        dyb = dy.astype(BF)
        lane = lax.broadcasted_iota(jnp.int32, (ll, LANE), 1)
        lane1 = lax.broadcasted_iota(jnp.int32, (1, LANE), 1)
        sub1 = lax.broadcasted_iota(jnp.int32, (LANE, 1), 0)
        dacl = jnp.zeros((ll, LANE), F32)
        dacl_t = jnp.zeros((LANE, ll), F32)
        d_ein, d_eend, d_etot = [], [], []
        for g in range(SSM_G):
            lo = g * gw
            sl = slice(lo, lo + gw)
            bg = xbc_ref[:, w + g * nn:w + (g + 1) * nn].astype(BF)
            cg = xbc_ref[:, w + gn + g * nn:w + gn + (g + 1) * nn].astype(BF)
            zg = _dot(cg, spb[:, sl])
            dzz = (dy[:, sl] * q["e_in"][:, sl]).astype(BF)
            d_ein.append(dy[:, sl] * zg)
            dcg = _dot(dzz, spb[:, sl], 1, 1)
            ds_sc[:, sl] = _dot(cg, dzz, 0, 0) + dsn[:, sl] * q["e_tot"][:, sl]
            d_etot.append(jnp.sum(dsn[:, sl] * sp[:, sl], axis=0, keepdims=True))
            dbg = _dot(xsb[:, sl], dsnb[:, sl], 1, 1)
            dxs_g = _dot(bg, dsnb[:, sl])
            d_eend.append(dxs_g * x[:, sl])
            cb = _dot(cg, bg, 1, 1)
            dcb = jnp.zeros((ll, ll), F32)
            for pr in range(gw // LANE):
                l0 = lo + pr * LANE
                xp = xb[:, l0:l0 + LANE]
                dyp = dyb[:, l0:l0 + LANE]
                dxp = []
                for hi, hh in enumerate((l0 // SSM_P, l0 // SSM_P + 1)):
                    lm = _decay(q, hh)
                    wm = (cb * lm).astype(BF)
                    in_head = (lane < SSM_P) if hi == 0 else (lane >= SSM_P)
                    dwm = _dot(jnp.where(in_head, dyp, jnp.zeros_like(dyp)), xp, 1, 1)
                    dxp.append(_dot(wm, dyp, 0, 0))
                    dlm = dwm * lm
                    dcb = dcb + dlm
                    dd = dlm * cb
                    dacl = dacl + jnp.sum(dd, axis=1, keepdims=True) * (lane1 == hh).astype(F32)
                    dacl_t = dacl_t + (sub1 == hh).astype(F32) * jnp.sum(dd, axis=0, keepdims=True)
                dx_sc[:, l0:l0 + LANE] = jnp.where(lane < SSM_P, dxp[0], dxp[1]) + dxs_g[:, pr * LANE:(pr + 1) * LANE] * q["e_end"][:, l0:l0 + LANE]
            dcbb = dcb.astype(BF)
            dxbc_ref[:, w + g * nn:w + (g + 1) * nn] = dbg + _dot(dcbb, cg, 0, 0)
            dxbc_ref[:, w + gn + g * nn:w + gn + (g + 1) * nn] = dcg + _dot(dcbb, bg)
        d_ein = jnp.concatenate(d_ein, axis=1) * q["e_in"]
        d_eend = jnp.concatenate(d_eend, axis=1) * q["e_end"]
        d_etot = jnp.concatenate(d_etot, axis=1) * q["e_tot"]
        et = et_ref[...]
        last_add = jnp.sum(d_eend, axis=0, keepdims=True) + d_etot
        last_add = _xdot(jnp.broadcast_to(last_add, (HALO, w)), et, 2, True)[0:1]
        row1 = lax.broadcasted_iota(jnp.int32, (ll, LANE), 0)
        dacl = dacl + _xdot(d_ein - d_eend, et, 2, True) + jnp.where(row1 == ll - 1, last_add, 0.0)
        da = _xdot(q["triu"], dacl, 2, False) - _xdot(q["triu"], dacl_t, 2, False, 1, 1)
        dxv = dx_sc[...]
        dxbc_ref[:, 0:w] = dexp_ref[...] * dy + dxv * q["dtx"]
        ddt_ref[...] = _xdot(dxv * q["xs"], et, 2, True) + da * q["a_row"]
        dal = jnp.sum(da * q["dtv"], axis=0, keepdims=True) * q["a_row"]
        ddv = jnp.sum(dy * q["xs"], axis=0, keepdims=True)
        ddv = _xdot(jnp.broadcast_to(ddv, (HALO, w)), et, 2, True)[0:1]
        _accumulate(dmn_ref, dmn, step == 0)
        _accumulate(dd_ref, ddv, step == 0)
        _accumulate(dal_ref, dal, step == 0)

    rev = lambda c_: pl.BlockSpec((ll, c_), lambda s: (nc - 1 - s, 0))
    vec = lambda s_: pl.BlockSpec(s_, lambda s: (0, 0))
    return host_call(
        body,
        name="ssd_bwd",
        grid=(nc,),
        in_specs=[
            rev(w), rev(w), rev(w), rev(cw), rev(LANE), pl.BlockSpec((1, nn, w), lambda s: (nc - 1 - s, 0, 0)),
            vec((1, LANE)), vec((1, w)), vec((1, w)), vec((LANE, w)), vec((w, LANE)),
        ],
        out_specs=[rev(w), rev(cw), rev(LANE), vec((1, w)), vec((1, LANE)), vec((1, LANE))],
        out_shape=[
            jax.ShapeDtypeStruct((t, w), BF), jax.ShapeDtypeStruct((t, cw), F32), jax.ShapeDtypeStruct((t, LANE), F32),
            jax.ShapeDtypeStruct((1, w), F32), jax.ShapeDtypeStruct((1, LANE), F32), jax.ShapeDtypeStruct((1, LANE), F32),
        ],
        scratch_shapes=[pltpu.VMEM((nn, w), F32), pltpu.VMEM((ll, w), F32), pltpu.VMEM((ll, w), F32)],
        operands=(dyn, y, z, xbc, dt, sprev, a_log, d_exp, m_norm, e_mat, et_mat),
        rider=rider,
    )


def _w1024_spec(d, nblk, iblk):
    r = nblk * (d // NCHIP)
    return pl.BlockSpec((NCHIP, r, d), lambda i: (0, iblk // nblk, 0))


def _whole(ref):
    v = ref[...]
    return v.reshape(v.shape[0] * v.shape[1], v.shape[2])


def mix_out_fwd(ya_in, yn, gates, h, w1024, rider=None):
    t, d = h.shape
    tm = _tile(t, 256)

    def body(ya_ref, yn_ref, g_ref, h_ref, wm_ref, wa_ref, wo_ref, ho_ref, oa_ref, om_ref, mg_ref):
        y_a = _dot(ya_ref[...], _whole(wa_ref))
        y_m = _dot(yn_ref[...], _whole(wm_ref))
        oa_ref[...] = y_a
        om_ref[...] = y_m
        gv = g_ref[...].astype(F32)
        mg = (_sigmoid(gv[:, :d]) * y_a + _sigmoid(gv[:, d:]) * y_m).astype(BF)
        mg_ref[...] = mg
        ho_ref[...] = h_ref[...] + _dot(mg, _whole(wo_ref))

    row = lambda c: pl.BlockSpec((tm, c), lambda i: (i, 0))
    return host_call(
        body,
        name="mix_out_fwd",
        grid=(t // tm,),
        in_specs=[row(d), row(2 * d), row(2 * d), row(d), _w1024_spec(d, 2, 0), _w1024_spec(d, 1, 2), _w1024_spec(d, 1, 3)],
        out_specs=[row(d), row(d), row(d), row(d)],
        out_shape=[jax.ShapeDtypeStruct((t, d), F32), jax.ShapeDtypeStruct((t, d), F32), jax.ShapeDtypeStruct((t, d), F32),
                   jax.ShapeDtypeStruct((t, d), BF)],
        scratch_shapes=[],
        operands=(ya_in, yn, gates, h, w1024, w1024, w1024),
        rider=rider,
    )


def mix_out_bwd(dh, gates, y_a, y_m, w1024):
    t, d = dh.shape
    tm = _tile(t, 256)

    def body(dh_ref, g_ref, ya_ref, ym_ref, wm_ref, wa_ref, wo_ref, dg_ref, dya_ref, dyn_ref, da_ref, dm_ref):
        dmg = _dot(dh_ref[...].astype(BF), _whole(wo_ref), 1, 1)
        gv = g_ref[...].astype(F32)
        sa, sm = _sigmoid(gv[:, :d]), _sigmoid(gv[:, d:])
        dg_ref[:, :d] = (dmg * ya_ref[...] * sa * (1.0 - sa)).astype(BF)
        dg_ref[:, d:] = (dmg * ym_ref[...] * sm * (1.0 - sm)).astype(BF)
        da = (dmg * sa).astype(BF)
        dm = (dmg * sm).astype(BF)
        da_ref[...] = da
        dm_ref[...] = dm
        dya_ref[...] = _dot(da, _whole(wa_ref), 1, 1)
        dyn_ref[...] = _dot(dm, _whole(wm_ref), 1, 1)

    row = lambda c: pl.BlockSpec((tm, c), lambda i: (i, 0))
    return pl.pallas_call(
        body,
        name="mix_out_bwd",
        grid=(t // tm,),
        in_specs=[row(d), row(2 * d), row(d), row(d), _w1024_spec(d, 2, 0), _w1024_spec(d, 1, 2), _w1024_spec(d, 1, 3)],
        out_specs=[row(2 * d), row(d), row(2 * d), row(d), row(d)],
        out_shape=[jax.ShapeDtypeStruct((t, 2 * d), BF), jax.ShapeDtypeStruct((t, d), F32), jax.ShapeDtypeStruct((t, 2 * d), F32),
                   jax.ShapeDtypeStruct((t, d), BF), jax.ShapeDtypeStruct((t, d), BF)],
        compiler_params=_params(("parallel",)),
    )(dh, gates, y_a, y_m, w1024, w1024, w1024)


def norm_bwd_add(dh, h, g, dn):
    def fn(i, nt, rows, vecs, prevs, nexts):
        dx, dg = _rms_bwd(rows[1], vecs[0], rows[2])
        return [rows[0] + dx], [dg]
    d = h.shape[1]
    return ew(fn, [dh, h, dn], [g], [(d, F32)], [(1, d)], tm=512, name="norm_bwd_add")


def _pe(p, wpp_ref):
    pb = p.astype(BF)
    return jnp.concatenate([_dot(pb, wpp_ref[k]) for k in range(NCHIP)], axis=1)


def ple_fwd(h, g, p, w1024, wpp):
    t, d = h.shape
    tm = _tile(t, 512)

    def body(h_ref, g_ref, p_ref, wg_ref, wp_ref, ho_ref):
        hv = h_ref[...]
        gate = _sigmoid(_dot(_rms(hv, g_ref[...]).astype(BF), _whole(wg_ref)))
        ho_ref[...] = hv + gate * _pe(p_ref[...], wp_ref)

    row = lambda c: pl.BlockSpec((tm, c), lambda i: (i, 0))
    wpp_spec = pl.BlockSpec(wpp.shape, lambda i: (0, 0, 0))
    return pl.pallas_call(
        body,
        name="ple_fwd",
        grid=(t // tm,),
        in_specs=[row(d), pl.BlockSpec((1, d), lambda i: (0, 0)), row(p.shape[1]), _w1024_spec(d, 1, 4), wpp_spec],
        out_specs=row(d),
        out_shape=jax.ShapeDtypeStruct((t, d), F32),
        compiler_params=_params(("parallel",)),
    )(h, g, p, w1024, wpp)


def ple_bwd(dho, h, g, p, w1024, wpp):
    t, d = h.shape
    tm = _tile(t, 512)

    def body(dho_ref, h_ref, g_ref, p_ref, wg_ref, wp_ref, dh_ref, dg_ref, n_ref, dgp_ref, dpe_ref):
        hv, dv = h_ref[...], dho_ref[...]
        n = _rms(hv, g_ref[...]).astype(BF)
        n_ref[...] = n
        wg = _whole(wg_ref)
        gate = _sigmoid(_dot(n, wg))
        pe = _pe(p_ref[...], wp_ref)
        dpe_ref[...] = (dv * gate).astype(BF)
        dgp = (dv * pe * gate * (1.0 - gate)).astype(BF)
        dgp_ref[...] = dgp
        dx, dg = _rms_bwd(hv, g_ref[...], _dot(dgp, wg, 1, 1))
        dh_ref[...] = dv + dx
        _accumulate(dg_ref, dg, pl.program_id(0) == 0)

    row = lambda c: pl.BlockSpec((tm, c), lambda i: (i, 0))
    wpp_spec = pl.BlockSpec(wpp.shape, lambda i: (0, 0, 0))
    return pl.pallas_call(
        body,
        name="ple_bwd",
        grid=(t // tm,),
        in_specs=[row(d), row(d), pl.BlockSpec((1, d), lambda i: (0, 0)), row(p.shape[1]), _w1024_spec(d, 1, 4), wpp_spec],
        out_specs=[row(d), pl.BlockSpec((1, d), lambda i: (0, 0)), row(d), row(d), row(d)],
        out_shape=[jax.ShapeDtypeStruct((t, d), F32), jax.ShapeDtypeStruct((1, d), F32), jax.ShapeDtypeStruct((t, d), BF),
                   jax.ShapeDtypeStruct((t, d), BF), jax.ShapeDtypeStruct((t, d), BF)],
        compiler_params=_params(("arbitrary",)),
    )(dho, h, g, p, w1024, wpp)


def loss_bwd(h, g, target):
    d = h.shape[1]

    def fn(i, nt, rows, vecs, prevs, nexts):
        err = _rms(rows[0], vecs[0]) - rows[1]
        dx, dg = _rms_bwd(rows[0], vecs[0], err * (1.0 / d))
        return [dx], [jnp.sum(err * err, axis=0, keepdims=True) * (0.5 / d), dg]

    return ew(fn, [h, target], [g], [(d, F32)], [(1, d), (1, d)], tm=512, name="loss_bwd")


def adamw(w, g, m, v, name):
    c1, c2 = 1.0 / (1.0 - ADAM_B1 ** ADAM_STEP), 1.0 / (1.0 - ADAM_B2 ** ADAM_STEP)

    def fn(i, nt, rows, vecs, prevs, nexts):
        wv, gv, mv, vv = rows
        mn = ADAM_B1 * mv + (1.0 - ADAM_B1) * gv
        vn = ADAM_B2 * vv + (1.0 - ADAM_B2) * (gv * gv)
        delta = -ADAM_LR * ((mn * c1) / (jnp.sqrt(vn * c2) + ADAM_EPS) + ADAM_WD * wv)
        return [delta, mn, vn], []

    c = w.shape[1]
    return ew(fn, [w, g, m, v], [], [(c, F32)] * 3, tm=_row_tile(w.shape[0], c, HALO), name=name)


def _place():
    return lax.axis_index("x"), lax.axis_index("y"), lax.axis_index("c")


def _other_chips(x, y):
    return [(1 - x, y), (x, 1 - y), (1 - x, 1 - y)]


ANY = pl.BlockSpec(memory_space=pl.ANY)


def _comm_call(body, name, ins, out_shapes, n_sems, aliases=None):
    return pl.pallas_call(
        body,
        name=name,
        in_specs=[ANY] * len(ins),
        out_specs=[ANY] * len(out_shapes),
        out_shape=out_shapes,
        scratch_shapes=[pltpu.SemaphoreType.DMA((n_sems,)), pltpu.SemaphoreType.DMA((n_sems,))],
        input_output_aliases=aliases or {},
    )(*ins)


def gather_rider(packs):
    nt = len(packs)

    def pieces(ins, outs, send_sems, recv_sems):
        x, y, cc = _place()
        chips = _other_chips(x, y)
        sibling = (x, y, 1 - cc)
        k_me = 2 * x + y

        def copy(k, src, dst, to):
            return pltpu.make_async_remote_copy(src_ref=src, dst_ref=dst, send_sem=send_sems.at[k], recv_sem=recv_sems.at[k],
                                                device_id=to, device_id_type=MESH)

        sends, forwards, arrivals = [], [], []
        for ti in range(nt):
            for j, (px, py) in enumerate(chips):
                sends.append(copy(7 * ti + j, ins[ti].at[cc], outs[ti].at[k_me, cc], (px, py, cc)))
                landed = outs[ti].at[2 * px + py, cc]
                forwards.append((copy(7 * ti + j, landed, landed, (px, py, cc)), copy(7 * ti + 3 + j, landed, landed, sibling)))
                passed = outs[ti].at[2 * px + py, 1 - cc]
                arrivals.append(copy(7 * ti + 3 + j, passed, passed, sibling))
            sends.append(copy(7 * ti + 6, ins[ti], outs[ti].at[k_me], sibling))
            own = outs[ti].at[k_me]
            arrivals.append(copy(7 * ti + 6, own, own, sibling))
        return sends, forwards, arrivals

    def start(*parts):
        for cp in pieces(*parts)[0]:
            cp.start()

    def mid(*parts):
        for landed, forward in pieces(*parts)[1]:
            landed.wait_recv()
            forward.start()

    def finish(*parts):
        sends, forwards, arrivals = pieces(*parts)
        for cp in arrivals:
            cp.wait_recv()
        for cp in sends + [f for _, f in forwards]:
            cp.wait_send()

    return Rider(packs, [jax.ShapeDtypeStruct((NCHIP,) + p.shape, p.dtype) for p in packs], 7 * nt, start, finish, mid)


def swap_packs(gs, name):
    nt = len(gs)
    hl = gs[0].shape[1] // 2

    def body(*refs):
        ins, outs, (send_sems, recv_sems) = refs[:nt], refs[nt:2 * nt], refs[2 * nt:]
        x, y, cc = _place()
        theirs = pl.ds((1 - cc) * hl, hl)
        cps = [pltpu.make_async_remote_copy(src_ref=ins[ti].at[:, theirs], dst_ref=outs[ti], send_sem=send_sems.at[ti], recv_sem=recv_sems.at[ti],
                                            device_id=(x, y, 1 - cc), device_id_type=MESH) for ti in range(nt)]
        for cp in cps:
            cp.start()
        for cp in cps:
            cp.wait()

    return _comm_call(body, name, gs, [jax.ShapeDtypeStruct((NCHIP, hl) + g.shape[2:], g.dtype) for g in gs], nt)


def scatter_packs(cs, name):
    return scatter_rider(cs).standalone(name)


def scatter_rider(cs):
    nt = len(cs)

    def copies(ins, outs, send_sems, recv_sems):
        x, y, cc = _place()
        cps = []
        for ti in range(nt):
            for j, (px, py) in enumerate(_other_chips(x, y)):
                cps.append(pltpu.make_async_remote_copy(src_ref=ins[ti].at[2 * px + py], dst_ref=outs[ti].at[j], send_sem=send_sems.at[3 * ti + j],
                                                        recv_sem=recv_sems.at[3 * ti + j], device_id=(px, py, cc), device_id_type=MESH))
        return cps

    def start(*parts):
        for cp in copies(*parts):
            cp.start()

    def finish(*parts):
        for cp in copies(*parts):
            cp.wait()

    return Rider(cs, [jax.ShapeDtypeStruct((3,) + c_.shape[1:], c_.dtype) for c_ in cs], 3 * nt, start, finish)


def join_packs(fulls, name):
    nt = len(fulls)
    hl = fulls[0].shape[0] // 2

    def body(*refs):
        ins, outs, (send_sems, recv_sems) = refs[:nt], refs[nt:2 * nt], refs[2 * nt:]
        x, y, cc = _place()
        mine = pl.ds(cc * hl, hl)
        cps = [pltpu.make_async_remote_copy(src_ref=ins[ti].at[mine], dst_ref=outs[ti].at[mine], send_sem=send_sems.at[ti], recv_sem=recv_sems.at[ti],
                                            device_id=(x, y, 1 - cc), device_id_type=MESH) for ti in range(nt)]
        for cp in cps:
            cp.start()
        for cp in cps:
            cp.wait()

    return _comm_call(body, name, fulls, [jax.ShapeDtypeStruct(f.shape, f.dtype) for f in fulls], nt, aliases={ti: ti for ti in range(nt)})


def add_sibling(g, recv, name):
    _, nl, r, c = g.shape
    hl = nl // 2
    tm, tc = _tile2(r, c)

    def body(g_ref, r_ref, o_ref):
        o_ref[...] = (g_ref[...].astype(F32) + r_ref[...].astype(F32)).astype(o_ref.dtype)

    blk = (None, None, tm, tc)
    return pl.pallas_call(
        body,
        name=name,
        grid=(NCHIP, hl, r // tm, c // tc),
        in_specs=[pl.BlockSpec(blk, lambda k, l, i, j: (k, lax.axis_index("c") * hl + l, i, j)), pl.BlockSpec(blk, lambda k, l, i, j: (k, l, i, j))],
        out_specs=pl.BlockSpec(blk, lambda k, l, i, j: (k, l, i, j)),
        out_shape=jax.ShapeDtypeStruct(recv.shape, BF),
        compiler_params=_params(("parallel",) * 4),
    )(g, recv)


def add_chips(cs, got, nl, name):
    _, hl, r, c = cs.shape
    tm, tc = _tile2(r, c)

    def body(own_ref, got_ref, o_ref):
        o_ref[...] = own_ref[...].astype(F32) + got_ref[0].astype(F32) + got_ref[1].astype(F32) + got_ref[2].astype(F32)

    return pl.pallas_call(
        body,
        name=name,
        grid=(hl, r // tm, c // tc),
        in_specs=[pl.BlockSpec((None, None, tm, tc), lambda l, i, j: (2 * lax.axis_index("x") + lax.axis_index("y"), l, i, j)),
                  pl.BlockSpec((3, None, tm, tc), lambda l, i, j: (0, l, i, j))],
        out_specs=pl.BlockSpec((None, tm, tc), lambda l, i, j: (lax.axis_index("c") * hl + l, i, j)),
        out_shape=jax.ShapeDtypeStruct((nl, r, c), F32),
        compiler_params=_params(("parallel",) * 3),
    )(cs, got)


def all_gather_xy(shard, name):
    r, c = shard.shape
    hr = r // 2
    assert r % 32 == 0

    def body(x_ref, out_ref, send_sems, recv_sems, local_sem):
        x, y, cc = _place()
        chips = _other_chips(x, y)
        mine = pl.ds(pl.multiple_of(cc * hr, 16), hr)
        theirs = pl.ds(pl.multiple_of((1 - cc) * hr, 16), hr)
        k_me = 2 * x + y

        def copy(k, src, dst, to):
            return pltpu.make_async_remote_copy(src_ref=src, dst_ref=dst, send_sem=send_sems.at[k], recv_sem=recv_sems.at[k],
                                                device_id=to, device_id_type=MESH)

        own = pltpu.make_async_copy(x_ref, out_ref.at[k_me], local_sem)
        own.start()
        first = [copy(j, x_ref.at[mine], out_ref.at[k_me, mine], (*chip, cc)) for j, chip in enumerate(chips)]
        for cp in first:
            cp.start()
        passed = []
        for j, (px, py) in enumerate(chips):
            landed = out_ref.at[2 * px + py, mine]
            copy(j, landed, landed, (px, py, cc)).wait_recv()
            fw = copy(3 + j, landed, landed, (x, y, 1 - cc))
            fw.start()
            passed.append(fw)
        for j, (px, py) in enumerate(chips):
            landed = out_ref.at[2 * px + py, theirs]
            copy(3 + j, landed, landed, (x, y, 1 - cc)).wait_recv()
        for cp in first + passed:
            cp.wait_send()
        own.wait()

    return pl.pallas_call(
        body,
        name=name,
        in_specs=[ANY],
        out_specs=ANY,
        out_shape=jax.ShapeDtypeStruct((NCHIP, r, c), shard.dtype),
        scratch_shapes=[pltpu.SemaphoreType.DMA((6,)), pltpu.SemaphoreType.DMA((6,)), pltpu.SemaphoreType.DMA],
    )(shard)


def all_gather_8(block, name):
    m, c = block.shape

    def body(x_ref, out_ref, send_sems, recv_sems, local_sem):
        x, y, cc = _place()
        me, sibling = (x, y, cc), (x, y, 1 - cc)
        chips = _other_chips(x, y)

        def rows(px, py, pc):
            return out_ref.at[4 * px + 2 * py + pc]

        def copy(k, blk, to, src=None):
            return pltpu.make_async_remote_copy(src_ref=rows(*blk) if src is None else src, dst_ref=rows(*blk), send_sem=send_sems.at[k],
                                                recv_sem=recv_sems.at[k], device_id=to, device_id_type=MESH)

        mine = pltpu.make_async_copy(x_ref, rows(*me), local_sem)
        mine.start()
        first = [copy(0, me, sibling, src=x_ref)]
        first += [copy(1 + j, me, (*chip, cc), src=x_ref) for j, chip in enumerate(chips)]
        for cp in first:
            cp.start()
        passed = [copy(4 + j, (*chip, cc), sibling) for j, chip in enumerate(chips)]
        for j, chip in enumerate(chips):
            copy(1 + j, (*chip, cc), me).wait_recv()
            passed[j].start()
        copy(0, sibling, me).wait_recv()
        for j, chip in enumerate(chips):
            copy(4 + j, (*chip, 1 - cc), me).wait_recv()
        for cp in first + passed:
            cp.wait_send()
        mine.wait()

    return pl.pallas_call(
        body,
        name=name,
        in_specs=[pl.BlockSpec(memory_space=pltpu.VMEM)],
        out_specs=pl.BlockSpec(memory_space=pltpu.VMEM),
        out_shape=jax.ShapeDtypeStruct((8, m, c), block.dtype),
        scratch_shapes=[pltpu.SemaphoreType.DMA((7,)), pltpu.SemaphoreType.DMA((7,)), pltpu.SemaphoreType.DMA],
        compiler_params=pltpu.CompilerParams(vmem_limit_bytes=VMEM_LIMIT),
    )(block)


def add_parts(parts, out_dtype, name, tm=512):
    def fn(i, nt, rows, vecs, prevs, nexts):
        acc = rows[0]
        for r_ in rows[1:]:
            acc = acc + r_
        return [acc], []
    r, c = parts[0].shape
    return ew(fn, list(parts), [], [(c, out_dtype)], tm=_tile(r, tm, 16), name=name)[0]


SMALL_SHARDED = ("sc_conv_w", "m_conv_w")
SMALL_REPL = ("ffn1_norm", "mix_norm", "m_conv_b", "m_dt_bias", "m_A_log", "m_D", "m_norm", "ffn2_norm", "ple_norm", "final_norm")
BIG = ("ffn1_wg", "ffn1_wu", "ffn1_wd", "w_in", "sc_w_out", "m_w_out", "w_o", "ffn2_wg", "ffn2_wu", "ffn2_wd", "ple_w_gate", "ple_w_proj")
TRANSPOSED = ("ffn1_wg", "ffn1_wu", "ffn2_wg", "ffn2_wu", "w_in")
ORDER = ("ffn1_norm", "ffn1_wg", "ffn1_wu", "ffn1_wd", "mix_norm", "w_in", "sc_conv_w", "sc_w_out", "m_conv_w", "m_conv_b", "m_dt_bias",
         "m_A_log", "m_D", "m_norm", "m_w_out", "w_o", "ffn2_norm", "ffn2_wg", "ffn2_wu", "ffn2_wd", "ple_norm", "ple_w_gate", "ple_w_proj",
         "final_norm")


def _pack(arrs, cols, row_mult):
    flat = jnp.concatenate([a.reshape(-1) for a in arrs])
    n = flat.shape[0]
    rows = -(-n // cols)
    rows = -(-rows // row_mult) * row_mult
    return jnp.pad(flat, (0, rows * cols - n)).reshape(rows, cols)


def _unpack(flat2d, shapes):
    flat = flat2d.reshape(-1)
    out, off = [], 0
    for s in shapes:
        n = int(np.prod(s))
        out.append(flat[off:off + n].reshape(s))
        off += n
    return out


def _row_cat(arrs, dtype):
    return jnp.concatenate([a.astype(dtype) for a in arrs], axis=1)


def kernel(x, p, ffn1_norm, ffn1_wg, ffn1_wu, ffn1_wd, mix_norm, w_in, sc_conv_w, sc_w_out, m_conv_w, m_conv_b, m_dt_bias, m_A_log, m_D, m_norm, m_w_out, w_o, ffn2_norm, ffn2_wg, ffn2_wu, ffn2_wd, ple_norm, ple_w_gate, ple_w_proj, final_norm, loss_target, m_ffn1_norm, m_ffn1_wg, m_ffn1_wu, m_ffn1_wd, m_mix_norm, m_w_in, m_sc_conv_w, m_sc_w_out, m_m_conv_w, m_m_conv_b, m_m_dt_bias, m_m_A_log, m_m_D, m_m_norm, m_m_w_out, m_w_o, m_ffn2_norm, m_ffn2_wg, m_ffn2_wu, m_ffn2_wd, m_ple_norm, m_ple_w_gate, m_ple_w_proj, m_final_norm, v_ffn1_norm, v_ffn1_wg, v_ffn1_wu, v_ffn1_wd, v_mix_norm, v_w_in, v_sc_conv_w, v_sc_w_out, v_m_conv_w, v_m_conv_b, v_m_dt_bias, v_m_A_log, v_m_D, v_m_norm, v_m_w_out, v_w_o, v_ffn2_norm, v_ffn2_wg, v_ffn2_wu, v_ffn2_wd, v_ple_norm, v_ple_w_gate, v_ple_w_proj, v_final_norm):
    args = dict(locals())
    wts = {n: args[n] for n in ORDER}
    mom = {n: args["m_" + n] for n in ORDER}
    vel = {n: args["v_" + n] for n in ORDER}

    depth = ffn1_norm.shape[0]
    d = x.shape[-1]
    w = 2 * d
    hh = w // SSM_P
    cw = w + 2 * SSM_G * SSM_N
    d4 = d // NCHIP
    my_x, my_y, my_c = _place()
    k_me = 2 * my_x + my_y

    tr = lambda a: jnp.swapaxes(a, 1, 2)
    gu_t = [_row_cat([tr(wg_), tr(wu_)], BF) for wg_, wu_ in ((ffn1_wg, ffn1_wu), (ffn2_wg, ffn2_wu))]
    wd_l = [ffn1_wd.astype(BF), ffn2_wd.astype(BF)]
    w1024_l = _row_cat([m_w_out, sc_w_out, w_o, ple_w_gate], BF)
    win_l, wpp_l = tr(w_in).astype(BF), ple_w_proj.astype(BF)
    halves = lambda a: a.reshape(2, a.shape[0] // 2, a.shape[1])
    whole = lambda g: g.reshape(NCHIP, g.shape[2] * 2, g.shape[3])

    def pieces(l):
        return {"small": [halves(w1024_l[l]), halves(wpp_l[l])], "win": [halves(win_l[l])], "gu1": [halves(gu_t[0][l])], "d1": [halves(wd_l[0][l])],
                "gu2": [halves(gu_t[1][l])], "d2": [halves(wd_l[1][l])]}

    small_local = [sc_conv_w, m_conv_w]
    gathered_s = all_gather_xy(_pack(small_local, LANE, 32), "gather_conv_weights")
    per_shard_s = [_unpack(gathered_s[k], [a.shape for a in small_local]) for k in range(NCHIP)]
    sc_conv_full = jnp.concatenate([per_shard_s[k][0] for k in range(NCHIP)], axis=2)
    m_conv_full = jnp.concatenate([per_shard_s[k][1] for k in range(NCHIP)], axis=2)

    pad_h = lambda a: jnp.pad(a, ((0, 0), (0, LANE - hh)))
    dt_bias_p, a_log_p = pad_h(m_dt_bias), pad_h(m_A_log)
    d_exp = jnp.repeat(m_D, SSM_P, axis=1)
    e_mat = (jnp.arange(w)[None, :] // SSM_P == jnp.arange(LANE)[:, None]).astype(F32)
    et_mat = e_mat.T
    o_z, o_xbc, o_dt, o_g = 3 * d, 5 * d, 5 * d + cw, 5 * d + cw + hh

    def layer_weights(got):
        wt = {"w1024": whole(got["small"][0]), "wpp": whole(got["small"][1])}
        wt.update({k: whole(got[k][0]) for k in ("gu1", "d1", "gu2", "d2")})
        wi = whole(got["win"][0]).reshape(-1, d)
        wt["sc3"], wt["z"], wt["xbc"], wt["g2"] = wi[:o_z], wi[o_z:o_xbc], wi[o_xbc:o_dt], wi[o_g:o_g + 2 * d]
        wt["dt"] = jnp.pad(wi[o_dt:o_g], ((0, LANE - hh), (0, 0)))
        wt["in_p"] = jnp.concatenate([wt["sc3"], wt["z"], wt["xbc"], wt["g2"], wt["dt"]], axis=0)
        return wt

    first = pieces(0)
    order = ("gu1", "d1", "win", "small", "gu2", "d2")
    flat = gather_rider([a for k in order for a in first[k]]).standalone("gather_weights")
    got, pos = {}, 0
    for k in order:
        got[k] = flat[pos:pos + len(first[k])]
        pos += len(first[k])
    wts_l = [layer_weights(got)]

    h = x[0]
    saved = []
    for i in range(depth):
        s, wt = {}, wts_l[i]
        nxt = pieces(i + 1) if i + 1 < depth else None
        ride = lambda k: gather_rider(nxt[k]) if nxt else None
        got = {}
        s["h0"] = h
        (s["ab1"], s4, s["n1"]), got["small"] = ffn_up(h, ffn1_norm[i:i + 1], wt["gu1"], rider=ride("small"))
        h, got["d1"] = ffn_down(s4, wt["d1"], h, rider=ride("d1"))
        s["h1"] = h
        u = norm_cast(h, mix_norm[i:i + 1])
        s["u"] = u
        s["sc3"] = mm(u, wt["sc3"], tb=True, out_dtype=BF, name="proj_sc")
        s["z"] = mm(u, wt["z"], tb=True, out_dtype=BF, name="proj_z")
        s["xbc_raw"] = mm(u, wt["xbc"], tb=True, out_dtype=BF, name="proj_xbc")
        s["gates"] = mm(u, wt["g2"], tb=True, out_dtype=BF, name="proj_gates")
        s["dt_raw"] = mm(u, wt["dt"], tb=True, name="proj_dt")
        s["ya_in"] = conv_a_fwd(s["sc3"], sc_conv_full[i])
        s["xbc"], s["dt"] = conv_m_fwd(s["xbc_raw"], s["dt_raw"], m_conv_full[i], m_conv_b[i:i + 1], dt_bias_p[i:i + 1])
        (s["yn"], s["y"], s["sprev"]), got["win"] = ssd_fwd(s["xbc"], s["dt"], s["z"], a_log_p[i:i + 1], d_exp[i:i + 1], m_norm[i:i + 1], e_mat,
                                                            rider=ride("win"))
        (h, s["y_a"], s["y_m"], s["merged"]), got["gu2"] = mix_out_fwd(s["ya_in"], s["yn"], s["gates"], h, wt["w1024"], rider=ride("gu2"))
        s["h2"] = h
        (s["ab2"], s4, s["n2"]), got["gu1"] = ffn_up(h, ffn2_norm[i:i + 1], wt["gu2"], rider=ride("gu1"))
        h, got["d2"] = ffn_down(s4, wt["d2"], h, rider=ride("d2"))
        s["h3"] = h
        h = ple_fwd(h, ple_norm[i:i + 1], p[i, 0], wt["w1024"], wt["wpp"])
        saved.append(s)
        if nxt:
            wts_l.append(layer_weights(got))

    dh, loss_lanes, g_final = loss_bwd(h, final_norm[None, :], loss_target[0])
    loss = lax.psum(jnp.sum(loss_lanes), ("x", "y", "c"))

    def finish_reduce(cs, got):
        halves = [add_chips(c_, g_, 2, "grad_add_chips") for c_, g_ in zip(cs, got, strict=True)]
        return [f.reshape(-1, f.shape[2]) for f in join_packs(halves, "grad_join_halves")]

    pending, reduced = None, [None] * depth
    gs = {n: [None] * depth for n in SMALL_SHARDED + SMALL_REPL if n != "final_norm"}
    for i in reversed(range(depth)):
        s = saved[i]
        wt = wts_l[i]
        dh, gs["ple_norm"][i], n3, dgp, dpe = ple_bwd(dh, s["h3"], ple_norm[i:i + 1], p[i, 0], wt["w1024"], wt["wpp"])
        g_pg = mm(n3, dgp, ta=True, out_dtype=BF, name="g_ple_gate", tm_cap=512, tn_cap=512)
        g_pp = mm(p[i, 0], dpe, ta=True, out_dtype=BF, name="g_ple_proj", tm_cap=512, tn_cap=512)
        g_pp = jnp.transpose(g_pp.reshape(g_pp.shape[0], NCHIP, d4), (1, 0, 2))
        dn2, s2, dab2 = ffn_bwd(dh, s["ab2"], wt["gu2"], wt["d2"])
        g_ffn2 = ffn_wgrads(s["n2"], dh, s2, dab2)
        dh, gs["ffn2_norm"][i] = norm_bwd_add(dh, s["h2"], ffn2_norm[i:i + 1], dn2)
        dgates, dya, dyn, dy_a, dy_m = mix_out_bwd(dh, s["gates"], s["y_a"], s["y_m"], wt["w1024"])
        g_wo = mm(s["merged"], dh, ta=True, out_dtype=BF, name="g_w_o", tm_cap=512, tn_cap=512)
        g_sco = mm(s["ya_in"], dy_a, ta=True, out_dtype=BF, name="g_sc_out", tm_cap=512, tn_cap=512)
        g_mo = mm(s["yn"], dy_m, ta=True, out_dtype=BF, name="g_m_out", tm_cap=512, tn_cap=512)
        g_1024 = jnp.concatenate([g_mo.reshape(NCHIP, 2 * d4, d), g_sco.reshape(NCHIP, d4, d), g_wo.reshape(NCHIP, d4, d),
                                  g_pg.reshape(NCHIP, d4, d)], axis=1)
        (dz, dxbc, ddt, gs["m_norm"][i], gd, gal), got_a = ssd_bwd(dyn, s["y"], s["z"], s["xbc"], s["dt"], s["sprev"], a_log_p[i:i + 1], d_exp[i:i + 1],
                                                                   m_norm[i:i + 1], e_mat, et_mat, rider=scatter_rider(pending[:1]) if pending else None)
        gs["m_D"][i], gs["m_A_log"][i] = gd[:, :hh], gal[:, :hh]
        dpre, ddt_raw, gdb = conv_m_bwd1(dxbc, s["xbc_raw"], ddt, s["dt_raw"], m_conv_full[i], m_conv_b[i:i + 1], dt_bias_p[i:i + 1])
        gs["m_dt_bias"][i] = gdb[:, :hh]
        dxbc_raw, gs["m_conv_w"][i], gs["m_conv_b"][i] = conv_bwd2(dpre, s["xbc_raw"], m_conv_full[i], "conv_m_bwd2")
        dcv, dsc_b, v = conv_a_bwd1(dya, s["sc3"], sc_conv_full[i])
        dsc_c, dsc_x, gs["sc_conv_w"][i] = conv_a_bwd2(dcv, v, s["sc3"], sc_conv_full[i])
        dproj = jnp.concatenate([dsc_b, dsc_c, dsc_x, dz, dxbc_raw, dgates, ddt_raw], axis=1)
        if pending:
            du, got_b = mm(dproj, wt["in_p"], name="d_proj_in", tn_cap=512, rider=scatter_rider(pending[2:3]))
            gwp, got_c = mm(dproj, s["u"], ta=True, out_dtype=BF, name="g_w_in", tm_cap=1152, tn_cap=512, rider=scatter_rider(pending[1:2] + pending[3:]))
            reduced[i + 1] = finish_reduce(pending, [got_a[0], got_c[0], got_b[0], got_c[1]])
        else:
            du = mm(dproj, wt["in_p"], name="d_proj_in", tn_cap=512)
            gwp = mm(dproj, s["u"], ta=True, out_dtype=BF, name="g_w_in", tm_cap=1152, tn_cap=512)
        gw_rows = jnp.concatenate([gwp[:5 * d + cw], gwp[7 * d + cw:7 * d + cw + hh], gwp[5 * d + cw:7 * d + cw]], axis=0)
        g_in = gw_rows.reshape(NCHIP, -1, d)
        dh, gs["mix_norm"][i] = norm_bwd_add(dh, s["h1"], mix_norm[i:i + 1], du)
        dn1, s1, dab1 = ffn_bwd(dh, s["ab1"], wt["gu1"], wt["d1"])
        g_ffn1 = ffn_wgrads(s["n1"], dh, s1, dab1)
        dh, gs["ffn1_norm"][i] = norm_bwd_add(dh, s["h0"], ffn1_norm[i:i + 1], dn1)
        g_layer = [jnp.concatenate([g_ffn1, g_ffn2], axis=1), g_1024, g_in, g_pp]
        g_layer = [g.reshape(NCHIP, 2, g.shape[1] // 2, g.shape[2]) for g in g_layer]
        from_sibling = swap_packs(g_layer, "grad_swap_halves")
        pending = [add_sibling(g, r_, "grad_add_sibling") for g, r_ in zip(g_layer, from_sibling, strict=True)]
    reduced[0] = finish_reduce(pending, scatter_packs(pending, "grad_scatter"))
    grad_x = dh[None]

    rf, r1024, rin, rpp = (jnp.stack([reduced[l][j] for l in range(depth)]) for j in range(4))
    f4 = rf.shape[1] // 6
    ffn_rows = lambda j: rf[:, j * f4:(j + 1) * f4]
    grads = {
        "ffn1_wg": ffn_rows(0), "ffn1_wu": ffn_rows(1), "ffn1_wd": ffn_rows(2), "ffn2_wg": ffn_rows(3), "ffn2_wu": ffn_rows(4), "ffn2_wd": ffn_rows(5),
        "m_w_out": r1024[:, :2 * d4], "sc_w_out": r1024[:, 2 * d4:3 * d4], "w_o": r1024[:, 3 * d4:4 * d4], "ple_w_gate": r1024[:, 4 * d4:],
        "w_in": rin, "ple_w_proj": rpp,
    }

    small_names = list(SMALL_SHARDED + SMALL_REPL)
    small_full = [g_final[0] if n == "final_norm" else jnp.stack(gs[n]) for n in small_names]
    small_pack = _pack(small_full, LANE, HALO)
    all8 = all_gather_8(small_pack, "gather_small_grads")
    small_sum = add_parts([all8[k] for k in range(8)], F32, "add_small_grads", tm=256)
    for n, tot in zip(small_names, _unpack(small_sum, [a.shape for a in small_full]), strict=True):
        if n in SMALL_SHARDED:
            cl = wts[n].shape[2]
            grads[n] = lax.dynamic_slice_in_dim(tot, k_me * cl, cl, axis=2)
        else:
            grads[n] = tot.reshape(wts[n].shape)

    delta, new_m, new_v = {}, {}, {}
    for n in BIG:
        view = tr if n in TRANSPOSED else (lambda a: a)
        shp = grads[n].shape
        two = lambda a: a.reshape(-1, shp[-1])
        dl, nm, nv = adamw(two(view(wts[n])), two(grads[n]), two(view(mom[n])), two(view(vel[n])), "adamw_" + "x".join(map(str, shp[1:])))
        grads[n], delta[n], new_m[n], new_v[n] = view(grads[n]), view(dl.reshape(shp)), view(nm.reshape(shp)), view(nv.reshape(shp))
    sm_shapes = [wts[n].shape for n in small_names]
    pk = lambda dct: _pack([dct[n] for n in small_names], LANE, HALO)
    dl, nm, nv = adamw(pk(wts), pk(grads), pk(mom), pk(vel), "adamw_small")
    for n, a, b_, c_ in zip(small_names, _unpack(dl, sm_shapes), _unpack(nm, sm_shapes), _unpack(nv, sm_shapes), strict=True):
        delta[n], new_m[n], new_v[n] = a, b_, c_

    return (loss, grad_x, *[grads[n] for n in ORDER], *[delta[n] for n in ORDER], *[new_m[n] for n in ORDER], *[new_v[n] for n in ORDER])
```

```python
import jax
import jax.numpy as jnp
import numpy as np
from jax import lax
from jax.experimental import pallas as pl
from jax.experimental.pallas import tpu as pltpu

BF = jnp.bfloat16
F32 = jnp.float32
EPS = 1e-6
LANE = 128
HALO = 8
SSM_P = 64
SSM_N = 128
SSM_G = 4
SSM_L = 128
ADAM_LR, ADAM_B1, ADAM_B2, ADAM_EPS, ADAM_WD, ADAM_STEP = 0.001, 0.9, 0.999, 1e-08, 0.01, 10
VMEM_LIMIT = 56 * 1024 * 1024
TILE_ELEMS = 400_000
NCHIP = 4
FFN_SUB = 256
MESH = pl.DeviceIdType.MESH
HI = lax.Precision.HIGHEST


def _tile(n, cap, mult=LANE):
    best = None
    t = mult
    while t <= min(n, cap):
        if n % t == 0:
            best = t
        t += mult
    return best if best is not None else n


def _row_tile(r, c, mult=16):
    return _tile(r, max(mult, TILE_ELEMS // c // mult * mult), mult)


def _tile2(r, c, mult=16):
    tm = _row_tile(r, c, mult)
    tc = c if tm * c <= TILE_ELEMS else _tile(c, max(LANE, TILE_ELEMS // tm // LANE * LANE))
    return tm, tc


def _params(sem):
    return pltpu.CompilerParams(dimension_semantics=sem, vmem_limit_bytes=VMEM_LIMIT)


def _sigmoid(x):
    return 1.0 / (1.0 + jnp.exp(-x))


def _dot(a, b, ca=1, cb=0, precision=None):
    return lax.dot_general(a, b, (((ca,), (cb,)), ((), ())), precision=precision, preferred_element_type=F32)


def _rms(x, g):
    r = lax.rsqrt(jnp.mean(x * x, axis=-1, keepdims=True) + EPS)
    return x * r * g


def _rms_bwd(x, g, dy):
    r = lax.rsqrt(jnp.mean(x * x, axis=-1, keepdims=True) + EPS)
    xh = x * r
    dxh = dy * g
    dx = r * (dxh - xh * jnp.mean(dxh * xh, axis=-1, keepdims=True))
    return dx, jnp.sum(dy * xh, axis=0, keepdims=True)


def _accumulate(ref, val, first):
    @pl.when(first)
    def _():
        ref[...] = val

    @pl.when(jnp.logical_not(first))
    def _():
        ref[...] += val


RIDER_MID = 1.0


class Rider:
    def __init__(self, ins, out_shapes, n_sems, start, finish, mid=None):
        self.ins, self.out_shapes, self.n_sems, self.start, self.mid, self.finish = list(ins), list(out_shapes), n_sems, start, mid, finish

    def standalone(self, name):
        ni, no = len(self.ins), len(self.out_shapes)

        def body(*refs):
            parts = (refs[:ni], refs[ni:ni + no], *refs[ni + no:])
            self.start(*parts)
            if self.mid is not None:
                self.mid(*parts)
            self.finish(*parts)

        return _comm_call(body, name, self.ins, self.out_shapes, self.n_sems)


def host_call(body, *, name, grid, in_specs, out_specs, out_shape, scratch_shapes, operands, rider=None):
    n_in, n_out = len(in_specs), len(out_specs)
    if rider is None:
        outs = pl.pallas_call(body, name=name, grid=grid, in_specs=in_specs, out_specs=out_specs, out_shape=out_shape,
                              scratch_shapes=scratch_shapes, compiler_params=_params(("arbitrary",) * len(grid)))(*operands)
        return list(outs), []
    ri, ro = len(rider.ins), len(rider.out_shapes)

    def hosted(*refs):
        ins, r_ins = refs[:n_in], refs[n_in:n_in + ri]
        outs, r_outs = refs[n_in + ri:n_in + ri + n_out], refs[n_in + ri + n_out:n_in + ri + n_out + ro]
        scratch, (send_sems, recv_sems) = refs[n_in + ri + n_out + ro:-2], refs[-2:]
        step, total = 0, 1
        for ax, n in enumerate(grid):
            step = step * n + pl.program_id(ax)
            total *= n
        parts = (r_ins, r_outs, send_sems, recv_sems)

        @pl.when(step == 0)
        def _():
            rider.start(*parts)

        if rider.mid is not None:
            @pl.when(step == min(total - 1, int(total * RIDER_MID)))
            def _():
                rider.mid(*parts)

        body(*ins, *outs, *scratch)

        @pl.when(step == total - 1)
        def _():
            rider.finish(*parts)

    outs = pl.pallas_call(
        hosted,
        name=name,
        grid=grid,
        in_specs=list(in_specs) + [ANY] * ri,
        out_specs=list(out_specs) + [ANY] * ro,
        out_shape=list(out_shape) + rider.out_shapes,
        scratch_shapes=list(scratch_shapes) + [pltpu.SemaphoreType.DMA((rider.n_sems,)), pltpu.SemaphoreType.DMA((rider.n_sems,))],
        compiler_params=_params(("arbitrary",) * len(grid)),
    )(*operands, *rider.ins)
    return list(outs[:n_out]), list(outs[n_out:])


def mmx(name, a, b, *, grid, a_spec, b_spec, o_spec, o_shape, o_dtype, ca, cb, acc_shape=None, scale=None, rider=None):
    nk = grid[-1] if acc_shape is not None else 1
    assert scale is None or nk == 1

    def body(a_ref, b_ref, o_ref, *acc):
        p = _dot(a_ref[...].astype(BF), b_ref[...].astype(BF), ca, cb)
        if scale is not None:
            p = p * scale
        if nk == 1:
            o_ref[...] = p.astype(o_ref.dtype)
        else:
            kk = pl.program_id(len(grid) - 1)
            _accumulate(acc[0], p, kk == 0)

            @pl.when(kk == nk - 1)
            def _():
                o_ref[...] = acc[0][...].astype(o_ref.dtype)

    if rider is not None:
        (out,), r_outs = host_call(body, name=name, grid=grid, in_specs=[a_spec, b_spec], out_specs=[o_spec], out_shape=[jax.ShapeDtypeStruct(o_shape, o_dtype)],
                                   scratch_shapes=[pltpu.VMEM(acc_shape, F32)] if nk > 1 else [], operands=(a, b), rider=rider)
        return out, r_outs
    sem = ("parallel",) * (len(grid) - 1) + ("arbitrary" if nk > 1 else "parallel",)
    return pl.pallas_call(
        body,
        name=name,
        grid=grid,
        in_specs=[a_spec, b_spec],
        out_specs=o_spec,
        out_shape=jax.ShapeDtypeStruct(o_shape, o_dtype),
        scratch_shapes=[pltpu.VMEM(acc_shape, F32)] if nk > 1 else [],
        compiler_params=_params(sem),
    )(a, b)


def mm(a, b, *, ta=False, tb=False, out_dtype=F32, name, tm_cap=1024, tn_cap=1024, tk_cap=4096, rider=None):
    m, k = (a.shape[1], a.shape[0]) if ta else a.shape
    n = b.shape[0] if tb else b.shape[1]
    assert (b.shape[1] if tb else b.shape[0]) == k
    tm, tn, tk = _tile(m, tm_cap), _tile(n, tn_cap), _tile(k, tk_cap)
    nk = k // tk
    a_spec = pl.BlockSpec((tk, tm), lambda i, j, kk: (kk, i)) if ta else pl.BlockSpec((tm, tk), lambda i, j, kk: (i, kk))
    b_spec = pl.BlockSpec((tn, tk), lambda i, j, kk: (j, kk)) if tb else pl.BlockSpec((tk, tn), lambda i, j, kk: (kk, j))
    return mmx(name, a, b, grid=(m // tm, n // tn, nk), a_spec=a_spec, b_spec=b_spec, o_spec=pl.BlockSpec((tm, tn), lambda i, j, kk: (i, j)),
               o_shape=(m, n), o_dtype=out_dtype, ca=0 if ta else 1, cb=1 if tb else 0, acc_shape=(tm, tn) if nk > 1 else None, rider=rider)


def ew(fn, rows, vecs, out_rows, out_red=(), *, tm, name, prev_halo=(), next_halo=()):
    t = rows[0].shape[0]
    tm = min(tm, t)
    nt = t // tm
    assert t % tm == 0 and (tm % HALO == 0 or (tm == t and not prev_halo and not next_halo))
    nr, nv, npv, nnx, nor = len(rows), len(vecs), len(prev_halo), len(next_halo), len(out_rows)
    hb = tm // HALO

    def body(*refs):
        i = pl.program_id(0)
        ins = [r[...].astype(F32) for r in refs[: nr + nv + npv + nnx]]
        outs = refs[nr + nv + npv + nnx:]
        o_rows, o_red = fn(i, nt, ins[:nr], ins[nr:nr + nv], ins[nr + nv:nr + nv + npv], ins[nr + nv + npv:])
        for ref, val in zip(outs[:nor], o_rows, strict=True):
            ref[...] = val.astype(ref.dtype)
        for ref, val in zip(outs[nor:], o_red, strict=True):
            _accumulate(ref, val, i == 0)

    in_specs = [pl.BlockSpec((tm, r.shape[1]), lambda i: (i, 0)) for r in rows]
    in_specs += [pl.BlockSpec(v.shape, lambda i: (0, 0)) for v in vecs]
    in_specs += [pl.BlockSpec((HALO, rows[k].shape[1]), lambda i: (jnp.maximum(i * hb - 1, 0), 0)) for k in prev_halo]
    in_specs += [pl.BlockSpec((HALO, rows[k].shape[1]), lambda i: (jnp.minimum((i + 1) * hb, t // HALO - 1), 0)) for k in next_halo]
    out_specs = [pl.BlockSpec((tm, c), lambda i: (i, 0)) for c, _ in out_rows]
    out_specs += [pl.BlockSpec(s, lambda i: (0, 0)) for s in out_red]
    out_shape = [jax.ShapeDtypeStruct((t, c), d) for c, d in out_rows] + [jax.ShapeDtypeStruct(s, F32) for s in out_red]
    return pl.pallas_call(
        body,
        name=name,
        grid=(nt,),
        in_specs=in_specs,
        out_specs=out_specs,
        out_shape=out_shape,
        compiler_params=_params(("arbitrary",) if out_red else ("parallel",)),
    )(*rows, *vecs, *[rows[k] for k in prev_halo], *[rows[k] for k in next_halo])


def _shift_down(x, prev, j):
    if j == 0:
        return x
    r = pltpu.roll(x, j, 0)
    rh = pltpu.roll(prev, j, 0)
    row = lax.broadcasted_iota(jnp.int32, (HALO, x.shape[1]), 0)
    head = jnp.where(row < j, rh, r[:HALO])
    return jnp.concatenate([head, r[HALO:]], axis=0)


def _shift_up(x, nxt, j):
    if j == 0:
        return x
    n = x.shape[0]
    r = pltpu.roll(x, n - j, 0)
    rh = pltpu.roll(nxt, HALO - j, 0)
    row = lax.broadcasted_iota(jnp.int32, (HALO, x.shape[1]), 0)
    tail = jnp.where(row >= HALO - j, rh, r[n - HALO:])
    return jnp.concatenate([r[: n - HALO], tail], axis=0)


def _conv_fwd(x, prev, w):
    kk = w.shape[0]
    acc = None
    for k in range(kk):
        term = w[k:k + 1, :] * _shift_down(x, prev, kk - 1 - k)
        acc = term if acc is None else acc + term
    return acc


def ffn_up(h, g, wf, rider=None):
    t, d = h.shape
    f4 = wf.shape[1] // 2
    tm = _tile(t, 1024)
    sub = _tile(tm, FFN_SUB, 16)

    def body(h_ref, g_ref, wg_ref, wu_ref, ab_ref, s_ref, n_ref):
        @pl.when(pl.program_id(1) == 0)
        def _():
            n_ref[...] = _rms(h_ref[...], g_ref[...]).astype(BF)

        for r in range(tm // sub):
            rows = slice(r * sub, (r + 1) * sub)
            n = n_ref[rows, :]
            a = _dot(n, wg_ref[...], 1, 1)
            b = _dot(n, wu_ref[...], 1, 1)
            ab_ref[0, rows, :] = a.astype(BF)
            ab_ref[1, rows, :] = b.astype(BF)
            s_ref[rows, :] = (a * _sigmoid(a) * b).astype(BF)

    wspec = lambda ib: pl.BlockSpec((None, f4, d), lambda i, j: (j, ib, 0))
    return host_call(
        body,
        name="ffn_up",
        grid=(t // tm, NCHIP),
        in_specs=[pl.BlockSpec((tm, d), lambda i, j: (i, 0)), pl.BlockSpec((1, d), lambda i, j: (0, 0)), wspec(0), wspec(1)],
        out_specs=[pl.BlockSpec((2, None, tm, f4), lambda i, j: (0, j, i, 0)), pl.BlockSpec((None, tm, f4), lambda i, j: (j, i, 0)),
                   pl.BlockSpec((tm, d), lambda i, j: (i, 0))],
        out_shape=[jax.ShapeDtypeStruct((2, NCHIP, t, f4), BF), jax.ShapeDtypeStruct((NCHIP, t, f4), BF), jax.ShapeDtypeStruct((t, d), BF)],
        scratch_shapes=[],
        operands=(h, g, wf, wf),
        rider=rider,
    )


def ffn_down(s4, wf, h, rider=None):
    t, d = h.shape
    f4 = s4.shape[2]
    tm = _tile(t, 512)

    def body(s_ref, w_ref, h_ref, o_ref):
        acc = _dot(s_ref[0], w_ref[0])
        for k in range(1, NCHIP):
            acc = acc + _dot(s_ref[k], w_ref[k])
        o_ref[...] = h_ref[...] + 0.5 * acc

    (out,), r_outs = host_call(
        body,
        name="ffn_down",
        grid=(t // tm,),
        in_specs=[pl.BlockSpec((NCHIP, tm, f4), lambda i: (0, i, 0)), pl.BlockSpec((NCHIP, f4, d), lambda i: (0, 0, 0)), pl.BlockSpec((tm, d), lambda i: (i, 0))],
        out_specs=[pl.BlockSpec((tm, d), lambda i: (i, 0))],
        out_shape=[jax.ShapeDtypeStruct((t, d), F32)],
        scratch_shapes=[],
        operands=(s4, wf, h),
        rider=rider,
    )
    return out, r_outs


def ffn_bwd(dho, ab, wf, wd):
    t, d = dho.shape
    f4 = wd.shape[1]
    tm = _tile(t, 1024)
    sub = _tile(tm, FFN_SUB, 16)

    def body(dho_ref, ab_ref, wg_ref, wu_ref, wd_ref, dn_ref, s_ref, dab_ref, do_sc):
        j = pl.program_id(1)

        @pl.when(j == 0)
        def _():
            do_sc[...] = (0.5 * dho_ref[...]).astype(BF)
            dn_ref[...] = jnp.zeros_like(dn_ref)

        for r in range(tm // sub):
            rows = slice(r * sub, (r + 1) * sub)
            ds = _dot(do_sc[rows, :], wd_ref[...], 1, 1)
            av, bv = ab_ref[0, rows, :].astype(F32), ab_ref[1, rows, :].astype(F32)
            sig = _sigmoid(av)
            sl = av * sig
            s_ref[rows, :] = (sl * bv).astype(BF)
            da = (ds * bv * (sig * (1.0 + av * (1.0 - sig)))).astype(BF)
            db = (ds * sl).astype(BF)
            dab_ref[0, rows, :] = da
            dab_ref[1, rows, :] = db
            dn_ref[rows, :] += _dot(da, wg_ref[...]) + _dot(db, wu_ref[...])

    row = lambda c: pl.BlockSpec((tm, c), lambda i, j: (i, 0))
    wspec = lambda ib: pl.BlockSpec((None, f4, d), lambda i, j: (j, ib, 0))
    ab_spec = pl.BlockSpec((2, None, tm, f4), lambda i, j: (0, j, i, 0))
    return pl.pallas_call(
        body,
        name="ffn_bwd",
        grid=(t // tm, NCHIP),
        in_specs=[row(d), ab_spec, wspec(0), wspec(1), wspec(0)],
        out_specs=[row(d), pl.BlockSpec((None, tm, f4), lambda i, j: (j, i, 0)), ab_spec],
        out_shape=[jax.ShapeDtypeStruct((t, d), F32), jax.ShapeDtypeStruct((NCHIP, t, f4), BF), jax.ShapeDtypeStruct((2, NCHIP, t, f4), BF)],
        scratch_shapes=[pltpu.VMEM((tm, d), BF)],
        compiler_params=_params(("parallel", "arbitrary")),
    )(dho, ab, wf, wf, wd)


def ffn_wgrads(n, dho, s4, dab):
    t, d = n.shape
    f4 = s4.shape[2]
    tn = _tile(d, 512)
    g_in = mmx("g_ffn_in", dab, n, grid=(2, NCHIP, d // tn), a_spec=pl.BlockSpec((None, None, t, f4), lambda wh, k, j: (wh, k, 0, 0)),
               b_spec=pl.BlockSpec((t, tn), lambda wh, k, j: (0, j)), o_spec=pl.BlockSpec((None, None, f4, tn), lambda wh, k, j: (k, wh, 0, j)),
               o_shape=(NCHIP, 2, f4, d), o_dtype=BF, ca=0, cb=0)
    g_out = mmx("g_ffn_out", s4, dho, grid=(NCHIP, d // tn), a_spec=pl.BlockSpec((None, t, f4), lambda k, j: (k, 0, 0)),
                b_spec=pl.BlockSpec((t, tn), lambda k, j: (0, j)), o_spec=pl.BlockSpec((None, f4, tn), lambda k, j: (k, 0, j)),
                o_shape=(NCHIP, f4, d), o_dtype=BF, ca=0, cb=0, scale=0.5)
    return jnp.concatenate([g_in.reshape(NCHIP, 2 * f4, d), g_out], axis=1)


def norm_cast(h, g):
    def fn(i, nt, rows, vecs, prevs, nexts):
        return [_rms(rows[0], vecs[0])], []
    return ew(fn, [h], [g], [(h.shape[1], BF)], tm=512, name="norm_cast")[0]


def _zero_if(cond, x):
    return jnp.where(cond, jnp.zeros_like(x), x)


def conv_a_fwd(sc3, w_sc):
    d = sc3.shape[1] // 3

    def fn(i, nt, rows, vecs, prevs, nexts):
        x, pv = rows[0], _zero_if(i == 0, prevs[0])
        v = x[:, d:2 * d] * x[:, 2 * d:]
        vp = pv[:, d:2 * d] * pv[:, 2 * d:]
        return [x[:, :d] * _conv_fwd(v, vp, vecs[0])], []

    return ew(fn, [sc3], [w_sc], [(d, BF)], tm=256, name="conv_a_fwd", prev_halo=(0,))[0]


def _softplus(x):
    e = jnp.exp(-jnp.abs(x))
    return jnp.maximum(x, 0.0) + jnp.where(e < 1e-4, e - 0.5 * e * e, jnp.log(1.0 + e))


def conv_m_fwd(xbc_raw, dt_raw, w_mc, b_mc, dt_bias):
    def fn(i, nt, rows, vecs, prevs, nexts):
        pre = _conv_fwd(rows[0], _zero_if(i == 0, prevs[0]), vecs[0]) + vecs[1]
        return [pre * _sigmoid(pre), _softplus(rows[1] + vecs[2])], []

    return ew(fn, [xbc_raw, dt_raw], [w_mc, b_mc, dt_bias], [(xbc_raw.shape[1], F32), (LANE, F32)], tm=256, name="conv_m_fwd",
              prev_halo=(0,))


def conv_m_bwd1(dxbc, xbc_raw, ddt, dt_raw, w_mc, b_mc, dt_bias):
    def fn(i, nt, rows, vecs, prevs, nexts):
        pre = _conv_fwd(rows[1], _zero_if(i == 0, prevs[0]), vecs[0]) + vecs[1]
        sig = _sigmoid(pre)
        dpre = rows[0] * (sig * (1.0 + pre * (1.0 - sig)))
        ddr = rows[2] * _sigmoid(rows[3] + vecs[2])
        return [dpre, ddr], [jnp.sum(ddr, axis=0, keepdims=True)]

    return ew(fn, [dxbc, xbc_raw, ddt, dt_raw], [w_mc, b_mc, dt_bias], [(dxbc.shape[1], F32), (LANE, BF)], [(1, LANE)], tm=256,
              name="conv_m_bwd1", prev_halo=(1,))


def conv_bwd2(dpre, x, w, name):
    kk = w.shape[0]

    def fn(i, nt, rows, vecs, prevs, nexts):
        dp, xv = rows[0], rows[1]
        nx = _zero_if(i == nt - 1, nexts[0])
        dx = None
        dws = []
        for k in range(kk):
            up = _shift_up(dp, nx, kk - 1 - k)
            term = vecs[0][k:k + 1, :] * up
            dx = term if dx is None else dx + term
            dws.append(jnp.sum(up * xv, axis=0, keepdims=True))
        return [dx], [jnp.concatenate(dws, axis=0), jnp.sum(dp, axis=0, keepdims=True)]

    c = x.shape[1]
    return ew(fn, [dpre, x], [w], [(c, BF)], [(kk, c), (1, c)], tm=256, name=name, next_halo=(0,))


def conv_a_bwd1(dya, sc3, w_sc):
    d = sc3.shape[1] // 3

    def fn(i, nt, rows, vecs, prevs, nexts):
        x, pv = rows[1], _zero_if(i == 0, prevs[0])
        v = x[:, d:2 * d] * x[:, 2 * d:]
        vp = pv[:, d:2 * d] * pv[:, 2 * d:]
        return [rows[0] * x[:, :d], rows[0] * _conv_fwd(v, vp, vecs[0]), v], []

    return ew(fn, [dya, sc3], [w_sc], [(d, F32), (d, BF), (d, F32)], tm=256, name="conv_a_bwd1", prev_halo=(1,))


def conv_a_bwd2(dcv, v, sc3, w_sc):
    d = v.shape[1]
    kk = w_sc.shape[0]

    def fn(i, nt, rows, vecs, prevs, nexts):
        dp, vv, x = rows
        nx = _zero_if(i == nt - 1, nexts[0])
        dv = None
        dws = []
        for k in range(kk):
            up = _shift_up(dp, nx, kk - 1 - k)
            term = vecs[0][k:k + 1, :] * up
            dv = term if dv is None else dv + term
            dws.append(jnp.sum(up * vv, axis=0, keepdims=True))
        return [dv * x[:, 2 * d:], dv * x[:, d:2 * d]], [jnp.concatenate(dws, axis=0)]

    return ew(fn, [dcv, v, sc3], [w_sc], [(d, BF), (d, BF)], [(kk, d)], tm=256, name="conv_a_bwd2", next_halo=(0,))


def _xdot(a, b, passes, split_lhs, ca=1, cb=0):
    parts, r = [], (a if split_lhs else b)
    for _ in range(passes):
        piece = r.astype(BF)
        parts.append(piece)
        r = r - piece.astype(F32)
    other = (b if split_lhs else a).astype(BF)
    acc = None
    for piece in parts:
        term = _dot(piece, other, ca, cb) if split_lhs else _dot(other, piece, ca, cb)
        acc = term if acc is None else acc + term
    return acc


def _ssd_common(xbc_ref, dt_ref, alog_ref, e_ref, w):
    ll = SSM_L
    xs = xbc_ref[:, 0:w]
    dtv = dt_ref[...]
    a_row = -jnp.exp(alog_ref[...])
    a = dtv * a_row
    row = lax.broadcasted_iota(jnp.int32, (ll, ll), 0)
    col = lax.broadcasted_iota(jnp.int32, (ll, ll), 1)
    tril = (row >= col).astype(F32)
    triu = (row <= col).astype(F32)
    acl = _xdot(tril, a, 3, False)
    acl_t = _xdot(a, triu, 3, True, 0, 0)
    e = e_ref[...]
    aclx = _xdot(acl, e, 3, True)
    dtx = _xdot(dtv, e, 2, True)
    last = aclx[ll - 1:ll, :]
    e_in = jnp.exp(aclx)
    e_end = jnp.exp(last - aclx)
    e_tot = jnp.exp(last)
    x = xs * dtx
    return dict(xs=xs, dtv=dtv, a_row=a_row, a=a, row=row, col=col, triu=triu, acl=acl, acl_t=acl_t, dtx=dtx, e_in=e_in, e_end=e_end,
                e_tot=e_tot, x=x)


def _decay(q, hh):
    diff = q["acl"][:, hh:hh + 1] - q["acl_t"][hh:hh + 1, :]
    return jnp.exp(jnp.where(q["row"] >= q["col"], diff, -jnp.inf))


def ssd_fwd(xbc, dt, z, a_log, d_exp, m_norm, e_mat, rider=None):
    t = xbc.shape[0]
    w = z.shape[1]
    gn = SSM_G * SSM_N
    gw = w // SSM_G
    ll, nn = SSM_L, SSM_N
    nc = t // ll
    cw = xbc.shape[1]

    def body(xbc_ref, dt_ref, z_ref, alog_ref, dexp_ref, mn_ref, e_ref, yn_ref, y_ref, sp_ref, s_sc):
        c = pl.program_id(0)

        @pl.when(c == 0)
        def _():
            s_sc[...] = jnp.zeros_like(s_sc)

        q = _ssd_common(xbc_ref, dt_ref, alog_ref, e_ref, w)
        xb = q["x"].astype(BF)
        xsb = (q["x"] * q["e_end"]).astype(BF)
        sp = s_sc[...]
        sp_ref[0] = sp
        spb = sp.astype(BF)
        lane = lax.broadcasted_iota(jnp.int32, (ll, LANE), 1)
        for g in range(SSM_G):
            lo = g * gw
            bg = xbc_ref[:, w + g * nn:w + (g + 1) * nn].astype(BF)
            cg = xbc_ref[:, w + gn + g * nn:w + gn + (g + 1) * nn].astype(BF)
            yoff = _dot(cg, spb[:, lo:lo + gw]) * q["e_in"][:, lo:lo + gw]
            s_sc[:, lo:lo + gw] = sp[:, lo:lo + gw] * q["e_tot"][:, lo:lo + gw] + _dot(bg, xsb[:, lo:lo + gw], 0, 0)
            cb = _dot(cg, bg, 1, 1)
            for pr in range(gw // LANE):
                l0 = lo + pr * LANE
                xp = xb[:, l0:l0 + LANE]
                ys = []
                for hh in (l0 // SSM_P, l0 // SSM_P + 1):
                    wm = (cb * _decay(q, hh)).astype(BF)
                    ys.append(_dot(wm, xp))
                ydiag = jnp.where(lane < SSM_P, ys[0], ys[1])
                y_ref[:, l0:l0 + LANE] = ydiag + yoff[:, pr * LANE:(pr + 1) * LANE] + dexp_ref[:, l0:l0 + LANE] * q["xs"][:, l0:l0 + LANE]
        zv = z_ref[...].astype(F32)
        yz = y_ref[...] * (zv * _sigmoid(zv))
        for g in range(SSM_G):
            lo = g * gw
            yn_ref[:, lo:lo + gw] = _rms(yz[:, lo:lo + gw], mn_ref[:, lo:lo + gw]).astype(BF)

    vec = lambda s: pl.BlockSpec(s, lambda c: (0, 0))
    return host_call(
        body,
        name="ssd_fwd",
        grid=(nc,),
        in_specs=[
            pl.BlockSpec((ll, cw), lambda c: (c, 0)), pl.BlockSpec((ll, LANE), lambda c: (c, 0)), pl.BlockSpec((ll, w), lambda c: (c, 0)),
            vec((1, LANE)), vec((1, w)), vec((1, w)), vec((LANE, w)),
        ],
        out_specs=[pl.BlockSpec((ll, w), lambda c: (c, 0)), pl.BlockSpec((ll, w), lambda c: (c, 0)), pl.BlockSpec((1, nn, w), lambda c: (c, 0, 0))],
        out_shape=[jax.ShapeDtypeStruct((t, w), BF), jax.ShapeDtypeStruct((t, w), F32), jax.ShapeDtypeStruct((nc, nn, w), F32)],
        scratch_shapes=[pltpu.VMEM((nn, w), F32)],
        operands=(xbc, dt, z, a_log, d_exp, m_norm, e_mat),
        rider=rider,
    )


def ssd_bwd(dyn, y, z, xbc, dt, sprev, a_log, d_exp, m_norm, e_mat, et_mat, rider=None):
    t = xbc.shape[0]
    w = z.shape[1]
    gn = SSM_G * SSM_N
    gw = w // SSM_G
    ll, nn = SSM_L, SSM_N
    nc = t // ll
    cw = xbc.shape[1]

    def body(dyn_ref, y_ref, z_ref, xbc_ref, dt_ref, sp_ref, alog_ref, dexp_ref, mn_ref, e_ref, et_ref,
             dz_ref, dxbc_ref, ddt_ref, dmn_ref, dd_ref, dal_ref, ds_sc, dy_sc, dx_sc):
        step = pl.program_id(0)

        @pl.when(step == 0)
        def _():
            ds_sc[...] = jnp.zeros_like(ds_sc)

        zv, yv = z_ref[...].astype(F32), y_ref[...]
        sg = _sigmoid(zv)
        sz = zv * sg
        yz = yv * sz
        dmn = []
        for g in range(SSM_G):
            lo = g * gw
            dseg, dmn_g = _rms_bwd(yz[:, lo:lo + gw], mn_ref[:, lo:lo + gw], dyn_ref[:, lo:lo + gw])
            dy_sc[:, lo:lo + gw] = dseg
            dmn.append(dmn_g)
        dmn = jnp.concatenate(dmn, axis=1)
        dyz = dy_sc[...]
        dz_ref[...] = (dyz * yv * (sg * (1.0 + zv * (1.0 - sg)))).astype(BF)
        dy = dyz * sz

        q = _ssd_common(xbc_ref, dt_ref, alog_ref, e_ref, w)
        x = q["x"]
        xb = x.astype(BF)
        xsb = (x * q["e_end"]).astype(BF)
        sp = sp_ref[0]
        spb = sp.astype(BF)
        dsn = ds_sc[...]
        dsnb = dsn.astype(BF)
        dyb = dy.astype(BF)
        lane = lax.broadcasted_iota(jnp.int32, (ll, LANE), 1)
        lane1 = lax.broadcasted_iota(jnp.int32, (1, LANE), 1)
        sub1 = lax.broadcasted_iota(jnp.int32, (LANE, 1), 0)
        dacl = jnp.zeros((ll, LANE), F32)
        dacl_t = jnp.zeros((LANE, ll), F32)
        d_ein, d_eend, d_etot = [], [], []
        for g in range(SSM_G):
            lo = g * gw
            sl = slice(lo, lo + gw)
            bg = xbc_ref[:, w + g * nn:w + (g + 1) * nn].astype(BF)
            cg = xbc_ref[:, w + gn + g * nn:w + gn + (g + 1) * nn].astype(BF)
            zg = _dot(cg, spb[:, sl])
            dzz = (dy[:, sl] * q["e_in"][:, sl]).astype(BF)
            d_ein.append(dy[:, sl] * zg)
            dcg = _dot(dzz, spb[:, sl], 1, 1)
            ds_sc[:, sl] = _dot(cg, dzz, 0, 0) + dsn[:, sl] * q["e_tot"][:, sl]
            d_etot.append(jnp.sum(dsn[:, sl] * sp[:, sl], axis=0, keepdims=True))
            dbg = _dot(xsb[:, sl], dsnb[:, sl], 1, 1)
            dxs_g = _dot(bg, dsnb[:, sl])
            d_eend.append(dxs_g * x[:, sl])
            cb = _dot(cg, bg, 1, 1)
            dcb = jnp.zeros((ll, ll), F32)
            for pr in range(gw // LANE):
                l0 = lo + pr * LANE
                xp = xb[:, l0:l0 + LANE]
                dyp = dyb[:, l0:l0 + LANE]
                dxp = []
                for hi, hh in enumerate((l0 // SSM_P, l0 // SSM_P + 1)):
                    lm = _decay(q, hh)
                    wm = (cb * lm).astype(BF)
                    in_head = (lane < SSM_P) if hi == 0 else (lane >= SSM_P)
                    dwm = _dot(jnp.where(in_head, dyp, jnp.zeros_like(dyp)), xp, 1, 1)
                    dxp.append(_dot(wm, dyp, 0, 0))
                    dlm = dwm * lm
                    dcb = dcb + dlm
                    dd = dlm * cb
                    dacl = dacl + jnp.sum(dd, axis=1, keepdims=True) * (lane1 == hh).astype(F32)
                    dacl_t = dacl_t + (sub1 == hh).astype(F32) * jnp.sum(dd, axis=0, keepdims=True)
                dx_sc[:, l0:l0 + LANE] = jnp.where(lane < SSM_P, dxp[0], dxp[1]) + dxs_g[:, pr * LANE:(pr + 1) * LANE] * q["e_end"][:, l0:l0 + LANE]
            dcbb = dcb.astype(BF)
            dxbc_ref[:, w + g * nn:w + (g + 1) * nn] = dbg + _dot(dcbb, cg, 0, 0)
            dxbc_ref[:, w + gn + g * nn:w + gn + (g + 1) * nn] = dcg + _dot(dcbb, bg)
        d_ein = jnp.concatenate(d_ein, axis=1) * q["e_in"]
        d_eend = jnp.concatenate(d_eend, axis=1) * q["e_end"]
        d_etot = jnp.concatenate(d_etot, axis=1) * q["e_tot"]
        et = et_ref[...]
        last_add = jnp.sum(d_eend, axis=0, keepdims=True) + d_etot
        last_add = _xdot(jnp.broadcast_to(last_add, (HALO, w)), et, 2, True)[0:1]
        row1 = lax.broadcasted_iota(jnp.int32, (ll, LANE), 0)
        dacl = dacl + _xdot(d_ein - d_eend, et, 2, True) + jnp.where(row1 == ll - 1, last_add, 0.0)
        da = _xdot(q["triu"], dacl, 2, False) - _xdot(q["triu"], dacl_t, 2, False, 1, 1)
        dxv = dx_sc[...]
        dxbc_ref[:, 0:w] = dexp_ref[...] * dy + dxv * q["dtx"]
        ddt_ref[...] = _xdot(dxv * q["xs"], et, 2, True) + da * q["a_row"]
        dal = jnp.sum(da * q["dtv"], axis=0, keepdims=True) * q["a_row"]
        ddv = jnp.sum(dy * q["xs"], axis=0, keepdims=True)
        ddv = _xdot(jnp.broadcast_to(ddv, (HALO, w)), et, 2, True)[0:1]
        _accumulate(dmn_ref, dmn, step == 0)
        _accumulate(dd_ref, ddv, step == 0)
        _accumulate(dal_ref, dal, step == 0)

    rev = lambda c_: pl.BlockSpec((ll, c_), lambda s: (nc - 1 - s, 0))
    vec = lambda s_: pl.BlockSpec(s_, lambda s: (0, 0))
    return host_call(
        body,
        name="ssd_bwd",
        grid=(nc,),
        in_specs=[
            rev(w), rev(w), rev(w), rev(cw), rev(LANE), pl.BlockSpec((1, nn, w), lambda s: (nc - 1 - s, 0, 0)),
            vec((1, LANE)), vec((1, w)), vec((1, w)), vec((LANE, w)), vec((w, LANE)),
        ],
        out_specs=[rev(w), rev(cw), rev(LANE), vec((1, w)), vec((1, LANE)), vec((1, LANE))],
        out_shape=[
            jax.ShapeDtypeStruct((t, w), BF), jax.ShapeDtypeStruct((t, cw), F32), jax.ShapeDtypeStruct((t, LANE), F32),
            jax.ShapeDtypeStruct((1, w), F32), jax.ShapeDtypeStruct((1, LANE), F32), jax.ShapeDtypeStruct((1, LANE), F32),
        ],
        scratch_shapes=[pltpu.VMEM((nn, w), F32), pltpu.VMEM((ll, w), F32), pltpu.VMEM((ll, w), F32)],
        operands=(dyn, y, z, xbc, dt, sprev, a_log, d_exp, m_norm, e_mat, et_mat),
        rider=rider,
    )


def _w1024_spec(d, nblk, iblk):
    r = nblk * (d // NCHIP)
    return pl.BlockSpec((NCHIP, r, d), lambda i: (0, iblk // nblk, 0))


def _whole(ref):
    v = ref[...]
    return v.reshape(v.shape[0] * v.shape[1], v.shape[2])


def mix_out_fwd(ya_in, yn, gates, h, w1024, rider=None):
    t, d = h.shape
    tm = _tile(t, 256)

    def body(ya_ref, yn_ref, g_ref, h_ref, wm_ref, wa_ref, wo_ref, ho_ref, oa_ref, om_ref, mg_ref):
        y_a = _dot(ya_ref[...], _whole(wa_ref))
        y_m = _dot(yn_ref[...], _whole(wm_ref))
        oa_ref[...] = y_a
        om_ref[...] = y_m
        gv = g_ref[...].astype(F32)
        mg = (_sigmoid(gv[:, :d]) * y_a + _sigmoid(gv[:, d:]) * y_m).astype(BF)
        mg_ref[...] = mg
        ho_ref[...] = h_ref[...] + _dot(mg, _whole(wo_ref))

    row = lambda c: pl.BlockSpec((tm, c), lambda i: (i, 0))
    return host_call(
        body,
        name="mix_out_fwd",
        grid=(t // tm,),
        in_specs=[row(d), row(2 * d), row(2 * d), row(d), _w1024_spec(d, 2, 0), _w1024_spec(d, 1, 2), _w1024_spec(d, 1, 3)],
        out_specs=[row(d), row(d), row(d), row(d)],
        out_shape=[jax.ShapeDtypeStruct((t, d), F32), jax.ShapeDtypeStruct((t, d), F32), jax.ShapeDtypeStruct((t, d), F32),
                   jax.ShapeDtypeStruct((t, d), BF)],
        scratch_shapes=[],
        operands=(ya_in, yn, gates, h, w1024, w1024, w1024),
        rider=rider,
    )


def mix_out_bwd(dh, gates, y_a, y_m, w1024):
    t, d = dh.shape
    tm = _tile(t, 256)

    def body(dh_ref, g_ref, ya_ref, ym_ref, wm_ref, wa_ref, wo_ref, dg_ref, dya_ref, dyn_ref, da_ref, dm_ref):
        dmg = _dot(dh_ref[...].astype(BF), _whole(wo_ref), 1, 1)
        gv = g_ref[...].astype(F32)
        sa, sm = _sigmoid(gv[:, :d]), _sigmoid(gv[:, d:])
        dg_ref[:, :d] = (dmg * ya_ref[...] * sa * (1.0 - sa)).astype(BF)
        dg_ref[:, d:] = (dmg * ym_ref[...] * sm * (1.0 - sm)).astype(BF)
        da = (dmg * sa).astype(BF)
        dm = (dmg * sm).astype(BF)
        da_ref[...] = da
        dm_ref[...] = dm
        dya_ref[...] = _dot(da, _whole(wa_ref), 1, 1)
        dyn_ref[...] = _dot(dm, _whole(wm_ref), 1, 1)

    row = lambda c: pl.BlockSpec((tm, c), lambda i: (i, 0))
    return pl.pallas_call(
        body,
        name="mix_out_bwd",
        grid=(t // tm,),
        in_specs=[row(d), row(2 * d), row(d), row(d), _w1024_spec(d, 2, 0), _w1024_spec(d, 1, 2), _w1024_spec(d, 1, 3)],
        out_specs=[row(2 * d), row(d), row(2 * d), row(d), row(d)],
        out_shape=[jax.ShapeDtypeStruct((t, 2 * d), BF), jax.ShapeDtypeStruct((t, d), F32), jax.ShapeDtypeStruct((t, 2 * d), F32),
                   jax.ShapeDtypeStruct((t, d), BF), jax.ShapeDtypeStruct((t, d), BF)],
        compiler_params=_params(("parallel",)),
    )(dh, gates, y_a, y_m, w1024, w1024, w1024)


def norm_bwd_add(dh, h, g, dn):
    def fn(i, nt, rows, vecs, prevs, nexts):
        dx, dg = _rms_bwd(rows[1], vecs[0], rows[2])
        return [rows[0] + dx], [dg]
    d = h.shape[1]
    return ew(fn, [dh, h, dn], [g], [(d, F32)], [(1, d)], tm=512, name="norm_bwd_add")


def _pe(p, wpp_ref):
    pb = p.astype(BF)
    return jnp.concatenate([_dot(pb, wpp_ref[k]) for k in range(NCHIP)], axis=1)


def ple_fwd(h, g, p, w1024, wpp):
    t, d = h.shape
    tm = _tile(t, 512)

    def body(h_ref, g_ref, p_ref, wg_ref, wp_ref, ho_ref):
        hv = h_ref[...]
        gate = _sigmoid(_dot(_rms(hv, g_ref[...]).astype(BF), _whole(wg_ref)))
        ho_ref[...] = hv + gate * _pe(p_ref[...], wp_ref)

    row = lambda c: pl.BlockSpec((tm, c), lambda i: (i, 0))
    wpp_spec = pl.BlockSpec(wpp.shape, lambda i: (0, 0, 0))
    return pl.pallas_call(
        body,
        name="ple_fwd",
        grid=(t // tm,),
        in_specs=[row(d), pl.BlockSpec((1, d), lambda i: (0, 0)), row(p.shape[1]), _w1024_spec(d, 1, 4), wpp_spec],
        out_specs=row(d),
        out_shape=jax.ShapeDtypeStruct((t, d), F32),
        compiler_params=_params(("parallel",)),
    )(h, g, p, w1024, wpp)


def ple_bwd(dho, h, g, p, w1024, wpp):
    t, d = h.shape
    tm = _tile(t, 512)

    def body(dho_ref, h_ref, g_ref, p_ref, wg_ref, wp_ref, dh_ref, dg_ref, n_ref, dgp_ref, dpe_ref):
        hv, dv = h_ref[...], dho_ref[...]
        n = _rms(hv, g_ref[...]).astype(BF)
        n_ref[...] = n
        wg = _whole(wg_ref)
        gate = _sigmoid(_dot(n, wg))
        pe = _pe(p_ref[...], wp_ref)
        dpe_ref[...] = (dv * gate).astype(BF)
        dgp = (dv * pe * gate * (1.0 - gate)).astype(BF)
        dgp_ref[...] = dgp
        dx, dg = _rms_bwd(hv, g_ref[...], _dot(dgp, wg, 1, 1))
        dh_ref[...] = dv + dx
        _accumulate(dg_ref, dg, pl.program_id(0) == 0)

    row = lambda c: pl.BlockSpec((tm, c), lambda i: (i, 0))
    wpp_spec = pl.BlockSpec(wpp.shape, lambda i: (0, 0, 0))
    return pl.pallas_call(
        body,
        name="ple_bwd",
        grid=(t // tm,),
        in_specs=[row(d), row(d), pl.BlockSpec((1, d), lambda i: (0, 0)), row(p.shape[1]), _w1024_spec(d, 1, 4), wpp_spec],
        out_specs=[row(d), pl.BlockSpec((1, d), lambda i: (0, 0)), row(d), row(d), row(d)],
        out_shape=[jax.ShapeDtypeStruct((t, d), F32), jax.ShapeDtypeStruct((1, d), F32), jax.ShapeDtypeStruct((t, d), BF),
                   jax.ShapeDtypeStruct((t, d), BF), jax.ShapeDtypeStruct((t, d), BF)],
        compiler_params=_params(("arbitrary",)),
    )(dho, h, g, p, w1024, wpp)


def loss_bwd(h, g, target):
    d = h.shape[1]

    def fn(i, nt, rows, vecs, prevs, nexts):
        err = _rms(rows[0], vecs[0]) - rows[1]
        dx, dg = _rms_bwd(rows[0], vecs[0], err * (1.0 / d))
        return [dx], [jnp.sum(err * err, axis=0, keepdims=True) * (0.5 / d), dg]

    return ew(fn, [h, target], [g], [(d, F32)], [(1, d), (1, d)], tm=512, name="loss_bwd")


def adamw(w, g, m, v, name):
    c1, c2 = 1.0 / (1.0 - ADAM_B1 ** ADAM_STEP), 1.0 / (1.0 - ADAM_B2 ** ADAM_STEP)

    def fn(i, nt, rows, vecs, prevs, nexts):
        wv, gv, mv, vv = rows
        mn = ADAM_B1 * mv + (1.0 - ADAM_B1) * gv
        vn = ADAM_B2 * vv + (1.0 - ADAM_B2) * (gv * gv)
        delta = -ADAM_LR * ((mn * c1) / (jnp.sqrt(vn * c2) + ADAM_EPS) + ADAM_WD * wv)
        return [delta, mn, vn], []

    c = w.shape[1]
    return ew(fn, [w, g, m, v], [], [(c, F32)] * 3, tm=_row_tile(w.shape[0], c, HALO), name=name)


def _place():
    return lax.axis_index("x"), lax.axis_index("y"), lax.axis_index("c")


def _other_chips(x, y):
    return [(1 - x, y), (x, 1 - y), (1 - x, 1 - y)]


ANY = pl.BlockSpec(memory_space=pl.ANY)


def _comm_call(body, name, ins, out_shapes, n_sems, aliases=None):
    return pl.pallas_call(
        body,
        name=name,
        in_specs=[ANY] * len(ins),
        out_specs=[ANY] * len(out_shapes),
        out_shape=out_shapes,
        scratch_shapes=[pltpu.SemaphoreType.DMA((n_sems,)), pltpu.SemaphoreType.DMA((n_sems,))],
        input_output_aliases=aliases or {},
    )(*ins)


def gather_rider(packs):
    nt = len(packs)

    def pieces(ins, outs, send_sems, recv_sems):
        x, y, cc = _place()
        chips = _other_chips(x, y)
        sibling = (x, y, 1 - cc)
        k_me = 2 * x + y

        def copy(k, src, dst, to):
            return pltpu.make_async_remote_copy(src_ref=src, dst_ref=dst, send_sem=send_sems.at[k], recv_sem=recv_sems.at[k],
                                                device_id=to, device_id_type=MESH)

        sends, forwards, arrivals = [], [], []
        for ti in range(nt):
            for j, (px, py) in enumerate(chips):
                sends.append(copy(7 * ti + j, ins[ti].at[cc], outs[ti].at[k_me, cc], (px, py, cc)))
                landed = outs[ti].at[2 * px + py, cc]
                forwards.append((copy(7 * ti + j, landed, landed, (px, py, cc)), copy(7 * ti + 3 + j, landed, landed, sibling)))
                passed = outs[ti].at[2 * px + py, 1 - cc]
                arrivals.append(copy(7 * ti + 3 + j, passed, passed, sibling))
            sends.append(copy(7 * ti + 6, ins[ti], outs[ti].at[k_me], sibling))
            own = outs[ti].at[k_me]
            arrivals.append(copy(7 * ti + 6, own, own, sibling))
        return sends, forwards, arrivals

    def start(*parts):
        for cp in pieces(*parts)[0]:
            cp.start()

    def mid(*parts):
        for landed, forward in pieces(*parts)[1]:
            landed.wait_recv()
            forward.start()

    def finish(*parts):
        sends, forwards, arrivals = pieces(*parts)
        for cp in arrivals:
            cp.wait_recv()
        for cp in sends + [f for _, f in forwards]:
            cp.wait_send()

    return Rider(packs, [jax.ShapeDtypeStruct((NCHIP,) + p.shape, p.dtype) for p in packs], 7 * nt, start, finish, mid)


def swap_packs(gs, name):
    nt = len(gs)
    hl = gs[0].shape[1] // 2

    def body(*refs):
        ins, outs, (send_sems, recv_sems) = refs[:nt], refs[nt:2 * nt], refs[2 * nt:]
        x, y, cc = _place()
        theirs = pl.ds((1 - cc) * hl, hl)
        cps = [pltpu.make_async_remote_copy(src_ref=ins[ti].at[:, theirs], dst_ref=outs[ti], send_sem=send_sems.at[ti], recv_sem=recv_sems.at[ti],
                                            device_id=(x, y, 1 - cc), device_id_type=MESH) for ti in range(nt)]
        for cp in cps:
            cp.start()
        for cp in cps:
            cp.wait()

    return _comm_call(body, name, gs, [jax.ShapeDtypeStruct((NCHIP, hl) + g.shape[2:], g.dtype) for g in gs], nt)


def scatter_packs(cs, name):
    return scatter_rider(cs).standalone(name)


def scatter_rider(cs):
    nt = len(cs)

    def copies(ins, outs, send_sems, recv_sems):
        x, y, cc = _place()
        cps = []
        for ti in range(nt):
            for j, (px, py) in enumerate(_other_chips(x, y)):
                cps.append(pltpu.make_async_remote_copy(src_ref=ins[ti].at[2 * px + py], dst_ref=outs[ti].at[j], send_sem=send_sems.at[3 * ti + j],
                                                        recv_sem=recv_sems.at[3 * ti + j], device_id=(px, py, cc), device_id_type=MESH))
        return cps

    def start(*parts):
        for cp in copies(*parts):
            cp.start()

    def finish(*parts):
        for cp in copies(*parts):
            cp.wait()

    return Rider(cs, [jax.ShapeDtypeStruct((3,) + c_.shape[1:], c_.dtype) for c_ in cs], 3 * nt, start, finish)


def join_packs(fulls, name):
    nt = len(fulls)
    hl = fulls[0].shape[0] // 2

    def body(*refs):
        ins, outs, (send_sems, recv_sems) = refs[:nt], refs[nt:2 * nt], refs[2 * nt:]
        x, y, cc = _place()
        mine = pl.ds(cc * hl, hl)
        cps = [pltpu.make_async_remote_copy(src_ref=ins[ti].at[mine], dst_ref=outs[ti].at[mine], send_sem=send_sems.at[ti], recv_sem=recv_sems.at[ti],
                                            device_id=(x, y, 1 - cc), device_id_type=MESH) for ti in range(nt)]
        for cp in cps:
            cp.start()
        for cp in cps:
            cp.wait()

    return _comm_call(body, name, fulls, [jax.ShapeDtypeStruct(f.shape, f.dtype) for f in fulls], nt, aliases={ti: ti for ti in range(nt)})


def add_sibling(g, recv, name):
    _, nl, r, c = g.shape
    hl = nl // 2
    tm, tc = _tile2(r, c)

    def body(g_ref, r_ref, o_ref):
        o_ref[...] = (g_ref[...].astype(F32) + r_ref[...].astype(F32)).astype(o_ref.dtype)

    blk = (None, None, tm, tc)
    return pl.pallas_call(
        body,
        name=name,
        grid=(NCHIP, hl, r // tm, c // tc),
        in_specs=[pl.BlockSpec(blk, lambda k, l, i, j: (k, lax.axis_index("c") * hl + l, i, j)), pl.BlockSpec(blk, lambda k, l, i, j: (k, l, i, j))],
        out_specs=pl.BlockSpec(blk, lambda k, l, i, j: (k, l, i, j)),
        out_shape=jax.ShapeDtypeStruct(recv.shape, BF),
        compiler_params=_params(("parallel",) * 4),
    )(g, recv)


def add_chips(cs, got, nl, name):
    _, hl, r, c = cs.shape
    tm, tc = _tile2(r, c)

    def body(own_ref, got_ref, o_ref):
        o_ref[...] = own_ref[...].astype(F32) + got_ref[0].astype(F32) + got_ref[1].astype(F32) + got_ref[2].astype(F32)

    return pl.pallas_call(
        body,
        name=name,
        grid=(hl, r // tm, c // tc),
        in_specs=[pl.BlockSpec((None, None, tm, tc), lambda l, i, j: (2 * lax.axis_index("x") + lax.axis_index("y"), l, i, j)),
                  pl.BlockSpec((3, None, tm, tc), lambda l, i, j: (0, l, i, j))],
        out_specs=pl.BlockSpec((None, tm, tc), lambda l, i, j: (lax.axis_index("c") * hl + l, i, j)),
        out_shape=jax.ShapeDtypeStruct((nl, r, c), F32),
        compiler_params=_params(("parallel",) * 3),
    )(cs, got)


def all_gather_xy(shard, name):
    r, c = shard.shape
    hr = r // 2
    assert r % 32 == 0

    def body(x_ref, out_ref, send_sems, recv_sems, local_sem):
        x, y, cc = _place()
        chips = _other_chips(x, y)
        mine = pl.ds(pl.multiple_of(cc * hr, 16), hr)
        theirs = pl.ds(pl.multiple_of((1 - cc) * hr, 16), hr)
        k_me = 2 * x + y

        def copy(k, src, dst, to):
            return pltpu.make_async_remote_copy(src_ref=src, dst_ref=dst, send_sem=send_sems.at[k], recv_sem=recv_sems.at[k],
                                                device_id=to, device_id_type=MESH)

        own = pltpu.make_async_copy(x_ref, out_ref.at[k_me], local_sem)
        own.start()
        first = [copy(j, x_ref.at[mine], out_ref.at[k_me, mine], (*chip, cc)) for j, chip in enumerate(chips)]
        for cp in first:
            cp.start()
        passed = []
        for j, (px, py) in enumerate(chips):
            landed = out_ref.at[2 * px + py, mine]
            copy(j, landed, landed, (px, py, cc)).wait_recv()
            fw = copy(3 + j, landed, landed, (x, y, 1 - cc))
            fw.start()
            passed.append(fw)
        for j, (px, py) in enumerate(chips):
            landed = out_ref.at[2 * px + py, theirs]
            copy(3 + j, landed, landed, (x, y, 1 - cc)).wait_recv()
        for cp in first + passed:
            cp.wait_send()
        own.wait()

    return pl.pallas_call(
        body,
        name=name,
        in_specs=[ANY],
        out_specs=ANY,
        out_shape=jax.ShapeDtypeStruct((NCHIP, r, c), shard.dtype),
        scratch_shapes=[pltpu.SemaphoreType.DMA((6,)), pltpu.SemaphoreType.DMA((6,)), pltpu.SemaphoreType.DMA],
    )(shard)


def all_gather_8(block, name):
    m, c = block.shape

    def body(x_ref, out_ref, send_sems, recv_sems, local_sem):
        x, y, cc = _place()
        me, sibling = (x, y, cc), (x, y, 1 - cc)
        chips = _other_chips(x, y)

        def rows(px, py, pc):
            return out_ref.at[4 * px + 2 * py + pc]

        def copy(k, blk, to, src=None):
            return pltpu.make_async_remote_copy(src_ref=rows(*blk) if src is None else src, dst_ref=rows(*blk), send_sem=send_sems.at[k],
                                                recv_sem=recv_sems.at[k], device_id=to, device_id_type=MESH)

        mine = pltpu.make_async_copy(x_ref, rows(*me), local_sem)
        mine.start()
        first = [copy(0, me, sibling, src=x_ref)]
        first += [copy(1 + j, me, (*chip, cc), src=x_ref) for j, chip in enumerate(chips)]
        for cp in first:
            cp.start()
        passed = [copy(4 + j, (*chip, cc), sibling) for j, chip in enumerate(chips)]
        for j, chip in enumerate(chips):
            copy(1 + j, (*chip, cc), me).wait_recv()
            passed[j].start()
        copy(0, sibling, me).wait_recv()
        for j, chip in enumerate(chips):
            copy(4 + j, (*chip, 1 - cc), me).wait_recv()
        for cp in first + passed:
            cp.wait_send()
        mine.wait()

    return pl.pallas_call(
        body,
        name=name,
        in_specs=[pl.BlockSpec(memory_space=pltpu.VMEM)],
        out_specs=pl.BlockSpec(memory_space=pltpu.VMEM),
        out_shape=jax.ShapeDtypeStruct((8, m, c), block.dtype),
        scratch_shapes=[pltpu.SemaphoreType.DMA((7,)), pltpu.SemaphoreType.DMA((7,)), pltpu.SemaphoreType.DMA],
        compiler_params=pltpu.CompilerParams(vmem_limit_bytes=VMEM_LIMIT),
    )(block)


def add_parts(parts, out_dtype, name, tm=512):
    def fn(i, nt, rows, vecs, prevs, nexts):
        acc = rows[0]
        for r_ in rows[1:]:
            acc = acc + r_
        return [acc], []
    r, c = parts[0].shape
    return ew(fn, list(parts), [], [(c, out_dtype)], tm=_tile(r, tm, 16), name=name)[0]


SMALL_SHARDED = ("sc_conv_w", "m_conv_w")
SMALL_REPL = ("ffn1_norm", "mix_norm", "m_conv_b", "m_dt_bias", "m_A_log", "m_D", "m_norm", "ffn2_norm", "ple_norm", "final_norm")
BIG = ("ffn1_wg", "ffn1_wu", "ffn1_wd", "w_in", "sc_w_out", "m_w_out", "w_o", "ffn2_wg", "ffn2_wu", "ffn2_wd", "ple_w_gate", "ple_w_proj")
TRANSPOSED = ("ffn1_wg", "ffn1_wu", "ffn2_wg", "ffn2_wu", "w_in")
ORDER = ("ffn1_norm", "ffn1_wg", "ffn1_wu", "ffn1_wd", "mix_norm", "w_in", "sc_conv_w", "sc_w_out", "m_conv_w", "m_conv_b", "m_dt_bias",
         "m_A_log", "m_D", "m_norm", "m_w_out", "w_o", "ffn2_norm", "ffn2_wg", "ffn2_wu", "ffn2_wd", "ple_norm", "ple_w_gate", "ple_w_proj",
         "final_norm")


def _pack(arrs, cols, row_mult):
    flat = jnp.concatenate([a.reshape(-1) for a in arrs])
    n = flat.shape[0]
    rows = -(-n // cols)
    rows = -(-rows // row_mult) * row_mult
    return jnp.pad(flat, (0, rows * cols - n)).reshape(rows, cols)


def _unpack(flat2d, shapes):
    flat = flat2d.reshape(-1)
    out, off = [], 0
    for s in shapes:
        n = int(np.prod(s))
        out.append(flat[off:off + n].reshape(s))
        off += n
    return out


def _row_cat(arrs, dtype):
    return jnp.concatenate([a.astype(dtype) for a in arrs], axis=1)


def kernel(x, p, ffn1_norm, ffn1_wg, ffn1_wu, ffn1_wd, mix_norm, w_in, sc_conv_w, sc_w_out, m_conv_w, m_conv_b, m_dt_bias, m_A_log, m_D, m_norm, m_w_out, w_o, ffn2_norm, ffn2_wg, ffn2_wu, ffn2_wd, ple_norm, ple_w_gate, ple_w_proj, final_norm, loss_target, m_ffn1_norm, m_ffn1_wg, m_ffn1_wu, m_ffn1_wd, m_mix_norm, m_w_in, m_sc_conv_w, m_sc_w_out, m_m_conv_w, m_m_conv_b, m_m_dt_bias, m_m_A_log, m_m_D, m_m_norm, m_m_w_out, m_w_o, m_ffn2_norm, m_ffn2_wg, m_ffn2_wu, m_ffn2_wd, m_ple_norm, m_ple_w_gate, m_ple_w_proj, m_final_norm, v_ffn1_norm, v_ffn1_wg, v_ffn1_wu, v_ffn1_wd, v_mix_norm, v_w_in, v_sc_conv_w, v_sc_w_out, v_m_conv_w, v_m_conv_b, v_m_dt_bias, v_m_A_log, v_m_D, v_m_norm, v_m_w_out, v_w_o, v_ffn2_norm, v_ffn2_wg, v_ffn2_wu, v_ffn2_wd, v_ple_norm, v_ple_w_gate, v_ple_w_proj, v_final_norm):
    args = dict(locals())
    wts = {n: args[n] for n in ORDER}
    mom = {n: args["m_" + n] for n in ORDER}
    vel = {n: args["v_" + n] for n in ORDER}

    depth = ffn1_norm.shape[0]
    d = x.shape[-1]
    w = 2 * d
    hh = w // SSM_P
    cw = w + 2 * SSM_G * SSM_N
    d4 = d // NCHIP
    my_x, my_y, my_c = _place()
    k_me = 2 * my_x + my_y

    tr = lambda a: jnp.swapaxes(a, 1, 2)
    gu_t = [_row_cat([tr(wg_), tr(wu_)], BF) for wg_, wu_ in ((ffn1_wg, ffn1_wu), (ffn2_wg, ffn2_wu))]
    wd_l = [ffn1_wd.astype(BF), ffn2_wd.astype(BF)]
    w1024_l = _row_cat([m_w_out, sc_w_out, w_o, ple_w_gate], BF)
    win_l, wpp_l = tr(w_in).astype(BF), ple_w_proj.astype(BF)
    halves = lambda a: a.reshape(2, a.shape[0] // 2, a.shape[1])
    whole = lambda g: g.reshape(NCHIP, g.shape[2] * 2, g.shape[3])

    def pieces(l):
        return {"small": [halves(w1024_l[l]), halves(wpp_l[l])], "win": [halves(win_l[l])], "gu1": [halves(gu_t[0][l])], "d1": [halves(wd_l[0][l])],
                "gu2": [halves(gu_t[1][l])], "d2": [halves(wd_l[1][l])]}

    small_local = [sc_conv_w, m_conv_w]
    gathered_s = all_gather_xy(_pack(small_local, LANE, 32), "gather_conv_weights")
    per_shard_s = [_unpack(gathered_s[k], [a.shape for a in small_local]) for k in range(NCHIP)]
    sc_conv_full = jnp.concatenate([per_shard_s[k][0] for k in range(NCHIP)], axis=2)
    m_conv_full = jnp.concatenate([per_shard_s[k][1] for k in range(NCHIP)], axis=2)

    pad_h = lambda a: jnp.pad(a, ((0, 0), (0, LANE - hh)))
    dt_bias_p, a_log_p = pad_h(m_dt_bias), pad_h(m_A_log)
    d_exp = jnp.repeat(m_D, SSM_P, axis=1)
    e_mat = (jnp.arange(w)[None, :] // SSM_P == jnp.arange(LANE)[:, None]).astype(F32)
    et_mat = e_mat.T
    o_z, o_xbc, o_dt, o_g = 3 * d, 5 * d, 5 * d + cw, 5 * d + cw + hh

    def layer_weights(got):
        wt = {"w1024": whole(got["small"][0]), "wpp": whole(got["small"][1])}
        wt.update({k: whole(got[k][0]) for k in ("gu1", "d1", "gu2", "d2")})
        wi = whole(got["win"][0]).reshape(-1, d)
        wt["sc3"], wt["z"], wt["xbc"], wt["g2"] = wi[:o_z], wi[o_z:o_xbc], wi[o_xbc:o_dt], wi[o_g:o_g + 2 * d]
        wt["dt"] = jnp.pad(wi[o_dt:o_g], ((0, LANE - hh), (0, 0)))
        wt["in_p"] = jnp.concatenate([wt["sc3"], wt["z"], wt["xbc"], wt["g2"], wt["dt"]], axis=0)
        return wt

    first = pieces(0)
    order = ("gu1", "d1", "win", "small", "gu2", "d2")
    flat = gather_rider([a for k in order for a in first[k]]).standalone("gather_weights")
    got, pos = {}, 0
    for k in order:
        got[k] = flat[pos:pos + len(first[k])]
        pos += len(first[k])
    wts_l = [layer_weights(got)]

    h = x[0]
    saved = []
    for i in range(depth):
        s, wt = {}, wts_l[i]
        nxt = pieces(i + 1) if i + 1 < depth else None
        ride = lambda k: gather_rider(nxt[k]) if nxt else None
        got = {}
        s["h0"] = h
        (s["ab1"], s4, s["n1"]), got["small"] = ffn_up(h, ffn1_norm[i:i + 1], wt["gu1"], rider=ride("small"))
        h, got["d1"] = ffn_down(s4, wt["d1"], h, rider=ride("d1"))
        s["h1"] = h
        u = norm_cast(h, mix_norm[i:i + 1])
        s["u"] = u
        s["sc3"] = mm(u, wt["sc3"], tb=True, out_dtype=BF, name="proj_sc")
        s["z"] = mm(u, wt["z"], tb=True, out_dtype=BF, name="proj_z")
        s["xbc_raw"] = mm(u, wt["xbc"], tb=True, out_dtype=BF, name="proj_xbc")
        s["gates"] = mm(u, wt["g2"], tb=True, out_dtype=BF, name="proj_gates")
        s["dt_raw"] = mm(u, wt["dt"], tb=True, name="proj_dt")
        s["ya_in"] = conv_a_fwd(s["sc3"], sc_conv_full[i])
        s["xbc"], s["dt"] = conv_m_fwd(s["xbc_raw"], s["dt_raw"], m_conv_full[i], m_conv_b[i:i + 1], dt_bias_p[i:i + 1])
        (s["yn"], s["y"], s["sprev"]), got["win"] = ssd_fwd(s["xbc"], s["dt"], s["z"], a_log_p[i:i + 1], d_exp[i:i + 1], m_norm[i:i + 1], e_mat,
                                                            rider=ride("win"))
        (h, s["y_a"], s["y_m"], s["merged"]), got["gu2"] = mix_out_fwd(s["ya_in"], s["yn"], s["gates"], h, wt["w1024"], rider=ride("gu2"))
        s["h2"] = h
        (s["ab2"], s4, s["n2"]), got["gu1"] = ffn_up(h, ffn2_norm[i:i + 1], wt["gu2"], rider=ride("gu1"))
        h, got["d2"] = ffn_down(s4, wt["d2"], h, rider=ride("d2"))
        s["h3"] = h
        h = ple_fwd(h, ple_norm[i:i + 1], p[i, 0], wt["w1024"], wt["wpp"])
        saved.append(s)
        if nxt:
            wts_l.append(layer_weights(got))

    dh, loss_lanes, g_final = loss_bwd(h, final_norm[None, :], loss_target[0])
    loss = lax.psum(jnp.sum(loss_lanes), ("x", "y", "c"))

    def finish_reduce(cs, got):
        halves = [add_chips(c_, g_, 2, "grad_add_chips") for c_, g_ in zip(cs, got, strict=True)]
        return [f.reshape(-1, f.shape[2]) for f in join_packs(halves, "grad_join_halves")]

    pending, reduced = None, [None] * depth
    gs = {n: [None] * depth for n in SMALL_SHARDED + SMALL_REPL if n != "final_norm"}
    for i in reversed(range(depth)):
        s = saved[i]
        wt = wts_l[i]
        dh, gs["ple_norm"][i], n3, dgp, dpe = ple_bwd(dh, s["h3"], ple_norm[i:i + 1], p[i, 0], wt["w1024"], wt["wpp"])
        g_pg = mm(n3, dgp, ta=True, out_dtype=BF, name="g_ple_gate", tm_cap=512, tn_cap=512)
        g_pp = mm(p[i, 0], dpe, ta=True, out_dtype=BF, name="g_ple_proj", tm_cap=512, tn_cap=512)
        g_pp = jnp.transpose(g_pp.reshape(g_pp.shape[0], NCHIP, d4), (1, 0, 2))
        dn2, s2, dab2 = ffn_bwd(dh, s["ab2"], wt["gu2"], wt["d2"])
        g_ffn2 = ffn_wgrads(s["n2"], dh, s2, dab2)
        dh, gs["ffn2_norm"][i] = norm_bwd_add(dh, s["h2"], ffn2_norm[i:i + 1], dn2)
        dgates, dya, dyn, dy_a, dy_m = mix_out_bwd(dh, s["gates"], s["y_a"], s["y_m"], wt["w1024"])
        g_wo = mm(s["merged"], dh, ta=True, out_dtype=BF, name="g_w_o", tm_cap=512, tn_cap=512)
        g_sco = mm(s["ya_in"], dy_a, ta=True, out_dtype=BF, name="g_sc_out", tm_cap=512, tn_cap=512)
        g_mo = mm(s["yn"], dy_m, ta=True, out_dtype=BF, name="g_m_out", tm_cap=512, tn_cap=512)
        g_1024 = jnp.concatenate([g_mo.reshape(NCHIP, 2 * d4, d), g_sco.reshape(NCHIP, d4, d), g_wo.reshape(NCHIP, d4, d),
                                  g_pg.reshape(NCHIP, d4, d)], axis=1)
        (dz, dxbc, ddt, gs["m_norm"][i], gd, gal), got_a = ssd_bwd(dyn, s["y"], s["z"], s["xbc"], s["dt"], s["sprev"], a_log_p[i:i + 1], d_exp[i:i + 1],
                                                                   m_norm[i:i + 1], e_mat, et_mat, rider=scatter_rider(pending[:1]) if pending else None)
        gs["m_D"][i], gs["m_A_log"][i] = gd[:, :hh], gal[:, :hh]
        dpre, ddt_raw, gdb = conv_m_bwd1(dxbc, s["xbc_raw"], ddt, s["dt_raw"], m_conv_full[i], m_conv_b[i:i + 1], dt_bias_p[i:i + 1])
        gs["m_dt_bias"][i] = gdb[:, :hh]
        dxbc_raw, gs["m_conv_w"][i], gs["m_conv_b"][i] = conv_bwd2(dpre, s["xbc_raw"], m_conv_full[i], "conv_m_bwd2")
        dcv, dsc_b, v = conv_a_bwd1(dya, s["sc3"], sc_conv_full[i])
        dsc_c, dsc_x, gs["sc_conv_w"][i] = conv_a_bwd2(dcv, v, s["sc3"], sc_conv_full[i])
        dproj = jnp.concatenate([dsc_b, dsc_c, dsc_x, dz, dxbc_raw, dgates, ddt_raw], axis=1)
        if pending:
            du, got_b = mm(dproj, wt["in_p"], name="d_proj_in", tn_cap=512, rider=scatter_rider(pending[2:3]))
            gwp, got_c = mm(dproj, s["u"], ta=True, out_dtype=BF, name="g_w_in", tm_cap=1152, tn_cap=512, rider=scatter_rider(pending[1:2] + pending[3:]))
            reduced[i + 1] = finish_reduce(pending, [got_a[0], got_c[0], got_b[0], got_c[1]])
        else:
            du = mm(dproj, wt["in_p"], name="d_proj_in", tn_cap=512)
            gwp = mm(dproj, s["u"], ta=True, out_dtype=BF, name="g_w_in", tm_cap=1152, tn_cap=512)
        gw_rows = jnp.concatenate([gwp[:5 * d + cw], gwp[7 * d + cw:7 * d + cw + hh], gwp[5 * d + cw:7 * d + cw]], axis=0)
        g_in = gw_rows.reshape(NCHIP, -1, d)
        dh, gs["mix_norm"][i] = norm_bwd_add(dh, s["h1"], mix_norm[i:i + 1], du)
        dn1, s1, dab1 = ffn_bwd(dh, s["ab1"], wt["gu1"], wt["d1"])
        g_ffn1 = ffn_wgrads(s["n1"], dh, s1, dab1)
        dh, gs["ffn1_norm"][i] = norm_bwd_add(dh, s["h0"], ffn1_norm[i:i + 1], dn1)
        g_layer = [jnp.concatenate([g_ffn1, g_ffn2], axis=1), g_1024, g_in, g_pp]
        g_layer = [g.reshape(NCHIP, 2, g.shape[1] // 2, g.shape[2]) for g in g_layer]
        from_sibling = swap_packs(g_layer, "grad_swap_halves")
        pending = [add_sibling(g, r_, "grad_add_sibling") for g, r_ in zip(g_layer, from_sibling, strict=True)]
    reduced[0] = finish_reduce(pending, scatter_packs(pending, "grad_scatter"))
    grad_x = dh[None]

    f4 = reduced[0][0].shape[0] // 6
    rows_of = lambda j, lo, hi: jnp.stack([reduced[l][j][lo:hi] for l in range(depth)])
    ffn_rows = lambda j: rows_of(0, j * f4, (j + 1) * f4)
    grads = {
        "ffn1_wg": ffn_rows(0), "ffn1_wu": ffn_rows(1), "ffn1_wd": ffn_rows(2), "ffn2_wg": ffn_rows(3), "ffn2_wu": ffn_rows(4), "ffn2_wd": ffn_rows(5),
        "m_w_out": rows_of(1, 0, 2 * d4), "sc_w_out": rows_of(1, 2 * d4, 3 * d4), "w_o": rows_of(1, 3 * d4, 4 * d4), "ple_w_gate": rows_of(1, 4 * d4, 5 * d4),
        "w_in": rows_of(2, 0, None), "ple_w_proj": rows_of(3, 0, None),
    }

    small_names = list(SMALL_SHARDED + SMALL_REPL)
    small_full = [g_final[0] if n == "final_norm" else jnp.stack(gs[n]) for n in small_names]
    small_pack = _pack(small_full, LANE, HALO)
    all8 = all_gather_8(small_pack, "gather_small_grads")
    small_sum = add_parts([all8[k] for k in range(8)], F32, "add_small_grads", tm=256)
    for n, tot in zip(small_names, _unpack(small_sum, [a.shape for a in small_full]), strict=True):
        if n in SMALL_SHARDED:
            cl = wts[n].shape[2]
            grads[n] = lax.dynamic_slice_in_dim(tot, k_me * cl, cl, axis=2)
        else:
            grads[n] = tot.reshape(wts[n].shape)

    delta, new_m, new_v = {}, {}, {}
    for n in BIG:
        view = tr if n in TRANSPOSED else (lambda a: a)
        shp = grads[n].shape
        two = lambda a: a.reshape(-1, shp[-1])
        dl, nm, nv = adamw(two(view(wts[n])), two(grads[n]), two(view(mom[n])), two(view(vel[n])), "adamw_" + "x".join(map(str, shp[1:])))
        grads[n], delta[n], new_m[n], new_v[n] = view(grads[n]), view(dl.reshape(shp)), view(nm.reshape(shp)), view(nv.reshape(shp))
    for n in small_names:
        shp = wts[n].shape
        two = lambda a: a.reshape(-1, shp[-1])
        dl, nm, nv = adamw(two(wts[n]), two(grads[n]), two(mom[n]), two(vel[n]), "adamw_small_" + "x".join(map(str, shp)))
        delta[n], new_m[n], new_v[n] = dl.reshape(shp), nm.reshape(shp), nv.reshape(shp)

    return (loss, grad_x, *[grads[n] for n in ORDER], *[delta[n] for n in ORDER], *[new_m[n] for n in ORDER], *[new_v[n] for n in ORDER])
```

```python
import jax
import jax.numpy as jnp
import numpy as np
from jax import lax
from jax.experimental import pallas as pl
from jax.experimental.pallas import tpu as pltpu

BF = jnp.bfloat16
F32 = jnp.float32
EPS = 1e-6
LANE = 128
HALO = 8
SSM_P = 64
SSM_N = 128
SSM_G = 4
SSM_L = 128
ADAM_LR, ADAM_B1, ADAM_B2, ADAM_EPS, ADAM_WD, ADAM_STEP = 0.001, 0.9, 0.999, 1e-08, 0.01, 10
VMEM_LIMIT = 56 * 1024 * 1024
TILE_ELEMS = 400_000
NCHIP = 4
FFN_SUB = 256
MESH = pl.DeviceIdType.MESH
HI = lax.Precision.HIGHEST


def _tile(n, cap, mult=LANE):
    best = None
    t = mult
    while t <= min(n, cap):
        if n % t == 0:
            best = t
        t += mult
    return best if best is not None else n


def _row_tile(r, c, mult=16):
    return _tile(r, max(mult, TILE_ELEMS // c // mult * mult), mult)


def _tile2(r, c, mult=16):
    tm = _row_tile(r, c, mult)
    tc = c if tm * c <= TILE_ELEMS else _tile(c, max(LANE, TILE_ELEMS // tm // LANE * LANE))
    return tm, tc


def _params(sem):
    return pltpu.CompilerParams(dimension_semantics=sem, vmem_limit_bytes=VMEM_LIMIT)


def _sigmoid(x):
    return 1.0 / (1.0 + jnp.exp(-x))


def _dot(a, b, ca=1, cb=0, precision=None):
    return lax.dot_general(a, b, (((ca,), (cb,)), ((), ())), precision=precision, preferred_element_type=F32)


def _rms(x, g):
    r = lax.rsqrt(jnp.mean(x * x, axis=-1, keepdims=True) + EPS)
    return x * r * g


def _rms_bwd(x, g, dy):
    r = lax.rsqrt(jnp.mean(x * x, axis=-1, keepdims=True) + EPS)
    xh = x * r
    dxh = dy * g
    dx = r * (dxh - xh * jnp.mean(dxh * xh, axis=-1, keepdims=True))
    return dx, jnp.sum(dy * xh, axis=0, keepdims=True)


def _accumulate(ref, val, first):
    @pl.when(first)
    def _():
        ref[...] = val

    @pl.when(jnp.logical_not(first))
    def _():
        ref[...] += val


RIDER_MID = 1.0


class Rider:
    def __init__(self, ins, out_shapes, n_sems, start, finish, mid=None):
        self.ins, self.out_shapes, self.n_sems, self.start, self.mid, self.finish = list(ins), list(out_shapes), n_sems, start, mid, finish

    def standalone(self, name):
        ni, no = len(self.ins), len(self.out_shapes)

        def body(*refs):
            parts = (refs[:ni], refs[ni:ni + no], *refs[ni + no:])
            self.start(*parts)
            if self.mid is not None:
                self.mid(*parts)
            self.finish(*parts)

        return _comm_call(body, name, self.ins, self.out_shapes, self.n_sems)


def host_call(body, *, name, grid, in_specs, out_specs, out_shape, scratch_shapes, operands, rider=None, aliases=None):
    n_in, n_out = len(in_specs), len(out_specs)
    aliases = aliases or {}
    if rider is None:
        outs = pl.pallas_call(body, name=name, grid=grid, in_specs=in_specs, out_specs=out_specs, out_shape=out_shape, scratch_shapes=scratch_shapes,
                              input_output_aliases=aliases, compiler_params=_params(("arbitrary",) * len(grid)))(*operands)
        return list(outs), []
    ri, ro = len(rider.ins), len(rider.out_shapes)

    def hosted(*refs):
        ins, r_ins = refs[:n_in], refs[n_in:n_in + ri]
        outs, r_outs = refs[n_in + ri:n_in + ri + n_out], refs[n_in + ri + n_out:n_in + ri + n_out + ro]
        scratch, (send_sems, recv_sems) = refs[n_in + ri + n_out + ro:-2], refs[-2:]
        step, total = 0, 1
        for ax, n in enumerate(grid):
            step = step * n + pl.program_id(ax)
            total *= n
        parts = (r_ins, r_outs, send_sems, recv_sems)

        @pl.when(step == 0)
        def _():
            rider.start(*parts)

        if rider.mid is not None:
            @pl.when(step == min(total - 1, int(total * RIDER_MID)))
            def _():
                rider.mid(*parts)

        body(*ins, *outs, *scratch)

        @pl.when(step == total - 1)
        def _():
            rider.finish(*parts)

    outs = pl.pallas_call(
        hosted,
        name=name,
        grid=grid,
        in_specs=list(in_specs) + [ANY] * ri,
        out_specs=list(out_specs) + [ANY] * ro,
        out_shape=list(out_shape) + rider.out_shapes,
        scratch_shapes=list(scratch_shapes) + [pltpu.SemaphoreType.DMA((rider.n_sems,)), pltpu.SemaphoreType.DMA((rider.n_sems,))],
        input_output_aliases=aliases,
        compiler_params=_params(("arbitrary",) * len(grid)),
    )(*operands, *rider.ins)
    return list(outs[:n_out]), list(outs[n_out:])


def mmx(name, a, b, *, grid, a_spec, b_spec, o_spec, o_shape, o_dtype, ca, cb, acc_shape=None, scale=None, rider=None):
    nk = grid[-1] if acc_shape is not None else 1
    assert scale is None or nk == 1

    def body(a_ref, b_ref, o_ref, *acc):
        p = _dot(a_ref[...].astype(BF), b_ref[...].astype(BF), ca, cb)
        if scale is not None:
            p = p * scale
        if nk == 1:
            o_ref[...] = p.astype(o_ref.dtype)
        else:
            kk = pl.program_id(len(grid) - 1)
            _accumulate(acc[0], p, kk == 0)

            @pl.when(kk == nk - 1)
            def _():
                o_ref[...] = acc[0][...].astype(o_ref.dtype)

    if rider is not None:
        (out,), r_outs = host_call(body, name=name, grid=grid, in_specs=[a_spec, b_spec], out_specs=[o_spec], out_shape=[jax.ShapeDtypeStruct(o_shape, o_dtype)],
                                   scratch_shapes=[pltpu.VMEM(acc_shape, F32)] if nk > 1 else [], operands=(a, b), rider=rider)
        return out, r_outs
    sem = ("parallel",) * (len(grid) - 1) + ("arbitrary" if nk > 1 else "parallel",)
    return pl.pallas_call(
        body,
        name=name,
        grid=grid,
        in_specs=[a_spec, b_spec],
        out_specs=o_spec,
        out_shape=jax.ShapeDtypeStruct(o_shape, o_dtype),
        scratch_shapes=[pltpu.VMEM(acc_shape, F32)] if nk > 1 else [],
        compiler_params=_params(sem),
    )(a, b)


def mm(a, b, *, ta=False, tb=False, out_dtype=F32, name, tm_cap=1024, tn_cap=1024, tk_cap=4096, rider=None):
    m, k = (a.shape[1], a.shape[0]) if ta else a.shape
    n = b.shape[0] if tb else b.shape[1]
    assert (b.shape[1] if tb else b.shape[0]) == k
    tm, tn, tk = _tile(m, tm_cap), _tile(n, tn_cap), _tile(k, tk_cap)
    nk = k // tk
    a_spec = pl.BlockSpec((tk, tm), lambda i, j, kk: (kk, i)) if ta else pl.BlockSpec((tm, tk), lambda i, j, kk: (i, kk))
    b_spec = pl.BlockSpec((tn, tk), lambda i, j, kk: (j, kk)) if tb else pl.BlockSpec((tk, tn), lambda i, j, kk: (kk, j))
    return mmx(name, a, b, grid=(m // tm, n // tn, nk), a_spec=a_spec, b_spec=b_spec, o_spec=pl.BlockSpec((tm, tn), lambda i, j, kk: (i, j)),
               o_shape=(m, n), o_dtype=out_dtype, ca=0 if ta else 1, cb=1 if tb else 0, acc_shape=(tm, tn) if nk > 1 else None, rider=rider)


class Window:
    def __init__(self, out, block, cols, buf=None):
        self.out, self.block, self.cols, self.buf = out, block, cols, buf


def ew(fn, rows, vecs, out_rows, out_red=(), *, tm, name, prev_halo=(), next_halo=(), window=None):
    t = rows[0].shape[0]
    tm = min(tm, t)
    nt = t // tm
    assert t % tm == 0 and (tm % HALO == 0 or (tm == t and not prev_halo and not next_halo))
    nr, nv, npv, nnx, nor = len(rows), len(vecs), len(prev_halo), len(next_halo), len(out_rows)
    hb = tm // HALO
    n_in = nr + nv + npv + nnx
    passed = window is not None and window.buf is not None

    def body(*refs):
        i = pl.program_id(0)
        ins = [r[...].astype(F32) for r in refs[:n_in]]
        outs = refs[n_in + passed:]
        o_rows, o_red = fn(i, nt, ins[:nr], ins[nr:nr + nv], ins[nr + nv:nr + nv + npv], ins[nr + nv + npv:])
        for ref, val in zip(outs[:nor], o_rows, strict=True):
            ref[...] = val.astype(ref.dtype)
        for ref, val in zip(outs[nor:], o_red, strict=True):
            _accumulate(ref, val, i == 0)

    in_specs = [pl.BlockSpec((tm, r.shape[1]), lambda i: (i, 0)) for r in rows]
    in_specs += [pl.BlockSpec(v.shape, lambda i: (0, 0)) for v in vecs]
    in_specs += [pl.BlockSpec((HALO, rows[k].shape[1]), lambda i: (jnp.maximum(i * hb - 1, 0), 0)) for k in prev_halo]
    in_specs += [pl.BlockSpec((HALO, rows[k].shape[1]), lambda i: (jnp.minimum((i + 1) * hb, t // HALO - 1), 0)) for k in next_halo]
    out_specs = [pl.BlockSpec((tm, c), lambda i: (i, 0)) for c, _ in out_rows]
    out_specs += [pl.BlockSpec(s, lambda i: (0, 0)) for s in out_red]
    out_shape = [jax.ShapeDtypeStruct((t, c), d) for c, d in out_rows] + [jax.ShapeDtypeStruct(s, F32) for s in out_red]
    operands = [*rows, *vecs, *[rows[k] for k in prev_halo], *[rows[k] for k in next_halo]]
    aliases = {}
    if window is not None:
        c, dt_ = out_rows[window.out]
        out_specs[window.out] = pl.BlockSpec((tm, c), lambda i: (i, window.block))
        out_shape[window.out] = jax.ShapeDtypeStruct((t, window.cols), dt_)
        if passed:
            in_specs.append(ANY)
            operands.append(window.buf)
            aliases = {n_in: window.out}
    return pl.pallas_call(
        body,
        name=name,
        grid=(nt,),
        in_specs=in_specs,
        out_specs=out_specs,
        out_shape=out_shape,
        input_output_aliases=aliases,
        compiler_params=_params(("arbitrary",) if out_red else ("parallel",)),
    )(*operands)


def _shift_down(x, prev, j):
    if j == 0:
        return x
    r = pltpu.roll(x, j, 0)
    rh = pltpu.roll(prev, j, 0)
    row = lax.broadcasted_iota(jnp.int32, (HALO, x.shape[1]), 0)
    head = jnp.where(row < j, rh, r[:HALO])
    return jnp.concatenate([head, r[HALO:]], axis=0)


def _shift_up(x, nxt, j):
    if j == 0:
        return x
    n = x.shape[0]
    r = pltpu.roll(x, n - j, 0)
    rh = pltpu.roll(nxt, HALO - j, 0)
    row = lax.broadcasted_iota(jnp.int32, (HALO, x.shape[1]), 0)
    tail = jnp.where(row >= HALO - j, rh, r[n - HALO:])
    return jnp.concatenate([r[: n - HALO], tail], axis=0)


def _conv_fwd(x, prev, w):
    kk = w.shape[0]
    acc = None
    for k in range(kk):
        term = w[k:k + 1, :] * _shift_down(x, prev, kk - 1 - k)
        acc = term if acc is None else acc + term
    return acc


def ffn_up(h, g, wf, rider=None):
    t, d = h.shape
    f4 = wf.shape[1] // 2
    tm = _tile(t, 1024)
    sub = _tile(tm, FFN_SUB, 16)

    def body(h_ref, g_ref, wg_ref, wu_ref, ab_ref, s_ref, n_ref):
        @pl.when(pl.program_id(1) == 0)
        def _():
            n_ref[...] = _rms(h_ref[...], g_ref[...]).astype(BF)

        for r in range(tm // sub):
            rows = slice(r * sub, (r + 1) * sub)
            n = n_ref[rows, :]
            a = _dot(n, wg_ref[...], 1, 1)
            b = _dot(n, wu_ref[...], 1, 1)
            ab_ref[0, rows, :] = a.astype(BF)
            ab_ref[1, rows, :] = b.astype(BF)
            s_ref[rows, :] = (a * _sigmoid(a) * b).astype(BF)

    wspec = lambda ib: pl.BlockSpec((None, f4, d), lambda i, j: (j, ib, 0))
    return host_call(
        body,
        name="ffn_up",
        grid=(t // tm, NCHIP),
        in_specs=[pl.BlockSpec((tm, d), lambda i, j: (i, 0)), pl.BlockSpec((1, d), lambda i, j: (0, 0)), wspec(0), wspec(1)],
        out_specs=[pl.BlockSpec((2, None, tm, f4), lambda i, j: (0, j, i, 0)), pl.BlockSpec((None, tm, f4), lambda i, j: (j, i, 0)),
                   pl.BlockSpec((tm, d), lambda i, j: (i, 0))],
        out_shape=[jax.ShapeDtypeStruct((2, NCHIP, t, f4), BF), jax.ShapeDtypeStruct((NCHIP, t, f4), BF), jax.ShapeDtypeStruct((t, d), BF)],
        scratch_shapes=[],
        operands=(h, g, wf, wf),
        rider=rider,
    )


def ffn_down(s4, wf, h, rider=None):
    t, d = h.shape
    f4 = s4.shape[2]
    tm = _tile(t, 512)

    def body(s_ref, w_ref, h_ref, o_ref):
        acc = _dot(s_ref[0], w_ref[0])
        for k in range(1, NCHIP):
            acc = acc + _dot(s_ref[k], w_ref[k])
        o_ref[...] = h_ref[...] + 0.5 * acc

    (out,), r_outs = host_call(
        body,
        name="ffn_down",
        grid=(t // tm,),
        in_specs=[pl.BlockSpec((NCHIP, tm, f4), lambda i: (0, i, 0)), pl.BlockSpec((NCHIP, f4, d), lambda i: (0, 0, 0)), pl.BlockSpec((tm, d), lambda i: (i, 0))],
        out_specs=[pl.BlockSpec((tm, d), lambda i: (i, 0))],
        out_shape=[jax.ShapeDtypeStruct((t, d), F32)],
        scratch_shapes=[],
        operands=(s4, wf, h),
        rider=rider,
    )
    return out, r_outs


def ffn_bwd(dho, ab, wf, wd):
    t, d = dho.shape
    f4 = wd.shape[1]
    tm = _tile(t, 1024)
    sub = _tile(tm, FFN_SUB, 16)

    def body(dho_ref, ab_ref, wg_ref, wu_ref, wd_ref, dn_ref, s_ref, dab_ref, do_sc):
        j = pl.program_id(1)

        @pl.when(j == 0)
        def _():
            do_sc[...] = (0.5 * dho_ref[...]).astype(BF)
            dn_ref[...] = jnp.zeros_like(dn_ref)

        for r in range(tm // sub):
            rows = slice(r * sub, (r + 1) * sub)
            ds = _dot(do_sc[rows, :], wd_ref[...], 1, 1)
            av, bv = ab_ref[0, rows, :].astype(F32), ab_ref[1, rows, :].astype(F32)
            sig = _sigmoid(av)
            sl = av * sig
            s_ref[rows, :] = (sl * bv).astype(BF)
            da = (ds * bv * (sig * (1.0 + av * (1.0 - sig)))).astype(BF)
            db = (ds * sl).astype(BF)
            dab_ref[0, rows, :] = da
            dab_ref[1, rows, :] = db
            dn_ref[rows, :] += _dot(da, wg_ref[...]) + _dot(db, wu_ref[...])

    row = lambda c: pl.BlockSpec((tm, c), lambda i, j: (i, 0))
    wspec = lambda ib: pl.BlockSpec((None, f4, d), lambda i, j: (j, ib, 0))
    ab_spec = pl.BlockSpec((2, None, tm, f4), lambda i, j: (0, j, i, 0))
    return pl.pallas_call(
        body,
        name="ffn_bwd",
        grid=(t // tm, NCHIP),
        in_specs=[row(d), ab_spec, wspec(0), wspec(1), wspec(0)],
        out_specs=[row(d), pl.BlockSpec((None, tm, f4), lambda i, j: (j, i, 0)), ab_spec],
        out_shape=[jax.ShapeDtypeStruct((t, d), F32), jax.ShapeDtypeStruct((NCHIP, t, f4), BF), jax.ShapeDtypeStruct((2, NCHIP, t, f4), BF)],
        scratch_shapes=[pltpu.VMEM((tm, d), BF)],
        compiler_params=_params(("parallel", "arbitrary")),
    )(dho, ab, wf, wf, wd)


def ffn_wgrads(n, dho, s4, dab):
    t, d = n.shape
    f4 = s4.shape[2]
    tn = _tile(d, 512)
    g_in = mmx("g_ffn_in", dab, n, grid=(2, NCHIP, d // tn), a_spec=pl.BlockSpec((None, None, t, f4), lambda wh, k, j: (wh, k, 0, 0)),
               b_spec=pl.BlockSpec((t, tn), lambda wh, k, j: (0, j)), o_spec=pl.BlockSpec((None, None, f4, tn), lambda wh, k, j: (k, wh, 0, j)),
               o_shape=(NCHIP, 2, f4, d), o_dtype=BF, ca=0, cb=0)
    g_out = mmx("g_ffn_out", s4, dho, grid=(NCHIP, d // tn), a_spec=pl.BlockSpec((None, t, f4), lambda k, j: (k, 0, 0)),
                b_spec=pl.BlockSpec((t, tn), lambda k, j: (0, j)), o_spec=pl.BlockSpec((None, f4, tn), lambda k, j: (k, 0, j)),
                o_shape=(NCHIP, f4, d), o_dtype=BF, ca=0, cb=0, scale=0.5)
    return jnp.concatenate([g_in.reshape(NCHIP, 2 * f4, d), g_out], axis=1)


def norm_cast(h, g):
    def fn(i, nt, rows, vecs, prevs, nexts):
        return [_rms(rows[0], vecs[0])], []
    return ew(fn, [h], [g], [(h.shape[1], BF)], tm=512, name="norm_cast")[0]


def _zero_if(cond, x):
    return jnp.where(cond, jnp.zeros_like(x), x)


def conv_a_fwd(sc3, w_sc):
    d = sc3.shape[1] // 3

    def fn(i, nt, rows, vecs, prevs, nexts):
        x, pv = rows[0], _zero_if(i == 0, prevs[0])
        v = x[:, d:2 * d] * x[:, 2 * d:]
        vp = pv[:, d:2 * d] * pv[:, 2 * d:]
        return [x[:, :d] * _conv_fwd(v, vp, vecs[0])], []

    return ew(fn, [sc3], [w_sc], [(d, BF)], tm=256, name="conv_a_fwd", prev_halo=(0,))[0]


def _softplus(x):
    e = jnp.exp(-jnp.abs(x))
    return jnp.maximum(x, 0.0) + jnp.where(e < 1e-4, e - 0.5 * e * e, jnp.log(1.0 + e))


def conv_m_fwd(xbc_raw, dt_raw, w_mc, b_mc, dt_bias):
    def fn(i, nt, rows, vecs, prevs, nexts):
        pre = _conv_fwd(rows[0], _zero_if(i == 0, prevs[0]), vecs[0]) + vecs[1]
        return [pre * _sigmoid(pre), _softplus(rows[1] + vecs[2])], []

    return ew(fn, [xbc_raw, dt_raw], [w_mc, b_mc, dt_bias], [(xbc_raw.shape[1], F32), (LANE, F32)], tm=256, name="conv_m_fwd",
              prev_halo=(0,))


def conv_m_bwd1(dxbc, xbc_raw, ddt, dt_raw, w_mc, b_mc, dt_bias, window):
    def fn(i, nt, rows, vecs, prevs, nexts):
        pre = _conv_fwd(rows[1], _zero_if(i == 0, prevs[0]), vecs[0]) + vecs[1]
        sig = _sigmoid(pre)
        dpre = rows[0] * (sig * (1.0 + pre * (1.0 - sig)))
        ddr = rows[2] * _sigmoid(rows[3] + vecs[2])
        return [dpre, ddr], [jnp.sum(ddr, axis=0, keepdims=True)]

    return ew(fn, [dxbc, xbc_raw, ddt, dt_raw], [w_mc, b_mc, dt_bias], [(dxbc.shape[1], F32), (LANE, BF)], [(1, LANE)], tm=256,
              name="conv_m_bwd1", prev_halo=(1,), window=window)


def conv_bwd2(dpre, x, w, name, window):
    kk = w.shape[0]

    def fn(i, nt, rows, vecs, prevs, nexts):
        dp, xv = rows[0], rows[1]
        nx = _zero_if(i == nt - 1, nexts[0])
        dx = None
        dws = []
        for k in range(kk):
            up = _shift_up(dp, nx, kk - 1 - k)
            term = vecs[0][k:k + 1, :] * up
            dx = term if dx is None else dx + term
            dws.append(jnp.sum(up * xv, axis=0, keepdims=True))
        return [dx], [jnp.concatenate(dws, axis=0), jnp.sum(dp, axis=0, keepdims=True)]

    c = x.shape[1]
    return ew(fn, [dpre, x], [w], [(c, BF)], [(kk, c), (1, c)], tm=256, name=name, next_halo=(0,), window=window)


def conv_a_bwd1(dya, sc3, w_sc, window):
    d = sc3.shape[1] // 3

    def fn(i, nt, rows, vecs, prevs, nexts):
        x, pv = rows[1], _zero_if(i == 0, prevs[0])
        v = x[:, d:2 * d] * x[:, 2 * d:]
        vp = pv[:, d:2 * d] * pv[:, 2 * d:]
        return [rows[0] * x[:, :d], rows[0] * _conv_fwd(v, vp, vecs[0]), v], []

    return ew(fn, [dya, sc3], [w_sc], [(d, F32), (d, BF), (d, F32)], tm=256, name="conv_a_bwd1", prev_halo=(1,), window=window)


def conv_a_bwd2(dcv, v, sc3, w_sc, window):
    d = v.shape[1]
    kk = w_sc.shape[0]

    def fn(i, nt, rows, vecs, prevs, nexts):
        dp, vv, x = rows
        nx = _zero_if(i == nt - 1, nexts[0])
        dv = None
        dws = []
        for k in range(kk):
            up = _shift_up(dp, nx, kk - 1 - k)
            term = vecs[0][k:k + 1, :] * up
            dv = term if dv is None else dv + term
            dws.append(jnp.sum(up * vv, axis=0, keepdims=True))
        return [jnp.concatenate([dv * x[:, 2 * d:], dv * x[:, d:2 * d]], axis=1)], [jnp.concatenate(dws, axis=0)]

    return ew(fn, [dcv, v, sc3], [w_sc], [(2 * d, BF)], [(kk, d)], tm=256, name="conv_a_bwd2", next_halo=(0,), window=window)


def _xdot(a, b, passes, split_lhs, ca=1, cb=0):
    parts, r = [], (a if split_lhs else b)
    for _ in range(passes):
        piece = r.astype(BF)
        parts.append(piece)
        r = r - piece.astype(F32)
    other = (b if split_lhs else a).astype(BF)
    acc = None
    for piece in parts:
        term = _dot(piece, other, ca, cb) if split_lhs else _dot(other, piece, ca, cb)
        acc = term if acc is None else acc + term
    return acc


def _ssd_common(xbc_ref, dt_ref, alog_ref, e_ref, w):
    ll = SSM_L
    xs = xbc_ref[:, 0:w]
    dtv = dt_ref[...]
    a_row = -jnp.exp(alog_ref[...])
    a = dtv * a_row
    row = lax.broadcasted_iota(jnp.int32, (ll, ll), 0)
    col = lax.broadcasted_iota(jnp.int32, (ll, ll), 1)
    tril = (row >= col).astype(F32)
    triu = (row <= col).astype(F32)
    acl = _xdot(tril, a, 3, False)
    acl_t = _xdot(a, triu, 3, True, 0, 0)
    e = e_ref[...]
    aclx = _xdot(acl, e, 3, True)
    dtx = _xdot(dtv, e, 2, True)
    last = aclx[ll - 1:ll, :]
    e_in = jnp.exp(aclx)
    e_end = jnp.exp(last - aclx)
    e_tot = jnp.exp(last)
    x = xs * dtx
    return dict(xs=xs, dtv=dtv, a_row=a_row, a=a, row=row, col=col, triu=triu, acl=acl, acl_t=acl_t, dtx=dtx, e_in=e_in, e_end=e_end,
                e_tot=e_tot, x=x)


def _decay(q, hh):
    diff = q["acl"][:, hh:hh + 1] - q["acl_t"][hh:hh + 1, :]
    return jnp.exp(jnp.where(q["row"] >= q["col"], diff, -jnp.inf))


def ssd_fwd(xbc, dt, z, a_log, d_exp, m_norm, e_mat, rider=None):
    t = xbc.shape[0]
    w = z.shape[1]
    gn = SSM_G * SSM_N
    gw = w // SSM_G
    ll, nn = SSM_L, SSM_N
    nc = t // ll
    cw = xbc.shape[1]

    def body(xbc_ref, dt_ref, z_ref, alog_ref, dexp_ref, mn_ref, e_ref, yn_ref, y_ref, sp_ref, s_sc):
        c = pl.program_id(0)

        @pl.when(c == 0)
        def _():
            s_sc[...] = jnp.zeros_like(s_sc)

        q = _ssd_common(xbc_ref, dt_ref, alog_ref, e_ref, w)
        xb = q["x"].astype(BF)
        xsb = (q["x"] * q["e_end"]).astype(BF)
        sp = s_sc[...]
        sp_ref[0] = sp
        spb = sp.astype(BF)
        lane = lax.broadcasted_iota(jnp.int32, (ll, LANE), 1)
        for g in range(SSM_G):
            lo = g * gw
            bg = xbc_ref[:, w + g * nn:w + (g + 1) * nn].astype(BF)
            cg = xbc_ref[:, w + gn + g * nn:w + gn + (g + 1) * nn].astype(BF)
            yoff = _dot(cg, spb[:, lo:lo + gw]) * q["e_in"][:, lo:lo + gw]
            s_sc[:, lo:lo + gw] = sp[:, lo:lo + gw] * q["e_tot"][:, lo:lo + gw] + _dot(bg, xsb[:, lo:lo + gw], 0, 0)
            cb = _dot(cg, bg, 1, 1)
            for pr in range(gw // LANE):
                l0 = lo + pr * LANE
                xp = xb[:, l0:l0 + LANE]
                ys = []
                for hh in (l0 // SSM_P, l0 // SSM_P + 1):
                    wm = (cb * _decay(q, hh)).astype(BF)
                    ys.append(_dot(wm, xp))
                ydiag = jnp.where(lane < SSM_P, ys[0], ys[1])
                y_ref[:, l0:l0 + LANE] = ydiag + yoff[:, pr * LANE:(pr + 1) * LANE] + dexp_ref[:, l0:l0 + LANE] * q["xs"][:, l0:l0 + LANE]
        zv = z_ref[...].astype(F32)
        yz = y_ref[...] * (zv * _sigmoid(zv))
        for g in range(SSM_G):
            lo = g * gw
            yn_ref[:, lo:lo + gw] = _rms(yz[:, lo:lo + gw], mn_ref[:, lo:lo + gw]).astype(BF)

    vec = lambda s: pl.BlockSpec(s, lambda c: (0, 0))
    return host_call(
        body,
        name="ssd_fwd",
        grid=(nc,),
        in_specs=[
            pl.BlockSpec((ll, cw), lambda c: (c, 0)), pl.BlockSpec((ll, LANE), lambda c: (c, 0)), pl.BlockSpec((ll, w), lambda c: (c, 0)),
            vec((1, LANE)), vec((1, w)), vec((1, w)), vec((LANE, w)),
        ],
        out_specs=[pl.BlockSpec((ll, w), lambda c: (c, 0)), pl.BlockSpec((ll, w), lambda c: (c, 0)), pl.BlockSpec((1, nn, w), lambda c: (c, 0, 0))],
        out_shape=[jax.ShapeDtypeStruct((t, w), BF), jax.ShapeDtypeStruct((t, w), F32), jax.ShapeDtypeStruct((nc, nn, w), F32)],
        scratch_shapes=[pltpu.VMEM((nn, w), F32)],
        operands=(xbc, dt, z, a_log, d_exp, m_norm, e_mat),
        rider=rider,
    )


def ssd_bwd(dyn, y, z, xbc, dt, sprev, a_log, d_exp, m_norm, e_mat, et_mat, window, rider=None):
    t = xbc.shape[0]
    w = z.shape[1]
    gn = SSM_G * SSM_N
    gw = w // SSM_G
    ll, nn = SSM_L, SSM_N
    nc = t // ll
    cw = xbc.shape[1]

    def body(dyn_ref, y_ref, z_ref, xbc_ref, dt_ref, sp_ref, alog_ref, dexp_ref, mn_ref, e_ref, et_ref,
             dz_ref, dxbc_ref, ddt_ref, dmn_ref, dd_ref, dal_ref, ds_sc, dy_sc, dx_sc):
        step = pl.program_id(0)

        @pl.when(step == 0)
        def _():
            ds_sc[...] = jnp.zeros_like(ds_sc)

        zv, yv = z_ref[...].astype(F32), y_ref[...]
        sg = _sigmoid(zv)
        sz = zv * sg
        yz = yv * sz
        dmn = []
        for g in range(SSM_G):
            lo = g * gw
            dseg, dmn_g = _rms_bwd(yz[:, lo:lo + gw], mn_ref[:, lo:lo + gw], dyn_ref[:, lo:lo + gw])
            dy_sc[:, lo:lo + gw] = dseg
            dmn.append(dmn_g)
        dmn = jnp.concatenate(dmn, axis=1)
        dyz = dy_sc[...]
        dz_ref[...] = (dyz * yv * (sg * (1.0 + zv * (1.0 - sg)))).astype(BF)
        dy = dyz * sz

        q = _ssd_common(xbc_ref, dt_ref, alog_ref, e_ref, w)
        x = q["x"]
        xb = x.astype(BF)
        xsb = (x * q["e_end"]).astype(BF)
        sp = sp_ref[0]
        spb = sp.astype(BF)
        dsn = ds_sc[...]
        dsnb = dsn.astype(BF)
        dyb = dy.astype(BF)
        lane = lax.broadcasted_iota(jnp.int32, (ll, LANE), 1)
        lane1 = lax.broadcasted_iota(jnp.int32, (1, LANE), 1)
        sub1 = lax.broadcasted_iota(jnp.int32, (LANE, 1), 0)
        dacl = jnp.zeros((ll, LANE), F32)
        dacl_t = jnp.zeros((LANE, ll), F32)
        d_ein, d_eend, d_etot = [], [], []
        for g in range(SSM_G):
            lo = g * gw
            sl = slice(lo, lo + gw)
            bg = xbc_ref[:, w + g * nn:w + (g + 1) * nn].astype(BF)
            cg = xbc_ref[:, w + gn + g * nn:w + gn + (g + 1) * nn].astype(BF)
            zg = _dot(cg, spb[:, sl])
            dzz = (dy[:, sl] * q["e_in"][:, sl]).astype(BF)
            d_ein.append(dy[:, sl] * zg)
            dcg = _dot(dzz, spb[:, sl], 1, 1)
            ds_sc[:, sl] = _dot(cg, dzz, 0, 0) + dsn[:, sl] * q["e_tot"][:, sl]
            d_etot.append(jnp.sum(dsn[:, sl] * sp[:, sl], axis=0, keepdims=True))
            dbg = _dot(xsb[:, sl], dsnb[:, sl], 1, 1)
            dxs_g = _dot(bg, dsnb[:, sl])
            d_eend.append(dxs_g * x[:, sl])
            cb = _dot(cg, bg, 1, 1)
            dcb = jnp.zeros((ll, ll), F32)
            for pr in range(gw // LANE):
                l0 = lo + pr * LANE
                xp = xb[:, l0:l0 + LANE]
                dyp = dyb[:, l0:l0 + LANE]
                dxp = []
                for hi, hh in enumerate((l0 // SSM_P, l0 // SSM_P + 1)):
                    lm = _decay(q, hh)
                    wm = (cb * lm).astype(BF)
                    in_head = (lane < SSM_P) if hi == 0 else (lane >= SSM_P)
                    dwm = _dot(jnp.where(in_head, dyp, jnp.zeros_like(dyp)), xp, 1, 1)
                    dxp.append(_dot(wm, dyp, 0, 0))
                    dlm = dwm * lm
                    dcb = dcb + dlm
                    dd = dlm * cb
                    dacl = dacl + jnp.sum(dd, axis=1, keepdims=True) * (lane1 == hh).astype(F32)
                    dacl_t = dacl_t + (sub1 == hh).astype(F32) * jnp.sum(dd, axis=0, keepdims=True)
                dx_sc[:, l0:l0 + LANE] = jnp.where(lane < SSM_P, dxp[0], dxp[1]) + dxs_g[:, pr * LANE:(pr + 1) * LANE] * q["e_end"][:, l0:l0 + LANE]
            dcbb = dcb.astype(BF)
            dxbc_ref[:, w + g * nn:w + (g + 1) * nn] = dbg + _dot(dcbb, cg, 0, 0)
            dxbc_ref[:, w + gn + g * nn:w + gn + (g + 1) * nn] = dcg + _dot(dcbb, bg)
        d_ein = jnp.concatenate(d_ein, axis=1) * q["e_in"]
        d_eend = jnp.concatenate(d_eend, axis=1) * q["e_end"]
        d_etot = jnp.concatenate(d_etot, axis=1) * q["e_tot"]
        et = et_ref[...]
        last_add = jnp.sum(d_eend, axis=0, keepdims=True) + d_etot
        last_add = _xdot(jnp.broadcast_to(last_add, (HALO, w)), et, 2, True)[0:1]
        row1 = lax.broadcasted_iota(jnp.int32, (ll, LANE), 0)
        dacl = dacl + _xdot(d_ein - d_eend, et, 2, True) + jnp.where(row1 == ll - 1, last_add, 0.0)
        da = _xdot(q["triu"], dacl, 2, False) - _xdot(q["triu"], dacl_t, 2, False, 1, 1)
        dxv = dx_sc[...]
        dxbc_ref[:, 0:w] = dexp_ref[...] * dy + dxv * q["dtx"]
        ddt_ref[...] = _xdot(dxv * q["xs"], et, 2, True) + da * q["a_row"]
        dal = jnp.sum(da * q["dtv"], axis=0, keepdims=True) * q["a_row"]
        ddv = jnp.sum(dy * q["xs"], axis=0, keepdims=True)
        ddv = _xdot(jnp.broadcast_to(ddv, (HALO, w)), et, 2, True)[0:1]
        _accumulate(dmn_ref, dmn, step == 0)
        _accumulate(dd_ref, ddv, step == 0)
        _accumulate(dal_ref, dal, step == 0)

    rev = lambda c_: pl.BlockSpec((ll, c_), lambda s: (nc - 1 - s, 0))
    vec = lambda s_: pl.BlockSpec(s_, lambda s: (0, 0))
    n_in = 11

    def body_skipping_buffer(*refs):
        body(*refs[:n_in], *refs[n_in + 1:])

    return host_call(
        body_skipping_buffer,
        name="ssd_bwd",
        grid=(nc,),
        in_specs=[
            rev(w), rev(w), rev(w), rev(cw), rev(LANE), pl.BlockSpec((1, nn, w), lambda s: (nc - 1 - s, 0, 0)),
            vec((1, LANE)), vec((1, w)), vec((1, w)), vec((LANE, w)), vec((w, LANE)), ANY,
        ],
        out_specs=[pl.BlockSpec((ll, w), lambda s: (nc - 1 - s, window.block)), rev(cw), rev(LANE), vec((1, w)), vec((1, LANE)), vec((1, LANE))],
        out_shape=[
            jax.ShapeDtypeStruct((t, window.cols), BF), jax.ShapeDtypeStruct((t, cw), F32), jax.ShapeDtypeStruct((t, LANE), F32),
            jax.ShapeDtypeStruct((1, w), F32), jax.ShapeDtypeStruct((1, LANE), F32), jax.ShapeDtypeStruct((1, LANE), F32),
        ],
        scratch_shapes=[pltpu.VMEM((nn, w), F32), pltpu.VMEM((ll, w), F32), pltpu.VMEM((ll, w), F32)],
        operands=(dyn, y, z, xbc, dt, sprev, a_log, d_exp, m_norm, e_mat, et_mat, window.buf),
        rider=rider,
        aliases={n_in: 0},
    )


def _w1024_spec(d, nblk, iblk):
    r = nblk * (d // NCHIP)
    return pl.BlockSpec((NCHIP, r, d), lambda i: (0, iblk // nblk, 0))


def _whole(ref):
    v = ref[...]
    return v.reshape(v.shape[0] * v.shape[1], v.shape[2])


def mix_out_fwd(ya_in, yn, gates, h, w1024, rider=None):
    t, d = h.shape
    tm = _tile(t, 256)

    def body(ya_ref, yn_ref, g_ref, h_ref, wm_ref, wa_ref, wo_ref, ho_ref, oa_ref, om_ref, mg_ref):
        y_a = _dot(ya_ref[...], _whole(wa_ref))
        y_m = _dot(yn_ref[...], _whole(wm_ref))
        oa_ref[...] = y_a
        om_ref[...] = y_m
        gv = g_ref[...].astype(F32)
        mg = (_sigmoid(gv[:, :d]) * y_a + _sigmoid(gv[:, d:]) * y_m).astype(BF)
        mg_ref[...] = mg
        ho_ref[...] = h_ref[...] + _dot(mg, _whole(wo_ref))

    row = lambda c: pl.BlockSpec((tm, c), lambda i: (i, 0))
    return host_call(
        body,
        name="mix_out_fwd",
        grid=(t // tm,),
        in_specs=[row(d), row(2 * d), row(2 * d), row(d), _w1024_spec(d, 2, 0), _w1024_spec(d, 1, 2), _w1024_spec(d, 1, 3)],
        out_specs=[row(d), row(d), row(d), row(d)],
        out_shape=[jax.ShapeDtypeStruct((t, d), F32), jax.ShapeDtypeStruct((t, d), F32), jax.ShapeDtypeStruct((t, d), F32),
                   jax.ShapeDtypeStruct((t, d), BF)],
        scratch_shapes=[],
        operands=(ya_in, yn, gates, h, w1024, w1024, w1024),
        rider=rider,
    )


def mix_out_bwd(dh, gates, y_a, y_m, w1024, cols):
    t, d = dh.shape
    tm = _tile(t, 256)

    def body(dh_ref, g_ref, ya_ref, ym_ref, wm_ref, wa_ref, wo_ref, dg_ref, dya_ref, dyn_ref, da_ref, dm_ref):
        dmg = _dot(dh_ref[...].astype(BF), _whole(wo_ref), 1, 1)
        gv = g_ref[...].astype(F32)
        sa, sm = _sigmoid(gv[:, :d]), _sigmoid(gv[:, d:])
        dg_ref[:, :d] = (dmg * ya_ref[...] * sa * (1.0 - sa)).astype(BF)
        dg_ref[:, d:] = (dmg * ym_ref[...] * sm * (1.0 - sm)).astype(BF)
        da = (dmg * sa).astype(BF)
        dm = (dmg * sm).astype(BF)
        da_ref[...] = da
        dm_ref[...] = dm
        dya_ref[...] = _dot(da, _whole(wa_ref), 1, 1)
        dyn_ref[...] = _dot(dm, _whole(wm_ref), 1, 1)

    row = lambda c: pl.BlockSpec((tm, c), lambda i: (i, 0))
    return pl.pallas_call(
        body,
        name="mix_out_bwd",
        grid=(t // tm,),
        in_specs=[row(d), row(2 * d), row(d), row(d), _w1024_spec(d, 2, 0), _w1024_spec(d, 1, 2), _w1024_spec(d, 1, 3)],
        out_specs=[row(2 * d), row(d), row(2 * d), row(d), row(d)],
        out_shape=[jax.ShapeDtypeStruct((t, cols), BF), jax.ShapeDtypeStruct((t, d), F32), jax.ShapeDtypeStruct((t, 2 * d), F32),
                   jax.ShapeDtypeStruct((t, d), BF), jax.ShapeDtypeStruct((t, d), BF)],
        compiler_params=_params(("parallel",)),
    )(dh, gates, y_a, y_m, w1024, w1024, w1024)


def norm_bwd_add(dh, h, g, dn):
    def fn(i, nt, rows, vecs, prevs, nexts):
        dx, dg = _rms_bwd(rows[1], vecs[0], rows[2])
        return [rows[0] + dx], [dg]
    d = h.shape[1]
    return ew(fn, [dh, h, dn], [g], [(d, F32)], [(1, d)], tm=512, name="norm_bwd_add")


def _pe(p, wpp_ref):
    pb = p.astype(BF)
    return jnp.concatenate([_dot(pb, wpp_ref[k]) for k in range(NCHIP)], axis=1)


def ple_fwd(h, g, p, w1024, wpp):
    t, d = h.shape
    tm = _tile(t, 512)

    def body(h_ref, g_ref, p_ref, wg_ref, wp_ref, ho_ref):
        hv = h_ref[...]
        gate = _sigmoid(_dot(_rms(hv, g_ref[...]).astype(BF), _whole(wg_ref)))
        ho_ref[...] = hv + gate * _pe(p_ref[...], wp_ref)

    row = lambda c: pl.BlockSpec((tm, c), lambda i: (i, 0))
    wpp_spec = pl.BlockSpec(wpp.shape, lambda i: (0, 0, 0))
    return pl.pallas_call(
        body,
        name="ple_fwd",
        grid=(t // tm,),
        in_specs=[row(d), pl.BlockSpec((1, d), lambda i: (0, 0)), row(p.shape[1]), _w1024_spec(d, 1, 4), wpp_spec],
        out_specs=row(d),
        out_shape=jax.ShapeDtypeStruct((t, d), F32),
        compiler_params=_params(("parallel",)),
    )(h, g, p, w1024, wpp)


def ple_bwd(dho, h, g, p, w1024, wpp):
    t, d = h.shape
    tm = _tile(t, 512)

    def body(dho_ref, h_ref, g_ref, p_ref, wg_ref, wp_ref, dh_ref, dg_ref, n_ref, dgp_ref, dpe_ref):
        hv, dv = h_ref[...], dho_ref[...]
        n = _rms(hv, g_ref[...]).astype(BF)
        n_ref[...] = n
        wg = _whole(wg_ref)
        gate = _sigmoid(_dot(n, wg))
        pe = _pe(p_ref[...], wp_ref)
        dpe_ref[...] = (dv * gate).astype(BF)
        dgp = (dv * pe * gate * (1.0 - gate)).astype(BF)
        dgp_ref[...] = dgp
        dx, dg = _rms_bwd(hv, g_ref[...], _dot(dgp, wg, 1, 1))
        dh_ref[...] = dv + dx
        _accumulate(dg_ref, dg, pl.program_id(0) == 0)

    row = lambda c: pl.BlockSpec((tm, c), lambda i: (i, 0))
    wpp_spec = pl.BlockSpec(wpp.shape, lambda i: (0, 0, 0))
    return pl.pallas_call(
        body,
        name="ple_bwd",
        grid=(t // tm,),
        in_specs=[row(d), row(d), pl.BlockSpec((1, d), lambda i: (0, 0)), row(p.shape[1]), _w1024_spec(d, 1, 4), wpp_spec],
        out_specs=[row(d), pl.BlockSpec((1, d), lambda i: (0, 0)), row(d), row(d), row(d)],
        out_shape=[jax.ShapeDtypeStruct((t, d), F32), jax.ShapeDtypeStruct((1, d), F32), jax.ShapeDtypeStruct((t, d), BF),
                   jax.ShapeDtypeStruct((t, d), BF), jax.ShapeDtypeStruct((t, d), BF)],
        compiler_params=_params(("arbitrary",)),
    )(dho, h, g, p, w1024, wpp)


def loss_bwd(h, g, target):
    d = h.shape[1]

    def fn(i, nt, rows, vecs, prevs, nexts):
        err = _rms(rows[0], vecs[0]) - rows[1]
        dx, dg = _rms_bwd(rows[0], vecs[0], err * (1.0 / d))
        return [dx], [jnp.sum(err * err, axis=0, keepdims=True) * (0.5 / d), dg]

    return ew(fn, [h, target], [g], [(d, F32)], [(1, d), (1, d)], tm=512, name="loss_bwd")


def adamw(w, g, m, v, name):
    c1, c2 = 1.0 / (1.0 - ADAM_B1 ** ADAM_STEP), 1.0 / (1.0 - ADAM_B2 ** ADAM_STEP)

    def fn(i, nt, rows, vecs, prevs, nexts):
        wv, gv, mv, vv = rows
        mn = ADAM_B1 * mv + (1.0 - ADAM_B1) * gv
        vn = ADAM_B2 * vv + (1.0 - ADAM_B2) * (gv * gv)
        delta = -ADAM_LR * ((mn * c1) / (jnp.sqrt(vn * c2) + ADAM_EPS) + ADAM_WD * wv)
        return [delta, mn, vn], []

    c = w.shape[1]
    return ew(fn, [w, g, m, v], [], [(c, F32)] * 3, tm=_row_tile(w.shape[0], c, HALO), name=name)


def _place():
    return lax.axis_index("x"), lax.axis_index("y"), lax.axis_index("c")


def _other_chips(x, y):
    return [(1 - x, y), (x, 1 - y), (1 - x, 1 - y)]


ANY = pl.BlockSpec(memory_space=pl.ANY)


def _comm_call(body, name, ins, out_shapes, n_sems, aliases=None):
    return pl.pallas_call(
        body,
        name=name,
        in_specs=[ANY] * len(ins),
        out_specs=[ANY] * len(out_shapes),
        out_shape=out_shapes,
        scratch_shapes=[pltpu.SemaphoreType.DMA((n_sems,)), pltpu.SemaphoreType.DMA((n_sems,))],
        input_output_aliases=aliases or {},
    )(*ins)


def gather_rider(packs):
    nt = len(packs)

    def pieces(ins, outs, send_sems, recv_sems):
        x, y, cc = _place()
        chips = _other_chips(x, y)
        sibling = (x, y, 1 - cc)
        k_me = 2 * x + y

        def copy(k, src, dst, to):
            return pltpu.make_async_remote_copy(src_ref=src, dst_ref=dst, send_sem=send_sems.at[k], recv_sem=recv_sems.at[k],
                                                device_id=to, device_id_type=MESH)

        sends, forwards, arrivals = [], [], []
        for ti in range(nt):
            for j, (px, py) in enumerate(chips):
                sends.append(copy(7 * ti + j, ins[ti].at[cc], outs[ti].at[k_me, cc], (px, py, cc)))
                landed = outs[ti].at[2 * px + py, cc]
                forwards.append((copy(7 * ti + j, landed, landed, (px, py, cc)), copy(7 * ti + 3 + j, landed, landed, sibling)))
                passed = outs[ti].at[2 * px + py, 1 - cc]
                arrivals.append(copy(7 * ti + 3 + j, passed, passed, sibling))
            sends.append(copy(7 * ti + 6, ins[ti], outs[ti].at[k_me], sibling))
            own = outs[ti].at[k_me]
            arrivals.append(copy(7 * ti + 6, own, own, sibling))
        return sends, forwards, arrivals

    def start(*parts):
        for cp in pieces(*parts)[0]:
            cp.start()

    def mid(*parts):
        for landed, forward in pieces(*parts)[1]:
            landed.wait_recv()
            forward.start()

    def finish(*parts):
        sends, forwards, arrivals = pieces(*parts)
        for cp in arrivals:
            cp.wait_recv()
        for cp in sends + [f for _, f in forwards]:
            cp.wait_send()

    return Rider(packs, [jax.ShapeDtypeStruct((NCHIP,) + p.shape, p.dtype) for p in packs], 7 * nt, start, finish, mid)


def swap_packs(gs, name):
    nt = len(gs)
    hl = gs[0].shape[1] // 2

    def body(*refs):
        ins, outs, (send_sems, recv_sems) = refs[:nt], refs[nt:2 * nt], refs[2 * nt:]
        x, y, cc = _place()
        theirs = pl.ds((1 - cc) * hl, hl)
        cps = [pltpu.make_async_remote_copy(src_ref=ins[ti].at[:, theirs], dst_ref=outs[ti], send_sem=send_sems.at[ti], recv_sem=recv_sems.at[ti],
                                            device_id=(x, y, 1 - cc), device_id_type=MESH) for ti in range(nt)]
        for cp in cps:
            cp.start()
        for cp in cps:
            cp.wait()

    return _comm_call(body, name, gs, [jax.ShapeDtypeStruct((NCHIP, hl) + g.shape[2:], g.dtype) for g in gs], nt)


def scatter_packs(cs, name):
    return scatter_rider(cs).standalone(name)


def scatter_rider(cs):
    nt = len(cs)

    def copies(ins, outs, send_sems, recv_sems):
        x, y, cc = _place()
        cps = []
        for ti in range(nt):
            for j, (px, py) in enumerate(_other_chips(x, y)):
                cps.append(pltpu.make_async_remote_copy(src_ref=ins[ti].at[2 * px + py], dst_ref=outs[ti].at[j], send_sem=send_sems.at[3 * ti + j],
                                                        recv_sem=recv_sems.at[3 * ti + j], device_id=(px, py, cc), device_id_type=MESH))
        return cps

    def start(*parts):
        for cp in copies(*parts):
            cp.start()

    def finish(*parts):
        for cp in copies(*parts):
            cp.wait()

    return Rider(cs, [jax.ShapeDtypeStruct((3,) + c_.shape[1:], c_.dtype) for c_ in cs], 3 * nt, start, finish)


def join_packs(fulls, name):
    nt = len(fulls)
    hl = fulls[0].shape[0] // 2

    def body(*refs):
        ins, outs, (send_sems, recv_sems) = refs[:nt], refs[nt:2 * nt], refs[2 * nt:]
        x, y, cc = _place()
        mine = pl.ds(cc * hl, hl)
        cps = [pltpu.make_async_remote_copy(src_ref=ins[ti].at[mine], dst_ref=outs[ti].at[mine], send_sem=send_sems.at[ti], recv_sem=recv_sems.at[ti],
                                            device_id=(x, y, 1 - cc), device_id_type=MESH) for ti in range(nt)]
        for cp in cps:
            cp.start()
        for cp in cps:
            cp.wait()

    return _comm_call(body, name, fulls, [jax.ShapeDtypeStruct(f.shape, f.dtype) for f in fulls], nt, aliases={ti: ti for ti in range(nt)})


def add_sibling(g, recv, name):
    _, nl, r, c = g.shape
    hl = nl // 2
    tm, tc = _tile2(r, c)

    def body(g_ref, r_ref, o_ref):
        o_ref[...] = (g_ref[...].astype(F32) + r_ref[...].astype(F32)).astype(o_ref.dtype)

    blk = (None, None, tm, tc)
    return pl.pallas_call(
        body,
        name=name,
        grid=(NCHIP, hl, r // tm, c // tc),
        in_specs=[pl.BlockSpec(blk, lambda k, l, i, j: (k, lax.axis_index("c") * hl + l, i, j)), pl.BlockSpec(blk, lambda k, l, i, j: (k, l, i, j))],
        out_specs=pl.BlockSpec(blk, lambda k, l, i, j: (k, l, i, j)),
        out_shape=jax.ShapeDtypeStruct(recv.shape, BF),
        compiler_params=_params(("parallel",) * 4),
    )(g, recv)


def add_chips(cs, got, nl, name):
    _, hl, r, c = cs.shape
    tm, tc = _tile2(r, c)

    def body(own_ref, got_ref, o_ref):
        o_ref[...] = own_ref[...].astype(F32) + got_ref[0].astype(F32) + got_ref[1].astype(F32) + got_ref[2].astype(F32)

    return pl.pallas_call(
        body,
        name=name,
        grid=(hl, r // tm, c // tc),
        in_specs=[pl.BlockSpec((None, None, tm, tc), lambda l, i, j: (2 * lax.axis_index("x") + lax.axis_index("y"), l, i, j)),
                  pl.BlockSpec((3, None, tm, tc), lambda l, i, j: (0, l, i, j))],
        out_specs=pl.BlockSpec((None, tm, tc), lambda l, i, j: (lax.axis_index("c") * hl + l, i, j)),
        out_shape=jax.ShapeDtypeStruct((nl, r, c), F32),
        compiler_params=_params(("parallel",) * 3),
    )(cs, got)


def all_gather_xy(shard, name):
    r, c = shard.shape
    hr = r // 2
    assert r % 32 == 0

    def body(x_ref, out_ref, send_sems, recv_sems, local_sem):
        x, y, cc = _place()
        chips = _other_chips(x, y)
        mine = pl.ds(pl.multiple_of(cc * hr, 16), hr)
        theirs = pl.ds(pl.multiple_of((1 - cc) * hr, 16), hr)
        k_me = 2 * x + y

        def copy(k, src, dst, to):
            return pltpu.make_async_remote_copy(src_ref=src, dst_ref=dst, send_sem=send_sems.at[k], recv_sem=recv_sems.at[k],
                                                device_id=to, device_id_type=MESH)

        own = pltpu.make_async_copy(x_ref, out_ref.at[k_me], local_sem)
        own.start()
        first = [copy(j, x_ref.at[mine], out_ref.at[k_me, mine], (*chip, cc)) for j, chip in enumerate(chips)]
        for cp in first:
            cp.start()
        passed = []
        for j, (px, py) in enumerate(chips):
            landed = out_ref.at[2 * px + py, mine]
            copy(j, landed, landed, (px, py, cc)).wait_recv()
            fw = copy(3 + j, landed, landed, (x, y, 1 - cc))
            fw.start()
            passed.append(fw)
        for j, (px, py) in enumerate(chips):
            landed = out_ref.at[2 * px + py, theirs]
            copy(3 + j, landed, landed, (x, y, 1 - cc)).wait_recv()
        for cp in first + passed:
            cp.wait_send()
        own.wait()

    return pl.pallas_call(
        body,
        name=name,
        in_specs=[ANY],
        out_specs=ANY,
        out_shape=jax.ShapeDtypeStruct((NCHIP, r, c), shard.dtype),
        scratch_shapes=[pltpu.SemaphoreType.DMA((6,)), pltpu.SemaphoreType.DMA((6,)), pltpu.SemaphoreType.DMA],
    )(shard)


def all_gather_8(block, name):
    m, c = block.shape

    def body(x_ref, out_ref, send_sems, recv_sems, local_sem):
        x, y, cc = _place()
        me, sibling = (x, y, cc), (x, y, 1 - cc)
        chips = _other_chips(x, y)

        def rows(px, py, pc):
            return out_ref.at[4 * px + 2 * py + pc]

        def copy(k, blk, to, src=None):
            return pltpu.make_async_remote_copy(src_ref=rows(*blk) if src is None else src, dst_ref=rows(*blk), send_sem=send_sems.at[k],
                                                recv_sem=recv_sems.at[k], device_id=to, device_id_type=MESH)

        mine = pltpu.make_async_copy(x_ref, rows(*me), local_sem)
        mine.start()
        first = [copy(0, me, sibling, src=x_ref)]
        first += [copy(1 + j, me, (*chip, cc), src=x_ref) for j, chip in enumerate(chips)]
        for cp in first:
            cp.start()
        passed = [copy(4 + j, (*chip, cc), sibling) for j, chip in enumerate(chips)]
        for j, chip in enumerate(chips):
            copy(1 + j, (*chip, cc), me).wait_recv()
            passed[j].start()
        copy(0, sibling, me).wait_recv()
        for j, chip in enumerate(chips):
            copy(4 + j, (*chip, 1 - cc), me).wait_recv()
        for cp in first + passed:
            cp.wait_send()
        mine.wait()

    return pl.pallas_call(
        body,
        name=name,
        in_specs=[pl.BlockSpec(memory_space=pltpu.VMEM)],
        out_specs=pl.BlockSpec(memory_space=pltpu.VMEM),
        out_shape=jax.ShapeDtypeStruct((8, m, c), block.dtype),
        scratch_shapes=[pltpu.SemaphoreType.DMA((7,)), pltpu.SemaphoreType.DMA((7,)), pltpu.SemaphoreType.DMA],
        compiler_params=pltpu.CompilerParams(vmem_limit_bytes=VMEM_LIMIT),
    )(block)


def add_parts(parts, out_dtype, name, tm=512):
    def fn(i, nt, rows, vecs, prevs, nexts):
        acc = rows[0]
        for r_ in rows[1:]:
            acc = acc + r_
        return [acc], []
    r, c = parts[0].shape
    return ew(fn, list(parts), [], [(c, out_dtype)], tm=_tile(r, tm, 16), name=name)[0]


SMALL_SHARDED = ("sc_conv_w", "m_conv_w")
SMALL_REPL = ("ffn1_norm", "mix_norm", "m_conv_b", "m_dt_bias", "m_A_log", "m_D", "m_norm", "ffn2_norm", "ple_norm", "final_norm")
BIG = ("ffn1_wg", "ffn1_wu", "ffn1_wd", "w_in", "sc_w_out", "m_w_out", "w_o", "ffn2_wg", "ffn2_wu", "ffn2_wd", "ple_w_gate", "ple_w_proj")
TRANSPOSED = ("ffn1_wg", "ffn1_wu", "ffn2_wg", "ffn2_wu", "w_in")
ORDER = ("ffn1_norm", "ffn1_wg", "ffn1_wu", "ffn1_wd", "mix_norm", "w_in", "sc_conv_w", "sc_w_out", "m_conv_w", "m_conv_b", "m_dt_bias",
         "m_A_log", "m_D", "m_norm", "m_w_out", "w_o", "ffn2_norm", "ffn2_wg", "ffn2_wu", "ffn2_wd", "ple_norm", "ple_w_gate", "ple_w_proj",
         "final_norm")


def _pack(arrs, cols, row_mult):
    flat = jnp.concatenate([a.reshape(-1) for a in arrs])
    n = flat.shape[0]
    rows = -(-n // cols)
    rows = -(-rows // row_mult) * row_mult
    return jnp.pad(flat, (0, rows * cols - n)).reshape(rows, cols)


def _unpack(flat2d, shapes):
    flat = flat2d.reshape(-1)
    out, off = [], 0
    for s in shapes:
        n = int(np.prod(s))
        out.append(flat[off:off + n].reshape(s))
        off += n
    return out


def _row_cat(arrs, dtype):
    return jnp.concatenate([a.astype(dtype) for a in arrs], axis=1)


def kernel(x, p, ffn1_norm, ffn1_wg, ffn1_wu, ffn1_wd, mix_norm, w_in, sc_conv_w, sc_w_out, m_conv_w, m_conv_b, m_dt_bias, m_A_log, m_D, m_norm, m_w_out, w_o, ffn2_norm, ffn2_wg, ffn2_wu, ffn2_wd, ple_norm, ple_w_gate, ple_w_proj, final_norm, loss_target, m_ffn1_norm, m_ffn1_wg, m_ffn1_wu, m_ffn1_wd, m_mix_norm, m_w_in, m_sc_conv_w, m_sc_w_out, m_m_conv_w, m_m_conv_b, m_m_dt_bias, m_m_A_log, m_m_D, m_m_norm, m_m_w_out, m_w_o, m_ffn2_norm, m_ffn2_wg, m_ffn2_wu, m_ffn2_wd, m_ple_norm, m_ple_w_gate, m_ple_w_proj, m_final_norm, v_ffn1_norm, v_ffn1_wg, v_ffn1_wu, v_ffn1_wd, v_mix_norm, v_w_in, v_sc_conv_w, v_sc_w_out, v_m_conv_w, v_m_conv_b, v_m_dt_bias, v_m_A_log, v_m_D, v_m_norm, v_m_w_out, v_w_o, v_ffn2_norm, v_ffn2_wg, v_ffn2_wu, v_ffn2_wd, v_ple_norm, v_ple_w_gate, v_ple_w_proj, v_final_norm):
    args = dict(locals())
    wts = {n: args[n] for n in ORDER}
    mom = {n: args["m_" + n] for n in ORDER}
    vel = {n: args["v_" + n] for n in ORDER}

    depth = ffn1_norm.shape[0]
    d = x.shape[-1]
    w = 2 * d
    hh = w // SSM_P
    cw = w + 2 * SSM_G * SSM_N
    d4 = d // NCHIP
    pp = 7 * d + cw + LANE
    my_x, my_y, my_c = _place()
    k_me = 2 * my_x + my_y

    tr = lambda a: jnp.swapaxes(a, 1, 2)
    gu_t = [_row_cat([tr(wg_), tr(wu_)], BF) for wg_, wu_ in ((ffn1_wg, ffn1_wu), (ffn2_wg, ffn2_wu))]
    wd_l = [ffn1_wd.astype(BF), ffn2_wd.astype(BF)]
    w1024_l = _row_cat([m_w_out, sc_w_out, w_o, ple_w_gate], BF)
    win_l, wpp_l = tr(w_in).astype(BF), ple_w_proj.astype(BF)
    halves = lambda a: a.reshape(2, a.shape[0] // 2, a.shape[1])
    whole = lambda g: g.reshape(NCHIP, g.shape[2] * 2, g.shape[3])

    def pieces(l):
        return {"small": [halves(w1024_l[l]), halves(wpp_l[l])], "win": [halves(win_l[l])], "gu1": [halves(gu_t[0][l])], "d1": [halves(wd_l[0][l])],
                "gu2": [halves(gu_t[1][l])], "d2": [halves(wd_l[1][l])]}

    small_local = [sc_conv_w, m_conv_w]
    gathered_s = all_gather_xy(_pack(small_local, LANE, 32), "gather_conv_weights")
    per_shard_s = [_unpack(gathered_s[k], [a.shape for a in small_local]) for k in range(NCHIP)]
    sc_conv_full = jnp.concatenate([per_shard_s[k][0] for k in range(NCHIP)], axis=2)
    m_conv_full = jnp.concatenate([per_shard_s[k][1] for k in range(NCHIP)], axis=2)

    pad_h = lambda a: jnp.pad(a, ((0, 0), (0, LANE - hh)))
    dt_bias_p, a_log_p = pad_h(m_dt_bias), pad_h(m_A_log)
    d_exp = jnp.repeat(m_D, SSM_P, axis=1)
    e_mat = (jnp.arange(w)[None, :] // SSM_P == jnp.arange(LANE)[:, None]).astype(F32)
    et_mat = e_mat.T
    o_z, o_xbc, o_dt, o_g = 3 * d, 5 * d, 5 * d + cw, 5 * d + cw + hh

    def layer_weights(got):
        wt = {"w1024": whole(got["small"][0]), "wpp": whole(got["small"][1])}
        wt.update({k: whole(got[k][0]) for k in ("gu1", "d1", "gu2", "d2")})
        wi = whole(got["win"][0]).reshape(-1, d)
        wt["sc3"], wt["z"], wt["xbc"], wt["g2"] = wi[:o_z], wi[o_z:o_xbc], wi[o_xbc:o_dt], wi[o_g:o_g + 2 * d]
        wt["dt"] = jnp.pad(wi[o_dt:o_g], ((0, LANE - hh), (0, 0)))
        wt["in_p"] = jnp.concatenate([wt["g2"], wt["z"], wt["sc3"][d:], wt["xbc"], wt["sc3"][:d], wt["dt"]], axis=0)
        return wt

    first = pieces(0)
    order = ("gu1", "d1", "win", "small", "gu2", "d2")
    flat = gather_rider([a for k in order for a in first[k]]).standalone("gather_weights")
    got, pos = {}, 0
    for k in order:
        got[k] = flat[pos:pos + len(first[k])]
        pos += len(first[k])
    wts_l = [layer_weights(got)]

    h = x[0]
    saved = []
    for i in range(depth):
        s, wt = {}, wts_l[i]
        nxt = pieces(i + 1) if i + 1 < depth else None
        ride = lambda k: gather_rider(nxt[k]) if nxt else None
        got = {}
        s["h0"] = h
        (s["ab1"], s4, s["n1"]), got["small"] = ffn_up(h, ffn1_norm[i:i + 1], wt["gu1"], rider=ride("small"))
        h, got["d1"] = ffn_down(s4, wt["d1"], h, rider=ride("d1"))
        s["h1"] = h
        u = norm_cast(h, mix_norm[i:i + 1])
        s["u"] = u
        s["sc3"] = mm(u, wt["sc3"], tb=True, out_dtype=BF, name="proj_sc")
        s["z"] = mm(u, wt["z"], tb=True, out_dtype=BF, name="proj_z")
        s["xbc_raw"] = mm(u, wt["xbc"], tb=True, out_dtype=BF, name="proj_xbc")
        s["gates"] = mm(u, wt["g2"], tb=True, out_dtype=BF, name="proj_gates")
        s["dt_raw"] = mm(u, wt["dt"], tb=True, name="proj_dt")
        s["ya_in"] = conv_a_fwd(s["sc3"], sc_conv_full[i])
        s["xbc"], s["dt"] = conv_m_fwd(s["xbc_raw"], s["dt_raw"], m_conv_full[i], m_conv_b[i:i + 1], dt_bias_p[i:i + 1])
        (s["yn"], s["y"], s["sprev"]), got["win"] = ssd_fwd(s["xbc"], s["dt"], s["z"], a_log_p[i:i + 1], d_exp[i:i + 1], m_norm[i:i + 1], e_mat,
                                                            rider=ride("win"))
        (h, s["y_a"], s["y_m"], s["merged"]), got["gu2"] = mix_out_fwd(s["ya_in"], s["yn"], s["gates"], h, wt["w1024"], rider=ride("gu2"))
        s["h2"] = h
        (s["ab2"], s4, s["n2"]), got["gu1"] = ffn_up(h, ffn2_norm[i:i + 1], wt["gu2"], rider=ride("gu1"))
        h, got["d2"] = ffn_down(s4, wt["d2"], h, rider=ride("d2"))
        s["h3"] = h
        h = ple_fwd(h, ple_norm[i:i + 1], p[i, 0], wt["w1024"], wt["wpp"])
        saved.append(s)
        if nxt:
            wts_l.append(layer_weights(got))

    dh, loss_lanes, g_final = loss_bwd(h, final_norm[None, :], loss_target[0])
    loss = lax.psum(jnp.sum(loss_lanes), ("x", "y", "c"))

    def finish_reduce(cs, got):
        halves = [add_chips(c_, g_, 2, "grad_add_chips") for c_, g_ in zip(cs, got, strict=True)]
        return [f.reshape(-1, f.shape[2]) for f in join_packs(halves, "grad_join_halves")]

    pending, reduced = None, [None] * depth
    gs = {n: [None] * depth for n in SMALL_SHARDED + SMALL_REPL if n != "final_norm"}
    for i in reversed(range(depth)):
        s = saved[i]
        wt = wts_l[i]
        dh, gs["ple_norm"][i], n3, dgp, dpe = ple_bwd(dh, s["h3"], ple_norm[i:i + 1], p[i, 0], wt["w1024"], wt["wpp"])
        g_pg = mm(n3, dgp, ta=True, out_dtype=BF, name="g_ple_gate", tm_cap=512, tn_cap=512)
        g_pp = mm(p[i, 0], dpe, ta=True, out_dtype=BF, name="g_ple_proj", tm_cap=512, tn_cap=512)
        g_pp = jnp.transpose(g_pp.reshape(g_pp.shape[0], NCHIP, d4), (1, 0, 2))
        dn2, s2, dab2 = ffn_bwd(dh, s["ab2"], wt["gu2"], wt["d2"])
        g_ffn2 = ffn_wgrads(s["n2"], dh, s2, dab2)
        dh, gs["ffn2_norm"][i] = norm_bwd_add(dh, s["h2"], ffn2_norm[i:i + 1], dn2)
        dproj, dya, dyn, dy_a, dy_m = mix_out_bwd(dh, s["gates"], s["y_a"], s["y_m"], wt["w1024"], pp)
        g_wo = mm(s["merged"], dh, ta=True, out_dtype=BF, name="g_w_o", tm_cap=512, tn_cap=512)
        g_sco = mm(s["ya_in"], dy_a, ta=True, out_dtype=BF, name="g_sc_out", tm_cap=512, tn_cap=512)
        g_mo = mm(s["yn"], dy_m, ta=True, out_dtype=BF, name="g_m_out", tm_cap=512, tn_cap=512)
        g_1024 = jnp.concatenate([g_mo.reshape(NCHIP, 2 * d4, d), g_sco.reshape(NCHIP, d4, d), g_wo.reshape(NCHIP, d4, d),
                                  g_pg.reshape(NCHIP, d4, d)], axis=1)
        (dproj, dxbc, ddt, gs["m_norm"][i], gd, gal), got_a = ssd_bwd(dyn, s["y"], s["z"], s["xbc"], s["dt"], s["sprev"], a_log_p[i:i + 1], d_exp[i:i + 1],
                                                                      m_norm[i:i + 1], e_mat, et_mat, Window(0, 1, pp, dproj),
                                                                      rider=scatter_rider(pending[:1]) if pending else None)
        gs["m_D"][i], gs["m_A_log"][i] = gd[:, :hh], gal[:, :hh]
        dpre, dproj, gdb = conv_m_bwd1(dxbc, s["xbc_raw"], ddt, s["dt_raw"], m_conv_full[i], m_conv_b[i:i + 1], dt_bias_p[i:i + 1],
                                       Window(1, (7 * d + cw) // LANE, pp, dproj))
        gs["m_dt_bias"][i] = gdb[:, :hh]
        dproj, gs["m_conv_w"][i], gs["m_conv_b"][i] = conv_bwd2(dpre, s["xbc_raw"], m_conv_full[i], "conv_m_bwd2", Window(0, 6 * d // cw, pp, dproj))
        dcv, dproj, v = conv_a_bwd1(dya, s["sc3"], sc_conv_full[i], Window(1, (6 * d + cw) // d, pp, dproj))
        dproj, gs["sc_conv_w"][i] = conv_a_bwd2(dcv, v, s["sc3"], sc_conv_full[i], Window(0, 2, pp, dproj))
        if pending:
            du, got_b = mm(dproj, wt["in_p"], name="d_proj_in", tn_cap=512, rider=scatter_rider(pending[2:3]))
            gwp, got_c = mm(dproj, s["u"], ta=True, out_dtype=BF, name="g_w_in", tm_cap=1152, tn_cap=512, rider=scatter_rider(pending[1:2] + pending[3:]))
            reduced[i + 1] = finish_reduce(pending, [got_a[0], got_c[0], got_b[0], got_c[1]])
        else:
            du = mm(dproj, wt["in_p"], name="d_proj_in", tn_cap=512)
            gwp = mm(dproj, s["u"], ta=True, out_dtype=BF, name="g_w_in", tm_cap=1152, tn_cap=512)
        gw_rows = jnp.concatenate([gwp[6 * d + cw:7 * d + cw], gwp[4 * d:6 * d], gwp[2 * d:4 * d], gwp[6 * d:6 * d + cw], gwp[7 * d + cw:7 * d + cw + hh],
                                   gwp[:2 * d]], axis=0)
        g_in = gw_rows.reshape(NCHIP, -1, d)
        dh, gs["mix_norm"][i] = norm_bwd_add(dh, s["h1"], mix_norm[i:i + 1], du)
        dn1, s1, dab1 = ffn_bwd(dh, s["ab1"], wt["gu1"], wt["d1"])
        g_ffn1 = ffn_wgrads(s["n1"], dh, s1, dab1)
        dh, gs["ffn1_norm"][i] = norm_bwd_add(dh, s["h0"], ffn1_norm[i:i + 1], dn1)
        g_layer = [jnp.concatenate([g_ffn1, g_ffn2], axis=1), g_1024, g_in, g_pp]
        g_layer = [g.reshape(NCHIP, 2, g.shape[1] // 2, g.shape[2]) for g in g_layer]
        from_sibling = swap_packs(g_layer, "grad_swap_halves")
        pending = [add_sibling(g, r_, "grad_add_sibling") for g, r_ in zip(g_layer, from_sibling, strict=True)]
    reduced[0] = finish_reduce(pending, scatter_packs(pending, "grad_scatter"))
    grad_x = dh[None]

    f4 = reduced[0][0].shape[0] // 6
    rows_of = lambda j, lo, hi: jnp.stack([reduced[l][j][lo:hi] for l in range(depth)])
    ffn_rows = lambda j: rows_of(0, j * f4, (j + 1) * f4)
    grads = {
        "ffn1_wg": ffn_rows(0), "ffn1_wu": ffn_rows(1), "ffn1_wd": ffn_rows(2), "ffn2_wg": ffn_rows(3), "ffn2_wu": ffn_rows(4), "ffn2_wd": ffn_rows(5),
        "m_w_out": rows_of(1, 0, 2 * d4), "sc_w_out": rows_of(1, 2 * d4, 3 * d4), "w_o": rows_of(1, 3 * d4, 4 * d4), "ple_w_gate": rows_of(1, 4 * d4, 5 * d4),
        "w_in": rows_of(2, 0, None), "ple_w_proj": rows_of(3, 0, None),
    }

    small_names = list(SMALL_SHARDED + SMALL_REPL)
    small_full = [g_final[0] if n == "final_norm" else jnp.stack(gs[n]) for n in small_names]
    small_pack = _pack(small_full, LANE, HALO)
    all8 = all_gather_8(small_pack, "gather_small_grads")
    small_sum = add_parts([all8[k] for k in range(8)], F32, "add_small_grads", tm=256)
    for n, tot in zip(small_names, _unpack(small_sum, [a.shape for a in small_full]), strict=True):
        if n in SMALL_SHARDED:
            cl = wts[n].shape[2]
            grads[n] = lax.dynamic_slice_in_dim(tot, k_me * cl, cl, axis=2)
        else:
            grads[n] = tot.reshape(wts[n].shape)

    delta, new_m, new_v = {}, {}, {}
    for n in BIG:
        view = tr if n in TRANSPOSED else (lambda a: a)
        shp = grads[n].shape
        two = lambda a: a.reshape(-1, shp[-1])
        dl, nm, nv = adamw(two(view(wts[n])), two(grads[n]), two(view(mom[n])), two(view(vel[n])), "adamw_" + "x".join(map(str, shp[1:])))
        grads[n], delta[n], new_m[n], new_v[n] = view(grads[n]), view(dl.reshape(shp)), view(nm.reshape(shp)), view(nv.reshape(shp))
    for n in small_names:
        shp = wts[n].shape
        two = lambda a: a.reshape(-1, shp[-1])
        dl, nm, nv = adamw(two(wts[n]), two(grads[n]), two(mom[n]), two(vel[n]), "adamw_small_" + "x".join(map(str, shp)))
        delta[n], new_m[n], new_v[n] = dl.reshape(shp), nm.reshape(shp), nv.reshape(shp)

    return (loss, grad_x, *[grads[n] for n in ORDER], *[delta[n] for n in ORDER], *[new_m[n] for n in ORDER], *[new_v[n] for n in ORDER])
```

```python
import jax
import jax.numpy as jnp
import numpy as np
from jax import lax
from jax.experimental import pallas as pl
from jax.experimental.pallas import tpu as pltpu

BF = jnp.bfloat16
F32 = jnp.float32
EPS = 1e-6
LANE = 128
HALO = 8
SSM_P = 64
SSM_N = 128
SSM_G = 4
SSM_L = 128
ADAM_LR, ADAM_B1, ADAM_B2, ADAM_EPS, ADAM_WD, ADAM_STEP = 0.001, 0.9, 0.999, 1e-08, 0.01, 10
VMEM_LIMIT = 56 * 1024 * 1024
TILE_ELEMS = 400_000
NCHIP = 4
FFN_SUB = 256
MESH = pl.DeviceIdType.MESH
HI = lax.Precision.HIGHEST


def _tile(n, cap, mult=LANE):
    best = None
    t = mult
    while t <= min(n, cap):
        if n % t == 0:
            best = t
        t += mult
    return best if best is not None else n


def _row_tile(r, c, mult=16):
    return _tile(r, max(mult, TILE_ELEMS // c // mult * mult), mult)


def _tile2(r, c, mult=16):
    tm = _row_tile(r, c, mult)
    tc = c if tm * c <= TILE_ELEMS else _tile(c, max(LANE, TILE_ELEMS // tm // LANE * LANE))
    return tm, tc


def _params(sem):
    return pltpu.CompilerParams(dimension_semantics=sem, vmem_limit_bytes=VMEM_LIMIT)


def _sigmoid(x):
    return 1.0 / (1.0 + jnp.exp(-x))


def _dot(a, b, ca=1, cb=0, precision=None):
    return lax.dot_general(a, b, (((ca,), (cb,)), ((), ())), precision=precision, preferred_element_type=F32)


def _rms(x, g):
    r = lax.rsqrt(jnp.mean(x * x, axis=-1, keepdims=True) + EPS)
    return x * r * g


def _rms_bwd(x, g, dy):
    r = lax.rsqrt(jnp.mean(x * x, axis=-1, keepdims=True) + EPS)
    xh = x * r
    dxh = dy * g
    dx = r * (dxh - xh * jnp.mean(dxh * xh, axis=-1, keepdims=True))
    return dx, jnp.sum(dy * xh, axis=0, keepdims=True)


def _accumulate(ref, val, first):
    @pl.when(first)
    def _():
        ref[...] = val

    @pl.when(jnp.logical_not(first))
    def _():
        ref[...] += val


RIDER_MID = 1.0


class Rider:
    def __init__(self, ins, out_shapes, n_sems, start, finish, mid=None):
        self.ins, self.out_shapes, self.n_sems, self.start, self.mid, self.finish = list(ins), list(out_shapes), n_sems, start, mid, finish

    def standalone(self, name):
        ni, no = len(self.ins), len(self.out_shapes)

        def body(*refs):
            parts = (refs[:ni], refs[ni:ni + no], *refs[ni + no:])
            self.start(*parts)
            if self.mid is not None:
                self.mid(*parts)
            self.finish(*parts)

        return _comm_call(body, name, self.ins, self.out_shapes, self.n_sems)


def host_call(body, *, name, grid, in_specs, out_specs, out_shape, scratch_shapes, operands, rider=None, aliases=None):
    n_in, n_out = len(in_specs), len(out_specs)
    aliases = aliases or {}
    if rider is None:
        outs = pl.pallas_call(body, name=name, grid=grid, in_specs=in_specs, out_specs=out_specs, out_shape=out_shape, scratch_shapes=scratch_shapes,
                              input_output_aliases=aliases, compiler_params=_params(("arbitrary",) * len(grid)))(*operands)
        return list(outs), []
    ri, ro = len(rider.ins), len(rider.out_shapes)

    def hosted(*refs):
        ins, r_ins = refs[:n_in], refs[n_in:n_in + ri]
        outs, r_outs = refs[n_in + ri:n_in + ri + n_out], refs[n_in + ri + n_out:n_in + ri + n_out + ro]
        scratch, (send_sems, recv_sems) = refs[n_in + ri + n_out + ro:-2], refs[-2:]
        step, total = 0, 1
        for ax, n in enumerate(grid):
            step = step * n + pl.program_id(ax)
            total *= n
        parts = (r_ins, r_outs, send_sems, recv_sems)

        @pl.when(step == 0)
        def _():
            rider.start(*parts)

        if rider.mid is not None:
            @pl.when(step == min(total - 1, int(total * RIDER_MID)))
            def _():
                rider.mid(*parts)

        body(*ins, *outs, *scratch)

        @pl.when(step == total - 1)
        def _():
            rider.finish(*parts)

    outs = pl.pallas_call(
        hosted,
        name=name,
        grid=grid,
        in_specs=list(in_specs) + [ANY] * ri,
        out_specs=list(out_specs) + [ANY] * ro,
        out_shape=list(out_shape) + rider.out_shapes,
        scratch_shapes=list(scratch_shapes) + [pltpu.SemaphoreType.DMA((rider.n_sems,)), pltpu.SemaphoreType.DMA((rider.n_sems,))],
        input_output_aliases=aliases,
        compiler_params=_params(("arbitrary",) * len(grid)),
    )(*operands, *rider.ins)
    return list(outs[:n_out]), list(outs[n_out:])


def mmx(name, a, b, *, grid, a_spec, b_spec, o_spec, o_shape, o_dtype, ca, cb, acc_shape=None, scale=None, rider=None):
    nk = grid[-1] if acc_shape is not None else 1
    assert scale is None or nk == 1

    def body(a_ref, b_ref, o_ref, *acc):
        p = _dot(a_ref[...].astype(BF), b_ref[...].astype(BF), ca, cb)
        if scale is not None:
            p = p * scale
        if nk == 1:
            o_ref[...] = p.astype(o_ref.dtype)
        else:
            kk = pl.program_id(len(grid) - 1)
            _accumulate(acc[0], p, kk == 0)

            @pl.when(kk == nk - 1)
            def _():
                o_ref[...] = acc[0][...].astype(o_ref.dtype)

    if rider is not None:
        (out,), r_outs = host_call(body, name=name, grid=grid, in_specs=[a_spec, b_spec], out_specs=[o_spec], out_shape=[jax.ShapeDtypeStruct(o_shape, o_dtype)],
                                   scratch_shapes=[pltpu.VMEM(acc_shape, F32)] if nk > 1 else [], operands=(a, b), rider=rider)
        return out, r_outs
    sem = ("parallel",) * (len(grid) - 1) + ("arbitrary" if nk > 1 else "parallel",)
    return pl.pallas_call(
        body,
        name=name,
        grid=grid,
        in_specs=[a_spec, b_spec],
        out_specs=o_spec,
        out_shape=jax.ShapeDtypeStruct(o_shape, o_dtype),
        scratch_shapes=[pltpu.VMEM(acc_shape, F32)] if nk > 1 else [],
        compiler_params=_params(sem),
    )(a, b)


def mm(a, b, *, ta=False, tb=False, out_dtype=F32, name, tm_cap=1024, tn_cap=1024, tk_cap=4096, rider=None):
    m, k = (a.shape[1], a.shape[0]) if ta else a.shape
    n = b.shape[0] if tb else b.shape[1]
    assert (b.shape[1] if tb else b.shape[0]) == k
    tm, tn, tk = _tile(m, tm_cap), _tile(n, tn_cap), _tile(k, tk_cap)
    nk = k // tk
    a_spec = pl.BlockSpec((tk, tm), lambda i, j, kk: (kk, i)) if ta else pl.BlockSpec((tm, tk), lambda i, j, kk: (i, kk))
    b_spec = pl.BlockSpec((tn, tk), lambda i, j, kk: (j, kk)) if tb else pl.BlockSpec((tk, tn), lambda i, j, kk: (kk, j))
    return mmx(name, a, b, grid=(m // tm, n // tn, nk), a_spec=a_spec, b_spec=b_spec, o_spec=pl.BlockSpec((tm, tn), lambda i, j, kk: (i, j)),
               o_shape=(m, n), o_dtype=out_dtype, ca=0 if ta else 1, cb=1 if tb else 0, acc_shape=(tm, tn) if nk > 1 else None, rider=rider)


class Window:
    def __init__(self, out, block, cols, buf=None):
        self.out, self.block, self.cols, self.buf = out, block, cols, buf


def ew(fn, rows, vecs, out_rows, out_red=(), *, tm, name, prev_halo=(), next_halo=(), window=None):
    t = rows[0].shape[0]
    tm = min(tm, t)
    nt = t // tm
    assert t % tm == 0 and (tm % HALO == 0 or (tm == t and not prev_halo and not next_halo))
    nr, nv, npv, nnx, nor = len(rows), len(vecs), len(prev_halo), len(next_halo), len(out_rows)
    hb = tm // HALO
    n_in = nr + nv + npv + nnx
    passed = window is not None and window.buf is not None

    def body(*refs):
        i = pl.program_id(0)
        ins = [r[...].astype(F32) for r in refs[:n_in]]
        outs = refs[n_in + passed:]
        o_rows, o_red = fn(i, nt, ins[:nr], ins[nr:nr + nv], ins[nr + nv:nr + nv + npv], ins[nr + nv + npv:])
        for ref, val in zip(outs[:nor], o_rows, strict=True):
            ref[...] = val.astype(ref.dtype)
        for ref, val in zip(outs[nor:], o_red, strict=True):
            _accumulate(ref, val, i == 0)

    in_specs = [pl.BlockSpec((tm, r.shape[1]), lambda i: (i, 0)) for r in rows]
    in_specs += [pl.BlockSpec(v.shape, lambda i: (0, 0)) for v in vecs]
    in_specs += [pl.BlockSpec((HALO, rows[k].shape[1]), lambda i: (jnp.maximum(i * hb - 1, 0), 0)) for k in prev_halo]
    in_specs += [pl.BlockSpec((HALO, rows[k].shape[1]), lambda i: (jnp.minimum((i + 1) * hb, t // HALO - 1), 0)) for k in next_halo]
    out_specs = [pl.BlockSpec((tm, c), lambda i: (i, 0)) for c, _ in out_rows]
    out_specs += [pl.BlockSpec(s, lambda i: (0, 0)) for s in out_red]
    out_shape = [jax.ShapeDtypeStruct((t, c), d) for c, d in out_rows] + [jax.ShapeDtypeStruct(s, F32) for s in out_red]
    operands = [*rows, *vecs, *[rows[k] for k in prev_halo], *[rows[k] for k in next_halo]]
    aliases = {}
    if window is not None:
        c, dt_ = out_rows[window.out]
        out_specs[window.out] = pl.BlockSpec((tm, c), lambda i: (i, window.block))
        out_shape[window.out] = jax.ShapeDtypeStruct((t, window.cols), dt_)
        if passed:
            in_specs.append(ANY)
            operands.append(window.buf)
            aliases = {n_in: window.out}
    return pl.pallas_call(
        body,
        name=name,
        grid=(nt,),
        in_specs=in_specs,
        out_specs=out_specs,
        out_shape=out_shape,
        input_output_aliases=aliases,
        compiler_params=_params(("arbitrary",) if out_red else ("parallel",)),
    )(*operands)


def _shift_down(x, prev, j):
    if j == 0:
        return x
    r = pltpu.roll(x, j, 0)
    rh = pltpu.roll(prev, j, 0)
    row = lax.broadcasted_iota(jnp.int32, (HALO, x.shape[1]), 0)
    head = jnp.where(row < j, rh, r[:HALO])
    return jnp.concatenate([head, r[HALO:]], axis=0)


def _shift_up(x, nxt, j):
    if j == 0:
        return x
    n = x.shape[0]
    r = pltpu.roll(x, n - j, 0)
    rh = pltpu.roll(nxt, HALO - j, 0)
    row = lax.broadcasted_iota(jnp.int32, (HALO, x.shape[1]), 0)
    tail = jnp.where(row >= HALO - j, rh, r[n - HALO:])
    return jnp.concatenate([r[: n - HALO], tail], axis=0)


def _conv_fwd(x, prev, w):
    kk = w.shape[0]
    acc = None
    for k in range(kk):
        term = w[k:k + 1, :] * _shift_down(x, prev, kk - 1 - k)
        acc = term if acc is None else acc + term
    return acc


def ffn_up(h, g, wf, rider=None):
    t, d = h.shape
    f4 = wf.shape[1] // 2
    tm = _tile(t, 1024)
    sub = _tile(tm, FFN_SUB, 16)

    def body(h_ref, g_ref, wg_ref, wu_ref, ab_ref, s_ref, n_ref):
        @pl.when(pl.program_id(1) == 0)
        def _():
            n_ref[...] = _rms(h_ref[...], g_ref[...]).astype(BF)

        for r in range(tm // sub):
            rows = slice(r * sub, (r + 1) * sub)
            n = n_ref[rows, :]
            a = _dot(n, wg_ref[...], 1, 1)
            b = _dot(n, wu_ref[...], 1, 1)
            ab_ref[0, rows, :] = a.astype(BF)
            ab_ref[1, rows, :] = b.astype(BF)
            s_ref[rows, :] = (a * _sigmoid(a) * b).astype(BF)

    wspec = lambda ib: pl.BlockSpec((None, f4, d), lambda i, j: (j, ib, 0))
    return host_call(
        body,
        name="ffn_up",
        grid=(t // tm, NCHIP),
        in_specs=[pl.BlockSpec((tm, d), lambda i, j: (i, 0)), pl.BlockSpec((1, d), lambda i, j: (0, 0)), wspec(0), wspec(1)],
        out_specs=[pl.BlockSpec((2, None, tm, f4), lambda i, j: (0, j, i, 0)), pl.BlockSpec((None, tm, f4), lambda i, j: (j, i, 0)),
                   pl.BlockSpec((tm, d), lambda i, j: (i, 0))],
        out_shape=[jax.ShapeDtypeStruct((2, NCHIP, t, f4), BF), jax.ShapeDtypeStruct((NCHIP, t, f4), BF), jax.ShapeDtypeStruct((t, d), BF)],
        scratch_shapes=[],
        operands=(h, g, wf, wf),
        rider=rider,
    )


def ffn_down(s4, wf, h, rider=None):
    t, d = h.shape
    f4 = s4.shape[2]
    tm = _tile(t, 512)

    def body(s_ref, w_ref, h_ref, o_ref):
        acc = _dot(s_ref[0], w_ref[0])
        for k in range(1, NCHIP):
            acc = acc + _dot(s_ref[k], w_ref[k])
        o_ref[...] = h_ref[...] + 0.5 * acc

    (out,), r_outs = host_call(
        body,
        name="ffn_down",
        grid=(t // tm,),
        in_specs=[pl.BlockSpec((NCHIP, tm, f4), lambda i: (0, i, 0)), pl.BlockSpec((NCHIP, f4, d), lambda i: (0, 0, 0)), pl.BlockSpec((tm, d), lambda i: (i, 0))],
        out_specs=[pl.BlockSpec((tm, d), lambda i: (i, 0))],
        out_shape=[jax.ShapeDtypeStruct((t, d), F32)],
        scratch_shapes=[],
        operands=(s4, wf, h),
        rider=rider,
    )
    return out, r_outs


def ffn_bwd(dho, ab, wf, wd, rider=None):
    t, d = dho.shape
    f4 = wd.shape[1]
    tm = _tile(t, 1024)
    sub = _tile(tm, FFN_SUB, 16)

    def body(dho_ref, ab_ref, wg_ref, wu_ref, wd_ref, dn_ref, s_ref, dab_ref, do_sc):
        j = pl.program_id(1)

        @pl.when(j == 0)
        def _():
            do_sc[...] = (0.5 * dho_ref[...]).astype(BF)
            dn_ref[...] = jnp.zeros_like(dn_ref)

        for r in range(tm // sub):
            rows = slice(r * sub, (r + 1) * sub)
            ds = _dot(do_sc[rows, :], wd_ref[...], 1, 1)
            av, bv = ab_ref[0, rows, :].astype(F32), ab_ref[1, rows, :].astype(F32)
            sig = _sigmoid(av)
            sl = av * sig
            s_ref[rows, :] = (sl * bv).astype(BF)
            da = (ds * bv * (sig * (1.0 + av * (1.0 - sig)))).astype(BF)
            db = (ds * sl).astype(BF)
            dab_ref[0, rows, :] = da
            dab_ref[1, rows, :] = db
            dn_ref[rows, :] += _dot(da, wg_ref[...]) + _dot(db, wu_ref[...])

    row = lambda c: pl.BlockSpec((tm, c), lambda i, j: (i, 0))
    wspec = lambda ib: pl.BlockSpec((None, f4, d), lambda i, j: (j, ib, 0))
    ab_spec = pl.BlockSpec((2, None, tm, f4), lambda i, j: (0, j, i, 0))
    return host_call(
        body,
        name="ffn_bwd",
        grid=(t // tm, NCHIP),
        in_specs=[row(d), ab_spec, wspec(0), wspec(1), wspec(0)],
        out_specs=[row(d), pl.BlockSpec((None, tm, f4), lambda i, j: (j, i, 0)), ab_spec],
        out_shape=[jax.ShapeDtypeStruct((t, d), F32), jax.ShapeDtypeStruct((NCHIP, t, f4), BF), jax.ShapeDtypeStruct((2, NCHIP, t, f4), BF)],
        scratch_shapes=[pltpu.VMEM((tm, d), BF)],
        operands=(dho, ab, wf, wf, wd),
        rider=rider,
    )


def ffn_wgrads(n, dho, s4, dab):
    t, d = n.shape
    f4 = s4.shape[2]
    tn = _tile(d, 512)
    g_in = mmx("g_ffn_in", dab, n, grid=(2, NCHIP, d // tn), a_spec=pl.BlockSpec((None, None, t, f4), lambda wh, k, j: (wh, k, 0, 0)),
               b_spec=pl.BlockSpec((t, tn), lambda wh, k, j: (0, j)), o_spec=pl.BlockSpec((None, None, f4, tn), lambda wh, k, j: (k, wh, 0, j)),
               o_shape=(NCHIP, 2, f4, d), o_dtype=BF, ca=0, cb=0)
    g_out = mmx("g_ffn_out", s4, dho, grid=(NCHIP, d // tn), a_spec=pl.BlockSpec((None, t, f4), lambda k, j: (k, 0, 0)),
                b_spec=pl.BlockSpec((t, tn), lambda k, j: (0, j)), o_spec=pl.BlockSpec((None, f4, tn), lambda k, j: (k, 0, j)),
                o_shape=(NCHIP, f4, d), o_dtype=BF, ca=0, cb=0, scale=0.5)
    return jnp.concatenate([g_in.reshape(NCHIP, 2 * f4, d), g_out], axis=1)


def norm_cast(h, g):
    def fn(i, nt, rows, vecs, prevs, nexts):
        return [_rms(rows[0], vecs[0])], []
    return ew(fn, [h], [g], [(h.shape[1], BF)], tm=512, name="norm_cast")[0]


def _zero_if(cond, x):
    return jnp.where(cond, jnp.zeros_like(x), x)


def conv_a_fwd(sc3, w_sc):
    d = sc3.shape[1] // 3

    def fn(i, nt, rows, vecs, prevs, nexts):
        x, pv = rows[0], _zero_if(i == 0, prevs[0])
        v = x[:, d:2 * d] * x[:, 2 * d:]
        vp = pv[:, d:2 * d] * pv[:, 2 * d:]
        return [x[:, :d] * _conv_fwd(v, vp, vecs[0])], []

    return ew(fn, [sc3], [w_sc], [(d, BF)], tm=256, name="conv_a_fwd", prev_halo=(0,))[0]


def _softplus(x):
    e = jnp.exp(-jnp.abs(x))
    return jnp.maximum(x, 0.0) + jnp.where(e < 1e-4, e - 0.5 * e * e, jnp.log(1.0 + e))


def conv_m_fwd(xbc_raw, dt_raw, w_mc, b_mc, dt_bias):
    def fn(i, nt, rows, vecs, prevs, nexts):
        pre = _conv_fwd(rows[0], _zero_if(i == 0, prevs[0]), vecs[0]) + vecs[1]
        return [pre * _sigmoid(pre), _softplus(rows[1] + vecs[2])], []

    return ew(fn, [xbc_raw, dt_raw], [w_mc, b_mc, dt_bias], [(xbc_raw.shape[1], F32), (LANE, F32)], tm=256, name="conv_m_fwd",
              prev_halo=(0,))


def conv_m_bwd1(dxbc, xbc_raw, ddt, dt_raw, w_mc, b_mc, dt_bias, window):
    def fn(i, nt, rows, vecs, prevs, nexts):
        pre = _conv_fwd(rows[1], _zero_if(i == 0, prevs[0]), vecs[0]) + vecs[1]
        sig = _sigmoid(pre)
        dpre = rows[0] * (sig * (1.0 + pre * (1.0 - sig)))
        ddr = rows[2] * _sigmoid(rows[3] + vecs[2])
        return [dpre, ddr], [jnp.sum(ddr, axis=0, keepdims=True)]

    return ew(fn, [dxbc, xbc_raw, ddt, dt_raw], [w_mc, b_mc, dt_bias], [(dxbc.shape[1], F32), (LANE, BF)], [(1, LANE)], tm=256,
              name="conv_m_bwd1", prev_halo=(1,), window=window)


def conv_bwd2(dpre, x, w, name, window):
    kk = w.shape[0]

    def fn(i, nt, rows, vecs, prevs, nexts):
        dp, xv = rows[0], rows[1]
        nx = _zero_if(i == nt - 1, nexts[0])
        dx = None
        dws = []
        for k in range(kk):
            up = _shift_up(dp, nx, kk - 1 - k)
            term = vecs[0][k:k + 1, :] * up
            dx = term if dx is None else dx + term
            dws.append(jnp.sum(up * xv, axis=0, keepdims=True))
        return [dx], [jnp.concatenate(dws, axis=0), jnp.sum(dp, axis=0, keepdims=True)]

    c = x.shape[1]
    return ew(fn, [dpre, x], [w], [(c, BF)], [(kk, c), (1, c)], tm=256, name=name, next_halo=(0,), window=window)


def conv_a_bwd1(dya, sc3, w_sc, window):
    d = sc3.shape[1] // 3

    def fn(i, nt, rows, vecs, prevs, nexts):
        x, pv = rows[1], _zero_if(i == 0, prevs[0])
        v = x[:, d:2 * d] * x[:, 2 * d:]
        vp = pv[:, d:2 * d] * pv[:, 2 * d:]
        return [rows[0] * x[:, :d], rows[0] * _conv_fwd(v, vp, vecs[0]), v], []

    return ew(fn, [dya, sc3], [w_sc], [(d, F32), (d, BF), (d, F32)], tm=256, name="conv_a_bwd1", prev_halo=(1,), window=window)


def conv_a_bwd2(dcv, v, sc3, w_sc, window):
    d = v.shape[1]
    kk = w_sc.shape[0]

    def fn(i, nt, rows, vecs, prevs, nexts):
        dp, vv, x = rows
        nx = _zero_if(i == nt - 1, nexts[0])
        dv = None
        dws = []
        for k in range(kk):
            up = _shift_up(dp, nx, kk - 1 - k)
            term = vecs[0][k:k + 1, :] * up
            dv = term if dv is None else dv + term
            dws.append(jnp.sum(up * vv, axis=0, keepdims=True))
        return [jnp.concatenate([dv * x[:, 2 * d:], dv * x[:, d:2 * d]], axis=1)], [jnp.concatenate(dws, axis=0)]

    return ew(fn, [dcv, v, sc3], [w_sc], [(2 * d, BF)], [(kk, d)], tm=256, name="conv_a_bwd2", next_halo=(0,), window=window)


def _xdot(a, b, passes, split_lhs, ca=1, cb=0):
    parts, r = [], (a if split_lhs else b)
    for _ in range(passes):
        piece = r.astype(BF)
        parts.append(piece)
        r = r - piece.astype(F32)
    other = (b if split_lhs else a).astype(BF)
    acc = None
    for piece in parts:
        term = _dot(piece, other, ca, cb) if split_lhs else _dot(other, piece, ca, cb)
        acc = term if acc is None else acc + term
    return acc


def _ssd_common(xbc_ref, dt_ref, alog_ref, e_ref, w):
    ll = SSM_L
    xs = xbc_ref[:, 0:w]
    dtv = dt_ref[...]
    a_row = -jnp.exp(alog_ref[...])
    a = dtv * a_row
    row = lax.broadcasted_iota(jnp.int32, (ll, ll), 0)
    col = lax.broadcasted_iota(jnp.int32, (ll, ll), 1)
    tril = (row >= col).astype(F32)
    triu = (row <= col).astype(F32)
    acl = _xdot(tril, a, 3, False)
    acl_t = _xdot(a, triu, 3, True, 0, 0)
    e = e_ref[...]
    aclx = _xdot(acl, e, 3, True)
    dtx = _xdot(dtv, e, 2, True)
    last = aclx[ll - 1:ll, :]
    e_in = jnp.exp(aclx)
    e_end = jnp.exp(last - aclx)
    e_tot = jnp.exp(last)
    x = xs * dtx
    return dict(xs=xs, dtv=dtv, a_row=a_row, a=a, row=row, col=col, triu=triu, acl=acl, acl_t=acl_t, dtx=dtx, e_in=e_in, e_end=e_end,
                e_tot=e_tot, x=x)


def _decay(q, hh):
    diff = q["acl"][:, hh:hh + 1] - q["acl_t"][hh:hh + 1, :]
    return jnp.exp(jnp.where(q["row"] >= q["col"], diff, -jnp.inf))


def ssd_fwd(xbc, dt, z, a_log, d_exp, m_norm, e_mat, rider=None):
    t = xbc.shape[0]
    w = z.shape[1]
    gn = SSM_G * SSM_N
    gw = w // SSM_G
    ll, nn = SSM_L, SSM_N
    nc = t // ll
    cw = xbc.shape[1]

    def body(xbc_ref, dt_ref, z_ref, alog_ref, dexp_ref, mn_ref, e_ref, yn_ref, y_ref, sp_ref, s_sc):
        c = pl.program_id(0)

        @pl.when(c == 0)
        def _():
            s_sc[...] = jnp.zeros_like(s_sc)

        q = _ssd_common(xbc_ref, dt_ref, alog_ref, e_ref, w)
        xb = q["x"].astype(BF)
        xsb = (q["x"] * q["e_end"]).astype(BF)
        sp = s_sc[...]
        sp_ref[0] = sp
        spb = sp.astype(BF)
        lane = lax.broadcasted_iota(jnp.int32, (ll, LANE), 1)
        for g in range(SSM_G):
            lo = g * gw
            bg = xbc_ref[:, w + g * nn:w + (g + 1) * nn].astype(BF)
            cg = xbc_ref[:, w + gn + g * nn:w + gn + (g + 1) * nn].astype(BF)
            yoff = _dot(cg, spb[:, lo:lo + gw]) * q["e_in"][:, lo:lo + gw]
            s_sc[:, lo:lo + gw] = sp[:, lo:lo + gw] * q["e_tot"][:, lo:lo + gw] + _dot(bg, xsb[:, lo:lo + gw], 0, 0)
            cb = _dot(cg, bg, 1, 1)
            for pr in range(gw // LANE):
                l0 = lo + pr * LANE
                xp = xb[:, l0:l0 + LANE]
                ys = []
                for hh in (l0 // SSM_P, l0 // SSM_P + 1):
                    wm = (cb * _decay(q, hh)).astype(BF)
                    ys.append(_dot(wm, xp))
                ydiag = jnp.where(lane < SSM_P, ys[0], ys[1])
                y_ref[:, l0:l0 + LANE] = ydiag + yoff[:, pr * LANE:(pr + 1) * LANE] + dexp_ref[:, l0:l0 + LANE] * q["xs"][:, l0:l0 + LANE]
        zv = z_ref[...].astype(F32)
        yz = y_ref[...] * (zv * _sigmoid(zv))
        for g in range(SSM_G):
            lo = g * gw
            yn_ref[:, lo:lo + gw] = _rms(yz[:, lo:lo + gw], mn_ref[:, lo:lo + gw]).astype(BF)

    vec = lambda s: pl.BlockSpec(s, lambda c: (0, 0))
    return host_call(
        body,
        name="ssd_fwd",
        grid=(nc,),
        in_specs=[
            pl.BlockSpec((ll, cw), lambda c: (c, 0)), pl.BlockSpec((ll, LANE), lambda c: (c, 0)), pl.BlockSpec((ll, w), lambda c: (c, 0)),
            vec((1, LANE)), vec((1, w)), vec((1, w)), vec((LANE, w)),
        ],
        out_specs=[pl.BlockSpec((ll, w), lambda c: (c, 0)), pl.BlockSpec((ll, w), lambda c: (c, 0)), pl.BlockSpec((1, nn, w), lambda c: (c, 0, 0))],
        out_shape=[jax.ShapeDtypeStruct((t, w), BF), jax.ShapeDtypeStruct((t, w), F32), jax.ShapeDtypeStruct((nc, nn, w), F32)],
        scratch_shapes=[pltpu.VMEM((nn, w), F32)],
        operands=(xbc, dt, z, a_log, d_exp, m_norm, e_mat),
        rider=rider,
    )


def ssd_bwd(dyn, y, z, xbc, dt, sprev, a_log, d_exp, m_norm, e_mat, et_mat, window, rider=None):
    t = xbc.shape[0]
    w = z.shape[1]
    gn = SSM_G * SSM_N
    gw = w // SSM_G
    ll, nn = SSM_L, SSM_N
    nc = t // ll
    cw = xbc.shape[1]

    def body(dyn_ref, y_ref, z_ref, xbc_ref, dt_ref, sp_ref, alog_ref, dexp_ref, mn_ref, e_ref, et_ref,
             dz_ref, dxbc_ref, ddt_ref, dmn_ref, dd_ref, dal_ref, ds_sc, dy_sc, dx_sc):
        step = pl.program_id(0)

        @pl.when(step == 0)
        def _():
            ds_sc[...] = jnp.zeros_like(ds_sc)

        zv, yv = z_ref[...].astype(F32), y_ref[...]
        sg = _sigmoid(zv)
        sz = zv * sg
        yz = yv * sz
        dmn = []
        for g in range(SSM_G):
            lo = g * gw
            dseg, dmn_g = _rms_bwd(yz[:, lo:lo + gw], mn_ref[:, lo:lo + gw], dyn_ref[:, lo:lo + gw])
            dy_sc[:, lo:lo + gw] = dseg
            dmn.append(dmn_g)
        dmn = jnp.concatenate(dmn, axis=1)
        dyz = dy_sc[...]
        dz_ref[...] = (dyz * yv * (sg * (1.0 + zv * (1.0 - sg)))).astype(BF)
        dy = dyz * sz

        q = _ssd_common(xbc_ref, dt_ref, alog_ref, e_ref, w)
        x = q["x"]
        xb = x.astype(BF)
        xsb = (x * q["e_end"]).astype(BF)
        sp = sp_ref[0]
        spb = sp.astype(BF)
        dsn = ds_sc[...]
        dsnb = dsn.astype(BF)
        dyb = dy.astype(BF)
        lane = lax.broadcasted_iota(jnp.int32, (ll, LANE), 1)
        lane1 = lax.broadcasted_iota(jnp.int32, (1, LANE), 1)
        sub1 = lax.broadcasted_iota(jnp.int32, (LANE, 1), 0)
        dacl = jnp.zeros((ll, LANE), F32)
        dacl_t = jnp.zeros((LANE, ll), F32)
        d_ein, d_eend, d_etot = [], [], []
        for g in range(SSM_G):
            lo = g * gw
            sl = slice(lo, lo + gw)
            bg = xbc_ref[:, w + g * nn:w + (g + 1) * nn].astype(BF)
            cg = xbc_ref[:, w + gn + g * nn:w + gn + (g + 1) * nn].astype(BF)
            zg = _dot(cg, spb[:, sl])
            dzz = (dy[:, sl] * q["e_in"][:, sl]).astype(BF)
            d_ein.append(dy[:, sl] * zg)
            dcg = _dot(dzz, spb[:, sl], 1, 1)
            ds_sc[:, sl] = _dot(cg, dzz, 0, 0) + dsn[:, sl] * q["e_tot"][:, sl]
            d_etot.append(jnp.sum(dsn[:, sl] * sp[:, sl], axis=0, keepdims=True))
            dbg = _dot(xsb[:, sl], dsnb[:, sl], 1, 1)
            dxs_g = _dot(bg, dsnb[:, sl])
            d_eend.append(dxs_g * x[:, sl])
            cb = _dot(cg, bg, 1, 1)
            dcb = jnp.zeros((ll, ll), F32)
            for pr in range(gw // LANE):
                l0 = lo + pr * LANE
                xp = xb[:, l0:l0 + LANE]
                dyp = dyb[:, l0:l0 + LANE]
                dxp = []
                for hi, hh in enumerate((l0 // SSM_P, l0 // SSM_P + 1)):
                    lm = _decay(q, hh)
                    wm = (cb * lm).astype(BF)
                    in_head = (lane < SSM_P) if hi == 0 else (lane >= SSM_P)
                    dwm = _dot(jnp.where(in_head, dyp, jnp.zeros_like(dyp)), xp, 1, 1)
                    dxp.append(_dot(wm, dyp, 0, 0))
                    dlm = dwm * lm
                    dcb = dcb + dlm
                    dd = dlm * cb
                    dacl = dacl + jnp.sum(dd, axis=1, keepdims=True) * (lane1 == hh).astype(F32)
                    dacl_t = dacl_t + (sub1 == hh).astype(F32) * jnp.sum(dd, axis=0, keepdims=True)
                dx_sc[:, l0:l0 + LANE] = jnp.where(lane < SSM_P, dxp[0], dxp[1]) + dxs_g[:, pr * LANE:(pr + 1) * LANE] * q["e_end"][:, l0:l0 + LANE]
            dcbb = dcb.astype(BF)
            dxbc_ref[:, w + g * nn:w + (g + 1) * nn] = dbg + _dot(dcbb, cg, 0, 0)
            dxbc_ref[:, w + gn + g * nn:w + gn + (g + 1) * nn] = dcg + _dot(dcbb, bg)
        d_ein = jnp.concatenate(d_ein, axis=1) * q["e_in"]
        d_eend = jnp.concatenate(d_eend, axis=1) * q["e_end"]
        d_etot = jnp.concatenate(d_etot, axis=1) * q["e_tot"]
        et = et_ref[...]
        last_add = jnp.sum(d_eend, axis=0, keepdims=True) + d_etot
        last_add = _xdot(jnp.broadcast_to(last_add, (HALO, w)), et, 2, True)[0:1]
        row1 = lax.broadcasted_iota(jnp.int32, (ll, LANE), 0)
        dacl = dacl + _xdot(d_ein - d_eend, et, 2, True) + jnp.where(row1 == ll - 1, last_add, 0.0)
        da = _xdot(q["triu"], dacl, 2, False) - _xdot(q["triu"], dacl_t, 2, False, 1, 1)
        dxv = dx_sc[...]
        dxbc_ref[:, 0:w] = dexp_ref[...] * dy + dxv * q["dtx"]
        ddt_ref[...] = _xdot(dxv * q["xs"], et, 2, True) + da * q["a_row"]
        dal = jnp.sum(da * q["dtv"], axis=0, keepdims=True) * q["a_row"]
        ddv = jnp.sum(dy * q["xs"], axis=0, keepdims=True)
        ddv = _xdot(jnp.broadcast_to(ddv, (HALO, w)), et, 2, True)[0:1]
        _accumulate(dmn_ref, dmn, step == 0)
        _accumulate(dd_ref, ddv, step == 0)
        _accumulate(dal_ref, dal, step == 0)

    rev = lambda c_: pl.BlockSpec((ll, c_), lambda s: (nc - 1 - s, 0))
    vec = lambda s_: pl.BlockSpec(s_, lambda s: (0, 0))
    n_in = 11

    def body_skipping_buffer(*refs):
        body(*refs[:n_in], *refs[n_in + 1:])

    return host_call(
        body_skipping_buffer,
        name="ssd_bwd",
        grid=(nc,),
        in_specs=[
            rev(w), rev(w), rev(w), rev(cw), rev(LANE), pl.BlockSpec((1, nn, w), lambda s: (nc - 1 - s, 0, 0)),
            vec((1, LANE)), vec((1, w)), vec((1, w)), vec((LANE, w)), vec((w, LANE)), ANY,
        ],
        out_specs=[pl.BlockSpec((ll, w), lambda s: (nc - 1 - s, window.block)), rev(cw), rev(LANE), vec((1, w)), vec((1, LANE)), vec((1, LANE))],
        out_shape=[
            jax.ShapeDtypeStruct((t, window.cols), BF), jax.ShapeDtypeStruct((t, cw), F32), jax.ShapeDtypeStruct((t, LANE), F32),
            jax.ShapeDtypeStruct((1, w), F32), jax.ShapeDtypeStruct((1, LANE), F32), jax.ShapeDtypeStruct((1, LANE), F32),
        ],
        scratch_shapes=[pltpu.VMEM((nn, w), F32), pltpu.VMEM((ll, w), F32), pltpu.VMEM((ll, w), F32)],
        operands=(dyn, y, z, xbc, dt, sprev, a_log, d_exp, m_norm, e_mat, et_mat, window.buf),
        rider=rider,
        aliases={n_in: 0},
    )


def _w1024_spec(d, nblk, iblk):
    r = nblk * (d // NCHIP)
    return pl.BlockSpec((NCHIP, r, d), lambda i: (0, iblk // nblk, 0))


def _whole(ref):
    v = ref[...]
    return v.reshape(v.shape[0] * v.shape[1], v.shape[2])


def mix_out_fwd(ya_in, yn, gates, h, w1024, rider=None):
    t, d = h.shape
    tm = _tile(t, 256)

    def body(ya_ref, yn_ref, g_ref, h_ref, wm_ref, wa_ref, wo_ref, ho_ref, oa_ref, om_ref, mg_ref):
        y_a = _dot(ya_ref[...], _whole(wa_ref))
        y_m = _dot(yn_ref[...], _whole(wm_ref))
        oa_ref[...] = y_a
        om_ref[...] = y_m
        gv = g_ref[...].astype(F32)
        mg = (_sigmoid(gv[:, :d]) * y_a + _sigmoid(gv[:, d:]) * y_m).astype(BF)
        mg_ref[...] = mg
        ho_ref[...] = h_ref[...] + _dot(mg, _whole(wo_ref))

    row = lambda c: pl.BlockSpec((tm, c), lambda i: (i, 0))
    return host_call(
        body,
        name="mix_out_fwd",
        grid=(t // tm,),
        in_specs=[row(d), row(2 * d), row(2 * d), row(d), _w1024_spec(d, 2, 0), _w1024_spec(d, 1, 2), _w1024_spec(d, 1, 3)],
        out_specs=[row(d), row(d), row(d), row(d)],
        out_shape=[jax.ShapeDtypeStruct((t, d), F32), jax.ShapeDtypeStruct((t, d), F32), jax.ShapeDtypeStruct((t, d), F32),
                   jax.ShapeDtypeStruct((t, d), BF)],
        scratch_shapes=[],
        operands=(ya_in, yn, gates, h, w1024, w1024, w1024),
        rider=rider,
    )


def mix_out_bwd(dh, gates, y_a, y_m, w1024, cols):
    t, d = dh.shape
    tm = _tile(t, 256)

    def body(dh_ref, g_ref, ya_ref, ym_ref, wm_ref, wa_ref, wo_ref, dg_ref, dya_ref, dyn_ref, da_ref, dm_ref):
        dmg = _dot(dh_ref[...].astype(BF), _whole(wo_ref), 1, 1)
        gv = g_ref[...].astype(F32)
        sa, sm = _sigmoid(gv[:, :d]), _sigmoid(gv[:, d:])
        dg_ref[:, :d] = (dmg * ya_ref[...] * sa * (1.0 - sa)).astype(BF)
        dg_ref[:, d:] = (dmg * ym_ref[...] * sm * (1.0 - sm)).astype(BF)
        da = (dmg * sa).astype(BF)
        dm = (dmg * sm).astype(BF)
        da_ref[...] = da
        dm_ref[...] = dm
        dya_ref[...] = _dot(da, _whole(wa_ref), 1, 1)
        dyn_ref[...] = _dot(dm, _whole(wm_ref), 1, 1)

    row = lambda c: pl.BlockSpec((tm, c), lambda i: (i, 0))
    return pl.pallas_call(
        body,
        name="mix_out_bwd",
        grid=(t // tm,),
        in_specs=[row(d), row(2 * d), row(d), row(d), _w1024_spec(d, 2, 0), _w1024_spec(d, 1, 2), _w1024_spec(d, 1, 3)],
        out_specs=[row(2 * d), row(d), row(2 * d), row(d), row(d)],
        out_shape=[jax.ShapeDtypeStruct((t, cols), BF), jax.ShapeDtypeStruct((t, d), F32), jax.ShapeDtypeStruct((t, 2 * d), F32),
                   jax.ShapeDtypeStruct((t, d), BF), jax.ShapeDtypeStruct((t, d), BF)],
        compiler_params=_params(("parallel",)),
    )(dh, gates, y_a, y_m, w1024, w1024, w1024)


def norm_bwd_add(dh, h, g, dn):
    def fn(i, nt, rows, vecs, prevs, nexts):
        dx, dg = _rms_bwd(rows[1], vecs[0], rows[2])
        return [rows[0] + dx], [dg]
    d = h.shape[1]
    return ew(fn, [dh, h, dn], [g], [(d, F32)], [(1, d)], tm=512, name="norm_bwd_add")


def _pe(p, wpp_ref):
    pb = p.astype(BF)
    return jnp.concatenate([_dot(pb, wpp_ref[k]) for k in range(NCHIP)], axis=1)


def ple_fwd(h, g, p, w1024, wpp):
    t, d = h.shape
    tm = _tile(t, 512)

    def body(h_ref, g_ref, p_ref, wg_ref, wp_ref, ho_ref):
        hv = h_ref[...]
        gate = _sigmoid(_dot(_rms(hv, g_ref[...]).astype(BF), _whole(wg_ref)))
        ho_ref[...] = hv + gate * _pe(p_ref[...], wp_ref)

    row = lambda c: pl.BlockSpec((tm, c), lambda i: (i, 0))
    wpp_spec = pl.BlockSpec(wpp.shape, lambda i: (0, 0, 0))
    return pl.pallas_call(
        body,
        name="ple_fwd",
        grid=(t // tm,),
        in_specs=[row(d), pl.BlockSpec((1, d), lambda i: (0, 0)), row(p.shape[1]), _w1024_spec(d, 1, 4), wpp_spec],
        out_specs=row(d),
        out_shape=jax.ShapeDtypeStruct((t, d), F32),
        compiler_params=_params(("parallel",)),
    )(h, g, p, w1024, wpp)


def ple_bwd(dho, h, g, p, w1024, wpp):
    t, d = h.shape
    tm = _tile(t, 512)

    def body(dho_ref, h_ref, g_ref, p_ref, wg_ref, wp_ref, dh_ref, dg_ref, n_ref, dgp_ref, dpe_ref):
        hv, dv = h_ref[...], dho_ref[...]
        n = _rms(hv, g_ref[...]).astype(BF)
        n_ref[...] = n
        wg = _whole(wg_ref)
        gate = _sigmoid(_dot(n, wg))
        pe = _pe(p_ref[...], wp_ref)
        dpe_ref[...] = (dv * gate).astype(BF)
        dgp = (dv * pe * gate * (1.0 - gate)).astype(BF)
        dgp_ref[...] = dgp
        dx, dg = _rms_bwd(hv, g_ref[...], _dot(dgp, wg, 1, 1))
        dh_ref[...] = dv + dx
        _accumulate(dg_ref, dg, pl.program_id(0) == 0)

    row = lambda c: pl.BlockSpec((tm, c), lambda i: (i, 0))
    wpp_spec = pl.BlockSpec(wpp.shape, lambda i: (0, 0, 0))
    return pl.pallas_call(
        body,
        name="ple_bwd",
        grid=(t // tm,),
        in_specs=[row(d), row(d), pl.BlockSpec((1, d), lambda i: (0, 0)), row(p.shape[1]), _w1024_spec(d, 1, 4), wpp_spec],
        out_specs=[row(d), pl.BlockSpec((1, d), lambda i: (0, 0)), row(d), row(d), row(d)],
        out_shape=[jax.ShapeDtypeStruct((t, d), F32), jax.ShapeDtypeStruct((1, d), F32), jax.ShapeDtypeStruct((t, d), BF),
                   jax.ShapeDtypeStruct((t, d), BF), jax.ShapeDtypeStruct((t, d), BF)],
        compiler_params=_params(("arbitrary",)),
    )(dho, h, g, p, w1024, wpp)


def loss_bwd(h, g, target):
    d = h.shape[1]

    def fn(i, nt, rows, vecs, prevs, nexts):
        err = _rms(rows[0], vecs[0]) - rows[1]
        dx, dg = _rms_bwd(rows[0], vecs[0], err * (1.0 / d))
        return [dx], [jnp.sum(err * err, axis=0, keepdims=True) * (0.5 / d), dg]

    return ew(fn, [h, target], [g], [(d, F32)], [(1, d), (1, d)], tm=512, name="loss_bwd")


def adamw(w, g, m, v, name):
    c1, c2 = 1.0 / (1.0 - ADAM_B1 ** ADAM_STEP), 1.0 / (1.0 - ADAM_B2 ** ADAM_STEP)

    def fn(i, nt, rows, vecs, prevs, nexts):
        wv, gv, mv, vv = rows
        mn = ADAM_B1 * mv + (1.0 - ADAM_B1) * gv
        vn = ADAM_B2 * vv + (1.0 - ADAM_B2) * (gv * gv)
        delta = -ADAM_LR * ((mn * c1) / (jnp.sqrt(vn * c2) + ADAM_EPS) + ADAM_WD * wv)
        return [delta, mn, vn], []

    c = w.shape[1]
    return ew(fn, [w, g, m, v], [], [(c, F32)] * 3, tm=_row_tile(w.shape[0], c, HALO), name=name)


def _place():
    return lax.axis_index("x"), lax.axis_index("y"), lax.axis_index("c")


def _other_chips(x, y):
    return [(1 - x, y), (x, 1 - y), (1 - x, 1 - y)]


ANY = pl.BlockSpec(memory_space=pl.ANY)


def _comm_call(body, name, ins, out_shapes, n_sems, aliases=None):
    return pl.pallas_call(
        body,
        name=name,
        in_specs=[ANY] * len(ins),
        out_specs=[ANY] * len(out_shapes),
        out_shape=out_shapes,
        scratch_shapes=[pltpu.SemaphoreType.DMA((n_sems,)), pltpu.SemaphoreType.DMA((n_sems,))],
        input_output_aliases=aliases or {},
    )(*ins)


def gather_rider(packs):
    nt = len(packs)

    def pieces(ins, outs, send_sems, recv_sems):
        x, y, cc = _place()
        chips = _other_chips(x, y)
        sibling = (x, y, 1 - cc)
        k_me = 2 * x + y

        def copy(k, src, dst, to):
            return pltpu.make_async_remote_copy(src_ref=src, dst_ref=dst, send_sem=send_sems.at[k], recv_sem=recv_sems.at[k],
                                                device_id=to, device_id_type=MESH)

        sends, forwards, arrivals = [], [], []
        for ti in range(nt):
            for j, (px, py) in enumerate(chips):
                sends.append(copy(7 * ti + j, ins[ti].at[cc], outs[ti].at[k_me, cc], (px, py, cc)))
                landed = outs[ti].at[2 * px + py, cc]
                forwards.append((copy(7 * ti + j, landed, landed, (px, py, cc)), copy(7 * ti + 3 + j, landed, landed, sibling)))
                passed = outs[ti].at[2 * px + py, 1 - cc]
                arrivals.append(copy(7 * ti + 3 + j, passed, passed, sibling))
            sends.append(copy(7 * ti + 6, ins[ti], outs[ti].at[k_me], sibling))
            own = outs[ti].at[k_me]
            arrivals.append(copy(7 * ti + 6, own, own, sibling))
        return sends, forwards, arrivals

    def start(*parts):
        for cp in pieces(*parts)[0]:
            cp.start()

    def mid(*parts):
        for landed, forward in pieces(*parts)[1]:
            landed.wait_recv()
            forward.start()

    def finish(*parts):
        sends, forwards, arrivals = pieces(*parts)
        for cp in arrivals:
            cp.wait_recv()
        for cp in sends + [f for _, f in forwards]:
            cp.wait_send()

    return Rider(packs, [jax.ShapeDtypeStruct((NCHIP,) + p.shape, p.dtype) for p in packs], 7 * nt, start, finish, mid)


def swap_rider(gs):
    nt = len(gs)
    hl = gs[0].shape[1] // 2

    def copies(ins, outs, send_sems, recv_sems):
        x, y, cc = _place()
        theirs = pl.ds((1 - cc) * hl, hl)
        return [pltpu.make_async_remote_copy(src_ref=ins[ti].at[:, theirs], dst_ref=outs[ti], send_sem=send_sems.at[ti], recv_sem=recv_sems.at[ti],
                                             device_id=(x, y, 1 - cc), device_id_type=MESH) for ti in range(nt)]

    def start(*parts):
        for cp in copies(*parts):
            cp.start()

    def finish(*parts):
        for cp in copies(*parts):
            cp.wait()

    return Rider(gs, [jax.ShapeDtypeStruct((NCHIP, hl) + g.shape[2:], g.dtype) for g in gs], nt, start, finish)


def scatter_packs(cs, name):
    return scatter_rider(cs).standalone(name)


def scatter_rider(cs):
    nt = len(cs)

    def copies(ins, outs, send_sems, recv_sems):
        x, y, cc = _place()
        cps = []
        for ti in range(nt):
            for j, (px, py) in enumerate(_other_chips(x, y)):
                cps.append(pltpu.make_async_remote_copy(src_ref=ins[ti].at[2 * px + py], dst_ref=outs[ti].at[j], send_sem=send_sems.at[3 * ti + j],
                                                        recv_sem=recv_sems.at[3 * ti + j], device_id=(px, py, cc), device_id_type=MESH))
        return cps

    def start(*parts):
        for cp in copies(*parts):
            cp.start()

    def finish(*parts):
        for cp in copies(*parts):
            cp.wait()

    return Rider(cs, [jax.ShapeDtypeStruct((3,) + c_.shape[1:], c_.dtype) for c_ in cs], 3 * nt, start, finish)


def join_packs(fulls, name):
    nt = len(fulls)
    hl = fulls[0].shape[0] // 2

    def body(*refs):
        ins, outs, (send_sems, recv_sems) = refs[:nt], refs[nt:2 * nt], refs[2 * nt:]
        x, y, cc = _place()
        mine = pl.ds(cc * hl, hl)
        cps = [pltpu.make_async_remote_copy(src_ref=ins[ti].at[mine], dst_ref=outs[ti].at[mine], send_sem=send_sems.at[ti], recv_sem=recv_sems.at[ti],
                                            device_id=(x, y, 1 - cc), device_id_type=MESH) for ti in range(nt)]
        for cp in cps:
            cp.start()
        for cp in cps:
            cp.wait()

    return _comm_call(body, name, fulls, [jax.ShapeDtypeStruct(f.shape, f.dtype) for f in fulls], nt, aliases={ti: ti for ti in range(nt)})


def add_sibling(g, recv, name):
    _, nl, r, c = g.shape
    hl = nl // 2
    tm, tc = _tile2(r, c)

    def body(g_ref, r_ref, o_ref):
        o_ref[...] = (g_ref[...].astype(F32) + r_ref[...].astype(F32)).astype(o_ref.dtype)

    blk = (None, None, tm, tc)
    return pl.pallas_call(
        body,
        name=name,
        grid=(NCHIP, hl, r // tm, c // tc),
        in_specs=[pl.BlockSpec(blk, lambda k, l, i, j: (k, lax.axis_index("c") * hl + l, i, j)), pl.BlockSpec(blk, lambda k, l, i, j: (k, l, i, j))],
        out_specs=pl.BlockSpec(blk, lambda k, l, i, j: (k, l, i, j)),
        out_shape=jax.ShapeDtypeStruct(recv.shape, BF),
        compiler_params=_params(("parallel",) * 4),
    )(g, recv)


def add_chips(cs, got, nl, name):
    _, hl, r, c = cs.shape
    tm, tc = _tile2(r, c)

    def body(own_ref, got_ref, o_ref):
        o_ref[...] = own_ref[...].astype(F32) + got_ref[0].astype(F32) + got_ref[1].astype(F32) + got_ref[2].astype(F32)

    return pl.pallas_call(
        body,
        name=name,
        grid=(hl, r // tm, c // tc),
        in_specs=[pl.BlockSpec((None, None, tm, tc), lambda l, i, j: (2 * lax.axis_index("x") + lax.axis_index("y"), l, i, j)),
                  pl.BlockSpec((3, None, tm, tc), lambda l, i, j: (0, l, i, j))],
        out_specs=pl.BlockSpec((None, tm, tc), lambda l, i, j: (lax.axis_index("c") * hl + l, i, j)),
        out_shape=jax.ShapeDtypeStruct((nl, r, c), F32),
        compiler_params=_params(("parallel",) * 3),
    )(cs, got)


def all_gather_xy(shard, name):
    r, c = shard.shape
    hr = r // 2
    assert r % 32 == 0

    def body(x_ref, out_ref, send_sems, recv_sems, local_sem):
        x, y, cc = _place()
        chips = _other_chips(x, y)
        mine = pl.ds(pl.multiple_of(cc * hr, 16), hr)
        theirs = pl.ds(pl.multiple_of((1 - cc) * hr, 16), hr)
        k_me = 2 * x + y

        def copy(k, src, dst, to):
            return pltpu.make_async_remote_copy(src_ref=src, dst_ref=dst, send_sem=send_sems.at[k], recv_sem=recv_sems.at[k],
                                                device_id=to, device_id_type=MESH)

        own = pltpu.make_async_copy(x_ref, out_ref.at[k_me], local_sem)
        own.start()
        first = [copy(j, x_ref.at[mine], out_ref.at[k_me, mine], (*chip, cc)) for j, chip in enumerate(chips)]
        for cp in first:
            cp.start()
        passed = []
        for j, (px, py) in enumerate(chips):
            landed = out_ref.at[2 * px + py, mine]
            copy(j, landed, landed, (px, py, cc)).wait_recv()
            fw = copy(3 + j, landed, landed, (x, y, 1 - cc))
            fw.start()
            passed.append(fw)
        for j, (px, py) in enumerate(chips):
            landed = out_ref.at[2 * px + py, theirs]
            copy(3 + j, landed, landed, (x, y, 1 - cc)).wait_recv()
        for cp in first + passed:
            cp.wait_send()
        own.wait()

    return pl.pallas_call(
        body,
        name=name,
        in_specs=[ANY],
        out_specs=ANY,
        out_shape=jax.ShapeDtypeStruct((NCHIP, r, c), shard.dtype),
        scratch_shapes=[pltpu.SemaphoreType.DMA((6,)), pltpu.SemaphoreType.DMA((6,)), pltpu.SemaphoreType.DMA],
    )(shard)


def all_gather_8(block, name):
    m, c = block.shape

    def body(x_ref, out_ref, send_sems, recv_sems, local_sem):
        x, y, cc = _place()
        me, sibling = (x, y, cc), (x, y, 1 - cc)
        chips = _other_chips(x, y)

        def rows(px, py, pc):
            return out_ref.at[4 * px + 2 * py + pc]

        def copy(k, blk, to, src=None):
            return pltpu.make_async_remote_copy(src_ref=rows(*blk) if src is None else src, dst_ref=rows(*blk), send_sem=send_sems.at[k],
                                                recv_sem=recv_sems.at[k], device_id=to, device_id_type=MESH)

        mine = pltpu.make_async_copy(x_ref, rows(*me), local_sem)
        mine.start()
        first = [copy(0, me, sibling, src=x_ref)]
        first += [copy(1 + j, me, (*chip, cc), src=x_ref) for j, chip in enumerate(chips)]
        for cp in first:
            cp.start()
        passed = [copy(4 + j, (*chip, cc), sibling) for j, chip in enumerate(chips)]
        for j, chip in enumerate(chips):
            copy(1 + j, (*chip, cc), me).wait_recv()
            passed[j].start()
        copy(0, sibling, me).wait_recv()
        for j, chip in enumerate(chips):
            copy(4 + j, (*chip, 1 - cc), me).wait_recv()
        for cp in first + passed:
            cp.wait_send()
        mine.wait()

    return pl.pallas_call(
        body,
        name=name,
        in_specs=[pl.BlockSpec(memory_space=pltpu.VMEM)],
        out_specs=pl.BlockSpec(memory_space=pltpu.VMEM),
        out_shape=jax.ShapeDtypeStruct((8, m, c), block.dtype),
        scratch_shapes=[pltpu.SemaphoreType.DMA((7,)), pltpu.SemaphoreType.DMA((7,)), pltpu.SemaphoreType.DMA],
        compiler_params=pltpu.CompilerParams(vmem_limit_bytes=VMEM_LIMIT),
    )(block)


def add_parts(parts, out_dtype, name, tm=512):
    def fn(i, nt, rows, vecs, prevs, nexts):
        acc = rows[0]
        for r_ in rows[1:]:
            acc = acc + r_
        return [acc], []
    r, c = parts[0].shape
    return ew(fn, list(parts), [], [(c, out_dtype)], tm=_tile(r, tm, 16), name=name)[0]


SMALL_SHARDED = ("sc_conv_w", "m_conv_w")
SMALL_REPL = ("ffn1_norm", "mix_norm", "m_conv_b", "m_dt_bias", "m_A_log", "m_D", "m_norm", "ffn2_norm", "ple_norm", "final_norm")
BIG = ("ffn1_wg", "ffn1_wu", "ffn1_wd", "w_in", "sc_w_out", "m_w_out", "w_o", "ffn2_wg", "ffn2_wu", "ffn2_wd", "ple_w_gate", "ple_w_proj")
TRANSPOSED = ("ffn1_wg", "ffn1_wu", "ffn2_wg", "ffn2_wu", "w_in")
ORDER = ("ffn1_norm", "ffn1_wg", "ffn1_wu", "ffn1_wd", "mix_norm", "w_in", "sc_conv_w", "sc_w_out", "m_conv_w", "m_conv_b", "m_dt_bias",
         "m_A_log", "m_D", "m_norm", "m_w_out", "w_o", "ffn2_norm", "ffn2_wg", "ffn2_wu", "ffn2_wd", "ple_norm", "ple_w_gate", "ple_w_proj",
         "final_norm")


def _pack(arrs, cols, row_mult):
    flat = jnp.concatenate([a.reshape(-1) for a in arrs])
    n = flat.shape[0]
    rows = -(-n // cols)
    rows = -(-rows // row_mult) * row_mult
    return jnp.pad(flat, (0, rows * cols - n)).reshape(rows, cols)


def _unpack(flat2d, shapes):
    flat = flat2d.reshape(-1)
    out, off = [], 0
    for s in shapes:
        n = int(np.prod(s))
        out.append(flat[off:off + n].reshape(s))
        off += n
    return out


def _row_cat(arrs, dtype):
    return jnp.concatenate([a.astype(dtype) for a in arrs], axis=1)


def kernel(x, p, ffn1_norm, ffn1_wg, ffn1_wu, ffn1_wd, mix_norm, w_in, sc_conv_w, sc_w_out, m_conv_w, m_conv_b, m_dt_bias, m_A_log, m_D, m_norm, m_w_out, w_o, ffn2_norm, ffn2_wg, ffn2_wu, ffn2_wd, ple_norm, ple_w_gate, ple_w_proj, final_norm, loss_target, m_ffn1_norm, m_ffn1_wg, m_ffn1_wu, m_ffn1_wd, m_mix_norm, m_w_in, m_sc_conv_w, m_sc_w_out, m_m_conv_w, m_m_conv_b, m_m_dt_bias, m_m_A_log, m_m_D, m_m_norm, m_m_w_out, m_w_o, m_ffn2_norm, m_ffn2_wg, m_ffn2_wu, m_ffn2_wd, m_ple_norm, m_ple_w_gate, m_ple_w_proj, m_final_norm, v_ffn1_norm, v_ffn1_wg, v_ffn1_wu, v_ffn1_wd, v_mix_norm, v_w_in, v_sc_conv_w, v_sc_w_out, v_m_conv_w, v_m_conv_b, v_m_dt_bias, v_m_A_log, v_m_D, v_m_norm, v_m_w_out, v_w_o, v_ffn2_norm, v_ffn2_wg, v_ffn2_wu, v_ffn2_wd, v_ple_norm, v_ple_w_gate, v_ple_w_proj, v_final_norm):
    args = dict(locals())
    wts = {n: args[n] for n in ORDER}
    mom = {n: args["m_" + n] for n in ORDER}
    vel = {n: args["v_" + n] for n in ORDER}

    depth = ffn1_norm.shape[0]
    d = x.shape[-1]
    w = 2 * d
    hh = w // SSM_P
    cw = w + 2 * SSM_G * SSM_N
    d4 = d // NCHIP
    pp = 7 * d + cw + LANE
    my_x, my_y, my_c = _place()
    k_me = 2 * my_x + my_y

    tr = lambda a: jnp.swapaxes(a, 1, 2)
    gu_t = [_row_cat([tr(wg_), tr(wu_)], BF) for wg_, wu_ in ((ffn1_wg, ffn1_wu), (ffn2_wg, ffn2_wu))]
    wd_l = [ffn1_wd.astype(BF), ffn2_wd.astype(BF)]
    w1024_l = _row_cat([m_w_out, sc_w_out, w_o, ple_w_gate], BF)
    p4 = w_in.shape[2]
    p4p = -(-p4 // 32) * 32
    win_l, wpp_l = jnp.pad(tr(w_in).astype(BF), ((0, 0), (0, p4p - p4), (0, 0))), ple_w_proj.astype(BF)
    halves = lambda a: a.reshape(2, a.shape[0] // 2, a.shape[1])
    whole = lambda g: g.reshape(NCHIP, g.shape[2] * 2, g.shape[3])

    def pieces(l):
        return {"small": [halves(w1024_l[l]), halves(wpp_l[l])], "win": [halves(win_l[l])], "gu1": [halves(gu_t[0][l])], "d1": [halves(wd_l[0][l])],
                "gu2": [halves(gu_t[1][l])], "d2": [halves(wd_l[1][l])]}

    small_local = [sc_conv_w, m_conv_w]
    gathered_s = all_gather_xy(_pack(small_local, LANE, 32), "gather_conv_weights")
    per_shard_s = [_unpack(gathered_s[k], [a.shape for a in small_local]) for k in range(NCHIP)]
    sc_conv_full = jnp.concatenate([per_shard_s[k][0] for k in range(NCHIP)], axis=2)
    m_conv_full = jnp.concatenate([per_shard_s[k][1] for k in range(NCHIP)], axis=2)

    pad_h = lambda a: jnp.pad(a, ((0, 0), (0, LANE - hh)))
    dt_bias_p, a_log_p = pad_h(m_dt_bias), pad_h(m_A_log)
    d_exp = jnp.repeat(m_D, SSM_P, axis=1)
    e_mat = (jnp.arange(w)[None, :] // SSM_P == jnp.arange(LANE)[:, None]).astype(F32)
    et_mat = e_mat.T
    o_z, o_xbc, o_dt, o_g = 3 * d, 5 * d, 5 * d + cw, 5 * d + cw + hh

    def layer_weights(got):
        wt = {"w1024": whole(got["small"][0]), "wpp": whole(got["small"][1])}
        wt.update({k: whole(got[k][0]) for k in ("gu1", "d1", "gu2", "d2")})
        gw = whole(got["win"][0])

        def wi(lo, hi):
            parts = [gw[k, max(lo - k * p4, 0):min(hi - k * p4, p4)] for k in range(NCHIP) if lo < (k + 1) * p4 and hi > k * p4]
            return parts[0] if len(parts) == 1 else jnp.concatenate(parts, axis=0)

        wt["sc3"], wt["z"], wt["xbc"], wt["g2"] = wi(0, o_z), wi(o_z, o_xbc), wi(o_xbc, o_dt), wi(o_g, o_g + 2 * d)
        wt["dt"] = jnp.pad(wi(o_dt, o_g), ((0, LANE - hh), (0, 0)))
        wt["in_p"] = jnp.concatenate([wt["g2"], wt["z"], wt["sc3"][d:], wt["xbc"], wt["sc3"][:d], wt["dt"]], axis=0)
        return wt

    first = pieces(0)
    order = ("gu1", "d1", "win", "small", "gu2", "d2")
    flat = gather_rider([a for k in order for a in first[k]]).standalone("gather_weights")
    got, pos = {}, 0
    for k in order:
        got[k] = flat[pos:pos + len(first[k])]
        pos += len(first[k])
    wts_l = [layer_weights(got)]

    h = x[0]
    saved = []
    for i in range(depth):
        s, wt = {}, wts_l[i]
        nxt = pieces(i + 1) if i + 1 < depth else None
        ride = lambda k: gather_rider(nxt[k]) if nxt else None
        got = {}
        s["h0"] = h
        (s["ab1"], s4, s["n1"]), got["small"] = ffn_up(h, ffn1_norm[i:i + 1], wt["gu1"], rider=ride("small"))
        h, got["d1"] = ffn_down(s4, wt["d1"], h, rider=ride("d1"))
        s["h1"] = h
        u = norm_cast(h, mix_norm[i:i + 1])
        s["u"] = u
        s["sc3"] = mm(u, wt["sc3"], tb=True, out_dtype=BF, name="proj_sc")
        s["z"] = mm(u, wt["z"], tb=True, out_dtype=BF, name="proj_z")
        s["xbc_raw"] = mm(u, wt["xbc"], tb=True, out_dtype=BF, name="proj_xbc")
        s["gates"] = mm(u, wt["g2"], tb=True, out_dtype=BF, name="proj_gates")
        s["dt_raw"] = mm(u, wt["dt"], tb=True, name="proj_dt")
        s["ya_in"] = conv_a_fwd(s["sc3"], sc_conv_full[i])
        s["xbc"], s["dt"] = conv_m_fwd(s["xbc_raw"], s["dt_raw"], m_conv_full[i], m_conv_b[i:i + 1], dt_bias_p[i:i + 1])
        (s["yn"], s["y"], s["sprev"]), got["win"] = ssd_fwd(s["xbc"], s["dt"], s["z"], a_log_p[i:i + 1], d_exp[i:i + 1], m_norm[i:i + 1], e_mat,
                                                            rider=ride("win"))
        (h, s["y_a"], s["y_m"], s["merged"]), got["gu2"] = mix_out_fwd(s["ya_in"], s["yn"], s["gates"], h, wt["w1024"], rider=ride("gu2"))
        s["h2"] = h
        (s["ab2"], s4, s["n2"]), got["gu1"] = ffn_up(h, ffn2_norm[i:i + 1], wt["gu2"], rider=ride("gu1"))
        h, got["d2"] = ffn_down(s4, wt["d2"], h, rider=ride("d2"))
        s["h3"] = h
        h = ple_fwd(h, ple_norm[i:i + 1], p[i, 0], wt["w1024"], wt["wpp"])
        saved.append(s)
        if nxt:
            wts_l.append(layer_weights(got))

    dh, loss_lanes, g_final = loss_bwd(h, final_norm[None, :], loss_target[0])
    loss = lax.psum(jnp.sum(loss_lanes), ("x", "y", "c"))

    def finish_reduce(cs, got):
        halves = [add_chips(c_, g_, 2, "grad_add_chips") for c_, g_ in zip(cs, got, strict=True)]
        return [f.reshape(-1, f.shape[2]) for f in join_packs(halves, "grad_join_halves")]

    g_layer, pending, reduced = None, None, [None] * depth
    gs = {n: [None] * depth for n in SMALL_SHARDED + SMALL_REPL if n != "final_norm"}
    for i in reversed(range(depth)):
        s = saved[i]
        wt = wts_l[i]
        dh, gs["ple_norm"][i], n3, dgp, dpe = ple_bwd(dh, s["h3"], ple_norm[i:i + 1], p[i, 0], wt["w1024"], wt["wpp"])
        g_pg = mm(n3, dgp, ta=True, out_dtype=BF, name="g_ple_gate", tm_cap=512, tn_cap=512)
        g_pp = mm(p[i, 0], dpe, ta=True, out_dtype=BF, name="g_ple_proj", tm_cap=512, tn_cap=512)
        g_pp = jnp.transpose(g_pp.reshape(g_pp.shape[0], NCHIP, d4), (1, 0, 2))
        (dn2, s2, dab2), from_sibling = ffn_bwd(dh, s["ab2"], wt["gu2"], wt["d2"], rider=swap_rider(g_layer) if g_layer else None)
        if g_layer:
            pending = [add_sibling(g, r_, "grad_add_sibling") for g, r_ in zip(g_layer, from_sibling, strict=True)]
        g_ffn2 = ffn_wgrads(s["n2"], dh, s2, dab2)
        dh, gs["ffn2_norm"][i] = norm_bwd_add(dh, s["h2"], ffn2_norm[i:i + 1], dn2)
        dproj, dya, dyn, dy_a, dy_m = mix_out_bwd(dh, s["gates"], s["y_a"], s["y_m"], wt["w1024"], pp)
        g_wo = mm(s["merged"], dh, ta=True, out_dtype=BF, name="g_w_o", tm_cap=512, tn_cap=512)
        g_sco = mm(s["ya_in"], dy_a, ta=True, out_dtype=BF, name="g_sc_out", tm_cap=512, tn_cap=512)
        g_mo = mm(s["yn"], dy_m, ta=True, out_dtype=BF, name="g_m_out", tm_cap=512, tn_cap=512)
        g_1024 = jnp.concatenate([g_mo.reshape(NCHIP, 2 * d4, d), g_sco.reshape(NCHIP, d4, d), g_wo.reshape(NCHIP, d4, d),
                                  g_pg.reshape(NCHIP, d4, d)], axis=1)
        (dproj, dxbc, ddt, gs["m_norm"][i], gd, gal), got_a = ssd_bwd(dyn, s["y"], s["z"], s["xbc"], s["dt"], s["sprev"], a_log_p[i:i + 1], d_exp[i:i + 1],
                                                                      m_norm[i:i + 1], e_mat, et_mat, Window(0, 1, pp, dproj),
                                                                      rider=scatter_rider(pending[:1]) if pending else None)
        gs["m_D"][i], gs["m_A_log"][i] = gd[:, :hh], gal[:, :hh]
        dpre, dproj, gdb = conv_m_bwd1(dxbc, s["xbc_raw"], ddt, s["dt_raw"], m_conv_full[i], m_conv_b[i:i + 1], dt_bias_p[i:i + 1],
                                       Window(1, (7 * d + cw) // LANE, pp, dproj))
        gs["m_dt_bias"][i] = gdb[:, :hh]
        dproj, gs["m_conv_w"][i], gs["m_conv_b"][i] = conv_bwd2(dpre, s["xbc_raw"], m_conv_full[i], "conv_m_bwd2", Window(0, 6 * d // cw, pp, dproj))
        dcv, dproj, v = conv_a_bwd1(dya, s["sc3"], sc_conv_full[i], Window(1, (6 * d + cw) // d, pp, dproj))
        dproj, gs["sc_conv_w"][i] = conv_a_bwd2(dcv, v, s["sc3"], sc_conv_full[i], Window(0, 2, pp, dproj))
        if pending:
            du, got_b = mm(dproj, wt["in_p"], name="d_proj_in", tn_cap=512, rider=scatter_rider(pending[2:3]))
            gwp, got_c = mm(dproj, s["u"], ta=True, out_dtype=BF, name="g_w_in", tm_cap=1152, tn_cap=512, rider=scatter_rider(pending[1:2] + pending[3:]))
            reduced[i + 1] = finish_reduce(pending, [got_a[0], got_c[0], got_b[0], got_c[1]])
        else:
            du = mm(dproj, wt["in_p"], name="d_proj_in", tn_cap=512)
            gwp = mm(dproj, s["u"], ta=True, out_dtype=BF, name="g_w_in", tm_cap=1152, tn_cap=512)
        gw_rows = jnp.concatenate([gwp[6 * d + cw:7 * d + cw], gwp[4 * d:6 * d], gwp[2 * d:4 * d], gwp[6 * d:6 * d + cw], gwp[7 * d + cw:7 * d + cw + hh],
                                   gwp[:2 * d]], axis=0)
        g_in = jnp.pad(gw_rows.reshape(NCHIP, p4, d), ((0, 0), (0, p4p - p4), (0, 0)))
        dh, gs["mix_norm"][i] = norm_bwd_add(dh, s["h1"], mix_norm[i:i + 1], du)
        (dn1, s1, dab1), _ = ffn_bwd(dh, s["ab1"], wt["gu1"], wt["d1"])
        g_ffn1 = ffn_wgrads(s["n1"], dh, s1, dab1)
        dh, gs["ffn1_norm"][i] = norm_bwd_add(dh, s["h0"], ffn1_norm[i:i + 1], dn1)
        g_layer = [jnp.concatenate([g_ffn1, g_ffn2], axis=1), g_1024, g_in, g_pp]
        g_layer = [g.reshape(NCHIP, 2, g.shape[1] // 2, g.shape[2]) for g in g_layer]
    from_sibling = swap_rider(g_layer).standalone("grad_swap_halves")
    pending = [add_sibling(g, r_, "grad_add_sibling") for g, r_ in zip(g_layer, from_sibling, strict=True)]
    reduced[0] = finish_reduce(pending, scatter_packs(pending, "grad_scatter"))
    grad_x = dh[None]

    f4 = reduced[0][0].shape[0] // 6
    rows_of = lambda j, lo, hi: jnp.stack([reduced[l][j][lo:hi] for l in range(depth)])
    ffn_rows = lambda j: rows_of(0, j * f4, (j + 1) * f4)
    grads = {
        "ffn1_wg": ffn_rows(0), "ffn1_wu": ffn_rows(1), "ffn1_wd": ffn_rows(2), "ffn2_wg": ffn_rows(3), "ffn2_wu": ffn_rows(4), "ffn2_wd": ffn_rows(5),
        "m_w_out": rows_of(1, 0, 2 * d4), "sc_w_out": rows_of(1, 2 * d4, 3 * d4), "w_o": rows_of(1, 3 * d4, 4 * d4), "ple_w_gate": rows_of(1, 4 * d4, 5 * d4),
        "w_in": rows_of(2, 0, p4), "ple_w_proj": rows_of(3, 0, None),
    }

    small_names = list(SMALL_SHARDED + SMALL_REPL)
    small_full = [g_final[0] if n == "final_norm" else jnp.stack(gs[n]) for n in small_names]
    small_pack = _pack(small_full, LANE, HALO)
    all8 = all_gather_8(small_pack, "gather_small_grads")
    small_sum = add_parts([all8[k] for k in range(8)], F32, "add_small_grads", tm=256)
    for n, tot in zip(small_names, _unpack(small_sum, [a.shape for a in small_full]), strict=True):
        if n in SMALL_SHARDED:
            cl = wts[n].shape[2]
            grads[n] = lax.dynamic_slice_in_dim(tot, k_me * cl, cl, axis=2)
        else:
            grads[n] = tot.reshape(wts[n].shape)

    delta, new_m, new_v = {}, {}, {}
    for n in BIG:
        view = tr if n in TRANSPOSED else (lambda a: a)
        shp = grads[n].shape
        two = lambda a: a.reshape(-1, shp[-1])
        dl, nm, nv = adamw(two(view(wts[n])), two(grads[n]), two(view(mom[n])), two(view(vel[n])), "adamw_" + "x".join(map(str, shp[1:])))
        grads[n], delta[n], new_m[n], new_v[n] = view(grads[n]), view(dl.reshape(shp)), view(nm.reshape(shp)), view(nv.reshape(shp))
    for n in small_names:
        shp = wts[n].shape
        two = lambda a: a.reshape(-1, shp[-1])
        dl, nm, nv = adamw(two(wts[n]), two(grads[n]), two(mom[n]), two(vel[n]), "adamw_small_" + "x".join(map(str, shp)))
        delta[n], new_m[n], new_v[n] = dl.reshape(shp), nm.reshape(shp), nv.reshape(shp)

    return (loss, grad_x, *[grads[n] for n in ORDER], *[delta[n] for n in ORDER], *[new_m[n] for n in ORDER], *[new_v[n] for n in ORDER])
```

```python
import jax
import jax.numpy as jnp
import numpy as np
from jax import lax
from jax.experimental import pallas as pl
from jax.experimental.pallas import tpu as pltpu

BF = jnp.bfloat16
F32 = jnp.float32
EPS = 1e-6
LANE = 128
HALO = 8
SSM_P = 64
SSM_N = 128
SSM_G = 4
SSM_L = 128
ADAM_LR, ADAM_B1, ADAM_B2, ADAM_EPS, ADAM_WD, ADAM_STEP = 0.001, 0.9, 0.999, 1e-08, 0.01, 10
VMEM_LIMIT = 56 * 1024 * 1024
TILE_ELEMS = 400_000
NCHIP = 4
FFN_SUB = 256
MESH = pl.DeviceIdType.MESH
HI = lax.Precision.HIGHEST


def _tile(n, cap, mult=LANE):
    best = None
    t = mult
    while t <= min(n, cap):
        if n % t == 0:
            best = t
        t += mult
    return best if best is not None else n


def _row_tile(r, c, mult=16):
    return _tile(r, max(mult, TILE_ELEMS // c // mult * mult), mult)


def _tile2(r, c, mult=16):
    tm = _row_tile(r, c, mult)
    tc = c if tm * c <= TILE_ELEMS else _tile(c, max(LANE, TILE_ELEMS // tm // LANE * LANE))
    return tm, tc


def _params(sem):
    return pltpu.CompilerParams(dimension_semantics=sem, vmem_limit_bytes=VMEM_LIMIT)


def _sigmoid(x):
    return 1.0 / (1.0 + jnp.exp(-x))


def _dot(a, b, ca=1, cb=0, precision=None):
    return lax.dot_general(a, b, (((ca,), (cb,)), ((), ())), precision=precision, preferred_element_type=F32)


def _rms(x, g):
    r = lax.rsqrt(jnp.mean(x * x, axis=-1, keepdims=True) + EPS)
    return x * r * g


def _rms_bwd(x, g, dy):
    r = lax.rsqrt(jnp.mean(x * x, axis=-1, keepdims=True) + EPS)
    xh = x * r
    dxh = dy * g
    dx = r * (dxh - xh * jnp.mean(dxh * xh, axis=-1, keepdims=True))
    return dx, jnp.sum(dy * xh, axis=0, keepdims=True)


def _accumulate(ref, val, first):
    @pl.when(first)
    def _():
        ref[...] = val

    @pl.when(jnp.logical_not(first))
    def _():
        ref[...] += val


RIDER_MID = 1.0


class Rider:
    def __init__(self, ins, out_shapes, n_sems, start, finish, mid=None):
        self.ins, self.out_shapes, self.n_sems, self.start, self.mid, self.finish = list(ins), list(out_shapes), n_sems, start, mid, finish

    def standalone(self, name):
        ni, no = len(self.ins), len(self.out_shapes)

        def body(*refs):
            parts = (refs[:ni], refs[ni:ni + no], *refs[ni + no:])
            self.start(*parts)
            if self.mid is not None:
                self.mid(*parts)
            self.finish(*parts)

        return _comm_call(body, name, self.ins, self.out_shapes, self.n_sems)


def host_call(body, *, name, grid, in_specs, out_specs, out_shape, scratch_shapes, operands, rider=None, aliases=None):
    n_in, n_out = len(in_specs), len(out_specs)
    aliases = aliases or {}
    if rider is None:
        outs = pl.pallas_call(body, name=name, grid=grid, in_specs=in_specs, out_specs=out_specs, out_shape=out_shape, scratch_shapes=scratch_shapes,
                              input_output_aliases=aliases, compiler_params=_params(("arbitrary",) * len(grid)))(*operands)
        return list(outs), []
    ri, ro = len(rider.ins), len(rider.out_shapes)

    def hosted(*refs):
        ins, r_ins = refs[:n_in], refs[n_in:n_in + ri]
        outs, r_outs = refs[n_in + ri:n_in + ri + n_out], refs[n_in + ri + n_out:n_in + ri + n_out + ro]
        scratch, (send_sems, recv_sems) = refs[n_in + ri + n_out + ro:-2], refs[-2:]
        step, total = 0, 1
        for ax, n in enumerate(grid):
            step = step * n + pl.program_id(ax)
            total *= n
        parts = (r_ins, r_outs, send_sems, recv_sems)

        @pl.when(step == 0)
        def _():
            rider.start(*parts)

        if rider.mid is not None:
            @pl.when(step == min(total - 1, int(total * RIDER_MID)))
            def _():
                rider.mid(*parts)

        body(*ins, *outs, *scratch)

        @pl.when(step == total - 1)
        def _():
            rider.finish(*parts)

    outs = pl.pallas_call(
        hosted,
        name=name,
        grid=grid,
        in_specs=list(in_specs) + [ANY] * ri,
        out_specs=list(out_specs) + [ANY] * ro,
        out_shape=list(out_shape) + rider.out_shapes,
        scratch_shapes=list(scratch_shapes) + [pltpu.SemaphoreType.DMA((rider.n_sems,)), pltpu.SemaphoreType.DMA((rider.n_sems,))],
        input_output_aliases=aliases,
        compiler_params=_params(("arbitrary",) * len(grid)),
    )(*operands, *rider.ins)
    return list(outs[:n_out]), list(outs[n_out:])


def mmx(name, a, b, *, grid, a_spec, b_spec, o_spec, o_shape, o_dtype, ca, cb, acc_shape=None, scale=None, rider=None):
    nk = grid[-1] if acc_shape is not None else 1
    assert scale is None or nk == 1

    def body(a_ref, b_ref, o_ref, *acc):
        p = _dot(a_ref[...].astype(BF), b_ref[...].astype(BF), ca, cb)
        if scale is not None:
            p = p * scale
        if nk == 1:
            o_ref[...] = p.astype(o_ref.dtype)
        else:
            kk = pl.program_id(len(grid) - 1)
            _accumulate(acc[0], p, kk == 0)

            @pl.when(kk == nk - 1)
            def _():
                o_ref[...] = acc[0][...].astype(o_ref.dtype)

    if rider is not None:
        (out,), r_outs = host_call(body, name=name, grid=grid, in_specs=[a_spec, b_spec], out_specs=[o_spec], out_shape=[jax.ShapeDtypeStruct(o_shape, o_dtype)],
                                   scratch_shapes=[pltpu.VMEM(acc_shape, F32)] if nk > 1 else [], operands=(a, b), rider=rider)
        return out, r_outs
    sem = ("parallel",) * (len(grid) - 1) + ("arbitrary" if nk > 1 else "parallel",)
    return pl.pallas_call(
        body,
        name=name,
        grid=grid,
        in_specs=[a_spec, b_spec],
        out_specs=o_spec,
        out_shape=jax.ShapeDtypeStruct(o_shape, o_dtype),
        scratch_shapes=[pltpu.VMEM(acc_shape, F32)] if nk > 1 else [],
        compiler_params=_params(sem),
    )(a, b)


def mm(a, b, *, ta=False, tb=False, out_dtype=F32, name, tm_cap=1024, tn_cap=1024, tk_cap=4096, rider=None):
    m, k = (a.shape[1], a.shape[0]) if ta else a.shape
    n = b.shape[0] if tb else b.shape[1]
    assert (b.shape[1] if tb else b.shape[0]) == k
    tm, tn, tk = _tile(m, tm_cap), _tile(n, tn_cap), _tile(k, tk_cap)
    nk = k // tk
    a_spec = pl.BlockSpec((tk, tm), lambda i, j, kk: (kk, i)) if ta else pl.BlockSpec((tm, tk), lambda i, j, kk: (i, kk))
    b_spec = pl.BlockSpec((tn, tk), lambda i, j, kk: (j, kk)) if tb else pl.BlockSpec((tk, tn), lambda i, j, kk: (kk, j))
    return mmx(name, a, b, grid=(m // tm, n // tn, nk), a_spec=a_spec, b_spec=b_spec, o_spec=pl.BlockSpec((tm, tn), lambda i, j, kk: (i, j)),
               o_shape=(m, n), o_dtype=out_dtype, ca=0 if ta else 1, cb=1 if tb else 0, acc_shape=(tm, tn) if nk > 1 else None, rider=rider)


class Window:
    def __init__(self, out, block, cols, buf=None):
        self.out, self.block, self.cols, self.buf = out, block, cols, buf


def ew(fn, rows, vecs, out_rows, out_red=(), *, tm, name, prev_halo=(), next_halo=(), window=None):
    t = rows[0].shape[0]
    tm = min(tm, t)
    nt = t // tm
    assert t % tm == 0 and (tm % HALO == 0 or (tm == t and not prev_halo and not next_halo))
    nr, nv, npv, nnx, nor = len(rows), len(vecs), len(prev_halo), len(next_halo), len(out_rows)
    hb = tm // HALO
    n_in = nr + nv + npv + nnx
    passed = window is not None and window.buf is not None

    def body(*refs):
        i = pl.program_id(0)
        ins = [r[...].astype(F32) for r in refs[:n_in]]
        outs = refs[n_in + passed:]
        o_rows, o_red = fn(i, nt, ins[:nr], ins[nr:nr + nv], ins[nr + nv:nr + nv + npv], ins[nr + nv + npv:])
        for ref, val in zip(outs[:nor], o_rows, strict=True):
            ref[...] = val.astype(ref.dtype)
        for ref, val in zip(outs[nor:], o_red, strict=True):
            _accumulate(ref, val, i == 0)

    in_specs = [pl.BlockSpec((tm, r.shape[1]), lambda i: (i, 0)) for r in rows]
    in_specs += [pl.BlockSpec(v.shape, lambda i: (0, 0)) for v in vecs]
    in_specs += [pl.BlockSpec((HALO, rows[k].shape[1]), lambda i: (jnp.maximum(i * hb - 1, 0), 0)) for k in prev_halo]
    in_specs += [pl.BlockSpec((HALO, rows[k].shape[1]), lambda i: (jnp.minimum((i + 1) * hb, t // HALO - 1), 0)) for k in next_halo]
    out_specs = [pl.BlockSpec((tm, c), lambda i: (i, 0)) for c, _ in out_rows]
    out_specs += [pl.BlockSpec(s, lambda i: (0, 0)) for s in out_red]
    out_shape = [jax.ShapeDtypeStruct((t, c), d) for c, d in out_rows] + [jax.ShapeDtypeStruct(s, F32) for s in out_red]
    operands = [*rows, *vecs, *[rows[k] for k in prev_halo], *[rows[k] for k in next_halo]]
    aliases = {}
    if window is not None:
        c, dt_ = out_rows[window.out]
        out_specs[window.out] = pl.BlockSpec((tm, c), lambda i: (i, window.block))
        out_shape[window.out] = jax.ShapeDtypeStruct((t, window.cols), dt_)
        if passed:
            in_specs.append(ANY)
            operands.append(window.buf)
            aliases = {n_in: window.out}
    return pl.pallas_call(
        body,
        name=name,
        grid=(nt,),
        in_specs=in_specs,
        out_specs=out_specs,
        out_shape=out_shape,
        input_output_aliases=aliases,
        compiler_params=_params(("arbitrary",) if out_red else ("parallel",)),
    )(*operands)


def _shift_down(x, prev, j):
    if j == 0:
        return x
    r = pltpu.roll(x, j, 0)
    rh = pltpu.roll(prev, j, 0)
    row = lax.broadcasted_iota(jnp.int32, (HALO, x.shape[1]), 0)
    head = jnp.where(row < j, rh, r[:HALO])
    return jnp.concatenate([head, r[HALO:]], axis=0)


def _shift_up(x, nxt, j):
    if j == 0:
        return x
    n = x.shape[0]
    r = pltpu.roll(x, n - j, 0)
    rh = pltpu.roll(nxt, HALO - j, 0)
    row = lax.broadcasted_iota(jnp.int32, (HALO, x.shape[1]), 0)
    tail = jnp.where(row >= HALO - j, rh, r[n - HALO:])
    return jnp.concatenate([r[: n - HALO], tail], axis=0)


def _conv_fwd(x, prev, w):
    kk = w.shape[0]
    acc = None
    for k in range(kk):
        term = w[k:k + 1, :] * _shift_down(x, prev, kk - 1 - k)
        acc = term if acc is None else acc + term
    return acc


def ffn_up(h, g, wf, rider=None):
    t, d = h.shape
    f4 = wf.shape[1] // 2
    tm = _tile(t, 1024)
    sub = _tile(tm, FFN_SUB, 16)

    def body(h_ref, g_ref, wg_ref, wu_ref, ab_ref, s_ref, n_ref):
        @pl.when(pl.program_id(1) == 0)
        def _():
            n_ref[...] = _rms(h_ref[...], g_ref[...]).astype(BF)

        for r in range(tm // sub):
            rows = slice(r * sub, (r + 1) * sub)
            n = n_ref[rows, :]
            a = _dot(n, wg_ref[...], 1, 1)
            b = _dot(n, wu_ref[...], 1, 1)
            ab_ref[0, rows, :] = a.astype(BF)
            ab_ref[1, rows, :] = b.astype(BF)
            s_ref[rows, :] = (a * _sigmoid(a) * b).astype(BF)

    wspec = lambda ib: pl.BlockSpec((None, f4, d), lambda i, j: (j, ib, 0))
    return host_call(
        body,
        name="ffn_up",
        grid=(t // tm, NCHIP),
        in_specs=[pl.BlockSpec((tm, d), lambda i, j: (i, 0)), pl.BlockSpec((1, d), lambda i, j: (0, 0)), wspec(0), wspec(1)],
        out_specs=[pl.BlockSpec((2, None, tm, f4), lambda i, j: (0, j, i, 0)), pl.BlockSpec((None, tm, f4), lambda i, j: (j, i, 0)),
                   pl.BlockSpec((tm, d), lambda i, j: (i, 0))],
        out_shape=[jax.ShapeDtypeStruct((2, NCHIP, t, f4), BF), jax.ShapeDtypeStruct((NCHIP, t, f4), BF), jax.ShapeDtypeStruct((t, d), BF)],
        scratch_shapes=[],
        operands=(h, g, wf, wf),
        rider=rider,
    )


def ffn_down(s4, wf, h, rider=None):
    t, d = h.shape
    f4 = s4.shape[2]
    tm = _tile(t, 512)

    def body(s_ref, w_ref, h_ref, o_ref):
        acc = _dot(s_ref[0], w_ref[0])
        for k in range(1, NCHIP):
            acc = acc + _dot(s_ref[k], w_ref[k])
        o_ref[...] = h_ref[...] + 0.5 * acc

    (out,), r_outs = host_call(
        body,
        name="ffn_down",
        grid=(t // tm,),
        in_specs=[pl.BlockSpec((NCHIP, tm, f4), lambda i: (0, i, 0)), pl.BlockSpec((NCHIP, f4, d), lambda i: (0, 0, 0)), pl.BlockSpec((tm, d), lambda i: (i, 0))],
        out_specs=[pl.BlockSpec((tm, d), lambda i: (i, 0))],
        out_shape=[jax.ShapeDtypeStruct((t, d), F32)],
        scratch_shapes=[],
        operands=(s4, wf, h),
        rider=rider,
    )
    return out, r_outs


def ffn_bwd(dho, ab, wf, wd, rider=None):
    t, d = dho.shape
    f4 = wd.shape[1]
    tm = _tile(t, 1024)
    sub = _tile(tm, FFN_SUB, 16)

    def body(dho_ref, ab_ref, wg_ref, wu_ref, wd_ref, dn_ref, s_ref, dab_ref, do_sc):
        j = pl.program_id(1)

        @pl.when(j == 0)
        def _():
            do_sc[...] = (0.5 * dho_ref[...]).astype(BF)
            dn_ref[...] = jnp.zeros_like(dn_ref)

        for r in range(tm // sub):
            rows = slice(r * sub, (r + 1) * sub)
            ds = _dot(do_sc[rows, :], wd_ref[...], 1, 1)
            av, bv = ab_ref[0, rows, :].astype(F32), ab_ref[1, rows, :].astype(F32)
            sig = _sigmoid(av)
            sl = av * sig
            s_ref[rows, :] = (sl * bv).astype(BF)
            da = (ds * bv * (sig * (1.0 + av * (1.0 - sig)))).astype(BF)
            db = (ds * sl).astype(BF)
            dab_ref[0, rows, :] = da
            dab_ref[1, rows, :] = db
            dn_ref[rows, :] += _dot(da, wg_ref[...]) + _dot(db, wu_ref[...])

    row = lambda c: pl.BlockSpec((tm, c), lambda i, j: (i, 0))
    wspec = lambda ib: pl.BlockSpec((None, f4, d), lambda i, j: (j, ib, 0))
    ab_spec = pl.BlockSpec((2, None, tm, f4), lambda i, j: (0, j, i, 0))
    return host_call(
        body,
        name="ffn_bwd",
        grid=(t // tm, NCHIP),
        in_specs=[row(d), ab_spec, wspec(0), wspec(1), wspec(0)],
        out_specs=[row(d), pl.BlockSpec((None, tm, f4), lambda i, j: (j, i, 0)), ab_spec],
        out_shape=[jax.ShapeDtypeStruct((t, d), F32), jax.ShapeDtypeStruct((NCHIP, t, f4), BF), jax.ShapeDtypeStruct((2, NCHIP, t, f4), BF)],
        scratch_shapes=[pltpu.VMEM((tm, d), BF)],
        operands=(dho, ab, wf, wf, wd),
        rider=rider,
    )


def ffn_wgrads(n, dho, s4, dab):
    t, d = n.shape
    f4 = s4.shape[2]
    tn = _tile(d, 512)
    g_in = mmx("g_ffn_in", dab, n, grid=(2, NCHIP, 1), a_spec=pl.BlockSpec((None, None, t, f4), lambda wh, k, j: (wh, k, 0, 0)),
               b_spec=pl.BlockSpec((t, d), lambda wh, k, j: (0, 0)), o_spec=pl.BlockSpec((None, None, f4, d), lambda wh, k, j: (k, wh, 0, 0)),
               o_shape=(NCHIP, 2, f4, d), o_dtype=BF, ca=0, cb=0)
    g_out = mmx("g_ffn_out", s4, dho, grid=(NCHIP, d // tn), a_spec=pl.BlockSpec((None, t, f4), lambda k, j: (k, 0, 0)),
                b_spec=pl.BlockSpec((t, tn), lambda k, j: (0, j)), o_spec=pl.BlockSpec((None, f4, tn), lambda k, j: (k, 0, j)),
                o_shape=(NCHIP, f4, d), o_dtype=BF, ca=0, cb=0, scale=0.5)
    return jnp.concatenate([g_in.reshape(NCHIP, 2 * f4, d), g_out], axis=1)


def norm_cast(h, g):
    def fn(i, nt, rows, vecs, prevs, nexts):
        return [_rms(rows[0], vecs[0])], []
    return ew(fn, [h], [g], [(h.shape[1], BF)], tm=512, name="norm_cast")[0]


def _zero_if(cond, x):
    return jnp.where(cond, jnp.zeros_like(x), x)


def conv_a_fwd(sc3, w_sc):
    d = sc3.shape[1] // 3

    def fn(i, nt, rows, vecs, prevs, nexts):
        x, pv = rows[0], _zero_if(i == 0, prevs[0])
        v = x[:, d:2 * d] * x[:, 2 * d:]
        vp = pv[:, d:2 * d] * pv[:, 2 * d:]
        return [x[:, :d] * _conv_fwd(v, vp, vecs[0])], []

    return ew(fn, [sc3], [w_sc], [(d, BF)], tm=256, name="conv_a_fwd", prev_halo=(0,))[0]


def _softplus(x):
    e = jnp.exp(-jnp.abs(x))
    return jnp.maximum(x, 0.0) + jnp.where(e < 1e-4, e - 0.5 * e * e, jnp.log(1.0 + e))


def conv_m_fwd(xbc_raw, dt_raw, w_mc, b_mc, dt_bias):
    def fn(i, nt, rows, vecs, prevs, nexts):
        pre = _conv_fwd(rows[0], _zero_if(i == 0, prevs[0]), vecs[0]) + vecs[1]
        return [pre * _sigmoid(pre), _softplus(rows[1] + vecs[2])], []

    return ew(fn, [xbc_raw, dt_raw], [w_mc, b_mc, dt_bias], [(xbc_raw.shape[1], F32), (LANE, F32)], tm=256, name="conv_m_fwd",
              prev_halo=(0,))


def conv_m_bwd1(dxbc, xbc_raw, ddt, dt_raw, w_mc, b_mc, dt_bias, window):
    def fn(i, nt, rows, vecs, prevs, nexts):
        pre = _conv_fwd(rows[1], _zero_if(i == 0, prevs[0]), vecs[0]) + vecs[1]
        sig = _sigmoid(pre)
        dpre = rows[0] * (sig * (1.0 + pre * (1.0 - sig)))
        ddr = rows[2] * _sigmoid(rows[3] + vecs[2])
        return [dpre, ddr], [jnp.sum(ddr, axis=0, keepdims=True)]

    return ew(fn, [dxbc, xbc_raw, ddt, dt_raw], [w_mc, b_mc, dt_bias], [(dxbc.shape[1], F32), (LANE, BF)], [(1, LANE)], tm=256,
              name="conv_m_bwd1", prev_halo=(1,), window=window)


def conv_bwd2(dpre, x, w, name, window):
    kk = w.shape[0]

    def fn(i, nt, rows, vecs, prevs, nexts):
        dp, xv = rows[0], rows[1]
        nx = _zero_if(i == nt - 1, nexts[0])
        dx = None
        dws = []
        for k in range(kk):
            up = _shift_up(dp, nx, kk - 1 - k)
            term = vecs[0][k:k + 1, :] * up
            dx = term if dx is None else dx + term
            dws.append(jnp.sum(up * xv, axis=0, keepdims=True))
        return [dx], [jnp.concatenate(dws, axis=0), jnp.sum(dp, axis=0, keepdims=True)]

    c = x.shape[1]
    return ew(fn, [dpre, x], [w], [(c, BF)], [(kk, c), (1, c)], tm=256, name=name, next_halo=(0,), window=window)


def conv_a_bwd1(dya, sc3, w_sc, window):
    d = sc3.shape[1] // 3

    def fn(i, nt, rows, vecs, prevs, nexts):
        x, pv = rows[1], _zero_if(i == 0, prevs[0])
        v = x[:, d:2 * d] * x[:, 2 * d:]
        vp = pv[:, d:2 * d] * pv[:, 2 * d:]
        return [rows[0] * x[:, :d], rows[0] * _conv_fwd(v, vp, vecs[0]), v], []

    return ew(fn, [dya, sc3], [w_sc], [(d, F32), (d, BF), (d, F32)], tm=256, name="conv_a_bwd1", prev_halo=(1,), window=window)


def conv_a_bwd2(dcv, v, sc3, w_sc, window):
    d = v.shape[1]
    kk = w_sc.shape[0]

    def fn(i, nt, rows, vecs, prevs, nexts):
        dp, vv, x = rows
        nx = _zero_if(i == nt - 1, nexts[0])
        dv = None
        dws = []
        for k in range(kk):
            up = _shift_up(dp, nx, kk - 1 - k)
            term = vecs[0][k:k + 1, :] * up
            dv = term if dv is None else dv + term
            dws.append(jnp.sum(up * vv, axis=0, keepdims=True))
        return [jnp.concatenate([dv * x[:, 2 * d:], dv * x[:, d:2 * d]], axis=1)], [jnp.concatenate(dws, axis=0)]

    return ew(fn, [dcv, v, sc3], [w_sc], [(2 * d, BF)], [(kk, d)], tm=256, name="conv_a_bwd2", next_halo=(0,), window=window)


def _xdot(a, b, passes, split_lhs, ca=1, cb=0):
    parts, r = [], (a if split_lhs else b)
    for _ in range(passes):
        piece = r.astype(BF)
        parts.append(piece)
        r = r - piece.astype(F32)
    other = (b if split_lhs else a).astype(BF)
    acc = None
    for piece in parts:
        term = _dot(piece, other, ca, cb) if split_lhs else _dot(other, piece, ca, cb)
        acc = term if acc is None else acc + term
    return acc


def _ssd_common(xbc_ref, dt_ref, alog_ref, e_ref, w):
    ll = SSM_L
    xs = xbc_ref[:, 0:w]
    dtv = dt_ref[...]
    a_row = -jnp.exp(alog_ref[...])
    a = dtv * a_row
    row = lax.broadcasted_iota(jnp.int32, (ll, ll), 0)
    col = lax.broadcasted_iota(jnp.int32, (ll, ll), 1)
    tril = (row >= col).astype(F32)
    triu = (row <= col).astype(F32)
    acl = _xdot(tril, a, 3, False)
    acl_t = _xdot(a, triu, 3, True, 0, 0)
    e = e_ref[...]
    aclx = _xdot(acl, e, 3, True)
    dtx = _xdot(dtv, e, 2, True)
    last = aclx[ll - 1:ll, :]
    e_in = jnp.exp(aclx)
    e_end = jnp.exp(last - aclx)
    e_tot = jnp.exp(last)
    x = xs * dtx
    return dict(xs=xs, dtv=dtv, a_row=a_row, a=a, row=row, col=col, triu=triu, acl=acl, acl_t=acl_t, dtx=dtx, e_in=e_in, e_end=e_end,
                e_tot=e_tot, x=x)


def _decay(q, hh):
    diff = q["acl"][:, hh:hh + 1] - q["acl_t"][hh:hh + 1, :]
    return jnp.exp(jnp.where(q["row"] >= q["col"], diff, -jnp.inf))


def ssd_fwd(xbc, dt, z, a_log, d_exp, m_norm, e_mat, rider=None):
    t = xbc.shape[0]
    w = z.shape[1]
    gn = SSM_G * SSM_N
    gw = w // SSM_G
    ll, nn = SSM_L, SSM_N
    nc = t // ll
    cw = xbc.shape[1]

    def body(xbc_ref, dt_ref, z_ref, alog_ref, dexp_ref, mn_ref, e_ref, yn_ref, y_ref, sp_ref, s_sc):
        c = pl.program_id(0)

        @pl.when(c == 0)
        def _():
            s_sc[...] = jnp.zeros_like(s_sc)

        q = _ssd_common(xbc_ref, dt_ref, alog_ref, e_ref, w)
        xb = q["x"].astype(BF)
        xsb = (q["x"] * q["e_end"]).astype(BF)
        sp = s_sc[...]
        sp_ref[0] = sp
        spb = sp.astype(BF)
        lane = lax.broadcasted_iota(jnp.int32, (ll, LANE), 1)
        for g in range(SSM_G):
            lo = g * gw
            bg = xbc_ref[:, w + g * nn:w + (g + 1) * nn].astype(BF)
            cg = xbc_ref[:, w + gn + g * nn:w + gn + (g + 1) * nn].astype(BF)
            yoff = _dot(cg, spb[:, lo:lo + gw]) * q["e_in"][:, lo:lo + gw]
            s_sc[:, lo:lo + gw] = sp[:, lo:lo + gw] * q["e_tot"][:, lo:lo + gw] + _dot(bg, xsb[:, lo:lo + gw], 0, 0)
            cb = _dot(cg, bg, 1, 1)
            for pr in range(gw // LANE):
                l0 = lo + pr * LANE
                xp = xb[:, l0:l0 + LANE]
                ys = []
                for hh in (l0 // SSM_P, l0 // SSM_P + 1):
                    wm = (cb * _decay(q, hh)).astype(BF)
                    ys.append(_dot(wm, xp))
                ydiag = jnp.where(lane < SSM_P, ys[0], ys[1])
                y_ref[:, l0:l0 + LANE] = ydiag + yoff[:, pr * LANE:(pr + 1) * LANE] + dexp_ref[:, l0:l0 + LANE] * q["xs"][:, l0:l0 + LANE]
        zv = z_ref[...].astype(F32)
        yz = y_ref[...] * (zv * _sigmoid(zv))
        for g in range(SSM_G):
            lo = g * gw
            yn_ref[:, lo:lo + gw] = _rms(yz[:, lo:lo + gw], mn_ref[:, lo:lo + gw]).astype(BF)

    vec = lambda s: pl.BlockSpec(s, lambda c: (0, 0))
    return host_call(
        body,
        name="ssd_fwd",
        grid=(nc,),
        in_specs=[
            pl.BlockSpec((ll, cw), lambda c: (c, 0)), pl.BlockSpec((ll, LANE), lambda c: (c, 0)), pl.BlockSpec((ll, w), lambda c: (c, 0)),
            vec((1, LANE)), vec((1, w)), vec((1, w)), vec((LANE, w)),
        ],
        out_specs=[pl.BlockSpec((ll, w), lambda c: (c, 0)), pl.BlockSpec((ll, w), lambda c: (c, 0)), pl.BlockSpec((1, nn, w), lambda c: (c, 0, 0))],
        out_shape=[jax.ShapeDtypeStruct((t, w), BF), jax.ShapeDtypeStruct((t, w), F32), jax.ShapeDtypeStruct((nc, nn, w), F32)],
        scratch_shapes=[pltpu.VMEM((nn, w), F32)],
        operands=(xbc, dt, z, a_log, d_exp, m_norm, e_mat),
        rider=rider,
    )


def ssd_bwd(dyn, y, z, xbc, dt, sprev, a_log, d_exp, m_norm, e_mat, et_mat, window, rider=None):
    t = xbc.shape[0]
    w = z.shape[1]
    gn = SSM_G * SSM_N
    gw = w // SSM_G
    ll, nn = SSM_L, SSM_N
    nc = t // ll
    cw = xbc.shape[1]

    def body(dyn_ref, y_ref, z_ref, xbc_ref, dt_ref, sp_ref, alog_ref, dexp_ref, mn_ref, e_ref, et_ref,
             dz_ref, dxbc_ref, ddt_ref, dmn_ref, dd_ref, dal_ref, ds_sc, dy_sc, dx_sc):
        step = pl.program_id(0)

        @pl.when(step == 0)
        def _():
            ds_sc[...] = jnp.zeros_like(ds_sc)

        zv, yv = z_ref[...].astype(F32), y_ref[...]
        sg = _sigmoid(zv)
        sz = zv * sg
        yz = yv * sz
        dmn = []
        for g in range(SSM_G):
            lo = g * gw
            dseg, dmn_g = _rms_bwd(yz[:, lo:lo + gw], mn_ref[:, lo:lo + gw], dyn_ref[:, lo:lo + gw])
            dy_sc[:, lo:lo + gw] = dseg
            dmn.append(dmn_g)
        dmn = jnp.concatenate(dmn, axis=1)
        dyz = dy_sc[...]
        dz_ref[...] = (dyz * yv * (sg * (1.0 + zv * (1.0 - sg)))).astype(BF)
        dy = dyz * sz

        q = _ssd_common(xbc_ref, dt_ref, alog_ref, e_ref, w)
        x = q["x"]
        xb = x.astype(BF)
        xsb = (x * q["e_end"]).astype(BF)
        sp = sp_ref[0]
        spb = sp.astype(BF)
        dsn = ds_sc[...]
        dsnb = dsn.astype(BF)
        dyb = dy.astype(BF)
        lane = lax.broadcasted_iota(jnp.int32, (ll, LANE), 1)
        lane1 = lax.broadcasted_iota(jnp.int32, (1, LANE), 1)
        sub1 = lax.broadcasted_iota(jnp.int32, (LANE, 1), 0)
        dacl = jnp.zeros((ll, LANE), F32)
        dacl_t = jnp.zeros((LANE, ll), F32)
        d_ein, d_eend, d_etot = [], [], []
        for g in range(SSM_G):
            lo = g * gw
            sl = slice(lo, lo + gw)
            bg = xbc_ref[:, w + g * nn:w + (g + 1) * nn].astype(BF)
            cg = xbc_ref[:, w + gn + g * nn:w + gn + (g + 1) * nn].astype(BF)
            zg = _dot(cg, spb[:, sl])
            dzz = (dy[:, sl] * q["e_in"][:, sl]).astype(BF)
            d_ein.append(dy[:, sl] * zg)
            dcg = _dot(dzz, spb[:, sl], 1, 1)
            ds_sc[:, sl] = _dot(cg, dzz, 0, 0) + dsn[:, sl] * q["e_tot"][:, sl]
            d_etot.append(jnp.sum(dsn[:, sl] * sp[:, sl], axis=0, keepdims=True))
            dbg = _dot(xsb[:, sl], dsnb[:, sl], 1, 1)
            dxs_g = _dot(bg, dsnb[:, sl])
            d_eend.append(dxs_g * x[:, sl])
            cb = _dot(cg, bg, 1, 1)
            dcb = jnp.zeros((ll, ll), F32)
            for pr in range(gw // LANE):
                l0 = lo + pr * LANE
                xp = xb[:, l0:l0 + LANE]
                dyp = dyb[:, l0:l0 + LANE]
                dxp = []
                for hi, hh in enumerate((l0 // SSM_P, l0 // SSM_P + 1)):
                    lm = _decay(q, hh)
                    wm = (cb * lm).astype(BF)
                    in_head = (lane < SSM_P) if hi == 0 else (lane >= SSM_P)
                    dwm = _dot(jnp.where(in_head, dyp, jnp.zeros_like(dyp)), xp, 1, 1)
                    dxp.append(_dot(wm, dyp, 0, 0))
                    dlm = dwm * lm
                    dcb = dcb + dlm
                    dd = dlm * cb
                    dacl = dacl + jnp.sum(dd, axis=1, keepdims=True) * (lane1 == hh).astype(F32)
                    dacl_t = dacl_t + (sub1 == hh).astype(F32) * jnp.sum(dd, axis=0, keepdims=True)
                dx_sc[:, l0:l0 + LANE] = jnp.where(lane < SSM_P, dxp[0], dxp[1]) + dxs_g[:, pr * LANE:(pr + 1) * LANE] * q["e_end"][:, l0:l0 + LANE]
            dcbb = dcb.astype(BF)
            dxbc_ref[:, w + g * nn:w + (g + 1) * nn] = dbg + _dot(dcbb, cg, 0, 0)
            dxbc_ref[:, w + gn + g * nn:w + gn + (g + 1) * nn] = dcg + _dot(dcbb, bg)
        d_ein = jnp.concatenate(d_ein, axis=1) * q["e_in"]
        d_eend = jnp.concatenate(d_eend, axis=1) * q["e_end"]
        d_etot = jnp.concatenate(d_etot, axis=1) * q["e_tot"]
        et = et_ref[...]
        last_add = jnp.sum(d_eend, axis=0, keepdims=True) + d_etot
        last_add = _xdot(jnp.broadcast_to(last_add, (HALO, w)), et, 2, True)[0:1]
        row1 = lax.broadcasted_iota(jnp.int32, (ll, LANE), 0)
        dacl = dacl + _xdot(d_ein - d_eend, et, 2, True) + jnp.where(row1 == ll - 1, last_add, 0.0)
        da = _xdot(q["triu"], dacl, 2, False) - _xdot(q["triu"], dacl_t, 2, False, 1, 1)
        dxv = dx_sc[...]
        dxbc_ref[:, 0:w] = dexp_ref[...] * dy + dxv * q["dtx"]
        ddt_ref[...] = _xdot(dxv * q["xs"], et, 2, True) + da * q["a_row"]
        dal = jnp.sum(da * q["dtv"], axis=0, keepdims=True) * q["a_row"]
        ddv = jnp.sum(dy * q["xs"], axis=0, keepdims=True)
        ddv = _xdot(jnp.broadcast_to(ddv, (HALO, w)), et, 2, True)[0:1]
        _accumulate(dmn_ref, dmn, step == 0)
        _accumulate(dd_ref, ddv, step == 0)
        _accumulate(dal_ref, dal, step == 0)

    rev = lambda c_: pl.BlockSpec((ll, c_), lambda s: (nc - 1 - s, 0))
    vec = lambda s_: pl.BlockSpec(s_, lambda s: (0, 0))
    n_in = 11

    def body_skipping_buffer(*refs):
        body(*refs[:n_in], *refs[n_in + 1:])

    return host_call(
        body_skipping_buffer,
        name="ssd_bwd",
        grid=(nc,),
        in_specs=[
            rev(w), rev(w), rev(w), rev(cw), rev(LANE), pl.BlockSpec((1, nn, w), lambda s: (nc - 1 - s, 0, 0)),
            vec((1, LANE)), vec((1, w)), vec((1, w)), vec((LANE, w)), vec((w, LANE)), ANY,
        ],
        out_specs=[pl.BlockSpec((ll, w), lambda s: (nc - 1 - s, window.block)), rev(cw), rev(LANE), vec((1, w)), vec((1, LANE)), vec((1, LANE))],
        out_shape=[
            jax.ShapeDtypeStruct((t, window.cols), BF), jax.ShapeDtypeStruct((t, cw), F32), jax.ShapeDtypeStruct((t, LANE), F32),
            jax.ShapeDtypeStruct((1, w), F32), jax.ShapeDtypeStruct((1, LANE), F32), jax.ShapeDtypeStruct((1, LANE), F32),
        ],
        scratch_shapes=[pltpu.VMEM((nn, w), F32), pltpu.VMEM((ll, w), F32), pltpu.VMEM((ll, w), F32)],
        operands=(dyn, y, z, xbc, dt, sprev, a_log, d_exp, m_norm, e_mat, et_mat, window.buf),
        rider=rider,
        aliases={n_in: 0},
    )


def _w1024_spec(d, nblk, iblk):
    r = nblk * (d // NCHIP)
    return pl.BlockSpec((NCHIP, r, d), lambda i: (0, iblk // nblk, 0))


def _whole(ref):
    v = ref[...]
    return v.reshape(v.shape[0] * v.shape[1], v.shape[2])


def mix_out_fwd(ya_in, yn, gates, h, w1024, rider=None):
    t, d = h.shape
    tm = _tile(t, 256)

    def body(ya_ref, yn_ref, g_ref, h_ref, wm_ref, wa_ref, wo_ref, ho_ref, oa_ref, om_ref, mg_ref):
        y_a = _dot(ya_ref[...], _whole(wa_ref))
        y_m = _dot(yn_ref[...], _whole(wm_ref))
        oa_ref[...] = y_a
        om_ref[...] = y_m
        gv = g_ref[...].astype(F32)
        mg = (_sigmoid(gv[:, :d]) * y_a + _sigmoid(gv[:, d:]) * y_m).astype(BF)
        mg_ref[...] = mg
        ho_ref[...] = h_ref[...] + _dot(mg, _whole(wo_ref))

    row = lambda c: pl.BlockSpec((tm, c), lambda i: (i, 0))
    return host_call(
        body,
        name="mix_out_fwd",
        grid=(t // tm,),
        in_specs=[row(d), row(2 * d), row(2 * d), row(d), _w1024_spec(d, 2, 0), _w1024_spec(d, 1, 2), _w1024_spec(d, 1, 3)],
        out_specs=[row(d), row(d), row(d), row(d)],
        out_shape=[jax.ShapeDtypeStruct((t, d), F32), jax.ShapeDtypeStruct((t, d), F32), jax.ShapeDtypeStruct((t, d), F32),
                   jax.ShapeDtypeStruct((t, d), BF)],
        scratch_shapes=[],
        operands=(ya_in, yn, gates, h, w1024, w1024, w1024),
        rider=rider,
    )


def mix_out_bwd(dh, gates, y_a, y_m, w1024, cols):
    t, d = dh.shape
    tm = _tile(t, 256)

    def body(dh_ref, g_ref, ya_ref, ym_ref, wm_ref, wa_ref, wo_ref, dg_ref, dya_ref, dyn_ref, da_ref, dm_ref):
        dmg = _dot(dh_ref[...].astype(BF), _whole(wo_ref), 1, 1)
        gv = g_ref[...].astype(F32)
        sa, sm = _sigmoid(gv[:, :d]), _sigmoid(gv[:, d:])
        dg_ref[:, :d] = (dmg * ya_ref[...] * sa * (1.0 - sa)).astype(BF)
        dg_ref[:, d:] = (dmg * ym_ref[...] * sm * (1.0 - sm)).astype(BF)
        da = (dmg * sa).astype(BF)
        dm = (dmg * sm).astype(BF)
        da_ref[...] = da
        dm_ref[...] = dm
        dya_ref[...] = _dot(da, _whole(wa_ref), 1, 1)
        dyn_ref[...] = _dot(dm, _whole(wm_ref), 1, 1)

    row = lambda c: pl.BlockSpec((tm, c), lambda i: (i, 0))
    return pl.pallas_call(
        body,
        name="mix_out_bwd",
        grid=(t // tm,),
        in_specs=[row(d), row(2 * d), row(d), row(d), _w1024_spec(d, 2, 0), _w1024_spec(d, 1, 2), _w1024_spec(d, 1, 3)],
        out_specs=[row(2 * d), row(d), row(2 * d), row(d), row(d)],
        out_shape=[jax.ShapeDtypeStruct((t, cols), BF), jax.ShapeDtypeStruct((t, d), F32), jax.ShapeDtypeStruct((t, 2 * d), F32),
                   jax.ShapeDtypeStruct((t, d), BF), jax.ShapeDtypeStruct((t, d), BF)],
        compiler_params=_params(("parallel",)),
    )(dh, gates, y_a, y_m, w1024, w1024, w1024)


def norm_bwd_add(dh, h, g, dn):
    def fn(i, nt, rows, vecs, prevs, nexts):
        dx, dg = _rms_bwd(rows[1], vecs[0], rows[2])
        return [rows[0] + dx], [dg]
    d = h.shape[1]
    return ew(fn, [dh, h, dn], [g], [(d, F32)], [(1, d)], tm=512, name="norm_bwd_add")


def _pe(p, wpp_ref):
    pb = p.astype(BF)
    return jnp.concatenate([_dot(pb, wpp_ref[k]) for k in range(NCHIP)], axis=1)


def ple_fwd(h, g, p, w1024, wpp):
    t, d = h.shape
    tm = _tile(t, 512)

    def body(h_ref, g_ref, p_ref, wg_ref, wp_ref, ho_ref):
        hv = h_ref[...]
        gate = _sigmoid(_dot(_rms(hv, g_ref[...]).astype(BF), _whole(wg_ref)))
        ho_ref[...] = hv + gate * _pe(p_ref[...], wp_ref)

    row = lambda c: pl.BlockSpec((tm, c), lambda i: (i, 0))
    wpp_spec = pl.BlockSpec(wpp.shape, lambda i: (0, 0, 0))
    return pl.pallas_call(
        body,
        name="ple_fwd",
        grid=(t // tm,),
        in_specs=[row(d), pl.BlockSpec((1, d), lambda i: (0, 0)), row(p.shape[1]), _w1024_spec(d, 1, 4), wpp_spec],
        out_specs=row(d),
        out_shape=jax.ShapeDtypeStruct((t, d), F32),
        compiler_params=_params(("parallel",)),
    )(h, g, p, w1024, wpp)


def ple_bwd(dho, h, g, p, w1024, wpp):
    t, d = h.shape
    tm = _tile(t, 512)

    def body(dho_ref, h_ref, g_ref, p_ref, wg_ref, wp_ref, dh_ref, dg_ref, n_ref, dgp_ref, dpe_ref):
        hv, dv = h_ref[...], dho_ref[...]
        n = _rms(hv, g_ref[...]).astype(BF)
        n_ref[...] = n
        wg = _whole(wg_ref)
        gate = _sigmoid(_dot(n, wg))
        pe = _pe(p_ref[...], wp_ref)
        dpe_ref[...] = (dv * gate).astype(BF)
        dgp = (dv * pe * gate * (1.0 - gate)).astype(BF)
        dgp_ref[...] = dgp
        dx, dg = _rms_bwd(hv, g_ref[...], _dot(dgp, wg, 1, 1))
        dh_ref[...] = dv + dx
        _accumulate(dg_ref, dg, pl.program_id(0) == 0)

    row = lambda c: pl.BlockSpec((tm, c), lambda i: (i, 0))
    wpp_spec = pl.BlockSpec(wpp.shape, lambda i: (0, 0, 0))
    return pl.pallas_call(
        body,
        name="ple_bwd",
        grid=(t // tm,),
        in_specs=[row(d), row(d), pl.BlockSpec((1, d), lambda i: (0, 0)), row(p.shape[1]), _w1024_spec(d, 1, 4), wpp_spec],
        out_specs=[row(d), pl.BlockSpec((1, d), lambda i: (0, 0)), row(d), row(d), row(d)],
        out_shape=[jax.ShapeDtypeStruct((t, d), F32), jax.ShapeDtypeStruct((1, d), F32), jax.ShapeDtypeStruct((t, d), BF),
                   jax.ShapeDtypeStruct((t, d), BF), jax.ShapeDtypeStruct((t, d), BF)],
        compiler_params=_params(("arbitrary",)),
    )(dho, h, g, p, w1024, wpp)


def loss_bwd(h, g, target):
    d = h.shape[1]

    def fn(i, nt, rows, vecs, prevs, nexts):
        err = _rms(rows[0], vecs[0]) - rows[1]
        dx, dg = _rms_bwd(rows[0], vecs[0], err * (1.0 / d))
        return [dx], [jnp.sum(err * err, axis=0, keepdims=True) * (0.5 / d), dg]

    return ew(fn, [h, target], [g], [(d, F32)], [(1, d), (1, d)], tm=512, name="loss_bwd")


def adamw(w, g, m, v, name):
    c1, c2 = 1.0 / (1.0 - ADAM_B1 ** ADAM_STEP), 1.0 / (1.0 - ADAM_B2 ** ADAM_STEP)

    def fn(i, nt, rows, vecs, prevs, nexts):
        wv, gv, mv, vv = rows
        mn = ADAM_B1 * mv + (1.0 - ADAM_B1) * gv
        vn = ADAM_B2 * vv + (1.0 - ADAM_B2) * (gv * gv)
        delta = -ADAM_LR * ((mn * c1) / (jnp.sqrt(vn * c2) + ADAM_EPS) + ADAM_WD * wv)
        return [delta, mn, vn], []

    c = w.shape[1]
    return ew(fn, [w, g, m, v], [], [(c, F32)] * 3, tm=_row_tile(w.shape[0], c, HALO), name=name)


def _place():
    return lax.axis_index("x"), lax.axis_index("y"), lax.axis_index("c")


def _other_chips(x, y):
    return [(1 - x, y), (x, 1 - y), (1 - x, 1 - y)]


ANY = pl.BlockSpec(memory_space=pl.ANY)


def _comm_call(body, name, ins, out_shapes, n_sems, aliases=None):
    return pl.pallas_call(
        body,
        name=name,
        in_specs=[ANY] * len(ins),
        out_specs=[ANY] * len(out_shapes),
        out_shape=out_shapes,
        scratch_shapes=[pltpu.SemaphoreType.DMA((n_sems,)), pltpu.SemaphoreType.DMA((n_sems,))],
        input_output_aliases=aliases or {},
    )(*ins)


def gather_rider(packs):
    nt = len(packs)

    def pieces(ins, outs, send_sems, recv_sems):
        x, y, cc = _place()
        chips = _other_chips(x, y)
        sibling = (x, y, 1 - cc)
        k_me = 2 * x + y

        def copy(k, src, dst, to):
            return pltpu.make_async_remote_copy(src_ref=src, dst_ref=dst, send_sem=send_sems.at[k], recv_sem=recv_sems.at[k],
                                                device_id=to, device_id_type=MESH)

        sends, forwards, arrivals = [], [], []
        for ti in range(nt):
            for j, (px, py) in enumerate(chips):
                sends.append(copy(7 * ti + j, ins[ti].at[cc], outs[ti].at[k_me, cc], (px, py, cc)))
                landed = outs[ti].at[2 * px + py, cc]
                forwards.append((copy(7 * ti + j, landed, landed, (px, py, cc)), copy(7 * ti + 3 + j, landed, landed, sibling)))
                passed = outs[ti].at[2 * px + py, 1 - cc]
                arrivals.append(copy(7 * ti + 3 + j, passed, passed, sibling))
            sends.append(copy(7 * ti + 6, ins[ti], outs[ti].at[k_me], sibling))
            own = outs[ti].at[k_me]
            arrivals.append(copy(7 * ti + 6, own, own, sibling))
        return sends, forwards, arrivals

    def start(*parts):
        for cp in pieces(*parts)[0]:
            cp.start()

    def mid(*parts):
        for landed, forward in pieces(*parts)[1]:
            landed.wait_recv()
            forward.start()

    def finish(*parts):
        sends, forwards, arrivals = pieces(*parts)
        for cp in arrivals:
            cp.wait_recv()
        for cp in sends + [f for _, f in forwards]:
            cp.wait_send()

    return Rider(packs, [jax.ShapeDtypeStruct((NCHIP,) + p.shape, p.dtype) for p in packs], 7 * nt, start, finish, mid)


def swap_rider(gs):
    nt = len(gs)
    hl = gs[0].shape[1] // 2

    def copies(ins, outs, send_sems, recv_sems):
        x, y, cc = _place()
        theirs = pl.ds((1 - cc) * hl, hl)
        return [pltpu.make_async_remote_copy(src_ref=ins[ti].at[:, theirs], dst_ref=outs[ti], send_sem=send_sems.at[ti], recv_sem=recv_sems.at[ti],
                                             device_id=(x, y, 1 - cc), device_id_type=MESH) for ti in range(nt)]

    def start(*parts):
        for cp in copies(*parts):
            cp.start()

    def finish(*parts):
        for cp in copies(*parts):
            cp.wait()

    return Rider(gs, [jax.ShapeDtypeStruct((NCHIP, hl) + g.shape[2:], g.dtype) for g in gs], nt, start, finish)


def scatter_packs(cs, name):
    return scatter_rider(cs).standalone(name)


def scatter_rider(cs):
    nt = len(cs)

    def copies(ins, outs, send_sems, recv_sems):
        x, y, cc = _place()
        cps = []
        for ti in range(nt):
            for j, (px, py) in enumerate(_other_chips(x, y)):
                cps.append(pltpu.make_async_remote_copy(src_ref=ins[ti].at[2 * px + py], dst_ref=outs[ti].at[j], send_sem=send_sems.at[3 * ti + j],
                                                        recv_sem=recv_sems.at[3 * ti + j], device_id=(px, py, cc), device_id_type=MESH))
        return cps

    def start(*parts):
        for cp in copies(*parts):
            cp.start()

    def finish(*parts):
        for cp in copies(*parts):
            cp.wait()

    return Rider(cs, [jax.ShapeDtypeStruct((3,) + c_.shape[1:], c_.dtype) for c_ in cs], 3 * nt, start, finish)


def join_packs(fulls, name):
    nt = len(fulls)
    hl = fulls[0].shape[0] // 2

    def body(*refs):
        ins, outs, (send_sems, recv_sems) = refs[:nt], refs[nt:2 * nt], refs[2 * nt:]
        x, y, cc = _place()
        mine = pl.ds(cc * hl, hl)
        cps = [pltpu.make_async_remote_copy(src_ref=ins[ti].at[mine], dst_ref=outs[ti].at[mine], send_sem=send_sems.at[ti], recv_sem=recv_sems.at[ti],
                                            device_id=(x, y, 1 - cc), device_id_type=MESH) for ti in range(nt)]
        for cp in cps:
            cp.start()
        for cp in cps:
            cp.wait()

    return _comm_call(body, name, fulls, [jax.ShapeDtypeStruct(f.shape, f.dtype) for f in fulls], nt, aliases={ti: ti for ti in range(nt)})


def add_sibling(g, recv, name):
    _, nl, r, c = g.shape
    hl = nl // 2
    tm, tc = _tile2(r, c)

    def body(g_ref, r_ref, o_ref):
        o_ref[...] = (g_ref[...].astype(F32) + r_ref[...].astype(F32)).astype(o_ref.dtype)

    blk = (None, None, tm, tc)
    return pl.pallas_call(
        body,
        name=name,
        grid=(NCHIP, hl, r // tm, c // tc),
        in_specs=[pl.BlockSpec(blk, lambda k, l, i, j: (k, lax.axis_index("c") * hl + l, i, j)), pl.BlockSpec(blk, lambda k, l, i, j: (k, l, i, j))],
        out_specs=pl.BlockSpec(blk, lambda k, l, i, j: (k, l, i, j)),
        out_shape=jax.ShapeDtypeStruct(recv.shape, BF),
        compiler_params=_params(("parallel",) * 4),
    )(g, recv)


def add_chips(cs, got, nl, name):
    _, hl, r, c = cs.shape
    tm, tc = _tile2(r, c)

    def body(own_ref, got_ref, o_ref):
        o_ref[...] = own_ref[...].astype(F32) + got_ref[0].astype(F32) + got_ref[1].astype(F32) + got_ref[2].astype(F32)

    return pl.pallas_call(
        body,
        name=name,
        grid=(hl, r // tm, c // tc),
        in_specs=[pl.BlockSpec((None, None, tm, tc), lambda l, i, j: (2 * lax.axis_index("x") + lax.axis_index("y"), l, i, j)),
                  pl.BlockSpec((3, None, tm, tc), lambda l, i, j: (0, l, i, j))],
        out_specs=pl.BlockSpec((None, tm, tc), lambda l, i, j: (lax.axis_index("c") * hl + l, i, j)),
        out_shape=jax.ShapeDtypeStruct((nl, r, c), F32),
        compiler_params=_params(("parallel",) * 3),
    )(cs, got)


def all_gather_xy(shard, name):
    r, c = shard.shape
    hr = r // 2
    assert r % 32 == 0

    def body(x_ref, out_ref, send_sems, recv_sems, local_sem):
        x, y, cc = _place()
        chips = _other_chips(x, y)
        mine = pl.ds(pl.multiple_of(cc * hr, 16), hr)
        theirs = pl.ds(pl.multiple_of((1 - cc) * hr, 16), hr)
        k_me = 2 * x + y

        def copy(k, src, dst, to):
            return pltpu.make_async_remote_copy(src_ref=src, dst_ref=dst, send_sem=send_sems.at[k], recv_sem=recv_sems.at[k],
                                                device_id=to, device_id_type=MESH)

        own = pltpu.make_async_copy(x_ref, out_ref.at[k_me], local_sem)
        own.start()
        first = [copy(j, x_ref.at[mine], out_ref.at[k_me, mine], (*chip, cc)) for j, chip in enumerate(chips)]
        for cp in first:
            cp.start()
        passed = []
        for j, (px, py) in enumerate(chips):
            landed = out_ref.at[2 * px + py, mine]
            copy(j, landed, landed, (px, py, cc)).wait_recv()
            fw = copy(3 + j, landed, landed, (x, y, 1 - cc))
            fw.start()
            passed.append(fw)
        for j, (px, py) in enumerate(chips):
            landed = out_ref.at[2 * px + py, theirs]
            copy(3 + j, landed, landed, (x, y, 1 - cc)).wait_recv()
        for cp in first + passed:
            cp.wait_send()
        own.wait()

    return pl.pallas_call(
        body,
        name=name,
        in_specs=[ANY],
        out_specs=ANY,
        out_shape=jax.ShapeDtypeStruct((NCHIP, r, c), shard.dtype),
        scratch_shapes=[pltpu.SemaphoreType.DMA((6,)), pltpu.SemaphoreType.DMA((6,)), pltpu.SemaphoreType.DMA],
    )(shard)


def all_gather_8(block, name):
    m, c = block.shape

    def body(x_ref, out_ref, send_sems, recv_sems, local_sem):
        x, y, cc = _place()
        me, sibling = (x, y, cc), (x, y, 1 - cc)
        chips = _other_chips(x, y)

        def rows(px, py, pc):
            return out_ref.at[4 * px + 2 * py + pc]

        def copy(k, blk, to, src=None):
            return pltpu.make_async_remote_copy(src_ref=rows(*blk) if src is None else src, dst_ref=rows(*blk), send_sem=send_sems.at[k],
                                                recv_sem=recv_sems.at[k], device_id=to, device_id_type=MESH)

        mine = pltpu.make_async_copy(x_ref, rows(*me), local_sem)
        mine.start()
        first = [copy(0, me, sibling, src=x_ref)]
        first += [copy(1 + j, me, (*chip, cc), src=x_ref) for j, chip in enumerate(chips)]
        for cp in first:
            cp.start()
        passed = [copy(4 + j, (*chip, cc), sibling) for j, chip in enumerate(chips)]
        for j, chip in enumerate(chips):
            copy(1 + j, (*chip, cc), me).wait_recv()
            passed[j].start()
        copy(0, sibling, me).wait_recv()
        for j, chip in enumerate(chips):
            copy(4 + j, (*chip, 1 - cc), me).wait_recv()
        for cp in first + passed:
            cp.wait_send()
        mine.wait()

    return pl.pallas_call(
        body,
        name=name,
        in_specs=[pl.BlockSpec(memory_space=pltpu.VMEM)],
        out_specs=pl.BlockSpec(memory_space=pltpu.VMEM),
        out_shape=jax.ShapeDtypeStruct((8, m, c), block.dtype),
        scratch_shapes=[pltpu.SemaphoreType.DMA((7,)), pltpu.SemaphoreType.DMA((7,)), pltpu.SemaphoreType.DMA],
        compiler_params=pltpu.CompilerParams(vmem_limit_bytes=VMEM_LIMIT),
    )(block)


def add_parts(parts, out_dtype, name, tm=512):
    def fn(i, nt, rows, vecs, prevs, nexts):
        acc = rows[0]
        for r_ in rows[1:]:
            acc = acc + r_
        return [acc], []
    r, c = parts[0].shape
    return ew(fn, list(parts), [], [(c, out_dtype)], tm=_tile(r, tm, 16), name=name)[0]


SMALL_SHARDED = ("sc_conv_w", "m_conv_w")
SMALL_REPL = ("ffn1_norm", "mix_norm", "m_conv_b", "m_dt_bias", "m_A_log", "m_D", "m_norm", "ffn2_norm", "ple_norm", "final_norm")
BIG = ("ffn1_wg", "ffn1_wu", "ffn1_wd", "w_in", "sc_w_out", "m_w_out", "w_o", "ffn2_wg", "ffn2_wu", "ffn2_wd", "ple_w_gate", "ple_w_proj")
TRANSPOSED = ("ffn1_wg", "ffn1_wu", "ffn2_wg", "ffn2_wu", "w_in")
ORDER = ("ffn1_norm", "ffn1_wg", "ffn1_wu", "ffn1_wd", "mix_norm", "w_in", "sc_conv_w", "sc_w_out", "m_conv_w", "m_conv_b", "m_dt_bias",
         "m_A_log", "m_D", "m_norm", "m_w_out", "w_o", "ffn2_norm", "ffn2_wg", "ffn2_wu", "ffn2_wd", "ple_norm", "ple_w_gate", "ple_w_proj",
         "final_norm")


def _pack(arrs, cols, row_mult):
    flat = jnp.concatenate([a.reshape(-1) for a in arrs])
    n = flat.shape[0]
    rows = -(-n // cols)
    rows = -(-rows // row_mult) * row_mult
    return jnp.pad(flat, (0, rows * cols - n)).reshape(rows, cols)


def _unpack(flat2d, shapes):
    flat = flat2d.reshape(-1)
    out, off = [], 0
    for s in shapes:
        n = int(np.prod(s))
        out.append(flat[off:off + n].reshape(s))
        off += n
    return out


def _row_cat(arrs, dtype):
    return jnp.concatenate([a.astype(dtype) for a in arrs], axis=1)


def kernel(x, p, ffn1_norm, ffn1_wg, ffn1_wu, ffn1_wd, mix_norm, w_in, sc_conv_w, sc_w_out, m_conv_w, m_conv_b, m_dt_bias, m_A_log, m_D, m_norm, m_w_out, w_o, ffn2_norm, ffn2_wg, ffn2_wu, ffn2_wd, ple_norm, ple_w_gate, ple_w_proj, final_norm, loss_target, m_ffn1_norm, m_ffn1_wg, m_ffn1_wu, m_ffn1_wd, m_mix_norm, m_w_in, m_sc_conv_w, m_sc_w_out, m_m_conv_w, m_m_conv_b, m_m_dt_bias, m_m_A_log, m_m_D, m_m_norm, m_m_w_out, m_w_o, m_ffn2_norm, m_ffn2_wg, m_ffn2_wu, m_ffn2_wd, m_ple_norm, m_ple_w_gate, m_ple_w_proj, m_final_norm, v_ffn1_norm, v_ffn1_wg, v_ffn1_wu, v_ffn1_wd, v_mix_norm, v_w_in, v_sc_conv_w, v_sc_w_out, v_m_conv_w, v_m_conv_b, v_m_dt_bias, v_m_A_log, v_m_D, v_m_norm, v_m_w_out, v_w_o, v_ffn2_norm, v_ffn2_wg, v_ffn2_wu, v_ffn2_wd, v_ple_norm, v_ple_w_gate, v_ple_w_proj, v_final_norm):
    args = dict(locals())
    wts = {n: args[n] for n in ORDER}
    mom = {n: args["m_" + n] for n in ORDER}
    vel = {n: args["v_" + n] for n in ORDER}

    depth = ffn1_norm.shape[0]
    d = x.shape[-1]
    w = 2 * d
    hh = w // SSM_P
    cw = w + 2 * SSM_G * SSM_N
    d4 = d // NCHIP
    pp = 7 * d + cw + LANE
    my_x, my_y, my_c = _place()
    k_me = 2 * my_x + my_y

    tr = lambda a: jnp.swapaxes(a, 1, 2)
    gu_t = [_row_cat([tr(wg_), tr(wu_)], BF) for wg_, wu_ in ((ffn1_wg, ffn1_wu), (ffn2_wg, ffn2_wu))]
    wd_l = [ffn1_wd.astype(BF), ffn2_wd.astype(BF)]
    w1024_l = _row_cat([m_w_out, sc_w_out, w_o, ple_w_gate], BF)
    p4 = w_in.shape[2]
    p4p = -(-p4 // 32) * 32
    win_l, wpp_l = jnp.pad(tr(w_in).astype(BF), ((0, 0), (0, p4p - p4), (0, 0))), ple_w_proj.astype(BF)
    halves = lambda a: a.reshape(2, a.shape[0] // 2, a.shape[1])
    whole = lambda g: g.reshape(NCHIP, g.shape[2] * 2, g.shape[3])

    def pieces(l):
        return {"small": [halves(w1024_l[l]), halves(wpp_l[l])], "win": [halves(win_l[l])], "gu1": [halves(gu_t[0][l])], "d1": [halves(wd_l[0][l])],
                "gu2": [halves(gu_t[1][l])], "d2": [halves(wd_l[1][l])]}

    small_local = [sc_conv_w, m_conv_w]
    gathered_s = all_gather_xy(_pack(small_local, LANE, 32), "gather_conv_weights")
    per_shard_s = [_unpack(gathered_s[k], [a.shape for a in small_local]) for k in range(NCHIP)]
    sc_conv_full = jnp.concatenate([per_shard_s[k][0] for k in range(NCHIP)], axis=2)
    m_conv_full = jnp.concatenate([per_shard_s[k][1] for k in range(NCHIP)], axis=2)

    pad_h = lambda a: jnp.pad(a, ((0, 0), (0, LANE - hh)))
    dt_bias_p, a_log_p = pad_h(m_dt_bias), pad_h(m_A_log)
    d_exp = jnp.repeat(m_D, SSM_P, axis=1)
    e_mat = (jnp.arange(w)[None, :] // SSM_P == jnp.arange(LANE)[:, None]).astype(F32)
    et_mat = e_mat.T
    o_z, o_xbc, o_dt, o_g = 3 * d, 5 * d, 5 * d + cw, 5 * d + cw + hh

    def layer_weights(got):
        wt = {"w1024": whole(got["small"][0]), "wpp": whole(got["small"][1])}
        wt.update({k: whole(got[k][0]) for k in ("gu1", "d1", "gu2", "d2")})
        gw = whole(got["win"][0])

        def wi(lo, hi):
            parts = [gw[k, max(lo - k * p4, 0):min(hi - k * p4, p4)] for k in range(NCHIP) if lo < (k + 1) * p4 and hi > k * p4]
            return parts[0] if len(parts) == 1 else jnp.concatenate(parts, axis=0)

        wt["sc3"], wt["z"], wt["xbc"], wt["g2"] = wi(0, o_z), wi(o_z, o_xbc), wi(o_xbc, o_dt), wi(o_g, o_g + 2 * d)
        wt["dt"] = jnp.pad(wi(o_dt, o_g), ((0, LANE - hh), (0, 0)))
        wt["in_p"] = jnp.concatenate([wt["g2"], wt["z"], wt["sc3"][d:], wt["xbc"], wt["sc3"][:d], wt["dt"]], axis=0)
        return wt

    first = pieces(0)
    order = ("gu1", "d1", "win", "small", "gu2", "d2")
    flat = gather_rider([a for k in order for a in first[k]]).standalone("gather_weights")
    got, pos = {}, 0
    for k in order:
        got[k] = flat[pos:pos + len(first[k])]
        pos += len(first[k])
    wts_l = [layer_weights(got)]

    h = x[0]
    saved = []
    for i in range(depth):
        s, wt = {}, wts_l[i]
        nxt = pieces(i + 1) if i + 1 < depth else None
        ride = lambda k: gather_rider(nxt[k]) if nxt else None
        got = {}
        s["h0"] = h
        (s["ab1"], s4, s["n1"]), got["small"] = ffn_up(h, ffn1_norm[i:i + 1], wt["gu1"], rider=ride("small"))
        h, got["d1"] = ffn_down(s4, wt["d1"], h, rider=ride("d1"))
        s["h1"] = h
        u = norm_cast(h, mix_norm[i:i + 1])
        s["u"] = u
        s["sc3"] = mm(u, wt["sc3"], tb=True, out_dtype=BF, name="proj_sc")
        s["z"] = mm(u, wt["z"], tb=True, out_dtype=BF, name="proj_z")
        s["xbc_raw"] = mm(u, wt["xbc"], tb=True, out_dtype=BF, name="proj_xbc")
        s["gates"] = mm(u, wt["g2"], tb=True, out_dtype=BF, name="proj_gates")
        s["dt_raw"] = mm(u, wt["dt"], tb=True, name="proj_dt")
        s["ya_in"] = conv_a_fwd(s["sc3"], sc_conv_full[i])
        s["xbc"], s["dt"] = conv_m_fwd(s["xbc_raw"], s["dt_raw"], m_conv_full[i], m_conv_b[i:i + 1], dt_bias_p[i:i + 1])
        (s["yn"], s["y"], s["sprev"]), got["win"] = ssd_fwd(s["xbc"], s["dt"], s["z"], a_log_p[i:i + 1], d_exp[i:i + 1], m_norm[i:i + 1], e_mat,
                                                            rider=ride("win"))
        (h, s["y_a"], s["y_m"], s["merged"]), got["gu2"] = mix_out_fwd(s["ya_in"], s["yn"], s["gates"], h, wt["w1024"], rider=ride("gu2"))
        s["h2"] = h
        (s["ab2"], s4, s["n2"]), got["gu1"] = ffn_up(h, ffn2_norm[i:i + 1], wt["gu2"], rider=ride("gu1"))
        h, got["d2"] = ffn_down(s4, wt["d2"], h, rider=ride("d2"))
        s["h3"] = h
        h = ple_fwd(h, ple_norm[i:i + 1], p[i, 0], wt["w1024"], wt["wpp"])
        saved.append(s)
        if nxt:
            wts_l.append(layer_weights(got))

    dh, loss_lanes, g_final = loss_bwd(h, final_norm[None, :], loss_target[0])
    loss = lax.psum(jnp.sum(loss_lanes), ("x", "y", "c"))

    def finish_reduce(cs, got):
        halves = [add_chips(c_, g_, 2, "grad_add_chips") for c_, g_ in zip(cs, got, strict=True)]
        return [f.reshape(-1, f.shape[2]) for f in join_packs(halves, "grad_join_halves")]

    g_layer, pending, reduced = None, None, [None] * depth
    gs = {n: [None] * depth for n in SMALL_SHARDED + SMALL_REPL if n != "final_norm"}
    for i in reversed(range(depth)):
        s = saved[i]
        wt = wts_l[i]
        dh, gs["ple_norm"][i], n3, dgp, dpe = ple_bwd(dh, s["h3"], ple_norm[i:i + 1], p[i, 0], wt["w1024"], wt["wpp"])
        g_pg = mm(n3, dgp, ta=True, out_dtype=BF, name="g_ple_gate", tm_cap=512, tn_cap=512)
        g_pp = mm(p[i, 0], dpe, ta=True, out_dtype=BF, name="g_ple_proj", tm_cap=512, tn_cap=512)
        g_pp = jnp.transpose(g_pp.reshape(g_pp.shape[0], NCHIP, d4), (1, 0, 2))
        (dn2, s2, dab2), from_sibling = ffn_bwd(dh, s["ab2"], wt["gu2"], wt["d2"], rider=swap_rider(g_layer) if g_layer else None)
        if g_layer:
            pending = [add_sibling(g, r_, "grad_add_sibling") for g, r_ in zip(g_layer, from_sibling, strict=True)]
        g_ffn2 = ffn_wgrads(s["n2"], dh, s2, dab2)
        dh, gs["ffn2_norm"][i] = norm_bwd_add(dh, s["h2"], ffn2_norm[i:i + 1], dn2)
        dproj, dya, dyn, dy_a, dy_m = mix_out_bwd(dh, s["gates"], s["y_a"], s["y_m"], wt["w1024"], pp)
        g_wo = mm(s["merged"], dh, ta=True, out_dtype=BF, name="g_w_o", tm_cap=512, tn_cap=512)
        g_sco = mm(s["ya_in"], dy_a, ta=True, out_dtype=BF, name="g_sc_out", tm_cap=512, tn_cap=512)
        g_mo = mm(s["yn"], dy_m, ta=True, out_dtype=BF, name="g_m_out", tm_cap=512, tn_cap=512)
        g_1024 = jnp.concatenate([g_mo.reshape(NCHIP, 2 * d4, d), g_sco.reshape(NCHIP, d4, d), g_wo.reshape(NCHIP, d4, d),
                                  g_pg.reshape(NCHIP, d4, d)], axis=1)
        (dproj, dxbc, ddt, gs["m_norm"][i], gd, gal), got_a = ssd_bwd(dyn, s["y"], s["z"], s["xbc"], s["dt"], s["sprev"], a_log_p[i:i + 1], d_exp[i:i + 1],
                                                                      m_norm[i:i + 1], e_mat, et_mat, Window(0, 1, pp, dproj),
                                                                      rider=scatter_rider(pending[:1]) if pending else None)
        gs["m_D"][i], gs["m_A_log"][i] = gd[:, :hh], gal[:, :hh]
        dpre, dproj, gdb = conv_m_bwd1(dxbc, s["xbc_raw"], ddt, s["dt_raw"], m_conv_full[i], m_conv_b[i:i + 1], dt_bias_p[i:i + 1],
                                       Window(1, (7 * d + cw) // LANE, pp, dproj))
        gs["m_dt_bias"][i] = gdb[:, :hh]
        dproj, gs["m_conv_w"][i], gs["m_conv_b"][i] = conv_bwd2(dpre, s["xbc_raw"], m_conv_full[i], "conv_m_bwd2", Window(0, 6 * d // cw, pp, dproj))
        dcv, dproj, v = conv_a_bwd1(dya, s["sc3"], sc_conv_full[i], Window(1, (6 * d + cw) // d, pp, dproj))
        dproj, gs["sc_conv_w"][i] = conv_a_bwd2(dcv, v, s["sc3"], sc_conv_full[i], Window(0, 2, pp, dproj))
        if pending:
            du, got_b = mm(dproj, wt["in_p"], name="d_proj_in", rider=scatter_rider(pending[2:3]))
            gwp, got_c = mm(dproj, s["u"], ta=True, out_dtype=BF, name="g_w_in", tm_cap=1152, tn_cap=512, rider=scatter_rider(pending[1:2] + pending[3:]))
            reduced[i + 1] = finish_reduce(pending, [got_a[0], got_c[0], got_b[0], got_c[1]])
        else:
            du = mm(dproj, wt["in_p"], name="d_proj_in")
            gwp = mm(dproj, s["u"], ta=True, out_dtype=BF, name="g_w_in", tm_cap=1152, tn_cap=512)
        gw_rows = jnp.concatenate([gwp[6 * d + cw:7 * d + cw], gwp[4 * d:6 * d], gwp[2 * d:4 * d], gwp[6 * d:6 * d + cw], gwp[7 * d + cw:7 * d + cw + hh],
                                   gwp[:2 * d]], axis=0)
        g_in = jnp.pad(gw_rows.reshape(NCHIP, p4, d), ((0, 0), (0, p4p - p4), (0, 0)))
        dh, gs["mix_norm"][i] = norm_bwd_add(dh, s["h1"], mix_norm[i:i + 1], du)
        (dn1, s1, dab1), _ = ffn_bwd(dh, s["ab1"], wt["gu1"], wt["d1"])
        g_ffn1 = ffn_wgrads(s["n1"], dh, s1, dab1)
        dh, gs["ffn1_norm"][i] = norm_bwd_add(dh, s["h0"], ffn1_norm[i:i + 1], dn1)
        g_layer = [jnp.concatenate([g_ffn1, g_ffn2], axis=1), g_1024, g_in, g_pp]
        g_layer = [g.reshape(NCHIP, 2, g.shape[1] // 2, g.shape[2]) for g in g_layer]
    from_sibling = swap_rider(g_layer).standalone("grad_swap_halves")
    pending = [add_sibling(g, r_, "grad_add_sibling") for g, r_ in zip(g_layer, from_sibling, strict=True)]
    reduced[0] = finish_reduce(pending, scatter_packs(pending, "grad_scatter"))
    grad_x = dh[None]

    f4 = reduced[0][0].shape[0] // 6
    rows_of = lambda j, lo, hi: jnp.stack([reduced[l][j][lo:hi] for l in range(depth)])
    ffn_rows = lambda j: rows_of(0, j * f4, (j + 1) * f4)
    grads = {
        "ffn1_wg": ffn_rows(0), "ffn1_wu": ffn_rows(1), "ffn1_wd": ffn_rows(2), "ffn2_wg": ffn_rows(3), "ffn2_wu": ffn_rows(4), "ffn2_wd": ffn_rows(5),
        "m_w_out": rows_of(1, 0, 2 * d4), "sc_w_out": rows_of(1, 2 * d4, 3 * d4), "w_o": rows_of(1, 3 * d4, 4 * d4), "ple_w_gate": rows_of(1, 4 * d4, 5 * d4),
        "w_in": rows_of(2, 0, p4), "ple_w_proj": rows_of(3, 0, None),
    }

    small_names = list(SMALL_SHARDED + SMALL_REPL)
    small_full = [g_final[0] if n == "final_norm" else jnp.stack(gs[n]) for n in small_names]
    small_pack = _pack(small_full, LANE, HALO)
    all8 = all_gather_8(small_pack, "gather_small_grads")
    small_sum = add_parts([all8[k] for k in range(8)], F32, "add_small_grads", tm=256)
    for n, tot in zip(small_names, _unpack(small_sum, [a.shape for a in small_full]), strict=True):
        if n in SMALL_SHARDED:
            cl = wts[n].shape[2]
            grads[n] = lax.dynamic_slice_in_dim(tot, k_me * cl, cl, axis=2)
        else:
            grads[n] = tot.reshape(wts[n].shape)

    delta, new_m, new_v = {}, {}, {}
    for n in BIG:
        view = tr if n in TRANSPOSED else (lambda a: a)
        shp = grads[n].shape
        two = lambda a: a.reshape(-1, shp[-1])
        dl, nm, nv = adamw(two(view(wts[n])), two(grads[n]), two(view(mom[n])), two(view(vel[n])), "adamw_" + "x".join(map(str, shp[1:])))
        grads[n], delta[n], new_m[n], new_v[n] = view(grads[n]), view(dl.reshape(shp)), view(nm.reshape(shp)), view(nv.reshape(shp))
    for n in small_names:
        shp = wts[n].shape
        two = lambda a: a.reshape(-1, shp[-1])
        dl, nm, nv = adamw(two(wts[n]), two(grads[n]), two(mom[n]), two(vel[n]), "adamw_small_" + "x".join(map(str, shp)))
        delta[n], new_m[n], new_v[n] = dl.reshape(shp), nm.reshape(shp), nv.reshape(shp)

    return (loss, grad_x, *[grads[n] for n in ORDER], *[delta[n] for n in ORDER], *[new_m[n] for n in ORDER], *[new_v[n] for n in ORDER])
```

```python
import jax
import jax.numpy as jnp
import numpy as np
from jax import lax
from jax.experimental import pallas as pl
from jax.experimental.pallas import tpu as pltpu

BF = jnp.bfloat16
F32 = jnp.float32
EPS = 1e-6
LANE = 128
HALO = 8
SSM_P = 64
SSM_N = 128
SSM_G = 4
SSM_L = 128
ADAM_LR, ADAM_B1, ADAM_B2, ADAM_EPS, ADAM_WD, ADAM_STEP = 0.001, 0.9, 0.999, 1e-08, 0.01, 10
VMEM_LIMIT = 56 * 1024 * 1024
TILE_ELEMS = 400_000
NCHIP = 4
FFN_SUB = 256
MESH = pl.DeviceIdType.MESH
HI = lax.Precision.HIGHEST


def _tile(n, cap, mult=LANE):
    best = None
    t = mult
    while t <= min(n, cap):
        if n % t == 0:
            best = t
        t += mult
    return best if best is not None else n


def _row_tile(r, c, mult=16):
    return _tile(r, max(mult, TILE_ELEMS // c // mult * mult), mult)


def _tile2(r, c, mult=16):
    tm = _row_tile(r, c, mult)
    tc = c if tm * c <= TILE_ELEMS else _tile(c, max(LANE, TILE_ELEMS // tm // LANE * LANE))
    return tm, tc


def _params(sem):
    return pltpu.CompilerParams(dimension_semantics=sem, vmem_limit_bytes=VMEM_LIMIT)


def _sigmoid(x):
    return 1.0 / (1.0 + jnp.exp(-x))


def _dot(a, b, ca=1, cb=0, precision=None):
    return lax.dot_general(a, b, (((ca,), (cb,)), ((), ())), precision=precision, preferred_element_type=F32)


def _rms(x, g):
    r = lax.rsqrt(jnp.mean(x * x, axis=-1, keepdims=True) + EPS)
    return x * r * g


def _rms_bwd(x, g, dy):
    r = lax.rsqrt(jnp.mean(x * x, axis=-1, keepdims=True) + EPS)
    xh = x * r
    dxh = dy * g
    dx = r * (dxh - xh * jnp.mean(dxh * xh, axis=-1, keepdims=True))
    return dx, jnp.sum(dy * xh, axis=0, keepdims=True)


def _accumulate(ref, val, first):
    @pl.when(first)
    def _():
        ref[...] = val

    @pl.when(jnp.logical_not(first))
    def _():
        ref[...] += val


RIDER_MID = 0.85


class Rider:
    def __init__(self, ins, out_shapes, n_sems, start, finish, mid=None):
        self.ins, self.out_shapes, self.n_sems, self.start, self.mid, self.finish = list(ins), list(out_shapes), n_sems, start, mid, finish

    def standalone(self, name):
        ni, no = len(self.ins), len(self.out_shapes)

        def body(*refs):
            parts = (refs[:ni], refs[ni:ni + no], *refs[ni + no:])
            self.start(*parts)
            if self.mid is not None:
                self.mid(*parts)
            self.finish(*parts)

        return _comm_call(body, name, self.ins, self.out_shapes, self.n_sems)


def host_call(body, *, name, grid, in_specs, out_specs, out_shape, scratch_shapes, operands, rider=None, aliases=None):
    n_in, n_out = len(in_specs), len(out_specs)
    aliases = aliases or {}
    if rider is None:
        outs = pl.pallas_call(body, name=name, grid=grid, in_specs=in_specs, out_specs=out_specs, out_shape=out_shape, scratch_shapes=scratch_shapes,
                              input_output_aliases=aliases, compiler_params=_params(("arbitrary",) * len(grid)))(*operands)
        return list(outs), []
    ri, ro = len(rider.ins), len(rider.out_shapes)

    def hosted(*refs):
        ins, r_ins = refs[:n_in], refs[n_in:n_in + ri]
        outs, r_outs = refs[n_in + ri:n_in + ri + n_out], refs[n_in + ri + n_out:n_in + ri + n_out + ro]
        scratch, (send_sems, recv_sems) = refs[n_in + ri + n_out + ro:-2], refs[-2:]
        step, total = 0, 1
        for ax, n in enumerate(grid):
            step = step * n + pl.program_id(ax)
            total *= n
        parts = (r_ins, r_outs, send_sems, recv_sems)

        @pl.when(step == 0)
        def _():
            rider.start(*parts)

        if rider.mid is not None:
            @pl.when(step == min(total - 1, int(total * RIDER_MID)))
            def _():
                rider.mid(*parts)

        body(*ins, *outs, *scratch)

        @pl.when(step == total - 1)
        def _():
            rider.finish(*parts)

    outs = pl.pallas_call(
        hosted,
        name=name,
        grid=grid,
        in_specs=list(in_specs) + [ANY] * ri,
        out_specs=list(out_specs) + [ANY] * ro,
        out_shape=list(out_shape) + rider.out_shapes,
        scratch_shapes=list(scratch_shapes) + [pltpu.SemaphoreType.DMA((rider.n_sems,)), pltpu.SemaphoreType.DMA((rider.n_sems,))],
        input_output_aliases=aliases,
        compiler_params=_params(("arbitrary",) * len(grid)),
    )(*operands, *rider.ins)
    return list(outs[:n_out]), list(outs[n_out:])


def mmx(name, a, b, *, grid, a_spec, b_spec, o_spec, o_shape, o_dtype, ca, cb, acc_shape=None, rider=None):
    nk = grid[-1] if acc_shape is not None else 1

    def body(a_ref, b_ref, o_ref, *acc):
        p = _dot(a_ref[...].astype(BF), b_ref[...].astype(BF), ca, cb)
        if nk == 1:
            o_ref[...] = p.astype(o_ref.dtype)
        else:
            kk = pl.program_id(len(grid) - 1)
            _accumulate(acc[0], p, kk == 0)

            @pl.when(kk == nk - 1)
            def _():
                o_ref[...] = acc[0][...].astype(o_ref.dtype)

    if rider is not None:
        (out,), r_outs = host_call(body, name=name, grid=grid, in_specs=[a_spec, b_spec], out_specs=[o_spec], out_shape=[jax.ShapeDtypeStruct(o_shape, o_dtype)],
                                   scratch_shapes=[pltpu.VMEM(acc_shape, F32)] if nk > 1 else [], operands=(a, b), rider=rider)
        return out, r_outs
    sem = ("parallel",) * (len(grid) - 1) + ("arbitrary" if nk > 1 else "parallel",)
    return pl.pallas_call(
        body,
        name=name,
        grid=grid,
        in_specs=[a_spec, b_spec],
        out_specs=o_spec,
        out_shape=jax.ShapeDtypeStruct(o_shape, o_dtype),
        scratch_shapes=[pltpu.VMEM(acc_shape, F32)] if nk > 1 else [],
        compiler_params=_params(sem),
    )(a, b)


def mm(a, b, *, ta=False, tb=False, out_dtype=F32, name, tm_cap=1024, tn_cap=1024, tk_cap=4096, rider=None):
    m, k = (a.shape[1], a.shape[0]) if ta else a.shape
    n = b.shape[0] if tb else b.shape[1]
    assert (b.shape[1] if tb else b.shape[0]) == k
    tm, tn, tk = _tile(m, tm_cap), _tile(n, tn_cap), _tile(k, tk_cap)
    nk = k // tk
    a_spec = pl.BlockSpec((tk, tm), lambda i, j, kk: (kk, i)) if ta else pl.BlockSpec((tm, tk), lambda i, j, kk: (i, kk))
    b_spec = pl.BlockSpec((tn, tk), lambda i, j, kk: (j, kk)) if tb else pl.BlockSpec((tk, tn), lambda i, j, kk: (kk, j))
    return mmx(name, a, b, grid=(m // tm, n // tn, nk), a_spec=a_spec, b_spec=b_spec, o_spec=pl.BlockSpec((tm, tn), lambda i, j, kk: (i, j)),
               o_shape=(m, n), o_dtype=out_dtype, ca=0 if ta else 1, cb=1 if tb else 0, acc_shape=(tm, tn) if nk > 1 else None, rider=rider)


class Window:
    def __init__(self, out, block, cols, buf=None):
        self.out, self.block, self.cols, self.buf = out, block, cols, buf


def ew(fn, rows, vecs, out_rows, out_red=(), *, tm, name, prev_halo=(), next_halo=(), window=None):
    t = rows[0].shape[0]
    tm = min(tm, t)
    nt = t // tm
    assert t % tm == 0 and (tm % HALO == 0 or (tm == t and not prev_halo and not next_halo))
    nr, nv, npv, nnx, nor = len(rows), len(vecs), len(prev_halo), len(next_halo), len(out_rows)
    hb = tm // HALO
    n_in = nr + nv + npv + nnx
    passed = window is not None and window.buf is not None

    def body(*refs):
        i = pl.program_id(0)
        ins = [r[...].astype(F32) for r in refs[:n_in]]
        outs = refs[n_in + passed:]
        o_rows, o_red = fn(i, nt, ins[:nr], ins[nr:nr + nv], ins[nr + nv:nr + nv + npv], ins[nr + nv + npv:])
        for ref, val in zip(outs[:nor], o_rows, strict=True):
            ref[...] = val.astype(ref.dtype)
        for ref, val in zip(outs[nor:], o_red, strict=True):
            _accumulate(ref, val, i == 0)

    in_specs = [pl.BlockSpec((tm, r.shape[1]), lambda i: (i, 0)) for r in rows]
    in_specs += [pl.BlockSpec(v.shape, lambda i: (0, 0)) for v in vecs]
    in_specs += [pl.BlockSpec((HALO, rows[k].shape[1]), lambda i: (jnp.maximum(i * hb - 1, 0), 0)) for k in prev_halo]
    in_specs += [pl.BlockSpec((HALO, rows[k].shape[1]), lambda i: (jnp.minimum((i + 1) * hb, t // HALO - 1), 0)) for k in next_halo]
    out_specs = [pl.BlockSpec((tm, c), lambda i: (i, 0)) for c, _ in out_rows]
    out_specs += [pl.BlockSpec(s, lambda i: (0, 0)) for s in out_red]
    out_shape = [jax.ShapeDtypeStruct((t, c), d) for c, d in out_rows] + [jax.ShapeDtypeStruct(s, F32) for s in out_red]
    operands = [*rows, *vecs, *[rows[k] for k in prev_halo], *[rows[k] for k in next_halo]]
    aliases = {}
    if window is not None:
        c, dt_ = out_rows[window.out]
        out_specs[window.out] = pl.BlockSpec((tm, c), lambda i: (i, window.block))
        out_shape[window.out] = jax.ShapeDtypeStruct((t, window.cols), dt_)
        if passed:
            in_specs.append(ANY)
            operands.append(window.buf)
            aliases = {n_in: window.out}
    return pl.pallas_call(
        body,
        name=name,
        grid=(nt,),
        in_specs=in_specs,
        out_specs=out_specs,
        out_shape=out_shape,
        input_output_aliases=aliases,
        compiler_params=_params(("arbitrary",) if out_red else ("parallel",)),
    )(*operands)


def _shift_down(x, prev, j):
    if j == 0:
        return x
    r = pltpu.roll(x, j, 0)
    rh = pltpu.roll(prev, j, 0)
    row = lax.broadcasted_iota(jnp.int32, (HALO, x.shape[1]), 0)
    head = jnp.where(row < j, rh, r[:HALO])
    return jnp.concatenate([head, r[HALO:]], axis=0)


def _shift_up(x, nxt, j):
    if j == 0:
        return x
    n = x.shape[0]
    r = pltpu.roll(x, n - j, 0)
    rh = pltpu.roll(nxt, HALO - j, 0)
    row = lax.broadcasted_iota(jnp.int32, (HALO, x.shape[1]), 0)
    tail = jnp.where(row >= HALO - j, rh, r[n - HALO:])
    return jnp.concatenate([r[: n - HALO], tail], axis=0)


def _conv_fwd(x, prev, w):
    kk = w.shape[0]
    acc = None
    for k in range(kk):
        term = w[k:k + 1, :] * _shift_down(x, prev, kk - 1 - k)
        acc = term if acc is None else acc + term
    return acc


def ffn_up(h, g, wf, rider=None):
    t, d = h.shape
    f4 = wf.shape[1] // 2
    tm = _tile(t, 1024)
    sub = _tile(tm, FFN_SUB, 16)

    def body(h_ref, g_ref, wg_ref, wu_ref, ab_ref, s_ref, n_ref):
        @pl.when(pl.program_id(1) == 0)
        def _():
            n_ref[...] = _rms(h_ref[...], g_ref[...]).astype(BF)

        for r in range(tm // sub):
            rows = slice(r * sub, (r + 1) * sub)
            n = n_ref[rows, :]
            a = _dot(n, wg_ref[...], 1, 1)
            b = _dot(n, wu_ref[...], 1, 1)
            ab_ref[0, rows, :] = a.astype(BF)
            ab_ref[1, rows, :] = b.astype(BF)
            s_ref[rows, :] = (a * _sigmoid(a) * b).astype(BF)

    wspec = lambda ib: pl.BlockSpec((None, f4, d), lambda i, j: (j, ib, 0))
    return host_call(
        body,
        name="ffn_up",
        grid=(t // tm, NCHIP),
        in_specs=[pl.BlockSpec((tm, d), lambda i, j: (i, 0)), pl.BlockSpec((1, d), lambda i, j: (0, 0)), wspec(0), wspec(1)],
        out_specs=[pl.BlockSpec((2, None, tm, f4), lambda i, j: (0, j, i, 0)), pl.BlockSpec((None, tm, f4), lambda i, j: (j, i, 0)),
                   pl.BlockSpec((tm, d), lambda i, j: (i, 0))],
        out_shape=[jax.ShapeDtypeStruct((2, NCHIP, t, f4), BF), jax.ShapeDtypeStruct((NCHIP, t, f4), BF), jax.ShapeDtypeStruct((t, d), BF)],
        scratch_shapes=[],
        operands=(h, g, wf, wf),
        rider=rider,
    )


def ffn_down(s4, wf, h, rider=None):
    t, d = h.shape
    f4 = s4.shape[2]
    tm = _tile(t, 512)

    def body(s_ref, w_ref, h_ref, o_ref):
        acc = _dot(s_ref[0], w_ref[0])
        for k in range(1, NCHIP):
            acc = acc + _dot(s_ref[k], w_ref[k])
        o_ref[...] = h_ref[...] + 0.5 * acc

    (out,), r_outs = host_call(
        body,
        name="ffn_down",
        grid=(t // tm,),
        in_specs=[pl.BlockSpec((NCHIP, tm, f4), lambda i: (0, i, 0)), pl.BlockSpec((NCHIP, f4, d), lambda i: (0, 0, 0)), pl.BlockSpec((tm, d), lambda i: (i, 0))],
        out_specs=[pl.BlockSpec((tm, d), lambda i: (i, 0))],
        out_shape=[jax.ShapeDtypeStruct((t, d), F32)],
        scratch_shapes=[],
        operands=(s4, wf, h),
        rider=rider,
    )
    return out, r_outs


def ffn_bwd(dho, ab, wf, wd, rider=None):
    t, d = dho.shape
    f4 = wd.shape[1]
    tm = _tile(t, 1024)
    sub = _tile(tm, FFN_SUB, 16)

    def body(dho_ref, ab_ref, wg_ref, wu_ref, wd_ref, dn_ref, s_ref, dab_ref, do_sc):
        j = pl.program_id(1)

        @pl.when(j == 0)
        def _():
            do_sc[...] = (0.5 * dho_ref[...]).astype(BF)
            dn_ref[...] = jnp.zeros_like(dn_ref)

        for r in range(tm // sub):
            rows = slice(r * sub, (r + 1) * sub)
            ds = _dot(do_sc[rows, :], wd_ref[...], 1, 1)
            av, bv = ab_ref[0, rows, :].astype(F32), ab_ref[1, rows, :].astype(F32)
            sig = _sigmoid(av)
            sl = av * sig
            s_ref[rows, :] = (sl * bv).astype(BF)
            da = (ds * bv * (sig * (1.0 + av * (1.0 - sig)))).astype(BF)
            db = (ds * sl).astype(BF)
            dab_ref[0, rows, :] = da
            dab_ref[1, rows, :] = db
            dn_ref[rows, :] += _dot(da, wg_ref[...]) + _dot(db, wu_ref[...])

    row = lambda c: pl.BlockSpec((tm, c), lambda i, j: (i, 0))
    wspec = lambda ib: pl.BlockSpec((None, f4, d), lambda i, j: (j, ib, 0))
    ab_spec = pl.BlockSpec((2, None, tm, f4), lambda i, j: (0, j, i, 0))
    return host_call(
        body,
        name="ffn_bwd",
        grid=(t // tm, NCHIP),
        in_specs=[row(d), ab_spec, wspec(0), wspec(1), wspec(0)],
        out_specs=[row(d), pl.BlockSpec((None, tm, f4), lambda i, j: (j, i, 0)), ab_spec, row(d)],
        out_shape=[jax.ShapeDtypeStruct((t, d), F32), jax.ShapeDtypeStruct((NCHIP, t, f4), BF), jax.ShapeDtypeStruct((2, NCHIP, t, f4), BF),
                   jax.ShapeDtypeStruct((t, d), BF)],
        scratch_shapes=[],
        operands=(dho, ab, wf, wf, wd),
        rider=rider,
    )


def ffn_wgrads(n, do, s4, dab):
    t, d = n.shape
    f4 = s4.shape[2]
    g_in = mmx("g_ffn_in", dab, n, grid=(2, NCHIP, 1), a_spec=pl.BlockSpec((None, None, t, f4), lambda wh, k, j: (wh, k, 0, 0)),
               b_spec=pl.BlockSpec((t, d), lambda wh, k, j: (0, 0)), o_spec=pl.BlockSpec((None, None, f4, d), lambda wh, k, j: (k, wh, 0, 0)),
               o_shape=(NCHIP, 2, f4, d), o_dtype=BF, ca=0, cb=0)
    g_out = mmx("g_ffn_out", s4, do, grid=(NCHIP, 1), a_spec=pl.BlockSpec((None, t, f4), lambda k, j: (k, 0, 0)),
                b_spec=pl.BlockSpec((t, d), lambda k, j: (0, 0)), o_spec=pl.BlockSpec((None, f4, d), lambda k, j: (k, 0, 0)),
                o_shape=(NCHIP, f4, d), o_dtype=BF, ca=0, cb=0)
    return jnp.concatenate([g_in.reshape(NCHIP, 2 * f4, d), g_out], axis=1)


def norm_cast(h, g):
    def fn(i, nt, rows, vecs, prevs, nexts):
        return [_rms(rows[0], vecs[0])], []
    return ew(fn, [h], [g], [(h.shape[1], BF)], tm=512, name="norm_cast")[0]


def _zero_if(cond, x):
    return jnp.where(cond, jnp.zeros_like(x), x)


def conv_a_fwd(sc3, w_sc):
    d = sc3.shape[1] // 3

    def fn(i, nt, rows, vecs, prevs, nexts):
        x, pv = rows[0], _zero_if(i == 0, prevs[0])
        v = x[:, d:2 * d] * x[:, 2 * d:]
        vp = pv[:, d:2 * d] * pv[:, 2 * d:]
        return [x[:, :d] * _conv_fwd(v, vp, vecs[0])], []

    return ew(fn, [sc3], [w_sc], [(d, BF)], tm=256, name="conv_a_fwd", prev_halo=(0,))[0]


def _softplus(x):
    e = jnp.exp(-jnp.abs(x))
    return jnp.maximum(x, 0.0) + jnp.where(e < 1e-4, e - 0.5 * e * e, jnp.log(1.0 + e))


def conv_m_fwd(xbc_raw, dt_raw, w_mc, b_mc, dt_bias):
    def fn(i, nt, rows, vecs, prevs, nexts):
        pre = _conv_fwd(rows[0], _zero_if(i == 0, prevs[0]), vecs[0]) + vecs[1]
        return [pre * _sigmoid(pre), _softplus(rows[1] + vecs[2])], []

    return ew(fn, [xbc_raw, dt_raw], [w_mc, b_mc, dt_bias], [(xbc_raw.shape[1], F32), (LANE, F32)], tm=256, name="conv_m_fwd",
              prev_halo=(0,))


def conv_m_bwd1(dxbc, xbc_raw, ddt, dt_raw, w_mc, b_mc, dt_bias, window):
    def fn(i, nt, rows, vecs, prevs, nexts):
        pre = _conv_fwd(rows[1], _zero_if(i == 0, prevs[0]), vecs[0]) + vecs[1]
        sig = _sigmoid(pre)
        dpre = rows[0] * (sig * (1.0 + pre * (1.0 - sig)))
        ddr = rows[2] * _sigmoid(rows[3] + vecs[2])
        return [dpre, ddr], [jnp.sum(ddr, axis=0, keepdims=True)]

    return ew(fn, [dxbc, xbc_raw, ddt, dt_raw], [w_mc, b_mc, dt_bias], [(dxbc.shape[1], F32), (LANE, BF)], [(1, LANE)], tm=256,
              name="conv_m_bwd1", prev_halo=(1,), window=window)


def conv_bwd2(dpre, x, w, name, window):
    kk = w.shape[0]

    def fn(i, nt, rows, vecs, prevs, nexts):
        dp, xv = rows[0], rows[1]
        nx = _zero_if(i == nt - 1, nexts[0])
        dx = None
        dws = []
        for k in range(kk):
            up = _shift_up(dp, nx, kk - 1 - k)
            term = vecs[0][k:k + 1, :] * up
            dx = term if dx is None else dx + term
            dws.append(jnp.sum(up * xv, axis=0, keepdims=True))
        return [dx], [jnp.concatenate(dws, axis=0), jnp.sum(dp, axis=0, keepdims=True)]

    c = x.shape[1]
    return ew(fn, [dpre, x], [w], [(c, BF)], [(kk, c), (1, c)], tm=256, name=name, next_halo=(0,), window=window)


def conv_a_bwd1(dya, sc3, w_sc, window):
    d = sc3.shape[1] // 3

    def fn(i, nt, rows, vecs, prevs, nexts):
        x, pv = rows[1], _zero_if(i == 0, prevs[0])
        v = x[:, d:2 * d] * x[:, 2 * d:]
        vp = pv[:, d:2 * d] * pv[:, 2 * d:]
        return [rows[0] * x[:, :d], rows[0] * _conv_fwd(v, vp, vecs[0]), v], []

    return ew(fn, [dya, sc3], [w_sc], [(d, F32), (d, BF), (d, F32)], tm=256, name="conv_a_bwd1", prev_halo=(1,), window=window)


def conv_a_bwd2(dcv, v, sc3, w_sc, window):
    d = v.shape[1]
    kk = w_sc.shape[0]

    def fn(i, nt, rows, vecs, prevs, nexts):
        dp, vv, x = rows
        nx = _zero_if(i == nt - 1, nexts[0])
        dv = None
        dws = []
        for k in range(kk):
            up = _shift_up(dp, nx, kk - 1 - k)
            term = vecs[0][k:k + 1, :] * up
            dv = term if dv is None else dv + term
            dws.append(jnp.sum(up * vv, axis=0, keepdims=True))
        return [jnp.concatenate([dv * x[:, 2 * d:], dv * x[:, d:2 * d]], axis=1)], [jnp.concatenate(dws, axis=0)]

    return ew(fn, [dcv, v, sc3], [w_sc], [(2 * d, BF)], [(kk, d)], tm=256, name="conv_a_bwd2", next_halo=(0,), window=window)


def _xdot(a, b, passes, split_lhs, ca=1, cb=0):
    parts, r = [], (a if split_lhs else b)
    for _ in range(passes):
        piece = r.astype(BF)
        parts.append(piece)
        r = r - piece.astype(F32)
    other = (b if split_lhs else a).astype(BF)
    acc = None
    for piece in parts:
        term = _dot(piece, other, ca, cb) if split_lhs else _dot(other, piece, ca, cb)
        acc = term if acc is None else acc + term
    return acc


def _ssd_common(xbc_ref, dt_ref, alog_ref, e_ref, w):
    ll = SSM_L
    xs = xbc_ref[:, 0:w]
    dtv = dt_ref[...]
    a_row = -jnp.exp(alog_ref[...])
    a = dtv * a_row
    row = lax.broadcasted_iota(jnp.int32, (ll, ll), 0)
    col = lax.broadcasted_iota(jnp.int32, (ll, ll), 1)
    tril = (row >= col).astype(F32)
    triu = (row <= col).astype(F32)
    acl = _xdot(tril, a, 3, False)
    acl_t = _xdot(a, triu, 3, True, 0, 0)
    e = e_ref[...]
    aclx = _xdot(acl, e, 3, True)
    dtx = _xdot(dtv, e, 2, True)
    last = aclx[ll - 1:ll, :]
    e_in = jnp.exp(aclx)
    e_end = jnp.exp(last - aclx)
    e_tot = jnp.exp(last)
    x = xs * dtx
    return dict(xs=xs, dtv=dtv, a_row=a_row, a=a, row=row, col=col, triu=triu, acl=acl, acl_t=acl_t, dtx=dtx, e_in=e_in, e_end=e_end,
                e_tot=e_tot, x=x)


def _decay(q, hh):
    diff = q["acl"][:, hh:hh + 1] - q["acl_t"][hh:hh + 1, :]
    return jnp.exp(jnp.where(q["row"] >= q["col"], diff, -jnp.inf))


def ssd_fwd(xbc, dt, z, a_log, d_exp, m_norm, e_mat, rider=None):
    t = xbc.shape[0]
    w = z.shape[1]
    gn = SSM_G * SSM_N
    gw = w // SSM_G
    ll, nn = SSM_L, SSM_N
    nc = t // ll
    cw = xbc.shape[1]

    def body(xbc_ref, dt_ref, z_ref, alog_ref, dexp_ref, mn_ref, e_ref, yn_ref, y_ref, sp_ref, s_sc):
        c = pl.program_id(0)

        @pl.when(c == 0)
        def _():
            s_sc[...] = jnp.zeros_like(s_sc)

        q = _ssd_common(xbc_ref, dt_ref, alog_ref, e_ref, w)
        xb = q["x"].astype(BF)
        xsb = (q["x"] * q["e_end"]).astype(BF)
        sp = s_sc[...]
        sp_ref[0] = sp
        spb = sp.astype(BF)
        lane = lax.broadcasted_iota(jnp.int32, (ll, LANE), 1)
        for g in range(SSM_G):
            lo = g * gw
            bg = xbc_ref[:, w + g * nn:w + (g + 1) * nn].astype(BF)
            cg = xbc_ref[:, w + gn + g * nn:w + gn + (g + 1) * nn].astype(BF)
            yoff = _dot(cg, spb[:, lo:lo + gw]) * q["e_in"][:, lo:lo + gw]
            s_sc[:, lo:lo + gw] = sp[:, lo:lo + gw] * q["e_tot"][:, lo:lo + gw] + _dot(bg, xsb[:, lo:lo + gw], 0, 0)
            cb = _dot(cg, bg, 1, 1)
            for pr in range(gw // LANE):
                l0 = lo + pr * LANE
                xp = xb[:, l0:l0 + LANE]
                ys = []
                for hh in (l0 // SSM_P, l0 // SSM_P + 1):
                    wm = (cb * _decay(q, hh)).astype(BF)
                    ys.append(_dot(wm, xp))
                ydiag = jnp.where(lane < SSM_P, ys[0], ys[1])
                y_ref[:, l0:l0 + LANE] = ydiag + yoff[:, pr * LANE:(pr + 1) * LANE] + dexp_ref[:, l0:l0 + LANE] * q["xs"][:, l0:l0 + LANE]
        zv = z_ref[...].astype(F32)
        yz = y_ref[...] * (zv * _sigmoid(zv))
        for g in range(SSM_G):
            lo = g * gw
            yn_ref[:, lo:lo + gw] = _rms(yz[:, lo:lo + gw], mn_ref[:, lo:lo + gw]).astype(BF)

    vec = lambda s: pl.BlockSpec(s, lambda c: (0, 0))
    return host_call(
        body,
        name="ssd_fwd",
        grid=(nc,),
        in_specs=[
            pl.BlockSpec((ll, cw), lambda c: (c, 0)), pl.BlockSpec((ll, LANE), lambda c: (c, 0)), pl.BlockSpec((ll, w), lambda c: (c, 0)),
            vec((1, LANE)), vec((1, w)), vec((1, w)), vec((LANE, w)),
        ],
        out_specs=[pl.BlockSpec((ll, w), lambda c: (c, 0)), pl.BlockSpec((ll, w), lambda c: (c, 0)), pl.BlockSpec((1, nn, w), lambda c: (c, 0, 0))],
        out_shape=[jax.ShapeDtypeStruct((t, w), BF), jax.ShapeDtypeStruct((t, w), F32), jax.ShapeDtypeStruct((nc, nn, w), F32)],
        scratch_shapes=[pltpu.VMEM((nn, w), F32)],
        operands=(xbc, dt, z, a_log, d_exp, m_norm, e_mat),
        rider=rider,
    )


def ssd_bwd(dyn, y, z, xbc, dt, sprev, a_log, d_exp, m_norm, e_mat, et_mat, window, rider=None):
    t = xbc.shape[0]
    w = z.shape[1]
    gn = SSM_G * SSM_N
    gw = w // SSM_G
    ll, nn = SSM_L, SSM_N
    nc = t // ll
    cw = xbc.shape[1]

    def body(dyn_ref, y_ref, z_ref, xbc_ref, dt_ref, sp_ref, alog_ref, dexp_ref, mn_ref, e_ref, et_ref,
             dz_ref, dxbc_ref, ddt_ref, dmn_ref, dd_ref, dal_ref, ds_sc, dy_sc, dx_sc):
        step = pl.program_id(0)

        @pl.when(step == 0)
        def _():
            ds_sc[...] = jnp.zeros_like(ds_sc)

        zv, yv = z_ref[...].astype(F32), y_ref[...]
        sg = _sigmoid(zv)
        sz = zv * sg
        yz = yv * sz
        dmn = []
        for g in range(SSM_G):
            lo = g * gw
            dseg, dmn_g = _rms_bwd(yz[:, lo:lo + gw], mn_ref[:, lo:lo + gw], dyn_ref[:, lo:lo + gw])
            dy_sc[:, lo:lo + gw] = dseg
            dmn.append(dmn_g)
        dmn = jnp.concatenate(dmn, axis=1)
        dyz = dy_sc[...]
        dz_ref[...] = (dyz * yv * (sg * (1.0 + zv * (1.0 - sg)))).astype(BF)
        dy = dyz * sz

        q = _ssd_common(xbc_ref, dt_ref, alog_ref, e_ref, w)
        x = q["x"]
        xb = x.astype(BF)
        xsb = (x * q["e_end"]).astype(BF)
        sp = sp_ref[0]
        spb = sp.astype(BF)
        dsn = ds_sc[...]
        dsnb = dsn.astype(BF)
        dyb = dy.astype(BF)
        lane = lax.broadcasted_iota(jnp.int32, (ll, LANE), 1)
        lane1 = lax.broadcasted_iota(jnp.int32, (1, LANE), 1)
        sub1 = lax.broadcasted_iota(jnp.int32, (LANE, 1), 0)
        dacl = jnp.zeros((ll, LANE), F32)
        dacl_t = jnp.zeros((LANE, ll), F32)
        d_ein, d_eend, d_etot = [], [], []
        for g in range(SSM_G):
            lo = g * gw
            sl = slice(lo, lo + gw)
            bg = xbc_ref[:, w + g * nn:w + (g + 1) * nn].astype(BF)
            cg = xbc_ref[:, w + gn + g * nn:w + gn + (g + 1) * nn].astype(BF)
            zg = _dot(cg, spb[:, sl])
            dzz = (dy[:, sl] * q["e_in"][:, sl]).astype(BF)
            d_ein.append(dy[:, sl] * zg)
            dcg = _dot(dzz, spb[:, sl], 1, 1)
            ds_sc[:, sl] = _dot(cg, dzz, 0, 0) + dsn[:, sl] * q["e_tot"][:, sl]
            d_etot.append(jnp.sum(dsn[:, sl] * sp[:, sl], axis=0, keepdims=True))
            dbg = _dot(xsb[:, sl], dsnb[:, sl], 1, 1)
            dxs_g = _dot(bg, dsnb[:, sl])
            d_eend.append(dxs_g * x[:, sl])
            cb = _dot(cg, bg, 1, 1)
            dcb = jnp.zeros((ll, ll), F32)
            for pr in range(gw // LANE):
                l0 = lo + pr * LANE
                xp = xb[:, l0:l0 + LANE]
                dyp = dyb[:, l0:l0 + LANE]
                dxp = []
                for hi, hh in enumerate((l0 // SSM_P, l0 // SSM_P + 1)):
                    lm = _decay(q, hh)
                    wm = (cb * lm).astype(BF)
                    in_head = (lane < SSM_P) if hi == 0 else (lane >= SSM_P)
                    dwm = _dot(jnp.where(in_head, dyp, jnp.zeros_like(dyp)), xp, 1, 1)
                    dxp.append(_dot(wm, dyp, 0, 0))
                    dlm = dwm * lm
                    dcb = dcb + dlm
                    dd = dlm * cb
                    dacl = dacl + jnp.sum(dd, axis=1, keepdims=True) * (lane1 == hh).astype(F32)
                    dacl_t = dacl_t + (sub1 == hh).astype(F32) * jnp.sum(dd, axis=0, keepdims=True)
                dx_sc[:, l0:l0 + LANE] = jnp.where(lane < SSM_P, dxp[0], dxp[1]) + dxs_g[:, pr * LANE:(pr + 1) * LANE] * q["e_end"][:, l0:l0 + LANE]
            dcbb = dcb.astype(BF)
            dxbc_ref[:, w + g * nn:w + (g + 1) * nn] = dbg + _dot(dcbb, cg, 0, 0)
            dxbc_ref[:, w + gn + g * nn:w + gn + (g + 1) * nn] = dcg + _dot(dcbb, bg)
        d_ein = jnp.concatenate(d_ein, axis=1) * q["e_in"]
        d_eend = jnp.concatenate(d_eend, axis=1) * q["e_end"]
        d_etot = jnp.concatenate(d_etot, axis=1) * q["e_tot"]
        et = et_ref[...]
        last_add = jnp.sum(d_eend, axis=0, keepdims=True) + d_etot
        last_add = _xdot(jnp.broadcast_to(last_add, (HALO, w)), et, 2, True)[0:1]
        row1 = lax.broadcasted_iota(jnp.int32, (ll, LANE), 0)
        dacl = dacl + _xdot(d_ein - d_eend, et, 2, True) + jnp.where(row1 == ll - 1, last_add, 0.0)
        da = _xdot(q["triu"], dacl, 2, False) - _xdot(q["triu"], dacl_t, 2, False, 1, 1)
        dxv = dx_sc[...]
        dxbc_ref[:, 0:w] = dexp_ref[...] * dy + dxv * q["dtx"]
        ddt_ref[...] = _xdot(dxv * q["xs"], et, 2, True) + da * q["a_row"]
        dal = jnp.sum(da * q["dtv"], axis=0, keepdims=True) * q["a_row"]
        ddv = jnp.sum(dy * q["xs"], axis=0, keepdims=True)
        ddv = _xdot(jnp.broadcast_to(ddv, (HALO, w)), et, 2, True)[0:1]
        _accumulate(dmn_ref, dmn, step == 0)
        _accumulate(dd_ref, ddv, step == 0)
        _accumulate(dal_ref, dal, step == 0)

    rev = lambda c_: pl.BlockSpec((ll, c_), lambda s: (nc - 1 - s, 0))
    vec = lambda s_: pl.BlockSpec(s_, lambda s: (0, 0))
    n_in = 11

    def body_skipping_buffer(*refs):
        body(*refs[:n_in], *refs[n_in + 1:])

    return host_call(
        body_skipping_buffer,
        name="ssd_bwd",
        grid=(nc,),
        in_specs=[
            rev(w), rev(w), rev(w), rev(cw), rev(LANE), pl.BlockSpec((1, nn, w), lambda s: (nc - 1 - s, 0, 0)),
            vec((1, LANE)), vec((1, w)), vec((1, w)), vec((LANE, w)), vec((w, LANE)), ANY,
        ],
        out_specs=[pl.BlockSpec((ll, w), lambda s: (nc - 1 - s, window.block)), rev(cw), rev(LANE), vec((1, w)), vec((1, LANE)), vec((1, LANE))],
        out_shape=[
            jax.ShapeDtypeStruct((t, window.cols), BF), jax.ShapeDtypeStruct((t, cw), F32), jax.ShapeDtypeStruct((t, LANE), F32),
            jax.ShapeDtypeStruct((1, w), F32), jax.ShapeDtypeStruct((1, LANE), F32), jax.ShapeDtypeStruct((1, LANE), F32),
        ],
        scratch_shapes=[pltpu.VMEM((nn, w), F32), pltpu.VMEM((ll, w), F32), pltpu.VMEM((ll, w), F32)],
        operands=(dyn, y, z, xbc, dt, sprev, a_log, d_exp, m_norm, e_mat, et_mat, window.buf),
        rider=rider,
        aliases={n_in: 0},
    )


def _w1024_spec(d, nblk, iblk):
    r = nblk * (d // NCHIP)
    return pl.BlockSpec((NCHIP, r, d), lambda i: (0, iblk // nblk, 0))


def _whole(ref):
    v = ref[...]
    return v.reshape(v.shape[0] * v.shape[1], v.shape[2])


def mix_out_fwd(ya_in, yn, gates, h, w1024, rider=None):
    t, d = h.shape
    tm = _tile(t, 256)

    def body(ya_ref, yn_ref, g_ref, h_ref, wm_ref, wa_ref, wo_ref, ho_ref, oa_ref, om_ref, mg_ref):
        y_a = _dot(ya_ref[...], _whole(wa_ref))
        y_m = _dot(yn_ref[...], _whole(wm_ref))
        oa_ref[...] = y_a
        om_ref[...] = y_m
        gv = g_ref[...].astype(F32)
        mg = (_sigmoid(gv[:, :d]) * y_a + _sigmoid(gv[:, d:]) * y_m).astype(BF)
        mg_ref[...] = mg
        ho_ref[...] = h_ref[...] + _dot(mg, _whole(wo_ref))

    row = lambda c: pl.BlockSpec((tm, c), lambda i: (i, 0))
    return host_call(
        body,
        name="mix_out_fwd",
        grid=(t // tm,),
        in_specs=[row(d), row(2 * d), row(2 * d), row(d), _w1024_spec(d, 2, 0), _w1024_spec(d, 1, 2), _w1024_spec(d, 1, 3)],
        out_specs=[row(d), row(d), row(d), row(d)],
        out_shape=[jax.ShapeDtypeStruct((t, d), F32), jax.ShapeDtypeStruct((t, d), F32), jax.ShapeDtypeStruct((t, d), F32),
                   jax.ShapeDtypeStruct((t, d), BF)],
        scratch_shapes=[],
        operands=(ya_in, yn, gates, h, w1024, w1024, w1024),
        rider=rider,
    )


def mix_out_bwd(dh, gates, y_a, y_m, w1024, cols):
    t, d = dh.shape
    tm = _tile(t, 256)

    def body(dh_ref, g_ref, ya_ref, ym_ref, wm_ref, wa_ref, wo_ref, dg_ref, dya_ref, dyn_ref, da_ref, dm_ref):
        dmg = _dot(dh_ref[...].astype(BF), _whole(wo_ref), 1, 1)
        gv = g_ref[...].astype(F32)
        sa, sm = _sigmoid(gv[:, :d]), _sigmoid(gv[:, d:])
        dg_ref[:, :d] = (dmg * ya_ref[...] * sa * (1.0 - sa)).astype(BF)
        dg_ref[:, d:] = (dmg * ym_ref[...] * sm * (1.0 - sm)).astype(BF)
        da = (dmg * sa).astype(BF)
        dm = (dmg * sm).astype(BF)
        da_ref[...] = da
        dm_ref[...] = dm
        dya_ref[...] = _dot(da, _whole(wa_ref), 1, 1)
        dyn_ref[...] = _dot(dm, _whole(wm_ref), 1, 1)

    row = lambda c: pl.BlockSpec((tm, c), lambda i: (i, 0))
    return pl.pallas_call(
        body,
        name="mix_out_bwd",
        grid=(t // tm,),
        in_specs=[row(d), row(2 * d), row(d), row(d), _w1024_spec(d, 2, 0), _w1024_spec(d, 1, 2), _w1024_spec(d, 1, 3)],
        out_specs=[row(2 * d), row(d), row(2 * d), row(d), row(d)],
        out_shape=[jax.ShapeDtypeStruct((t, cols), BF), jax.ShapeDtypeStruct((t, d), F32), jax.ShapeDtypeStruct((t, 2 * d), F32),
                   jax.ShapeDtypeStruct((t, d), BF), jax.ShapeDtypeStruct((t, d), BF)],
        compiler_params=_params(("parallel",)),
    )(dh, gates, y_a, y_m, w1024, w1024, w1024)


def norm_bwd_add(dh, h, g, dn):
    def fn(i, nt, rows, vecs, prevs, nexts):
        dx, dg = _rms_bwd(rows[1], vecs[0], rows[2])
        return [rows[0] + dx], [dg]
    d = h.shape[1]
    return ew(fn, [dh, h, dn], [g], [(d, F32)], [(1, d)], tm=512, name="norm_bwd_add")


def _pe(p, wpp_ref):
    pb = p.astype(BF)
    return jnp.concatenate([_dot(pb, wpp_ref[k]) for k in range(NCHIP)], axis=1)


def ple_fwd(h, g, p, w1024, wpp):
    t, d = h.shape
    tm = _tile(t, 512)

    def body(h_ref, g_ref, p_ref, wg_ref, wp_ref, ho_ref):
        hv = h_ref[...]
        gate = _sigmoid(_dot(_rms(hv, g_ref[...]).astype(BF), _whole(wg_ref)))
        ho_ref[...] = hv + gate * _pe(p_ref[...], wp_ref)

    row = lambda c: pl.BlockSpec((tm, c), lambda i: (i, 0))
    wpp_spec = pl.BlockSpec(wpp.shape, lambda i: (0, 0, 0))
    return pl.pallas_call(
        body,
        name="ple_fwd",
        grid=(t // tm,),
        in_specs=[row(d), pl.BlockSpec((1, d), lambda i: (0, 0)), row(p.shape[1]), _w1024_spec(d, 1, 4), wpp_spec],
        out_specs=row(d),
        out_shape=jax.ShapeDtypeStruct((t, d), F32),
        compiler_params=_params(("parallel",)),
    )(h, g, p, w1024, wpp)


def ple_bwd(dho, h, g, p, w1024, wpp):
    t, d = h.shape
    tm = _tile(t, 512)

    def body(dho_ref, h_ref, g_ref, p_ref, wg_ref, wp_ref, dh_ref, dg_ref, n_ref, dgp_ref, dpe_ref):
        hv, dv = h_ref[...], dho_ref[...]
        n = _rms(hv, g_ref[...]).astype(BF)
        n_ref[...] = n
        wg = _whole(wg_ref)
        gate = _sigmoid(_dot(n, wg))
        pe = _pe(p_ref[...], wp_ref)
        dpe_ref[...] = (dv * gate).astype(BF)
        dgp = (dv * pe * gate * (1.0 - gate)).astype(BF)
        dgp_ref[...] = dgp
        dx, dg = _rms_bwd(hv, g_ref[...], _dot(dgp, wg, 1, 1))
        dh_ref[...] = dv + dx
        _accumulate(dg_ref, dg, pl.program_id(0) == 0)

    row = lambda c: pl.BlockSpec((tm, c), lambda i: (i, 0))
    wpp_spec = pl.BlockSpec(wpp.shape, lambda i: (0, 0, 0))
    return pl.pallas_call(
        body,
        name="ple_bwd",
        grid=(t // tm,),
        in_specs=[row(d), row(d), pl.BlockSpec((1, d), lambda i: (0, 0)), row(p.shape[1]), _w1024_spec(d, 1, 4), wpp_spec],
        out_specs=[row(d), pl.BlockSpec((1, d), lambda i: (0, 0)), row(d), row(d), row(d)],
        out_shape=[jax.ShapeDtypeStruct((t, d), F32), jax.ShapeDtypeStruct((1, d), F32), jax.ShapeDtypeStruct((t, d), BF),
                   jax.ShapeDtypeStruct((t, d), BF), jax.ShapeDtypeStruct((t, d), BF)],
        compiler_params=_params(("arbitrary",)),
    )(dho, h, g, p, w1024, wpp)


def loss_bwd(h, g, target):
    d = h.shape[1]

    def fn(i, nt, rows, vecs, prevs, nexts):
        err = _rms(rows[0], vecs[0]) - rows[1]
        dx, dg = _rms_bwd(rows[0], vecs[0], err * (1.0 / d))
        return [dx], [jnp.sum(err * err, axis=0, keepdims=True) * (0.5 / d), dg]

    return ew(fn, [h, target], [g], [(d, F32)], [(1, d), (1, d)], tm=512, name="loss_bwd")


def adamw(w, g, m, v, name):
    c1, c2 = 1.0 / (1.0 - ADAM_B1 ** ADAM_STEP), 1.0 / (1.0 - ADAM_B2 ** ADAM_STEP)

    def fn(i, nt, rows, vecs, prevs, nexts):
        wv, gv, mv, vv = rows
        mn = ADAM_B1 * mv + (1.0 - ADAM_B1) * gv
        vn = ADAM_B2 * vv + (1.0 - ADAM_B2) * (gv * gv)
        delta = -ADAM_LR * ((mn * c1) / (jnp.sqrt(vn * c2) + ADAM_EPS) + ADAM_WD * wv)
        return [delta, mn, vn], []

    c = w.shape[1]
    return ew(fn, [w, g, m, v], [], [(c, F32)] * 3, tm=_row_tile(w.shape[0], c, HALO), name=name)


def _place():
    return lax.axis_index("x"), lax.axis_index("y"), lax.axis_index("c")


def _other_chips(x, y):
    return [(1 - x, y), (x, 1 - y), (1 - x, 1 - y)]


ANY = pl.BlockSpec(memory_space=pl.ANY)


def _comm_call(body, name, ins, out_shapes, n_sems, aliases=None):
    return pl.pallas_call(
        body,
        name=name,
        in_specs=[ANY] * len(ins),
        out_specs=[ANY] * len(out_shapes),
        out_shape=out_shapes,
        scratch_shapes=[pltpu.SemaphoreType.DMA((n_sems,)), pltpu.SemaphoreType.DMA((n_sems,))],
        input_output_aliases=aliases or {},
    )(*ins)


def gather_rider(packs):
    nt = len(packs)

    def pieces(ins, outs, send_sems, recv_sems):
        x, y, cc = _place()
        chips = _other_chips(x, y)
        sibling = (x, y, 1 - cc)
        k_me = 2 * x + y

        def copy(k, src, dst, to):
            return pltpu.make_async_remote_copy(src_ref=src, dst_ref=dst, send_sem=send_sems.at[k], recv_sem=recv_sems.at[k],
                                                device_id=to, device_id_type=MESH)

        sends, forwards, arrivals = [], [], []
        for ti in range(nt):
            for j, (px, py) in enumerate(chips):
                sends.append(copy(7 * ti + j, ins[ti].at[cc], outs[ti].at[k_me, cc], (px, py, cc)))
                landed = outs[ti].at[2 * px + py, cc]
                forwards.append((copy(7 * ti + j, landed, landed, (px, py, cc)), copy(7 * ti + 3 + j, landed, landed, sibling)))
                passed = outs[ti].at[2 * px + py, 1 - cc]
                arrivals.append(copy(7 * ti + 3 + j, passed, passed, sibling))
            sends.append(copy(7 * ti + 6, ins[ti], outs[ti].at[k_me], sibling))
            own = outs[ti].at[k_me]
            arrivals.append(copy(7 * ti + 6, own, own, sibling))
        return sends, forwards, arrivals

    def start(*parts):
        for cp in pieces(*parts)[0]:
            cp.start()

    def mid(*parts):
        for landed, forward in pieces(*parts)[1]:
            landed.wait_recv()
            forward.start()

    def finish(*parts):
        sends, forwards, arrivals = pieces(*parts)
        for cp in arrivals:
            cp.wait_recv()
        for cp in sends + [f for _, f in forwards]:
            cp.wait_send()

    return Rider(packs, [jax.ShapeDtypeStruct((NCHIP,) + p.shape, p.dtype) for p in packs], 7 * nt, start, finish, mid)


def swap_rider(gs):
    nt = len(gs)
    hl = gs[0].shape[1] // 2

    def copies(ins, outs, send_sems, recv_sems):
        x, y, cc = _place()
        theirs = pl.ds((1 - cc) * hl, hl)
        return [pltpu.make_async_remote_copy(src_ref=ins[ti].at[:, theirs], dst_ref=outs[ti], send_sem=send_sems.at[ti], recv_sem=recv_sems.at[ti],
                                             device_id=(x, y, 1 - cc), device_id_type=MESH) for ti in range(nt)]

    def start(*parts):
        for cp in copies(*parts):
            cp.start()

    def finish(*parts):
        for cp in copies(*parts):
            cp.wait()

    return Rider(gs, [jax.ShapeDtypeStruct((NCHIP, hl) + g.shape[2:], g.dtype) for g in gs], nt, start, finish)


def scatter_packs(cs, name):
    return scatter_rider(cs).standalone(name)


def scatter_rider(cs):
    nt = len(cs)

    def copies(ins, outs, send_sems, recv_sems):
        x, y, cc = _place()
        cps = []
        for ti in range(nt):
            for j, (px, py) in enumerate(_other_chips(x, y)):
                cps.append(pltpu.make_async_remote_copy(src_ref=ins[ti].at[2 * px + py], dst_ref=outs[ti].at[j], send_sem=send_sems.at[3 * ti + j],
                                                        recv_sem=recv_sems.at[3 * ti + j], device_id=(px, py, cc), device_id_type=MESH))
        return cps

    def start(*parts):
        for cp in copies(*parts):
            cp.start()

    def finish(*parts):
        for cp in copies(*parts):
            cp.wait()

    return Rider(cs, [jax.ShapeDtypeStruct((3,) + c_.shape[1:], c_.dtype) for c_ in cs], 3 * nt, start, finish)


def join_packs(fulls, name):
    nt = len(fulls)
    hl = fulls[0].shape[0] // 2

    def body(*refs):
        ins, outs, (send_sems, recv_sems) = refs[:nt], refs[nt:2 * nt], refs[2 * nt:]
        x, y, cc = _place()
        mine = pl.ds(cc * hl, hl)
        cps = [pltpu.make_async_remote_copy(src_ref=ins[ti].at[mine], dst_ref=outs[ti].at[mine], send_sem=send_sems.at[ti], recv_sem=recv_sems.at[ti],
                                            device_id=(x, y, 1 - cc), device_id_type=MESH) for ti in range(nt)]
        for cp in cps:
            cp.start()
        for cp in cps:
            cp.wait()

    return _comm_call(body, name, fulls, [jax.ShapeDtypeStruct(f.shape, f.dtype) for f in fulls], nt, aliases={ti: ti for ti in range(nt)})


def add_sibling(g, recv, name):
    _, nl, r, c = g.shape
    hl = nl // 2
    tm, tc = _tile2(r, c)

    def body(g_ref, r_ref, o_ref):
        o_ref[...] = (g_ref[...].astype(F32) + r_ref[...].astype(F32)).astype(o_ref.dtype)

    blk = (None, None, tm, tc)
    return pl.pallas_call(
        body,
        name=name,
        grid=(NCHIP, hl, r // tm, c // tc),
        in_specs=[pl.BlockSpec(blk, lambda k, l, i, j: (k, lax.axis_index("c") * hl + l, i, j)), pl.BlockSpec(blk, lambda k, l, i, j: (k, l, i, j))],
        out_specs=pl.BlockSpec(blk, lambda k, l, i, j: (k, l, i, j)),
        out_shape=jax.ShapeDtypeStruct(recv.shape, BF),
        compiler_params=_params(("parallel",) * 4),
    )(g, recv)


def add_chips(cs, got, nl, name):
    _, hl, r, c = cs.shape
    tm, tc = _tile2(r, c)

    def body(own_ref, got_ref, o_ref):
        o_ref[...] = own_ref[...].astype(F32) + got_ref[0].astype(F32) + got_ref[1].astype(F32) + got_ref[2].astype(F32)

    return pl.pallas_call(
        body,
        name=name,
        grid=(hl, r // tm, c // tc),
        in_specs=[pl.BlockSpec((None, None, tm, tc), lambda l, i, j: (2 * lax.axis_index("x") + lax.axis_index("y"), l, i, j)),
                  pl.BlockSpec((3, None, tm, tc), lambda l, i, j: (0, l, i, j))],
        out_specs=pl.BlockSpec((None, tm, tc), lambda l, i, j: (lax.axis_index("c") * hl + l, i, j)),
        out_shape=jax.ShapeDtypeStruct((nl, r, c), F32),
        compiler_params=_params(("parallel",) * 3),
    )(cs, got)


def all_gather_xy(shard, name):
    r, c = shard.shape
    hr = r // 2
    assert r % 32 == 0

    def body(x_ref, out_ref, send_sems, recv_sems, local_sem):
        x, y, cc = _place()
        chips = _other_chips(x, y)
        mine = pl.ds(pl.multiple_of(cc * hr, 16), hr)
        theirs = pl.ds(pl.multiple_of((1 - cc) * hr, 16), hr)
        k_me = 2 * x + y

        def copy(k, src, dst, to):
            return pltpu.make_async_remote_copy(src_ref=src, dst_ref=dst, send_sem=send_sems.at[k], recv_sem=recv_sems.at[k],
                                                device_id=to, device_id_type=MESH)

        own = pltpu.make_async_copy(x_ref, out_ref.at[k_me], local_sem)
        own.start()
        first = [copy(j, x_ref.at[mine], out_ref.at[k_me, mine], (*chip, cc)) for j, chip in enumerate(chips)]
        for cp in first:
            cp.start()
        passed = []
        for j, (px, py) in enumerate(chips):
            landed = out_ref.at[2 * px + py, mine]
            copy(j, landed, landed, (px, py, cc)).wait_recv()
            fw = copy(3 + j, landed, landed, (x, y, 1 - cc))
            fw.start()
            passed.append(fw)
        for j, (px, py) in enumerate(chips):
            landed = out_ref.at[2 * px + py, theirs]
            copy(3 + j, landed, landed, (x, y, 1 - cc)).wait_recv()
        for cp in first + passed:
            cp.wait_send()
        own.wait()

    return pl.pallas_call(
        body,
        name=name,
        in_specs=[ANY],
        out_specs=ANY,
        out_shape=jax.ShapeDtypeStruct((NCHIP, r, c), shard.dtype),
        scratch_shapes=[pltpu.SemaphoreType.DMA((6,)), pltpu.SemaphoreType.DMA((6,)), pltpu.SemaphoreType.DMA],
    )(shard)


def all_gather_8(block, name):
    m, c = block.shape

    def body(x_ref, out_ref, send_sems, recv_sems, local_sem):
        x, y, cc = _place()
        me, sibling = (x, y, cc), (x, y, 1 - cc)
        chips = _other_chips(x, y)

        def rows(px, py, pc):
            return out_ref.at[4 * px + 2 * py + pc]

        def copy(k, blk, to, src=None):
            return pltpu.make_async_remote_copy(src_ref=rows(*blk) if src is None else src, dst_ref=rows(*blk), send_sem=send_sems.at[k],
                                                recv_sem=recv_sems.at[k], device_id=to, device_id_type=MESH)

        mine = pltpu.make_async_copy(x_ref, rows(*me), local_sem)
        mine.start()
        first = [copy(0, me, sibling, src=x_ref)]
        first += [copy(1 + j, me, (*chip, cc), src=x_ref) for j, chip in enumerate(chips)]
        for cp in first:
            cp.start()
        passed = [copy(4 + j, (*chip, cc), sibling) for j, chip in enumerate(chips)]
        for j, chip in enumerate(chips):
            copy(1 + j, (*chip, cc), me).wait_recv()
            passed[j].start()
        copy(0, sibling, me).wait_recv()
        for j, chip in enumerate(chips):
            copy(4 + j, (*chip, 1 - cc), me).wait_recv()
        for cp in first + passed:
            cp.wait_send()
        mine.wait()

    return pl.pallas_call(
        body,
        name=name,
        in_specs=[pl.BlockSpec(memory_space=pltpu.VMEM)],
        out_specs=pl.BlockSpec(memory_space=pltpu.VMEM),
        out_shape=jax.ShapeDtypeStruct((8, m, c), block.dtype),
        scratch_shapes=[pltpu.SemaphoreType.DMA((7,)), pltpu.SemaphoreType.DMA((7,)), pltpu.SemaphoreType.DMA],
        compiler_params=pltpu.CompilerParams(vmem_limit_bytes=VMEM_LIMIT),
    )(block)


def add_parts(parts, out_dtype, name, tm=512):
    def fn(i, nt, rows, vecs, prevs, nexts):
        acc = rows[0]
        for r_ in rows[1:]:
            acc = acc + r_
        return [acc], []
    r, c = parts[0].shape
    return ew(fn, list(parts), [], [(c, out_dtype)], tm=_tile(r, tm, 16), name=name)[0]


SMALL_SHARDED = ("sc_conv_w", "m_conv_w")
SMALL_REPL = ("ffn1_norm", "mix_norm", "m_conv_b", "m_dt_bias", "m_A_log", "m_D", "m_norm", "ffn2_norm", "ple_norm", "final_norm")
BIG = ("ffn1_wg", "ffn1_wu", "ffn1_wd", "w_in", "sc_w_out", "m_w_out", "w_o", "ffn2_wg", "ffn2_wu", "ffn2_wd", "ple_w_gate", "ple_w_proj")
TRANSPOSED = ("ffn1_wg", "ffn1_wu", "ffn2_wg", "ffn2_wu", "w_in")
ORDER = ("ffn1_norm", "ffn1_wg", "ffn1_wu", "ffn1_wd", "mix_norm", "w_in", "sc_conv_w", "sc_w_out", "m_conv_w", "m_conv_b", "m_dt_bias",
         "m_A_log", "m_D", "m_norm", "m_w_out", "w_o", "ffn2_norm", "ffn2_wg", "ffn2_wu", "ffn2_wd", "ple_norm", "ple_w_gate", "ple_w_proj",
         "final_norm")


def _pack(arrs, cols, row_mult):
    flat = jnp.concatenate([a.reshape(-1) for a in arrs])
    n = flat.shape[0]
    rows = -(-n // cols)
    rows = -(-rows // row_mult) * row_mult
    return jnp.pad(flat, (0, rows * cols - n)).reshape(rows, cols)


def _unpack(flat2d, shapes):
    flat = flat2d.reshape(-1)
    out, off = [], 0
    for s in shapes:
        n = int(np.prod(s))
        out.append(flat[off:off + n].reshape(s))
        off += n
    return out


def _row_cat(arrs, dtype):
    return jnp.concatenate([a.astype(dtype) for a in arrs], axis=1)


def kernel(x, p, ffn1_norm, ffn1_wg, ffn1_wu, ffn1_wd, mix_norm, w_in, sc_conv_w, sc_w_out, m_conv_w, m_conv_b, m_dt_bias, m_A_log, m_D, m_norm, m_w_out, w_o, ffn2_norm, ffn2_wg, ffn2_wu, ffn2_wd, ple_norm, ple_w_gate, ple_w_proj, final_norm, loss_target, m_ffn1_norm, m_ffn1_wg, m_ffn1_wu, m_ffn1_wd, m_mix_norm, m_w_in, m_sc_conv_w, m_sc_w_out, m_m_conv_w, m_m_conv_b, m_m_dt_bias, m_m_A_log, m_m_D, m_m_norm, m_m_w_out, m_w_o, m_ffn2_norm, m_ffn2_wg, m_ffn2_wu, m_ffn2_wd, m_ple_norm, m_ple_w_gate, m_ple_w_proj, m_final_norm, v_ffn1_norm, v_ffn1_wg, v_ffn1_wu, v_ffn1_wd, v_mix_norm, v_w_in, v_sc_conv_w, v_sc_w_out, v_m_conv_w, v_m_conv_b, v_m_dt_bias, v_m_A_log, v_m_D, v_m_norm, v_m_w_out, v_w_o, v_ffn2_norm, v_ffn2_wg, v_ffn2_wu, v_ffn2_wd, v_ple_norm, v_ple_w_gate, v_ple_w_proj, v_final_norm):
    args = dict(locals())
    wts = {n: args[n] for n in ORDER}
    mom = {n: args["m_" + n] for n in ORDER}
    vel = {n: args["v_" + n] for n in ORDER}

    depth = ffn1_norm.shape[0]
    d = x.shape[-1]
    w = 2 * d
    hh = w // SSM_P
    cw = w + 2 * SSM_G * SSM_N
    d4 = d // NCHIP
    pp = 7 * d + cw + LANE
    my_x, my_y, my_c = _place()
    k_me = 2 * my_x + my_y

    tr = lambda a: jnp.swapaxes(a, 1, 2)
    gu_t = [_row_cat([tr(wg_), tr(wu_)], BF) for wg_, wu_ in ((ffn1_wg, ffn1_wu), (ffn2_wg, ffn2_wu))]
    wd_l = [ffn1_wd.astype(BF), ffn2_wd.astype(BF)]
    w1024_l = _row_cat([m_w_out, sc_w_out, w_o, ple_w_gate], BF)
    p4 = w_in.shape[2]
    p4p = -(-p4 // 32) * 32
    win_l, wpp_l = jnp.pad(tr(w_in).astype(BF), ((0, 0), (0, p4p - p4), (0, 0))), ple_w_proj.astype(BF)
    halves = lambda a: a.reshape(2, a.shape[0] // 2, a.shape[1])
    whole = lambda g: g.reshape(NCHIP, g.shape[2] * 2, g.shape[3])

    def pieces(l):
        return {"small": [halves(w1024_l[l]), halves(wpp_l[l])], "win": [halves(win_l[l])], "gu1": [halves(gu_t[0][l])], "d1": [halves(wd_l[0][l])],
                "gu2": [halves(gu_t[1][l])], "d2": [halves(wd_l[1][l])]}

    small_local = [sc_conv_w, m_conv_w]
    gathered_s = all_gather_xy(_pack(small_local, LANE, 32), "gather_conv_weights")
    per_shard_s = [_unpack(gathered_s[k], [a.shape for a in small_local]) for k in range(NCHIP)]
    sc_conv_full = jnp.concatenate([per_shard_s[k][0] for k in range(NCHIP)], axis=2)
    m_conv_full = jnp.concatenate([per_shard_s[k][1] for k in range(NCHIP)], axis=2)

    pad_h = lambda a: jnp.pad(a, ((0, 0), (0, LANE - hh)))
    dt_bias_p, a_log_p = pad_h(m_dt_bias), pad_h(m_A_log)
    d_exp = jnp.repeat(m_D, SSM_P, axis=1)
    e_mat = (jnp.arange(w)[None, :] // SSM_P == jnp.arange(LANE)[:, None]).astype(F32)
    et_mat = e_mat.T
    o_z, o_xbc, o_dt, o_g = 3 * d, 5 * d, 5 * d + cw, 5 * d + cw + hh

    def layer_weights(got):
        wt = {"w1024": whole(got["small"][0]), "wpp": whole(got["small"][1])}
        wt.update({k: whole(got[k][0]) for k in ("gu1", "d1", "gu2", "d2")})
        gw = whole(got["win"][0])

        def wi(lo, hi):
            parts = [gw[k, max(lo - k * p4, 0):min(hi - k * p4, p4)] for k in range(NCHIP) if lo < (k + 1) * p4 and hi > k * p4]
            return parts[0] if len(parts) == 1 else jnp.concatenate(parts, axis=0)

        wt["sc3"], wt["z"], wt["xbc"], wt["g2"] = wi(0, o_z), wi(o_z, o_xbc), wi(o_xbc, o_dt), wi(o_g, o_g + 2 * d)
        wt["dt"] = jnp.pad(wi(o_dt, o_g), ((0, LANE - hh), (0, 0)))
        wt["in_p"] = jnp.concatenate([wt["g2"], wt["z"], wt["sc3"][d:], wt["xbc"], wt["sc3"][:d], wt["dt"]], axis=0)
        return wt

    first = pieces(0)
    order = ("gu1", "d1", "win", "small", "gu2", "d2")
    flat = gather_rider([a for k in order for a in first[k]]).standalone("gather_weights")
    got, pos = {}, 0
    for k in order:
        got[k] = flat[pos:pos + len(first[k])]
        pos += len(first[k])
    wts_l = [layer_weights(got)]

    h = x[0]
    saved = []
    for i in range(depth):
        s, wt = {}, wts_l[i]
        nxt = pieces(i + 1) if i + 1 < depth else None
        ride = lambda k: gather_rider(nxt[k]) if nxt else None
        got = {}
        s["h0"] = h
        (s["ab1"], s4, s["n1"]), got["small"] = ffn_up(h, ffn1_norm[i:i + 1], wt["gu1"], rider=ride("small"))
        h, got["d1"] = ffn_down(s4, wt["d1"], h, rider=ride("d1"))
        s["h1"] = h
        u = norm_cast(h, mix_norm[i:i + 1])
        s["u"] = u
        s["sc3"] = mm(u, wt["sc3"], tb=True, out_dtype=BF, name="proj_sc")
        s["z"] = mm(u, wt["z"], tb=True, out_dtype=BF, name="proj_z")
        s["xbc_raw"] = mm(u, wt["xbc"], tb=True, out_dtype=BF, name="proj_xbc")
        s["gates"] = mm(u, wt["g2"], tb=True, out_dtype=BF, name="proj_gates")
        s["dt_raw"] = mm(u, wt["dt"], tb=True, name="proj_dt")
        s["ya_in"] = conv_a_fwd(s["sc3"], sc_conv_full[i])
        s["xbc"], s["dt"] = conv_m_fwd(s["xbc_raw"], s["dt_raw"], m_conv_full[i], m_conv_b[i:i + 1], dt_bias_p[i:i + 1])
        (s["yn"], s["y"], s["sprev"]), got["win"] = ssd_fwd(s["xbc"], s["dt"], s["z"], a_log_p[i:i + 1], d_exp[i:i + 1], m_norm[i:i + 1], e_mat,
                                                            rider=ride("win"))
        (h, s["y_a"], s["y_m"], s["merged"]), got["gu2"] = mix_out_fwd(s["ya_in"], s["yn"], s["gates"], h, wt["w1024"], rider=ride("gu2"))
        s["h2"] = h
        (s["ab2"], s4, s["n2"]), got["gu1"] = ffn_up(h, ffn2_norm[i:i + 1], wt["gu2"], rider=ride("gu1"))
        h, got["d2"] = ffn_down(s4, wt["d2"], h, rider=ride("d2"))
        s["h3"] = h
        h = ple_fwd(h, ple_norm[i:i + 1], p[i, 0], wt["w1024"], wt["wpp"])
        saved.append(s)
        if nxt:
            wts_l.append(layer_weights(got))

    dh, loss_lanes, g_final = loss_bwd(h, final_norm[None, :], loss_target[0])
    loss = lax.psum(jnp.sum(loss_lanes), ("x", "y", "c"))

    def finish_reduce(cs, got):
        halves = [add_chips(c_, g_, 2, "grad_add_chips") for c_, g_ in zip(cs, got, strict=True)]
        return [f.reshape(-1, f.shape[2]) for f in join_packs(halves, "grad_join_halves")]

    g_layer, pending, reduced = None, None, [None] * depth
    gs = {n: [None] * depth for n in SMALL_SHARDED + SMALL_REPL if n != "final_norm"}
    for i in reversed(range(depth)):
        s = saved[i]
        wt = wts_l[i]
        dh, gs["ple_norm"][i], n3, dgp, dpe = ple_bwd(dh, s["h3"], ple_norm[i:i + 1], p[i, 0], wt["w1024"], wt["wpp"])
        g_pg = mm(n3, dgp, ta=True, out_dtype=BF, name="g_ple_gate", tm_cap=512, tn_cap=512)
        g_pp = mm(p[i, 0], dpe, ta=True, out_dtype=BF, name="g_ple_proj", tm_cap=512, tn_cap=512)
        g_pp = jnp.transpose(g_pp.reshape(g_pp.shape[0], NCHIP, d4), (1, 0, 2))
        (dn2, s2, dab2, do2), from_sibling = ffn_bwd(dh, s["ab2"], wt["gu2"], wt["d2"], rider=swap_rider(g_layer) if g_layer else None)
        if g_layer:
            pending = [add_sibling(g, r_, "grad_add_sibling") for g, r_ in zip(g_layer, from_sibling, strict=True)]
        g_ffn2 = ffn_wgrads(s["n2"], do2, s2, dab2)
        dh, gs["ffn2_norm"][i] = norm_bwd_add(dh, s["h2"], ffn2_norm[i:i + 1], dn2)
        dproj, dya, dyn, dy_a, dy_m = mix_out_bwd(dh, s["gates"], s["y_a"], s["y_m"], wt["w1024"], pp)
        g_wo = mm(s["merged"], dh, ta=True, out_dtype=BF, name="g_w_o", tm_cap=512, tn_cap=512)
        g_sco = mm(s["ya_in"], dy_a, ta=True, out_dtype=BF, name="g_sc_out", tm_cap=512, tn_cap=512)
        g_mo = mm(s["yn"], dy_m, ta=True, out_dtype=BF, name="g_m_out", tm_cap=512, tn_cap=512)
        g_1024 = jnp.concatenate([g_mo.reshape(NCHIP, 2 * d4, d), g_sco.reshape(NCHIP, d4, d), g_wo.reshape(NCHIP, d4, d),
                                  g_pg.reshape(NCHIP, d4, d)], axis=1)
        (dproj, dxbc, ddt, gs["m_norm"][i], gd, gal), got_a = ssd_bwd(dyn, s["y"], s["z"], s["xbc"], s["dt"], s["sprev"], a_log_p[i:i + 1], d_exp[i:i + 1],
                                                                      m_norm[i:i + 1], e_mat, et_mat, Window(0, 1, pp, dproj),
                                                                      rider=scatter_rider(pending[:1]) if pending else None)
        gs["m_D"][i], gs["m_A_log"][i] = gd[:, :hh], gal[:, :hh]
        dpre, dproj, gdb = conv_m_bwd1(dxbc, s["xbc_raw"], ddt, s["dt_raw"], m_conv_full[i], m_conv_b[i:i + 1], dt_bias_p[i:i + 1],
                                       Window(1, (7 * d + cw) // LANE, pp, dproj))
        gs["m_dt_bias"][i] = gdb[:, :hh]
        dproj, gs["m_conv_w"][i], gs["m_conv_b"][i] = conv_bwd2(dpre, s["xbc_raw"], m_conv_full[i], "conv_m_bwd2", Window(0, 6 * d // cw, pp, dproj))
        dcv, dproj, v = conv_a_bwd1(dya, s["sc3"], sc_conv_full[i], Window(1, (6 * d + cw) // d, pp, dproj))
        dproj, gs["sc_conv_w"][i] = conv_a_bwd2(dcv, v, s["sc3"], sc_conv_full[i], Window(0, 2, pp, dproj))
        if pending:
            du, got_b = mm(dproj, wt["in_p"], name="d_proj_in", rider=scatter_rider(pending[2:3]))
            gwp, got_c = mm(dproj, s["u"], ta=True, out_dtype=BF, name="g_w_in", tm_cap=1152, tn_cap=512, rider=scatter_rider(pending[1:2] + pending[3:]))
            reduced[i + 1] = finish_reduce(pending, [got_a[0], got_c[0], got_b[0], got_c[1]])
        else:
            du = mm(dproj, wt["in_p"], name="d_proj_in")
            gwp = mm(dproj, s["u"], ta=True, out_dtype=BF, name="g_w_in", tm_cap=1152, tn_cap=512)
        gw_rows = jnp.concatenate([gwp[6 * d + cw:7 * d + cw], gwp[4 * d:6 * d], gwp[2 * d:4 * d], gwp[6 * d:6 * d + cw], gwp[7 * d + cw:7 * d + cw + hh],
                                   gwp[:2 * d]], axis=0)
        g_in = jnp.pad(gw_rows.reshape(NCHIP, p4, d), ((0, 0), (0, p4p - p4), (0, 0)))
        dh, gs["mix_norm"][i] = norm_bwd_add(dh, s["h1"], mix_norm[i:i + 1], du)
        (dn1, s1, dab1, do1), _ = ffn_bwd(dh, s["ab1"], wt["gu1"], wt["d1"])
        g_ffn1 = ffn_wgrads(s["n1"], do1, s1, dab1)
        dh, gs["ffn1_norm"][i] = norm_bwd_add(dh, s["h0"], ffn1_norm[i:i + 1], dn1)
        g_layer = [jnp.concatenate([g_ffn1, g_ffn2], axis=1), g_1024, g_in, g_pp]
        g_layer = [g.reshape(NCHIP, 2, g.shape[1] // 2, g.shape[2]) for g in g_layer]
    from_sibling = swap_rider(g_layer).standalone("grad_swap_halves")
    pending = [add_sibling(g, r_, "grad_add_sibling") for g, r_ in zip(g_layer, from_sibling, strict=True)]
    reduced[0] = finish_reduce(pending, scatter_packs(pending, "grad_scatter"))
    grad_x = dh[None]

    f4 = reduced[0][0].shape[0] // 6
    rows_of = lambda j, lo, hi: jnp.stack([reduced[l][j][lo:hi] for l in range(depth)])
    ffn_rows = lambda j: rows_of(0, j * f4, (j + 1) * f4)
    grads = {
        "ffn1_wg": ffn_rows(0), "ffn1_wu": ffn_rows(1), "ffn1_wd": ffn_rows(2), "ffn2_wg": ffn_rows(3), "ffn2_wu": ffn_rows(4), "ffn2_wd": ffn_rows(5),
        "m_w_out": rows_of(1, 0, 2 * d4), "sc_w_out": rows_of(1, 2 * d4, 3 * d4), "w_o": rows_of(1, 3 * d4, 4 * d4), "ple_w_gate": rows_of(1, 4 * d4, 5 * d4),
        "w_in": rows_of(2, 0, p4), "ple_w_proj": rows_of(3, 0, None),
    }

    small_names = list(SMALL_SHARDED + SMALL_REPL)
    small_full = [g_final[0] if n == "final_norm" else jnp.stack(gs[n]) for n in small_names]
    small_pack = _pack(small_full, LANE, HALO)
    all8 = all_gather_8(small_pack, "gather_small_grads")
    small_sum = add_parts([all8[k] for k in range(8)], F32, "add_small_grads", tm=256)
    for n, tot in zip(small_names, _unpack(small_sum, [a.shape for a in small_full]), strict=True):
        if n in SMALL_SHARDED:
            cl = wts[n].shape[2]
            grads[n] = lax.dynamic_slice_in_dim(tot, k_me * cl, cl, axis=2)
        else:
            grads[n] = tot.reshape(wts[n].shape)

    delta, new_m, new_v = {}, {}, {}
    for n in BIG:
        view = tr if n in TRANSPOSED else (lambda a: a)
        shp = grads[n].shape
        two = lambda a: a.reshape(-1, shp[-1])
        dl, nm, nv = adamw(two(view(wts[n])), two(grads[n]), two(view(mom[n])), two(view(vel[n])), "adamw_" + "x".join(map(str, shp[1:])))
        grads[n], delta[n], new_m[n], new_v[n] = view(grads[n]), view(dl.reshape(shp)), view(nm.reshape(shp)), view(nv.reshape(shp))
    for n in small_names:
        shp = wts[n].shape
        two = lambda a: a.reshape(-1, shp[-1])
        dl, nm, nv = adamw(two(wts[n]), two(grads[n]), two(mom[n]), two(vel[n]), "adamw_small_" + "x".join(map(str, shp)))
        delta[n], new_m[n], new_v[n] = dl.reshape(shp), nm.reshape(shp), nv.reshape(shp)

    return (loss, grad_x, *[grads[n] for n in ORDER], *[delta[n] for n in ORDER], *[new_m[n] for n in ORDER], *[new_v[n] for n in ORDER])
```

```python
import jax
import jax.numpy as jnp
import numpy as np
from jax import lax
from jax.experimental import pallas as pl
from jax.experimental.pallas import tpu as pltpu

BF = jnp.bfloat16
F32 = jnp.float32
EPS = 1e-6
LANE = 128
HALO = 8
SSM_P = 64
SSM_N = 128
SSM_G = 4
SSM_L = 128
ADAM_LR, ADAM_B1, ADAM_B2, ADAM_EPS, ADAM_WD, ADAM_STEP = 0.001, 0.9, 0.999, 1e-08, 0.01, 10
VMEM_LIMIT = 56 * 1024 * 1024
TILE_ELEMS = 400_000
NCHIP = 4
FFN_SUB = 256
MESH = pl.DeviceIdType.MESH


def _tile(n, cap, mult=LANE):
    best = None
    t = mult
    while t <= min(n, cap):
        if n % t == 0:
            best = t
        t += mult
    return best if best is not None else n


def _row_tile(r, c, mult=16):
    return _tile(r, max(mult, TILE_ELEMS // c // mult * mult), mult)


def _tile2(r, c, mult=16):
    tm = _row_tile(r, c, mult)
    tc = c if tm * c <= TILE_ELEMS else _tile(c, max(LANE, TILE_ELEMS // tm // LANE * LANE))
    return tm, tc


def _params(sem):
    return pltpu.CompilerParams(dimension_semantics=sem, vmem_limit_bytes=VMEM_LIMIT)


def _sigmoid(x):
    return 1.0 / (1.0 + jnp.exp(-x))


def _dot(a, b, ca=1, cb=0, precision=None):
    return lax.dot_general(a, b, (((ca,), (cb,)), ((), ())), precision=precision, preferred_element_type=F32)


def _rms(x, g):
    r = lax.rsqrt(jnp.mean(x * x, axis=-1, keepdims=True) + EPS)
    return x * r * g


def _rms_bwd(x, g, dy):
    r = lax.rsqrt(jnp.mean(x * x, axis=-1, keepdims=True) + EPS)
    xh = x * r
    dxh = dy * g
    dx = r * (dxh - xh * jnp.mean(dxh * xh, axis=-1, keepdims=True))
    return dx, jnp.sum(dy * xh, axis=0, keepdims=True)


def _accumulate(ref, val, first):
    @pl.when(first)
    def _():
        ref[...] = val

    @pl.when(jnp.logical_not(first))
    def _():
        ref[...] += val


RIDER_MID = 1.0


class Rider:
    def __init__(self, ins, out_shapes, n_sems, start, finish, mid=None):
        self.ins, self.out_shapes, self.n_sems, self.start, self.mid, self.finish = list(ins), list(out_shapes), n_sems, start, mid, finish

    def standalone(self, name):
        ni, no = len(self.ins), len(self.out_shapes)

        def body(*refs):
            parts = (refs[:ni], refs[ni:ni + no], *refs[ni + no:])
            self.start(*parts)
            if self.mid is not None:
                self.mid(*parts)
            self.finish(*parts)

        return _comm_call(body, name, self.ins, self.out_shapes, self.n_sems)


def host_call(body, *, name, grid, in_specs, out_specs, out_shape, scratch_shapes, operands, rider=None, aliases=None):
    n_in, n_out = len(in_specs), len(out_specs)
    aliases = aliases or {}
    if rider is None:
        outs = pl.pallas_call(body, name=name, grid=grid, in_specs=in_specs, out_specs=out_specs, out_shape=out_shape, scratch_shapes=scratch_shapes,
                              input_output_aliases=aliases, compiler_params=_params(("arbitrary",) * len(grid)))(*operands)
        return list(outs), []
    ri, ro = len(rider.ins), len(rider.out_shapes)

    def hosted(*refs):
        ins, r_ins = refs[:n_in], refs[n_in:n_in + ri]
        outs, r_outs = refs[n_in + ri:n_in + ri + n_out], refs[n_in + ri + n_out:n_in + ri + n_out + ro]
        scratch, (send_sems, recv_sems) = refs[n_in + ri + n_out + ro:-2], refs[-2:]
        step, total = 0, 1
        for ax, n in enumerate(grid):
            step = step * n + pl.program_id(ax)
            total *= n
        parts = (r_ins, r_outs, send_sems, recv_sems)

        @pl.when(step == 0)
        def _():
            rider.start(*parts)

        if rider.mid is not None:
            @pl.when(step == min(total - 1, int(total * RIDER_MID)))
            def _():
                rider.mid(*parts)

        body(*ins, *outs, *scratch)

        @pl.when(step == total - 1)
        def _():
            rider.finish(*parts)

    outs = pl.pallas_call(
        hosted,
        name=name,
        grid=grid,
        in_specs=list(in_specs) + [ANY] * ri,
        out_specs=list(out_specs) + [ANY] * ro,
        out_shape=list(out_shape) + rider.out_shapes,
        scratch_shapes=list(scratch_shapes) + [pltpu.SemaphoreType.DMA((rider.n_sems,)), pltpu.SemaphoreType.DMA((rider.n_sems,))],
        input_output_aliases=aliases,
        compiler_params=_params(("arbitrary",) * len(grid)),
    )(*operands, *rider.ins)
    return list(outs[:n_out]), list(outs[n_out:])


def mmx(name, a, b, *, grid, a_spec, b_spec, o_spec, o_shape, o_dtype, ca, cb, acc_shape=None, rider=None):
    nk = grid[-1] if acc_shape is not None else 1

    def body(a_ref, b_ref, o_ref, *acc):
        p = _dot(a_ref[...].astype(BF), b_ref[...].astype(BF), ca, cb)
        if nk == 1:
            o_ref[...] = p.astype(o_ref.dtype)
        else:
            kk = pl.program_id(len(grid) - 1)
            _accumulate(acc[0], p, kk == 0)

            @pl.when(kk == nk - 1)
            def _():
                o_ref[...] = acc[0][...].astype(o_ref.dtype)

    if rider is not None:
        (out,), r_outs = host_call(body, name=name, grid=grid, in_specs=[a_spec, b_spec], out_specs=[o_spec], out_shape=[jax.ShapeDtypeStruct(o_shape, o_dtype)],
                                   scratch_shapes=[pltpu.VMEM(acc_shape, F32)] if nk > 1 else [], operands=(a, b), rider=rider)
        return out, r_outs
    sem = ("parallel",) * (len(grid) - 1) + ("arbitrary" if nk > 1 else "parallel",)
    return pl.pallas_call(
        body,
        name=name,
        grid=grid,
        in_specs=[a_spec, b_spec],
        out_specs=o_spec,
        out_shape=jax.ShapeDtypeStruct(o_shape, o_dtype),
        scratch_shapes=[pltpu.VMEM(acc_shape, F32)] if nk > 1 else [],
        compiler_params=_params(sem),
    )(a, b)


def mm(a, b, *, ta=False, tb=False, out_dtype=F32, name, tm_cap=1024, tn_cap=1024, tk_cap=4096, rider=None):
    m, k = (a.shape[1], a.shape[0]) if ta else a.shape
    n = b.shape[0] if tb else b.shape[1]
    assert (b.shape[1] if tb else b.shape[0]) == k
    tm, tn, tk = _tile(m, tm_cap), _tile(n, tn_cap), _tile(k, tk_cap)
    nk = k // tk
    a_spec = pl.BlockSpec((tk, tm), lambda i, j, kk: (kk, i)) if ta else pl.BlockSpec((tm, tk), lambda i, j, kk: (i, kk))
    b_spec = pl.BlockSpec((tn, tk), lambda i, j, kk: (j, kk)) if tb else pl.BlockSpec((tk, tn), lambda i, j, kk: (kk, j))
    return mmx(name, a, b, grid=(m // tm, n // tn, nk), a_spec=a_spec, b_spec=b_spec, o_spec=pl.BlockSpec((tm, tn), lambda i, j, kk: (i, j)),
               o_shape=(m, n), o_dtype=out_dtype, ca=0 if ta else 1, cb=1 if tb else 0, acc_shape=(tm, tn) if nk > 1 else None, rider=rider)


class Window:
    def __init__(self, out, block, cols, buf=None):
        self.out, self.block, self.cols, self.buf = out, block, cols, buf


def ew(fn, rows, vecs, out_rows, out_red=(), *, tm, name, prev_halo=(), next_halo=(), window=None):
    t = rows[0].shape[0]
    tm = min(tm, t)
    nt = t // tm
    assert t % tm == 0 and (tm % HALO == 0 or (tm == t and not prev_halo and not next_halo))
    nr, nv, npv, nnx, nor = len(rows), len(vecs), len(prev_halo), len(next_halo), len(out_rows)
    hb = tm // HALO
    n_in = nr + nv + npv + nnx
    passed = window is not None and window.buf is not None

    def body(*refs):
        i = pl.program_id(0)
        ins = [r[...].astype(F32) for r in refs[:n_in]]
        outs = refs[n_in + passed:]
        o_rows, o_red = fn(i, nt, ins[:nr], ins[nr:nr + nv], ins[nr + nv:nr + nv + npv], ins[nr + nv + npv:])
        for ref, val in zip(outs[:nor], o_rows, strict=True):
            ref[...] = val.astype(ref.dtype)
        for ref, val in zip(outs[nor:], o_red, strict=True):
            _accumulate(ref, val, i == 0)

    in_specs = [pl.BlockSpec((tm, r.shape[1]), lambda i: (i, 0)) for r in rows]
    in_specs += [pl.BlockSpec(v.shape, lambda i: (0, 0)) for v in vecs]
    in_specs += [pl.BlockSpec((HALO, rows[k].shape[1]), lambda i: (jnp.maximum(i * hb - 1, 0), 0)) for k in prev_halo]
    in_specs += [pl.BlockSpec((HALO, rows[k].shape[1]), lambda i: (jnp.minimum((i + 1) * hb, t // HALO - 1), 0)) for k in next_halo]
    out_specs = [pl.BlockSpec((tm, c), lambda i: (i, 0)) for c, _ in out_rows]
    out_specs += [pl.BlockSpec(s, lambda i: (0, 0)) for s in out_red]
    out_shape = [jax.ShapeDtypeStruct((t, c), d) for c, d in out_rows] + [jax.ShapeDtypeStruct(s, F32) for s in out_red]
    operands = [*rows, *vecs, *[rows[k] for k in prev_halo], *[rows[k] for k in next_halo]]
    aliases = {}
    if window is not None:
        c, dt_ = out_rows[window.out]
        out_specs[window.out] = pl.BlockSpec((tm, c), lambda i: (i, window.block))
        out_shape[window.out] = jax.ShapeDtypeStruct((t, window.cols), dt_)
        if passed:
            in_specs.append(ANY)
            operands.append(window.buf)
            aliases = {n_in: window.out}
    return pl.pallas_call(
        body,
        name=name,
        grid=(nt,),
        in_specs=in_specs,
        out_specs=out_specs,
        out_shape=out_shape,
        input_output_aliases=aliases,
        compiler_params=_params(("arbitrary",) if out_red else ("parallel",)),
    )(*operands)


def _shift_down(x, prev, j):
    if j == 0:
        return x
    r = pltpu.roll(x, j, 0)
    rh = pltpu.roll(prev, j, 0)
    row = lax.broadcasted_iota(jnp.int32, (HALO, x.shape[1]), 0)
    head = jnp.where(row < j, rh, r[:HALO])
    return jnp.concatenate([head, r[HALO:]], axis=0)


def _shift_up(x, nxt, j):
    if j == 0:
        return x
    n = x.shape[0]
    r = pltpu.roll(x, n - j, 0)
    rh = pltpu.roll(nxt, HALO - j, 0)
    row = lax.broadcasted_iota(jnp.int32, (HALO, x.shape[1]), 0)
    tail = jnp.where(row >= HALO - j, rh, r[n - HALO:])
    return jnp.concatenate([r[: n - HALO], tail], axis=0)


def _conv_fwd(x, prev, w):
    kk = w.shape[0]
    acc = None
    for k in range(kk):
        term = w[k:k + 1, :] * _shift_down(x, prev, kk - 1 - k)
        acc = term if acc is None else acc + term
    return acc


def ffn_up(h, g, wf, rider=None):
    t, d = h.shape
    f4 = wf.shape[1] // 2
    tm = _tile(t, 1024)
    sub = _tile(tm, FFN_SUB, 16)

    def body(h_ref, g_ref, wg_ref, wu_ref, ab_ref, s_ref, n_ref):
        @pl.when(pl.program_id(1) == 0)
        def _():
            n_ref[...] = _rms(h_ref[...], g_ref[...]).astype(BF)

        for r in range(tm // sub):
            rows = slice(r * sub, (r + 1) * sub)
            n = n_ref[rows, :]
            a = _dot(n, wg_ref[...], 1, 1)
            b = _dot(n, wu_ref[...], 1, 1)
            ab_ref[0, rows, :] = a.astype(BF)
            ab_ref[1, rows, :] = b.astype(BF)
            s_ref[rows, :] = (a * _sigmoid(a) * b).astype(BF)

    wspec = lambda ib: pl.BlockSpec((None, f4, d), lambda i, j: (j, ib, 0))
    return host_call(
        body,
        name="ffn_up",
        grid=(t // tm, NCHIP),
        in_specs=[pl.BlockSpec((tm, d), lambda i, j: (i, 0)), pl.BlockSpec((1, d), lambda i, j: (0, 0)), wspec(0), wspec(1)],
        out_specs=[pl.BlockSpec((2, None, tm, f4), lambda i, j: (0, j, i, 0)), pl.BlockSpec((None, tm, f4), lambda i, j: (j, i, 0)),
                   pl.BlockSpec((tm, d), lambda i, j: (i, 0))],
        out_shape=[jax.ShapeDtypeStruct((2, NCHIP, t, f4), BF), jax.ShapeDtypeStruct((NCHIP, t, f4), BF), jax.ShapeDtypeStruct((t, d), BF)],
        scratch_shapes=[],
        operands=(h, g, wf, wf),
        rider=rider,
    )


def ffn_down(s4, wf, h, rider=None):
    t, d = h.shape
    f4 = s4.shape[2]
    tm = _tile(t, 512)

    def body(s_ref, w_ref, h_ref, o_ref):
        acc = _dot(s_ref[0], w_ref[0])
        for k in range(1, NCHIP):
            acc = acc + _dot(s_ref[k], w_ref[k])
        o_ref[...] = h_ref[...] + 0.5 * acc

    (out,), r_outs = host_call(
        body,
        name="ffn_down",
        grid=(t // tm,),
        in_specs=[pl.BlockSpec((NCHIP, tm, f4), lambda i: (0, i, 0)), pl.BlockSpec((NCHIP, f4, d), lambda i: (0, 0, 0)), pl.BlockSpec((tm, d), lambda i: (i, 0))],
        out_specs=[pl.BlockSpec((tm, d), lambda i: (i, 0))],
        out_shape=[jax.ShapeDtypeStruct((t, d), F32)],
        scratch_shapes=[],
        operands=(s4, wf, h),
        rider=rider,
    )
    return out, r_outs


def ffn_bwd(dho, ab, wf, wd, rider=None):
    t, d = dho.shape
    f4 = wd.shape[1]
    tm = _tile(t, 1024)
    sub = _tile(tm, FFN_SUB, 16)

    def body(dho_ref, ab_ref, wg_ref, wu_ref, wd_ref, dn_ref, s_ref, dab_ref, do_sc):
        j = pl.program_id(1)

        @pl.when(j == 0)
        def _():
            do_sc[...] = (0.5 * dho_ref[...]).astype(BF)
            dn_ref[...] = jnp.zeros_like(dn_ref)

        for r in range(tm // sub):
            rows = slice(r * sub, (r + 1) * sub)
            ds = _dot(do_sc[rows, :], wd_ref[...], 1, 1)
            av, bv = ab_ref[0, rows, :].astype(F32), ab_ref[1, rows, :].astype(F32)
            sig = _sigmoid(av)
            sl = av * sig
            s_ref[rows, :] = (sl * bv).astype(BF)
            da = (ds * bv * (sig * (1.0 + av * (1.0 - sig)))).astype(BF)
            db = (ds * sl).astype(BF)
            dab_ref[0, rows, :] = da
            dab_ref[1, rows, :] = db
            dn_ref[rows, :] += _dot(da, wg_ref[...]) + _dot(db, wu_ref[...])

    row = lambda c: pl.BlockSpec((tm, c), lambda i, j: (i, 0))
    wspec = lambda ib: pl.BlockSpec((None, f4, d), lambda i, j: (j, ib, 0))
    ab_spec = pl.BlockSpec((2, None, tm, f4), lambda i, j: (0, j, i, 0))
    return host_call(
        body,
        name="ffn_bwd",
        grid=(t // tm, NCHIP),
        in_specs=[row(d), ab_spec, wspec(0), wspec(1), wspec(0)],
        out_specs=[row(d), pl.BlockSpec((None, tm, f4), lambda i, j: (j, i, 0)), ab_spec, row(d)],
        out_shape=[jax.ShapeDtypeStruct((t, d), F32), jax.ShapeDtypeStruct((NCHIP, t, f4), BF), jax.ShapeDtypeStruct((2, NCHIP, t, f4), BF),
                   jax.ShapeDtypeStruct((t, d), BF)],
        scratch_shapes=[],
        operands=(dho, ab, wf, wf, wd),
        rider=rider,
    )


def ffn_wgrads(n, do, s4, dab):
    t, d = n.shape
    f4 = s4.shape[2]
    g_in = mmx("g_ffn_in", dab, n, grid=(2, NCHIP, 1), a_spec=pl.BlockSpec((None, None, t, f4), lambda wh, k, j: (wh, k, 0, 0)),
               b_spec=pl.BlockSpec((t, d), lambda wh, k, j: (0, 0)), o_spec=pl.BlockSpec((None, None, f4, d), lambda wh, k, j: (k, wh, 0, 0)),
               o_shape=(NCHIP, 2, f4, d), o_dtype=BF, ca=0, cb=0)
    g_out = mmx("g_ffn_out", s4, do, grid=(NCHIP, 1), a_spec=pl.BlockSpec((None, t, f4), lambda k, j: (k, 0, 0)),
                b_spec=pl.BlockSpec((t, d), lambda k, j: (0, 0)), o_spec=pl.BlockSpec((None, f4, d), lambda k, j: (k, 0, 0)),
                o_shape=(NCHIP, f4, d), o_dtype=BF, ca=0, cb=0)
    return jnp.concatenate([g_in.reshape(NCHIP, 2 * f4, d), g_out], axis=1)


def norm_cast(h, g):
    def fn(i, nt, rows, vecs, prevs, nexts):
        return [_rms(rows[0], vecs[0])], []
    return ew(fn, [h], [g], [(h.shape[1], BF)], tm=512, name="norm_cast")[0]


def _zero_if(cond, x):
    return jnp.where(cond, jnp.zeros_like(x), x)


def conv_a_fwd(sc3, w_sc):
    d = sc3.shape[1] // 3

    def fn(i, nt, rows, vecs, prevs, nexts):
        x, pv = rows[0], _zero_if(i == 0, prevs[0])
        v = x[:, d:2 * d] * x[:, 2 * d:]
        vp = pv[:, d:2 * d] * pv[:, 2 * d:]
        return [x[:, :d] * _conv_fwd(v, vp, vecs[0])], []

    return ew(fn, [sc3], [w_sc], [(d, BF)], tm=256, name="conv_a_fwd", prev_halo=(0,))[0]


def _softplus(x):
    e = jnp.exp(-jnp.abs(x))
    return jnp.maximum(x, 0.0) + jnp.where(e < 1e-4, e - 0.5 * e * e, jnp.log(1.0 + e))


def conv_m_fwd(xbc_raw, dt_raw, w_mc, b_mc, dt_bias):
    def fn(i, nt, rows, vecs, prevs, nexts):
        pre = _conv_fwd(rows[0], _zero_if(i == 0, prevs[0]), vecs[0]) + vecs[1]
        return [pre * _sigmoid(pre), _softplus(rows[1] + vecs[2])], []

    return ew(fn, [xbc_raw, dt_raw], [w_mc, b_mc, dt_bias], [(xbc_raw.shape[1], F32), (LANE, F32)], tm=256, name="conv_m_fwd",
              prev_halo=(0,))


def conv_m_bwd1(dxbc, xbc_raw, ddt, dt_raw, w_mc, b_mc, dt_bias, window):
    def fn(i, nt, rows, vecs, prevs, nexts):
        pre = _conv_fwd(rows[1], _zero_if(i == 0, prevs[0]), vecs[0]) + vecs[1]
        sig = _sigmoid(pre)
        dpre = rows[0] * (sig * (1.0 + pre * (1.0 - sig)))
        ddr = rows[2] * _sigmoid(rows[3] + vecs[2])
        return [dpre, ddr], [jnp.sum(ddr, axis=0, keepdims=True)]

    return ew(fn, [dxbc, xbc_raw, ddt, dt_raw], [w_mc, b_mc, dt_bias], [(dxbc.shape[1], F32), (LANE, BF)], [(1, LANE)], tm=256,
              name="conv_m_bwd1", prev_halo=(1,), window=window)


def conv_bwd2(dpre, x, w, name, window):
    kk = w.shape[0]

    def fn(i, nt, rows, vecs, prevs, nexts):
        dp, xv = rows[0], rows[1]
        nx = _zero_if(i == nt - 1, nexts[0])
        dx = None
        dws = []
        for k in range(kk):
            up = _shift_up(dp, nx, kk - 1 - k)
            term = vecs[0][k:k + 1, :] * up
            dx = term if dx is None else dx + term
            dws.append(jnp.sum(up * xv, axis=0, keepdims=True))
        return [dx], [jnp.concatenate(dws, axis=0), jnp.sum(dp, axis=0, keepdims=True)]

    c = x.shape[1]
    return ew(fn, [dpre, x], [w], [(c, BF)], [(kk, c), (1, c)], tm=256, name=name, next_halo=(0,), window=window)


def conv_a_bwd1(dya, sc3, w_sc, window):
    d = sc3.shape[1] // 3

    def fn(i, nt, rows, vecs, prevs, nexts):
        x, pv = rows[1], _zero_if(i == 0, prevs[0])
        v = x[:, d:2 * d] * x[:, 2 * d:]
        vp = pv[:, d:2 * d] * pv[:, 2 * d:]
        return [rows[0] * x[:, :d], rows[0] * _conv_fwd(v, vp, vecs[0]), v], []

    return ew(fn, [dya, sc3], [w_sc], [(d, F32), (d, BF), (d, F32)], tm=256, name="conv_a_bwd1", prev_halo=(1,), window=window)


def conv_a_bwd2(dcv, v, sc3, w_sc, window):
    d = v.shape[1]
    kk = w_sc.shape[0]

    def fn(i, nt, rows, vecs, prevs, nexts):
        dp, vv, x = rows
        nx = _zero_if(i == nt - 1, nexts[0])
        dv = None
        dws = []
        for k in range(kk):
            up = _shift_up(dp, nx, kk - 1 - k)
            term = vecs[0][k:k + 1, :] * up
            dv = term if dv is None else dv + term
            dws.append(jnp.sum(up * vv, axis=0, keepdims=True))
        return [jnp.concatenate([dv * x[:, 2 * d:], dv * x[:, d:2 * d]], axis=1)], [jnp.concatenate(dws, axis=0)]

    return ew(fn, [dcv, v, sc3], [w_sc], [(2 * d, BF)], [(kk, d)], tm=256, name="conv_a_bwd2", next_halo=(0,), window=window)


def _xdot(a, b, passes, split_lhs, ca=1, cb=0):
    parts, r = [], (a if split_lhs else b)
    for _ in range(passes):
        piece = r.astype(BF)
        parts.append(piece)
        r = r - piece.astype(F32)
    other = (b if split_lhs else a).astype(BF)
    acc = None
    for piece in parts:
        term = _dot(piece, other, ca, cb) if split_lhs else _dot(other, piece, ca, cb)
        acc = term if acc is None else acc + term
    return acc


def _ssd_common(xbc_ref, dt_ref, alog_ref, e_ref, w):
    ll = SSM_L
    xs = xbc_ref[:, 0:w]
    dtv = dt_ref[...]
    a_row = -jnp.exp(alog_ref[...])
    a = dtv * a_row
    row = lax.broadcasted_iota(jnp.int32, (ll, ll), 0)
    col = lax.broadcasted_iota(jnp.int32, (ll, ll), 1)
    tril = (row >= col).astype(F32)
    triu = (row <= col).astype(F32)
    acl = _xdot(tril, a, 3, False)
    acl_t = _xdot(a, triu, 3, True, 0, 0)
    e = e_ref[...]
    aclx = _xdot(acl, e, 3, True)
    dtx = _xdot(dtv, e, 2, True)
    last = aclx[ll - 1:ll, :]
    e_in = jnp.exp(aclx)
    e_end = jnp.exp(last - aclx)
    e_tot = jnp.exp(last)
    x = xs * dtx
    return dict(xs=xs, dtv=dtv, a_row=a_row, a=a, row=row, col=col, triu=triu, acl=acl, acl_t=acl_t, dtx=dtx, e_in=e_in, e_end=e_end,
                e_tot=e_tot, x=x)


def _decay(q, hh):
    diff = q["acl"][:, hh:hh + 1] - q["acl_t"][hh:hh + 1, :]
    return jnp.exp(jnp.where(q["row"] >= q["col"], diff, -jnp.inf))


def ssd_fwd(xbc, dt, z, a_log, d_exp, m_norm, e_mat, rider=None):
    t = xbc.shape[0]
    w = z.shape[1]
    gn = SSM_G * SSM_N
    gw = w // SSM_G
    ll, nn = SSM_L, SSM_N
    nc = t // ll
    cw = xbc.shape[1]

    def body(xbc_ref, dt_ref, z_ref, alog_ref, dexp_ref, mn_ref, e_ref, yn_ref, y_ref, sp_ref, s_sc):
        c = pl.program_id(0)

        @pl.when(c == 0)
        def _():
            s_sc[...] = jnp.zeros_like(s_sc)

        q = _ssd_common(xbc_ref, dt_ref, alog_ref, e_ref, w)
        xb = q["x"].astype(BF)
        xsb = (q["x"] * q["e_end"]).astype(BF)
        sp = s_sc[...]
        sp_ref[0] = sp
        spb = sp.astype(BF)
        lane = lax.broadcasted_iota(jnp.int32, (ll, LANE), 1)
        for g in range(SSM_G):
            lo = g * gw
            bg = xbc_ref[:, w + g * nn:w + (g + 1) * nn].astype(BF)
            cg = xbc_ref[:, w + gn + g * nn:w + gn + (g + 1) * nn].astype(BF)
            yoff = _dot(cg, spb[:, lo:lo + gw]) * q["e_in"][:, lo:lo + gw]
            s_sc[:, lo:lo + gw] = sp[:, lo:lo + gw] * q["e_tot"][:, lo:lo + gw] + _dot(bg, xsb[:, lo:lo + gw], 0, 0)
            cb = _dot(cg, bg, 1, 1)
            for pr in range(gw // LANE):
                l0 = lo + pr * LANE
                xp = xb[:, l0:l0 + LANE]
                ys = []
                for hh in (l0 // SSM_P, l0 // SSM_P + 1):
                    wm = (cb * _decay(q, hh)).astype(BF)
                    ys.append(_dot(wm, xp))
                ydiag = jnp.where(lane < SSM_P, ys[0], ys[1])
                y_ref[:, l0:l0 + LANE] = ydiag + yoff[:, pr * LANE:(pr + 1) * LANE] + dexp_ref[:, l0:l0 + LANE] * q["xs"][:, l0:l0 + LANE]
        zv = z_ref[...].astype(F32)
        yz = y_ref[...] * (zv * _sigmoid(zv))
        for g in range(SSM_G):
            lo = g * gw
            yn_ref[:, lo:lo + gw] = _rms(yz[:, lo:lo + gw], mn_ref[:, lo:lo + gw]).astype(BF)

    vec = lambda s: pl.BlockSpec(s, lambda c: (0, 0))
    return host_call(
        body,
        name="ssd_fwd",
        grid=(nc,),
        in_specs=[
            pl.BlockSpec((ll, cw), lambda c: (c, 0)), pl.BlockSpec((ll, LANE), lambda c: (c, 0)), pl.BlockSpec((ll, w), lambda c: (c, 0)),
            vec((1, LANE)), vec((1, w)), vec((1, w)), vec((LANE, w)),
        ],
        out_specs=[pl.BlockSpec((ll, w), lambda c: (c, 0)), pl.BlockSpec((ll, w), lambda c: (c, 0)), pl.BlockSpec((1, nn, w), lambda c: (c, 0, 0))],
        out_shape=[jax.ShapeDtypeStruct((t, w), BF), jax.ShapeDtypeStruct((t, w), F32), jax.ShapeDtypeStruct((nc, nn, w), F32)],
        scratch_shapes=[pltpu.VMEM((nn, w), F32)],
        operands=(xbc, dt, z, a_log, d_exp, m_norm, e_mat),
        rider=rider,
    )


def ssd_bwd(dyn, y, z, xbc, dt, sprev, a_log, d_exp, m_norm, e_mat, et_mat, window, rider=None):
    t = xbc.shape[0]
    w = z.shape[1]
    gn = SSM_G * SSM_N
    gw = w // SSM_G
    ll, nn = SSM_L, SSM_N
    nc = t // ll
    cw = xbc.shape[1]

    def body(dyn_ref, y_ref, z_ref, xbc_ref, dt_ref, sp_ref, alog_ref, dexp_ref, mn_ref, e_ref, et_ref,
             dz_ref, dxbc_ref, ddt_ref, dmn_ref, dd_ref, dal_ref, ds_sc, dy_sc, dx_sc):
        step = pl.program_id(0)

        @pl.when(step == 0)
        def _():
            ds_sc[...] = jnp.zeros_like(ds_sc)

        zv, yv = z_ref[...].astype(F32), y_ref[...]
        sg = _sigmoid(zv)
        sz = zv * sg
        yz = yv * sz
        dmn = []
        for g in range(SSM_G):
            lo = g * gw
            dseg, dmn_g = _rms_bwd(yz[:, lo:lo + gw], mn_ref[:, lo:lo + gw], dyn_ref[:, lo:lo + gw])
            dy_sc[:, lo:lo + gw] = dseg
            dmn.append(dmn_g)
        dmn = jnp.concatenate(dmn, axis=1)
        dyz = dy_sc[...]
        dz_ref[...] = (dyz * yv * (sg * (1.0 + zv * (1.0 - sg)))).astype(BF)
        dy = dyz * sz

        q = _ssd_common(xbc_ref, dt_ref, alog_ref, e_ref, w)
        x = q["x"]
        xb = x.astype(BF)
        xsb = (x * q["e_end"]).astype(BF)
        sp = sp_ref[0]
        spb = sp.astype(BF)
        dsn = ds_sc[...]
        dsnb = dsn.astype(BF)
        dyb = dy.astype(BF)
        lane = lax.broadcasted_iota(jnp.int32, (ll, LANE), 1)
        lane1 = lax.broadcasted_iota(jnp.int32, (1, LANE), 1)
        sub1 = lax.broadcasted_iota(jnp.int32, (LANE, 1), 0)
        dacl = jnp.zeros((ll, LANE), F32)
        dacl_t = jnp.zeros((LANE, ll), F32)
        d_ein, d_eend, d_etot = [], [], []
        for g in range(SSM_G):
            lo = g * gw
            sl = slice(lo, lo + gw)
            bg = xbc_ref[:, w + g * nn:w + (g + 1) * nn].astype(BF)
            cg = xbc_ref[:, w + gn + g * nn:w + gn + (g + 1) * nn].astype(BF)
            zg = _dot(cg, spb[:, sl])
            dzz = (dy[:, sl] * q["e_in"][:, sl]).astype(BF)
            d_ein.append(dy[:, sl] * zg)
            dcg = _dot(dzz, spb[:, sl], 1, 1)
            ds_sc[:, sl] = _dot(cg, dzz, 0, 0) + dsn[:, sl] * q["e_tot"][:, sl]
            d_etot.append(jnp.sum(dsn[:, sl] * sp[:, sl], axis=0, keepdims=True))
            dbg = _dot(xsb[:, sl], dsnb[:, sl], 1, 1)
            dxs_g = _dot(bg, dsnb[:, sl])
            d_eend.append(dxs_g * x[:, sl])
            cb = _dot(cg, bg, 1, 1)
            dcb = jnp.zeros((ll, ll), F32)
            for pr in range(gw // LANE):
                l0 = lo + pr * LANE
                xp = xb[:, l0:l0 + LANE]
                dyp = dyb[:, l0:l0 + LANE]
                dxp = []
                for hi, hh in enumerate((l0 // SSM_P, l0 // SSM_P + 1)):
                    lm = _decay(q, hh)
                    wm = (cb * lm).astype(BF)
                    in_head = (lane < SSM_P) if hi == 0 else (lane >= SSM_P)
                    dwm = _dot(jnp.where(in_head, dyp, jnp.zeros_like(dyp)), xp, 1, 1)
                    dxp.append(_dot(wm, dyp, 0, 0))
                    dlm = dwm * lm
                    dcb = dcb + dlm
                    dd = dlm * cb
                    dacl = dacl + jnp.sum(dd, axis=1, keepdims=True) * (lane1 == hh).astype(F32)
                    dacl_t = dacl_t + (sub1 == hh).astype(F32) * jnp.sum(dd, axis=0, keepdims=True)
                dx_sc[:, l0:l0 + LANE] = jnp.where(lane < SSM_P, dxp[0], dxp[1]) + dxs_g[:, pr * LANE:(pr + 1) * LANE] * q["e_end"][:, l0:l0 + LANE]
            dcbb = dcb.astype(BF)
            dxbc_ref[:, w + g * nn:w + (g + 1) * nn] = dbg + _dot(dcbb, cg, 0, 0)
            dxbc_ref[:, w + gn + g * nn:w + gn + (g + 1) * nn] = dcg + _dot(dcbb, bg)
        d_ein = jnp.concatenate(d_ein, axis=1) * q["e_in"]
        d_eend = jnp.concatenate(d_eend, axis=1) * q["e_end"]
        d_etot = jnp.concatenate(d_etot, axis=1) * q["e_tot"]
        et = et_ref[...]
        last_add = jnp.sum(d_eend, axis=0, keepdims=True) + d_etot
        last_add = _xdot(jnp.broadcast_to(last_add, (HALO, w)), et, 2, True)[0:1]
        row1 = lax.broadcasted_iota(jnp.int32, (ll, LANE), 0)
        dacl = dacl + _xdot(d_ein - d_eend, et, 2, True) + jnp.where(row1 == ll - 1, last_add, 0.0)
        da = _xdot(q["triu"], dacl, 2, False) - _xdot(q["triu"], dacl_t, 2, False, 1, 1)
        dxv = dx_sc[...]
        dxbc_ref[:, 0:w] = dexp_ref[...] * dy + dxv * q["dtx"]
        ddt_ref[...] = _xdot(dxv * q["xs"], et, 2, True) + da * q["a_row"]
        dal = jnp.sum(da * q["dtv"], axis=0, keepdims=True) * q["a_row"]
        ddv = jnp.sum(dy * q["xs"], axis=0, keepdims=True)
        ddv = _xdot(jnp.broadcast_to(ddv, (HALO, w)), et, 2, True)[0:1]
        _accumulate(dmn_ref, dmn, step == 0)
        _accumulate(dd_ref, ddv, step == 0)
        _accumulate(dal_ref, dal, step == 0)

    rev = lambda c_: pl.BlockSpec((ll, c_), lambda s: (nc - 1 - s, 0))
    vec = lambda s_: pl.BlockSpec(s_, lambda s: (0, 0))
    n_in = 11

    def body_skipping_buffer(*refs):
        body(*refs[:n_in], *refs[n_in + 1:])

    return host_call(
        body_skipping_buffer,
        name="ssd_bwd",
        grid=(nc,),
        in_specs=[
            rev(w), rev(w), rev(w), rev(cw), rev(LANE), pl.BlockSpec((1, nn, w), lambda s: (nc - 1 - s, 0, 0)),
            vec((1, LANE)), vec((1, w)), vec((1, w)), vec((LANE, w)), vec((w, LANE)), ANY,
        ],
        out_specs=[pl.BlockSpec((ll, w), lambda s: (nc - 1 - s, window.block)), rev(cw), rev(LANE), vec((1, w)), vec((1, LANE)), vec((1, LANE))],
        out_shape=[
            jax.ShapeDtypeStruct((t, window.cols), BF), jax.ShapeDtypeStruct((t, cw), F32), jax.ShapeDtypeStruct((t, LANE), F32),
            jax.ShapeDtypeStruct((1, w), F32), jax.ShapeDtypeStruct((1, LANE), F32), jax.ShapeDtypeStruct((1, LANE), F32),
        ],
        scratch_shapes=[pltpu.VMEM((nn, w), F32), pltpu.VMEM((ll, w), F32), pltpu.VMEM((ll, w), F32)],
        operands=(dyn, y, z, xbc, dt, sprev, a_log, d_exp, m_norm, e_mat, et_mat, window.buf),
        rider=rider,
        aliases={n_in: 0},
    )


def _w1024_spec(d, nblk, iblk):
    r = nblk * (d // NCHIP)
    return pl.BlockSpec((NCHIP, r, d), lambda i: (0, iblk // nblk, 0))


def _whole(ref):
    v = ref[...]
    return v.reshape(v.shape[0] * v.shape[1], v.shape[2])


def mix_out_fwd(ya_in, yn, gates, h, w1024, rider=None):
    t, d = h.shape
    tm = _tile(t, 256)

    def body(ya_ref, yn_ref, g_ref, h_ref, wm_ref, wa_ref, wo_ref, ho_ref, oa_ref, om_ref, mg_ref):
        y_a = _dot(ya_ref[...], _whole(wa_ref))
        y_m = _dot(yn_ref[...], _whole(wm_ref))
        oa_ref[...] = y_a
        om_ref[...] = y_m
        gv = g_ref[...].astype(F32)
        mg = (_sigmoid(gv[:, :d]) * y_a + _sigmoid(gv[:, d:]) * y_m).astype(BF)
        mg_ref[...] = mg
        ho_ref[...] = h_ref[...] + _dot(mg, _whole(wo_ref))

    row = lambda c: pl.BlockSpec((tm, c), lambda i: (i, 0))
    return host_call(
        body,
        name="mix_out_fwd",
        grid=(t // tm,),
        in_specs=[row(d), row(2 * d), row(2 * d), row(d), _w1024_spec(d, 2, 0), _w1024_spec(d, 1, 2), _w1024_spec(d, 1, 3)],
        out_specs=[row(d), row(d), row(d), row(d)],
        out_shape=[jax.ShapeDtypeStruct((t, d), F32), jax.ShapeDtypeStruct((t, d), F32), jax.ShapeDtypeStruct((t, d), F32),
                   jax.ShapeDtypeStruct((t, d), BF)],
        scratch_shapes=[],
        operands=(ya_in, yn, gates, h, w1024, w1024, w1024),
        rider=rider,
    )


def mix_out_bwd(dh, gates, y_a, y_m, w1024, cols):
    t, d = dh.shape
    tm = _tile(t, 256)

    def body(dh_ref, g_ref, ya_ref, ym_ref, wm_ref, wa_ref, wo_ref, dg_ref, dya_ref, dyn_ref, da_ref, dm_ref):
        dmg = _dot(dh_ref[...].astype(BF), _whole(wo_ref), 1, 1)
        gv = g_ref[...].astype(F32)
        sa, sm = _sigmoid(gv[:, :d]), _sigmoid(gv[:, d:])
        dg_ref[:, :d] = (dmg * ya_ref[...] * sa * (1.0 - sa)).astype(BF)
        dg_ref[:, d:] = (dmg * ym_ref[...] * sm * (1.0 - sm)).astype(BF)
        da = (dmg * sa).astype(BF)
        dm = (dmg * sm).astype(BF)
        da_ref[...] = da
        dm_ref[...] = dm
        dya_ref[...] = _dot(da, _whole(wa_ref), 1, 1)
        dyn_ref[...] = _dot(dm, _whole(wm_ref), 1, 1)

    row = lambda c: pl.BlockSpec((tm, c), lambda i: (i, 0))
    return pl.pallas_call(
        body,
        name="mix_out_bwd",
        grid=(t // tm,),
        in_specs=[row(d), row(2 * d), row(d), row(d), _w1024_spec(d, 2, 0), _w1024_spec(d, 1, 2), _w1024_spec(d, 1, 3)],
        out_specs=[row(2 * d), row(d), row(2 * d), row(d), row(d)],
        out_shape=[jax.ShapeDtypeStruct((t, cols), BF), jax.ShapeDtypeStruct((t, d), F32), jax.ShapeDtypeStruct((t, 2 * d), F32),
                   jax.ShapeDtypeStruct((t, d), BF), jax.ShapeDtypeStruct((t, d), BF)],
        compiler_params=_params(("parallel",)),
    )(dh, gates, y_a, y_m, w1024, w1024, w1024)


def norm_bwd_add(dh, h, g, dn):
    def fn(i, nt, rows, vecs, prevs, nexts):
        dx, dg = _rms_bwd(rows[1], vecs[0], rows[2])
        return [rows[0] + dx], [dg]
    d = h.shape[1]
    return ew(fn, [dh, h, dn], [g], [(d, F32)], [(1, d)], tm=512, name="norm_bwd_add")


def _pe(p, wpp_ref):
    pb = p.astype(BF)
    return jnp.concatenate([_dot(pb, wpp_ref[k]) for k in range(NCHIP)], axis=1)


def ple_fwd(h, g, p, w1024, wpp):
    t, d = h.shape
    tm = _tile(t, 512)

    def body(h_ref, g_ref, p_ref, wg_ref, wp_ref, ho_ref):
        hv = h_ref[...]
        gate = _sigmoid(_dot(_rms(hv, g_ref[...]).astype(BF), _whole(wg_ref)))
        ho_ref[...] = hv + gate * _pe(p_ref[...], wp_ref)

    row = lambda c: pl.BlockSpec((tm, c), lambda i: (i, 0))
    wpp_spec = pl.BlockSpec(wpp.shape, lambda i: (0, 0, 0))
    return pl.pallas_call(
        body,
        name="ple_fwd",
        grid=(t // tm,),
        in_specs=[row(d), pl.BlockSpec((1, d), lambda i: (0, 0)), row(p.shape[1]), _w1024_spec(d, 1, 4), wpp_spec],
        out_specs=row(d),
        out_shape=jax.ShapeDtypeStruct((t, d), F32),
        compiler_params=_params(("parallel",)),
    )(h, g, p, w1024, wpp)


def ple_bwd(dho, h, g, p, w1024, wpp):
    t, d = h.shape
    tm = _tile(t, 512)

    def body(dho_ref, h_ref, g_ref, p_ref, wg_ref, wp_ref, dh_ref, dg_ref, n_ref, dgp_ref, dpe_ref):
        hv, dv = h_ref[...], dho_ref[...]
        n = _rms(hv, g_ref[...]).astype(BF)
        n_ref[...] = n
        wg = _whole(wg_ref)
        gate = _sigmoid(_dot(n, wg))
        pe = _pe(p_ref[...], wp_ref)
        dpe_ref[...] = (dv * gate).astype(BF)
        dgp = (dv * pe * gate * (1.0 - gate)).astype(BF)
        dgp_ref[...] = dgp
        dx, dg = _rms_bwd(hv, g_ref[...], _dot(dgp, wg, 1, 1))
        dh_ref[...] = dv + dx
        _accumulate(dg_ref, dg, pl.program_id(0) == 0)

    row = lambda c: pl.BlockSpec((tm, c), lambda i: (i, 0))
    wpp_spec = pl.BlockSpec(wpp.shape, lambda i: (0, 0, 0))
    return pl.pallas_call(
        body,
        name="ple_bwd",
        grid=(t // tm,),
        in_specs=[row(d), row(d), pl.BlockSpec((1, d), lambda i: (0, 0)), row(p.shape[1]), _w1024_spec(d, 1, 4), wpp_spec],
        out_specs=[row(d), pl.BlockSpec((1, d), lambda i: (0, 0)), row(d), row(d), row(d)],
        out_shape=[jax.ShapeDtypeStruct((t, d), F32), jax.ShapeDtypeStruct((1, d), F32), jax.ShapeDtypeStruct((t, d), BF),
                   jax.ShapeDtypeStruct((t, d), BF), jax.ShapeDtypeStruct((t, d), BF)],
        compiler_params=_params(("arbitrary",)),
    )(dho, h, g, p, w1024, wpp)


def loss_bwd(h, g, target):
    d = h.shape[1]

    def fn(i, nt, rows, vecs, prevs, nexts):
        err = _rms(rows[0], vecs[0]) - rows[1]
        dx, dg = _rms_bwd(rows[0], vecs[0], err * (1.0 / d))
        return [dx], [jnp.sum(err * err, axis=0, keepdims=True) * (0.5 / d), dg]

    return ew(fn, [h, target], [g], [(d, F32)], [(1, d), (1, d)], tm=512, name="loss_bwd")


def adamw(w, g, m, v, name):
    c1, c2 = 1.0 / (1.0 - ADAM_B1 ** ADAM_STEP), 1.0 / (1.0 - ADAM_B2 ** ADAM_STEP)

    def fn(i, nt, rows, vecs, prevs, nexts):
        wv, gv, mv, vv = rows
        mn = ADAM_B1 * mv + (1.0 - ADAM_B1) * gv
        vn = ADAM_B2 * vv + (1.0 - ADAM_B2) * (gv * gv)
        delta = -ADAM_LR * ((mn * c1) / (jnp.sqrt(vn * c2) + ADAM_EPS) + ADAM_WD * wv)
        return [delta, mn, vn], []

    c = w.shape[1]
    return ew(fn, [w, g, m, v], [], [(c, F32)] * 3, tm=_row_tile(w.shape[0], c, HALO), name=name)


def _place():
    return lax.axis_index("x"), lax.axis_index("y"), lax.axis_index("c")


def _other_chips(x, y):
    return [(1 - x, y), (x, 1 - y), (1 - x, 1 - y)]


ANY = pl.BlockSpec(memory_space=pl.ANY)


def _comm_call(body, name, ins, out_shapes, n_sems, aliases=None):
    return pl.pallas_call(
        body,
        name=name,
        in_specs=[ANY] * len(ins),
        out_specs=[ANY] * len(out_shapes),
        out_shape=out_shapes,
        scratch_shapes=[pltpu.SemaphoreType.DMA((n_sems,)), pltpu.SemaphoreType.DMA((n_sems,))],
        input_output_aliases=aliases or {},
    )(*ins)


def gather_rider(packs):
    nt = len(packs)

    def pieces(ins, outs, send_sems, recv_sems):
        x, y, cc = _place()
        chips = _other_chips(x, y)
        sibling = (x, y, 1 - cc)
        k_me = 2 * x + y

        def copy(k, src, dst, to):
            return pltpu.make_async_remote_copy(src_ref=src, dst_ref=dst, send_sem=send_sems.at[k], recv_sem=recv_sems.at[k],
                                                device_id=to, device_id_type=MESH)

        sends, forwards, arrivals = [], [], []
        for ti in range(nt):
            for j, (px, py) in enumerate(chips):
                sends.append(copy(7 * ti + j, ins[ti].at[cc], outs[ti].at[k_me, cc], (px, py, cc)))
                landed = outs[ti].at[2 * px + py, cc]
                forwards.append((copy(7 * ti + j, landed, landed, (px, py, cc)), copy(7 * ti + 3 + j, landed, landed, sibling)))
                passed = outs[ti].at[2 * px + py, 1 - cc]
                arrivals.append(copy(7 * ti + 3 + j, passed, passed, sibling))
            sends.append(copy(7 * ti + 6, ins[ti], outs[ti].at[k_me], sibling))
            own = outs[ti].at[k_me]
            arrivals.append(copy(7 * ti + 6, own, own, sibling))
        return sends, forwards, arrivals

    def start(*parts):
        for cp in pieces(*parts)[0]:
            cp.start()

    def mid(*parts):
        for landed, forward in pieces(*parts)[1]:
            landed.wait_recv()
            forward.start()

    def finish(*parts):
        sends, forwards, arrivals = pieces(*parts)
        for cp in arrivals:
            cp.wait_recv()
        for cp in sends + [f for _, f in forwards]:
            cp.wait_send()

    return Rider(packs, [jax.ShapeDtypeStruct((NCHIP,) + p.shape, p.dtype) for p in packs], 7 * nt, start, finish, mid)


def swap_rider(gs):
    nt = len(gs)
    hl = gs[0].shape[1] // 2

    def copies(ins, outs, send_sems, recv_sems):
        x, y, cc = _place()
        theirs = pl.ds((1 - cc) * hl, hl)
        return [pltpu.make_async_remote_copy(src_ref=ins[ti].at[:, theirs], dst_ref=outs[ti], send_sem=send_sems.at[ti], recv_sem=recv_sems.at[ti],
                                             device_id=(x, y, 1 - cc), device_id_type=MESH) for ti in range(nt)]

    def start(*parts):
        for cp in copies(*parts):
            cp.start()

    def finish(*parts):
        for cp in copies(*parts):
            cp.wait()

    return Rider(gs, [jax.ShapeDtypeStruct((NCHIP, hl) + g.shape[2:], g.dtype) for g in gs], nt, start, finish)


def scatter_packs(cs, name):
    return scatter_rider(cs).standalone(name)


def scatter_rider(cs):
    nt = len(cs)

    def copies(ins, outs, send_sems, recv_sems):
        x, y, cc = _place()
        cps = []
        for ti in range(nt):
            for j, (px, py) in enumerate(_other_chips(x, y)):
                cps.append(pltpu.make_async_remote_copy(src_ref=ins[ti].at[2 * px + py], dst_ref=outs[ti].at[j], send_sem=send_sems.at[3 * ti + j],
                                                        recv_sem=recv_sems.at[3 * ti + j], device_id=(px, py, cc), device_id_type=MESH))
        return cps

    def start(*parts):
        for cp in copies(*parts):
            cp.start()

    def finish(*parts):
        for cp in copies(*parts):
            cp.wait()

    return Rider(cs, [jax.ShapeDtypeStruct((3,) + c_.shape[1:], c_.dtype) for c_ in cs], 3 * nt, start, finish)


def join_packs(fulls, name):
    nt = len(fulls)
    hl = fulls[0].shape[0] // 2

    def body(*refs):
        ins, outs, (send_sems, recv_sems) = refs[:nt], refs[nt:2 * nt], refs[2 * nt:]
        x, y, cc = _place()
        mine = pl.ds(cc * hl, hl)
        cps = [pltpu.make_async_remote_copy(src_ref=ins[ti].at[mine], dst_ref=outs[ti].at[mine], send_sem=send_sems.at[ti], recv_sem=recv_sems.at[ti],
                                            device_id=(x, y, 1 - cc), device_id_type=MESH) for ti in range(nt)]
        for cp in cps:
            cp.start()
        for cp in cps:
            cp.wait()

    return _comm_call(body, name, fulls, [jax.ShapeDtypeStruct(f.shape, f.dtype) for f in fulls], nt, aliases={ti: ti for ti in range(nt)})


def add_sibling(g, recv, name):
    _, nl, r, c = g.shape
    hl = nl // 2
    tm, tc = _tile2(r, c)

    def body(g_ref, r_ref, o_ref):
        o_ref[...] = (g_ref[...].astype(F32) + r_ref[...].astype(F32)).astype(o_ref.dtype)

    blk = (None, None, tm, tc)
    return pl.pallas_call(
        body,
        name=name,
        grid=(NCHIP, hl, r // tm, c // tc),
        in_specs=[pl.BlockSpec(blk, lambda k, l, i, j: (k, lax.axis_index("c") * hl + l, i, j)), pl.BlockSpec(blk, lambda k, l, i, j: (k, l, i, j))],
        out_specs=pl.BlockSpec(blk, lambda k, l, i, j: (k, l, i, j)),
        out_shape=jax.ShapeDtypeStruct(recv.shape, BF),
        compiler_params=_params(("parallel",) * 4),
    )(g, recv)


def add_chips(cs, got, nl, name):
    _, hl, r, c = cs.shape
    tm, tc = _tile2(r, c)

    def body(own_ref, got_ref, o_ref):
        o_ref[...] = own_ref[...].astype(F32) + got_ref[0].astype(F32) + got_ref[1].astype(F32) + got_ref[2].astype(F32)

    return pl.pallas_call(
        body,
        name=name,
        grid=(hl, r // tm, c // tc),
        in_specs=[pl.BlockSpec((None, None, tm, tc), lambda l, i, j: (2 * lax.axis_index("x") + lax.axis_index("y"), l, i, j)),
                  pl.BlockSpec((3, None, tm, tc), lambda l, i, j: (0, l, i, j))],
        out_specs=pl.BlockSpec((None, tm, tc), lambda l, i, j: (lax.axis_index("c") * hl + l, i, j)),
        out_shape=jax.ShapeDtypeStruct((nl, r, c), F32),
        compiler_params=_params(("parallel",) * 3),
    )(cs, got)


def all_gather_xy(shard, name):
    r, c = shard.shape
    hr = r // 2
    assert r % 32 == 0

    def body(x_ref, out_ref, send_sems, recv_sems, local_sem):
        x, y, cc = _place()
        chips = _other_chips(x, y)
        mine = pl.ds(pl.multiple_of(cc * hr, 16), hr)
        theirs = pl.ds(pl.multiple_of((1 - cc) * hr, 16), hr)
        k_me = 2 * x + y

        def copy(k, src, dst, to):
            return pltpu.make_async_remote_copy(src_ref=src, dst_ref=dst, send_sem=send_sems.at[k], recv_sem=recv_sems.at[k],
                                                device_id=to, device_id_type=MESH)

        own = pltpu.make_async_copy(x_ref, out_ref.at[k_me], local_sem)
        own.start()
        first = [copy(j, x_ref.at[mine], out_ref.at[k_me, mine], (*chip, cc)) for j, chip in enumerate(chips)]
        for cp in first:
            cp.start()
        passed = []
        for j, (px, py) in enumerate(chips):
            landed = out_ref.at[2 * px + py, mine]
            copy(j, landed, landed, (px, py, cc)).wait_recv()
            fw = copy(3 + j, landed, landed, (x, y, 1 - cc))
            fw.start()
            passed.append(fw)
        for j, (px, py) in enumerate(chips):
            landed = out_ref.at[2 * px + py, theirs]
            copy(3 + j, landed, landed, (x, y, 1 - cc)).wait_recv()
        for cp in first + passed:
            cp.wait_send()
        own.wait()

    return pl.pallas_call(
        body,
        name=name,
        in_specs=[ANY],
        out_specs=ANY,
        out_shape=jax.ShapeDtypeStruct((NCHIP, r, c), shard.dtype),
        scratch_shapes=[pltpu.SemaphoreType.DMA((6,)), pltpu.SemaphoreType.DMA((6,)), pltpu.SemaphoreType.DMA],
    )(shard)


def all_gather_8(block, name):
    m, c = block.shape

    def body(x_ref, out_ref, send_sems, recv_sems, local_sem):
        x, y, cc = _place()
        me, sibling = (x, y, cc), (x, y, 1 - cc)
        chips = _other_chips(x, y)

        def rows(px, py, pc):
            return out_ref.at[4 * px + 2 * py + pc]

        def copy(k, blk, to, src=None):
            return pltpu.make_async_remote_copy(src_ref=rows(*blk) if src is None else src, dst_ref=rows(*blk), send_sem=send_sems.at[k],
                                                recv_sem=recv_sems.at[k], device_id=to, device_id_type=MESH)

        mine = pltpu.make_async_copy(x_ref, rows(*me), local_sem)
        mine.start()
        first = [copy(0, me, sibling, src=x_ref)]
        first += [copy(1 + j, me, (*chip, cc), src=x_ref) for j, chip in enumerate(chips)]
        for cp in first:
            cp.start()
        passed = [copy(4 + j, (*chip, cc), sibling) for j, chip in enumerate(chips)]
        for j, chip in enumerate(chips):
            copy(1 + j, (*chip, cc), me).wait_recv()
            passed[j].start()
        copy(0, sibling, me).wait_recv()
        for j, chip in enumerate(chips):
            copy(4 + j, (*chip, 1 - cc), me).wait_recv()
        for cp in first + passed:
            cp.wait_send()
        mine.wait()

    return pl.pallas_call(
        body,
        name=name,
        in_specs=[pl.BlockSpec(memory_space=pltpu.VMEM)],
        out_specs=pl.BlockSpec(memory_space=pltpu.VMEM),
        out_shape=jax.ShapeDtypeStruct((8, m, c), block.dtype),
        scratch_shapes=[pltpu.SemaphoreType.DMA((7,)), pltpu.SemaphoreType.DMA((7,)), pltpu.SemaphoreType.DMA],
        compiler_params=pltpu.CompilerParams(vmem_limit_bytes=VMEM_LIMIT),
    )(block)


def add_parts(parts, out_dtype, name, tm=512):
    def fn(i, nt, rows, vecs, prevs, nexts):
        acc = rows[0]
        for r_ in rows[1:]:
            acc = acc + r_
        return [acc], []
    r, c = parts[0].shape
    return ew(fn, list(parts), [], [(c, out_dtype)], tm=_tile(r, tm, 16), name=name)[0]


SMALL_SHARDED = ("sc_conv_w", "m_conv_w")
SMALL_REPL = ("ffn1_norm", "mix_norm", "m_conv_b", "m_dt_bias", "m_A_log", "m_D", "m_norm", "ffn2_norm", "ple_norm", "final_norm")
BIG = ("ffn1_wg", "ffn1_wu", "ffn1_wd", "w_in", "sc_w_out", "m_w_out", "w_o", "ffn2_wg", "ffn2_wu", "ffn2_wd", "ple_w_gate", "ple_w_proj")
TRANSPOSED = ("ffn1_wg", "ffn1_wu", "ffn2_wg", "ffn2_wu", "w_in")
ORDER = ("ffn1_norm", "ffn1_wg", "ffn1_wu", "ffn1_wd", "mix_norm", "w_in", "sc_conv_w", "sc_w_out", "m_conv_w", "m_conv_b", "m_dt_bias",
         "m_A_log", "m_D", "m_norm", "m_w_out", "w_o", "ffn2_norm", "ffn2_wg", "ffn2_wu", "ffn2_wd", "ple_norm", "ple_w_gate", "ple_w_proj",
         "final_norm")


def _pack(arrs, cols, row_mult):
    flat = jnp.concatenate([a.reshape(-1) for a in arrs])
    n = flat.shape[0]
    rows = -(-n // cols)
    rows = -(-rows // row_mult) * row_mult
    return jnp.pad(flat, (0, rows * cols - n)).reshape(rows, cols)


def _unpack(flat2d, shapes):
    flat = flat2d.reshape(-1)
    out, off = [], 0
    for s in shapes:
        n = int(np.prod(s))
        out.append(flat[off:off + n].reshape(s))
        off += n
    return out


def _row_cat(arrs, dtype):
    return jnp.concatenate([a.astype(dtype) for a in arrs], axis=1)


def kernel(x, p, ffn1_norm, ffn1_wg, ffn1_wu, ffn1_wd, mix_norm, w_in, sc_conv_w, sc_w_out, m_conv_w, m_conv_b, m_dt_bias, m_A_log, m_D, m_norm, m_w_out, w_o, ffn2_norm, ffn2_wg, ffn2_wu, ffn2_wd, ple_norm, ple_w_gate, ple_w_proj, final_norm, loss_target, m_ffn1_norm, m_ffn1_wg, m_ffn1_wu, m_ffn1_wd, m_mix_norm, m_w_in, m_sc_conv_w, m_sc_w_out, m_m_conv_w, m_m_conv_b, m_m_dt_bias, m_m_A_log, m_m_D, m_m_norm, m_m_w_out, m_w_o, m_ffn2_norm, m_ffn2_wg, m_ffn2_wu, m_ffn2_wd, m_ple_norm, m_ple_w_gate, m_ple_w_proj, m_final_norm, v_ffn1_norm, v_ffn1_wg, v_ffn1_wu, v_ffn1_wd, v_mix_norm, v_w_in, v_sc_conv_w, v_sc_w_out, v_m_conv_w, v_m_conv_b, v_m_dt_bias, v_m_A_log, v_m_D, v_m_norm, v_m_w_out, v_w_o, v_ffn2_norm, v_ffn2_wg, v_ffn2_wu, v_ffn2_wd, v_ple_norm, v_ple_w_gate, v_ple_w_proj, v_final_norm):
    args = dict(locals())
    wts = {n: args[n] for n in ORDER}
    mom = {n: args["m_" + n] for n in ORDER}
    vel = {n: args["v_" + n] for n in ORDER}

    depth = ffn1_norm.shape[0]
    d = x.shape[-1]
    w = 2 * d
    hh = w // SSM_P
    cw = w + 2 * SSM_G * SSM_N
    d4 = d // NCHIP
    pp = 7 * d + cw + LANE
    my_x, my_y, my_c = _place()
    k_me = 2 * my_x + my_y

    tr = lambda a: jnp.swapaxes(a, 1, 2)
    gu_t = [_row_cat([tr(wg_), tr(wu_)], BF) for wg_, wu_ in ((ffn1_wg, ffn1_wu), (ffn2_wg, ffn2_wu))]
    wd_l = [ffn1_wd.astype(BF), ffn2_wd.astype(BF)]
    w1024_l = _row_cat([m_w_out, sc_w_out, w_o, ple_w_gate], BF)
    p4 = w_in.shape[2]
    p4p = -(-p4 // 32) * 32
    win_l, wpp_l = jnp.pad(tr(w_in).astype(BF), ((0, 0), (0, p4p - p4), (0, 0))), ple_w_proj.astype(BF)
    halves = lambda a: a.reshape(2, a.shape[0] // 2, a.shape[1])
    whole = lambda g: g.reshape(NCHIP, g.shape[2] * 2, g.shape[3])

    def pieces(l):
        return {"small": [halves(w1024_l[l]), halves(wpp_l[l])], "win": [halves(win_l[l])], "gu1": [halves(gu_t[0][l])], "d1": [halves(wd_l[0][l])],
                "gu2": [halves(gu_t[1][l])], "d2": [halves(wd_l[1][l])]}

    small_local = [sc_conv_w, m_conv_w]
    gathered_s = all_gather_xy(_pack(small_local, LANE, 32), "gather_conv_weights")
    per_shard_s = [_unpack(gathered_s[k], [a.shape for a in small_local]) for k in range(NCHIP)]
    sc_conv_full = jnp.concatenate([per_shard_s[k][0] for k in range(NCHIP)], axis=2)
    m_conv_full = jnp.concatenate([per_shard_s[k][1] for k in range(NCHIP)], axis=2)

    pad_h = lambda a: jnp.pad(a, ((0, 0), (0, LANE - hh)))
    dt_bias_p, a_log_p = pad_h(m_dt_bias), pad_h(m_A_log)
    d_exp = jnp.repeat(m_D, SSM_P, axis=1)
    e_mat = (jnp.arange(w)[None, :] // SSM_P == jnp.arange(LANE)[:, None]).astype(F32)
    et_mat = e_mat.T
    o_z, o_xbc, o_dt, o_g = 3 * d, 5 * d, 5 * d + cw, 5 * d + cw + hh

    def layer_weights(got):
        wt = {"w1024": whole(got["small"][0]), "wpp": whole(got["small"][1])}
        wt.update({k: whole(got[k][0]) for k in ("gu1", "d1", "gu2", "d2")})
        gw = whole(got["win"][0])

        def wi(lo, hi):
            parts = [gw[k, max(lo - k * p4, 0):min(hi - k * p4, p4)] for k in range(NCHIP) if lo < (k + 1) * p4 and hi > k * p4]
            return parts[0] if len(parts) == 1 else jnp.concatenate(parts, axis=0)

        wt["sc3"], wt["z"], wt["xbc"], wt["g2"] = wi(0, o_z), wi(o_z, o_xbc), wi(o_xbc, o_dt), wi(o_g, o_g + 2 * d)
        wt["dt"] = jnp.pad(wi(o_dt, o_g), ((0, LANE - hh), (0, 0)))
        wt["in_p"] = jnp.concatenate([wt["g2"], wt["z"], wt["sc3"][d:], wt["xbc"], wt["sc3"][:d], wt["dt"]], axis=0)
        return wt

    first = pieces(0)
    order = ("gu1", "d1", "win", "small", "gu2", "d2")
    flat = gather_rider([a for k in order for a in first[k]]).standalone("gather_weights")
    got, pos = {}, 0
    for k in order:
        got[k] = flat[pos:pos + len(first[k])]
        pos += len(first[k])
    wts_l = [layer_weights(got)]

    h = x[0]
    saved = []
    for i in range(depth):
        s, wt = {}, wts_l[i]
        nxt = pieces(i + 1) if i + 1 < depth else None
        ride = lambda k: gather_rider(nxt[k]) if nxt else None
        got = {}
        s["h0"] = h
        (s["ab1"], s4, s["n1"]), got["small"] = ffn_up(h, ffn1_norm[i:i + 1], wt["gu1"], rider=ride("small"))
        h, got["d1"] = ffn_down(s4, wt["d1"], h, rider=ride("d1"))
        s["h1"] = h
        u = norm_cast(h, mix_norm[i:i + 1])
        s["u"] = u
        s["sc3"] = mm(u, wt["sc3"], tb=True, out_dtype=BF, name="proj_sc")
        s["z"] = mm(u, wt["z"], tb=True, out_dtype=BF, name="proj_z")
        s["xbc_raw"] = mm(u, wt["xbc"], tb=True, out_dtype=BF, name="proj_xbc")
        s["gates"] = mm(u, wt["g2"], tb=True, out_dtype=BF, name="proj_gates")
        s["dt_raw"] = mm(u, wt["dt"], tb=True, name="proj_dt")
        s["ya_in"] = conv_a_fwd(s["sc3"], sc_conv_full[i])
        s["xbc"], s["dt"] = conv_m_fwd(s["xbc_raw"], s["dt_raw"], m_conv_full[i], m_conv_b[i:i + 1], dt_bias_p[i:i + 1])
        (s["yn"], s["y"], s["sprev"]), got["win"] = ssd_fwd(s["xbc"], s["dt"], s["z"], a_log_p[i:i + 1], d_exp[i:i + 1], m_norm[i:i + 1], e_mat,
                                                            rider=ride("win"))
        (h, s["y_a"], s["y_m"], s["merged"]), got["gu2"] = mix_out_fwd(s["ya_in"], s["yn"], s["gates"], h, wt["w1024"], rider=ride("gu2"))
        s["h2"] = h
        (s["ab2"], s4, s["n2"]), got["gu1"] = ffn_up(h, ffn2_norm[i:i + 1], wt["gu2"], rider=ride("gu1"))
        h, got["d2"] = ffn_down(s4, wt["d2"], h, rider=ride("d2"))
        s["h3"] = h
        h = ple_fwd(h, ple_norm[i:i + 1], p[i, 0], wt["w1024"], wt["wpp"])
        saved.append(s)
        if nxt:
            wts_l.append(layer_weights(got))

    dh, loss_lanes, g_final = loss_bwd(h, final_norm[None, :], loss_target[0])
    loss = lax.psum(jnp.sum(loss_lanes), ("x", "y", "c"))

    def finish_reduce(cs, got):
        halves = [add_chips(c_, g_, 2, "grad_add_chips") for c_, g_ in zip(cs, got, strict=True)]
        return [f.reshape(-1, f.shape[2]) for f in join_packs(halves, "grad_join_halves")]

    g_layer, pending, reduced = None, None, [None] * depth
    gs = {n: [None] * depth for n in SMALL_SHARDED + SMALL_REPL if n != "final_norm"}
    for i in reversed(range(depth)):
        s = saved[i]
        wt = wts_l[i]
        dh, gs["ple_norm"][i], n3, dgp, dpe = ple_bwd(dh, s["h3"], ple_norm[i:i + 1], p[i, 0], wt["w1024"], wt["wpp"])
        g_pg = mm(n3, dgp, ta=True, out_dtype=BF, name="g_ple_gate", tm_cap=512, tn_cap=512)
        g_pp = mm(p[i, 0], dpe, ta=True, out_dtype=BF, name="g_ple_proj", tm_cap=512, tn_cap=512)
        g_pp = jnp.transpose(g_pp.reshape(g_pp.shape[0], NCHIP, d4), (1, 0, 2))
        (dn2, s2, dab2, do2), from_sibling = ffn_bwd(dh, s["ab2"], wt["gu2"], wt["d2"], rider=swap_rider(g_layer) if g_layer else None)
        if g_layer:
            pending = [add_sibling(g, r_, "grad_add_sibling") for g, r_ in zip(g_layer, from_sibling, strict=True)]
        g_ffn2 = ffn_wgrads(s["n2"], do2, s2, dab2)
        dh, gs["ffn2_norm"][i] = norm_bwd_add(dh, s["h2"], ffn2_norm[i:i + 1], dn2)
        dproj, dya, dyn, dy_a, dy_m = mix_out_bwd(dh, s["gates"], s["y_a"], s["y_m"], wt["w1024"], pp)
        g_wo = mm(s["merged"], dh, ta=True, out_dtype=BF, name="g_w_o", tm_cap=512, tn_cap=512)
        g_sco = mm(s["ya_in"], dy_a, ta=True, out_dtype=BF, name="g_sc_out", tm_cap=512, tn_cap=512)
        g_mo = mm(s["yn"], dy_m, ta=True, out_dtype=BF, name="g_m_out", tm_cap=512, tn_cap=512)
        g_1024 = jnp.concatenate([g_mo.reshape(NCHIP, 2 * d4, d), g_sco.reshape(NCHIP, d4, d), g_wo.reshape(NCHIP, d4, d),
                                  g_pg.reshape(NCHIP, d4, d)], axis=1)
        (dproj, dxbc, ddt, gs["m_norm"][i], gd, gal), got_a = ssd_bwd(dyn, s["y"], s["z"], s["xbc"], s["dt"], s["sprev"], a_log_p[i:i + 1], d_exp[i:i + 1],
                                                                      m_norm[i:i + 1], e_mat, et_mat, Window(0, 1, pp, dproj),
                                                                      rider=scatter_rider(pending[:1]) if pending else None)
        gs["m_D"][i], gs["m_A_log"][i] = gd[:, :hh], gal[:, :hh]
        dpre, dproj, gdb = conv_m_bwd1(dxbc, s["xbc_raw"], ddt, s["dt_raw"], m_conv_full[i], m_conv_b[i:i + 1], dt_bias_p[i:i + 1],
                                       Window(1, (7 * d + cw) // LANE, pp, dproj))
        gs["m_dt_bias"][i] = gdb[:, :hh]
        dproj, gs["m_conv_w"][i], gs["m_conv_b"][i] = conv_bwd2(dpre, s["xbc_raw"], m_conv_full[i], "conv_m_bwd2", Window(0, 6 * d // cw, pp, dproj))
        dcv, dproj, v = conv_a_bwd1(dya, s["sc3"], sc_conv_full[i], Window(1, (6 * d + cw) // d, pp, dproj))
        dproj, gs["sc_conv_w"][i] = conv_a_bwd2(dcv, v, s["sc3"], sc_conv_full[i], Window(0, 2, pp, dproj))
        if pending:
            du, got_b = mm(dproj, wt["in_p"], name="d_proj_in", rider=scatter_rider(pending[2:3]))
            gwp, got_c = mm(dproj, s["u"], ta=True, out_dtype=BF, name="g_w_in", tm_cap=1152, tn_cap=512, rider=scatter_rider(pending[1:2] + pending[3:]))
            reduced[i + 1] = finish_reduce(pending, [got_a[0], got_c[0], got_b[0], got_c[1]])
        else:
            du = mm(dproj, wt["in_p"], name="d_proj_in")
            gwp = mm(dproj, s["u"], ta=True, out_dtype=BF, name="g_w_in", tm_cap=1152, tn_cap=512)
        gw_rows = jnp.concatenate([gwp[6 * d + cw:7 * d + cw], gwp[4 * d:6 * d], gwp[2 * d:4 * d], gwp[6 * d:6 * d + cw], gwp[7 * d + cw:7 * d + cw + hh],
                                   gwp[:2 * d]], axis=0)
        g_in = jnp.pad(gw_rows.reshape(NCHIP, p4, d), ((0, 0), (0, p4p - p4), (0, 0)))
        dh, gs["mix_norm"][i] = norm_bwd_add(dh, s["h1"], mix_norm[i:i + 1], du)
        (dn1, s1, dab1, do1), _ = ffn_bwd(dh, s["ab1"], wt["gu1"], wt["d1"])
        g_ffn1 = ffn_wgrads(s["n1"], do1, s1, dab1)
        dh, gs["ffn1_norm"][i] = norm_bwd_add(dh, s["h0"], ffn1_norm[i:i + 1], dn1)
        g_layer = [jnp.concatenate([g_ffn1, g_ffn2], axis=1), g_1024, g_in, g_pp]
        g_layer = [g.reshape(NCHIP, 2, g.shape[1] // 2, g.shape[2]) for g in g_layer]
    from_sibling = swap_rider(g_layer).standalone("grad_swap_halves")
    pending = [add_sibling(g, r_, "grad_add_sibling") for g, r_ in zip(g_layer, from_sibling, strict=True)]
    reduced[0] = finish_reduce(pending, scatter_packs(pending, "grad_scatter"))
    grad_x = dh[None]

    f4 = reduced[0][0].shape[0] // 6
    rows_of = lambda j, lo, hi: jnp.stack([reduced[l][j][lo:hi] for l in range(depth)])
    ffn_rows = lambda j: rows_of(0, j * f4, (j + 1) * f4)
    grads = {
        "ffn1_wg": ffn_rows(0), "ffn1_wu": ffn_rows(1), "ffn1_wd": ffn_rows(2), "ffn2_wg": ffn_rows(3), "ffn2_wu": ffn_rows(4), "ffn2_wd": ffn_rows(5),
        "m_w_out": rows_of(1, 0, 2 * d4), "sc_w_out": rows_of(1, 2 * d4, 3 * d4), "w_o": rows_of(1, 3 * d4, 4 * d4), "ple_w_gate": rows_of(1, 4 * d4, 5 * d4),
        "w_in": rows_of(2, 0, p4), "ple_w_proj": rows_of(3, 0, None),
    }

    small_names = list(SMALL_SHARDED + SMALL_REPL)
    small_full = [g_final[0] if n == "final_norm" else jnp.stack(gs[n]) for n in small_names]
    small_pack = _pack(small_full, LANE, HALO)
    all8 = all_gather_8(small_pack, "gather_small_grads")
    small_sum = add_parts([all8[k] for k in range(8)], F32, "add_small_grads", tm=256)
    for n, tot in zip(small_names, _unpack(small_sum, [a.shape for a in small_full]), strict=True):
        if n in SMALL_SHARDED:
            cl = wts[n].shape[2]
            grads[n] = lax.dynamic_slice_in_dim(tot, k_me * cl, cl, axis=2)
        else:
            grads[n] = tot.reshape(wts[n].shape)

    delta, new_m, new_v = {}, {}, {}
    for n in BIG:
        view = tr if n in TRANSPOSED else (lambda a: a)
        shp = grads[n].shape
        two = lambda a: a.reshape(-1, shp[-1])
        dl, nm, nv = adamw(two(view(wts[n])), two(grads[n]), two(view(mom[n])), two(view(vel[n])), "adamw_" + "x".join(map(str, shp[1:])))
        grads[n], delta[n], new_m[n], new_v[n] = view(grads[n]), view(dl.reshape(shp)), view(nm.reshape(shp)), view(nv.reshape(shp))
    for n in small_names:
        shp = wts[n].shape
        two = lambda a: a.reshape(-1, shp[-1])
        dl, nm, nv = adamw(two(wts[n]), two(grads[n]), two(mom[n]), two(vel[n]), "adamw_small_" + "x".join(map(str, shp)))
        delta[n], new_m[n], new_v[n] = dl.reshape(shp), nm.reshape(shp), nv.reshape(shp)

    return (loss, grad_x, *[grads[n] for n in ORDER], *[delta[n] for n in ORDER], *[new_m[n] for n in ORDER], *[new_v[n] for n in ORDER])
```

```python
import jax
import jax.numpy as jnp
import numpy as np
from jax import lax
from jax.experimental import pallas as pl
from jax.experimental.pallas import tpu as pltpu

BF = jnp.bfloat16
F32 = jnp.float32
EPS = 1e-6
LANE = 128
HALO = 8
SSM_P = 64
SSM_N = 128
SSM_G = 4
SSM_L = 128
ADAM_LR, ADAM_B1, ADAM_B2, ADAM_EPS, ADAM_WD, ADAM_STEP = 0.001, 0.9, 0.999, 1e-08, 0.01, 10
VMEM_LIMIT = 56 * 1024 * 1024
TILE_ELEMS = 400_000
ADD_TILE_ELEMS = 1_000_000
NCHIP = 4
FFN_SUB = 256
MESH = pl.DeviceIdType.MESH


def _tile(n, cap, mult=LANE):
    best = None
    t = mult
    while t <= min(n, cap):
        if n % t == 0:
            best = t
        t += mult
    return best if best is not None else n


def _row_tile(r, c, mult=16, elems=TILE_ELEMS):
    return _tile(r, max(mult, elems // c // mult * mult), mult)


def _tile2(r, c, mult=16, elems=ADD_TILE_ELEMS):
    tm = _row_tile(r, c, mult, elems)
    tc = c if tm * c <= elems else _tile(c, max(LANE, elems // tm // LANE * LANE))
    return tm, tc


def _params(sem):
    return pltpu.CompilerParams(dimension_semantics=sem, vmem_limit_bytes=VMEM_LIMIT)


def _sigmoid(x):
    return 1.0 / (1.0 + jnp.exp(-x))


def _dot(a, b, ca=1, cb=0, precision=None):
    return lax.dot_general(a, b, (((ca,), (cb,)), ((), ())), precision=precision, preferred_element_type=F32)


def _rms(x, g):
    r = lax.rsqrt(jnp.mean(x * x, axis=-1, keepdims=True) + EPS)
    return x * r * g


def _rms_bwd(x, g, dy):
    r = lax.rsqrt(jnp.mean(x * x, axis=-1, keepdims=True) + EPS)
    xh = x * r
    dxh = dy * g
    dx = r * (dxh - xh * jnp.mean(dxh * xh, axis=-1, keepdims=True))
    return dx, jnp.sum(dy * xh, axis=0, keepdims=True)


def _accumulate(ref, val, first):
    @pl.when(first)
    def _():
        ref[...] = val

    @pl.when(jnp.logical_not(first))
    def _():
        ref[...] += val


RIDER_MID = 1.0


class Rider:
    def __init__(self, ins, out_shapes, n_sems, start, finish, mid=None):
        self.ins, self.out_shapes, self.n_sems, self.start, self.mid, self.finish = list(ins), list(out_shapes), n_sems, start, mid, finish

    def standalone(self, name):
        ni, no = len(self.ins), len(self.out_shapes)

        def body(*refs):
            parts = (refs[:ni], refs[ni:ni + no], *refs[ni + no:])
            self.start(*parts)
            if self.mid is not None:
                self.mid(*parts)
            self.finish(*parts)

        return _comm_call(body, name, self.ins, self.out_shapes, self.n_sems)


def host_call(body, *, name, grid, in_specs, out_specs, out_shape, scratch_shapes, operands, rider=None, aliases=None):
    n_in, n_out = len(in_specs), len(out_specs)
    aliases = aliases or {}
    if rider is None:
        outs = pl.pallas_call(body, name=name, grid=grid, in_specs=in_specs, out_specs=out_specs, out_shape=out_shape, scratch_shapes=scratch_shapes,
                              input_output_aliases=aliases, compiler_params=_params(("arbitrary",) * len(grid)))(*operands)
        return list(outs), []
    ri, ro = len(rider.ins), len(rider.out_shapes)

    def hosted(*refs):
        ins, r_ins = refs[:n_in], refs[n_in:n_in + ri]
        outs, r_outs = refs[n_in + ri:n_in + ri + n_out], refs[n_in + ri + n_out:n_in + ri + n_out + ro]
        scratch, (send_sems, recv_sems) = refs[n_in + ri + n_out + ro:-2], refs[-2:]
        step, total = 0, 1
        for ax, n in enumerate(grid):
            step = step * n + pl.program_id(ax)
            total *= n
        parts = (r_ins, r_outs, send_sems, recv_sems)

        @pl.when(step == 0)
        def _():
            rider.start(*parts)

        if rider.mid is not None:
            @pl.when(step == min(total - 1, int(total * RIDER_MID)))
            def _():
                rider.mid(*parts)

        body(*ins, *outs, *scratch)

        @pl.when(step == total - 1)
        def _():
            rider.finish(*parts)

    outs = pl.pallas_call(
        hosted,
        name=name,
        grid=grid,
        in_specs=list(in_specs) + [ANY] * ri,
        out_specs=list(out_specs) + [ANY] * ro,
        out_shape=list(out_shape) + rider.out_shapes,
        scratch_shapes=list(scratch_shapes) + [pltpu.SemaphoreType.DMA((rider.n_sems,)), pltpu.SemaphoreType.DMA((rider.n_sems,))],
        input_output_aliases=aliases,
        compiler_params=_params(("arbitrary",) * len(grid)),
    )(*operands, *rider.ins)
    return list(outs[:n_out]), list(outs[n_out:])


def mmx(name, a, b, *, grid, a_spec, b_spec, o_spec, o_shape, o_dtype, ca, cb, acc_shape=None, rider=None):
    nk = grid[-1] if acc_shape is not None else 1

    def body(a_ref, b_ref, o_ref, *acc):
        p = _dot(a_ref[...].astype(BF), b_ref[...].astype(BF), ca, cb)
        if nk == 1:
            o_ref[...] = p.astype(o_ref.dtype)
        else:
            kk = pl.program_id(len(grid) - 1)
            _accumulate(acc[0], p, kk == 0)

            @pl.when(kk == nk - 1)
            def _():
                o_ref[...] = acc[0][...].astype(o_ref.dtype)

    if rider is not None:
        (out,), r_outs = host_call(body, name=name, grid=grid, in_specs=[a_spec, b_spec], out_specs=[o_spec], out_shape=[jax.ShapeDtypeStruct(o_shape, o_dtype)],
                                   scratch_shapes=[pltpu.VMEM(acc_shape, F32)] if nk > 1 else [], operands=(a, b), rider=rider)
        return out, r_outs
    sem = ("parallel",) * (len(grid) - 1) + ("arbitrary" if nk > 1 else "parallel",)
    return pl.pallas_call(
        body,
        name=name,
        grid=grid,
        in_specs=[a_spec, b_spec],
        out_specs=o_spec,
        out_shape=jax.ShapeDtypeStruct(o_shape, o_dtype),
        scratch_shapes=[pltpu.VMEM(acc_shape, F32)] if nk > 1 else [],
        compiler_params=_params(sem),
    )(a, b)


def mm(a, b, *, ta=False, tb=False, out_dtype=F32, name, tm_cap=1024, tn_cap=1024, tk_cap=4096, rider=None):
    m, k = (a.shape[1], a.shape[0]) if ta else a.shape
    n = b.shape[0] if tb else b.shape[1]
    assert (b.shape[1] if tb else b.shape[0]) == k
    tm, tn, tk = _tile(m, tm_cap), _tile(n, tn_cap), _tile(k, tk_cap)
    nk = k // tk
    a_spec = pl.BlockSpec((tk, tm), lambda i, j, kk: (kk, i)) if ta else pl.BlockSpec((tm, tk), lambda i, j, kk: (i, kk))
    b_spec = pl.BlockSpec((tn, tk), lambda i, j, kk: (j, kk)) if tb else pl.BlockSpec((tk, tn), lambda i, j, kk: (kk, j))
    return mmx(name, a, b, grid=(m // tm, n // tn, nk), a_spec=a_spec, b_spec=b_spec, o_spec=pl.BlockSpec((tm, tn), lambda i, j, kk: (i, j)),
               o_shape=(m, n), o_dtype=out_dtype, ca=0 if ta else 1, cb=1 if tb else 0, acc_shape=(tm, tn) if nk > 1 else None, rider=rider)


class Window:
    def __init__(self, out, block, cols, buf=None):
        self.out, self.block, self.cols, self.buf = out, block, cols, buf


def ew(fn, rows, vecs, out_rows, out_red=(), *, tm, name, prev_halo=(), next_halo=(), window=None):
    t = rows[0].shape[0]
    tm = min(tm, t)
    nt = t // tm
    assert t % tm == 0 and (tm % HALO == 0 or (tm == t and not prev_halo and not next_halo))
    nr, nv, npv, nnx, nor = len(rows), len(vecs), len(prev_halo), len(next_halo), len(out_rows)
    hb = tm // HALO
    n_in = nr + nv + npv + nnx
    passed = window is not None and window.buf is not None

    def body(*refs):
        i = pl.program_id(0)
        ins = [r[...].astype(F32) for r in refs[:n_in]]
        outs = refs[n_in + passed:]
        o_rows, o_red = fn(i, nt, ins[:nr], ins[nr:nr + nv], ins[nr + nv:nr + nv + npv], ins[nr + nv + npv:])
        for ref, val in zip(outs[:nor], o_rows, strict=True):
            ref[...] = val.astype(ref.dtype)
        for ref, val in zip(outs[nor:], o_red, strict=True):
            _accumulate(ref, val, i == 0)

    in_specs = [pl.BlockSpec((tm, r.shape[1]), lambda i: (i, 0)) for r in rows]
    in_specs += [pl.BlockSpec(v.shape, lambda i: (0, 0)) for v in vecs]
    in_specs += [pl.BlockSpec((HALO, rows[k].shape[1]), lambda i: (jnp.maximum(i * hb - 1, 0), 0)) for k in prev_halo]
    in_specs += [pl.BlockSpec((HALO, rows[k].shape[1]), lambda i: (jnp.minimum((i + 1) * hb, t // HALO - 1), 0)) for k in next_halo]
    out_specs = [pl.BlockSpec((tm, c), lambda i: (i, 0)) for c, _ in out_rows]
    out_specs += [pl.BlockSpec(s, lambda i: (0, 0)) for s in out_red]
    out_shape = [jax.ShapeDtypeStruct((t, c), d) for c, d in out_rows] + [jax.ShapeDtypeStruct(s, F32) for s in out_red]
    operands = [*rows, *vecs, *[rows[k] for k in prev_halo], *[rows[k] for k in next_halo]]
    aliases = {}
    if window is not None:
        c, dt_ = out_rows[window.out]
        out_specs[window.out] = pl.BlockSpec((tm, c), lambda i: (i, window.block))
        out_shape[window.out] = jax.ShapeDtypeStruct((t, window.cols), dt_)
        if passed:
            in_specs.append(ANY)
            operands.append(window.buf)
            aliases = {n_in: window.out}
    return pl.pallas_call(
        body,
        name=name,
        grid=(nt,),
        in_specs=in_specs,
        out_specs=out_specs,
        out_shape=out_shape,
        input_output_aliases=aliases,
        compiler_params=_params(("arbitrary",) if out_red else ("parallel",)),
    )(*operands)


def _shift_down(x, prev, j):
    if j == 0:
        return x
    r = pltpu.roll(x, j, 0)
    rh = pltpu.roll(prev, j, 0)
    row = lax.broadcasted_iota(jnp.int32, (HALO, x.shape[1]), 0)
    head = jnp.where(row < j, rh, r[:HALO])
    return jnp.concatenate([head, r[HALO:]], axis=0)


def _shift_up(x, nxt, j):
    if j == 0:
        return x
    n = x.shape[0]
    r = pltpu.roll(x, n - j, 0)
    rh = pltpu.roll(nxt, HALO - j, 0)
    row = lax.broadcasted_iota(jnp.int32, (HALO, x.shape[1]), 0)
    tail = jnp.where(row >= HALO - j, rh, r[n - HALO:])
    return jnp.concatenate([r[: n - HALO], tail], axis=0)


def _conv_fwd(x, prev, w):
    kk = w.shape[0]
    acc = None
    for k in range(kk):
        term = w[k:k + 1, :] * _shift_down(x, prev, kk - 1 - k)
        acc = term if acc is None else acc + term
    return acc


def ffn_up(h, g, wf, rider=None):
    t, d = h.shape
    f4 = wf.shape[1] // 2
    tm = _tile(t, 1024)
    sub = _tile(tm, FFN_SUB, 16)

    def body(h_ref, g_ref, wg_ref, wu_ref, ab_ref, s_ref, n_ref):
        @pl.when(pl.program_id(1) == 0)
        def _():
            n_ref[...] = _rms(h_ref[...], g_ref[...]).astype(BF)

        for r in range(tm // sub):
            rows = slice(r * sub, (r + 1) * sub)
            n = n_ref[rows, :]
            a = _dot(n, wg_ref[...], 1, 1)
            b = _dot(n, wu_ref[...], 1, 1)
            ab_ref[0, rows, :] = a.astype(BF)
            ab_ref[1, rows, :] = b.astype(BF)
            s_ref[rows, :] = (a * _sigmoid(a) * b).astype(BF)

    wspec = lambda ib: pl.BlockSpec((None, f4, d), lambda i, j: (j, ib, 0))
    return host_call(
        body,
        name="ffn_up",
        grid=(t // tm, NCHIP),
        in_specs=[pl.BlockSpec((tm, d), lambda i, j: (i, 0)), pl.BlockSpec((1, d), lambda i, j: (0, 0)), wspec(0), wspec(1)],
        out_specs=[pl.BlockSpec((2, None, tm, f4), lambda i, j: (0, j, i, 0)), pl.BlockSpec((None, tm, f4), lambda i, j: (j, i, 0)),
                   pl.BlockSpec((tm, d), lambda i, j: (i, 0))],
        out_shape=[jax.ShapeDtypeStruct((2, NCHIP, t, f4), BF), jax.ShapeDtypeStruct((NCHIP, t, f4), BF), jax.ShapeDtypeStruct((t, d), BF)],
        scratch_shapes=[],
        operands=(h, g, wf, wf),
        rider=rider,
    )


def ffn_down(s4, wf, h, rider=None):
    t, d = h.shape
    f4 = s4.shape[2]
    tm = _tile(t, 512)

    def body(s_ref, w_ref, h_ref, o_ref):
        acc = _dot(s_ref[0], w_ref[0])
        for k in range(1, NCHIP):
            acc = acc + _dot(s_ref[k], w_ref[k])
        o_ref[...] = h_ref[...] + 0.5 * acc

    (out,), r_outs = host_call(
        body,
        name="ffn_down",
        grid=(t // tm,),
        in_specs=[pl.BlockSpec((NCHIP, tm, f4), lambda i: (0, i, 0)), pl.BlockSpec((NCHIP, f4, d), lambda i: (0, 0, 0)), pl.BlockSpec((tm, d), lambda i: (i, 0))],
        out_specs=[pl.BlockSpec((tm, d), lambda i: (i, 0))],
        out_shape=[jax.ShapeDtypeStruct((t, d), F32)],
        scratch_shapes=[],
        operands=(s4, wf, h),
        rider=rider,
    )
    return out, r_outs


def ffn_bwd(dho, ab, wf, wd, rider=None):
    t, d = dho.shape
    f4 = wd.shape[1]
    tm = _tile(t, 1024)
    sub = _tile(tm, FFN_SUB, 16)

    def body(dho_ref, ab_ref, wg_ref, wu_ref, wd_ref, dn_ref, s_ref, dab_ref, do_sc):
        j = pl.program_id(1)

        @pl.when(j == 0)
        def _():
            do_sc[...] = (0.5 * dho_ref[...]).astype(BF)
            dn_ref[...] = jnp.zeros_like(dn_ref)

        for r in range(tm // sub):
            rows = slice(r * sub, (r + 1) * sub)
            ds = _dot(do_sc[rows, :], wd_ref[...], 1, 1)
            av, bv = ab_ref[0, rows, :].astype(F32), ab_ref[1, rows, :].astype(F32)
            sig = _sigmoid(av)
            sl = av * sig
            s_ref[rows, :] = (sl * bv).astype(BF)
            da = (ds * bv * (sig * (1.0 + av * (1.0 - sig)))).astype(BF)
            db = (ds * sl).astype(BF)
            dab_ref[0, rows, :] = da
            dab_ref[1, rows, :] = db
            dn_ref[rows, :] += _dot(da, wg_ref[...]) + _dot(db, wu_ref[...])

    row = lambda c: pl.BlockSpec((tm, c), lambda i, j: (i, 0))
    wspec = lambda ib: pl.BlockSpec((None, f4, d), lambda i, j: (j, ib, 0))
    ab_spec = pl.BlockSpec((2, None, tm, f4), lambda i, j: (0, j, i, 0))
    return host_call(
        body,
        name="ffn_bwd",
        grid=(t // tm, NCHIP),
        in_specs=[row(d), ab_spec, wspec(0), wspec(1), wspec(0)],
        out_specs=[row(d), pl.BlockSpec((None, tm, f4), lambda i, j: (j, i, 0)), ab_spec, row(d)],
        out_shape=[jax.ShapeDtypeStruct((t, d), F32), jax.ShapeDtypeStruct((NCHIP, t, f4), BF), jax.ShapeDtypeStruct((2, NCHIP, t, f4), BF),
                   jax.ShapeDtypeStruct((t, d), BF)],
        scratch_shapes=[],
        operands=(dho, ab, wf, wf, wd),
        rider=rider,
    )


def ffn_wgrads(n, do, s4, dab):
    t, d = n.shape
    f4 = s4.shape[2]
    g_in = mmx("g_ffn_in", dab, n, grid=(2, NCHIP, 1), a_spec=pl.BlockSpec((None, None, t, f4), lambda wh, k, j: (wh, k, 0, 0)),
               b_spec=pl.BlockSpec((t, d), lambda wh, k, j: (0, 0)), o_spec=pl.BlockSpec((None, None, f4, d), lambda wh, k, j: (k, wh, 0, 0)),
               o_shape=(NCHIP, 2, f4, d), o_dtype=BF, ca=0, cb=0)
    g_out = mmx("g_ffn_out", s4, do, grid=(NCHIP, 1), a_spec=pl.BlockSpec((None, t, f4), lambda k, j: (k, 0, 0)),
                b_spec=pl.BlockSpec((t, d), lambda k, j: (0, 0)), o_spec=pl.BlockSpec((None, f4, d), lambda k, j: (k, 0, 0)),
                o_shape=(NCHIP, f4, d), o_dtype=BF, ca=0, cb=0)
    return jnp.concatenate([g_in.reshape(NCHIP, 2 * f4, d), g_out], axis=1)


def norm_cast(h, g):
    def fn(i, nt, rows, vecs, prevs, nexts):
        return [_rms(rows[0], vecs[0])], []
    return ew(fn, [h], [g], [(h.shape[1], BF)], tm=512, name="norm_cast")[0]


def _zero_if(cond, x):
    return jnp.where(cond, jnp.zeros_like(x), x)


def conv_a_fwd(sc3, w_sc):
    d = sc3.shape[1] // 3

    def fn(i, nt, rows, vecs, prevs, nexts):
        x, pv = rows[0], _zero_if(i == 0, prevs[0])
        v = x[:, d:2 * d] * x[:, 2 * d:]
        vp = pv[:, d:2 * d] * pv[:, 2 * d:]
        return [x[:, :d] * _conv_fwd(v, vp, vecs[0])], []

    return ew(fn, [sc3], [w_sc], [(d, BF)], tm=256, name="conv_a_fwd", prev_halo=(0,))[0]


def _softplus(x):
    e = jnp.exp(-jnp.abs(x))
    return jnp.maximum(x, 0.0) + jnp.where(e < 1e-4, e - 0.5 * e * e, jnp.log(1.0 + e))


def conv_m_fwd(xbc_raw, dt_raw, w_mc, b_mc, dt_bias):
    def fn(i, nt, rows, vecs, prevs, nexts):
        pre = _conv_fwd(rows[0], _zero_if(i == 0, prevs[0]), vecs[0]) + vecs[1]
        return [pre * _sigmoid(pre), _softplus(rows[1] + vecs[2])], []

    return ew(fn, [xbc_raw, dt_raw], [w_mc, b_mc, dt_bias], [(xbc_raw.shape[1], F32), (LANE, F32)], tm=256, name="conv_m_fwd",
              prev_halo=(0,))


def conv_m_bwd1(dxbc, xbc_raw, ddt, dt_raw, w_mc, b_mc, dt_bias, window):
    def fn(i, nt, rows, vecs, prevs, nexts):
        pre = _conv_fwd(rows[1], _zero_if(i == 0, prevs[0]), vecs[0]) + vecs[1]
        sig = _sigmoid(pre)
        dpre = rows[0] * (sig * (1.0 + pre * (1.0 - sig)))
        ddr = rows[2] * _sigmoid(rows[3] + vecs[2])
        return [dpre, ddr], [jnp.sum(ddr, axis=0, keepdims=True)]

    return ew(fn, [dxbc, xbc_raw, ddt, dt_raw], [w_mc, b_mc, dt_bias], [(dxbc.shape[1], F32), (LANE, BF)], [(1, LANE)], tm=256,
              name="conv_m_bwd1", prev_halo=(1,), window=window)


def conv_bwd2(dpre, x, w, name, window):
    kk = w.shape[0]

    def fn(i, nt, rows, vecs, prevs, nexts):
        dp, xv = rows[0], rows[1]
        nx = _zero_if(i == nt - 1, nexts[0])
        dx = None
        dws = []
        for k in range(kk):
            up = _shift_up(dp, nx, kk - 1 - k)
            term = vecs[0][k:k + 1, :] * up
            dx = term if dx is None else dx + term
            dws.append(jnp.sum(up * xv, axis=0, keepdims=True))
        return [dx], [jnp.concatenate(dws, axis=0), jnp.sum(dp, axis=0, keepdims=True)]

    c = x.shape[1]
    return ew(fn, [dpre, x], [w], [(c, BF)], [(kk, c), (1, c)], tm=256, name=name, next_halo=(0,), window=window)


def conv_a_bwd1(dya, sc3, w_sc, window):
    d = sc3.shape[1] // 3

    def fn(i, nt, rows, vecs, prevs, nexts):
        x, pv = rows[1], _zero_if(i == 0, prevs[0])
        v = x[:, d:2 * d] * x[:, 2 * d:]
        vp = pv[:, d:2 * d] * pv[:, 2 * d:]
        return [rows[0] * x[:, :d], rows[0] * _conv_fwd(v, vp, vecs[0]), v], []

    return ew(fn, [dya, sc3], [w_sc], [(d, F32), (d, BF), (d, F32)], tm=256, name="conv_a_bwd1", prev_halo=(1,), window=window)


def conv_a_bwd2(dcv, v, sc3, w_sc, window):
    d = v.shape[1]
    kk = w_sc.shape[0]

    def fn(i, nt, rows, vecs, prevs, nexts):
        dp, vv, x = rows
        nx = _zero_if(i == nt - 1, nexts[0])
        dv = None
        dws = []
        for k in range(kk):
            up = _shift_up(dp, nx, kk - 1 - k)
            term = vecs[0][k:k + 1, :] * up
            dv = term if dv is None else dv + term
            dws.append(jnp.sum(up * vv, axis=0, keepdims=True))
        return [jnp.concatenate([dv * x[:, 2 * d:], dv * x[:, d:2 * d]], axis=1)], [jnp.concatenate(dws, axis=0)]

    return ew(fn, [dcv, v, sc3], [w_sc], [(2 * d, BF)], [(kk, d)], tm=256, name="conv_a_bwd2", next_halo=(0,), window=window)


def _xdot(a, b, passes, split_lhs, ca=1, cb=0):
    parts, r = [], (a if split_lhs else b)
    for _ in range(passes):
        piece = r.astype(BF)
        parts.append(piece)
        r = r - piece.astype(F32)
    other = (b if split_lhs else a).astype(BF)
    acc = None
    for piece in parts:
        term = _dot(piece, other, ca, cb) if split_lhs else _dot(other, piece, ca, cb)
        acc = term if acc is None else acc + term
    return acc


def _ssd_common(xbc_ref, dt_ref, alog_ref, e_ref, w):
    ll = SSM_L
    xs = xbc_ref[:, 0:w]
    dtv = dt_ref[...]
    a_row = -jnp.exp(alog_ref[...])
    a = dtv * a_row
    row = lax.broadcasted_iota(jnp.int32, (ll, ll), 0)
    col = lax.broadcasted_iota(jnp.int32, (ll, ll), 1)
    tril = (row >= col).astype(F32)
    triu = (row <= col).astype(F32)
    acl = _xdot(tril, a, 3, False)
    acl_t = _xdot(a, triu, 3, True, 0, 0)
    e = e_ref[...]
    aclx = _xdot(acl, e, 3, True)
    dtx = _xdot(dtv, e, 2, True)
    last = aclx[ll - 1:ll, :]
    e_in = jnp.exp(aclx)
    e_end = jnp.exp(last - aclx)
    e_tot = jnp.exp(last)
    x = xs * dtx
    return dict(xs=xs, dtv=dtv, a_row=a_row, a=a, row=row, col=col, triu=triu, acl=acl, acl_t=acl_t, dtx=dtx, e_in=e_in, e_end=e_end,
                e_tot=e_tot, x=x)


def _decay(q, hh):
    diff = q["acl"][:, hh:hh + 1] - q["acl_t"][hh:hh + 1, :]
    return jnp.exp(jnp.where(q["row"] >= q["col"], diff, -jnp.inf))


def ssd_fwd(xbc, dt, z, a_log, d_exp, m_norm, e_mat, rider=None):
    t = xbc.shape[0]
    w = z.shape[1]
    gn = SSM_G * SSM_N
    gw = w // SSM_G
    ll, nn = SSM_L, SSM_N
    nc = t // ll
    cw = xbc.shape[1]

    def body(xbc_ref, dt_ref, z_ref, alog_ref, dexp_ref, mn_ref, e_ref, yn_ref, y_ref, sp_ref, s_sc):
        c = pl.program_id(0)

        @pl.when(c == 0)
        def _():
            s_sc[...] = jnp.zeros_like(s_sc)

        q = _ssd_common(xbc_ref, dt_ref, alog_ref, e_ref, w)
        xb = q["x"].astype(BF)
        xsb = (q["x"] * q["e_end"]).astype(BF)
        sp = s_sc[...]
        sp_ref[0] = sp
        spb = sp.astype(BF)
        lane = lax.broadcasted_iota(jnp.int32, (ll, LANE), 1)
        for g in range(SSM_G):
            lo = g * gw
            bg = xbc_ref[:, w + g * nn:w + (g + 1) * nn].astype(BF)
            cg = xbc_ref[:, w + gn + g * nn:w + gn + (g + 1) * nn].astype(BF)
            yoff = _dot(cg, spb[:, lo:lo + gw]) * q["e_in"][:, lo:lo + gw]
            s_sc[:, lo:lo + gw] = sp[:, lo:lo + gw] * q["e_tot"][:, lo:lo + gw] + _dot(bg, xsb[:, lo:lo + gw], 0, 0)
            cb = _dot(cg, bg, 1, 1)
            for pr in range(gw // LANE):
                l0 = lo + pr * LANE
                xp = xb[:, l0:l0 + LANE]
                ys = []
                for hh in (l0 // SSM_P, l0 // SSM_P + 1):
                    wm = (cb * _decay(q, hh)).astype(BF)
                    ys.append(_dot(wm, xp))
                ydiag = jnp.where(lane < SSM_P, ys[0], ys[1])
                y_ref[:, l0:l0 + LANE] = ydiag + yoff[:, pr * LANE:(pr + 1) * LANE] + dexp_ref[:, l0:l0 + LANE] * q["xs"][:, l0:l0 + LANE]
        zv = z_ref[...].astype(F32)
        yz = y_ref[...] * (zv * _sigmoid(zv))
        for g in range(SSM_G):
            lo = g * gw
            yn_ref[:, lo:lo + gw] = _rms(yz[:, lo:lo + gw], mn_ref[:, lo:lo + gw]).astype(BF)

    vec = lambda s: pl.BlockSpec(s, lambda c: (0, 0))
    return host_call(
        body,
        name="ssd_fwd",
        grid=(nc,),
        in_specs=[
            pl.BlockSpec((ll, cw), lambda c: (c, 0)), pl.BlockSpec((ll, LANE), lambda c: (c, 0)), pl.BlockSpec((ll, w), lambda c: (c, 0)),
            vec((1, LANE)), vec((1, w)), vec((1, w)), vec((LANE, w)),
        ],
        out_specs=[pl.BlockSpec((ll, w), lambda c: (c, 0)), pl.BlockSpec((ll, w), lambda c: (c, 0)), pl.BlockSpec((1, nn, w), lambda c: (c, 0, 0))],
        out_shape=[jax.ShapeDtypeStruct((t, w), BF), jax.ShapeDtypeStruct((t, w), F32), jax.ShapeDtypeStruct((nc, nn, w), F32)],
        scratch_shapes=[pltpu.VMEM((nn, w), F32)],
        operands=(xbc, dt, z, a_log, d_exp, m_norm, e_mat),
        rider=rider,
    )


def ssd_bwd(dyn, y, z, xbc, dt, sprev, a_log, d_exp, m_norm, e_mat, et_mat, window, rider=None):
    t = xbc.shape[0]
    w = z.shape[1]
    gn = SSM_G * SSM_N
    gw = w // SSM_G
    ll, nn = SSM_L, SSM_N
    nc = t // ll
    cw = xbc.shape[1]

    def body(dyn_ref, y_ref, z_ref, xbc_ref, dt_ref, sp_ref, alog_ref, dexp_ref, mn_ref, e_ref, et_ref,
             dz_ref, dxbc_ref, ddt_ref, dmn_ref, dd_ref, dal_ref, ds_sc, dy_sc, dx_sc):
        step = pl.program_id(0)

        @pl.when(step == 0)
        def _():
            ds_sc[...] = jnp.zeros_like(ds_sc)

        zv, yv = z_ref[...].astype(F32), y_ref[...]
        sg = _sigmoid(zv)
        sz = zv * sg
        yz = yv * sz
        dmn = []
        for g in range(SSM_G):
            lo = g * gw
            dseg, dmn_g = _rms_bwd(yz[:, lo:lo + gw], mn_ref[:, lo:lo + gw], dyn_ref[:, lo:lo + gw])
            dy_sc[:, lo:lo + gw] = dseg
            dmn.append(dmn_g)
        dmn = jnp.concatenate(dmn, axis=1)
        dyz = dy_sc[...]
        dz_ref[...] = (dyz * yv * (sg * (1.0 + zv * (1.0 - sg)))).astype(BF)
        dy = dyz * sz

        q = _ssd_common(xbc_ref, dt_ref, alog_ref, e_ref, w)
        x = q["x"]
        xb = x.astype(BF)
        xsb = (x * q["e_end"]).astype(BF)
        sp = sp_ref[0]
        spb = sp.astype(BF)
        dsn = ds_sc[...]
        dsnb = dsn.astype(BF)
        dyb = dy.astype(BF)
        lane = lax.broadcasted_iota(jnp.int32, (ll, LANE), 1)
        lane1 = lax.broadcasted_iota(jnp.int32, (1, LANE), 1)
        sub1 = lax.broadcasted_iota(jnp.int32, (LANE, 1), 0)
        dacl = jnp.zeros((ll, LANE), F32)
        dacl_t = jnp.zeros((LANE, ll), F32)
        d_ein, d_eend, d_etot = [], [], []
        for g in range(SSM_G):
            lo = g * gw
            sl = slice(lo, lo + gw)
            bg = xbc_ref[:, w + g * nn:w + (g + 1) * nn].astype(BF)
            cg = xbc_ref[:, w + gn + g * nn:w + gn + (g + 1) * nn].astype(BF)
            zg = _dot(cg, spb[:, sl])
            dzz = (dy[:, sl] * q["e_in"][:, sl]).astype(BF)
            d_ein.append(dy[:, sl] * zg)
            dcg = _dot(dzz, spb[:, sl], 1, 1)
            ds_sc[:, sl] = _dot(cg, dzz, 0, 0) + dsn[:, sl] * q["e_tot"][:, sl]
            d_etot.append(jnp.sum(dsn[:, sl] * sp[:, sl], axis=0, keepdims=True))
            dbg = _dot(xsb[:, sl], dsnb[:, sl], 1, 1)
            dxs_g = _dot(bg, dsnb[:, sl])
            d_eend.append(dxs_g * x[:, sl])
            cb = _dot(cg, bg, 1, 1)
            dcb = jnp.zeros((ll, ll), F32)
            for pr in range(gw // LANE):
                l0 = lo + pr * LANE
                xp = xb[:, l0:l0 + LANE]
                dyp = dyb[:, l0:l0 + LANE]
                dxp = []
                for hi, hh in enumerate((l0 // SSM_P, l0 // SSM_P + 1)):
                    lm = _decay(q, hh)
                    wm = (cb * lm).astype(BF)
                    in_head = (lane < SSM_P) if hi == 0 else (lane >= SSM_P)
                    dwm = _dot(jnp.where(in_head, dyp, jnp.zeros_like(dyp)), xp, 1, 1)
                    dxp.append(_dot(wm, dyp, 0, 0))
                    dlm = dwm * lm
                    dcb = dcb + dlm
                    dd = dlm * cb
                    dacl = dacl + jnp.sum(dd, axis=1, keepdims=True) * (lane1 == hh).astype(F32)
                    dacl_t = dacl_t + (sub1 == hh).astype(F32) * jnp.sum(dd, axis=0, keepdims=True)
                dx_sc[:, l0:l0 + LANE] = jnp.where(lane < SSM_P, dxp[0], dxp[1]) + dxs_g[:, pr * LANE:(pr + 1) * LANE] * q["e_end"][:, l0:l0 + LANE]
            dcbb = dcb.astype(BF)
            dxbc_ref[:, w + g * nn:w + (g + 1) * nn] = dbg + _dot(dcbb, cg, 0, 0)
            dxbc_ref[:, w + gn + g * nn:w + gn + (g + 1) * nn] = dcg + _dot(dcbb, bg)
        d_ein = jnp.concatenate(d_ein, axis=1) * q["e_in"]
        d_eend = jnp.concatenate(d_eend, axis=1) * q["e_end"]
        d_etot = jnp.concatenate(d_etot, axis=1) * q["e_tot"]
        et = et_ref[...]
        last_add = jnp.sum(d_eend, axis=0, keepdims=True) + d_etot
        last_add = _xdot(jnp.broadcast_to(last_add, (HALO, w)), et, 2, True)[0:1]
        row1 = lax.broadcasted_iota(jnp.int32, (ll, LANE), 0)
        dacl = dacl + _xdot(d_ein - d_eend, et, 2, True) + jnp.where(row1 == ll - 1, last_add, 0.0)
        da = _xdot(q["triu"], dacl, 2, False) - _xdot(q["triu"], dacl_t, 2, False, 1, 1)
        dxv = dx_sc[...]
        dxbc_ref[:, 0:w] = dexp_ref[...] * dy + dxv * q["dtx"]
        ddt_ref[...] = _xdot(dxv * q["xs"], et, 2, True) + da * q["a_row"]
        dal = jnp.sum(da * q["dtv"], axis=0, keepdims=True) * q["a_row"]
        ddv = jnp.sum(dy * q["xs"], axis=0, keepdims=True)
        ddv = _xdot(jnp.broadcast_to(ddv, (HALO, w)), et, 2, True)[0:1]
        _accumulate(dmn_ref, dmn, step == 0)
        _accumulate(dd_ref, ddv, step == 0)
        _accumulate(dal_ref, dal, step == 0)

    rev = lambda c_: pl.BlockSpec((ll, c_), lambda s: (nc - 1 - s, 0))
    vec = lambda s_: pl.BlockSpec(s_, lambda s: (0, 0))
    n_in = 11

    def body_skipping_buffer(*refs):
        body(*refs[:n_in], *refs[n_in + 1:])

    return host_call(
        body_skipping_buffer,
        name="ssd_bwd",
        grid=(nc,),
        in_specs=[
            rev(w), rev(w), rev(w), rev(cw), rev(LANE), pl.BlockSpec((1, nn, w), lambda s: (nc - 1 - s, 0, 0)),
            vec((1, LANE)), vec((1, w)), vec((1, w)), vec((LANE, w)), vec((w, LANE)), ANY,
        ],
        out_specs=[pl.BlockSpec((ll, w), lambda s: (nc - 1 - s, window.block)), rev(cw), rev(LANE), vec((1, w)), vec((1, LANE)), vec((1, LANE))],
        out_shape=[
            jax.ShapeDtypeStruct((t, window.cols), BF), jax.ShapeDtypeStruct((t, cw), F32), jax.ShapeDtypeStruct((t, LANE), F32),
            jax.ShapeDtypeStruct((1, w), F32), jax.ShapeDtypeStruct((1, LANE), F32), jax.ShapeDtypeStruct((1, LANE), F32),
        ],
        scratch_shapes=[pltpu.VMEM((nn, w), F32), pltpu.VMEM((ll, w), F32), pltpu.VMEM((ll, w), F32)],
        operands=(dyn, y, z, xbc, dt, sprev, a_log, d_exp, m_norm, e_mat, et_mat, window.buf),
        rider=rider,
        aliases={n_in: 0},
    )


def _w1024_spec(d, nblk, iblk):
    r = nblk * (d // NCHIP)
    return pl.BlockSpec((NCHIP, r, d), lambda i: (0, iblk // nblk, 0))


def _whole(ref):
    v = ref[...]
    return v.reshape(v.shape[0] * v.shape[1], v.shape[2])


def mix_out_fwd(ya_in, yn, gates, h, w1024, rider=None):
    t, d = h.shape
    tm = _tile(t, 256)

    def body(ya_ref, yn_ref, g_ref, h_ref, wm_ref, wa_ref, wo_ref, ho_ref, oa_ref, om_ref, mg_ref):
        y_a = _dot(ya_ref[...], _whole(wa_ref))
        y_m = _dot(yn_ref[...], _whole(wm_ref))
        oa_ref[...] = y_a
        om_ref[...] = y_m
        gv = g_ref[...].astype(F32)
        mg = (_sigmoid(gv[:, :d]) * y_a + _sigmoid(gv[:, d:]) * y_m).astype(BF)
        mg_ref[...] = mg
        ho_ref[...] = h_ref[...] + _dot(mg, _whole(wo_ref))

    row = lambda c: pl.BlockSpec((tm, c), lambda i: (i, 0))
    return host_call(
        body,
        name="mix_out_fwd",
        grid=(t // tm,),
        in_specs=[row(d), row(2 * d), row(2 * d), row(d), _w1024_spec(d, 2, 0), _w1024_spec(d, 1, 2), _w1024_spec(d, 1, 3)],
        out_specs=[row(d), row(d), row(d), row(d)],
        out_shape=[jax.ShapeDtypeStruct((t, d), F32), jax.ShapeDtypeStruct((t, d), F32), jax.ShapeDtypeStruct((t, d), F32),
                   jax.ShapeDtypeStruct((t, d), BF)],
        scratch_shapes=[],
        operands=(ya_in, yn, gates, h, w1024, w1024, w1024),
        rider=rider,
    )


def mix_out_bwd(dh, gates, y_a, y_m, w1024, cols):
    t, d = dh.shape
    tm = _tile(t, 256)

    def body(dh_ref, g_ref, ya_ref, ym_ref, wm_ref, wa_ref, wo_ref, dg_ref, dya_ref, dyn_ref, da_ref, dm_ref):
        dmg = _dot(dh_ref[...].astype(BF), _whole(wo_ref), 1, 1)
        gv = g_ref[...].astype(F32)
        sa, sm = _sigmoid(gv[:, :d]), _sigmoid(gv[:, d:])
        dg_ref[:, :d] = (dmg * ya_ref[...] * sa * (1.0 - sa)).astype(BF)
        dg_ref[:, d:] = (dmg * ym_ref[...] * sm * (1.0 - sm)).astype(BF)
        da = (dmg * sa).astype(BF)
        dm = (dmg * sm).astype(BF)
        da_ref[...] = da
        dm_ref[...] = dm
        dya_ref[...] = _dot(da, _whole(wa_ref), 1, 1)
        dyn_ref[...] = _dot(dm, _whole(wm_ref), 1, 1)

    row = lambda c: pl.BlockSpec((tm, c), lambda i: (i, 0))
    return pl.pallas_call(
        body,
        name="mix_out_bwd",
        grid=(t // tm,),
        in_specs=[row(d), row(2 * d), row(d), row(d), _w1024_spec(d, 2, 0), _w1024_spec(d, 1, 2), _w1024_spec(d, 1, 3)],
        out_specs=[row(2 * d), row(d), row(2 * d), row(d), row(d)],
        out_shape=[jax.ShapeDtypeStruct((t, cols), BF), jax.ShapeDtypeStruct((t, d), F32), jax.ShapeDtypeStruct((t, 2 * d), F32),
                   jax.ShapeDtypeStruct((t, d), BF), jax.ShapeDtypeStruct((t, d), BF)],
        compiler_params=_params(("parallel",)),
    )(dh, gates, y_a, y_m, w1024, w1024, w1024)


def norm_bwd_add(dh, h, g, dn):
    def fn(i, nt, rows, vecs, prevs, nexts):
        dx, dg = _rms_bwd(rows[1], vecs[0], rows[2])
        return [rows[0] + dx], [dg]
    d = h.shape[1]
    return ew(fn, [dh, h, dn], [g], [(d, F32)], [(1, d)], tm=512, name="norm_bwd_add")


def _pe(p, wpp_ref):
    pb = p.astype(BF)
    return jnp.concatenate([_dot(pb, wpp_ref[k]) for k in range(NCHIP)], axis=1)


def ple_fwd(h, g, p, w1024, wpp):
    t, d = h.shape
    tm = _tile(t, 512)

    def body(h_ref, g_ref, p_ref, wg_ref, wp_ref, ho_ref):
        hv = h_ref[...]
        gate = _sigmoid(_dot(_rms(hv, g_ref[...]).astype(BF), _whole(wg_ref)))
        ho_ref[...] = hv + gate * _pe(p_ref[...], wp_ref)

    row = lambda c: pl.BlockSpec((tm, c), lambda i: (i, 0))
    wpp_spec = pl.BlockSpec(wpp.shape, lambda i: (0, 0, 0))
    return pl.pallas_call(
        body,
        name="ple_fwd",
        grid=(t // tm,),
        in_specs=[row(d), pl.BlockSpec((1, d), lambda i: (0, 0)), row(p.shape[1]), _w1024_spec(d, 1, 4), wpp_spec],
        out_specs=row(d),
        out_shape=jax.ShapeDtypeStruct((t, d), F32),
        compiler_params=_params(("parallel",)),
    )(h, g, p, w1024, wpp)


def ple_bwd(dho, h, g, p, w1024, wpp):
    t, d = h.shape
    tm = _tile(t, 512)

    def body(dho_ref, h_ref, g_ref, p_ref, wg_ref, wp_ref, dh_ref, dg_ref, n_ref, dgp_ref, dpe_ref):
        hv, dv = h_ref[...], dho_ref[...]
        n = _rms(hv, g_ref[...]).astype(BF)
        n_ref[...] = n
        wg = _whole(wg_ref)
        gate = _sigmoid(_dot(n, wg))
        pe = _pe(p_ref[...], wp_ref)
        dpe_ref[...] = (dv * gate).astype(BF)
        dgp = (dv * pe * gate * (1.0 - gate)).astype(BF)
        dgp_ref[...] = dgp
        dx, dg = _rms_bwd(hv, g_ref[...], _dot(dgp, wg, 1, 1))
        dh_ref[...] = dv + dx
        _accumulate(dg_ref, dg, pl.program_id(0) == 0)

    row = lambda c: pl.BlockSpec((tm, c), lambda i: (i, 0))
    wpp_spec = pl.BlockSpec(wpp.shape, lambda i: (0, 0, 0))
    return pl.pallas_call(
        body,
        name="ple_bwd",
        grid=(t // tm,),
        in_specs=[row(d), row(d), pl.BlockSpec((1, d), lambda i: (0, 0)), row(p.shape[1]), _w1024_spec(d, 1, 4), wpp_spec],
        out_specs=[row(d), pl.BlockSpec((1, d), lambda i: (0, 0)), row(d), row(d), row(d)],
        out_shape=[jax.ShapeDtypeStruct((t, d), F32), jax.ShapeDtypeStruct((1, d), F32), jax.ShapeDtypeStruct((t, d), BF),
                   jax.ShapeDtypeStruct((t, d), BF), jax.ShapeDtypeStruct((t, d), BF)],
        compiler_params=_params(("arbitrary",)),
    )(dho, h, g, p, w1024, wpp)


def loss_bwd(h, g, target):
    d = h.shape[1]

    def fn(i, nt, rows, vecs, prevs, nexts):
        err = _rms(rows[0], vecs[0]) - rows[1]
        dx, dg = _rms_bwd(rows[0], vecs[0], err * (1.0 / d))
        return [dx], [jnp.sum(err * err, axis=0, keepdims=True) * (0.5 / d), dg]

    return ew(fn, [h, target], [g], [(d, F32)], [(1, d), (1, d)], tm=512, name="loss_bwd")


def adamw(w, g, m, v, name):
    c1, c2 = 1.0 / (1.0 - ADAM_B1 ** ADAM_STEP), 1.0 / (1.0 - ADAM_B2 ** ADAM_STEP)

    def fn(i, nt, rows, vecs, prevs, nexts):
        wv, gv, mv, vv = rows
        mn = ADAM_B1 * mv + (1.0 - ADAM_B1) * gv
        vn = ADAM_B2 * vv + (1.0 - ADAM_B2) * (gv * gv)
        delta = -ADAM_LR * ((mn * c1) / (jnp.sqrt(vn * c2) + ADAM_EPS) + ADAM_WD * wv)
        return [delta, mn, vn], []

    c = w.shape[1]
    return ew(fn, [w, g, m, v], [], [(c, F32)] * 3, tm=_row_tile(w.shape[0], c, HALO), name=name)


def _place():
    return lax.axis_index("x"), lax.axis_index("y"), lax.axis_index("c")


def _other_chips(x, y):
    return [(1 - x, y), (x, 1 - y), (1 - x, 1 - y)]


ANY = pl.BlockSpec(memory_space=pl.ANY)


def _comm_call(body, name, ins, out_shapes, n_sems, aliases=None):
    return pl.pallas_call(
        body,
        name=name,
        in_specs=[ANY] * len(ins),
        out_specs=[ANY] * len(out_shapes),
        out_shape=out_shapes,
        scratch_shapes=[pltpu.SemaphoreType.DMA((n_sems,)), pltpu.SemaphoreType.DMA((n_sems,))],
        input_output_aliases=aliases or {},
    )(*ins)


def gather_rider(packs):
    nt = len(packs)

    def pieces(ins, outs, send_sems, recv_sems):
        x, y, cc = _place()
        chips = _other_chips(x, y)
        sibling = (x, y, 1 - cc)
        k_me = 2 * x + y

        def copy(k, src, dst, to):
            return pltpu.make_async_remote_copy(src_ref=src, dst_ref=dst, send_sem=send_sems.at[k], recv_sem=recv_sems.at[k],
                                                device_id=to, device_id_type=MESH)

        sends, forwards, arrivals = [], [], []
        for ti in range(nt):
            for j, (px, py) in enumerate(chips):
                sends.append(copy(7 * ti + j, ins[ti].at[cc], outs[ti].at[k_me, cc], (px, py, cc)))
                landed = outs[ti].at[2 * px + py, cc]
                forwards.append((copy(7 * ti + j, landed, landed, (px, py, cc)), copy(7 * ti + 3 + j, landed, landed, sibling)))
                passed = outs[ti].at[2 * px + py, 1 - cc]
                arrivals.append(copy(7 * ti + 3 + j, passed, passed, sibling))
            sends.append(copy(7 * ti + 6, ins[ti], outs[ti].at[k_me], sibling))
            own = outs[ti].at[k_me]
            arrivals.append(copy(7 * ti + 6, own, own, sibling))
        return sends, forwards, arrivals

    def start(*parts):
        for cp in pieces(*parts)[0]:
            cp.start()

    def mid(*parts):
        for landed, forward in pieces(*parts)[1]:
            landed.wait_recv()
            forward.start()

    def finish(*parts):
        sends, forwards, arrivals = pieces(*parts)
        for cp in arrivals:
            cp.wait_recv()
        for cp in sends + [f for _, f in forwards]:
            cp.wait_send()

    return Rider(packs, [jax.ShapeDtypeStruct((NCHIP,) + p.shape, p.dtype) for p in packs], 7 * nt, start, finish, mid)


def swap_rider(gs):
    nt = len(gs)
    hl = gs[0].shape[1] // 2

    def copies(ins, outs, send_sems, recv_sems):
        x, y, cc = _place()
        theirs = pl.ds((1 - cc) * hl, hl)
        return [pltpu.make_async_remote_copy(src_ref=ins[ti].at[:, theirs], dst_ref=outs[ti], send_sem=send_sems.at[ti], recv_sem=recv_sems.at[ti],
                                             device_id=(x, y, 1 - cc), device_id_type=MESH) for ti in range(nt)]

    def start(*parts):
        for cp in copies(*parts):
            cp.start()

    def finish(*parts):
        for cp in copies(*parts):
            cp.wait()

    return Rider(gs, [jax.ShapeDtypeStruct((NCHIP, hl) + g.shape[2:], g.dtype) for g in gs], nt, start, finish)


def scatter_packs(cs, name):
    return scatter_rider(cs).standalone(name)


def scatter_rider(cs):
    nt = len(cs)

    def copies(ins, outs, send_sems, recv_sems):
        x, y, cc = _place()
        cps = []
        for ti in range(nt):
            for j, (px, py) in enumerate(_other_chips(x, y)):
                cps.append(pltpu.make_async_remote_copy(src_ref=ins[ti].at[2 * px + py], dst_ref=outs[ti].at[j], send_sem=send_sems.at[3 * ti + j],
                                                        recv_sem=recv_sems.at[3 * ti + j], device_id=(px, py, cc), device_id_type=MESH))
        return cps

    def start(*parts):
        for cp in copies(*parts):
            cp.start()

    def finish(*parts):
        for cp in copies(*parts):
            cp.wait()

    return Rider(cs, [jax.ShapeDtypeStruct((3,) + c_.shape[1:], c_.dtype) for c_ in cs], 3 * nt, start, finish)


def join_packs(fulls, name):
    nt = len(fulls)
    hl = fulls[0].shape[0] // 2

    def body(*refs):
        ins, outs, (send_sems, recv_sems) = refs[:nt], refs[nt:2 * nt], refs[2 * nt:]
        x, y, cc = _place()
        mine = pl.ds(cc * hl, hl)
        cps = [pltpu.make_async_remote_copy(src_ref=ins[ti].at[mine], dst_ref=outs[ti].at[mine], send_sem=send_sems.at[ti], recv_sem=recv_sems.at[ti],
                                            device_id=(x, y, 1 - cc), device_id_type=MESH) for ti in range(nt)]
        for cp in cps:
            cp.start()
        for cp in cps:
            cp.wait()

    return _comm_call(body, name, fulls, [jax.ShapeDtypeStruct(f.shape, f.dtype) for f in fulls], nt, aliases={ti: ti for ti in range(nt)})


def add_sibling(g, recv, name):
    _, nl, r, c = g.shape
    hl = nl // 2
    tm, tc = _tile2(r, c)

    def body(g_ref, r_ref, o_ref):
        o_ref[...] = (g_ref[...].astype(F32) + r_ref[...].astype(F32)).astype(o_ref.dtype)

    blk = (None, None, tm, tc)
    return pl.pallas_call(
        body,
        name=name,
        grid=(NCHIP, hl, r // tm, c // tc),
        in_specs=[pl.BlockSpec(blk, lambda k, l, i, j: (k, lax.axis_index("c") * hl + l, i, j)), pl.BlockSpec(blk, lambda k, l, i, j: (k, l, i, j))],
        out_specs=pl.BlockSpec(blk, lambda k, l, i, j: (k, l, i, j)),
        out_shape=jax.ShapeDtypeStruct(recv.shape, BF),
        compiler_params=_params(("parallel",) * 4),
    )(g, recv)


def add_chips(cs, got, nl, name):
    _, hl, r, c = cs.shape
    tm, tc = _tile2(r, c)

    def body(own_ref, got_ref, o_ref):
        o_ref[...] = own_ref[...].astype(F32) + got_ref[0].astype(F32) + got_ref[1].astype(F32) + got_ref[2].astype(F32)

    return pl.pallas_call(
        body,
        name=name,
        grid=(hl, r // tm, c // tc),
        in_specs=[pl.BlockSpec((None, None, tm, tc), lambda l, i, j: (2 * lax.axis_index("x") + lax.axis_index("y"), l, i, j)),
                  pl.BlockSpec((3, None, tm, tc), lambda l, i, j: (0, l, i, j))],
        out_specs=pl.BlockSpec((None, tm, tc), lambda l, i, j: (lax.axis_index("c") * hl + l, i, j)),
        out_shape=jax.ShapeDtypeStruct((nl, r, c), F32),
        compiler_params=_params(("parallel",) * 3),
    )(cs, got)


def all_gather_xy(shard, name):
    r, c = shard.shape
    hr = r // 2
    assert r % 32 == 0

    def body(x_ref, out_ref, send_sems, recv_sems, local_sem):
        x, y, cc = _place()
        chips = _other_chips(x, y)
        mine = pl.ds(pl.multiple_of(cc * hr, 16), hr)
        theirs = pl.ds(pl.multiple_of((1 - cc) * hr, 16), hr)
        k_me = 2 * x + y

        def copy(k, src, dst, to):
            return pltpu.make_async_remote_copy(src_ref=src, dst_ref=dst, send_sem=send_sems.at[k], recv_sem=recv_sems.at[k],
                                                device_id=to, device_id_type=MESH)

        own = pltpu.make_async_copy(x_ref, out_ref.at[k_me], local_sem)
        own.start()
        first = [copy(j, x_ref.at[mine], out_ref.at[k_me, mine], (*chip, cc)) for j, chip in enumerate(chips)]
        for cp in first:
            cp.start()
        passed = []
        for j, (px, py) in enumerate(chips):
            landed = out_ref.at[2 * px + py, mine]
            copy(j, landed, landed, (px, py, cc)).wait_recv()
            fw = copy(3 + j, landed, landed, (x, y, 1 - cc))
            fw.start()
            passed.append(fw)
        for j, (px, py) in enumerate(chips):
            landed = out_ref.at[2 * px + py, theirs]
            copy(3 + j, landed, landed, (x, y, 1 - cc)).wait_recv()
        for cp in first + passed:
            cp.wait_send()
        own.wait()

    return pl.pallas_call(
        body,
        name=name,
        in_specs=[ANY],
        out_specs=ANY,
        out_shape=jax.ShapeDtypeStruct((NCHIP, r, c), shard.dtype),
        scratch_shapes=[pltpu.SemaphoreType.DMA((6,)), pltpu.SemaphoreType.DMA((6,)), pltpu.SemaphoreType.DMA],
    )(shard)


def all_gather_8(block, name):
    m, c = block.shape

    def body(x_ref, out_ref, send_sems, recv_sems, local_sem):
        x, y, cc = _place()
        me, sibling = (x, y, cc), (x, y, 1 - cc)
        chips = _other_chips(x, y)

        def rows(px, py, pc):
            return out_ref.at[4 * px + 2 * py + pc]

        def copy(k, blk, to, src=None):
            return pltpu.make_async_remote_copy(src_ref=rows(*blk) if src is None else src, dst_ref=rows(*blk), send_sem=send_sems.at[k],
                                                recv_sem=recv_sems.at[k], device_id=to, device_id_type=MESH)

        mine = pltpu.make_async_copy(x_ref, rows(*me), local_sem)
        mine.start()
        first = [copy(0, me, sibling, src=x_ref)]
        first += [copy(1 + j, me, (*chip, cc), src=x_ref) for j, chip in enumerate(chips)]
        for cp in first:
            cp.start()
        passed = [copy(4 + j, (*chip, cc), sibling) for j, chip in enumerate(chips)]
        for j, chip in enumerate(chips):
            copy(1 + j, (*chip, cc), me).wait_recv()
            passed[j].start()
        copy(0, sibling, me).wait_recv()
        for j, chip in enumerate(chips):
            copy(4 + j, (*chip, 1 - cc), me).wait_recv()
        for cp in first + passed:
            cp.wait_send()
        mine.wait()

    return pl.pallas_call(
        body,
        name=name,
        in_specs=[pl.BlockSpec(memory_space=pltpu.VMEM)],
        out_specs=pl.BlockSpec(memory_space=pltpu.VMEM),
        out_shape=jax.ShapeDtypeStruct((8, m, c), block.dtype),
        scratch_shapes=[pltpu.SemaphoreType.DMA((7,)), pltpu.SemaphoreType.DMA((7,)), pltpu.SemaphoreType.DMA],
        compiler_params=pltpu.CompilerParams(vmem_limit_bytes=VMEM_LIMIT),
    )(block)


def add_parts(parts, out_dtype, name, tm=512):
    def fn(i, nt, rows, vecs, prevs, nexts):
        acc = rows[0]
        for r_ in rows[1:]:
            acc = acc + r_
        return [acc], []
    r, c = parts[0].shape
    return ew(fn, list(parts), [], [(c, out_dtype)], tm=_tile(r, tm, 16), name=name)[0]


SMALL_SHARDED = ("sc_conv_w", "m_conv_w")
SMALL_REPL = ("ffn1_norm", "mix_norm", "m_conv_b", "m_dt_bias", "m_A_log", "m_D", "m_norm", "ffn2_norm", "ple_norm", "final_norm")
BIG = ("ffn1_wg", "ffn1_wu", "ffn1_wd", "w_in", "sc_w_out", "m_w_out", "w_o", "ffn2_wg", "ffn2_wu", "ffn2_wd", "ple_w_gate", "ple_w_proj")
TRANSPOSED = ("ffn1_wg", "ffn1_wu", "ffn2_wg", "ffn2_wu", "w_in")
ORDER = ("ffn1_norm", "ffn1_wg", "ffn1_wu", "ffn1_wd", "mix_norm", "w_in", "sc_conv_w", "sc_w_out", "m_conv_w", "m_conv_b", "m_dt_bias",
         "m_A_log", "m_D", "m_norm", "m_w_out", "w_o", "ffn2_norm", "ffn2_wg", "ffn2_wu", "ffn2_wd", "ple_norm", "ple_w_gate", "ple_w_proj",
         "final_norm")


def _pack(arrs, cols, row_mult):
    flat = jnp.concatenate([a.reshape(-1) for a in arrs])
    n = flat.shape[0]
    rows = -(-n // cols)
    rows = -(-rows // row_mult) * row_mult
    return jnp.pad(flat, (0, rows * cols - n)).reshape(rows, cols)


def _unpack(flat2d, shapes):
    flat = flat2d.reshape(-1)
    out, off = [], 0
    for s in shapes:
        n = int(np.prod(s))
        out.append(flat[off:off + n].reshape(s))
        off += n
    return out


def _row_cat(arrs, dtype):
    return jnp.concatenate([a.astype(dtype) for a in arrs], axis=1)


def kernel(x, p, ffn1_norm, ffn1_wg, ffn1_wu, ffn1_wd, mix_norm, w_in, sc_conv_w, sc_w_out, m_conv_w, m_conv_b, m_dt_bias, m_A_log, m_D, m_norm, m_w_out, w_o, ffn2_norm, ffn2_wg, ffn2_wu, ffn2_wd, ple_norm, ple_w_gate, ple_w_proj, final_norm, loss_target, m_ffn1_norm, m_ffn1_wg, m_ffn1_wu, m_ffn1_wd, m_mix_norm, m_w_in, m_sc_conv_w, m_sc_w_out, m_m_conv_w, m_m_conv_b, m_m_dt_bias, m_m_A_log, m_m_D, m_m_norm, m_m_w_out, m_w_o, m_ffn2_norm, m_ffn2_wg, m_ffn2_wu, m_ffn2_wd, m_ple_norm, m_ple_w_gate, m_ple_w_proj, m_final_norm, v_ffn1_norm, v_ffn1_wg, v_ffn1_wu, v_ffn1_wd, v_mix_norm, v_w_in, v_sc_conv_w, v_sc_w_out, v_m_conv_w, v_m_conv_b, v_m_dt_bias, v_m_A_log, v_m_D, v_m_norm, v_m_w_out, v_w_o, v_ffn2_norm, v_ffn2_wg, v_ffn2_wu, v_ffn2_wd, v_ple_norm, v_ple_w_gate, v_ple_w_proj, v_final_norm):
    args = dict(locals())
    wts = {n: args[n] for n in ORDER}
    mom = {n: args["m_" + n] for n in ORDER}
    vel = {n: args["v_" + n] for n in ORDER}

    depth = ffn1_norm.shape[0]
    d = x.shape[-1]
    w = 2 * d
    hh = w // SSM_P
    cw = w + 2 * SSM_G * SSM_N
    d4 = d // NCHIP
    pp = 7 * d + cw + LANE
    my_x, my_y, my_c = _place()
    k_me = 2 * my_x + my_y

    tr = lambda a: jnp.swapaxes(a, 1, 2)
    gu_t = [_row_cat([tr(wg_), tr(wu_)], BF) for wg_, wu_ in ((ffn1_wg, ffn1_wu), (ffn2_wg, ffn2_wu))]
    wd_l = [ffn1_wd.astype(BF), ffn2_wd.astype(BF)]
    w1024_l = _row_cat([m_w_out, sc_w_out, w_o, ple_w_gate], BF)
    p4 = w_in.shape[2]
    p4p = -(-p4 // 32) * 32
    win_l, wpp_l = jnp.pad(tr(w_in).astype(BF), ((0, 0), (0, p4p - p4), (0, 0))), ple_w_proj.astype(BF)
    halves = lambda a: a.reshape(2, a.shape[0] // 2, a.shape[1])
    whole = lambda g: g.reshape(NCHIP, g.shape[2] * 2, g.shape[3])

    def pieces(l):
        return {"small": [halves(w1024_l[l]), halves(wpp_l[l])], "win": [halves(win_l[l])], "gu1": [halves(gu_t[0][l])], "d1": [halves(wd_l[0][l])],
                "gu2": [halves(gu_t[1][l])], "d2": [halves(wd_l[1][l])]}

    small_local = [sc_conv_w, m_conv_w]
    gathered_s = all_gather_xy(_pack(small_local, LANE, 32), "gather_conv_weights")
    per_shard_s = [_unpack(gathered_s[k], [a.shape for a in small_local]) for k in range(NCHIP)]
    sc_conv_full = jnp.concatenate([per_shard_s[k][0] for k in range(NCHIP)], axis=2)
    m_conv_full = jnp.concatenate([per_shard_s[k][1] for k in range(NCHIP)], axis=2)

    pad_h = lambda a: jnp.pad(a, ((0, 0), (0, LANE - hh)))
    dt_bias_p, a_log_p = pad_h(m_dt_bias), pad_h(m_A_log)
    d_exp = jnp.repeat(m_D, SSM_P, axis=1)
    e_mat = (jnp.arange(w)[None, :] // SSM_P == jnp.arange(LANE)[:, None]).astype(F32)
    et_mat = e_mat.T
    o_z, o_xbc, o_dt, o_g = 3 * d, 5 * d, 5 * d + cw, 5 * d + cw + hh

    def layer_weights(got):
        wt = {"w1024": whole(got["small"][0]), "wpp": whole(got["small"][1])}
        wt.update({k: whole(got[k][0]) for k in ("gu1", "d1", "gu2", "d2")})
        gw = whole(got["win"][0])

        def wi(lo, hi):
            parts = [gw[k, max(lo - k * p4, 0):min(hi - k * p4, p4)] for k in range(NCHIP) if lo < (k + 1) * p4 and hi > k * p4]
            return parts[0] if len(parts) == 1 else jnp.concatenate(parts, axis=0)

        wt["sc3"], wt["z"], wt["xbc"], wt["g2"] = wi(0, o_z), wi(o_z, o_xbc), wi(o_xbc, o_dt), wi(o_g, o_g + 2 * d)
        wt["dt"] = jnp.pad(wi(o_dt, o_g), ((0, LANE - hh), (0, 0)))
        wt["in_p"] = jnp.concatenate([wt["g2"], wt["z"], wt["sc3"][d:], wt["xbc"], wt["sc3"][:d], wt["dt"]], axis=0)
        return wt

    first = pieces(0)
    order = ("gu1", "d1", "win", "small", "gu2", "d2")
    flat = gather_rider([a for k in order for a in first[k]]).standalone("gather_weights")
    got, pos = {}, 0
    for k in order:
        got[k] = flat[pos:pos + len(first[k])]
        pos += len(first[k])
    wts_l = [layer_weights(got)]

    h = x[0]
    saved = []
    for i in range(depth):
        s, wt = {}, wts_l[i]
        nxt = pieces(i + 1) if i + 1 < depth else None
        ride = lambda k: gather_rider(nxt[k]) if nxt else None
        got = {}
        s["h0"] = h
        (s["ab1"], s4, s["n1"]), got["small"] = ffn_up(h, ffn1_norm[i:i + 1], wt["gu1"], rider=ride("small"))
        h, got["d1"] = ffn_down(s4, wt["d1"], h, rider=ride("d1"))
        s["h1"] = h
        u = norm_cast(h, mix_norm[i:i + 1])
        s["u"] = u
        s["sc3"] = mm(u, wt["sc3"], tb=True, out_dtype=BF, name="proj_sc")
        s["z"] = mm(u, wt["z"], tb=True, out_dtype=BF, name="proj_z")
        s["xbc_raw"] = mm(u, wt["xbc"], tb=True, out_dtype=BF, name="proj_xbc")
        s["gates"] = mm(u, wt["g2"], tb=True, out_dtype=BF, name="proj_gates")
        s["dt_raw"] = mm(u, wt["dt"], tb=True, name="proj_dt")
        s["ya_in"] = conv_a_fwd(s["sc3"], sc_conv_full[i])
        s["xbc"], s["dt"] = conv_m_fwd(s["xbc_raw"], s["dt_raw"], m_conv_full[i], m_conv_b[i:i + 1], dt_bias_p[i:i + 1])
        (s["yn"], s["y"], s["sprev"]), got["win"] = ssd_fwd(s["xbc"], s["dt"], s["z"], a_log_p[i:i + 1], d_exp[i:i + 1], m_norm[i:i + 1], e_mat,
                                                            rider=ride("win"))
        (h, s["y_a"], s["y_m"], s["merged"]), got["gu2"] = mix_out_fwd(s["ya_in"], s["yn"], s["gates"], h, wt["w1024"], rider=ride("gu2"))
        s["h2"] = h
        (s["ab2"], s4, s["n2"]), got["gu1"] = ffn_up(h, ffn2_norm[i:i + 1], wt["gu2"], rider=ride("gu1"))
        h, got["d2"] = ffn_down(s4, wt["d2"], h, rider=ride("d2"))
        s["h3"] = h
        h = ple_fwd(h, ple_norm[i:i + 1], p[i, 0], wt["w1024"], wt["wpp"])
        saved.append(s)
        if nxt:
            wts_l.append(layer_weights(got))

    dh, loss_lanes, g_final = loss_bwd(h, final_norm[None, :], loss_target[0])
    loss = lax.psum(jnp.sum(loss_lanes), ("x", "y", "c"))

    def finish_reduce(cs, got):
        halves = [add_chips(c_, g_, 2, "grad_add_chips") for c_, g_ in zip(cs, got, strict=True)]
        return [f.reshape(-1, f.shape[2]) for f in join_packs(halves, "grad_join_halves")]

    g_layer, pending, reduced = None, None, [None] * depth
    gs = {n: [None] * depth for n in SMALL_SHARDED + SMALL_REPL if n != "final_norm"}
    for i in reversed(range(depth)):
        s = saved[i]
        wt = wts_l[i]
        dh, gs["ple_norm"][i], n3, dgp, dpe = ple_bwd(dh, s["h3"], ple_norm[i:i + 1], p[i, 0], wt["w1024"], wt["wpp"])
        g_pg = mm(n3, dgp, ta=True, out_dtype=BF, name="g_ple_gate", tm_cap=512, tn_cap=512)
        g_pp = mm(p[i, 0], dpe, ta=True, out_dtype=BF, name="g_ple_proj", tm_cap=512, tn_cap=512)
        g_pp = jnp.transpose(g_pp.reshape(g_pp.shape[0], NCHIP, d4), (1, 0, 2))
        (dn2, s2, dab2, do2), from_sibling = ffn_bwd(dh, s["ab2"], wt["gu2"], wt["d2"], rider=swap_rider(g_layer) if g_layer else None)
        if g_layer:
            pending = [add_sibling(g, r_, "grad_add_sibling") for g, r_ in zip(g_layer, from_sibling, strict=True)]
        g_ffn2 = ffn_wgrads(s["n2"], do2, s2, dab2)
        dh, gs["ffn2_norm"][i] = norm_bwd_add(dh, s["h2"], ffn2_norm[i:i + 1], dn2)
        dproj, dya, dyn, dy_a, dy_m = mix_out_bwd(dh, s["gates"], s["y_a"], s["y_m"], wt["w1024"], pp)
        g_wo = mm(s["merged"], dh, ta=True, out_dtype=BF, name="g_w_o", tm_cap=512, tn_cap=512)
        g_sco = mm(s["ya_in"], dy_a, ta=True, out_dtype=BF, name="g_sc_out", tm_cap=512, tn_cap=512)
        g_mo = mm(s["yn"], dy_m, ta=True, out_dtype=BF, name="g_m_out", tm_cap=512, tn_cap=512)
        g_1024 = jnp.concatenate([g_mo.reshape(NCHIP, 2 * d4, d), g_sco.reshape(NCHIP, d4, d), g_wo.reshape(NCHIP, d4, d),
                                  g_pg.reshape(NCHIP, d4, d)], axis=1)
        (dproj, dxbc, ddt, gs["m_norm"][i], gd, gal), got_a = ssd_bwd(dyn, s["y"], s["z"], s["xbc"], s["dt"], s["sprev"], a_log_p[i:i + 1], d_exp[i:i + 1],
                                                                      m_norm[i:i + 1], e_mat, et_mat, Window(0, 1, pp, dproj),
                                                                      rider=scatter_rider(pending[:1]) if pending else None)
        gs["m_D"][i], gs["m_A_log"][i] = gd[:, :hh], gal[:, :hh]
        dpre, dproj, gdb = conv_m_bwd1(dxbc, s["xbc_raw"], ddt, s["dt_raw"], m_conv_full[i], m_conv_b[i:i + 1], dt_bias_p[i:i + 1],
                                       Window(1, (7 * d + cw) // LANE, pp, dproj))
        gs["m_dt_bias"][i] = gdb[:, :hh]
        dproj, gs["m_conv_w"][i], gs["m_conv_b"][i] = conv_bwd2(dpre, s["xbc_raw"], m_conv_full[i], "conv_m_bwd2", Window(0, 6 * d // cw, pp, dproj))
        dcv, dproj, v = conv_a_bwd1(dya, s["sc3"], sc_conv_full[i], Window(1, (6 * d + cw) // d, pp, dproj))
        dproj, gs["sc_conv_w"][i] = conv_a_bwd2(dcv, v, s["sc3"], sc_conv_full[i], Window(0, 2, pp, dproj))
        if pending:
            du, got_b = mm(dproj, wt["in_p"], name="d_proj_in", rider=scatter_rider(pending[2:3]))
            gwp, got_c = mm(dproj, s["u"], ta=True, out_dtype=BF, name="g_w_in", tm_cap=1152, tn_cap=512, rider=scatter_rider(pending[1:2] + pending[3:]))
            reduced[i + 1] = finish_reduce(pending, [got_a[0], got_c[0], got_b[0], got_c[1]])
        else:
            du = mm(dproj, wt["in_p"], name="d_proj_in")
            gwp = mm(dproj, s["u"], ta=True, out_dtype=BF, name="g_w_in", tm_cap=1152, tn_cap=512)
        gw_rows = jnp.concatenate([gwp[6 * d + cw:7 * d + cw], gwp[4 * d:6 * d], gwp[2 * d:4 * d], gwp[6 * d:6 * d + cw], gwp[7 * d + cw:7 * d + cw + hh],
                                   gwp[:2 * d]], axis=0)
        g_in = jnp.pad(gw_rows.reshape(NCHIP, p4, d), ((0, 0), (0, p4p - p4), (0, 0)))
        dh, gs["mix_norm"][i] = norm_bwd_add(dh, s["h1"], mix_norm[i:i + 1], du)
        (dn1, s1, dab1, do1), _ = ffn_bwd(dh, s["ab1"], wt["gu1"], wt["d1"])
        g_ffn1 = ffn_wgrads(s["n1"], do1, s1, dab1)
        dh, gs["ffn1_norm"][i] = norm_bwd_add(dh, s["h0"], ffn1_norm[i:i + 1], dn1)
        g_layer = [jnp.concatenate([g_ffn1, g_ffn2], axis=1), g_1024, g_in, g_pp]
        g_layer = [g.reshape(NCHIP, 2, g.shape[1] // 2, g.shape[2]) for g in g_layer]
    from_sibling = swap_rider(g_layer).standalone("grad_swap_halves")
    pending = [add_sibling(g, r_, "grad_add_sibling") for g, r_ in zip(g_layer, from_sibling, strict=True)]
    reduced[0] = finish_reduce(pending, scatter_packs(pending, "grad_scatter"))
    grad_x = dh[None]

    f4 = reduced[0][0].shape[0] // 6
    rows_of = lambda j, lo, hi: jnp.stack([reduced[l][j][lo:hi] for l in range(depth)])
    ffn_rows = lambda j: rows_of(0, j * f4, (j + 1) * f4)
    grads = {
        "ffn1_wg": ffn_rows(0), "ffn1_wu": ffn_rows(1), "ffn1_wd": ffn_rows(2), "ffn2_wg": ffn_rows(3), "ffn2_wu": ffn_rows(4), "ffn2_wd": ffn_rows(5),
        "m_w_out": rows_of(1, 0, 2 * d4), "sc_w_out": rows_of(1, 2 * d4, 3 * d4), "w_o": rows_of(1, 3 * d4, 4 * d4), "ple_w_gate": rows_of(1, 4 * d4, 5 * d4),
        "w_in": rows_of(2, 0, p4), "ple_w_proj": rows_of(3, 0, None),
    }

    small_names = list(SMALL_SHARDED + SMALL_REPL)
    small_full = [g_final[0] if n == "final_norm" else jnp.stack(gs[n]) for n in small_names]
    small_pack = _pack(small_full, LANE, HALO)
    all8 = all_gather_8(small_pack, "gather_small_grads")
    small_sum = add_parts([all8[k] for k in range(8)], F32, "add_small_grads", tm=256)
    for n, tot in zip(small_names, _unpack(small_sum, [a.shape for a in small_full]), strict=True):
        if n in SMALL_SHARDED:
            cl = wts[n].shape[2]
            grads[n] = lax.dynamic_slice_in_dim(tot, k_me * cl, cl, axis=2)
        else:
            grads[n] = tot.reshape(wts[n].shape)

    delta, new_m, new_v = {}, {}, {}
    for n in BIG:
        view = tr if n in TRANSPOSED else (lambda a: a)
        shp = grads[n].shape
        two = lambda a: a.reshape(-1, shp[-1])
        dl, nm, nv = adamw(two(view(wts[n])), two(grads[n]), two(view(mom[n])), two(view(vel[n])), "adamw_" + "x".join(map(str, shp[1:])))
        grads[n], delta[n], new_m[n], new_v[n] = view(grads[n]), view(dl.reshape(shp)), view(nm.reshape(shp)), view(nv.reshape(shp))
    for n in small_names:
        shp = wts[n].shape
        two = lambda a: a.reshape(-1, shp[-1])
        dl, nm, nv = adamw(two(wts[n]), two(grads[n]), two(mom[n]), two(vel[n]), "adamw_small_" + "x".join(map(str, shp)))
        delta[n], new_m[n], new_v[n] = dl.reshape(shp), nm.reshape(shp), nv.reshape(shp)

    return (loss, grad_x, *[grads[n] for n in ORDER], *[delta[n] for n in ORDER], *[new_m[n] for n in ORDER], *[new_v[n] for n in ORDER])
```

```python
import jax
import jax.numpy as jnp
import numpy as np
from jax import lax
from jax.experimental import pallas as pl
from jax.experimental.pallas import tpu as pltpu

BF = jnp.bfloat16
F32 = jnp.float32
EPS = 1e-6
LANE = 128
HALO = 8
SSM_P = 64
SSM_N = 128
SSM_G = 4
SSM_L = 128
ADAM_LR, ADAM_B1, ADAM_B2, ADAM_EPS, ADAM_WD, ADAM_STEP = 0.001, 0.9, 0.999, 1e-08, 0.01, 10
VMEM_LIMIT = 56 * 1024 * 1024
TILE_ELEMS = 600_000
ADD_TILE_ELEMS = 1_000_000
NCHIP = 4
FFN_SUB = 256
MESH = pl.DeviceIdType.MESH


def _tile(n, cap, mult=LANE):
    best = None
    t = mult
    while t <= min(n, cap):
        if n % t == 0:
            best = t
        t += mult
    return best if best is not None else n


def _row_tile(r, c, mult=16, elems=TILE_ELEMS):
    return _tile(r, max(mult, elems // c // mult * mult), mult)


def _tile2(r, c, mult=16, elems=ADD_TILE_ELEMS):
    tm = _row_tile(r, c, mult, elems)
    tc = c if tm * c <= elems else _tile(c, max(LANE, elems // tm // LANE * LANE))
    return tm, tc


def _params(sem):
    return pltpu.CompilerParams(dimension_semantics=sem, vmem_limit_bytes=VMEM_LIMIT)


def _sigmoid(x):
    return 1.0 / (1.0 + jnp.exp(-x))


def _dot(a, b, ca=1, cb=0, precision=None):
    return lax.dot_general(a, b, (((ca,), (cb,)), ((), ())), precision=precision, preferred_element_type=F32)


def _rms(x, g):
    r = lax.rsqrt(jnp.mean(x * x, axis=-1, keepdims=True) + EPS)
    return x * r * g


def _rms_bwd(x, g, dy):
    r = lax.rsqrt(jnp.mean(x * x, axis=-1, keepdims=True) + EPS)
    xh = x * r
    dxh = dy * g
    dx = r * (dxh - xh * jnp.mean(dxh * xh, axis=-1, keepdims=True))
    return dx, jnp.sum(dy * xh, axis=0, keepdims=True)


def _accumulate(ref, val, first):
    @pl.when(first)
    def _():
        ref[...] = val

    @pl.when(jnp.logical_not(first))
    def _():
        ref[...] += val


RIDER_MID = 1.0


class Rider:
    def __init__(self, ins, out_shapes, n_sems, start, finish, mid=None):
        self.ins, self.out_shapes, self.n_sems, self.start, self.mid, self.finish = list(ins), list(out_shapes), n_sems, start, mid, finish

    def standalone(self, name):
        ni, no = len(self.ins), len(self.out_shapes)

        def body(*refs):
            parts = (refs[:ni], refs[ni:ni + no], *refs[ni + no:])
            self.start(*parts)
            if self.mid is not None:
                self.mid(*parts)
            self.finish(*parts)

        return _comm_call(body, name, self.ins, self.out_shapes, self.n_sems)


def host_call(body, *, name, grid, in_specs, out_specs, out_shape, scratch_shapes, operands, rider=None, aliases=None):
    n_in, n_out = len(in_specs), len(out_specs)
    aliases = aliases or {}
    if rider is None:
        outs = pl.pallas_call(body, name=name, grid=grid, in_specs=in_specs, out_specs=out_specs, out_shape=out_shape, scratch_shapes=scratch_shapes,
                              input_output_aliases=aliases, compiler_params=_params(("arbitrary",) * len(grid)))(*operands)
        return list(outs), []
    ri, ro = len(rider.ins), len(rider.out_shapes)

    def hosted(*refs):
        ins, r_ins = refs[:n_in], refs[n_in:n_in + ri]
        outs, r_outs = refs[n_in + ri:n_in + ri + n_out], refs[n_in + ri + n_out:n_in + ri + n_out + ro]
        scratch, (send_sems, recv_sems) = refs[n_in + ri + n_out + ro:-2], refs[-2:]
        step, total = 0, 1
        for ax, n in enumerate(grid):
            step = step * n + pl.program_id(ax)
            total *= n
        parts = (r_ins, r_outs, send_sems, recv_sems)

        @pl.when(step == 0)
        def _():
            rider.start(*parts)

        if rider.mid is not None:
            @pl.when(step == min(total - 1, int(total * RIDER_MID)))
            def _():
                rider.mid(*parts)

        body(*ins, *outs, *scratch)

        @pl.when(step == total - 1)
        def _():
            rider.finish(*parts)

    outs = pl.pallas_call(
        hosted,
        name=name,
        grid=grid,
        in_specs=list(in_specs) + [ANY] * ri,
        out_specs=list(out_specs) + [ANY] * ro,
        out_shape=list(out_shape) + rider.out_shapes,
        scratch_shapes=list(scratch_shapes) + [pltpu.SemaphoreType.DMA((rider.n_sems,)), pltpu.SemaphoreType.DMA((rider.n_sems,))],
        input_output_aliases=aliases,
        compiler_params=_params(("arbitrary",) * len(grid)),
    )(*operands, *rider.ins)
    return list(outs[:n_out]), list(outs[n_out:])


def mmx(name, a, b, *, grid, a_spec, b_spec, o_spec, o_shape, o_dtype, ca, cb, acc_shape=None, rider=None):
    nk = grid[-1] if acc_shape is not None else 1

    def body(a_ref, b_ref, o_ref, *acc):
        p = _dot(a_ref[...].astype(BF), b_ref[...].astype(BF), ca, cb)
        if nk == 1:
            o_ref[...] = p.astype(o_ref.dtype)
        else:
            kk = pl.program_id(len(grid) - 1)
            _accumulate(acc[0], p, kk == 0)

            @pl.when(kk == nk - 1)
            def _():
                o_ref[...] = acc[0][...].astype(o_ref.dtype)

    if rider is not None:
        (out,), r_outs = host_call(body, name=name, grid=grid, in_specs=[a_spec, b_spec], out_specs=[o_spec], out_shape=[jax.ShapeDtypeStruct(o_shape, o_dtype)],
                                   scratch_shapes=[pltpu.VMEM(acc_shape, F32)] if nk > 1 else [], operands=(a, b), rider=rider)
        return out, r_outs
    sem = ("parallel",) * (len(grid) - 1) + ("arbitrary" if nk > 1 else "parallel",)
    return pl.pallas_call(
        body,
        name=name,
        grid=grid,
        in_specs=[a_spec, b_spec],
        out_specs=o_spec,
        out_shape=jax.ShapeDtypeStruct(o_shape, o_dtype),
        scratch_shapes=[pltpu.VMEM(acc_shape, F32)] if nk > 1 else [],
        compiler_params=_params(sem),
    )(a, b)


def mm(a, b, *, ta=False, tb=False, out_dtype=F32, name, tm_cap=1024, tn_cap=1024, tk_cap=4096, rider=None):
    m, k = (a.shape[1], a.shape[0]) if ta else a.shape
    n = b.shape[0] if tb else b.shape[1]
    assert (b.shape[1] if tb else b.shape[0]) == k
    tm, tn, tk = _tile(m, tm_cap), _tile(n, tn_cap), _tile(k, tk_cap)
    nk = k // tk
    a_spec = pl.BlockSpec((tk, tm), lambda i, j, kk: (kk, i)) if ta else pl.BlockSpec((tm, tk), lambda i, j, kk: (i, kk))
    b_spec = pl.BlockSpec((tn, tk), lambda i, j, kk: (j, kk)) if tb else pl.BlockSpec((tk, tn), lambda i, j, kk: (kk, j))
    return mmx(name, a, b, grid=(m // tm, n // tn, nk), a_spec=a_spec, b_spec=b_spec, o_spec=pl.BlockSpec((tm, tn), lambda i, j, kk: (i, j)),
               o_shape=(m, n), o_dtype=out_dtype, ca=0 if ta else 1, cb=1 if tb else 0, acc_shape=(tm, tn) if nk > 1 else None, rider=rider)


class Window:
    def __init__(self, out, block, cols, buf=None):
        self.out, self.block, self.cols, self.buf = out, block, cols, buf


def ew(fn, rows, vecs, out_rows, out_red=(), *, tm, name, prev_halo=(), next_halo=(), window=None):
    t = rows[0].shape[0]
    tm = min(tm, t)
    nt = t // tm
    assert t % tm == 0 and (tm % HALO == 0 or (tm == t and not prev_halo and not next_halo))
    nr, nv, npv, nnx, nor = len(rows), len(vecs), len(prev_halo), len(next_halo), len(out_rows)
    hb = tm // HALO
    n_in = nr + nv + npv + nnx
    passed = window is not None and window.buf is not None

    def body(*refs):
        i = pl.program_id(0)
        ins = [r[...].astype(F32) for r in refs[:n_in]]
        outs = refs[n_in + passed:]
        o_rows, o_red = fn(i, nt, ins[:nr], ins[nr:nr + nv], ins[nr + nv:nr + nv + npv], ins[nr + nv + npv:])
        for ref, val in zip(outs[:nor], o_rows, strict=True):
            ref[...] = val.astype(ref.dtype)
        for ref, val in zip(outs[nor:], o_red, strict=True):
            _accumulate(ref, val, i == 0)

    in_specs = [pl.BlockSpec((tm, r.shape[1]), lambda i: (i, 0)) for r in rows]
    in_specs += [pl.BlockSpec(v.shape, lambda i: (0, 0)) for v in vecs]
    in_specs += [pl.BlockSpec((HALO, rows[k].shape[1]), lambda i: (jnp.maximum(i * hb - 1, 0), 0)) for k in prev_halo]
    in_specs += [pl.BlockSpec((HALO, rows[k].shape[1]), lambda i: (jnp.minimum((i + 1) * hb, t // HALO - 1), 0)) for k in next_halo]
    out_specs = [pl.BlockSpec((tm, c), lambda i: (i, 0)) for c, _ in out_rows]
    out_specs += [pl.BlockSpec(s, lambda i: (0, 0)) for s in out_red]
    out_shape = [jax.ShapeDtypeStruct((t, c), d) for c, d in out_rows] + [jax.ShapeDtypeStruct(s, F32) for s in out_red]
    operands = [*rows, *vecs, *[rows[k] for k in prev_halo], *[rows[k] for k in next_halo]]
    aliases = {}
    if window is not None:
        c, dt_ = out_rows[window.out]
        out_specs[window.out] = pl.BlockSpec((tm, c), lambda i: (i, window.block))
        out_shape[window.out] = jax.ShapeDtypeStruct((t, window.cols), dt_)
        if passed:
            in_specs.append(ANY)
            operands.append(window.buf)
            aliases = {n_in: window.out}
    return pl.pallas_call(
        body,
        name=name,
        grid=(nt,),
        in_specs=in_specs,
        out_specs=out_specs,
        out_shape=out_shape,
        input_output_aliases=aliases,
        compiler_params=_params(("arbitrary",) if out_red else ("parallel",)),
    )(*operands)


def _shift_down(x, prev, j):
    if j == 0:
        return x
    r = pltpu.roll(x, j, 0)
    rh = pltpu.roll(prev, j, 0)
    row = lax.broadcasted_iota(jnp.int32, (HALO, x.shape[1]), 0)
    head = jnp.where(row < j, rh, r[:HALO])
    return jnp.concatenate([head, r[HALO:]], axis=0)


def _shift_up(x, nxt, j):
    if j == 0:
        return x
    n = x.shape[0]
    r = pltpu.roll(x, n - j, 0)
    rh = pltpu.roll(nxt, HALO - j, 0)
    row = lax.broadcasted_iota(jnp.int32, (HALO, x.shape[1]), 0)
    tail = jnp.where(row >= HALO - j, rh, r[n - HALO:])
    return jnp.concatenate([r[: n - HALO], tail], axis=0)


def _conv_fwd(x, prev, w):
    kk = w.shape[0]
    acc = None
    for k in range(kk):
        term = w[k:k + 1, :] * _shift_down(x, prev, kk - 1 - k)
        acc = term if acc is None else acc + term
    return acc


def ffn_up(h, g, wf, rider=None):
    t, d = h.shape
    f4 = wf.shape[1] // 2
    tm = _tile(t, 1024)
    sub = _tile(tm, FFN_SUB, 16)

    def body(h_ref, g_ref, wg_ref, wu_ref, ab_ref, s_ref, n_ref):
        @pl.when(pl.program_id(1) == 0)
        def _():
            n_ref[...] = _rms(h_ref[...], g_ref[...]).astype(BF)

        for r in range(tm // sub):
            rows = slice(r * sub, (r + 1) * sub)
            n = n_ref[rows, :]
            a = _dot(n, wg_ref[...], 1, 1)
            b = _dot(n, wu_ref[...], 1, 1)
            ab_ref[0, rows, :] = a.astype(BF)
            ab_ref[1, rows, :] = b.astype(BF)
            s_ref[rows, :] = (a * _sigmoid(a) * b).astype(BF)

    wspec = lambda ib: pl.BlockSpec((None, f4, d), lambda i, j: (j, ib, 0))
    return host_call(
        body,
        name="ffn_up",
        grid=(t // tm, NCHIP),
        in_specs=[pl.BlockSpec((tm, d), lambda i, j: (i, 0)), pl.BlockSpec((1, d), lambda i, j: (0, 0)), wspec(0), wspec(1)],
        out_specs=[pl.BlockSpec((2, None, tm, f4), lambda i, j: (0, j, i, 0)), pl.BlockSpec((None, tm, f4), lambda i, j: (j, i, 0)),
                   pl.BlockSpec((tm, d), lambda i, j: (i, 0))],
        out_shape=[jax.ShapeDtypeStruct((2, NCHIP, t, f4), BF), jax.ShapeDtypeStruct((NCHIP, t, f4), BF), jax.ShapeDtypeStruct((t, d), BF)],
        scratch_shapes=[],
        operands=(h, g, wf, wf),
        rider=rider,
    )


def ffn_down(s4, wf, h, rider=None):
    t, d = h.shape
    f4 = s4.shape[2]
    tm = _tile(t, 512)

    def body(s_ref, w_ref, h_ref, o_ref):
        acc = _dot(s_ref[0], w_ref[0])
        for k in range(1, NCHIP):
            acc = acc + _dot(s_ref[k], w_ref[k])
        o_ref[...] = h_ref[...] + 0.5 * acc

    (out,), r_outs = host_call(
        body,
        name="ffn_down",
        grid=(t // tm,),
        in_specs=[pl.BlockSpec((NCHIP, tm, f4), lambda i: (0, i, 0)), pl.BlockSpec((NCHIP, f4, d), lambda i: (0, 0, 0)), pl.BlockSpec((tm, d), lambda i: (i, 0))],
        out_specs=[pl.BlockSpec((tm, d), lambda i: (i, 0))],
        out_shape=[jax.ShapeDtypeStruct((t, d), F32)],
        scratch_shapes=[],
        operands=(s4, wf, h),
        rider=rider,
    )
    return out, r_outs


def ffn_bwd(dho, ab, wf, wd, rider=None):
    t, d = dho.shape
    f4 = wd.shape[1]
    tm = _tile(t, 1024)
    sub = _tile(tm, FFN_SUB, 16)

    def body(dho_ref, ab_ref, wg_ref, wu_ref, wd_ref, dn_ref, s_ref, dab_ref, do_sc):
        j = pl.program_id(1)

        @pl.when(j == 0)
        def _():
            do_sc[...] = (0.5 * dho_ref[...]).astype(BF)
            dn_ref[...] = jnp.zeros_like(dn_ref)

        for r in range(tm // sub):
            rows = slice(r * sub, (r + 1) * sub)
            ds = _dot(do_sc[rows, :], wd_ref[...], 1, 1)
            av, bv = ab_ref[0, rows, :].astype(F32), ab_ref[1, rows, :].astype(F32)
            sig = _sigmoid(av)
            sl = av * sig
            s_ref[rows, :] = (sl * bv).astype(BF)
            da = (ds * bv * (sig * (1.0 + av * (1.0 - sig)))).astype(BF)
            db = (ds * sl).astype(BF)
            dab_ref[0, rows, :] = da
            dab_ref[1, rows, :] = db
            dn_ref[rows, :] += _dot(da, wg_ref[...]) + _dot(db, wu_ref[...])

    row = lambda c: pl.BlockSpec((tm, c), lambda i, j: (i, 0))
    wspec = lambda ib: pl.BlockSpec((None, f4, d), lambda i, j: (j, ib, 0))
    ab_spec = pl.BlockSpec((2, None, tm, f4), lambda i, j: (0, j, i, 0))
    return host_call(
        body,
        name="ffn_bwd",
        grid=(t // tm, NCHIP),
        in_specs=[row(d), ab_spec, wspec(0), wspec(1), wspec(0)],
        out_specs=[row(d), pl.BlockSpec((None, tm, f4), lambda i, j: (j, i, 0)), ab_spec, row(d)],
        out_shape=[jax.ShapeDtypeStruct((t, d), F32), jax.ShapeDtypeStruct((NCHIP, t, f4), BF), jax.ShapeDtypeStruct((2, NCHIP, t, f4), BF),
                   jax.ShapeDtypeStruct((t, d), BF)],
        scratch_shapes=[],
        operands=(dho, ab, wf, wf, wd),
        rider=rider,
    )


def ffn_wgrads(n, do, s4, dab):
    t, d = n.shape
    f4 = s4.shape[2]
    g_in = mmx("g_ffn_in", dab, n, grid=(2, NCHIP, 1), a_spec=pl.BlockSpec((None, None, t, f4), lambda wh, k, j: (wh, k, 0, 0)),
               b_spec=pl.BlockSpec((t, d), lambda wh, k, j: (0, 0)), o_spec=pl.BlockSpec((None, None, f4, d), lambda wh, k, j: (k, wh, 0, 0)),
               o_shape=(NCHIP, 2, f4, d), o_dtype=BF, ca=0, cb=0)
    g_out = mmx("g_ffn_out", s4, do, grid=(NCHIP, 1), a_spec=pl.BlockSpec((None, t, f4), lambda k, j: (k, 0, 0)),
                b_spec=pl.BlockSpec((t, d), lambda k, j: (0, 0)), o_spec=pl.BlockSpec((None, f4, d), lambda k, j: (k, 0, 0)),
                o_shape=(NCHIP, f4, d), o_dtype=BF, ca=0, cb=0)
    return jnp.concatenate([g_in.reshape(NCHIP, 2 * f4, d), g_out], axis=1)


def norm_cast(h, g):
    def fn(i, nt, rows, vecs, prevs, nexts):
        return [_rms(rows[0], vecs[0])], []
    return ew(fn, [h], [g], [(h.shape[1], BF)], tm=512, name="norm_cast")[0]


def _zero_if(cond, x):
    return jnp.where(cond, jnp.zeros_like(x), x)


def conv_a_fwd(sc3, w_sc):
    d = sc3.shape[1] // 3

    def fn(i, nt, rows, vecs, prevs, nexts):
        x, pv = rows[0], _zero_if(i == 0, prevs[0])
        v = x[:, d:2 * d] * x[:, 2 * d:]
        vp = pv[:, d:2 * d] * pv[:, 2 * d:]
        return [x[:, :d] * _conv_fwd(v, vp, vecs[0])], []

    return ew(fn, [sc3], [w_sc], [(d, BF)], tm=256, name="conv_a_fwd", prev_halo=(0,))[0]


def _softplus(x):
    e = jnp.exp(-jnp.abs(x))
    return jnp.maximum(x, 0.0) + jnp.where(e < 1e-4, e - 0.5 * e * e, jnp.log(1.0 + e))


def conv_m_fwd(xbc_raw, dt_raw, w_mc, b_mc, dt_bias):
    def fn(i, nt, rows, vecs, prevs, nexts):
        pre = _conv_fwd(rows[0], _zero_if(i == 0, prevs[0]), vecs[0]) + vecs[1]
        return [pre * _sigmoid(pre), _softplus(rows[1] + vecs[2])], []

    return ew(fn, [xbc_raw, dt_raw], [w_mc, b_mc, dt_bias], [(xbc_raw.shape[1], F32), (LANE, F32)], tm=256, name="conv_m_fwd",
              prev_halo=(0,))


def conv_m_bwd1(dxbc, xbc_raw, ddt, dt_raw, w_mc, b_mc, dt_bias, window):
    def fn(i, nt, rows, vecs, prevs, nexts):
        pre = _conv_fwd(rows[1], _zero_if(i == 0, prevs[0]), vecs[0]) + vecs[1]
        sig = _sigmoid(pre)
        dpre = rows[0] * (sig * (1.0 + pre * (1.0 - sig)))
        ddr = rows[2] * _sigmoid(rows[3] + vecs[2])
        return [dpre, ddr], [jnp.sum(ddr, axis=0, keepdims=True)]

    return ew(fn, [dxbc, xbc_raw, ddt, dt_raw], [w_mc, b_mc, dt_bias], [(dxbc.shape[1], F32), (LANE, BF)], [(1, LANE)], tm=256,
              name="conv_m_bwd1", prev_halo=(1,), window=window)


def conv_bwd2(dpre, x, w, name, window):
    kk = w.shape[0]

    def fn(i, nt, rows, vecs, prevs, nexts):
        dp, xv = rows[0], rows[1]
        nx = _zero_if(i == nt - 1, nexts[0])
        dx = None
        dws = []
        for k in range(kk):
            up = _shift_up(dp, nx, kk - 1 - k)
            term = vecs[0][k:k + 1, :] * up
            dx = term if dx is None else dx + term
            dws.append(jnp.sum(up * xv, axis=0, keepdims=True))
        return [dx], [jnp.concatenate(dws, axis=0), jnp.sum(dp, axis=0, keepdims=True)]

    c = x.shape[1]
    return ew(fn, [dpre, x], [w], [(c, BF)], [(kk, c), (1, c)], tm=256, name=name, next_halo=(0,), window=window)


def conv_a_bwd1(dya, sc3, w_sc, window):
    d = sc3.shape[1] // 3

    def fn(i, nt, rows, vecs, prevs, nexts):
        x, pv = rows[1], _zero_if(i == 0, prevs[0])
        v = x[:, d:2 * d] * x[:, 2 * d:]
        vp = pv[:, d:2 * d] * pv[:, 2 * d:]
        return [rows[0] * x[:, :d], rows[0] * _conv_fwd(v, vp, vecs[0]), v], []

    return ew(fn, [dya, sc3], [w_sc], [(d, F32), (d, BF), (d, F32)], tm=256, name="conv_a_bwd1", prev_halo=(1,), window=window)


def conv_a_bwd2(dcv, v, sc3, w_sc, window):
    d = v.shape[1]
    kk = w_sc.shape[0]

    def fn(i, nt, rows, vecs, prevs, nexts):
        dp, vv, x = rows
        nx = _zero_if(i == nt - 1, nexts[0])
        dv = None
        dws = []
        for k in range(kk):
            up = _shift_up(dp, nx, kk - 1 - k)
            term = vecs[0][k:k + 1, :] * up
            dv = term if dv is None else dv + term
            dws.append(jnp.sum(up * vv, axis=0, keepdims=True))
        return [jnp.concatenate([dv * x[:, 2 * d:], dv * x[:, d:2 * d]], axis=1)], [jnp.concatenate(dws, axis=0)]

    return ew(fn, [dcv, v, sc3], [w_sc], [(2 * d, BF)], [(kk, d)], tm=256, name="conv_a_bwd2", next_halo=(0,), window=window)


def _xdot(a, b, passes, split_lhs, ca=1, cb=0):
    parts, r = [], (a if split_lhs else b)
    for _ in range(passes):
        piece = r.astype(BF)
        parts.append(piece)
        r = r - piece.astype(F32)
    other = (b if split_lhs else a).astype(BF)
    acc = None
    for piece in parts:
        term = _dot(piece, other, ca, cb) if split_lhs else _dot(other, piece, ca, cb)
        acc = term if acc is None else acc + term
    return acc


def _ssd_common(xbc_ref, dt_ref, alog_ref, e_ref, w):
    ll = SSM_L
    xs = xbc_ref[:, 0:w]
    dtv = dt_ref[...]
    a_row = -jnp.exp(alog_ref[...])
    a = dtv * a_row
    row = lax.broadcasted_iota(jnp.int32, (ll, ll), 0)
    col = lax.broadcasted_iota(jnp.int32, (ll, ll), 1)
    tril = (row >= col).astype(F32)
    triu = (row <= col).astype(F32)
    acl = _xdot(tril, a, 3, False)
    acl_t = _xdot(a, triu, 3, True, 0, 0)
    e = e_ref[...]
    aclx = _xdot(acl, e, 3, True)
    dtx = _xdot(dtv, e, 2, True)
    last = aclx[ll - 1:ll, :]
    e_in = jnp.exp(aclx)
    e_end = jnp.exp(last - aclx)
    e_tot = jnp.exp(last)
    x = xs * dtx
    return dict(xs=xs, dtv=dtv, a_row=a_row, a=a, row=row, col=col, triu=triu, acl=acl, acl_t=acl_t, dtx=dtx, e_in=e_in, e_end=e_end,
                e_tot=e_tot, x=x)


def _decay(q, hh):
    diff = q["acl"][:, hh:hh + 1] - q["acl_t"][hh:hh + 1, :]
    return jnp.exp(jnp.where(q["row"] >= q["col"], diff, -jnp.inf))


def ssd_fwd(xbc, dt, z, a_log, d_exp, m_norm, e_mat, rider=None):
    t = xbc.shape[0]
    w = z.shape[1]
    gn = SSM_G * SSM_N
    gw = w // SSM_G
    ll, nn = SSM_L, SSM_N
    nc = t // ll
    cw = xbc.shape[1]

    def body(xbc_ref, dt_ref, z_ref, alog_ref, dexp_ref, mn_ref, e_ref, yn_ref, y_ref, sp_ref, s_sc):
        c = pl.program_id(0)

        @pl.when(c == 0)
        def _():
            s_sc[...] = jnp.zeros_like(s_sc)

        q = _ssd_common(xbc_ref, dt_ref, alog_ref, e_ref, w)
        xb = q["x"].astype(BF)
        xsb = (q["x"] * q["e_end"]).astype(BF)
        sp = s_sc[...]
        sp_ref[0] = sp
        spb = sp.astype(BF)
        lane = lax.broadcasted_iota(jnp.int32, (ll, LANE), 1)
        for g in range(SSM_G):
            lo = g * gw
            bg = xbc_ref[:, w + g * nn:w + (g + 1) * nn].astype(BF)
            cg = xbc_ref[:, w + gn + g * nn:w + gn + (g + 1) * nn].astype(BF)
            yoff = _dot(cg, spb[:, lo:lo + gw]) * q["e_in"][:, lo:lo + gw]
            s_sc[:, lo:lo + gw] = sp[:, lo:lo + gw] * q["e_tot"][:, lo:lo + gw] + _dot(bg, xsb[:, lo:lo + gw], 0, 0)
            cb = _dot(cg, bg, 1, 1)
            for pr in range(gw // LANE):
                l0 = lo + pr * LANE
                xp = xb[:, l0:l0 + LANE]
                ys = []
                for hh in (l0 // SSM_P, l0 // SSM_P + 1):
                    wm = (cb * _decay(q, hh)).astype(BF)
                    ys.append(_dot(wm, xp))
                ydiag = jnp.where(lane < SSM_P, ys[0], ys[1])
                y_ref[:, l0:l0 + LANE] = ydiag + yoff[:, pr * LANE:(pr + 1) * LANE] + dexp_ref[:, l0:l0 + LANE] * q["xs"][:, l0:l0 + LANE]
        zv = z_ref[...].astype(F32)
        yz = y_ref[...] * (zv * _sigmoid(zv))
        for g in range(SSM_G):
            lo = g * gw
            yn_ref[:, lo:lo + gw] = _rms(yz[:, lo:lo + gw], mn_ref[:, lo:lo + gw]).astype(BF)

    vec = lambda s: pl.BlockSpec(s, lambda c: (0, 0))
    return host_call(
        body,
        name="ssd_fwd",
        grid=(nc,),
        in_specs=[
            pl.BlockSpec((ll, cw), lambda c: (c, 0)), pl.BlockSpec((ll, LANE), lambda c: (c, 0)), pl.BlockSpec((ll, w), lambda c: (c, 0)),
            vec((1, LANE)), vec((1, w)), vec((1, w)), vec((LANE, w)),
        ],
        out_specs=[pl.BlockSpec((ll, w), lambda c: (c, 0)), pl.BlockSpec((ll, w), lambda c: (c, 0)), pl.BlockSpec((1, nn, w), lambda c: (c, 0, 0))],
        out_shape=[jax.ShapeDtypeStruct((t, w), BF), jax.ShapeDtypeStruct((t, w), F32), jax.ShapeDtypeStruct((nc, nn, w), F32)],
        scratch_shapes=[pltpu.VMEM((nn, w), F32)],
        operands=(xbc, dt, z, a_log, d_exp, m_norm, e_mat),
        rider=rider,
    )


def ssd_bwd(dyn, y, z, xbc, dt, sprev, a_log, d_exp, m_norm, e_mat, et_mat, window, rider=None):
    t = xbc.shape[0]
    w = z.shape[1]
    gn = SSM_G * SSM_N
    gw = w // SSM_G
    ll, nn = SSM_L, SSM_N
    nc = t // ll
    cw = xbc.shape[1]

    def body(dyn_ref, y_ref, z_ref, xbc_ref, dt_ref, sp_ref, alog_ref, dexp_ref, mn_ref, e_ref, et_ref,
             dz_ref, dxbc_ref, ddt_ref, dmn_ref, dd_ref, dal_ref, ds_sc, dy_sc, dx_sc):
        step = pl.program_id(0)

        @pl.when(step == 0)
        def _():
            ds_sc[...] = jnp.zeros_like(ds_sc)

        zv, yv = z_ref[...].astype(F32), y_ref[...]
        sg = _sigmoid(zv)
        sz = zv * sg
        yz = yv * sz
        dmn = []
        for g in range(SSM_G):
            lo = g * gw
            dseg, dmn_g = _rms_bwd(yz[:, lo:lo + gw], mn_ref[:, lo:lo + gw], dyn_ref[:, lo:lo + gw])
            dy_sc[:, lo:lo + gw] = dseg
            dmn.append(dmn_g)
        dmn = jnp.concatenate(dmn, axis=1)
        dyz = dy_sc[...]
        dz_ref[...] = (dyz * yv * (sg * (1.0 + zv * (1.0 - sg)))).astype(BF)
        dy = dyz * sz

        q = _ssd_common(xbc_ref, dt_ref, alog_ref, e_ref, w)
        x = q["x"]
        xb = x.astype(BF)
        xsb = (x * q["e_end"]).astype(BF)
        sp = sp_ref[0]
        spb = sp.astype(BF)
        dsn = ds_sc[...]
        dsnb = dsn.astype(BF)
        dyb = dy.astype(BF)
        lane = lax.broadcasted_iota(jnp.int32, (ll, LANE), 1)
        lane1 = lax.broadcasted_iota(jnp.int32, (1, LANE), 1)
        sub1 = lax.broadcasted_iota(jnp.int32, (LANE, 1), 0)
        dacl = jnp.zeros((ll, LANE), F32)
        dacl_t = jnp.zeros((LANE, ll), F32)
        d_ein, d_eend, d_etot = [], [], []
        for g in range(SSM_G):
            lo = g * gw
            sl = slice(lo, lo + gw)
            bg = xbc_ref[:, w + g * nn:w + (g + 1) * nn].astype(BF)
            cg = xbc_ref[:, w + gn + g * nn:w + gn + (g + 1) * nn].astype(BF)
            zg = _dot(cg, spb[:, sl])
            dzz = (dy[:, sl] * q["e_in"][:, sl]).astype(BF)
            d_ein.append(dy[:, sl] * zg)
            dcg = _dot(dzz, spb[:, sl], 1, 1)
            ds_sc[:, sl] = _dot(cg, dzz, 0, 0) + dsn[:, sl] * q["e_tot"][:, sl]
            d_etot.append(jnp.sum(dsn[:, sl] * sp[:, sl], axis=0, keepdims=True))
            dbg = _dot(xsb[:, sl], dsnb[:, sl], 1, 1)
            dxs_g = _dot(bg, dsnb[:, sl])
            d_eend.append(dxs_g * x[:, sl])
            cb = _dot(cg, bg, 1, 1)
            dcb = jnp.zeros((ll, ll), F32)
            for pr in range(gw // LANE):
                l0 = lo + pr * LANE
                xp = xb[:, l0:l0 + LANE]
                dyp = dyb[:, l0:l0 + LANE]
                dxp = []
                for hi, hh in enumerate((l0 // SSM_P, l0 // SSM_P + 1)):
                    lm = _decay(q, hh)
                    wm = (cb * lm).astype(BF)
                    in_head = (lane < SSM_P) if hi == 0 else (lane >= SSM_P)
                    dwm = _dot(jnp.where(in_head, dyp, jnp.zeros_like(dyp)), xp, 1, 1)
                    dxp.append(_dot(wm, dyp, 0, 0))
                    dlm = dwm * lm
                    dcb = dcb + dlm
                    dd = dlm * cb
                    dacl = dacl + jnp.sum(dd, axis=1, keepdims=True) * (lane1 == hh).astype(F32)
                    dacl_t = dacl_t + (sub1 == hh).astype(F32) * jnp.sum(dd, axis=0, keepdims=True)
                dx_sc[:, l0:l0 + LANE] = jnp.where(lane < SSM_P, dxp[0], dxp[1]) + dxs_g[:, pr * LANE:(pr + 1) * LANE] * q["e_end"][:, l0:l0 + LANE]
            dcbb = dcb.astype(BF)
            dxbc_ref[:, w + g * nn:w + (g + 1) * nn] = dbg + _dot(dcbb, cg, 0, 0)
            dxbc_ref[:, w + gn + g * nn:w + gn + (g + 1) * nn] = dcg + _dot(dcbb, bg)
        d_ein = jnp.concatenate(d_ein, axis=1) * q["e_in"]
        d_eend = jnp.concatenate(d_eend, axis=1) * q["e_end"]
        d_etot = jnp.concatenate(d_etot, axis=1) * q["e_tot"]
        et = et_ref[...]
        last_add = jnp.sum(d_eend, axis=0, keepdims=True) + d_etot
        last_add = _xdot(jnp.broadcast_to(last_add, (HALO, w)), et, 2, True)[0:1]
        row1 = lax.broadcasted_iota(jnp.int32, (ll, LANE), 0)
        dacl = dacl + _xdot(d_ein - d_eend, et, 2, True) + jnp.where(row1 == ll - 1, last_add, 0.0)
        da = _xdot(q["triu"], dacl, 2, False) - _xdot(q["triu"], dacl_t, 2, False, 1, 1)
        dxv = dx_sc[...]
        dxbc_ref[:, 0:w] = dexp_ref[...] * dy + dxv * q["dtx"]
        ddt_ref[...] = _xdot(dxv * q["xs"], et, 2, True) + da * q["a_row"]
        dal = jnp.sum(da * q["dtv"], axis=0, keepdims=True) * q["a_row"]
        ddv = jnp.sum(dy * q["xs"], axis=0, keepdims=True)
        ddv = _xdot(jnp.broadcast_to(ddv, (HALO, w)), et, 2, True)[0:1]
        _accumulate(dmn_ref, dmn, step == 0)
        _accumulate(dd_ref, ddv, step == 0)
        _accumulate(dal_ref, dal, step == 0)

    rev = lambda c_: pl.BlockSpec((ll, c_), lambda s: (nc - 1 - s, 0))
    vec = lambda s_: pl.BlockSpec(s_, lambda s: (0, 0))
    n_in = 11

    def body_skipping_buffer(*refs):
        body(*refs[:n_in], *refs[n_in + 1:])

    return host_call(
        body_skipping_buffer,
        name="ssd_bwd",
        grid=(nc,),
        in_specs=[
            rev(w), rev(w), rev(w), rev(cw), rev(LANE), pl.BlockSpec((1, nn, w), lambda s: (nc - 1 - s, 0, 0)),
            vec((1, LANE)), vec((1, w)), vec((1, w)), vec((LANE, w)), vec((w, LANE)), ANY,
        ],
        out_specs=[pl.BlockSpec((ll, w), lambda s: (nc - 1 - s, window.block)), rev(cw), rev(LANE), vec((1, w)), vec((1, LANE)), vec((1, LANE))],
        out_shape=[
            jax.ShapeDtypeStruct((t, window.cols), BF), jax.ShapeDtypeStruct((t, cw), F32), jax.ShapeDtypeStruct((t, LANE), F32),
            jax.ShapeDtypeStruct((1, w), F32), jax.ShapeDtypeStruct((1, LANE), F32), jax.ShapeDtypeStruct((1, LANE), F32),
        ],
        scratch_shapes=[pltpu.VMEM((nn, w), F32), pltpu.VMEM((ll, w), F32), pltpu.VMEM((ll, w), F32)],
        operands=(dyn, y, z, xbc, dt, sprev, a_log, d_exp, m_norm, e_mat, et_mat, window.buf),
        rider=rider,
        aliases={n_in: 0},
    )


def _w1024_spec(d, nblk, iblk):
    r = nblk * (d // NCHIP)
    return pl.BlockSpec((NCHIP, r, d), lambda i: (0, iblk // nblk, 0))


def _whole(ref):
    v = ref[...]
    return v.reshape(v.shape[0] * v.shape[1], v.shape[2])


def mix_out_fwd(ya_in, yn, gates, h, w1024, rider=None):
    t, d = h.shape
    tm = _tile(t, 256)

    def body(ya_ref, yn_ref, g_ref, h_ref, wm_ref, wa_ref, wo_ref, ho_ref, oa_ref, om_ref, mg_ref):
        y_a = _dot(ya_ref[...], _whole(wa_ref))
        y_m = _dot(yn_ref[...], _whole(wm_ref))
        oa_ref[...] = y_a
        om_ref[...] = y_m
        gv = g_ref[...].astype(F32)
        mg = (_sigmoid(gv[:, :d]) * y_a + _sigmoid(gv[:, d:]) * y_m).astype(BF)
        mg_ref[...] = mg
        ho_ref[...] = h_ref[...] + _dot(mg, _whole(wo_ref))

    row = lambda c: pl.BlockSpec((tm, c), lambda i: (i, 0))
    return host_call(
        body,
        name="mix_out_fwd",
        grid=(t // tm,),
        in_specs=[row(d), row(2 * d), row(2 * d), row(d), _w1024_spec(d, 2, 0), _w1024_spec(d, 1, 2), _w1024_spec(d, 1, 3)],
        out_specs=[row(d), row(d), row(d), row(d)],
        out_shape=[jax.ShapeDtypeStruct((t, d), F32), jax.ShapeDtypeStruct((t, d), F32), jax.ShapeDtypeStruct((t, d), F32),
                   jax.ShapeDtypeStruct((t, d), BF)],
        scratch_shapes=[],
        operands=(ya_in, yn, gates, h, w1024, w1024, w1024),
        rider=rider,
    )


def mix_out_bwd(dh, gates, y_a, y_m, w1024, cols):
    t, d = dh.shape
    tm = _tile(t, 256)

    def body(dh_ref, g_ref, ya_ref, ym_ref, wm_ref, wa_ref, wo_ref, dg_ref, dya_ref, dyn_ref, da_ref, dm_ref):
        dmg = _dot(dh_ref[...].astype(BF), _whole(wo_ref), 1, 1)
        gv = g_ref[...].astype(F32)
        sa, sm = _sigmoid(gv[:, :d]), _sigmoid(gv[:, d:])
        dg_ref[:, :d] = (dmg * ya_ref[...] * sa * (1.0 - sa)).astype(BF)
        dg_ref[:, d:] = (dmg * ym_ref[...] * sm * (1.0 - sm)).astype(BF)
        da = (dmg * sa).astype(BF)
        dm = (dmg * sm).astype(BF)
        da_ref[...] = da
        dm_ref[...] = dm
        dya_ref[...] = _dot(da, _whole(wa_ref), 1, 1)
        dyn_ref[...] = _dot(dm, _whole(wm_ref), 1, 1)

    row = lambda c: pl.BlockSpec((tm, c), lambda i: (i, 0))
    return pl.pallas_call(
        body,
        name="mix_out_bwd",
        grid=(t // tm,),
        in_specs=[row(d), row(2 * d), row(d), row(d), _w1024_spec(d, 2, 0), _w1024_spec(d, 1, 2), _w1024_spec(d, 1, 3)],
        out_specs=[row(2 * d), row(d), row(2 * d), row(d), row(d)],
        out_shape=[jax.ShapeDtypeStruct((t, cols), BF), jax.ShapeDtypeStruct((t, d), F32), jax.ShapeDtypeStruct((t, 2 * d), F32),
                   jax.ShapeDtypeStruct((t, d), BF), jax.ShapeDtypeStruct((t, d), BF)],
        compiler_params=_params(("parallel",)),
    )(dh, gates, y_a, y_m, w1024, w1024, w1024)


def norm_bwd_add(dh, h, g, dn):
    def fn(i, nt, rows, vecs, prevs, nexts):
        dx, dg = _rms_bwd(rows[1], vecs[0], rows[2])
        return [rows[0] + dx], [dg]
    d = h.shape[1]
    return ew(fn, [dh, h, dn], [g], [(d, F32)], [(1, d)], tm=1024, name="norm_bwd_add")


def _pe(p, wpp_ref):
    pb = p.astype(BF)
    return jnp.concatenate([_dot(pb, wpp_ref[k]) for k in range(NCHIP)], axis=1)


def ple_fwd(h, g, p, w1024, wpp):
    t, d = h.shape
    tm = _tile(t, 512)

    def body(h_ref, g_ref, p_ref, wg_ref, wp_ref, ho_ref):
        hv = h_ref[...]
        gate = _sigmoid(_dot(_rms(hv, g_ref[...]).astype(BF), _whole(wg_ref)))
        ho_ref[...] = hv + gate * _pe(p_ref[...], wp_ref)

    row = lambda c: pl.BlockSpec((tm, c), lambda i: (i, 0))
    wpp_spec = pl.BlockSpec(wpp.shape, lambda i: (0, 0, 0))
    return pl.pallas_call(
        body,
        name="ple_fwd",
        grid=(t // tm,),
        in_specs=[row(d), pl.BlockSpec((1, d), lambda i: (0, 0)), row(p.shape[1]), _w1024_spec(d, 1, 4), wpp_spec],
        out_specs=row(d),
        out_shape=jax.ShapeDtypeStruct((t, d), F32),
        compiler_params=_params(("parallel",)),
    )(h, g, p, w1024, wpp)


def ple_bwd(dho, h, g, p, w1024, wpp):
    t, d = h.shape
    tm = _tile(t, 512)

    def body(dho_ref, h_ref, g_ref, p_ref, wg_ref, wp_ref, dh_ref, dg_ref, n_ref, dgp_ref, dpe_ref):
        hv, dv = h_ref[...], dho_ref[...]
        n = _rms(hv, g_ref[...]).astype(BF)
        n_ref[...] = n
        wg = _whole(wg_ref)
        gate = _sigmoid(_dot(n, wg))
        pe = _pe(p_ref[...], wp_ref)
        dpe_ref[...] = (dv * gate).astype(BF)
        dgp = (dv * pe * gate * (1.0 - gate)).astype(BF)
        dgp_ref[...] = dgp
        dx, dg = _rms_bwd(hv, g_ref[...], _dot(dgp, wg, 1, 1))
        dh_ref[...] = dv + dx
        _accumulate(dg_ref, dg, pl.program_id(0) == 0)

    row = lambda c: pl.BlockSpec((tm, c), lambda i: (i, 0))
    wpp_spec = pl.BlockSpec(wpp.shape, lambda i: (0, 0, 0))
    return pl.pallas_call(
        body,
        name="ple_bwd",
        grid=(t // tm,),
        in_specs=[row(d), row(d), pl.BlockSpec((1, d), lambda i: (0, 0)), row(p.shape[1]), _w1024_spec(d, 1, 4), wpp_spec],
        out_specs=[row(d), pl.BlockSpec((1, d), lambda i: (0, 0)), row(d), row(d), row(d)],
        out_shape=[jax.ShapeDtypeStruct((t, d), F32), jax.ShapeDtypeStruct((1, d), F32), jax.ShapeDtypeStruct((t, d), BF),
                   jax.ShapeDtypeStruct((t, d), BF), jax.ShapeDtypeStruct((t, d), BF)],
        compiler_params=_params(("arbitrary",)),
    )(dho, h, g, p, w1024, wpp)


def loss_bwd(h, g, target):
    d = h.shape[1]

    def fn(i, nt, rows, vecs, prevs, nexts):
        err = _rms(rows[0], vecs[0]) - rows[1]
        dx, dg = _rms_bwd(rows[0], vecs[0], err * (1.0 / d))
        return [dx], [jnp.sum(err * err, axis=0, keepdims=True) * (0.5 / d), dg]

    return ew(fn, [h, target], [g], [(d, F32)], [(1, d), (1, d)], tm=512, name="loss_bwd")


def adamw(w, g, m, v, name):
    c1, c2 = 1.0 / (1.0 - ADAM_B1 ** ADAM_STEP), 1.0 / (1.0 - ADAM_B2 ** ADAM_STEP)

    def fn(i, nt, rows, vecs, prevs, nexts):
        wv, gv, mv, vv = rows
        mn = ADAM_B1 * mv + (1.0 - ADAM_B1) * gv
        vn = ADAM_B2 * vv + (1.0 - ADAM_B2) * (gv * gv)
        delta = -ADAM_LR * ((mn * c1) / (jnp.sqrt(vn * c2) + ADAM_EPS) + ADAM_WD * wv)
        return [delta, mn, vn], []

    c = w.shape[1]
    return ew(fn, [w, g, m, v], [], [(c, F32)] * 3, tm=_row_tile(w.shape[0], c, HALO), name=name)


def _place():
    return lax.axis_index("x"), lax.axis_index("y"), lax.axis_index("c")


def _other_chips(x, y):
    return [(1 - x, y), (x, 1 - y), (1 - x, 1 - y)]


ANY = pl.BlockSpec(memory_space=pl.ANY)


def _comm_call(body, name, ins, out_shapes, n_sems, aliases=None):
    return pl.pallas_call(
        body,
        name=name,
        in_specs=[ANY] * len(ins),
        out_specs=[ANY] * len(out_shapes),
        out_shape=out_shapes,
        scratch_shapes=[pltpu.SemaphoreType.DMA((n_sems,)), pltpu.SemaphoreType.DMA((n_sems,))],
        input_output_aliases=aliases or {},
    )(*ins)


def gather_rider(packs):
    nt = len(packs)

    def pieces(ins, outs, send_sems, recv_sems):
        x, y, cc = _place()
        chips = _other_chips(x, y)
        sibling = (x, y, 1 - cc)
        k_me = 2 * x + y

        def copy(k, src, dst, to):
            return pltpu.make_async_remote_copy(src_ref=src, dst_ref=dst, send_sem=send_sems.at[k], recv_sem=recv_sems.at[k],
                                                device_id=to, device_id_type=MESH)

        sends, forwards, arrivals = [], [], []
        for ti in range(nt):
            for j, (px, py) in enumerate(chips):
                sends.append(copy(7 * ti + j, ins[ti].at[cc], outs[ti].at[k_me, cc], (px, py, cc)))
                landed = outs[ti].at[2 * px + py, cc]
                forwards.append((copy(7 * ti + j, landed, landed, (px, py, cc)), copy(7 * ti + 3 + j, landed, landed, sibling)))
                passed = outs[ti].at[2 * px + py, 1 - cc]
                arrivals.append(copy(7 * ti + 3 + j, passed, passed, sibling))
            sends.append(copy(7 * ti + 6, ins[ti], outs[ti].at[k_me], sibling))
            own = outs[ti].at[k_me]
            arrivals.append(copy(7 * ti + 6, own, own, sibling))
        return sends, forwards, arrivals

    def start(*parts):
        for cp in pieces(*parts)[0]:
            cp.start()

    def mid(*parts):
        for landed, forward in pieces(*parts)[1]:
            landed.wait_recv()
            forward.start()

    def finish(*parts):
        sends, forwards, arrivals = pieces(*parts)
        for cp in arrivals:
            cp.wait_recv()
        for cp in sends + [f for _, f in forwards]:
            cp.wait_send()

    return Rider(packs, [jax.ShapeDtypeStruct((NCHIP,) + p.shape, p.dtype) for p in packs], 7 * nt, start, finish, mid)


def swap_rider(gs):
    nt = len(gs)
    hl = gs[0].shape[1] // 2

    def copies(ins, outs, send_sems, recv_sems):
        x, y, cc = _place()
        theirs = pl.ds((1 - cc) * hl, hl)
        return [pltpu.make_async_remote_copy(src_ref=ins[ti].at[:, theirs], dst_ref=outs[ti], send_sem=send_sems.at[ti], recv_sem=recv_sems.at[ti],
                                             device_id=(x, y, 1 - cc), device_id_type=MESH) for ti in range(nt)]

    def start(*parts):
        for cp in copies(*parts):
            cp.start()

    def finish(*parts):
        for cp in copies(*parts):
            cp.wait()

    return Rider(gs, [jax.ShapeDtypeStruct((NCHIP, hl) + g.shape[2:], g.dtype) for g in gs], nt, start, finish)


def scatter_packs(cs, name):
    return scatter_rider(cs).standalone(name)


def scatter_rider(cs):
    nt = len(cs)

    def copies(ins, outs, send_sems, recv_sems):
        x, y, cc = _place()
        cps = []
        for ti in range(nt):
            for j, (px, py) in enumerate(_other_chips(x, y)):
                cps.append(pltpu.make_async_remote_copy(src_ref=ins[ti].at[2 * px + py], dst_ref=outs[ti].at[j], send_sem=send_sems.at[3 * ti + j],
                                                        recv_sem=recv_sems.at[3 * ti + j], device_id=(px, py, cc), device_id_type=MESH))
        return cps

    def start(*parts):
        for cp in copies(*parts):
            cp.start()

    def finish(*parts):
        for cp in copies(*parts):
            cp.wait()

    return Rider(cs, [jax.ShapeDtypeStruct((3,) + c_.shape[1:], c_.dtype) for c_ in cs], 3 * nt, start, finish)


def join_packs(fulls, name):
    nt = len(fulls)
    hl = fulls[0].shape[0] // 2

    def body(*refs):
        ins, outs, (send_sems, recv_sems) = refs[:nt], refs[nt:2 * nt], refs[2 * nt:]
        x, y, cc = _place()
        mine = pl.ds(cc * hl, hl)
        cps = [pltpu.make_async_remote_copy(src_ref=ins[ti].at[mine], dst_ref=outs[ti].at[mine], send_sem=send_sems.at[ti], recv_sem=recv_sems.at[ti],
                                            device_id=(x, y, 1 - cc), device_id_type=MESH) for ti in range(nt)]
        for cp in cps:
            cp.start()
        for cp in cps:
            cp.wait()

    return _comm_call(body, name, fulls, [jax.ShapeDtypeStruct(f.shape, f.dtype) for f in fulls], nt, aliases={ti: ti for ti in range(nt)})


def add_sibling(g, recv, name):
    _, nl, r, c = g.shape
    hl = nl // 2
    tm, tc = _tile2(r, c)

    def body(g_ref, r_ref, o_ref):
        o_ref[...] = (g_ref[...].astype(F32) + r_ref[...].astype(F32)).astype(o_ref.dtype)

    blk = (None, None, tm, tc)
    return pl.pallas_call(
        body,
        name=name,
        grid=(NCHIP, hl, r // tm, c // tc),
        in_specs=[pl.BlockSpec(blk, lambda k, l, i, j: (k, lax.axis_index("c") * hl + l, i, j)), pl.BlockSpec(blk, lambda k, l, i, j: (k, l, i, j))],
        out_specs=pl.BlockSpec(blk, lambda k, l, i, j: (k, l, i, j)),
        out_shape=jax.ShapeDtypeStruct(recv.shape, BF),
        compiler_params=_params(("parallel",) * 4),
    )(g, recv)


def add_chips(cs, got, nl, name):
    _, hl, r, c = cs.shape
    tm, tc = _tile2(r, c)

    def body(own_ref, got_ref, o_ref):
        o_ref[...] = own_ref[...].astype(F32) + got_ref[0].astype(F32) + got_ref[1].astype(F32) + got_ref[2].astype(F32)

    return pl.pallas_call(
        body,
        name=name,
        grid=(hl, r // tm, c // tc),
        in_specs=[pl.BlockSpec((None, None, tm, tc), lambda l, i, j: (2 * lax.axis_index("x") + lax.axis_index("y"), l, i, j)),
                  pl.BlockSpec((3, None, tm, tc), lambda l, i, j: (0, l, i, j))],
        out_specs=pl.BlockSpec((None, tm, tc), lambda l, i, j: (lax.axis_index("c") * hl + l, i, j)),
        out_shape=jax.ShapeDtypeStruct((nl, r, c), F32),
        compiler_params=_params(("parallel",) * 3),
    )(cs, got)


def all_gather_xy(shard, name):
    r, c = shard.shape
    hr = r // 2
    assert r % 32 == 0

    def body(x_ref, out_ref, send_sems, recv_sems, local_sem):
        x, y, cc = _place()
        chips = _other_chips(x, y)
        mine = pl.ds(pl.multiple_of(cc * hr, 16), hr)
        theirs = pl.ds(pl.multiple_of((1 - cc) * hr, 16), hr)
        k_me = 2 * x + y

        def copy(k, src, dst, to):
            return pltpu.make_async_remote_copy(src_ref=src, dst_ref=dst, send_sem=send_sems.at[k], recv_sem=recv_sems.at[k],
                                                device_id=to, device_id_type=MESH)

        own = pltpu.make_async_copy(x_ref, out_ref.at[k_me], local_sem)
        own.start()
        first = [copy(j, x_ref.at[mine], out_ref.at[k_me, mine], (*chip, cc)) for j, chip in enumerate(chips)]
        for cp in first:
            cp.start()
        passed = []
        for j, (px, py) in enumerate(chips):
            landed = out_ref.at[2 * px + py, mine]
            copy(j, landed, landed, (px, py, cc)).wait_recv()
            fw = copy(3 + j, landed, landed, (x, y, 1 - cc))
            fw.start()
            passed.append(fw)
        for j, (px, py) in enumerate(chips):
            landed = out_ref.at[2 * px + py, theirs]
            copy(3 + j, landed, landed, (x, y, 1 - cc)).wait_recv()
        for cp in first + passed:
            cp.wait_send()
        own.wait()

    return pl.pallas_call(
        body,
        name=name,
        in_specs=[ANY],
        out_specs=ANY,
        out_shape=jax.ShapeDtypeStruct((NCHIP, r, c), shard.dtype),
        scratch_shapes=[pltpu.SemaphoreType.DMA((6,)), pltpu.SemaphoreType.DMA((6,)), pltpu.SemaphoreType.DMA],
    )(shard)


def all_gather_8(block, name):
    m, c = block.shape

    def body(x_ref, out_ref, send_sems, recv_sems, local_sem):
        x, y, cc = _place()
        me, sibling = (x, y, cc), (x, y, 1 - cc)
        chips = _other_chips(x, y)

        def rows(px, py, pc):
            return out_ref.at[4 * px + 2 * py + pc]

        def copy(k, blk, to, src=None):
            return pltpu.make_async_remote_copy(src_ref=rows(*blk) if src is None else src, dst_ref=rows(*blk), send_sem=send_sems.at[k],
                                                recv_sem=recv_sems.at[k], device_id=to, device_id_type=MESH)

        mine = pltpu.make_async_copy(x_ref, rows(*me), local_sem)
        mine.start()
        first = [copy(0, me, sibling, src=x_ref)]
        first += [copy(1 + j, me, (*chip, cc), src=x_ref) for j, chip in enumerate(chips)]
        for cp in first:
            cp.start()
        passed = [copy(4 + j, (*chip, cc), sibling) for j, chip in enumerate(chips)]
        for j, chip in enumerate(chips):
            copy(1 + j, (*chip, cc), me).wait_recv()
            passed[j].start()
        copy(0, sibling, me).wait_recv()
        for j, chip in enumerate(chips):
            copy(4 + j, (*chip, 1 - cc), me).wait_recv()
        for cp in first + passed:
            cp.wait_send()
        mine.wait()

    return pl.pallas_call(
        body,
        name=name,
        in_specs=[pl.BlockSpec(memory_space=pltpu.VMEM)],
        out_specs=pl.BlockSpec(memory_space=pltpu.VMEM),
        out_shape=jax.ShapeDtypeStruct((8, m, c), block.dtype),
        scratch_shapes=[pltpu.SemaphoreType.DMA((7,)), pltpu.SemaphoreType.DMA((7,)), pltpu.SemaphoreType.DMA],
        compiler_params=pltpu.CompilerParams(vmem_limit_bytes=VMEM_LIMIT),
    )(block)


def add_parts(parts, out_dtype, name, tm=512):
    def fn(i, nt, rows, vecs, prevs, nexts):
        acc = rows[0]
        for r_ in rows[1:]:
            acc = acc + r_
        return [acc], []
    r, c = parts[0].shape
    return ew(fn, list(parts), [], [(c, out_dtype)], tm=_tile(r, tm, 16), name=name)[0]


SMALL_SHARDED = ("sc_conv_w", "m_conv_w")
SMALL_REPL = ("ffn1_norm", "mix_norm", "m_conv_b", "m_dt_bias", "m_A_log", "m_D", "m_norm", "ffn2_norm", "ple_norm", "final_norm")
BIG = ("ffn1_wg", "ffn1_wu", "ffn1_wd", "w_in", "sc_w_out", "m_w_out", "w_o", "ffn2_wg", "ffn2_wu", "ffn2_wd", "ple_w_gate", "ple_w_proj")
TRANSPOSED = ("ffn1_wg", "ffn1_wu", "ffn2_wg", "ffn2_wu", "w_in")
ORDER = ("ffn1_norm", "ffn1_wg", "ffn1_wu", "ffn1_wd", "mix_norm", "w_in", "sc_conv_w", "sc_w_out", "m_conv_w", "m_conv_b", "m_dt_bias",
         "m_A_log", "m_D", "m_norm", "m_w_out", "w_o", "ffn2_norm", "ffn2_wg", "ffn2_wu", "ffn2_wd", "ple_norm", "ple_w_gate", "ple_w_proj",
         "final_norm")


def _pack(arrs, cols, row_mult):
    flat = jnp.concatenate([a.reshape(-1) for a in arrs])
    n = flat.shape[0]
    rows = -(-n // cols)
    rows = -(-rows // row_mult) * row_mult
    return jnp.pad(flat, (0, rows * cols - n)).reshape(rows, cols)


def _unpack(flat2d, shapes):
    flat = flat2d.reshape(-1)
    out, off = [], 0
    for s in shapes:
        n = int(np.prod(s))
        out.append(flat[off:off + n].reshape(s))
        off += n
    return out


def _row_cat(arrs, dtype):
    return jnp.concatenate([a.astype(dtype) for a in arrs], axis=1)


def kernel(x, p, ffn1_norm, ffn1_wg, ffn1_wu, ffn1_wd, mix_norm, w_in, sc_conv_w, sc_w_out, m_conv_w, m_conv_b, m_dt_bias, m_A_log, m_D, m_norm, m_w_out, w_o, ffn2_norm, ffn2_wg, ffn2_wu, ffn2_wd, ple_norm, ple_w_gate, ple_w_proj, final_norm, loss_target, m_ffn1_norm, m_ffn1_wg, m_ffn1_wu, m_ffn1_wd, m_mix_norm, m_w_in, m_sc_conv_w, m_sc_w_out, m_m_conv_w, m_m_conv_b, m_m_dt_bias, m_m_A_log, m_m_D, m_m_norm, m_m_w_out, m_w_o, m_ffn2_norm, m_ffn2_wg, m_ffn2_wu, m_ffn2_wd, m_ple_norm, m_ple_w_gate, m_ple_w_proj, m_final_norm, v_ffn1_norm, v_ffn1_wg, v_ffn1_wu, v_ffn1_wd, v_mix_norm, v_w_in, v_sc_conv_w, v_sc_w_out, v_m_conv_w, v_m_conv_b, v_m_dt_bias, v_m_A_log, v_m_D, v_m_norm, v_m_w_out, v_w_o, v_ffn2_norm, v_ffn2_wg, v_ffn2_wu, v_ffn2_wd, v_ple_norm, v_ple_w_gate, v_ple_w_proj, v_final_norm):
    args = dict(locals())
    wts = {n: args[n] for n in ORDER}
    mom = {n: args["m_" + n] for n in ORDER}
    vel = {n: args["v_" + n] for n in ORDER}

    depth = ffn1_norm.shape[0]
    d = x.shape[-1]
    w = 2 * d
    hh = w // SSM_P
    cw = w + 2 * SSM_G * SSM_N
    d4 = d // NCHIP
    pp = 7 * d + cw + LANE
    my_x, my_y, my_c = _place()
    k_me = 2 * my_x + my_y

    tr = lambda a: jnp.swapaxes(a, 1, 2)
    gu_t = [_row_cat([tr(wg_), tr(wu_)], BF) for wg_, wu_ in ((ffn1_wg, ffn1_wu), (ffn2_wg, ffn2_wu))]
    wd_l = [ffn1_wd.astype(BF), ffn2_wd.astype(BF)]
    w1024_l = _row_cat([m_w_out, sc_w_out, w_o, ple_w_gate], BF)
    p4 = w_in.shape[2]
    p4p = -(-p4 // 32) * 32
    win_l, wpp_l = jnp.pad(tr(w_in).astype(BF), ((0, 0), (0, p4p - p4), (0, 0))), ple_w_proj.astype(BF)
    halves = lambda a: a.reshape(2, a.shape[0] // 2, a.shape[1])
    whole = lambda g: g.reshape(NCHIP, g.shape[2] * 2, g.shape[3])

    def pieces(l):
        return {"small": [halves(w1024_l[l]), halves(wpp_l[l])], "win": [halves(win_l[l])], "gu1": [halves(gu_t[0][l])], "d1": [halves(wd_l[0][l])],
                "gu2": [halves(gu_t[1][l])], "d2": [halves(wd_l[1][l])]}

    small_local = [sc_conv_w, m_conv_w]
    gathered_s = all_gather_xy(_pack(small_local, LANE, 32), "gather_conv_weights")
    per_shard_s = [_unpack(gathered_s[k], [a.shape for a in small_local]) for k in range(NCHIP)]
    sc_conv_full = jnp.concatenate([per_shard_s[k][0] for k in range(NCHIP)], axis=2)
    m_conv_full = jnp.concatenate([per_shard_s[k][1] for k in range(NCHIP)], axis=2)

    pad_h = lambda a: jnp.pad(a, ((0, 0), (0, LANE - hh)))
    dt_bias_p, a_log_p = pad_h(m_dt_bias), pad_h(m_A_log)
    d_exp = jnp.repeat(m_D, SSM_P, axis=1)
    e_mat = (jnp.arange(w)[None, :] // SSM_P == jnp.arange(LANE)[:, None]).astype(F32)
    et_mat = e_mat.T
    o_z, o_xbc, o_dt, o_g = 3 * d, 5 * d, 5 * d + cw, 5 * d + cw + hh

    def layer_weights(got):
        wt = {"w1024": whole(got["small"][0]), "wpp": whole(got["small"][1])}
        wt.update({k: whole(got[k][0]) for k in ("gu1", "d1", "gu2", "d2")})
        gw = whole(got["win"][0])

        def wi(lo, hi):
            parts = [gw[k, max(lo - k * p4, 0):min(hi - k * p4, p4)] for k in range(NCHIP) if lo < (k + 1) * p4 and hi > k * p4]
            return parts[0] if len(parts) == 1 else jnp.concatenate(parts, axis=0)

        wt["sc3"], wt["z"], wt["xbc"], wt["g2"] = wi(0, o_z), wi(o_z, o_xbc), wi(o_xbc, o_dt), wi(o_g, o_g + 2 * d)
        wt["dt"] = jnp.pad(wi(o_dt, o_g), ((0, LANE - hh), (0, 0)))
        wt["in_p"] = jnp.concatenate([wt["g2"], wt["z"], wt["sc3"][d:], wt["xbc"], wt["sc3"][:d], wt["dt"]], axis=0)
        return wt

    first = pieces(0)
    order = ("gu1", "d1", "win", "small", "gu2", "d2")
    flat = gather_rider([a for k in order for a in first[k]]).standalone("gather_weights")
    got, pos = {}, 0
    for k in order:
        got[k] = flat[pos:pos + len(first[k])]
        pos += len(first[k])
    wts_l = [layer_weights(got)]

    h = x[0]
    saved = []
    for i in range(depth):
        s, wt = {}, wts_l[i]
        nxt = pieces(i + 1) if i + 1 < depth else None
        ride = lambda k: gather_rider(nxt[k]) if nxt else None
        got = {}
        s["h0"] = h
        (s["ab1"], s4, s["n1"]), got["small"] = ffn_up(h, ffn1_norm[i:i + 1], wt["gu1"], rider=ride("small"))
        h, got["d1"] = ffn_down(s4, wt["d1"], h, rider=ride("d1"))
        s["h1"] = h
        u = norm_cast(h, mix_norm[i:i + 1])
        s["u"] = u
        s["sc3"] = mm(u, wt["sc3"], tb=True, out_dtype=BF, name="proj_sc")
        s["z"] = mm(u, wt["z"], tb=True, out_dtype=BF, name="proj_z")
        s["xbc_raw"] = mm(u, wt["xbc"], tb=True, out_dtype=BF, name="proj_xbc")
        s["gates"] = mm(u, wt["g2"], tb=True, out_dtype=BF, name="proj_gates")
        s["dt_raw"] = mm(u, wt["dt"], tb=True, name="proj_dt")
        s["ya_in"] = conv_a_fwd(s["sc3"], sc_conv_full[i])
        s["xbc"], s["dt"] = conv_m_fwd(s["xbc_raw"], s["dt_raw"], m_conv_full[i], m_conv_b[i:i + 1], dt_bias_p[i:i + 1])
        (s["yn"], s["y"], s["sprev"]), got["win"] = ssd_fwd(s["xbc"], s["dt"], s["z"], a_log_p[i:i + 1], d_exp[i:i + 1], m_norm[i:i + 1], e_mat,
                                                            rider=ride("win"))
        (h, s["y_a"], s["y_m"], s["merged"]), got["gu2"] = mix_out_fwd(s["ya_in"], s["yn"], s["gates"], h, wt["w1024"], rider=ride("gu2"))
        s["h2"] = h
        (s["ab2"], s4, s["n2"]), got["gu1"] = ffn_up(h, ffn2_norm[i:i + 1], wt["gu2"], rider=ride("gu1"))
        h, got["d2"] = ffn_down(s4, wt["d2"], h, rider=ride("d2"))
        s["h3"] = h
        h = ple_fwd(h, ple_norm[i:i + 1], p[i, 0], wt["w1024"], wt["wpp"])
        saved.append(s)
        if nxt:
            wts_l.append(layer_weights(got))

    dh, loss_lanes, g_final = loss_bwd(h, final_norm[None, :], loss_target[0])
    loss = lax.psum(jnp.sum(loss_lanes), ("x", "y", "c"))

    def finish_reduce(cs, got):
        halves = [add_chips(c_, g_, 2, "grad_add_chips") for c_, g_ in zip(cs, got, strict=True)]
        return [f.reshape(-1, f.shape[2]) for f in join_packs(halves, "grad_join_halves")]

    g_layer, pending, reduced = None, None, [None] * depth
    gs = {n: [None] * depth for n in SMALL_SHARDED + SMALL_REPL if n != "final_norm"}
    for i in reversed(range(depth)):
        s = saved[i]
        wt = wts_l[i]
        dh, gs["ple_norm"][i], n3, dgp, dpe = ple_bwd(dh, s["h3"], ple_norm[i:i + 1], p[i, 0], wt["w1024"], wt["wpp"])
        g_pg = mm(n3, dgp, ta=True, out_dtype=BF, name="g_ple_gate", tm_cap=512, tn_cap=512)
        g_pp = mm(p[i, 0], dpe, ta=True, out_dtype=BF, name="g_ple_proj", tm_cap=512, tn_cap=512)
        g_pp = jnp.transpose(g_pp.reshape(g_pp.shape[0], NCHIP, d4), (1, 0, 2))
        (dn2, s2, dab2, do2), from_sibling = ffn_bwd(dh, s["ab2"], wt["gu2"], wt["d2"], rider=swap_rider(g_layer) if g_layer else None)
        if g_layer:
            pending = [add_sibling(g, r_, "grad_add_sibling") for g, r_ in zip(g_layer, from_sibling, strict=True)]
        g_ffn2 = ffn_wgrads(s["n2"], do2, s2, dab2)
        dh, gs["ffn2_norm"][i] = norm_bwd_add(dh, s["h2"], ffn2_norm[i:i + 1], dn2)
        dproj, dya, dyn, dy_a, dy_m = mix_out_bwd(dh, s["gates"], s["y_a"], s["y_m"], wt["w1024"], pp)
        g_wo = mm(s["merged"], dh, ta=True, out_dtype=BF, name="g_w_o", tm_cap=512, tn_cap=512)
        g_sco = mm(s["ya_in"], dy_a, ta=True, out_dtype=BF, name="g_sc_out", tm_cap=512, tn_cap=512)
        g_mo = mm(s["yn"], dy_m, ta=True, out_dtype=BF, name="g_m_out", tm_cap=512, tn_cap=512)
        g_1024 = jnp.concatenate([g_mo.reshape(NCHIP, 2 * d4, d), g_sco.reshape(NCHIP, d4, d), g_wo.reshape(NCHIP, d4, d),
                                  g_pg.reshape(NCHIP, d4, d)], axis=1)
        (dproj, dxbc, ddt, gs["m_norm"][i], gd, gal), got_a = ssd_bwd(dyn, s["y"], s["z"], s["xbc"], s["dt"], s["sprev"], a_log_p[i:i + 1], d_exp[i:i + 1],
                                                                      m_norm[i:i + 1], e_mat, et_mat, Window(0, 1, pp, dproj),
                                                                      rider=scatter_rider(pending[:1]) if pending else None)
        gs["m_D"][i], gs["m_A_log"][i] = gd[:, :hh], gal[:, :hh]
        dpre, dproj, gdb = conv_m_bwd1(dxbc, s["xbc_raw"], ddt, s["dt_raw"], m_conv_full[i], m_conv_b[i:i + 1], dt_bias_p[i:i + 1],
                                       Window(1, (7 * d + cw) // LANE, pp, dproj))
        gs["m_dt_bias"][i] = gdb[:, :hh]
        dproj, gs["m_conv_w"][i], gs["m_conv_b"][i] = conv_bwd2(dpre, s["xbc_raw"], m_conv_full[i], "conv_m_bwd2", Window(0, 6 * d // cw, pp, dproj))
        dcv, dproj, v = conv_a_bwd1(dya, s["sc3"], sc_conv_full[i], Window(1, (6 * d + cw) // d, pp, dproj))
        dproj, gs["sc_conv_w"][i] = conv_a_bwd2(dcv, v, s["sc3"], sc_conv_full[i], Window(0, 2, pp, dproj))
        if pending:
            du, got_b = mm(dproj, wt["in_p"], name="d_proj_in", rider=scatter_rider(pending[2:3]))
            gwp, got_c = mm(dproj, s["u"], ta=True, out_dtype=BF, name="g_w_in", tm_cap=1152, tn_cap=512, rider=scatter_rider(pending[1:2] + pending[3:]))
            reduced[i + 1] = finish_reduce(pending, [got_a[0], got_c[0], got_b[0], got_c[1]])
        else:
            du = mm(dproj, wt["in_p"], name="d_proj_in")
            gwp = mm(dproj, s["u"], ta=True, out_dtype=BF, name="g_w_in", tm_cap=1152, tn_cap=512)
        gw_rows = jnp.concatenate([gwp[6 * d + cw:7 * d + cw], gwp[4 * d:6 * d], gwp[2 * d:4 * d], gwp[6 * d:6 * d + cw], gwp[7 * d + cw:7 * d + cw + hh],
                                   gwp[:2 * d]], axis=0)
        g_in = jnp.pad(gw_rows.reshape(NCHIP, p4, d), ((0, 0), (0, p4p - p4), (0, 0)))
        dh, gs["mix_norm"][i] = norm_bwd_add(dh, s["h1"], mix_norm[i:i + 1], du)
        (dn1, s1, dab1, do1), _ = ffn_bwd(dh, s["ab1"], wt["gu1"], wt["d1"])
        g_ffn1 = ffn_wgrads(s["n1"], do1, s1, dab1)
        dh, gs["ffn1_norm"][i] = norm_bwd_add(dh, s["h0"], ffn1_norm[i:i + 1], dn1)
        g_layer = [jnp.concatenate([g_ffn1, g_ffn2], axis=1), g_1024, g_in, g_pp]
        g_layer = [g.reshape(NCHIP, 2, g.shape[1] // 2, g.shape[2]) for g in g_layer]
    from_sibling = swap_rider(g_layer).standalone("grad_swap_halves")
    pending = [add_sibling(g, r_, "grad_add_sibling") for g, r_ in zip(g_layer, from_sibling, strict=True)]
    reduced[0] = finish_reduce(pending, scatter_packs(pending, "grad_scatter"))
    grad_x = dh[None]

    f4 = reduced[0][0].shape[0] // 6
    rows_of = lambda j, lo, hi: jnp.stack([reduced[l][j][lo:hi] for l in range(depth)])
    ffn_rows = lambda j: rows_of(0, j * f4, (j + 1) * f4)
    grads = {
        "ffn1_wg": ffn_rows(0), "ffn1_wu": ffn_rows(1), "ffn1_wd": ffn_rows(2), "ffn2_wg": ffn_rows(3), "ffn2_wu": ffn_rows(4), "ffn2_wd": ffn_rows(5),
        "m_w_out": rows_of(1, 0, 2 * d4), "sc_w_out": rows_of(1, 2 * d4, 3 * d4), "w_o": rows_of(1, 3 * d4, 4 * d4), "ple_w_gate": rows_of(1, 4 * d4, 5 * d4),
        "w_in": rows_of(2, 0, p4), "ple_w_proj": rows_of(3, 0, None),
    }

    small_names = list(SMALL_SHARDED + SMALL_REPL)
    small_full = [g_final[0] if n == "final_norm" else jnp.stack(gs[n]) for n in small_names]
    small_pack = _pack(small_full, LANE, HALO)
    all8 = all_gather_8(small_pack, "gather_small_grads")
    small_sum = add_parts([all8[k] for k in range(8)], F32, "add_small_grads", tm=256)
    for n, tot in zip(small_names, _unpack(small_sum, [a.shape for a in small_full]), strict=True):
        if n in SMALL_SHARDED:
            cl = wts[n].shape[2]
            grads[n] = lax.dynamic_slice_in_dim(tot, k_me * cl, cl, axis=2)
        else:
            grads[n] = tot.reshape(wts[n].shape)

    delta, new_m, new_v = {}, {}, {}
    for n in BIG:
        view = tr if n in TRANSPOSED else (lambda a: a)
        shp = grads[n].shape
        two = lambda a: a.reshape(-1, shp[-1])
        dl, nm, nv = adamw(two(view(wts[n])), two(grads[n]), two(view(mom[n])), two(view(vel[n])), "adamw_" + "x".join(map(str, shp[1:])))
        grads[n], delta[n], new_m[n], new_v[n] = view(grads[n]), view(dl.reshape(shp)), view(nm.reshape(shp)), view(nv.reshape(shp))
    for n in small_names:
        shp = wts[n].shape
        two = lambda a: a.reshape(-1, shp[-1])
        dl, nm, nv = adamw(two(wts[n]), two(grads[n]), two(mom[n]), two(vel[n]), "adamw_small_" + "x".join(map(str, shp)))
        delta[n], new_m[n], new_v[n] = dl.reshape(shp), nm.reshape(shp), nv.reshape(shp)

    return (loss, grad_x, *[grads[n] for n in ORDER], *[delta[n] for n in ORDER], *[new_m[n] for n in ORDER], *[new_v[n] for n in ORDER])
```

```python
import jax
import jax.numpy as jnp
import numpy as np
from jax import lax
from jax.experimental import pallas as pl
from jax.experimental.pallas import tpu as pltpu

BF = jnp.bfloat16
F32 = jnp.float32
EPS = 1e-6
LANE = 128
HALO = 8
SSM_P = 64
SSM_N = 128
SSM_G = 4
SSM_L = 128
ADAM_LR, ADAM_B1, ADAM_B2, ADAM_EPS, ADAM_WD, ADAM_STEP = 0.001, 0.9, 0.999, 1e-08, 0.01, 10
VMEM_LIMIT = 56 * 1024 * 1024
TILE_ELEMS = 600_000
ADD_TILE_ELEMS = 1_000_000
NCHIP = 4
FFN_SUB = 256
MESH = pl.DeviceIdType.MESH


def _tile(n, cap, mult=LANE):
    best = None
    t = mult
    while t <= min(n, cap):
        if n % t == 0:
            best = t
        t += mult
    return best if best is not None else n


def _row_tile(r, c, mult=16, elems=TILE_ELEMS):
    return _tile(r, max(mult, elems // c // mult * mult), mult)


def _tile2(r, c, mult=16, elems=ADD_TILE_ELEMS):
    tm = _row_tile(r, c, mult, elems)
    tc = c if tm * c <= elems else _tile(c, max(LANE, elems // tm // LANE * LANE))
    return tm, tc


def _params(sem):
    return pltpu.CompilerParams(dimension_semantics=sem, vmem_limit_bytes=VMEM_LIMIT)


def _sigmoid(x):
    return 1.0 / (1.0 + jnp.exp(-x))


def _dot(a, b, ca=1, cb=0, precision=None):
    return lax.dot_general(a, b, (((ca,), (cb,)), ((), ())), precision=precision, preferred_element_type=F32)


def _rms(x, g):
    r = lax.rsqrt(jnp.mean(x * x, axis=-1, keepdims=True) + EPS)
    return x * r * g


def _rms_bwd(x, g, dy):
    r = lax.rsqrt(jnp.mean(x * x, axis=-1, keepdims=True) + EPS)
    xh = x * r
    dxh = dy * g
    dx = r * (dxh - xh * jnp.mean(dxh * xh, axis=-1, keepdims=True))
    return dx, jnp.sum(dy * xh, axis=0, keepdims=True)


def _accumulate(ref, val, first):
    @pl.when(first)
    def _():
        ref[...] = val

    @pl.when(jnp.logical_not(first))
    def _():
        ref[...] += val


RIDER_MID = 1.0


class Rider:
    def __init__(self, ins, out_shapes, n_sems, start, finish, mid=None):
        self.ins, self.out_shapes, self.n_sems, self.start, self.mid, self.finish = list(ins), list(out_shapes), n_sems, start, mid, finish

    def standalone(self, name):
        ni, no = len(self.ins), len(self.out_shapes)

        def body(*refs):
            parts = (refs[:ni], refs[ni:ni + no], *refs[ni + no:])
            self.start(*parts)
            if self.mid is not None:
                self.mid(*parts)
            self.finish(*parts)

        return _comm_call(body, name, self.ins, self.out_shapes, self.n_sems)


def host_call(body, *, name, grid, in_specs, out_specs, out_shape, scratch_shapes, operands, rider=None, aliases=None):
    n_in, n_out = len(in_specs), len(out_specs)
    aliases = aliases or {}
    if rider is None:
        outs = pl.pallas_call(body, name=name, grid=grid, in_specs=in_specs, out_specs=out_specs, out_shape=out_shape, scratch_shapes=scratch_shapes,
                              input_output_aliases=aliases, compiler_params=_params(("arbitrary",) * len(grid)))(*operands)
        return list(outs), []
    ri, ro = len(rider.ins), len(rider.out_shapes)

    def hosted(*refs):
        ins, r_ins = refs[:n_in], refs[n_in:n_in + ri]
        outs, r_outs = refs[n_in + ri:n_in + ri + n_out], refs[n_in + ri + n_out:n_in + ri + n_out + ro]
        scratch, (send_sems, recv_sems) = refs[n_in + ri + n_out + ro:-2], refs[-2:]
        step, total = 0, 1
        for ax, n in enumerate(grid):
            step = step * n + pl.program_id(ax)
            total *= n
        parts = (r_ins, r_outs, send_sems, recv_sems)

        @pl.when(step == 0)
        def _():
            rider.start(*parts)

        if rider.mid is not None:
            @pl.when(step == min(total - 1, int(total * RIDER_MID)))
            def _():
                rider.mid(*parts)

        body(*ins, *outs, *scratch)

        @pl.when(step == total - 1)
        def _():
            rider.finish(*parts)

    outs = pl.pallas_call(
        hosted,
        name=name,
        grid=grid,
        in_specs=list(in_specs) + [ANY] * ri,
        out_specs=list(out_specs) + [ANY] * ro,
        out_shape=list(out_shape) + rider.out_shapes,
        scratch_shapes=list(scratch_shapes) + [pltpu.SemaphoreType.DMA((rider.n_sems,)), pltpu.SemaphoreType.DMA((rider.n_sems,))],
        input_output_aliases=aliases,
        compiler_params=_params(("arbitrary",) * len(grid)),
    )(*operands, *rider.ins)
    return list(outs[:n_out]), list(outs[n_out:])


def mmx(name, a, b, *, grid, a_spec, b_spec, o_spec, o_shape, o_dtype, ca, cb, acc_shape=None, rider=None):
    nk = grid[-1] if acc_shape is not None else 1

    def body(a_ref, b_ref, o_ref, *acc):
        p = _dot(a_ref[...].astype(BF), b_ref[...].astype(BF), ca, cb)
        if nk == 1:
            o_ref[...] = p.astype(o_ref.dtype)
        else:
            kk = pl.program_id(len(grid) - 1)
            _accumulate(acc[0], p, kk == 0)

            @pl.when(kk == nk - 1)
            def _():
                o_ref[...] = acc[0][...].astype(o_ref.dtype)

    if rider is not None:
        (out,), r_outs = host_call(body, name=name, grid=grid, in_specs=[a_spec, b_spec], out_specs=[o_spec], out_shape=[jax.ShapeDtypeStruct(o_shape, o_dtype)],
                                   scratch_shapes=[pltpu.VMEM(acc_shape, F32)] if nk > 1 else [], operands=(a, b), rider=rider)
        return out, r_outs
    sem = ("parallel",) * (len(grid) - 1) + ("arbitrary" if nk > 1 else "parallel",)
    return pl.pallas_call(
        body,
        name=name,
        grid=grid,
        in_specs=[a_spec, b_spec],
        out_specs=o_spec,
        out_shape=jax.ShapeDtypeStruct(o_shape, o_dtype),
        scratch_shapes=[pltpu.VMEM(acc_shape, F32)] if nk > 1 else [],
        compiler_params=_params(sem),
    )(a, b)


def mm(a, b, *, ta=False, tb=False, out_dtype=F32, name, tm_cap=1024, tn_cap=1024, tk_cap=4096, rider=None):
    m, k = (a.shape[1], a.shape[0]) if ta else a.shape
    n = b.shape[0] if tb else b.shape[1]
    assert (b.shape[1] if tb else b.shape[0]) == k
    tm, tn, tk = _tile(m, tm_cap), _tile(n, tn_cap), _tile(k, tk_cap)
    nk = k // tk
    a_spec = pl.BlockSpec((tk, tm), lambda i, j, kk: (kk, i)) if ta else pl.BlockSpec((tm, tk), lambda i, j, kk: (i, kk))
    b_spec = pl.BlockSpec((tn, tk), lambda i, j, kk: (j, kk)) if tb else pl.BlockSpec((tk, tn), lambda i, j, kk: (kk, j))
    return mmx(name, a, b, grid=(m // tm, n // tn, nk), a_spec=a_spec, b_spec=b_spec, o_spec=pl.BlockSpec((tm, tn), lambda i, j, kk: (i, j)),
               o_shape=(m, n), o_dtype=out_dtype, ca=0 if ta else 1, cb=1 if tb else 0, acc_shape=(tm, tn) if nk > 1 else None, rider=rider)


class Window:
    def __init__(self, out, block, cols, buf=None):
        self.out, self.block, self.cols, self.buf = out, block, cols, buf


def ew(fn, rows, vecs, out_rows, out_red=(), *, tm, name, prev_halo=(), next_halo=(), window=None):
    t = rows[0].shape[0]
    tm = min(tm, t)
    nt = t // tm
    assert t % tm == 0 and (tm % HALO == 0 or (tm == t and not prev_halo and not next_halo))
    nr, nv, npv, nnx, nor = len(rows), len(vecs), len(prev_halo), len(next_halo), len(out_rows)
    hb = tm // HALO
    n_in = nr + nv + npv + nnx
    passed = window is not None and window.buf is not None

    def body(*refs):
        i = pl.program_id(0)
        ins = [r[...].astype(F32) for r in refs[:n_in]]
        outs = refs[n_in + passed:]
        o_rows, o_red = fn(i, nt, ins[:nr], ins[nr:nr + nv], ins[nr + nv:nr + nv + npv], ins[nr + nv + npv:])
        for ref, val in zip(outs[:nor], o_rows, strict=True):
            ref[...] = val.astype(ref.dtype)
        for ref, val in zip(outs[nor:], o_red, strict=True):
            _accumulate(ref, val, i == 0)

    in_specs = [pl.BlockSpec((tm, r.shape[1]), lambda i: (i, 0)) for r in rows]
    in_specs += [pl.BlockSpec(v.shape, lambda i: (0, 0)) for v in vecs]
    in_specs += [pl.BlockSpec((HALO, rows[k].shape[1]), lambda i: (jnp.maximum(i * hb - 1, 0), 0)) for k in prev_halo]
    in_specs += [pl.BlockSpec((HALO, rows[k].shape[1]), lambda i: (jnp.minimum((i + 1) * hb, t // HALO - 1), 0)) for k in next_halo]
    out_specs = [pl.BlockSpec((tm, c), lambda i: (i, 0)) for c, _ in out_rows]
    out_specs += [pl.BlockSpec(s, lambda i: (0, 0)) for s in out_red]
    out_shape = [jax.ShapeDtypeStruct((t, c), d) for c, d in out_rows] + [jax.ShapeDtypeStruct(s, F32) for s in out_red]
    operands = [*rows, *vecs, *[rows[k] for k in prev_halo], *[rows[k] for k in next_halo]]
    aliases = {}
    if window is not None:
        c, dt_ = out_rows[window.out]
        out_specs[window.out] = pl.BlockSpec((tm, c), lambda i: (i, window.block))
        out_shape[window.out] = jax.ShapeDtypeStruct((t, window.cols), dt_)
        if passed:
            in_specs.append(ANY)
            operands.append(window.buf)
            aliases = {n_in: window.out}
    return pl.pallas_call(
        body,
        name=name,
        grid=(nt,),
        in_specs=in_specs,
        out_specs=out_specs,
        out_shape=out_shape,
        input_output_aliases=aliases,
        compiler_params=_params(("arbitrary",) if out_red else ("parallel",)),
    )(*operands)


def _shift_down(x, prev, j):
    if j == 0:
        return x
    r = pltpu.roll(x, j, 0)
    rh = pltpu.roll(prev, j, 0)
    row = lax.broadcasted_iota(jnp.int32, (HALO, x.shape[1]), 0)
    head = jnp.where(row < j, rh, r[:HALO])
    return jnp.concatenate([head, r[HALO:]], axis=0)


def _shift_up(x, nxt, j):
    if j == 0:
        return x
    n = x.shape[0]
    r = pltpu.roll(x, n - j, 0)
    rh = pltpu.roll(nxt, HALO - j, 0)
    row = lax.broadcasted_iota(jnp.int32, (HALO, x.shape[1]), 0)
    tail = jnp.where(row >= HALO - j, rh, r[n - HALO:])
    return jnp.concatenate([r[: n - HALO], tail], axis=0)


def _conv_fwd(x, prev, w):
    kk = w.shape[0]
    acc = None
    for k in range(kk):
        term = w[k:k + 1, :] * _shift_down(x, prev, kk - 1 - k)
        acc = term if acc is None else acc + term
    return acc


def ffn_up(h, g, wf, rider=None):
    t, d = h.shape
    f4 = wf.shape[1] // 2
    tm = _tile(t, 1024)
    sub = _tile(tm, FFN_SUB, 16)

    def body(h_ref, g_ref, wg_ref, wu_ref, ab_ref, s_ref, n_ref):
        @pl.when(pl.program_id(1) == 0)
        def _():
            n_ref[...] = _rms(h_ref[...], g_ref[...]).astype(BF)

        for r in range(tm // sub):
            rows = slice(r * sub, (r + 1) * sub)
            n = n_ref[rows, :]
            a = _dot(n, wg_ref[...], 1, 1)
            b = _dot(n, wu_ref[...], 1, 1)
            ab_ref[0, rows, :] = a.astype(BF)
            ab_ref[1, rows, :] = b.astype(BF)
            s_ref[rows, :] = (a * _sigmoid(a) * b).astype(BF)

    wspec = lambda ib: pl.BlockSpec((None, f4, d), lambda i, j: (j, ib, 0))
    return host_call(
        body,
        name="ffn_up",
        grid=(t // tm, NCHIP),
        in_specs=[pl.BlockSpec((tm, d), lambda i, j: (i, 0)), pl.BlockSpec((1, d), lambda i, j: (0, 0)), wspec(0), wspec(1)],
        out_specs=[pl.BlockSpec((2, None, tm, f4), lambda i, j: (0, j, i, 0)), pl.BlockSpec((None, tm, f4), lambda i, j: (j, i, 0)),
                   pl.BlockSpec((tm, d), lambda i, j: (i, 0))],
        out_shape=[jax.ShapeDtypeStruct((2, NCHIP, t, f4), BF), jax.ShapeDtypeStruct((NCHIP, t, f4), BF), jax.ShapeDtypeStruct((t, d), BF)],
        scratch_shapes=[],
        operands=(h, g, wf, wf),
        rider=rider,
    )


def ffn_down(s4, wf, h, rider=None):
    t, d = h.shape
    f4 = s4.shape[2]
    tm = _tile(t, 512)

    def body(s_ref, w_ref, h_ref, o_ref):
        acc = _dot(s_ref[0], w_ref[0])
        for k in range(1, NCHIP):
            acc = acc + _dot(s_ref[k], w_ref[k])
        o_ref[...] = h_ref[...] + 0.5 * acc

    (out,), r_outs = host_call(
        body,
        name="ffn_down",
        grid=(t // tm,),
        in_specs=[pl.BlockSpec((NCHIP, tm, f4), lambda i: (0, i, 0)), pl.BlockSpec((NCHIP, f4, d), lambda i: (0, 0, 0)), pl.BlockSpec((tm, d), lambda i: (i, 0))],
        out_specs=[pl.BlockSpec((tm, d), lambda i: (i, 0))],
        out_shape=[jax.ShapeDtypeStruct((t, d), F32)],
        scratch_shapes=[],
        operands=(s4, wf, h),
        rider=rider,
    )
    return out, r_outs


def ffn_bwd(dho, ab, wf, wd, rider=None):
    t, d = dho.shape
    f4 = wd.shape[1]
    tm = _tile(t, 1024)
    sub = _tile(tm, FFN_SUB, 16)

    def body(dho_ref, ab_ref, wg_ref, wu_ref, wd_ref, dn_ref, s_ref, dab_ref, do_sc):
        j = pl.program_id(1)

        @pl.when(j == 0)
        def _():
            do_sc[...] = (0.5 * dho_ref[...]).astype(BF)
            dn_ref[...] = jnp.zeros_like(dn_ref)

        for r in range(tm // sub):
            rows = slice(r * sub, (r + 1) * sub)
            ds = _dot(do_sc[rows, :], wd_ref[...], 1, 1)
            av, bv = ab_ref[0, rows, :].astype(F32), ab_ref[1, rows, :].astype(F32)
            sig = _sigmoid(av)
            sl = av * sig
            s_ref[rows, :] = (sl * bv).astype(BF)
            da = (ds * bv * (sig * (1.0 + av * (1.0 - sig)))).astype(BF)
            db = (ds * sl).astype(BF)
            dab_ref[0, rows, :] = da
            dab_ref[1, rows, :] = db
            dn_ref[rows, :] += _dot(da, wg_ref[...]) + _dot(db, wu_ref[...])

    row = lambda c: pl.BlockSpec((tm, c), lambda i, j: (i, 0))
    wspec = lambda ib: pl.BlockSpec((None, f4, d), lambda i, j: (j, ib, 0))
    ab_spec = pl.BlockSpec((2, None, tm, f4), lambda i, j: (0, j, i, 0))
    return host_call(
        body,
        name="ffn_bwd",
        grid=(t // tm, NCHIP),
        in_specs=[row(d), ab_spec, wspec(0), wspec(1), wspec(0)],
        out_specs=[row(d), pl.BlockSpec((None, tm, f4), lambda i, j: (j, i, 0)), ab_spec, row(d)],
        out_shape=[jax.ShapeDtypeStruct((t, d), F32), jax.ShapeDtypeStruct((NCHIP, t, f4), BF), jax.ShapeDtypeStruct((2, NCHIP, t, f4), BF),
                   jax.ShapeDtypeStruct((t, d), BF)],
        scratch_shapes=[],
        operands=(dho, ab, wf, wf, wd),
        rider=rider,
    )


def ffn_wgrads(n, do, s4, dab):
    t, d = n.shape
    f4 = s4.shape[2]
    g_in = mmx("g_ffn_in", dab, n, grid=(2, NCHIP, 1), a_spec=pl.BlockSpec((None, None, t, f4), lambda wh, k, j: (wh, k, 0, 0)),
               b_spec=pl.BlockSpec((t, d), lambda wh, k, j: (0, 0)), o_spec=pl.BlockSpec((None, None, f4, d), lambda wh, k, j: (k, wh, 0, 0)),
               o_shape=(NCHIP, 2, f4, d), o_dtype=BF, ca=0, cb=0)
    g_out = mmx("g_ffn_out", s4, do, grid=(NCHIP, 1), a_spec=pl.BlockSpec((None, t, f4), lambda k, j: (k, 0, 0)),
                b_spec=pl.BlockSpec((t, d), lambda k, j: (0, 0)), o_spec=pl.BlockSpec((None, f4, d), lambda k, j: (k, 0, 0)),
                o_shape=(NCHIP, f4, d), o_dtype=BF, ca=0, cb=0)
    return jnp.concatenate([g_in.reshape(NCHIP, 2 * f4, d), g_out], axis=1)


def norm_cast(h, g):
    def fn(i, nt, rows, vecs, prevs, nexts):
        return [_rms(rows[0], vecs[0])], []
    return ew(fn, [h], [g], [(h.shape[1], BF)], tm=512, name="norm_cast")[0]


def _zero_if(cond, x):
    return jnp.where(cond, jnp.zeros_like(x), x)


def conv_a_fwd(sc3, w_sc):
    d = sc3.shape[1] // 3

    def fn(i, nt, rows, vecs, prevs, nexts):
        x, pv = rows[0], _zero_if(i == 0, prevs[0])
        v = x[:, d:2 * d] * x[:, 2 * d:]
        vp = pv[:, d:2 * d] * pv[:, 2 * d:]
        return [x[:, :d] * _conv_fwd(v, vp, vecs[0])], []

    return ew(fn, [sc3], [w_sc], [(d, BF)], tm=256, name="conv_a_fwd", prev_halo=(0,))[0]


def _softplus(x):
    e = jnp.exp(-jnp.abs(x))
    return jnp.maximum(x, 0.0) + jnp.where(e < 1e-4, e - 0.5 * e * e, jnp.log(1.0 + e))


def conv_m_fwd(xbc_raw, dt_raw, w_mc, b_mc, dt_bias):
    def fn(i, nt, rows, vecs, prevs, nexts):
        pre = _conv_fwd(rows[0], _zero_if(i == 0, prevs[0]), vecs[0]) + vecs[1]
        return [pre * _sigmoid(pre), _softplus(rows[1] + vecs[2])], []

    return ew(fn, [xbc_raw, dt_raw], [w_mc, b_mc, dt_bias], [(xbc_raw.shape[1], F32), (LANE, F32)], tm=256, name="conv_m_fwd",
              prev_halo=(0,))


def conv_m_bwd1(dxbc, xbc_raw, ddt, dt_raw, w_mc, b_mc, dt_bias, window):
    def fn(i, nt, rows, vecs, prevs, nexts):
        pre = _conv_fwd(rows[1], _zero_if(i == 0, prevs[0]), vecs[0]) + vecs[1]
        sig = _sigmoid(pre)
        dpre = rows[0] * (sig * (1.0 + pre * (1.0 - sig)))
        ddr = rows[2] * _sigmoid(rows[3] + vecs[2])
        return [dpre, ddr], [jnp.sum(ddr, axis=0, keepdims=True)]

    return ew(fn, [dxbc, xbc_raw, ddt, dt_raw], [w_mc, b_mc, dt_bias], [(dxbc.shape[1], F32), (LANE, BF)], [(1, LANE)], tm=256,
              name="conv_m_bwd1", prev_halo=(1,), window=window)


def conv_bwd2(dpre, x, w, name, window):
    kk = w.shape[0]

    def fn(i, nt, rows, vecs, prevs, nexts):
        dp, xv = rows[0], rows[1]
        nx = _zero_if(i == nt - 1, nexts[0])
        dx = None
        dws = []
        for k in range(kk):
            up = _shift_up(dp, nx, kk - 1 - k)
            term = vecs[0][k:k + 1, :] * up
            dx = term if dx is None else dx + term
            dws.append(jnp.sum(up * xv, axis=0, keepdims=True))
        return [dx], [jnp.concatenate(dws, axis=0), jnp.sum(dp, axis=0, keepdims=True)]

    c = x.shape[1]
    return ew(fn, [dpre, x], [w], [(c, BF)], [(kk, c), (1, c)], tm=256, name=name, next_halo=(0,), window=window)


def conv_a_bwd1(dya, sc3, w_sc, window):
    d = sc3.shape[1] // 3

    def fn(i, nt, rows, vecs, prevs, nexts):
        x, pv = rows[1], _zero_if(i == 0, prevs[0])
        v = x[:, d:2 * d] * x[:, 2 * d:]
        vp = pv[:, d:2 * d] * pv[:, 2 * d:]
        return [rows[0] * x[:, :d], rows[0] * _conv_fwd(v, vp, vecs[0]), v], []

    return ew(fn, [dya, sc3], [w_sc], [(d, F32), (d, BF), (d, F32)], tm=256, name="conv_a_bwd1", prev_halo=(1,), window=window)


def conv_a_bwd2(dcv, v, sc3, w_sc, window):
    d = v.shape[1]
    kk = w_sc.shape[0]

    def fn(i, nt, rows, vecs, prevs, nexts):
        dp, vv, x = rows
        nx = _zero_if(i == nt - 1, nexts[0])
        dv = None
        dws = []
        for k in range(kk):
            up = _shift_up(dp, nx, kk - 1 - k)
            term = vecs[0][k:k + 1, :] * up
            dv = term if dv is None else dv + term
            dws.append(jnp.sum(up * vv, axis=0, keepdims=True))
        return [jnp.concatenate([dv * x[:, 2 * d:], dv * x[:, d:2 * d]], axis=1)], [jnp.concatenate(dws, axis=0)]

    return ew(fn, [dcv, v, sc3], [w_sc], [(2 * d, BF)], [(kk, d)], tm=256, name="conv_a_bwd2", next_halo=(0,), window=window)


def _xdot(a, b, passes, split_lhs, ca=1, cb=0):
    parts, r = [], (a if split_lhs else b)
    for _ in range(passes):
        piece = r.astype(BF)
        parts.append(piece)
        r = r - piece.astype(F32)
    other = (b if split_lhs else a).astype(BF)
    acc = None
    for piece in parts:
        term = _dot(piece, other, ca, cb) if split_lhs else _dot(other, piece, ca, cb)
        acc = term if acc is None else acc + term
    return acc


def _ssd_common(xbc_ref, dt_ref, alog_ref, e_ref, w):
    ll = SSM_L
    xs = xbc_ref[:, 0:w]
    dtv = dt_ref[...]
    a_row = -jnp.exp(alog_ref[...])
    a = dtv * a_row
    row = lax.broadcasted_iota(jnp.int32, (ll, ll), 0)
    col = lax.broadcasted_iota(jnp.int32, (ll, ll), 1)
    tril = (row >= col).astype(F32)
    triu = (row <= col).astype(F32)
    acl = _xdot(tril, a, 3, False)
    acl_t = _xdot(a, triu, 3, True, 0, 0)
    e = e_ref[...]
    aclx = _xdot(acl, e, 3, True)
    dtx = _xdot(dtv, e, 2, True)
    last = aclx[ll - 1:ll, :]
    e_in = jnp.exp(aclx)
    e_end = jnp.exp(last - aclx)
    e_tot = jnp.exp(last)
    x = xs * dtx
    return dict(xs=xs, dtv=dtv, a_row=a_row, a=a, row=row, col=col, triu=triu, acl=acl, acl_t=acl_t, dtx=dtx, e_in=e_in, e_end=e_end,
                e_tot=e_tot, x=x)


def _decay(q, hh):
    diff = q["acl"][:, hh:hh + 1] - q["acl_t"][hh:hh + 1, :]
    return jnp.exp(jnp.where(q["row"] >= q["col"], diff, -jnp.inf))


def ssd_fwd(xbc, dt, z, a_log, d_exp, m_norm, e_mat, rider=None):
    t = xbc.shape[0]
    w = z.shape[1]
    gn = SSM_G * SSM_N
    gw = w // SSM_G
    ll, nn = SSM_L, SSM_N
    nc = t // ll
    cw = xbc.shape[1]

    def body(xbc_ref, dt_ref, z_ref, alog_ref, dexp_ref, mn_ref, e_ref, yn_ref, y_ref, sp_ref, s_sc):
        c = pl.program_id(0)

        @pl.when(c == 0)
        def _():
            s_sc[...] = jnp.zeros_like(s_sc)

        q = _ssd_common(xbc_ref, dt_ref, alog_ref, e_ref, w)
        xb = q["x"].astype(BF)
        xsb = (q["x"] * q["e_end"]).astype(BF)
        sp = s_sc[...]
        sp_ref[0] = sp
        spb = sp.astype(BF)
        lane = lax.broadcasted_iota(jnp.int32, (ll, LANE), 1)
        for g in range(SSM_G):
            lo = g * gw
            bg = xbc_ref[:, w + g * nn:w + (g + 1) * nn].astype(BF)
            cg = xbc_ref[:, w + gn + g * nn:w + gn + (g + 1) * nn].astype(BF)
            yoff = _dot(cg, spb[:, lo:lo + gw]) * q["e_in"][:, lo:lo + gw]
            s_sc[:, lo:lo + gw] = sp[:, lo:lo + gw] * q["e_tot"][:, lo:lo + gw] + _dot(bg, xsb[:, lo:lo + gw], 0, 0)
            cb = _dot(cg, bg, 1, 1)
            for pr in range(gw // LANE):
                l0 = lo + pr * LANE
                xp = xb[:, l0:l0 + LANE]
                ys = []
                for hh in (l0 // SSM_P, l0 // SSM_P + 1):
                    wm = (cb * _decay(q, hh)).astype(BF)
                    ys.append(_dot(wm, xp))
                ydiag = jnp.where(lane < SSM_P, ys[0], ys[1])
                y_ref[:, l0:l0 + LANE] = ydiag + yoff[:, pr * LANE:(pr + 1) * LANE] + dexp_ref[:, l0:l0 + LANE] * q["xs"][:, l0:l0 + LANE]
        zv = z_ref[...].astype(F32)
        yz = y_ref[...] * (zv * _sigmoid(zv))
        for g in range(SSM_G):
            lo = g * gw
            yn_ref[:, lo:lo + gw] = _rms(yz[:, lo:lo + gw], mn_ref[:, lo:lo + gw]).astype(BF)

    vec = lambda s: pl.BlockSpec(s, lambda c: (0, 0))
    return host_call(
        body,
        name="ssd_fwd",
        grid=(nc,),
        in_specs=[
            pl.BlockSpec((ll, cw), lambda c: (c, 0)), pl.BlockSpec((ll, LANE), lambda c: (c, 0)), pl.BlockSpec((ll, w), lambda c: (c, 0)),
            vec((1, LANE)), vec((1, w)), vec((1, w)), vec((LANE, w)),
        ],
        out_specs=[pl.BlockSpec((ll, w), lambda c: (c, 0)), pl.BlockSpec((ll, w), lambda c: (c, 0)), pl.BlockSpec((1, nn, w), lambda c: (c, 0, 0))],
        out_shape=[jax.ShapeDtypeStruct((t, w), BF), jax.ShapeDtypeStruct((t, w), F32), jax.ShapeDtypeStruct((nc, nn, w), F32)],
        scratch_shapes=[pltpu.VMEM((nn, w), F32)],
        operands=(xbc, dt, z, a_log, d_exp, m_norm, e_mat),
        rider=rider,
    )


def ssd_bwd(dyn, y, z, xbc, dt, sprev, a_log, d_exp, m_norm, e_mat, et_mat, window, rider=None):
    t = xbc.shape[0]
    w = z.shape[1]
    gn = SSM_G * SSM_N
    gw = w // SSM_G
    ll, nn = SSM_L, SSM_N
    nc = t // ll
    cw = xbc.shape[1]

    def body(dyn_ref, y_ref, z_ref, xbc_ref, dt_ref, sp_ref, alog_ref, dexp_ref, mn_ref, e_ref, et_ref,
             dz_ref, dxbc_ref, ddt_ref, dmn_ref, dd_ref, dal_ref, ds_sc, dy_sc, dx_sc):
        step = pl.program_id(0)

        @pl.when(step == 0)
        def _():
            ds_sc[...] = jnp.zeros_like(ds_sc)

        zv, yv = z_ref[...].astype(F32), y_ref[...]
        sg = _sigmoid(zv)
        sz = zv * sg
        yz = yv * sz
        dmn = []
        for g in range(SSM_G):
            lo = g * gw
            dseg, dmn_g = _rms_bwd(yz[:, lo:lo + gw], mn_ref[:, lo:lo + gw], dyn_ref[:, lo:lo + gw])
            dy_sc[:, lo:lo + gw] = dseg
            dmn.append(dmn_g)
        dmn = jnp.concatenate(dmn, axis=1)
        dyz = dy_sc[...]
        dz_ref[...] = (dyz * yv * (sg * (1.0 + zv * (1.0 - sg)))).astype(BF)
        dy = dyz * sz

        q = _ssd_common(xbc_ref, dt_ref, alog_ref, e_ref, w)
        x = q["x"]
        xb = x.astype(BF)
        xsb = (x * q["e_end"]).astype(BF)
        sp = sp_ref[0]
        spb = sp.astype(BF)
        dsn = ds_sc[...]
        dsnb = dsn.astype(BF)
        dyb = dy.astype(BF)
        lane = lax.broadcasted_iota(jnp.int32, (ll, LANE), 1)
        lane1 = lax.broadcasted_iota(jnp.int32, (1, LANE), 1)
        sub1 = lax.broadcasted_iota(jnp.int32, (LANE, 1), 0)
        dacl = jnp.zeros((ll, LANE), F32)
        dacl_t = jnp.zeros((LANE, ll), F32)
        d_ein, d_eend, d_etot = [], [], []
        for g in range(SSM_G):
            lo = g * gw
            sl = slice(lo, lo + gw)
            bg = xbc_ref[:, w + g * nn:w + (g + 1) * nn].astype(BF)
            cg = xbc_ref[:, w + gn + g * nn:w + gn + (g + 1) * nn].astype(BF)
            zg = _dot(cg, spb[:, sl])
            dzz = (dy[:, sl] * q["e_in"][:, sl]).astype(BF)
            d_ein.append(dy[:, sl] * zg)
            dcg = _dot(dzz, spb[:, sl], 1, 1)
            ds_sc[:, sl] = _dot(cg, dzz, 0, 0) + dsn[:, sl] * q["e_tot"][:, sl]
            d_etot.append(jnp.sum(dsn[:, sl] * sp[:, sl], axis=0, keepdims=True))
            dbg = _dot(xsb[:, sl], dsnb[:, sl], 1, 1)
            dxs_g = _dot(bg, dsnb[:, sl])
            d_eend.append(dxs_g * x[:, sl])
            cb = _dot(cg, bg, 1, 1)
            dcb = jnp.zeros((ll, ll), F32)
            for pr in range(gw // LANE):
                l0 = lo + pr * LANE
                xp = xb[:, l0:l0 + LANE]
                dyp = dyb[:, l0:l0 + LANE]
                dxp = []
                for hi, hh in enumerate((l0 // SSM_P, l0 // SSM_P + 1)):
                    lm = _decay(q, hh)
                    wm = (cb * lm).astype(BF)
                    in_head = (lane < SSM_P) if hi == 0 else (lane >= SSM_P)
                    dwm = _dot(jnp.where(in_head, dyp, jnp.zeros_like(dyp)), xp, 1, 1)
                    dxp.append(_dot(wm, dyp, 0, 0))
                    dlm = dwm * lm
                    dcb = dcb + dlm
                    dd = dlm * cb
                    dacl = dacl + jnp.sum(dd, axis=1, keepdims=True) * (lane1 == hh).astype(F32)
                    dacl_t = dacl_t + (sub1 == hh).astype(F32) * jnp.sum(dd, axis=0, keepdims=True)
                dx_sc[:, l0:l0 + LANE] = jnp.where(lane < SSM_P, dxp[0], dxp[1]) + dxs_g[:, pr * LANE:(pr + 1) * LANE] * q["e_end"][:, l0:l0 + LANE]
            dcbb = dcb.astype(BF)
            dxbc_ref[:, w + g * nn:w + (g + 1) * nn] = dbg + _dot(dcbb, cg, 0, 0)
            dxbc_ref[:, w + gn + g * nn:w + gn + (g + 1) * nn] = dcg + _dot(dcbb, bg)
        d_ein = jnp.concatenate(d_ein, axis=1) * q["e_in"]
        d_eend = jnp.concatenate(d_eend, axis=1) * q["e_end"]
        d_etot = jnp.concatenate(d_etot, axis=1) * q["e_tot"]
        et = et_ref[...]
        last_add = jnp.sum(d_eend, axis=0, keepdims=True) + d_etot
        last_add = _xdot(jnp.broadcast_to(last_add, (HALO, w)), et, 2, True)[0:1]
        row1 = lax.broadcasted_iota(jnp.int32, (ll, LANE), 0)
        dacl = dacl + _xdot(d_ein - d_eend, et, 2, True) + jnp.where(row1 == ll - 1, last_add, 0.0)
        da = _xdot(q["triu"], dacl, 2, False) - _xdot(q["triu"], dacl_t, 2, False, 1, 1)
        dxv = dx_sc[...]
        dxbc_ref[:, 0:w] = dexp_ref[...] * dy + dxv * q["dtx"]
        ddt_ref[...] = _xdot(dxv * q["xs"], et, 2, True) + da * q["a_row"]
        dal = jnp.sum(da * q["dtv"], axis=0, keepdims=True) * q["a_row"]
        ddv = jnp.sum(dy * q["xs"], axis=0, keepdims=True)
        ddv = _xdot(jnp.broadcast_to(ddv, (HALO, w)), et, 2, True)[0:1]
        _accumulate(dmn_ref, dmn, step == 0)
        _accumulate(dd_ref, ddv, step == 0)
        _accumulate(dal_ref, dal, step == 0)

    rev = lambda c_: pl.BlockSpec((ll, c_), lambda s: (nc - 1 - s, 0))
    vec = lambda s_: pl.BlockSpec(s_, lambda s: (0, 0))
    n_in = 11

    def body_skipping_buffer(*refs):
        body(*refs[:n_in], *refs[n_in + 1:])

    return host_call(
        body_skipping_buffer,
        name="ssd_bwd",
        grid=(nc,),
        in_specs=[
            rev(w), rev(w), rev(w), rev(cw), rev(LANE), pl.BlockSpec((1, nn, w), lambda s: (nc - 1 - s, 0, 0)),
            vec((1, LANE)), vec((1, w)), vec((1, w)), vec((LANE, w)), vec((w, LANE)), ANY,
        ],
        out_specs=[pl.BlockSpec((ll, w), lambda s: (nc - 1 - s, window.block)), rev(cw), rev(LANE), vec((1, w)), vec((1, LANE)), vec((1, LANE))],
        out_shape=[
            jax.ShapeDtypeStruct((t, window.cols), BF), jax.ShapeDtypeStruct((t, cw), F32), jax.ShapeDtypeStruct((t, LANE), F32),
            jax.ShapeDtypeStruct((1, w), F32), jax.ShapeDtypeStruct((1, LANE), F32), jax.ShapeDtypeStruct((1, LANE), F32),
        ],
        scratch_shapes=[pltpu.VMEM((nn, w), F32), pltpu.VMEM((ll, w), F32), pltpu.VMEM((ll, w), F32)],
        operands=(dyn, y, z, xbc, dt, sprev, a_log, d_exp, m_norm, e_mat, et_mat, window.buf),
        rider=rider,
        aliases={n_in: 0},
    )


def _w1024_spec(d, nblk, iblk):
    r = nblk * (d // NCHIP)
    return pl.BlockSpec((NCHIP, r, d), lambda i: (0, iblk // nblk, 0))


def _whole(ref):
    v = ref[...]
    return v.reshape(v.shape[0] * v.shape[1], v.shape[2])


def mix_out_fwd(ya_in, yn, gates, h, w1024, rider=None):
    t, d = h.shape
    tm = _tile(t, 256)

    def body(ya_ref, yn_ref, g_ref, h_ref, wm_ref, wa_ref, wo_ref, ho_ref, oa_ref, om_ref, mg_ref):
        y_a = _dot(ya_ref[...], _whole(wa_ref))
        y_m = _dot(yn_ref[...], _whole(wm_ref))
        oa_ref[...] = y_a
        om_ref[...] = y_m
        gv = g_ref[...].astype(F32)
        mg = (_sigmoid(gv[:, :d]) * y_a + _sigmoid(gv[:, d:]) * y_m).astype(BF)
        mg_ref[...] = mg
        ho_ref[...] = h_ref[...] + _dot(mg, _whole(wo_ref))

    row = lambda c: pl.BlockSpec((tm, c), lambda i: (i, 0))
    return host_call(
        body,
        name="mix_out_fwd",
        grid=(t // tm,),
        in_specs=[row(d), row(2 * d), row(2 * d), row(d), _w1024_spec(d, 2, 0), _w1024_spec(d, 1, 2), _w1024_spec(d, 1, 3)],
        out_specs=[row(d), row(d), row(d), row(d)],
        out_shape=[jax.ShapeDtypeStruct((t, d), F32), jax.ShapeDtypeStruct((t, d), F32), jax.ShapeDtypeStruct((t, d), F32),
                   jax.ShapeDtypeStruct((t, d), BF)],
        scratch_shapes=[],
        operands=(ya_in, yn, gates, h, w1024, w1024, w1024),
        rider=rider,
    )


def mix_out_bwd(dh, gates, y_a, y_m, w1024, cols):
    t, d = dh.shape
    tm = _tile(t, 256)

    def body(dh_ref, g_ref, ya_ref, ym_ref, wm_ref, wa_ref, wo_ref, dg_ref, dya_ref, dyn_ref, da_ref, dm_ref):
        dmg = _dot(dh_ref[...].astype(BF), _whole(wo_ref), 1, 1)
        gv = g_ref[...].astype(F32)
        sa, sm = _sigmoid(gv[:, :d]), _sigmoid(gv[:, d:])
        dg_ref[:, :d] = (dmg * ya_ref[...] * sa * (1.0 - sa)).astype(BF)
        dg_ref[:, d:] = (dmg * ym_ref[...] * sm * (1.0 - sm)).astype(BF)
        da = (dmg * sa).astype(BF)
        dm = (dmg * sm).astype(BF)
        da_ref[...] = da
        dm_ref[...] = dm
        dya_ref[...] = _dot(da, _whole(wa_ref), 1, 1)
        dyn_ref[...] = _dot(dm, _whole(wm_ref), 1, 1)

    row = lambda c: pl.BlockSpec((tm, c), lambda i: (i, 0))
    return pl.pallas_call(
        body,
        name="mix_out_bwd",
        grid=(t // tm,),
        in_specs=[row(d), row(2 * d), row(d), row(d), _w1024_spec(d, 2, 0), _w1024_spec(d, 1, 2), _w1024_spec(d, 1, 3)],
        out_specs=[row(2 * d), row(d), row(2 * d), row(d), row(d)],
        out_shape=[jax.ShapeDtypeStruct((t, cols), BF), jax.ShapeDtypeStruct((t, d), F32), jax.ShapeDtypeStruct((t, 2 * d), F32),
                   jax.ShapeDtypeStruct((t, d), BF), jax.ShapeDtypeStruct((t, d), BF)],
        compiler_params=_params(("parallel",)),
    )(dh, gates, y_a, y_m, w1024, w1024, w1024)


def proj_in_bwd(dproj, w_in_p, dh, h, g, rider=None):
    t, kdim = dproj.shape
    d = h.shape[1]
    tm, tk = _tile(t, 512), _tile(kdim, 4096)
    nk = kdim // tk

    def body(a_ref, b_ref, dh_ref, h_ref, g_ref, o_ref, dg_ref, acc):
        i, kk = pl.program_id(0), pl.program_id(1)
        _accumulate(acc, _dot(a_ref[...], b_ref[...]), kk == 0)

        @pl.when(kk == nk - 1)
        def _():
            dx, dg = _rms_bwd(h_ref[...], g_ref[...], acc[...])
            o_ref[...] = dh_ref[...] + dx
            _accumulate(dg_ref, dg, i == 0)

    row = pl.BlockSpec((tm, d), lambda i, kk: (i, 0))
    vec = pl.BlockSpec((1, d), lambda i, kk: (0, 0))
    (out, dg), r_outs = host_call(
        body,
        name="d_proj_in",
        grid=(t // tm, nk),
        in_specs=[pl.BlockSpec((tm, tk), lambda i, kk: (i, kk)), pl.BlockSpec((tk, d), lambda i, kk: (kk, 0)), row, row, vec],
        out_specs=[row, vec],
        out_shape=[jax.ShapeDtypeStruct((t, d), F32), jax.ShapeDtypeStruct((1, d), F32)],
        scratch_shapes=[pltpu.VMEM((tm, d), F32)],
        operands=(dproj, w_in_p, dh, h, g),
        rider=rider,
    )
    return out, dg, r_outs


def norm_bwd_add(dh, h, g, dn):
    def fn(i, nt, rows, vecs, prevs, nexts):
        dx, dg = _rms_bwd(rows[1], vecs[0], rows[2])
        return [rows[0] + dx], [dg]
    d = h.shape[1]
    return ew(fn, [dh, h, dn], [g], [(d, F32)], [(1, d)], tm=1024, name="norm_bwd_add")


def _pe(p, wpp_ref):
    pb = p.astype(BF)
    return jnp.concatenate([_dot(pb, wpp_ref[k]) for k in range(NCHIP)], axis=1)


def ple_fwd(h, g, p, w1024, wpp):
    t, d = h.shape
    tm = _tile(t, 512)

    def body(h_ref, g_ref, p_ref, wg_ref, wp_ref, ho_ref):
        hv = h_ref[...]
        gate = _sigmoid(_dot(_rms(hv, g_ref[...]).astype(BF), _whole(wg_ref)))
        ho_ref[...] = hv + gate * _pe(p_ref[...], wp_ref)

    row = lambda c: pl.BlockSpec((tm, c), lambda i: (i, 0))
    wpp_spec = pl.BlockSpec(wpp.shape, lambda i: (0, 0, 0))
    return pl.pallas_call(
        body,
        name="ple_fwd",
        grid=(t // tm,),
        in_specs=[row(d), pl.BlockSpec((1, d), lambda i: (0, 0)), row(p.shape[1]), _w1024_spec(d, 1, 4), wpp_spec],
        out_specs=row(d),
        out_shape=jax.ShapeDtypeStruct((t, d), F32),
        compiler_params=_params(("parallel",)),
    )(h, g, p, w1024, wpp)


def ple_bwd(dho, h, g, p, w1024, wpp):
    t, d = h.shape
    tm = _tile(t, 512)

    def body(dho_ref, h_ref, g_ref, p_ref, wg_ref, wp_ref, dh_ref, dg_ref, n_ref, dgp_ref, dpe_ref):
        hv, dv = h_ref[...], dho_ref[...]
        n = _rms(hv, g_ref[...]).astype(BF)
        n_ref[...] = n
        wg = _whole(wg_ref)
        gate = _sigmoid(_dot(n, wg))
        pe = _pe(p_ref[...], wp_ref)
        dpe_ref[...] = (dv * gate).astype(BF)
        dgp = (dv * pe * gate * (1.0 - gate)).astype(BF)
        dgp_ref[...] = dgp
        dx, dg = _rms_bwd(hv, g_ref[...], _dot(dgp, wg, 1, 1))
        dh_ref[...] = dv + dx
        _accumulate(dg_ref, dg, pl.program_id(0) == 0)

    row = lambda c: pl.BlockSpec((tm, c), lambda i: (i, 0))
    wpp_spec = pl.BlockSpec(wpp.shape, lambda i: (0, 0, 0))
    return pl.pallas_call(
        body,
        name="ple_bwd",
        grid=(t // tm,),
        in_specs=[row(d), row(d), pl.BlockSpec((1, d), lambda i: (0, 0)), row(p.shape[1]), _w1024_spec(d, 1, 4), wpp_spec],
        out_specs=[row(d), pl.BlockSpec((1, d), lambda i: (0, 0)), row(d), row(d), row(d)],
        out_shape=[jax.ShapeDtypeStruct((t, d), F32), jax.ShapeDtypeStruct((1, d), F32), jax.ShapeDtypeStruct((t, d), BF),
                   jax.ShapeDtypeStruct((t, d), BF), jax.ShapeDtypeStruct((t, d), BF)],
        compiler_params=_params(("arbitrary",)),
    )(dho, h, g, p, w1024, wpp)


def loss_bwd(h, g, target):
    d = h.shape[1]

    def fn(i, nt, rows, vecs, prevs, nexts):
        err = _rms(rows[0], vecs[0]) - rows[1]
        dx, dg = _rms_bwd(rows[0], vecs[0], err * (1.0 / d))
        return [dx], [jnp.sum(err * err, axis=0, keepdims=True) * (0.5 / d), dg]

    return ew(fn, [h, target], [g], [(d, F32)], [(1, d), (1, d)], tm=512, name="loss_bwd")


def adamw(w, g, m, v, name):
    c1, c2 = 1.0 / (1.0 - ADAM_B1 ** ADAM_STEP), 1.0 / (1.0 - ADAM_B2 ** ADAM_STEP)

    def fn(i, nt, rows, vecs, prevs, nexts):
        wv, gv, mv, vv = rows
        mn = ADAM_B1 * mv + (1.0 - ADAM_B1) * gv
        vn = ADAM_B2 * vv + (1.0 - ADAM_B2) * (gv * gv)
        delta = -ADAM_LR * ((mn * c1) / (jnp.sqrt(vn * c2) + ADAM_EPS) + ADAM_WD * wv)
        return [delta, mn, vn], []

    c = w.shape[1]
    return ew(fn, [w, g, m, v], [], [(c, F32)] * 3, tm=_row_tile(w.shape[0], c, HALO), name=name)


def _place():
    return lax.axis_index("x"), lax.axis_index("y"), lax.axis_index("c")


def _other_chips(x, y):
    return [(1 - x, y), (x, 1 - y), (1 - x, 1 - y)]


ANY = pl.BlockSpec(memory_space=pl.ANY)


def _comm_call(body, name, ins, out_shapes, n_sems, aliases=None):
    return pl.pallas_call(
        body,
        name=name,
        in_specs=[ANY] * len(ins),
        out_specs=[ANY] * len(out_shapes),
        out_shape=out_shapes,
        scratch_shapes=[pltpu.SemaphoreType.DMA((n_sems,)), pltpu.SemaphoreType.DMA((n_sems,))],
        input_output_aliases=aliases or {},
    )(*ins)


def gather_rider(packs):
    nt = len(packs)

    def pieces(ins, outs, send_sems, recv_sems):
        x, y, cc = _place()
        chips = _other_chips(x, y)
        sibling = (x, y, 1 - cc)
        k_me = 2 * x + y

        def copy(k, src, dst, to):
            return pltpu.make_async_remote_copy(src_ref=src, dst_ref=dst, send_sem=send_sems.at[k], recv_sem=recv_sems.at[k],
                                                device_id=to, device_id_type=MESH)

        sends, forwards, arrivals = [], [], []
        for ti in range(nt):
            for j, (px, py) in enumerate(chips):
                sends.append(copy(7 * ti + j, ins[ti].at[cc], outs[ti].at[k_me, cc], (px, py, cc)))
                landed = outs[ti].at[2 * px + py, cc]
                forwards.append((copy(7 * ti + j, landed, landed, (px, py, cc)), copy(7 * ti + 3 + j, landed, landed, sibling)))
                passed = outs[ti].at[2 * px + py, 1 - cc]
                arrivals.append(copy(7 * ti + 3 + j, passed, passed, sibling))
            sends.append(copy(7 * ti + 6, ins[ti], outs[ti].at[k_me], sibling))
            own = outs[ti].at[k_me]
            arrivals.append(copy(7 * ti + 6, own, own, sibling))
        return sends, forwards, arrivals

    def start(*parts):
        for cp in pieces(*parts)[0]:
            cp.start()

    def mid(*parts):
        for landed, forward in pieces(*parts)[1]:
            landed.wait_recv()
            forward.start()

    def finish(*parts):
        sends, forwards, arrivals = pieces(*parts)
        for cp in arrivals:
            cp.wait_recv()
        for cp in sends + [f for _, f in forwards]:
            cp.wait_send()

    return Rider(packs, [jax.ShapeDtypeStruct((NCHIP,) + p.shape, p.dtype) for p in packs], 7 * nt, start, finish, mid)


def swap_rider(gs):
    nt = len(gs)
    hl = gs[0].shape[1] // 2

    def copies(ins, outs, send_sems, recv_sems):
        x, y, cc = _place()
        theirs = pl.ds((1 - cc) * hl, hl)
        return [pltpu.make_async_remote_copy(src_ref=ins[ti].at[:, theirs], dst_ref=outs[ti], send_sem=send_sems.at[ti], recv_sem=recv_sems.at[ti],
                                             device_id=(x, y, 1 - cc), device_id_type=MESH) for ti in range(nt)]

    def start(*parts):
        for cp in copies(*parts):
            cp.start()

    def finish(*parts):
        for cp in copies(*parts):
            cp.wait()

    return Rider(gs, [jax.ShapeDtypeStruct((NCHIP, hl) + g.shape[2:], g.dtype) for g in gs], nt, start, finish)


def scatter_packs(cs, name):
    return scatter_rider(cs).standalone(name)


def scatter_rider(cs):
    nt = len(cs)

    def copies(ins, outs, send_sems, recv_sems):
        x, y, cc = _place()
        cps = []
        for ti in range(nt):
            for j, (px, py) in enumerate(_other_chips(x, y)):
                cps.append(pltpu.make_async_remote_copy(src_ref=ins[ti].at[2 * px + py], dst_ref=outs[ti].at[j], send_sem=send_sems.at[3 * ti + j],
                                                        recv_sem=recv_sems.at[3 * ti + j], device_id=(px, py, cc), device_id_type=MESH))
        return cps

    def start(*parts):
        for cp in copies(*parts):
            cp.start()

    def finish(*parts):
        for cp in copies(*parts):
            cp.wait()

    return Rider(cs, [jax.ShapeDtypeStruct((3,) + c_.shape[1:], c_.dtype) for c_ in cs], 3 * nt, start, finish)


def join_packs(fulls, name):
    nt = len(fulls)
    hl = fulls[0].shape[0] // 2

    def body(*refs):
        ins, outs, (send_sems, recv_sems) = refs[:nt], refs[nt:2 * nt], refs[2 * nt:]
        x, y, cc = _place()
        mine = pl.ds(cc * hl, hl)
        cps = [pltpu.make_async_remote_copy(src_ref=ins[ti].at[mine], dst_ref=outs[ti].at[mine], send_sem=send_sems.at[ti], recv_sem=recv_sems.at[ti],
                                            device_id=(x, y, 1 - cc), device_id_type=MESH) for ti in range(nt)]
        for cp in cps:
            cp.start()
        for cp in cps:
            cp.wait()

    return _comm_call(body, name, fulls, [jax.ShapeDtypeStruct(f.shape, f.dtype) for f in fulls], nt, aliases={ti: ti for ti in range(nt)})


def add_sibling(g, recv, name):
    _, nl, r, c = g.shape
    hl = nl // 2
    tm, tc = _tile2(r, c)

    def body(g_ref, r_ref, o_ref):
        o_ref[...] = (g_ref[...].astype(F32) + r_ref[...].astype(F32)).astype(o_ref.dtype)

    blk = (None, None, tm, tc)
    return pl.pallas_call(
        body,
        name=name,
        grid=(NCHIP, hl, r // tm, c // tc),
        in_specs=[pl.BlockSpec(blk, lambda k, l, i, j: (k, lax.axis_index("c") * hl + l, i, j)), pl.BlockSpec(blk, lambda k, l, i, j: (k, l, i, j))],
        out_specs=pl.BlockSpec(blk, lambda k, l, i, j: (k, l, i, j)),
        out_shape=jax.ShapeDtypeStruct(recv.shape, BF),
        compiler_params=_params(("parallel",) * 4),
    )(g, recv)


def add_chips(cs, got, nl, name):
    _, hl, r, c = cs.shape
    tm, tc = _tile2(r, c)

    def body(own_ref, got_ref, o_ref):
        o_ref[...] = own_ref[...].astype(F32) + got_ref[0].astype(F32) + got_ref[1].astype(F32) + got_ref[2].astype(F32)

    return pl.pallas_call(
        body,
        name=name,
        grid=(hl, r // tm, c // tc),
        in_specs=[pl.BlockSpec((None, None, tm, tc), lambda l, i, j: (2 * lax.axis_index("x") + lax.axis_index("y"), l, i, j)),
                  pl.BlockSpec((3, None, tm, tc), lambda l, i, j: (0, l, i, j))],
        out_specs=pl.BlockSpec((None, tm, tc), lambda l, i, j: (lax.axis_index("c") * hl + l, i, j)),
        out_shape=jax.ShapeDtypeStruct((nl, r, c), F32),
        compiler_params=_params(("parallel",) * 3),
    )(cs, got)


def all_gather_xy(shard, name):
    r, c = shard.shape
    hr = r // 2
    assert r % 32 == 0

    def body(x_ref, out_ref, send_sems, recv_sems, local_sem):
        x, y, cc = _place()
        chips = _other_chips(x, y)
        mine = pl.ds(pl.multiple_of(cc * hr, 16), hr)
        theirs = pl.ds(pl.multiple_of((1 - cc) * hr, 16), hr)
        k_me = 2 * x + y

        def copy(k, src, dst, to):
            return pltpu.make_async_remote_copy(src_ref=src, dst_ref=dst, send_sem=send_sems.at[k], recv_sem=recv_sems.at[k],
                                                device_id=to, device_id_type=MESH)

        own = pltpu.make_async_copy(x_ref, out_ref.at[k_me], local_sem)
        own.start()
        first = [copy(j, x_ref.at[mine], out_ref.at[k_me, mine], (*chip, cc)) for j, chip in enumerate(chips)]
        for cp in first:
            cp.start()
        passed = []
        for j, (px, py) in enumerate(chips):
            landed = out_ref.at[2 * px + py, mine]
            copy(j, landed, landed, (px, py, cc)).wait_recv()
            fw = copy(3 + j, landed, landed, (x, y, 1 - cc))
            fw.start()
            passed.append(fw)
        for j, (px, py) in enumerate(chips):
            landed = out_ref.at[2 * px + py, theirs]
            copy(3 + j, landed, landed, (x, y, 1 - cc)).wait_recv()
        for cp in first + passed:
            cp.wait_send()
        own.wait()

    return pl.pallas_call(
        body,
        name=name,
        in_specs=[ANY],
        out_specs=ANY,
        out_shape=jax.ShapeDtypeStruct((NCHIP, r, c), shard.dtype),
        scratch_shapes=[pltpu.SemaphoreType.DMA((6,)), pltpu.SemaphoreType.DMA((6,)), pltpu.SemaphoreType.DMA],
    )(shard)


def all_gather_8(block, name):
    m, c = block.shape

    def body(x_ref, out_ref, send_sems, recv_sems, local_sem):
        x, y, cc = _place()
        me, sibling = (x, y, cc), (x, y, 1 - cc)
        chips = _other_chips(x, y)

        def rows(px, py, pc):
            return out_ref.at[4 * px + 2 * py + pc]

        def copy(k, blk, to, src=None):
            return pltpu.make_async_remote_copy(src_ref=rows(*blk) if src is None else src, dst_ref=rows(*blk), send_sem=send_sems.at[k],
                                                recv_sem=recv_sems.at[k], device_id=to, device_id_type=MESH)

        mine = pltpu.make_async_copy(x_ref, rows(*me), local_sem)
        mine.start()
        first = [copy(0, me, sibling, src=x_ref)]
        first += [copy(1 + j, me, (*chip, cc), src=x_ref) for j, chip in enumerate(chips)]
        for cp in first:
            cp.start()
        passed = [copy(4 + j, (*chip, cc), sibling) for j, chip in enumerate(chips)]
        for j, chip in enumerate(chips):
            copy(1 + j, (*chip, cc), me).wait_recv()
            passed[j].start()
        copy(0, sibling, me).wait_recv()
        for j, chip in enumerate(chips):
            copy(4 + j, (*chip, 1 - cc), me).wait_recv()
        for cp in first + passed:
            cp.wait_send()
        mine.wait()

    return pl.pallas_call(
        body,
        name=name,
        in_specs=[pl.BlockSpec(memory_space=pltpu.VMEM)],
        out_specs=pl.BlockSpec(memory_space=pltpu.VMEM),
        out_shape=jax.ShapeDtypeStruct((8, m, c), block.dtype),
        scratch_shapes=[pltpu.SemaphoreType.DMA((7,)), pltpu.SemaphoreType.DMA((7,)), pltpu.SemaphoreType.DMA],
        compiler_params=pltpu.CompilerParams(vmem_limit_bytes=VMEM_LIMIT),
    )(block)


def add_parts(parts, out_dtype, name, tm=512):
    def fn(i, nt, rows, vecs, prevs, nexts):
        acc = rows[0]
        for r_ in rows[1:]:
            acc = acc + r_
        return [acc], []
    r, c = parts[0].shape
    return ew(fn, list(parts), [], [(c, out_dtype)], tm=_tile(r, tm, 16), name=name)[0]


SMALL_SHARDED = ("sc_conv_w", "m_conv_w")
SMALL_REPL = ("ffn1_norm", "mix_norm", "m_conv_b", "m_dt_bias", "m_A_log", "m_D", "m_norm", "ffn2_norm", "ple_norm", "final_norm")
BIG = ("ffn1_wg", "ffn1_wu", "ffn1_wd", "w_in", "sc_w_out", "m_w_out", "w_o", "ffn2_wg", "ffn2_wu", "ffn2_wd", "ple_w_gate", "ple_w_proj")
TRANSPOSED = ("ffn1_wg", "ffn1_wu", "ffn2_wg", "ffn2_wu", "w_in")
ORDER = ("ffn1_norm", "ffn1_wg", "ffn1_wu", "ffn1_wd", "mix_norm", "w_in", "sc_conv_w", "sc_w_out", "m_conv_w", "m_conv_b", "m_dt_bias",
         "m_A_log", "m_D", "m_norm", "m_w_out", "w_o", "ffn2_norm", "ffn2_wg", "ffn2_wu", "ffn2_wd", "ple_norm", "ple_w_gate", "ple_w_proj",
         "final_norm")


def _pack(arrs, cols, row_mult):
    flat = jnp.concatenate([a.reshape(-1) for a in arrs])
    n = flat.shape[0]
    rows = -(-n // cols)
    rows = -(-rows // row_mult) * row_mult
    return jnp.pad(flat, (0, rows * cols - n)).reshape(rows, cols)


def _unpack(flat2d, shapes):
    flat = flat2d.reshape(-1)
    out, off = [], 0
    for s in shapes:
        n = int(np.prod(s))
        out.append(flat[off:off + n].reshape(s))
        off += n
    return out


def _row_cat(arrs, dtype):
    return jnp.concatenate([a.astype(dtype) for a in arrs], axis=1)


def kernel(x, p, ffn1_norm, ffn1_wg, ffn1_wu, ffn1_wd, mix_norm, w_in, sc_conv_w, sc_w_out, m_conv_w, m_conv_b, m_dt_bias, m_A_log, m_D, m_norm, m_w_out, w_o, ffn2_norm, ffn2_wg, ffn2_wu, ffn2_wd, ple_norm, ple_w_gate, ple_w_proj, final_norm, loss_target, m_ffn1_norm, m_ffn1_wg, m_ffn1_wu, m_ffn1_wd, m_mix_norm, m_w_in, m_sc_conv_w, m_sc_w_out, m_m_conv_w, m_m_conv_b, m_m_dt_bias, m_m_A_log, m_m_D, m_m_norm, m_m_w_out, m_w_o, m_ffn2_norm, m_ffn2_wg, m_ffn2_wu, m_ffn2_wd, m_ple_norm, m_ple_w_gate, m_ple_w_proj, m_final_norm, v_ffn1_norm, v_ffn1_wg, v_ffn1_wu, v_ffn1_wd, v_mix_norm, v_w_in, v_sc_conv_w, v_sc_w_out, v_m_conv_w, v_m_conv_b, v_m_dt_bias, v_m_A_log, v_m_D, v_m_norm, v_m_w_out, v_w_o, v_ffn2_norm, v_ffn2_wg, v_ffn2_wu, v_ffn2_wd, v_ple_norm, v_ple_w_gate, v_ple_w_proj, v_final_norm):
    args = dict(locals())
    wts = {n: args[n] for n in ORDER}
    mom = {n: args["m_" + n] for n in ORDER}
    vel = {n: args["v_" + n] for n in ORDER}

    depth = ffn1_norm.shape[0]
    d = x.shape[-1]
    w = 2 * d
    hh = w // SSM_P
    cw = w + 2 * SSM_G * SSM_N
    d4 = d // NCHIP
    pp = 7 * d + cw + LANE
    my_x, my_y, my_c = _place()
    k_me = 2 * my_x + my_y

    tr = lambda a: jnp.swapaxes(a, 1, 2)
    gu_t = [_row_cat([tr(wg_), tr(wu_)], BF) for wg_, wu_ in ((ffn1_wg, ffn1_wu), (ffn2_wg, ffn2_wu))]
    wd_l = [ffn1_wd.astype(BF), ffn2_wd.astype(BF)]
    w1024_l = _row_cat([m_w_out, sc_w_out, w_o, ple_w_gate], BF)
    p4 = w_in.shape[2]
    p4p = -(-p4 // 32) * 32
    win_l, wpp_l = jnp.pad(tr(w_in).astype(BF), ((0, 0), (0, p4p - p4), (0, 0))), ple_w_proj.astype(BF)
    halves = lambda a: a.reshape(2, a.shape[0] // 2, a.shape[1])
    whole = lambda g: g.reshape(NCHIP, g.shape[2] * 2, g.shape[3])

    def pieces(l):
        return {"small": [halves(w1024_l[l]), halves(wpp_l[l])], "win": [halves(win_l[l])], "gu1": [halves(gu_t[0][l])], "d1": [halves(wd_l[0][l])],
                "gu2": [halves(gu_t[1][l])], "d2": [halves(wd_l[1][l])]}

    small_local = [sc_conv_w, m_conv_w]
    gathered_s = all_gather_xy(_pack(small_local, LANE, 32), "gather_conv_weights")
    per_shard_s = [_unpack(gathered_s[k], [a.shape for a in small_local]) for k in range(NCHIP)]
    sc_conv_full = jnp.concatenate([per_shard_s[k][0] for k in range(NCHIP)], axis=2)
    m_conv_full = jnp.concatenate([per_shard_s[k][1] for k in range(NCHIP)], axis=2)

    pad_h = lambda a: jnp.pad(a, ((0, 0), (0, LANE - hh)))
    dt_bias_p, a_log_p = pad_h(m_dt_bias), pad_h(m_A_log)
    d_exp = jnp.repeat(m_D, SSM_P, axis=1)
    e_mat = (jnp.arange(w)[None, :] // SSM_P == jnp.arange(LANE)[:, None]).astype(F32)
    et_mat = e_mat.T
    o_z, o_xbc, o_dt, o_g = 3 * d, 5 * d, 5 * d + cw, 5 * d + cw + hh

    def layer_weights(got):
        wt = {"w1024": whole(got["small"][0]), "wpp": whole(got["small"][1])}
        wt.update({k: whole(got[k][0]) for k in ("gu1", "d1", "gu2", "d2")})
        gw = whole(got["win"][0])

        def wi(lo, hi):
            parts = [gw[k, max(lo - k * p4, 0):min(hi - k * p4, p4)] for k in range(NCHIP) if lo < (k + 1) * p4 and hi > k * p4]
            return parts[0] if len(parts) == 1 else jnp.concatenate(parts, axis=0)

        wt["sc3"], wt["z"], wt["xbc"], wt["g2"] = wi(0, o_z), wi(o_z, o_xbc), wi(o_xbc, o_dt), wi(o_g, o_g + 2 * d)
        wt["dt"] = jnp.pad(wi(o_dt, o_g), ((0, LANE - hh), (0, 0)))
        wt["in_p"] = jnp.concatenate([wt["g2"], wt["z"], wt["sc3"][d:], wt["xbc"], wt["sc3"][:d], wt["dt"]], axis=0)
        return wt

    first = pieces(0)
    order = ("gu1", "d1", "win", "small", "gu2", "d2")
    flat = gather_rider([a for k in order for a in first[k]]).standalone("gather_weights")
    got, pos = {}, 0
    for k in order:
        got[k] = flat[pos:pos + len(first[k])]
        pos += len(first[k])
    wts_l = [layer_weights(got)]

    h = x[0]
    saved = []
    for i in range(depth):
        s, wt = {}, wts_l[i]
        nxt = pieces(i + 1) if i + 1 < depth else None
        ride = lambda k: gather_rider(nxt[k]) if nxt else None
        got = {}
        s["h0"] = h
        (s["ab1"], s4, s["n1"]), got["small"] = ffn_up(h, ffn1_norm[i:i + 1], wt["gu1"], rider=ride("small"))
        h, got["d1"] = ffn_down(s4, wt["d1"], h, rider=ride("d1"))
        s["h1"] = h
        u = norm_cast(h, mix_norm[i:i + 1])
        s["u"] = u
        s["sc3"] = mm(u, wt["sc3"], tb=True, out_dtype=BF, name="proj_sc")
        s["z"] = mm(u, wt["z"], tb=True, out_dtype=BF, name="proj_z")
        s["xbc_raw"] = mm(u, wt["xbc"], tb=True, out_dtype=BF, name="proj_xbc")
        s["gates"] = mm(u, wt["g2"], tb=True, out_dtype=BF, name="proj_gates")
        s["dt_raw"] = mm(u, wt["dt"], tb=True, name="proj_dt")
        s["ya_in"] = conv_a_fwd(s["sc3"], sc_conv_full[i])
        s["xbc"], s["dt"] = conv_m_fwd(s["xbc_raw"], s["dt_raw"], m_conv_full[i], m_conv_b[i:i + 1], dt_bias_p[i:i + 1])
        (s["yn"], s["y"], s["sprev"]), got["win"] = ssd_fwd(s["xbc"], s["dt"], s["z"], a_log_p[i:i + 1], d_exp[i:i + 1], m_norm[i:i + 1], e_mat,
                                                            rider=ride("win"))
        (h, s["y_a"], s["y_m"], s["merged"]), got["gu2"] = mix_out_fwd(s["ya_in"], s["yn"], s["gates"], h, wt["w1024"], rider=ride("gu2"))
        s["h2"] = h
        (s["ab2"], s4, s["n2"]), got["gu1"] = ffn_up(h, ffn2_norm[i:i + 1], wt["gu2"], rider=ride("gu1"))
        h, got["d2"] = ffn_down(s4, wt["d2"], h, rider=ride("d2"))
        s["h3"] = h
        h = ple_fwd(h, ple_norm[i:i + 1], p[i, 0], wt["w1024"], wt["wpp"])
        saved.append(s)
        if nxt:
            wts_l.append(layer_weights(got))

    dh, loss_lanes, g_final = loss_bwd(h, final_norm[None, :], loss_target[0])
    loss = lax.psum(jnp.sum(loss_lanes), ("x", "y", "c"))

    def finish_reduce(cs, got):
        halves = [add_chips(c_, g_, 2, "grad_add_chips") for c_, g_ in zip(cs, got, strict=True)]
        return [f.reshape(-1, f.shape[2]) for f in join_packs(halves, "grad_join_halves")]

    g_layer, pending, reduced = None, None, [None] * depth
    gs = {n: [None] * depth for n in SMALL_SHARDED + SMALL_REPL if n != "final_norm"}
    for i in reversed(range(depth)):
        s = saved[i]
        wt = wts_l[i]
        dh, gs["ple_norm"][i], n3, dgp, dpe = ple_bwd(dh, s["h3"], ple_norm[i:i + 1], p[i, 0], wt["w1024"], wt["wpp"])
        g_pg = mm(n3, dgp, ta=True, out_dtype=BF, name="g_ple_gate", tm_cap=512, tn_cap=512)
        g_pp = mm(p[i, 0], dpe, ta=True, out_dtype=BF, name="g_ple_proj", tm_cap=512, tn_cap=512)
        g_pp = jnp.transpose(g_pp.reshape(g_pp.shape[0], NCHIP, d4), (1, 0, 2))
        (dn2, s2, dab2, do2), from_sibling = ffn_bwd(dh, s["ab2"], wt["gu2"], wt["d2"], rider=swap_rider(g_layer) if g_layer else None)
        if g_layer:
            pending = [add_sibling(g, r_, "grad_add_sibling") for g, r_ in zip(g_layer, from_sibling, strict=True)]
        g_ffn2 = ffn_wgrads(s["n2"], do2, s2, dab2)
        dh, gs["ffn2_norm"][i] = norm_bwd_add(dh, s["h2"], ffn2_norm[i:i + 1], dn2)
        dproj, dya, dyn, dy_a, dy_m = mix_out_bwd(dh, s["gates"], s["y_a"], s["y_m"], wt["w1024"], pp)
        g_wo = mm(s["merged"], dh, ta=True, out_dtype=BF, name="g_w_o", tm_cap=512, tn_cap=512)
        g_sco = mm(s["ya_in"], dy_a, ta=True, out_dtype=BF, name="g_sc_out", tm_cap=512, tn_cap=512)
        g_mo = mm(s["yn"], dy_m, ta=True, out_dtype=BF, name="g_m_out", tm_cap=512, tn_cap=512)
        g_1024 = jnp.concatenate([g_mo.reshape(NCHIP, 2 * d4, d), g_sco.reshape(NCHIP, d4, d), g_wo.reshape(NCHIP, d4, d),
                                  g_pg.reshape(NCHIP, d4, d)], axis=1)
        (dproj, dxbc, ddt, gs["m_norm"][i], gd, gal), got_a = ssd_bwd(dyn, s["y"], s["z"], s["xbc"], s["dt"], s["sprev"], a_log_p[i:i + 1], d_exp[i:i + 1],
                                                                      m_norm[i:i + 1], e_mat, et_mat, Window(0, 1, pp, dproj),
                                                                      rider=scatter_rider(pending[:1]) if pending else None)
        gs["m_D"][i], gs["m_A_log"][i] = gd[:, :hh], gal[:, :hh]
        dpre, dproj, gdb = conv_m_bwd1(dxbc, s["xbc_raw"], ddt, s["dt_raw"], m_conv_full[i], m_conv_b[i:i + 1], dt_bias_p[i:i + 1],
                                       Window(1, (7 * d + cw) // LANE, pp, dproj))
        gs["m_dt_bias"][i] = gdb[:, :hh]
        dproj, gs["m_conv_w"][i], gs["m_conv_b"][i] = conv_bwd2(dpre, s["xbc_raw"], m_conv_full[i], "conv_m_bwd2", Window(0, 6 * d // cw, pp, dproj))
        dcv, dproj, v = conv_a_bwd1(dya, s["sc3"], sc_conv_full[i], Window(1, (6 * d + cw) // d, pp, dproj))
        dproj, gs["sc_conv_w"][i] = conv_a_bwd2(dcv, v, s["sc3"], sc_conv_full[i], Window(0, 2, pp, dproj))
        if pending:
            dh_mix, gs["mix_norm"][i], got_b = proj_in_bwd(dproj, wt["in_p"], dh, s["h1"], mix_norm[i:i + 1], rider=scatter_rider(pending[2:3]))
            gwp, got_c = mm(dproj, s["u"], ta=True, out_dtype=BF, name="g_w_in", tm_cap=1152, tn_cap=512, rider=scatter_rider(pending[1:2] + pending[3:]))
            reduced[i + 1] = finish_reduce(pending, [got_a[0], got_c[0], got_b[0], got_c[1]])
        else:
            dh_mix, gs["mix_norm"][i], _ = proj_in_bwd(dproj, wt["in_p"], dh, s["h1"], mix_norm[i:i + 1])
            gwp = mm(dproj, s["u"], ta=True, out_dtype=BF, name="g_w_in", tm_cap=1152, tn_cap=512)
        gw_rows = jnp.concatenate([gwp[6 * d + cw:7 * d + cw], gwp[4 * d:6 * d], gwp[2 * d:4 * d], gwp[6 * d:6 * d + cw], gwp[7 * d + cw:7 * d + cw + hh],
                                   gwp[:2 * d]], axis=0)
        g_in = jnp.pad(gw_rows.reshape(NCHIP, p4, d), ((0, 0), (0, p4p - p4), (0, 0)))
        dh = dh_mix
        (dn1, s1, dab1, do1), _ = ffn_bwd(dh, s["ab1"], wt["gu1"], wt["d1"])
        g_ffn1 = ffn_wgrads(s["n1"], do1, s1, dab1)
        dh, gs["ffn1_norm"][i] = norm_bwd_add(dh, s["h0"], ffn1_norm[i:i + 1], dn1)
        g_layer = [jnp.concatenate([g_ffn1, g_ffn2], axis=1), g_1024, g_in, g_pp]
        g_layer = [g.reshape(NCHIP, 2, g.shape[1] // 2, g.shape[2]) for g in g_layer]
    from_sibling = swap_rider(g_layer).standalone("grad_swap_halves")
    pending = [add_sibling(g, r_, "grad_add_sibling") for g, r_ in zip(g_layer, from_sibling, strict=True)]
    reduced[0] = finish_reduce(pending, scatter_packs(pending, "grad_scatter"))
    grad_x = dh[None]

    f4 = reduced[0][0].shape[0] // 6
    rows_of = lambda j, lo, hi: jnp.stack([reduced[l][j][lo:hi] for l in range(depth)])
    ffn_rows = lambda j: rows_of(0, j * f4, (j + 1) * f4)
    grads = {
        "ffn1_wg": ffn_rows(0), "ffn1_wu": ffn_rows(1), "ffn1_wd": ffn_rows(2), "ffn2_wg": ffn_rows(3), "ffn2_wu": ffn_rows(4), "ffn2_wd": ffn_rows(5),
        "m_w_out": rows_of(1, 0, 2 * d4), "sc_w_out": rows_of(1, 2 * d4, 3 * d4), "w_o": rows_of(1, 3 * d4, 4 * d4), "ple_w_gate": rows_of(1, 4 * d4, 5 * d4),
        "w_in": rows_of(2, 0, p4), "ple_w_proj": rows_of(3, 0, None),
    }

    small_names = list(SMALL_SHARDED + SMALL_REPL)
    small_full = [g_final[0] if n == "final_norm" else jnp.stack(gs[n]) for n in small_names]
    small_pack = _pack(small_full, LANE, HALO)
    all8 = all_gather_8(small_pack, "gather_small_grads")
    small_sum = add_parts([all8[k] for k in range(8)], F32, "add_small_grads", tm=256)
    for n, tot in zip(small_names, _unpack(small_sum, [a.shape for a in small_full]), strict=True):
        if n in SMALL_SHARDED:
            cl = wts[n].shape[2]
            grads[n] = lax.dynamic_slice_in_dim(tot, k_me * cl, cl, axis=2)
        else:
            grads[n] = tot.reshape(wts[n].shape)

    delta, new_m, new_v = {}, {}, {}
    for n in BIG:
        view = tr if n in TRANSPOSED else (lambda a: a)
        shp = grads[n].shape
        two = lambda a: a.reshape(-1, shp[-1])
        dl, nm, nv = adamw(two(view(wts[n])), two(grads[n]), two(view(mom[n])), two(view(vel[n])), "adamw_" + "x".join(map(str, shp[1:])))
        grads[n], delta[n], new_m[n], new_v[n] = view(grads[n]), view(dl.reshape(shp)), view(nm.reshape(shp)), view(nv.reshape(shp))
    for n in small_names:
        shp = wts[n].shape
        two = lambda a: a.reshape(-1, shp[-1])
        dl, nm, nv = adamw(two(wts[n]), two(grads[n]), two(mom[n]), two(vel[n]), "adamw_small_" + "x".join(map(str, shp)))
        delta[n], new_m[n], new_v[n] = dl.reshape(shp), nm.reshape(shp), nv.reshape(shp)

    return (loss, grad_x, *[grads[n] for n in ORDER], *[delta[n] for n in ORDER], *[new_m[n] for n in ORDER], *[new_v[n] for n in ORDER])
```

```python
import jax
import jax.numpy as jnp
import numpy as np
from jax import lax
from jax.experimental import pallas as pl
from jax.experimental.pallas import tpu as pltpu

BF = jnp.bfloat16
F32 = jnp.float32
EPS = 1e-6
LANE = 128
HALO = 8
SSM_P = 64
SSM_N = 128
SSM_G = 4
SSM_L = 128
ADAM_LR, ADAM_B1, ADAM_B2, ADAM_EPS, ADAM_WD, ADAM_STEP = 0.001, 0.9, 0.999, 1e-08, 0.01, 10
VMEM_LIMIT = 56 * 1024 * 1024
TILE_ELEMS = 600_000
ADD_TILE_ELEMS = 1_000_000
NCHIP = 4
FFN_SUB = 256
MESH = pl.DeviceIdType.MESH


def _tile(n, cap, mult=LANE):
    best = None
    t = mult
    while t <= min(n, cap):
        if n % t == 0:
            best = t
        t += mult
    return best if best is not None else n


def _row_tile(r, c, mult=16, elems=TILE_ELEMS):
    return _tile(r, max(mult, elems // c // mult * mult), mult)


def _tile2(r, c, mult=16, elems=ADD_TILE_ELEMS):
    tm = _row_tile(r, c, mult, elems)
    tc = c if tm * c <= elems else _tile(c, max(LANE, elems // tm // LANE * LANE))
    return tm, tc


def _params(sem):
    return pltpu.CompilerParams(dimension_semantics=sem, vmem_limit_bytes=VMEM_LIMIT)


def _sigmoid(x):
    return 1.0 / (1.0 + jnp.exp(-x))


def _dot(a, b, ca=1, cb=0, precision=None):
    return lax.dot_general(a, b, (((ca,), (cb,)), ((), ())), precision=precision, preferred_element_type=F32)


def _rms(x, g):
    r = lax.rsqrt(jnp.mean(x * x, axis=-1, keepdims=True) + EPS)
    return x * r * g


def _rms_bwd(x, g, dy):
    r = lax.rsqrt(jnp.mean(x * x, axis=-1, keepdims=True) + EPS)
    xh = x * r
    dxh = dy * g
    dx = r * (dxh - xh * jnp.mean(dxh * xh, axis=-1, keepdims=True))
    return dx, jnp.sum(dy * xh, axis=0, keepdims=True)


def _accumulate(ref, val, first):
    @pl.when(first)
    def _():
        ref[...] = val

    @pl.when(jnp.logical_not(first))
    def _():
        ref[...] += val


RIDER_MID = 1.0


class Rider:
    def __init__(self, ins, out_shapes, n_sems, start, finish, mid=None):
        self.ins, self.out_shapes, self.n_sems, self.start, self.mid, self.finish = list(ins), list(out_shapes), n_sems, start, mid, finish

    def standalone(self, name):
        ni, no = len(self.ins), len(self.out_shapes)

        def body(*refs):
            parts = (refs[:ni], refs[ni:ni + no], *refs[ni + no:])
            self.start(*parts)
            if self.mid is not None:
                self.mid(*parts)
            self.finish(*parts)

        return _comm_call(body, name, self.ins, self.out_shapes, self.n_sems)


def host_call(body, *, name, grid, in_specs, out_specs, out_shape, scratch_shapes, operands, rider=None, aliases=None):
    n_in, n_out = len(in_specs), len(out_specs)
    aliases = aliases or {}
    if rider is None:
        outs = pl.pallas_call(body, name=name, grid=grid, in_specs=in_specs, out_specs=out_specs, out_shape=out_shape, scratch_shapes=scratch_shapes,
                              input_output_aliases=aliases, compiler_params=_params(("arbitrary",) * len(grid)))(*operands)
        return list(outs), []
    ri, ro = len(rider.ins), len(rider.out_shapes)

    def hosted(*refs):
        ins, r_ins = refs[:n_in], refs[n_in:n_in + ri]
        outs, r_outs = refs[n_in + ri:n_in + ri + n_out], refs[n_in + ri + n_out:n_in + ri + n_out + ro]
        scratch, (send_sems, recv_sems) = refs[n_in + ri + n_out + ro:-2], refs[-2:]
        step, total = 0, 1
        for ax, n in enumerate(grid):
            step = step * n + pl.program_id(ax)
            total *= n
        parts = (r_ins, r_outs, send_sems, recv_sems)

        @pl.when(step == 0)
        def _():
            rider.start(*parts)

        if rider.mid is not None:
            @pl.when(step == min(total - 1, int(total * RIDER_MID)))
            def _():
                rider.mid(*parts)

        body(*ins, *outs, *scratch)

        @pl.when(step == total - 1)
        def _():
            rider.finish(*parts)

    outs = pl.pallas_call(
        hosted,
        name=name,
        grid=grid,
        in_specs=list(in_specs) + [ANY] * ri,
        out_specs=list(out_specs) + [ANY] * ro,
        out_shape=list(out_shape) + rider.out_shapes,
        scratch_shapes=list(scratch_shapes) + [pltpu.SemaphoreType.DMA((rider.n_sems,)), pltpu.SemaphoreType.DMA((rider.n_sems,))],
        input_output_aliases=aliases,
        compiler_params=_params(("arbitrary",) * len(grid)),
    )(*operands, *rider.ins)
    return list(outs[:n_out]), list(outs[n_out:])


def mmx(name, a, b, *, grid, a_spec, b_spec, o_spec, o_shape, o_dtype, ca, cb, acc_shape=None, rider=None):
    nk = grid[-1] if acc_shape is not None else 1

    def body(a_ref, b_ref, o_ref, *acc):
        p = _dot(a_ref[...].astype(BF), b_ref[...].astype(BF), ca, cb)
        if nk == 1:
            o_ref[...] = p.astype(o_ref.dtype)
        else:
            kk = pl.program_id(len(grid) - 1)
            _accumulate(acc[0], p, kk == 0)

            @pl.when(kk == nk - 1)
            def _():
                o_ref[...] = acc[0][...].astype(o_ref.dtype)

    if rider is not None:
        (out,), r_outs = host_call(body, name=name, grid=grid, in_specs=[a_spec, b_spec], out_specs=[o_spec], out_shape=[jax.ShapeDtypeStruct(o_shape, o_dtype)],
                                   scratch_shapes=[pltpu.VMEM(acc_shape, F32)] if nk > 1 else [], operands=(a, b), rider=rider)
        return out, r_outs
    sem = ("parallel",) * (len(grid) - 1) + ("arbitrary" if nk > 1 else "parallel",)
    return pl.pallas_call(
        body,
        name=name,
        grid=grid,
        in_specs=[a_spec, b_spec],
        out_specs=o_spec,
        out_shape=jax.ShapeDtypeStruct(o_shape, o_dtype),
        scratch_shapes=[pltpu.VMEM(acc_shape, F32)] if nk > 1 else [],
        compiler_params=_params(sem),
    )(a, b)


def mm(a, b, *, ta=False, tb=False, out_dtype=F32, name, tm_cap=1024, tn_cap=1024, tk_cap=4096, rider=None):
    m, k = (a.shape[1], a.shape[0]) if ta else a.shape
    n = b.shape[0] if tb else b.shape[1]
    assert (b.shape[1] if tb else b.shape[0]) == k
    tm, tn, tk = _tile(m, tm_cap), _tile(n, tn_cap), _tile(k, tk_cap)
    nk = k // tk
    a_spec = pl.BlockSpec((tk, tm), lambda i, j, kk: (kk, i)) if ta else pl.BlockSpec((tm, tk), lambda i, j, kk: (i, kk))
    b_spec = pl.BlockSpec((tn, tk), lambda i, j, kk: (j, kk)) if tb else pl.BlockSpec((tk, tn), lambda i, j, kk: (kk, j))
    return mmx(name, a, b, grid=(m // tm, n // tn, nk), a_spec=a_spec, b_spec=b_spec, o_spec=pl.BlockSpec((tm, tn), lambda i, j, kk: (i, j)),
               o_shape=(m, n), o_dtype=out_dtype, ca=0 if ta else 1, cb=1 if tb else 0, acc_shape=(tm, tn) if nk > 1 else None, rider=rider)


class Window:
    def __init__(self, out, block, cols, buf=None):
        self.out, self.block, self.cols, self.buf = out, block, cols, buf


def ew(fn, rows, vecs, out_rows, out_red=(), *, tm, name, prev_halo=(), next_halo=(), window=None):
    t = rows[0].shape[0]
    tm = min(tm, t)
    nt = t // tm
    assert t % tm == 0 and (tm % HALO == 0 or (tm == t and not prev_halo and not next_halo))
    nr, nv, npv, nnx, nor = len(rows), len(vecs), len(prev_halo), len(next_halo), len(out_rows)
    hb = tm // HALO
    n_in = nr + nv + npv + nnx
    passed = window is not None and window.buf is not None

    def body(*refs):
        i = pl.program_id(0)
        ins = [r[...].astype(F32) for r in refs[:n_in]]
        outs = refs[n_in + passed:]
        o_rows, o_red = fn(i, nt, ins[:nr], ins[nr:nr + nv], ins[nr + nv:nr + nv + npv], ins[nr + nv + npv:])
        for ref, val in zip(outs[:nor], o_rows, strict=True):
            ref[...] = val.astype(ref.dtype)
        for ref, val in zip(outs[nor:], o_red, strict=True):
            _accumulate(ref, val, i == 0)

    in_specs = [pl.BlockSpec((tm, r.shape[1]), lambda i: (i, 0)) for r in rows]
    in_specs += [pl.BlockSpec(v.shape, lambda i: (0, 0)) for v in vecs]
    in_specs += [pl.BlockSpec((HALO, rows[k].shape[1]), lambda i: (jnp.maximum(i * hb - 1, 0), 0)) for k in prev_halo]
    in_specs += [pl.BlockSpec((HALO, rows[k].shape[1]), lambda i: (jnp.minimum((i + 1) * hb, t // HALO - 1), 0)) for k in next_halo]
    out_specs = [pl.BlockSpec((tm, c), lambda i: (i, 0)) for c, _ in out_rows]
    out_specs += [pl.BlockSpec(s, lambda i: (0, 0)) for s in out_red]
    out_shape = [jax.ShapeDtypeStruct((t, c), d) for c, d in out_rows] + [jax.ShapeDtypeStruct(s, F32) for s in out_red]
    operands = [*rows, *vecs, *[rows[k] for k in prev_halo], *[rows[k] for k in next_halo]]
    aliases = {}
    if window is not None:
        c, dt_ = out_rows[window.out]
        out_specs[window.out] = pl.BlockSpec((tm, c), lambda i: (i, window.block))
        out_shape[window.out] = jax.ShapeDtypeStruct((t, window.cols), dt_)
        if passed:
            in_specs.append(ANY)
            operands.append(window.buf)
            aliases = {n_in: window.out}
    return pl.pallas_call(
        body,
        name=name,
        grid=(nt,),
        in_specs=in_specs,
        out_specs=out_specs,
        out_shape=out_shape,
        input_output_aliases=aliases,
        compiler_params=_params(("arbitrary",) if out_red else ("parallel",)),
    )(*operands)


def _shift_down(x, prev, j):
    if j == 0:
        return x
    r = pltpu.roll(x, j, 0)
    rh = pltpu.roll(prev, j, 0)
    row = lax.broadcasted_iota(jnp.int32, (HALO, x.shape[1]), 0)
    head = jnp.where(row < j, rh, r[:HALO])
    return jnp.concatenate([head, r[HALO:]], axis=0)


def _shift_up(x, nxt, j):
    if j == 0:
        return x
    n = x.shape[0]
    r = pltpu.roll(x, n - j, 0)
    rh = pltpu.roll(nxt, HALO - j, 0)
    row = lax.broadcasted_iota(jnp.int32, (HALO, x.shape[1]), 0)
    tail = jnp.where(row >= HALO - j, rh, r[n - HALO:])
    return jnp.concatenate([r[: n - HALO], tail], axis=0)


def _conv_fwd(x, prev, w):
    kk = w.shape[0]
    acc = None
    for k in range(kk):
        term = w[k:k + 1, :] * _shift_down(x, prev, kk - 1 - k)
        acc = term if acc is None else acc + term
    return acc


def ffn_up(h, g, wf, rider=None):
    t, d = h.shape
    f4 = wf.shape[1] // 2
    tm = _tile(t, 1024)
    sub = _tile(tm, FFN_SUB, 16)

    def body(h_ref, g_ref, wg_ref, wu_ref, ab_ref, s_ref, n_ref):
        @pl.when(pl.program_id(1) == 0)
        def _():
            n_ref[...] = _rms(h_ref[...], g_ref[...]).astype(BF)

        for r in range(tm // sub):
            rows = slice(r * sub, (r + 1) * sub)
            n = n_ref[rows, :]
            a = _dot(n, wg_ref[...], 1, 1)
            b = _dot(n, wu_ref[...], 1, 1)
            ab_ref[0, rows, :] = a.astype(BF)
            ab_ref[1, rows, :] = b.astype(BF)
            s_ref[rows, :] = (a * _sigmoid(a) * b).astype(BF)

    wspec = lambda ib: pl.BlockSpec((None, f4, d), lambda i, j: (j, ib, 0))
    return host_call(
        body,
        name="ffn_up",
        grid=(t // tm, NCHIP),
        in_specs=[pl.BlockSpec((tm, d), lambda i, j: (i, 0)), pl.BlockSpec((1, d), lambda i, j: (0, 0)), wspec(0), wspec(1)],
        out_specs=[pl.BlockSpec((2, None, tm, f4), lambda i, j: (0, j, i, 0)), pl.BlockSpec((None, tm, f4), lambda i, j: (j, i, 0)),
                   pl.BlockSpec((tm, d), lambda i, j: (i, 0))],
        out_shape=[jax.ShapeDtypeStruct((2, NCHIP, t, f4), BF), jax.ShapeDtypeStruct((NCHIP, t, f4), BF), jax.ShapeDtypeStruct((t, d), BF)],
        scratch_shapes=[],
        operands=(h, g, wf, wf),
        rider=rider,
    )


def ffn_down(s4, wf, h, g_next, rider=None):
    t, d = h.shape
    f4 = s4.shape[2]
    tm = _tile(t, 512)

    def body(s_ref, w_ref, h_ref, g_ref, o_ref, n_ref):
        acc = _dot(s_ref[0], w_ref[0])
        for k in range(1, NCHIP):
            acc = acc + _dot(s_ref[k], w_ref[k])
        out = h_ref[...] + 0.5 * acc
        o_ref[...] = out
        n_ref[...] = _rms(out, g_ref[...]).astype(BF)

    row = pl.BlockSpec((tm, d), lambda i: (i, 0))
    (out, nrm), r_outs = host_call(
        body,
        name="ffn_down",
        grid=(t // tm,),
        in_specs=[pl.BlockSpec((NCHIP, tm, f4), lambda i: (0, i, 0)), pl.BlockSpec((NCHIP, f4, d), lambda i: (0, 0, 0)), row, pl.BlockSpec((1, d), lambda i: (0, 0))],
        out_specs=[row, row],
        out_shape=[jax.ShapeDtypeStruct((t, d), F32), jax.ShapeDtypeStruct((t, d), BF)],
        scratch_shapes=[],
        operands=(s4, wf, h, g_next),
        rider=rider,
    )
    return out, nrm, r_outs


def ffn_bwd(dho, ab, wf, wd, rider=None):
    t, d = dho.shape
    f4 = wd.shape[1]
    tm = _tile(t, 1024)
    sub = _tile(tm, FFN_SUB, 16)

    def body(dho_ref, ab_ref, wg_ref, wu_ref, wd_ref, dn_ref, s_ref, dab_ref, do_sc):
        j = pl.program_id(1)

        @pl.when(j == 0)
        def _():
            do_sc[...] = (0.5 * dho_ref[...]).astype(BF)
            dn_ref[...] = jnp.zeros_like(dn_ref)

        for r in range(tm // sub):
            rows = slice(r * sub, (r + 1) * sub)
            ds = _dot(do_sc[rows, :], wd_ref[...], 1, 1)
            av, bv = ab_ref[0, rows, :].astype(F32), ab_ref[1, rows, :].astype(F32)
            sig = _sigmoid(av)
            sl = av * sig
            s_ref[rows, :] = (sl * bv).astype(BF)
            da = (ds * bv * (sig * (1.0 + av * (1.0 - sig)))).astype(BF)
            db = (ds * sl).astype(BF)
            dab_ref[0, rows, :] = da
            dab_ref[1, rows, :] = db
            dn_ref[rows, :] += _dot(da, wg_ref[...]) + _dot(db, wu_ref[...])

    row = lambda c: pl.BlockSpec((tm, c), lambda i, j: (i, 0))
    wspec = lambda ib: pl.BlockSpec((None, f4, d), lambda i, j: (j, ib, 0))
    ab_spec = pl.BlockSpec((2, None, tm, f4), lambda i, j: (0, j, i, 0))
    return host_call(
        body,
        name="ffn_bwd",
        grid=(t // tm, NCHIP),
        in_specs=[row(d), ab_spec, wspec(0), wspec(1), wspec(0)],
        out_specs=[row(d), pl.BlockSpec((None, tm, f4), lambda i, j: (j, i, 0)), ab_spec, row(d)],
        out_shape=[jax.ShapeDtypeStruct((t, d), F32), jax.ShapeDtypeStruct((NCHIP, t, f4), BF), jax.ShapeDtypeStruct((2, NCHIP, t, f4), BF),
                   jax.ShapeDtypeStruct((t, d), BF)],
        scratch_shapes=[],
        operands=(dho, ab, wf, wf, wd),
        rider=rider,
    )


def ffn_wgrads(n, do, s4, dab):
    t, d = n.shape
    f4 = s4.shape[2]
    g_in = mmx("g_ffn_in", dab, n, grid=(2, NCHIP, 1), a_spec=pl.BlockSpec((None, None, t, f4), lambda wh, k, j: (wh, k, 0, 0)),
               b_spec=pl.BlockSpec((t, d), lambda wh, k, j: (0, 0)), o_spec=pl.BlockSpec((None, None, f4, d), lambda wh, k, j: (k, wh, 0, 0)),
               o_shape=(NCHIP, 2, f4, d), o_dtype=BF, ca=0, cb=0)
    g_out = mmx("g_ffn_out", s4, do, grid=(NCHIP, 1), a_spec=pl.BlockSpec((None, t, f4), lambda k, j: (k, 0, 0)),
                b_spec=pl.BlockSpec((t, d), lambda k, j: (0, 0)), o_spec=pl.BlockSpec((None, f4, d), lambda k, j: (k, 0, 0)),
                o_shape=(NCHIP, f4, d), o_dtype=BF, ca=0, cb=0)
    return jnp.concatenate([g_in.reshape(NCHIP, 2 * f4, d), g_out], axis=1)


def norm_cast(h, g):
    def fn(i, nt, rows, vecs, prevs, nexts):
        return [_rms(rows[0], vecs[0])], []
    return ew(fn, [h], [g], [(h.shape[1], BF)], tm=512, name="norm_cast")[0]


def _zero_if(cond, x):
    return jnp.where(cond, jnp.zeros_like(x), x)


def conv_a_fwd(sc3, w_sc):
    d = sc3.shape[1] // 3

    def fn(i, nt, rows, vecs, prevs, nexts):
        x, pv = rows[0], _zero_if(i == 0, prevs[0])
        v = x[:, d:2 * d] * x[:, 2 * d:]
        vp = pv[:, d:2 * d] * pv[:, 2 * d:]
        return [x[:, :d] * _conv_fwd(v, vp, vecs[0])], []

    return ew(fn, [sc3], [w_sc], [(d, BF)], tm=256, name="conv_a_fwd", prev_halo=(0,))[0]


def _softplus(x):
    e = jnp.exp(-jnp.abs(x))
    return jnp.maximum(x, 0.0) + jnp.where(e < 1e-4, e - 0.5 * e * e, jnp.log(1.0 + e))


def conv_m_fwd(xbc_raw, dt_raw, w_mc, b_mc, dt_bias):
    def fn(i, nt, rows, vecs, prevs, nexts):
        pre = _conv_fwd(rows[0], _zero_if(i == 0, prevs[0]), vecs[0]) + vecs[1]
        return [pre * _sigmoid(pre), _softplus(rows[1] + vecs[2])], []

    return ew(fn, [xbc_raw, dt_raw], [w_mc, b_mc, dt_bias], [(xbc_raw.shape[1], F32), (LANE, F32)], tm=256, name="conv_m_fwd",
              prev_halo=(0,))


def conv_m_bwd1(dxbc, xbc_raw, ddt, dt_raw, w_mc, b_mc, dt_bias, window):
    def fn(i, nt, rows, vecs, prevs, nexts):
        pre = _conv_fwd(rows[1], _zero_if(i == 0, prevs[0]), vecs[0]) + vecs[1]
        sig = _sigmoid(pre)
        dpre = rows[0] * (sig * (1.0 + pre * (1.0 - sig)))
        ddr = rows[2] * _sigmoid(rows[3] + vecs[2])
        return [dpre, ddr], [jnp.sum(ddr, axis=0, keepdims=True)]

    return ew(fn, [dxbc, xbc_raw, ddt, dt_raw], [w_mc, b_mc, dt_bias], [(dxbc.shape[1], F32), (LANE, BF)], [(1, LANE)], tm=256,
              name="conv_m_bwd1", prev_halo=(1,), window=window)


def conv_bwd2(dpre, x, w, name, window):
    kk = w.shape[0]

    def fn(i, nt, rows, vecs, prevs, nexts):
        dp, xv = rows[0], rows[1]
        nx = _zero_if(i == nt - 1, nexts[0])
        dx = None
        dws = []
        for k in range(kk):
            up = _shift_up(dp, nx, kk - 1 - k)
            term = vecs[0][k:k + 1, :] * up
            dx = term if dx is None else dx + term
            dws.append(jnp.sum(up * xv, axis=0, keepdims=True))
        return [dx], [jnp.concatenate(dws, axis=0), jnp.sum(dp, axis=0, keepdims=True)]

    c = x.shape[1]
    return ew(fn, [dpre, x], [w], [(c, BF)], [(kk, c), (1, c)], tm=256, name=name, next_halo=(0,), window=window)


def conv_a_bwd1(dya, sc3, w_sc, window):
    d = sc3.shape[1] // 3

    def fn(i, nt, rows, vecs, prevs, nexts):
        x, pv = rows[1], _zero_if(i == 0, prevs[0])
        v = x[:, d:2 * d] * x[:, 2 * d:]
        vp = pv[:, d:2 * d] * pv[:, 2 * d:]
        return [rows[0] * x[:, :d], rows[0] * _conv_fwd(v, vp, vecs[0]), v], []

    return ew(fn, [dya, sc3], [w_sc], [(d, F32), (d, BF), (d, F32)], tm=256, name="conv_a_bwd1", prev_halo=(1,), window=window)


def conv_a_bwd2(dcv, v, sc3, w_sc, window):
    d = v.shape[1]
    kk = w_sc.shape[0]

    def fn(i, nt, rows, vecs, prevs, nexts):
        dp, vv, x = rows
        nx = _zero_if(i == nt - 1, nexts[0])
        dv = None
        dws = []
        for k in range(kk):
            up = _shift_up(dp, nx, kk - 1 - k)
            term = vecs[0][k:k + 1, :] * up
            dv = term if dv is None else dv + term
            dws.append(jnp.sum(up * vv, axis=0, keepdims=True))
        return [jnp.concatenate([dv * x[:, 2 * d:], dv * x[:, d:2 * d]], axis=1)], [jnp.concatenate(dws, axis=0)]

    return ew(fn, [dcv, v, sc3], [w_sc], [(2 * d, BF)], [(kk, d)], tm=256, name="conv_a_bwd2", next_halo=(0,), window=window)


def _xdot(a, b, passes, split_lhs, ca=1, cb=0):
    parts, r = [], (a if split_lhs else b)
    for _ in range(passes):
        piece = r.astype(BF)
        parts.append(piece)
        r = r - piece.astype(F32)
    other = (b if split_lhs else a).astype(BF)
    acc = None
    for piece in parts:
        term = _dot(piece, other, ca, cb) if split_lhs else _dot(other, piece, ca, cb)
        acc = term if acc is None else acc + term
    return acc


def _ssd_common(xbc_ref, dt_ref, alog_ref, e_ref, w):
    ll = SSM_L
    xs = xbc_ref[:, 0:w]
    dtv = dt_ref[...]
    a_row = -jnp.exp(alog_ref[...])
    a = dtv * a_row
    row = lax.broadcasted_iota(jnp.int32, (ll, ll), 0)
    col = lax.broadcasted_iota(jnp.int32, (ll, ll), 1)
    tril = (row >= col).astype(F32)
    triu = (row <= col).astype(F32)
    acl = _xdot(tril, a, 3, False)
    acl_t = _xdot(a, triu, 3, True, 0, 0)
    e = e_ref[...]
    aclx = _xdot(acl, e, 3, True)
    dtx = _xdot(dtv, e, 2, True)
    last = aclx[ll - 1:ll, :]
    e_in = jnp.exp(aclx)
    e_end = jnp.exp(last - aclx)
    e_tot = jnp.exp(last)
    x = xs * dtx
    return dict(xs=xs, dtv=dtv, a_row=a_row, a=a, row=row, col=col, triu=triu, acl=acl, acl_t=acl_t, dtx=dtx, e_in=e_in, e_end=e_end,
                e_tot=e_tot, x=x)


def _decay(q, hh):
    diff = q["acl"][:, hh:hh + 1] - q["acl_t"][hh:hh + 1, :]
    return jnp.exp(jnp.where(q["row"] >= q["col"], diff, -jnp.inf))


def ssd_fwd(xbc, dt, z, a_log, d_exp, m_norm, e_mat, rider=None):
    t = xbc.shape[0]
    w = z.shape[1]
    gn = SSM_G * SSM_N
    gw = w // SSM_G
    ll, nn = SSM_L, SSM_N
    nc = t // ll
    cw = xbc.shape[1]

    def body(xbc_ref, dt_ref, z_ref, alog_ref, dexp_ref, mn_ref, e_ref, yn_ref, y_ref, sp_ref, s_sc):
        c = pl.program_id(0)

        @pl.when(c == 0)
        def _():
            s_sc[...] = jnp.zeros_like(s_sc)

        q = _ssd_common(xbc_ref, dt_ref, alog_ref, e_ref, w)
        xb = q["x"].astype(BF)
        xsb = (q["x"] * q["e_end"]).astype(BF)
        sp = s_sc[...]
        sp_ref[0] = sp
        spb = sp.astype(BF)
        lane = lax.broadcasted_iota(jnp.int32, (ll, LANE), 1)
        for g in range(SSM_G):
            lo = g * gw
            bg = xbc_ref[:, w + g * nn:w + (g + 1) * nn].astype(BF)
            cg = xbc_ref[:, w + gn + g * nn:w + gn + (g + 1) * nn].astype(BF)
            yoff = _dot(cg, spb[:, lo:lo + gw]) * q["e_in"][:, lo:lo + gw]
            s_sc[:, lo:lo + gw] = sp[:, lo:lo + gw] * q["e_tot"][:, lo:lo + gw] + _dot(bg, xsb[:, lo:lo + gw], 0, 0)
            cb = _dot(cg, bg, 1, 1)
            for pr in range(gw // LANE):
                l0 = lo + pr * LANE
                xp = xb[:, l0:l0 + LANE]
                ys = []
                for hh in (l0 // SSM_P, l0 // SSM_P + 1):
                    wm = (cb * _decay(q, hh)).astype(BF)
                    ys.append(_dot(wm, xp))
                ydiag = jnp.where(lane < SSM_P, ys[0], ys[1])
                y_ref[:, l0:l0 + LANE] = ydiag + yoff[:, pr * LANE:(pr + 1) * LANE] + dexp_ref[:, l0:l0 + LANE] * q["xs"][:, l0:l0 + LANE]
        zv = z_ref[...].astype(F32)
        yz = y_ref[...] * (zv * _sigmoid(zv))
        for g in range(SSM_G):
            lo = g * gw
            yn_ref[:, lo:lo + gw] = _rms(yz[:, lo:lo + gw], mn_ref[:, lo:lo + gw]).astype(BF)

    vec = lambda s: pl.BlockSpec(s, lambda c: (0, 0))
    return host_call(
        body,
        name="ssd_fwd",
        grid=(nc,),
        in_specs=[
            pl.BlockSpec((ll, cw), lambda c: (c, 0)), pl.BlockSpec((ll, LANE), lambda c: (c, 0)), pl.BlockSpec((ll, w), lambda c: (c, 0)),
            vec((1, LANE)), vec((1, w)), vec((1, w)), vec((LANE, w)),
        ],
        out_specs=[pl.BlockSpec((ll, w), lambda c: (c, 0)), pl.BlockSpec((ll, w), lambda c: (c, 0)), pl.BlockSpec((1, nn, w), lambda c: (c, 0, 0))],
        out_shape=[jax.ShapeDtypeStruct((t, w), BF), jax.ShapeDtypeStruct((t, w), F32), jax.ShapeDtypeStruct((nc, nn, w), F32)],
        scratch_shapes=[pltpu.VMEM((nn, w), F32)],
        operands=(xbc, dt, z, a_log, d_exp, m_norm, e_mat),
        rider=rider,
    )


def ssd_bwd(dyn, y, z, xbc, dt, sprev, a_log, d_exp, m_norm, e_mat, et_mat, window, rider=None):
    t = xbc.shape[0]
    w = z.shape[1]
    gn = SSM_G * SSM_N
    gw = w // SSM_G
    ll, nn = SSM_L, SSM_N
    nc = t // ll
    cw = xbc.shape[1]

    def body(dyn_ref, y_ref, z_ref, xbc_ref, dt_ref, sp_ref, alog_ref, dexp_ref, mn_ref, e_ref, et_ref,
             dz_ref, dxbc_ref, ddt_ref, dmn_ref, dd_ref, dal_ref, ds_sc, dy_sc, dx_sc):
        step = pl.program_id(0)

        @pl.when(step == 0)
        def _():
            ds_sc[...] = jnp.zeros_like(ds_sc)

        zv, yv = z_ref[...].astype(F32), y_ref[...]
        sg = _sigmoid(zv)
        sz = zv * sg
        yz = yv * sz
        dmn = []
        for g in range(SSM_G):
            lo = g * gw
            dseg, dmn_g = _rms_bwd(yz[:, lo:lo + gw], mn_ref[:, lo:lo + gw], dyn_ref[:, lo:lo + gw])
            dy_sc[:, lo:lo + gw] = dseg
            dmn.append(dmn_g)
        dmn = jnp.concatenate(dmn, axis=1)
        dyz = dy_sc[...]
        dz_ref[...] = (dyz * yv * (sg * (1.0 + zv * (1.0 - sg)))).astype(BF)
        dy = dyz * sz

        q = _ssd_common(xbc_ref, dt_ref, alog_ref, e_ref, w)
        x = q["x"]
        xb = x.astype(BF)
        xsb = (x * q["e_end"]).astype(BF)
        sp = sp_ref[0]
        spb = sp.astype(BF)
        dsn = ds_sc[...]
        dsnb = dsn.astype(BF)
        dyb = dy.astype(BF)
        lane = lax.broadcasted_iota(jnp.int32, (ll, LANE), 1)
        lane1 = lax.broadcasted_iota(jnp.int32, (1, LANE), 1)
        sub1 = lax.broadcasted_iota(jnp.int32, (LANE, 1), 0)
        dacl = jnp.zeros((ll, LANE), F32)
        dacl_t = jnp.zeros((LANE, ll), F32)
        d_ein, d_eend, d_etot = [], [], []
        for g in range(SSM_G):
            lo = g * gw
            sl = slice(lo, lo + gw)
            bg = xbc_ref[:, w + g * nn:w + (g + 1) * nn].astype(BF)
            cg = xbc_ref[:, w + gn + g * nn:w + gn + (g + 1) * nn].astype(BF)
            zg = _dot(cg, spb[:, sl])
            dzz = (dy[:, sl] * q["e_in"][:, sl]).astype(BF)
            d_ein.append(dy[:, sl] * zg)
            dcg = _dot(dzz, spb[:, sl], 1, 1)
            ds_sc[:, sl] = _dot(cg, dzz, 0, 0) + dsn[:, sl] * q["e_tot"][:, sl]
            d_etot.append(jnp.sum(dsn[:, sl] * sp[:, sl], axis=0, keepdims=True))
            dbg = _dot(xsb[:, sl], dsnb[:, sl], 1, 1)
            dxs_g = _dot(bg, dsnb[:, sl])
            d_eend.append(dxs_g * x[:, sl])
            cb = _dot(cg, bg, 1, 1)
            dcb = jnp.zeros((ll, ll), F32)
            for pr in range(gw // LANE):
                l0 = lo + pr * LANE
                xp = xb[:, l0:l0 + LANE]
                dyp = dyb[:, l0:l0 + LANE]
                dxp = []
                for hi, hh in enumerate((l0 // SSM_P, l0 // SSM_P + 1)):
                    lm = _decay(q, hh)
                    wm = (cb * lm).astype(BF)
                    in_head = (lane < SSM_P) if hi == 0 else (lane >= SSM_P)
                    dwm = _dot(jnp.where(in_head, dyp, jnp.zeros_like(dyp)), xp, 1, 1)
                    dxp.append(_dot(wm, dyp, 0, 0))
                    dlm = dwm * lm
                    dcb = dcb + dlm
                    dd = dlm * cb
                    dacl = dacl + jnp.sum(dd, axis=1, keepdims=True) * (lane1 == hh).astype(F32)
                    dacl_t = dacl_t + (sub1 == hh).astype(F32) * jnp.sum(dd, axis=0, keepdims=True)
                dx_sc[:, l0:l0 + LANE] = jnp.where(lane < SSM_P, dxp[0], dxp[1]) + dxs_g[:, pr * LANE:(pr + 1) * LANE] * q["e_end"][:, l0:l0 + LANE]
            dcbb = dcb.astype(BF)
            dxbc_ref[:, w + g * nn:w + (g + 1) * nn] = dbg + _dot(dcbb, cg, 0, 0)
            dxbc_ref[:, w + gn + g * nn:w + gn + (g + 1) * nn] = dcg + _dot(dcbb, bg)
        d_ein = jnp.concatenate(d_ein, axis=1) * q["e_in"]
        d_eend = jnp.concatenate(d_eend, axis=1) * q["e_end"]
        d_etot = jnp.concatenate(d_etot, axis=1) * q["e_tot"]
        et = et_ref[...]
        last_add = jnp.sum(d_eend, axis=0, keepdims=True) + d_etot
        last_add = _xdot(jnp.broadcast_to(last_add, (HALO, w)), et, 2, True)[0:1]
        row1 = lax.broadcasted_iota(jnp.int32, (ll, LANE), 0)
        dacl = dacl + _xdot(d_ein - d_eend, et, 2, True) + jnp.where(row1 == ll - 1, last_add, 0.0)
        da = _xdot(q["triu"], dacl, 2, False) - _xdot(q["triu"], dacl_t, 2, False, 1, 1)
        dxv = dx_sc[...]
        dxbc_ref[:, 0:w] = dexp_ref[...] * dy + dxv * q["dtx"]
        ddt_ref[...] = _xdot(dxv * q["xs"], et, 2, True) + da * q["a_row"]
        dal = jnp.sum(da * q["dtv"], axis=0, keepdims=True) * q["a_row"]
        ddv = jnp.sum(dy * q["xs"], axis=0, keepdims=True)
        ddv = _xdot(jnp.broadcast_to(ddv, (HALO, w)), et, 2, True)[0:1]
        _accumulate(dmn_ref, dmn, step == 0)
        _accumulate(dd_ref, ddv, step == 0)
        _accumulate(dal_ref, dal, step == 0)

    rev = lambda c_: pl.BlockSpec((ll, c_), lambda s: (nc - 1 - s, 0))
    vec = lambda s_: pl.BlockSpec(s_, lambda s: (0, 0))
    n_in = 11

    def body_skipping_buffer(*refs):
        body(*refs[:n_in], *refs[n_in + 1:])

    return host_call(
        body_skipping_buffer,
        name="ssd_bwd",
        grid=(nc,),
        in_specs=[
            rev(w), rev(w), rev(w), rev(cw), rev(LANE), pl.BlockSpec((1, nn, w), lambda s: (nc - 1 - s, 0, 0)),
            vec((1, LANE)), vec((1, w)), vec((1, w)), vec((LANE, w)), vec((w, LANE)), ANY,
        ],
        out_specs=[pl.BlockSpec((ll, w), lambda s: (nc - 1 - s, window.block)), rev(cw), rev(LANE), vec((1, w)), vec((1, LANE)), vec((1, LANE))],
        out_shape=[
            jax.ShapeDtypeStruct((t, window.cols), BF), jax.ShapeDtypeStruct((t, cw), F32), jax.ShapeDtypeStruct((t, LANE), F32),
            jax.ShapeDtypeStruct((1, w), F32), jax.ShapeDtypeStruct((1, LANE), F32), jax.ShapeDtypeStruct((1, LANE), F32),
        ],
        scratch_shapes=[pltpu.VMEM((nn, w), F32), pltpu.VMEM((ll, w), F32), pltpu.VMEM((ll, w), F32)],
        operands=(dyn, y, z, xbc, dt, sprev, a_log, d_exp, m_norm, e_mat, et_mat, window.buf),
        rider=rider,
        aliases={n_in: 0},
    )


def _w1024_spec(d, nblk, iblk):
    r = nblk * (d // NCHIP)
    return pl.BlockSpec((NCHIP, r, d), lambda i: (0, iblk // nblk, 0))


def _whole(ref):
    v = ref[...]
    return v.reshape(v.shape[0] * v.shape[1], v.shape[2])


def mix_out_fwd(ya_in, yn, gates, h, w1024, rider=None):
    t, d = h.shape
    tm = _tile(t, 256)

    def body(ya_ref, yn_ref, g_ref, h_ref, wm_ref, wa_ref, wo_ref, ho_ref, oa_ref, om_ref, mg_ref):
        y_a = _dot(ya_ref[...], _whole(wa_ref))
        y_m = _dot(yn_ref[...], _whole(wm_ref))
        oa_ref[...] = y_a
        om_ref[...] = y_m
        gv = g_ref[...].astype(F32)
        mg = (_sigmoid(gv[:, :d]) * y_a + _sigmoid(gv[:, d:]) * y_m).astype(BF)
        mg_ref[...] = mg
        ho_ref[...] = h_ref[...] + _dot(mg, _whole(wo_ref))

    row = lambda c: pl.BlockSpec((tm, c), lambda i: (i, 0))
    return host_call(
        body,
        name="mix_out_fwd",
        grid=(t // tm,),
        in_specs=[row(d), row(2 * d), row(2 * d), row(d), _w1024_spec(d, 2, 0), _w1024_spec(d, 1, 2), _w1024_spec(d, 1, 3)],
        out_specs=[row(d), row(d), row(d), row(d)],
        out_shape=[jax.ShapeDtypeStruct((t, d), F32), jax.ShapeDtypeStruct((t, d), F32), jax.ShapeDtypeStruct((t, d), F32),
                   jax.ShapeDtypeStruct((t, d), BF)],
        scratch_shapes=[],
        operands=(ya_in, yn, gates, h, w1024, w1024, w1024),
        rider=rider,
    )


def mix_out_bwd(dh, gates, y_a, y_m, w1024, cols):
    t, d = dh.shape
    tm = _tile(t, 256)

    def body(dh_ref, g_ref, ya_ref, ym_ref, wm_ref, wa_ref, wo_ref, dg_ref, dya_ref, dyn_ref, da_ref, dm_ref):
        dmg = _dot(dh_ref[...].astype(BF), _whole(wo_ref), 1, 1)
        gv = g_ref[...].astype(F32)
        sa, sm = _sigmoid(gv[:, :d]), _sigmoid(gv[:, d:])
        dg_ref[:, :d] = (dmg * ya_ref[...] * sa * (1.0 - sa)).astype(BF)
        dg_ref[:, d:] = (dmg * ym_ref[...] * sm * (1.0 - sm)).astype(BF)
        da = (dmg * sa).astype(BF)
        dm = (dmg * sm).astype(BF)
        da_ref[...] = da
        dm_ref[...] = dm
        dya_ref[...] = _dot(da, _whole(wa_ref), 1, 1)
        dyn_ref[...] = _dot(dm, _whole(wm_ref), 1, 1)

    row = lambda c: pl.BlockSpec((tm, c), lambda i: (i, 0))
    return pl.pallas_call(
        body,
        name="mix_out_bwd",
        grid=(t // tm,),
        in_specs=[row(d), row(2 * d), row(d), row(d), _w1024_spec(d, 2, 0), _w1024_spec(d, 1, 2), _w1024_spec(d, 1, 3)],
        out_specs=[row(2 * d), row(d), row(2 * d), row(d), row(d)],
        out_shape=[jax.ShapeDtypeStruct((t, cols), BF), jax.ShapeDtypeStruct((t, d), F32), jax.ShapeDtypeStruct((t, 2 * d), F32),
                   jax.ShapeDtypeStruct((t, d), BF), jax.ShapeDtypeStruct((t, d), BF)],
        compiler_params=_params(("parallel",)),
    )(dh, gates, y_a, y_m, w1024, w1024, w1024)


def proj_in_bwd(dproj, w_in_p, dh, h, g, rider=None):
    t, kdim = dproj.shape
    d = h.shape[1]
    tm, tk = _tile(t, 512), _tile(kdim, 4096)
    nk = kdim // tk

    def body(a_ref, b_ref, dh_ref, h_ref, g_ref, o_ref, dg_ref, acc):
        i, kk = pl.program_id(0), pl.program_id(1)
        _accumulate(acc, _dot(a_ref[...], b_ref[...]), kk == 0)

        @pl.when(kk == nk - 1)
        def _():
            dx, dg = _rms_bwd(h_ref[...], g_ref[...], acc[...])
            o_ref[...] = dh_ref[...] + dx
            _accumulate(dg_ref, dg, i == 0)

    row = pl.BlockSpec((tm, d), lambda i, kk: (i, 0))
    vec = pl.BlockSpec((1, d), lambda i, kk: (0, 0))
    (out, dg), r_outs = host_call(
        body,
        name="d_proj_in",
        grid=(t // tm, nk),
        in_specs=[pl.BlockSpec((tm, tk), lambda i, kk: (i, kk)), pl.BlockSpec((tk, d), lambda i, kk: (kk, 0)), row, row, vec],
        out_specs=[row, vec],
        out_shape=[jax.ShapeDtypeStruct((t, d), F32), jax.ShapeDtypeStruct((1, d), F32)],
        scratch_shapes=[pltpu.VMEM((tm, d), F32)],
        operands=(dproj, w_in_p, dh, h, g),
        rider=rider,
    )
    return out, dg, r_outs


def norm_bwd_add(dh, h, g, dn):
    def fn(i, nt, rows, vecs, prevs, nexts):
        dx, dg = _rms_bwd(rows[1], vecs[0], rows[2])
        return [rows[0] + dx], [dg]
    d = h.shape[1]
    return ew(fn, [dh, h, dn], [g], [(d, F32)], [(1, d)], tm=1024, name="norm_bwd_add")


def _pe(p, wpp_ref):
    pb = p.astype(BF)
    return jnp.concatenate([_dot(pb, wpp_ref[k]) for k in range(NCHIP)], axis=1)


def ple_fwd(h, g, p, w1024, wpp):
    t, d = h.shape
    tm = _tile(t, 512)

    def body(h_ref, g_ref, p_ref, wg_ref, wp_ref, ho_ref):
        hv = h_ref[...]
        gate = _sigmoid(_dot(_rms(hv, g_ref[...]).astype(BF), _whole(wg_ref)))
        ho_ref[...] = hv + gate * _pe(p_ref[...], wp_ref)

    row = lambda c: pl.BlockSpec((tm, c), lambda i: (i, 0))
    wpp_spec = pl.BlockSpec(wpp.shape, lambda i: (0, 0, 0))
    return pl.pallas_call(
        body,
        name="ple_fwd",
        grid=(t // tm,),
        in_specs=[row(d), pl.BlockSpec((1, d), lambda i: (0, 0)), row(p.shape[1]), _w1024_spec(d, 1, 4), wpp_spec],
        out_specs=row(d),
        out_shape=jax.ShapeDtypeStruct((t, d), F32),
        compiler_params=_params(("parallel",)),
    )(h, g, p, w1024, wpp)


def ple_bwd(dho, h, g, p, w1024, wpp):
    t, d = h.shape
    tm = _tile(t, 512)

    def body(dho_ref, h_ref, g_ref, p_ref, wg_ref, wp_ref, dh_ref, dg_ref, n_ref, dgp_ref, dpe_ref):
        hv, dv = h_ref[...], dho_ref[...]
        n = _rms(hv, g_ref[...]).astype(BF)
        n_ref[...] = n
        wg = _whole(wg_ref)
        gate = _sigmoid(_dot(n, wg))
        pe = _pe(p_ref[...], wp_ref)
        dpe_ref[...] = (dv * gate).astype(BF)
        dgp = (dv * pe * gate * (1.0 - gate)).astype(BF)
        dgp_ref[...] = dgp
        dx, dg = _rms_bwd(hv, g_ref[...], _dot(dgp, wg, 1, 1))
        dh_ref[...] = dv + dx
        _accumulate(dg_ref, dg, pl.program_id(0) == 0)

    row = lambda c: pl.BlockSpec((tm, c), lambda i: (i, 0))
    wpp_spec = pl.BlockSpec(wpp.shape, lambda i: (0, 0, 0))
    return pl.pallas_call(
        body,
        name="ple_bwd",
        grid=(t // tm,),
        in_specs=[row(d), row(d), pl.BlockSpec((1, d), lambda i: (0, 0)), row(p.shape[1]), _w1024_spec(d, 1, 4), wpp_spec],
        out_specs=[row(d), pl.BlockSpec((1, d), lambda i: (0, 0)), row(d), row(d), row(d)],
        out_shape=[jax.ShapeDtypeStruct((t, d), F32), jax.ShapeDtypeStruct((1, d), F32), jax.ShapeDtypeStruct((t, d), BF),
                   jax.ShapeDtypeStruct((t, d), BF), jax.ShapeDtypeStruct((t, d), BF)],
        compiler_params=_params(("arbitrary",)),
    )(dho, h, g, p, w1024, wpp)


def loss_bwd(h, g, target):
    d = h.shape[1]

    def fn(i, nt, rows, vecs, prevs, nexts):
        err = _rms(rows[0], vecs[0]) - rows[1]
        dx, dg = _rms_bwd(rows[0], vecs[0], err * (1.0 / d))
        return [dx], [jnp.sum(err * err, axis=0, keepdims=True) * (0.5 / d), dg]

    return ew(fn, [h, target], [g], [(d, F32)], [(1, d), (1, d)], tm=512, name="loss_bwd")


def adamw(w, g, m, v, name):
    c1, c2 = 1.0 / (1.0 - ADAM_B1 ** ADAM_STEP), 1.0 / (1.0 - ADAM_B2 ** ADAM_STEP)

    def fn(i, nt, rows, vecs, prevs, nexts):
        wv, gv, mv, vv = rows
        mn = ADAM_B1 * mv + (1.0 - ADAM_B1) * gv
        vn = ADAM_B2 * vv + (1.0 - ADAM_B2) * (gv * gv)
        delta = -ADAM_LR * ((mn * c1) / (jnp.sqrt(vn * c2) + ADAM_EPS) + ADAM_WD * wv)
        return [delta, mn, vn], []

    c = w.shape[1]
    return ew(fn, [w, g, m, v], [], [(c, F32)] * 3, tm=_row_tile(w.shape[0], c, HALO), name=name)


def _place():
    return lax.axis_index("x"), lax.axis_index("y"), lax.axis_index("c")


def _other_chips(x, y):
    return [(1 - x, y), (x, 1 - y), (1 - x, 1 - y)]


ANY = pl.BlockSpec(memory_space=pl.ANY)


def _comm_call(body, name, ins, out_shapes, n_sems, aliases=None):
    return pl.pallas_call(
        body,
        name=name,
        in_specs=[ANY] * len(ins),
        out_specs=[ANY] * len(out_shapes),
        out_shape=out_shapes,
        scratch_shapes=[pltpu.SemaphoreType.DMA((n_sems,)), pltpu.SemaphoreType.DMA((n_sems,))],
        input_output_aliases=aliases or {},
    )(*ins)


def gather_rider(packs):
    nt = len(packs)

    def pieces(ins, outs, send_sems, recv_sems):
        x, y, cc = _place()
        chips = _other_chips(x, y)
        sibling = (x, y, 1 - cc)
        k_me = 2 * x + y

        def copy(k, src, dst, to):
            return pltpu.make_async_remote_copy(src_ref=src, dst_ref=dst, send_sem=send_sems.at[k], recv_sem=recv_sems.at[k],
                                                device_id=to, device_id_type=MESH)

        sends, forwards, arrivals = [], [], []
        for ti in range(nt):
            for j, (px, py) in enumerate(chips):
                sends.append(copy(7 * ti + j, ins[ti].at[cc], outs[ti].at[k_me, cc], (px, py, cc)))
                landed = outs[ti].at[2 * px + py, cc]
                forwards.append((copy(7 * ti + j, landed, landed, (px, py, cc)), copy(7 * ti + 3 + j, landed, landed, sibling)))
                passed = outs[ti].at[2 * px + py, 1 - cc]
                arrivals.append(copy(7 * ti + 3 + j, passed, passed, sibling))
            sends.append(copy(7 * ti + 6, ins[ti], outs[ti].at[k_me], sibling))
            own = outs[ti].at[k_me]
            arrivals.append(copy(7 * ti + 6, own, own, sibling))
        return sends, forwards, arrivals

    def start(*parts):
        for cp in pieces(*parts)[0]:
            cp.start()

    def mid(*parts):
        for landed, forward in pieces(*parts)[1]:
            landed.wait_recv()
            forward.start()

    def finish(*parts):
        sends, forwards, arrivals = pieces(*parts)
        for cp in arrivals:
            cp.wait_recv()
        for cp in sends + [f for _, f in forwards]:
            cp.wait_send()

    return Rider(packs, [jax.ShapeDtypeStruct((NCHIP,) + p.shape, p.dtype) for p in packs], 7 * nt, start, finish, mid)


def swap_rider(gs):
    nt = len(gs)
    hl = gs[0].shape[1] // 2

    def copies(ins, outs, send_sems, recv_sems):
        x, y, cc = _place()
        theirs = pl.ds((1 - cc) * hl, hl)
        return [pltpu.make_async_remote_copy(src_ref=ins[ti].at[:, theirs], dst_ref=outs[ti], send_sem=send_sems.at[ti], recv_sem=recv_sems.at[ti],
                                             device_id=(x, y, 1 - cc), device_id_type=MESH) for ti in range(nt)]

    def start(*parts):
        for cp in copies(*parts):
            cp.start()

    def finish(*parts):
        for cp in copies(*parts):
            cp.wait()

    return Rider(gs, [jax.ShapeDtypeStruct((NCHIP, hl) + g.shape[2:], g.dtype) for g in gs], nt, start, finish)


def scatter_packs(cs, name):
    return scatter_rider(cs).standalone(name)


def scatter_rider(cs):
    nt = len(cs)

    def copies(ins, outs, send_sems, recv_sems):
        x, y, cc = _place()
        cps = []
        for ti in range(nt):
            for j, (px, py) in enumerate(_other_chips(x, y)):
                cps.append(pltpu.make_async_remote_copy(src_ref=ins[ti].at[2 * px + py], dst_ref=outs[ti].at[j], send_sem=send_sems.at[3 * ti + j],
                                                        recv_sem=recv_sems.at[3 * ti + j], device_id=(px, py, cc), device_id_type=MESH))
        return cps

    def start(*parts):
        for cp in copies(*parts):
            cp.start()

    def finish(*parts):
        for cp in copies(*parts):
            cp.wait()

    return Rider(cs, [jax.ShapeDtypeStruct((3,) + c_.shape[1:], c_.dtype) for c_ in cs], 3 * nt, start, finish)


def join_packs(fulls, name):
    nt = len(fulls)
    hl = fulls[0].shape[0] // 2

    def body(*refs):
        ins, outs, (send_sems, recv_sems) = refs[:nt], refs[nt:2 * nt], refs[2 * nt:]
        x, y, cc = _place()
        mine = pl.ds(cc * hl, hl)
        cps = [pltpu.make_async_remote_copy(src_ref=ins[ti].at[mine], dst_ref=outs[ti].at[mine], send_sem=send_sems.at[ti], recv_sem=recv_sems.at[ti],
                                            device_id=(x, y, 1 - cc), device_id_type=MESH) for ti in range(nt)]
        for cp in cps:
            cp.start()
        for cp in cps:
            cp.wait()

    return _comm_call(body, name, fulls, [jax.ShapeDtypeStruct(f.shape, f.dtype) for f in fulls], nt, aliases={ti: ti for ti in range(nt)})


def add_sibling(g, recv, name):
    _, nl, r, c = g.shape
    hl = nl // 2
    tm, tc = _tile2(r, c)

    def body(g_ref, r_ref, o_ref):
        o_ref[...] = (g_ref[...].astype(F32) + r_ref[...].astype(F32)).astype(o_ref.dtype)

    blk = (None, None, tm, tc)
    return pl.pallas_call(
        body,
        name=name,
        grid=(NCHIP, hl, r // tm, c // tc),
        in_specs=[pl.BlockSpec(blk, lambda k, l, i, j: (k, lax.axis_index("c") * hl + l, i, j)), pl.BlockSpec(blk, lambda k, l, i, j: (k, l, i, j))],
        out_specs=pl.BlockSpec(blk, lambda k, l, i, j: (k, l, i, j)),
        out_shape=jax.ShapeDtypeStruct(recv.shape, BF),
        compiler_params=_params(("parallel",) * 4),
    )(g, recv)


def add_chips(cs, got, nl, name):
    _, hl, r, c = cs.shape
    tm, tc = _tile2(r, c)

    def body(own_ref, got_ref, o_ref):
        o_ref[...] = own_ref[...].astype(F32) + got_ref[0].astype(F32) + got_ref[1].astype(F32) + got_ref[2].astype(F32)

    return pl.pallas_call(
        body,
        name=name,
        grid=(hl, r // tm, c // tc),
        in_specs=[pl.BlockSpec((None, None, tm, tc), lambda l, i, j: (2 * lax.axis_index("x") + lax.axis_index("y"), l, i, j)),
                  pl.BlockSpec((3, None, tm, tc), lambda l, i, j: (0, l, i, j))],
        out_specs=pl.BlockSpec((None, tm, tc), lambda l, i, j: (lax.axis_index("c") * hl + l, i, j)),
        out_shape=jax.ShapeDtypeStruct((nl, r, c), F32),
        compiler_params=_params(("parallel",) * 3),
    )(cs, got)


def all_gather_xy(shard, name):
    r, c = shard.shape
    hr = r // 2
    assert r % 32 == 0

    def body(x_ref, out_ref, send_sems, recv_sems, local_sem):
        x, y, cc = _place()
        chips = _other_chips(x, y)
        mine = pl.ds(pl.multiple_of(cc * hr, 16), hr)
        theirs = pl.ds(pl.multiple_of((1 - cc) * hr, 16), hr)
        k_me = 2 * x + y

        def copy(k, src, dst, to):
            return pltpu.make_async_remote_copy(src_ref=src, dst_ref=dst, send_sem=send_sems.at[k], recv_sem=recv_sems.at[k],
                                                device_id=to, device_id_type=MESH)

        own = pltpu.make_async_copy(x_ref, out_ref.at[k_me], local_sem)
        own.start()
        first = [copy(j, x_ref.at[mine], out_ref.at[k_me, mine], (*chip, cc)) for j, chip in enumerate(chips)]
        for cp in first:
            cp.start()
        passed = []
        for j, (px, py) in enumerate(chips):
            landed = out_ref.at[2 * px + py, mine]
            copy(j, landed, landed, (px, py, cc)).wait_recv()
            fw = copy(3 + j, landed, landed, (x, y, 1 - cc))
            fw.start()
            passed.append(fw)
        for j, (px, py) in enumerate(chips):
            landed = out_ref.at[2 * px + py, theirs]
            copy(3 + j, landed, landed, (x, y, 1 - cc)).wait_recv()
        for cp in first + passed:
            cp.wait_send()
        own.wait()

    return pl.pallas_call(
        body,
        name=name,
        in_specs=[ANY],
        out_specs=ANY,
        out_shape=jax.ShapeDtypeStruct((NCHIP, r, c), shard.dtype),
        scratch_shapes=[pltpu.SemaphoreType.DMA((6,)), pltpu.SemaphoreType.DMA((6,)), pltpu.SemaphoreType.DMA],
    )(shard)


def all_gather_8(block, name):
    m, c = block.shape

    def body(x_ref, out_ref, send_sems, recv_sems, local_sem):
        x, y, cc = _place()
        me, sibling = (x, y, cc), (x, y, 1 - cc)
        chips = _other_chips(x, y)

        def rows(px, py, pc):
            return out_ref.at[4 * px + 2 * py + pc]

        def copy(k, blk, to, src=None):
            return pltpu.make_async_remote_copy(src_ref=rows(*blk) if src is None else src, dst_ref=rows(*blk), send_sem=send_sems.at[k],
                                                recv_sem=recv_sems.at[k], device_id=to, device_id_type=MESH)

        mine = pltpu.make_async_copy(x_ref, rows(*me), local_sem)
        mine.start()
        first = [copy(0, me, sibling, src=x_ref)]
        first += [copy(1 + j, me, (*chip, cc), src=x_ref) for j, chip in enumerate(chips)]
        for cp in first:
            cp.start()
        passed = [copy(4 + j, (*chip, cc), sibling) for j, chip in enumerate(chips)]
        for j, chip in enumerate(chips):
            copy(1 + j, (*chip, cc), me).wait_recv()
            passed[j].start()
        copy(0, sibling, me).wait_recv()
        for j, chip in enumerate(chips):
            copy(4 + j, (*chip, 1 - cc), me).wait_recv()
        for cp in first + passed:
            cp.wait_send()
        mine.wait()

    return pl.pallas_call(
        body,
        name=name,
        in_specs=[pl.BlockSpec(memory_space=pltpu.VMEM)],
        out_specs=pl.BlockSpec(memory_space=pltpu.VMEM),
        out_shape=jax.ShapeDtypeStruct((8, m, c), block.dtype),
        scratch_shapes=[pltpu.SemaphoreType.DMA((7,)), pltpu.SemaphoreType.DMA((7,)), pltpu.SemaphoreType.DMA],
        compiler_params=pltpu.CompilerParams(vmem_limit_bytes=VMEM_LIMIT),
    )(block)


def add_parts(parts, out_dtype, name, tm=512):
    def fn(i, nt, rows, vecs, prevs, nexts):
        acc = rows[0]
        for r_ in rows[1:]:
            acc = acc + r_
        return [acc], []
    r, c = parts[0].shape
    return ew(fn, list(parts), [], [(c, out_dtype)], tm=_tile(r, tm, 16), name=name)[0]


SMALL_SHARDED = ("sc_conv_w", "m_conv_w")
SMALL_REPL = ("ffn1_norm", "mix_norm", "m_conv_b", "m_dt_bias", "m_A_log", "m_D", "m_norm", "ffn2_norm", "ple_norm", "final_norm")
BIG = ("ffn1_wg", "ffn1_wu", "ffn1_wd", "w_in", "sc_w_out", "m_w_out", "w_o", "ffn2_wg", "ffn2_wu", "ffn2_wd", "ple_w_gate", "ple_w_proj")
TRANSPOSED = ("ffn1_wg", "ffn1_wu", "ffn2_wg", "ffn2_wu", "w_in")
ORDER = ("ffn1_norm", "ffn1_wg", "ffn1_wu", "ffn1_wd", "mix_norm", "w_in", "sc_conv_w", "sc_w_out", "m_conv_w", "m_conv_b", "m_dt_bias",
         "m_A_log", "m_D", "m_norm", "m_w_out", "w_o", "ffn2_norm", "ffn2_wg", "ffn2_wu", "ffn2_wd", "ple_norm", "ple_w_gate", "ple_w_proj",
         "final_norm")


def _pack(arrs, cols, row_mult):
    flat = jnp.concatenate([a.reshape(-1) for a in arrs])
    n = flat.shape[0]
    rows = -(-n // cols)
    rows = -(-rows // row_mult) * row_mult
    return jnp.pad(flat, (0, rows * cols - n)).reshape(rows, cols)


def _unpack(flat2d, shapes):
    flat = flat2d.reshape(-1)
    out, off = [], 0
    for s in shapes:
        n = int(np.prod(s))
        out.append(flat[off:off + n].reshape(s))
        off += n
    return out


def _row_cat(arrs, dtype):
    return jnp.concatenate([a.astype(dtype) for a in arrs], axis=1)


def kernel(x, p, ffn1_norm, ffn1_wg, ffn1_wu, ffn1_wd, mix_norm, w_in, sc_conv_w, sc_w_out, m_conv_w, m_conv_b, m_dt_bias, m_A_log, m_D, m_norm, m_w_out, w_o, ffn2_norm, ffn2_wg, ffn2_wu, ffn2_wd, ple_norm, ple_w_gate, ple_w_proj, final_norm, loss_target, m_ffn1_norm, m_ffn1_wg, m_ffn1_wu, m_ffn1_wd, m_mix_norm, m_w_in, m_sc_conv_w, m_sc_w_out, m_m_conv_w, m_m_conv_b, m_m_dt_bias, m_m_A_log, m_m_D, m_m_norm, m_m_w_out, m_w_o, m_ffn2_norm, m_ffn2_wg, m_ffn2_wu, m_ffn2_wd, m_ple_norm, m_ple_w_gate, m_ple_w_proj, m_final_norm, v_ffn1_norm, v_ffn1_wg, v_ffn1_wu, v_ffn1_wd, v_mix_norm, v_w_in, v_sc_conv_w, v_sc_w_out, v_m_conv_w, v_m_conv_b, v_m_dt_bias, v_m_A_log, v_m_D, v_m_norm, v_m_w_out, v_w_o, v_ffn2_norm, v_ffn2_wg, v_ffn2_wu, v_ffn2_wd, v_ple_norm, v_ple_w_gate, v_ple_w_proj, v_final_norm):
    args = dict(locals())
    wts = {n: args[n] for n in ORDER}
    mom = {n: args["m_" + n] for n in ORDER}
    vel = {n: args["v_" + n] for n in ORDER}

    depth = ffn1_norm.shape[0]
    d = x.shape[-1]
    w = 2 * d
    hh = w // SSM_P
    cw = w + 2 * SSM_G * SSM_N
    d4 = d // NCHIP
    pp = 7 * d + cw + LANE
    my_x, my_y, my_c = _place()
    k_me = 2 * my_x + my_y

    tr = lambda a: jnp.swapaxes(a, 1, 2)
    gu_t = [_row_cat([tr(wg_), tr(wu_)], BF) for wg_, wu_ in ((ffn1_wg, ffn1_wu), (ffn2_wg, ffn2_wu))]
    wd_l = [ffn1_wd.astype(BF), ffn2_wd.astype(BF)]
    w1024_l = _row_cat([m_w_out, sc_w_out, w_o, ple_w_gate], BF)
    p4 = w_in.shape[2]
    p4p = -(-p4 // 32) * 32
    win_l, wpp_l = jnp.pad(tr(w_in).astype(BF), ((0, 0), (0, p4p - p4), (0, 0))), ple_w_proj.astype(BF)
    halves = lambda a: a.reshape(2, a.shape[0] // 2, a.shape[1])
    whole = lambda g: g.reshape(NCHIP, g.shape[2] * 2, g.shape[3])

    def pieces(l):
        return {"small": [halves(w1024_l[l]), halves(wpp_l[l])], "win": [halves(win_l[l])], "gu1": [halves(gu_t[0][l])], "d1": [halves(wd_l[0][l])],
                "gu2": [halves(gu_t[1][l])], "d2": [halves(wd_l[1][l])]}

    small_local = [sc_conv_w, m_conv_w]
    gathered_s = all_gather_xy(_pack(small_local, LANE, 32), "gather_conv_weights")
    per_shard_s = [_unpack(gathered_s[k], [a.shape for a in small_local]) for k in range(NCHIP)]
    sc_conv_full = jnp.concatenate([per_shard_s[k][0] for k in range(NCHIP)], axis=2)
    m_conv_full = jnp.concatenate([per_shard_s[k][1] for k in range(NCHIP)], axis=2)

    pad_h = lambda a: jnp.pad(a, ((0, 0), (0, LANE - hh)))
    dt_bias_p, a_log_p = pad_h(m_dt_bias), pad_h(m_A_log)
    d_exp = jnp.repeat(m_D, SSM_P, axis=1)
    e_mat = (jnp.arange(w)[None, :] // SSM_P == jnp.arange(LANE)[:, None]).astype(F32)
    et_mat = e_mat.T
    o_z, o_xbc, o_dt, o_g = 3 * d, 5 * d, 5 * d + cw, 5 * d + cw + hh

    def layer_weights(got):
        wt = {"w1024": whole(got["small"][0]), "wpp": whole(got["small"][1])}
        wt.update({k: whole(got[k][0]) for k in ("gu1", "d1", "gu2", "d2")})
        gw = whole(got["win"][0])

        def wi(lo, hi):
            parts = [gw[k, max(lo - k * p4, 0):min(hi - k * p4, p4)] for k in range(NCHIP) if lo < (k + 1) * p4 and hi > k * p4]
            return parts[0] if len(parts) == 1 else jnp.concatenate(parts, axis=0)

        wt["sc3"], wt["z"], wt["xbc"], wt["g2"] = wi(0, o_z), wi(o_z, o_xbc), wi(o_xbc, o_dt), wi(o_g, o_g + 2 * d)
        wt["dt"] = jnp.pad(wi(o_dt, o_g), ((0, LANE - hh), (0, 0)))
        wt["in_p"] = jnp.concatenate([wt["g2"], wt["z"], wt["sc3"][d:], wt["xbc"], wt["sc3"][:d], wt["dt"]], axis=0)
        return wt

    first = pieces(0)
    order = ("gu1", "d1", "win", "small", "gu2", "d2")
    flat = gather_rider([a for k in order for a in first[k]]).standalone("gather_weights")
    got, pos = {}, 0
    for k in order:
        got[k] = flat[pos:pos + len(first[k])]
        pos += len(first[k])
    wts_l = [layer_weights(got)]

    h = x[0]
    saved = []
    for i in range(depth):
        s, wt = {}, wts_l[i]
        nxt = pieces(i + 1) if i + 1 < depth else None
        ride = lambda k: gather_rider(nxt[k]) if nxt else None
        got = {}
        s["h0"] = h
        (s["ab1"], s4, s["n1"]), got["small"] = ffn_up(h, ffn1_norm[i:i + 1], wt["gu1"], rider=ride("small"))
        h, u, got["d1"] = ffn_down(s4, wt["d1"], h, mix_norm[i:i + 1], rider=ride("d1"))
        s["h1"] = h
        s["u"] = u
        s["sc3"] = mm(u, wt["sc3"], tb=True, out_dtype=BF, name="proj_sc")
        s["z"] = mm(u, wt["z"], tb=True, out_dtype=BF, name="proj_z")
        s["xbc_raw"] = mm(u, wt["xbc"], tb=True, out_dtype=BF, name="proj_xbc")
        s["gates"] = mm(u, wt["g2"], tb=True, out_dtype=BF, name="proj_gates")
        s["dt_raw"] = mm(u, wt["dt"], tb=True, name="proj_dt")
        s["ya_in"] = conv_a_fwd(s["sc3"], sc_conv_full[i])
        s["xbc"], s["dt"] = conv_m_fwd(s["xbc_raw"], s["dt_raw"], m_conv_full[i], m_conv_b[i:i + 1], dt_bias_p[i:i + 1])
        (s["yn"], s["y"], s["sprev"]), got["win"] = ssd_fwd(s["xbc"], s["dt"], s["z"], a_log_p[i:i + 1], d_exp[i:i + 1], m_norm[i:i + 1], e_mat,
                                                            rider=ride("win"))
        (h, s["y_a"], s["y_m"], s["merged"]), got["gu2"] = mix_out_fwd(s["ya_in"], s["yn"], s["gates"], h, wt["w1024"], rider=ride("gu2"))
        s["h2"] = h
        (s["ab2"], s4, s["n2"]), got["gu1"] = ffn_up(h, ffn2_norm[i:i + 1], wt["gu2"], rider=ride("gu1"))
        h, _, got["d2"] = ffn_down(s4, wt["d2"], h, ple_norm[i:i + 1], rider=ride("d2"))
        s["h3"] = h
        h = ple_fwd(h, ple_norm[i:i + 1], p[i, 0], wt["w1024"], wt["wpp"])
        saved.append(s)
        if nxt:
            wts_l.append(layer_weights(got))

    dh, loss_lanes, g_final = loss_bwd(h, final_norm[None, :], loss_target[0])
    loss = lax.psum(jnp.sum(loss_lanes), ("x", "y", "c"))

    def finish_reduce(cs, got):
        halves = [add_chips(c_, g_, 2, "grad_add_chips") for c_, g_ in zip(cs, got, strict=True)]
        return [f.reshape(-1, f.shape[2]) for f in join_packs(halves, "grad_join_halves")]

    g_layer, pending, reduced = None, None, [None] * depth
    gs = {n: [None] * depth for n in SMALL_SHARDED + SMALL_REPL if n != "final_norm"}
    for i in reversed(range(depth)):
        s = saved[i]
        wt = wts_l[i]
        dh, gs["ple_norm"][i], n3, dgp, dpe = ple_bwd(dh, s["h3"], ple_norm[i:i + 1], p[i, 0], wt["w1024"], wt["wpp"])
        g_pg = mm(n3, dgp, ta=True, out_dtype=BF, name="g_ple_gate", tm_cap=512, tn_cap=512)
        g_pp = mm(p[i, 0], dpe, ta=True, out_dtype=BF, name="g_ple_proj", tm_cap=512, tn_cap=512)
        g_pp = jnp.transpose(g_pp.reshape(g_pp.shape[0], NCHIP, d4), (1, 0, 2))
        (dn2, s2, dab2, do2), from_sibling = ffn_bwd(dh, s["ab2"], wt["gu2"], wt["d2"], rider=swap_rider(g_layer) if g_layer else None)
        if g_layer:
            pending = [add_sibling(g, r_, "grad_add_sibling") for g, r_ in zip(g_layer, from_sibling, strict=True)]
        g_ffn2 = ffn_wgrads(s["n2"], do2, s2, dab2)
        dh, gs["ffn2_norm"][i] = norm_bwd_add(dh, s["h2"], ffn2_norm[i:i + 1], dn2)
        dproj, dya, dyn, dy_a, dy_m = mix_out_bwd(dh, s["gates"], s["y_a"], s["y_m"], wt["w1024"], pp)
        g_wo = mm(s["merged"], dh, ta=True, out_dtype=BF, name="g_w_o", tm_cap=512, tn_cap=512)
        g_sco = mm(s["ya_in"], dy_a, ta=True, out_dtype=BF, name="g_sc_out", tm_cap=512, tn_cap=512)
        g_mo = mm(s["yn"], dy_m, ta=True, out_dtype=BF, name="g_m_out", tm_cap=512, tn_cap=512)
        g_1024 = jnp.concatenate([g_mo.reshape(NCHIP, 2 * d4, d), g_sco.reshape(NCHIP, d4, d), g_wo.reshape(NCHIP, d4, d),
                                  g_pg.reshape(NCHIP, d4, d)], axis=1)
        (dproj, dxbc, ddt, gs["m_norm"][i], gd, gal), got_a = ssd_bwd(dyn, s["y"], s["z"], s["xbc"], s["dt"], s["sprev"], a_log_p[i:i + 1], d_exp[i:i + 1],
                                                                      m_norm[i:i + 1], e_mat, et_mat, Window(0, 1, pp, dproj),
                                                                      rider=scatter_rider(pending[:1]) if pending else None)
        gs["m_D"][i], gs["m_A_log"][i] = gd[:, :hh], gal[:, :hh]
        dpre, dproj, gdb = conv_m_bwd1(dxbc, s["xbc_raw"], ddt, s["dt_raw"], m_conv_full[i], m_conv_b[i:i + 1], dt_bias_p[i:i + 1],
                                       Window(1, (7 * d + cw) // LANE, pp, dproj))
        gs["m_dt_bias"][i] = gdb[:, :hh]
        dproj, gs["m_conv_w"][i], gs["m_conv_b"][i] = conv_bwd2(dpre, s["xbc_raw"], m_conv_full[i], "conv_m_bwd2", Window(0, 6 * d // cw, pp, dproj))
        dcv, dproj, v = conv_a_bwd1(dya, s["sc3"], sc_conv_full[i], Window(1, (6 * d + cw) // d, pp, dproj))
        dproj, gs["sc_conv_w"][i] = conv_a_bwd2(dcv, v, s["sc3"], sc_conv_full[i], Window(0, 2, pp, dproj))
        if pending:
            dh_mix, gs["mix_norm"][i], got_b = proj_in_bwd(dproj, wt["in_p"], dh, s["h1"], mix_norm[i:i + 1], rider=scatter_rider(pending[2:3]))
            gwp, got_c = mm(dproj, s["u"], ta=True, out_dtype=BF, name="g_w_in", tm_cap=1152, tn_cap=512, rider=scatter_rider(pending[1:2] + pending[3:]))
            reduced[i + 1] = finish_reduce(pending, [got_a[0], got_c[0], got_b[0], got_c[1]])
        else:
            dh_mix, gs["mix_norm"][i], _ = proj_in_bwd(dproj, wt["in_p"], dh, s["h1"], mix_norm[i:i + 1])
            gwp = mm(dproj, s["u"], ta=True, out_dtype=BF, name="g_w_in", tm_cap=1152, tn_cap=512)
        gw_rows = jnp.concatenate([gwp[6 * d + cw:7 * d + cw], gwp[4 * d:6 * d], gwp[2 * d:4 * d], gwp[6 * d:6 * d + cw], gwp[7 * d + cw:7 * d + cw + hh],
                                   gwp[:2 * d]], axis=0)
        g_in = jnp.pad(gw_rows.reshape(NCHIP, p4, d), ((0, 0), (0, p4p - p4), (0, 0)))
        dh = dh_mix
        (dn1, s1, dab1, do1), _ = ffn_bwd(dh, s["ab1"], wt["gu1"], wt["d1"])
        g_ffn1 = ffn_wgrads(s["n1"], do1, s1, dab1)
        dh, gs["ffn1_norm"][i] = norm_bwd_add(dh, s["h0"], ffn1_norm[i:i + 1], dn1)
        g_layer = [jnp.concatenate([g_ffn1, g_ffn2], axis=1), g_1024, g_in, g_pp]
        g_layer = [g.reshape(NCHIP, 2, g.shape[1] // 2, g.shape[2]) for g in g_layer]
    from_sibling = swap_rider(g_layer).standalone("grad_swap_halves")
    pending = [add_sibling(g, r_, "grad_add_sibling") for g, r_ in zip(g_layer, from_sibling, strict=True)]
    reduced[0] = finish_reduce(pending, scatter_packs(pending, "grad_scatter"))
    grad_x = dh[None]

    f4 = reduced[0][0].shape[0] // 6
    rows_of = lambda j, lo, hi: jnp.stack([reduced[l][j][lo:hi] for l in range(depth)])
    ffn_rows = lambda j: rows_of(0, j * f4, (j + 1) * f4)
    grads = {
        "ffn1_wg": ffn_rows(0), "ffn1_wu": ffn_rows(1), "ffn1_wd": ffn_rows(2), "ffn2_wg": ffn_rows(3), "ffn2_wu": ffn_rows(4), "ffn2_wd": ffn_rows(5),
        "m_w_out": rows_of(1, 0, 2 * d4), "sc_w_out": rows_of(1, 2 * d4, 3 * d4), "w_o": rows_of(1, 3 * d4, 4 * d4), "ple_w_gate": rows_of(1, 4 * d4, 5 * d4),
        "w_in": rows_of(2, 0, p4), "ple_w_proj": rows_of(3, 0, None),
    }

    small_names = list(SMALL_SHARDED + SMALL_REPL)
    small_full = [g_final[0] if n == "final_norm" else jnp.stack(gs[n]) for n in small_names]
    small_pack = _pack(small_full, LANE, HALO)
    all8 = all_gather_8(small_pack, "gather_small_grads")
    small_sum = add_parts([all8[k] for k in range(8)], F32, "add_small_grads", tm=256)
    for n, tot in zip(small_names, _unpack(small_sum, [a.shape for a in small_full]), strict=True):
        if n in SMALL_SHARDED:
            cl = wts[n].shape[2]
            grads[n] = lax.dynamic_slice_in_dim(tot, k_me * cl, cl, axis=2)
        else:
            grads[n] = tot.reshape(wts[n].shape)

    delta, new_m, new_v = {}, {}, {}
    for n in BIG:
        view = tr if n in TRANSPOSED else (lambda a: a)
        shp = grads[n].shape
        two = lambda a: a.reshape(-1, shp[-1])
        dl, nm, nv = adamw(two(view(wts[n])), two(grads[n]), two(view(mom[n])), two(view(vel[n])), "adamw_" + "x".join(map(str, shp[1:])))
        grads[n], delta[n], new_m[n], new_v[n] = view(grads[n]), view(dl.reshape(shp)), view(nm.reshape(shp)), view(nv.reshape(shp))
    for n in small_names:
        shp = wts[n].shape
        two = lambda a: a.reshape(-1, shp[-1])
        dl, nm, nv = adamw(two(wts[n]), two(grads[n]), two(mom[n]), two(vel[n]), "adamw_small_" + "x".join(map(str, shp)))
        delta[n], new_m[n], new_v[n] = dl.reshape(shp), nm.reshape(shp), nv.reshape(shp)

    return (loss, grad_x, *[grads[n] for n in ORDER], *[delta[n] for n in ORDER], *[new_m[n] for n in ORDER], *[new_v[n] for n in ORDER])
```
